```python
import math
import jax, jax.numpy as jnp
from jax import lax
import numpy as np

D_MODEL = 1024
BATCH = 16
SEQ = 2048
DEPTH = 4

D_CONV = D_MODEL
CONV_K = 3
D_SSM = D_MODEL
SSM_HEAD_DIM = 64
SSM_HEADS = D_SSM // SSM_HEAD_DIM
SSM_GROUPS = 2
HEADS_PER_GROUP = SSM_HEADS // SSM_GROUPS
D_STATE = 128
SSM_CONV_K = 4
CHUNK = 128
SSM_CONV_DIM = D_SSM + 2 * SSM_GROUPS * D_STATE
D_MIX = D_CONV + D_SSM
IN_COLS = 3 * D_CONV + D_SSM + SSM_CONV_DIM + SSM_HEADS
D_FF = 4 * D_MODEL
EPS = 1e-6

kernel_name = "hybrid_shortconv_ssd_parallel_groups"


def _rmsnorm(x, g):
    xf = x.astype(jnp.float32)
    y = xf * lax.rsqrt(jnp.mean(xf * xf, axis=-1, keepdims=True) + EPS)
    return (y * g.astype(jnp.float32)).astype(x.dtype)


def _causal_depthwise_conv(u, w):
    k, ch = w.shape
    return lax.conv_general_dilated(
        u, w[:, None, :].astype(u.dtype), window_strides=(1,), padding=[(k - 1, 0)],
        dimension_numbers=("NWC", "WIO", "NWC"), feature_group_count=ch)


def _ssd_chunked(x, dt, a, b, c):
    bsz, t = x.shape[:2]
    nc = t // CHUNK
    g, e, p, n = SSM_GROUPS, HEADS_PER_GROUP, SSM_HEAD_DIM, D_STATE
    xdt = (x * dt[..., None]).reshape(bsz, nc, CHUNK, g, e, p)
    adt = (dt * a).reshape(bsz, nc, CHUNK, g, e).transpose(0, 1, 3, 4, 2)
    bc = b.reshape(bsz, nc, CHUNK, g, n)
    cc = c.reshape(bsz, nc, CHUNK, g, n)
    cs = jnp.cumsum(adt, axis=-1)
    mask = jnp.tril(jnp.ones((CHUNK, CHUNK), dtype=bool))
    seg = jnp.where(mask, cs[..., :, None] - cs[..., None, :], -jnp.inf)
    decay_ls = jnp.exp(seg)
    scores = jnp.einsum("bclgn,bcsgn->bcgls", cc, bc)
    m = scores[:, :, :, None] * decay_ls
    y_diag = jnp.einsum("bcgels,bcsgep->bclgep", m, xdt)
    decay_to_end = jnp.exp(cs[..., -1:] - cs)
    states = jnp.einsum("bclgn,bcgel,bclgep->bcgepn", bc, decay_to_end, xdt)
    chunk_decay = jnp.exp(cs[..., -1])

    def step(carry, inp):
        s_c, d_c = inp
        return carry * d_c[..., None, None] + s_c, carry

    init = jnp.zeros((bsz, g, e, p, n), jnp.float32)
    _, prev = lax.scan(step, init, (jnp.moveaxis(states, 1, 0), jnp.moveaxis(chunk_decay, 1, 0)))
    prev = jnp.moveaxis(prev, 0, 1)
    y_off = jnp.einsum("bclgn,bcgepn,bcgel->bclgep", cc, prev, jnp.exp(cs))
    return (y_diag + y_off).reshape(bsz, t, SSM_HEADS, p)


def _fwd_setup_inputs(seed: int = 0) -> dict:
    key = jax.random.key(seed)
    ks = jax.random.split(key, 20)
    f32 = jnp.float32
    nrm = lambda k, s, scale: jax.random.normal(k, s, f32) * scale
    gain = lambda k, s: 1.0 + 0.02 * jax.random.normal(k, s, f32)
    x = jax.random.normal(ks[0], (BATCH, SEQ, D_MODEL), f32)
    dt_min, dt_max = 1e-3, 1e-1
    u = jax.random.uniform(ks[6], (DEPTH, SSM_HEADS), f32)
    dt0 = jnp.exp(u * (math.log(dt_max) - math.log(dt_min)) + math.log(dt_min))
    dt_bias = dt0 + jnp.log(-jnp.expm1(-dt0))
    a_log = jnp.log(jax.random.uniform(ks[7], (DEPTH, SSM_HEADS), f32, 1.0, 16.0))
    return {
        "x": x,
        "norm_mix_pre": gain(ks[1], (DEPTH, D_MODEL)),
        "w_in": nrm(ks[2], (DEPTH, D_MODEL, IN_COLS), D_MODEL ** -0.5),
        "conv_a_w": nrm(ks[3], (DEPTH, CONV_K, D_CONV), CONV_K ** -0.5),
        "ssm_conv_w": nrm(ks[4], (DEPTH, SSM_CONV_K, SSM_CONV_DIM), SSM_CONV_K ** -0.5),
        "ssm_conv_b": nrm(ks[5], (DEPTH, SSM_CONV_DIM), 0.02),
        "dt_bias": dt_bias,
        "a_log": a_log,
        "d_skip": gain(ks[8], (DEPTH, SSM_HEADS)),
        "conv_out_norm": gain(ks[9], (DEPTH, D_CONV)),
        "ssm_out_norm": gain(ks[10], (DEPTH, D_SSM)),
        "w_out": nrm(ks[11], (DEPTH, D_MIX, D_MODEL), D_MIX ** -0.5),
        "norm_mix_post": gain(ks[12], (DEPTH, D_MODEL)),
        "norm_mlp_pre": gain(ks[13], (DEPTH, D_MODEL)),
        "w_up": nrm(ks[14], (DEPTH, D_MODEL, D_FF), D_MODEL ** -0.5),
        "w_down": nrm(ks[15], (DEPTH, D_FF, D_MODEL), D_FF ** -0.5),
        "norm_mlp_post": gain(ks[16], (DEPTH, D_MODEL)),
    }


def _fwd_reference(x, norm_mix_pre, w_in, conv_a_w, ssm_conv_w, ssm_conv_b, dt_bias, a_log, d_skip,
              conv_out_norm, ssm_out_norm, w_out, norm_mix_post, norm_mlp_pre, w_up, w_down,
              norm_mlp_post):
    bsz, t, _ = x.shape
    split_at = [D_CONV, 2 * D_CONV, 3 * D_CONV, 3 * D_CONV + D_SSM,
                3 * D_CONV + D_SSM + SSM_CONV_DIM]
    for i in range(DEPTH):
        h = _rmsnorm(x, norm_mix_pre[i])
        proj = jnp.einsum("btd,de->bte", h, w_in[i])
        x_a, c_a, b_a, z, xbc, dt_raw = jnp.split(proj, split_at, axis=-1)
        y_a = b_a * _causal_depthwise_conv(c_a * x_a, conv_a_w[i])
        y_a = _rmsnorm(y_a, conv_out_norm[i])
        xbc = _causal_depthwise_conv(xbc, ssm_conv_w[i]) + ssm_conv_b[i].astype(xbc.dtype)
        xbc = jax.nn.silu(xbc)
        xs, bs, cs_ = jnp.split(xbc, [D_SSM, D_SSM + SSM_GROUPS * D_STATE], axis=-1)
        xs = xs.reshape(bsz, t, SSM_HEADS, SSM_HEAD_DIM).astype(jnp.float32)
        bs = bs.reshape(bsz, t, SSM_GROUPS, D_STATE).astype(jnp.float32)
        cs_ = cs_.reshape(bsz, t, SSM_GROUPS, D_STATE).astype(jnp.float32)
        dt = jax.nn.softplus(dt_raw.astype(jnp.float32) + dt_bias[i].astype(jnp.float32))
        a = -jnp.exp(a_log[i].astype(jnp.float32))
        y_s = _ssd_chunked(xs, dt, a, bs, cs_) + d_skip[i].astype(jnp.float32)[:, None] * xs
        y_s = y_s.reshape(bsz, t, D_SSM) * jax.nn.silu(z.astype(jnp.float32))
        y_s = _rmsnorm(y_s.reshape(bsz, t, SSM_GROUPS, D_SSM // SSM_GROUPS),
                       ssm_out_norm[i].reshape(SSM_GROUPS, D_SSM // SSM_GROUPS))
        y_s = y_s.reshape(bsz, t, D_SSM).astype(x.dtype)
        mix = jnp.einsum("bte,ed->btd", jnp.concatenate([y_a, y_s], axis=-1), w_out[i])
        x = x + _rmsnorm(mix, norm_mix_post[i])
        h = _rmsnorm(x, norm_mlp_pre[i])
        f = jnp.square(jax.nn.relu(jnp.einsum("btd,df->btf", h, w_up[i])))
        f = jnp.einsum("btf,fd->btd", f, w_down[i])
        x = x + _rmsnorm(f, norm_mlp_post[i])
    return x


import jax as _jax
import jax.numpy as _jnp

TWIN_FORMAT = 'train_step'
FWD_PARAMS = ['x', 'norm_mix_pre', 'w_in', 'conv_a_w', 'ssm_conv_w', 'ssm_conv_b', 'dt_bias', 'a_log', 'd_skip', 'conv_out_norm', 'ssm_out_norm', 'w_out', 'norm_mix_post', 'norm_mlp_pre', 'w_up', 'w_down', 'norm_mlp_post']
TWIN_WEIGHTS = ['norm_mix_pre', 'w_in', 'conv_a_w', 'ssm_conv_w', 'ssm_conv_b', 'dt_bias', 'a_log', 'd_skip', 'conv_out_norm', 'ssm_out_norm', 'w_out', 'norm_mix_post', 'norm_mlp_pre', 'w_up', 'w_down', 'norm_mlp_post']
TWIN_DIFF_INPUT = 'x'
TWIN_INPUTS = ['x', 'norm_mix_pre', 'w_in', 'conv_a_w', 'ssm_conv_w', 'ssm_conv_b', 'dt_bias', 'a_log', 'd_skip', 'conv_out_norm', 'ssm_out_norm', 'w_out', 'norm_mix_post', 'norm_mlp_pre', 'w_up', 'w_down', 'norm_mlp_post', 'loss_target', 'm_norm_mix_pre', 'm_w_in', 'm_conv_a_w', 'm_ssm_conv_w', 'm_ssm_conv_b', 'm_dt_bias', 'm_a_log', 'm_d_skip', 'm_conv_out_norm', 'm_ssm_out_norm', 'm_w_out', 'm_norm_mix_post', 'm_norm_mlp_pre', 'm_w_up', 'm_w_down', 'm_norm_mlp_post', 'v_norm_mix_pre', 'v_w_in', 'v_conv_a_w', 'v_ssm_conv_w', 'v_ssm_conv_b', 'v_dt_bias', 'v_a_log', 'v_d_skip', 'v_conv_out_norm', 'v_ssm_out_norm', 'v_w_out', 'v_norm_mix_post', 'v_norm_mlp_pre', 'v_w_up', 'v_w_down', 'v_norm_mlp_post']
TWIN_OUTPUTS = ['loss', 'grad_x', 'grad_norm_mix_pre', 'grad_w_in', 'grad_conv_a_w', 'grad_ssm_conv_w', 'grad_ssm_conv_b', 'grad_dt_bias', 'grad_a_log', 'grad_d_skip', 'grad_conv_out_norm', 'grad_ssm_out_norm', 'grad_w_out', 'grad_norm_mix_post', 'grad_norm_mlp_pre', 'grad_w_up', 'grad_w_down', 'grad_norm_mlp_post', 'delta_norm_mix_pre', 'delta_w_in', 'delta_conv_a_w', 'delta_ssm_conv_w', 'delta_ssm_conv_b', 'delta_dt_bias', 'delta_a_log', 'delta_d_skip', 'delta_conv_out_norm', 'delta_ssm_out_norm', 'delta_w_out', 'delta_norm_mix_post', 'delta_norm_mlp_pre', 'delta_w_up', 'delta_w_down', 'delta_norm_mlp_post', 'new_m_norm_mix_pre', 'new_m_w_in', 'new_m_conv_a_w', 'new_m_ssm_conv_w', 'new_m_ssm_conv_b', 'new_m_dt_bias', 'new_m_a_log', 'new_m_d_skip', 'new_m_conv_out_norm', 'new_m_ssm_out_norm', 'new_m_w_out', 'new_m_norm_mix_post', 'new_m_norm_mlp_pre', 'new_m_w_up', 'new_m_w_down', 'new_m_norm_mlp_post', 'new_v_norm_mix_pre', 'new_v_w_in', 'new_v_conv_a_w', 'new_v_ssm_conv_w', 'new_v_ssm_conv_b', 'new_v_dt_bias', 'new_v_a_log', 'new_v_d_skip', 'new_v_conv_out_norm', 'new_v_ssm_out_norm', 'new_v_w_out', 'new_v_norm_mix_post', 'new_v_norm_mlp_pre', 'new_v_w_up', 'new_v_w_down', 'new_v_norm_mlp_post']
TWIN_LEAF_KINDS = {'loss': 'loss', 'grad_x': 'grad_x', 'grad_norm_mix_pre': 'grad_w', 'grad_w_in': 'grad_w', 'grad_conv_a_w': 'grad_w', 'grad_ssm_conv_w': 'grad_w', 'grad_ssm_conv_b': 'grad_w', 'grad_dt_bias': 'grad_w', 'grad_a_log': 'grad_w', 'grad_d_skip': 'grad_w', 'grad_conv_out_norm': 'grad_w', 'grad_ssm_out_norm': 'grad_w', 'grad_w_out': 'grad_w', 'grad_norm_mix_post': 'grad_w', 'grad_norm_mlp_pre': 'grad_w', 'grad_w_up': 'grad_w', 'grad_w_down': 'grad_w', 'grad_norm_mlp_post': 'grad_w', 'delta_norm_mix_pre': 'delta_w', 'delta_w_in': 'delta_w', 'delta_conv_a_w': 'delta_w', 'delta_ssm_conv_w': 'delta_w', 'delta_ssm_conv_b': 'delta_w', 'delta_dt_bias': 'delta_w', 'delta_a_log': 'delta_w', 'delta_d_skip': 'delta_w', 'delta_conv_out_norm': 'delta_w', 'delta_ssm_out_norm': 'delta_w', 'delta_w_out': 'delta_w', 'delta_norm_mix_post': 'delta_w', 'delta_norm_mlp_pre': 'delta_w', 'delta_w_up': 'delta_w', 'delta_w_down': 'delta_w', 'delta_norm_mlp_post': 'delta_w', 'new_m_norm_mix_pre': 'new_m', 'new_m_w_in': 'new_m', 'new_m_conv_a_w': 'new_m', 'new_m_ssm_conv_w': 'new_m', 'new_m_ssm_conv_b': 'new_m', 'new_m_dt_bias': 'new_m', 'new_m_a_log': 'new_m', 'new_m_d_skip': 'new_m', 'new_m_conv_out_norm': 'new_m', 'new_m_ssm_out_norm': 'new_m', 'new_m_w_out': 'new_m', 'new_m_norm_mix_post': 'new_m', 'new_m_norm_mlp_pre': 'new_m', 'new_m_w_up': 'new_m', 'new_m_w_down': 'new_m', 'new_m_norm_mlp_post': 'new_m', 'new_v_norm_mix_pre': 'new_v', 'new_v_w_in': 'new_v', 'new_v_conv_a_w': 'new_v', 'new_v_ssm_conv_w': 'new_v', 'new_v_ssm_conv_b': 'new_v', 'new_v_dt_bias': 'new_v', 'new_v_a_log': 'new_v', 'new_v_d_skip': 'new_v', 'new_v_conv_out_norm': 'new_v', 'new_v_ssm_out_norm': 'new_v', 'new_v_w_out': 'new_v', 'new_v_norm_mix_post': 'new_v', 'new_v_norm_mlp_pre': 'new_v', 'new_v_w_up': 'new_v', 'new_v_w_down': 'new_v', 'new_v_norm_mlp_post': 'new_v'}


def _forward(args):
    return _fwd_reference(*[args[k] for k in FWD_PARAMS])


def _output_shape():
    out = _jax.eval_shape(lambda: _forward(_fwd_setup_inputs(0)))
    return out.shape, out.dtype

N_MICROBATCH = 1
ADAM_LR = 0.001
ADAM_B1 = 0.9
ADAM_B2 = 0.999
ADAM_EPS = 1e-08
ADAM_WD = 0.01
ADAM_STEP = 10
PER_EXAMPLE_BATCH_AXIS = {'x': 0, 'loss_target': 0}
SHARED_INPUTS = []
_WEIGHT_DTYPES = {'norm_mix_pre': _jnp.float32, 'w_in': _jnp.float32, 'conv_a_w': _jnp.float32, 'ssm_conv_w': _jnp.float32, 'ssm_conv_b': _jnp.float32, 'dt_bias': _jnp.float32, 'a_log': _jnp.float32, 'd_skip': _jnp.float32, 'conv_out_norm': _jnp.float32, 'ssm_out_norm': _jnp.float32, 'w_out': _jnp.float32, 'norm_mix_post': _jnp.float32, 'norm_mlp_pre': _jnp.float32, 'w_up': _jnp.float32, 'w_down': _jnp.float32, 'norm_mlp_post': _jnp.float32}
MOMENT_SCALE = {'norm_mix_pre': 3.120787e+00, 'w_in': 1.372841e+00, 'conv_a_w': 9.138962e-01, 'ssm_conv_w': 2.948003e+00, 'ssm_conv_b': 8.503325e+00, 'dt_bias': 2.644504e+00, 'a_log': 2.446995e+01, 'd_skip': 1.246287e+01, 'conv_out_norm': 9.123463e-01, 'ssm_out_norm': 4.918017e+00, 'w_out': 4.850920e+00, 'norm_mix_post': 3.207872e+01, 'norm_mlp_pre': 4.052157e+00, 'w_up': 1.933182e+00, 'w_down': 1.230251e+01, 'norm_mlp_post': 3.583668e+01}


def _to_microbatches(a, axis):
    t = _jnp.moveaxis(a, axis, 0)
    t = t.reshape((N_MICROBATCH, t.shape[0] // N_MICROBATCH) + t.shape[1:])
    return _jnp.moveaxis(t, 1, axis + 1)


def setup_inputs(seed: int = 0) -> dict:
    inp = _fwd_setup_inputs(seed)
    key = _jax.random.fold_in(_jax.random.key(seed), 7919)
    shape, _ = _output_shape()
    out = dict(inp)
    out["loss_target"] = _jax.random.normal(_jax.random.fold_in(key, 0), shape, _jnp.float32)
    for i, name in enumerate(TWIN_WEIGHTS):
        w = inp[name].astype(_jnp.float32)
        if MOMENT_SCALE is None:
            s = _jnp.sqrt(_jnp.mean(_jnp.square(w)) + 1e-30)
        else:
            s = MOMENT_SCALE[name]
        km, kv = _jax.random.split(_jax.random.fold_in(key, i + 1))
        out[name] = w
        out["m_" + name] = s * _jax.random.normal(km, w.shape, _jnp.float32)
        out["v_" + name] = (s * s) * _jax.random.uniform(kv, w.shape, _jnp.float32, 0.5, 1.5)
    if N_MICROBATCH > 1:
        for name, axis in PER_EXAMPLE_BATCH_AXIS.items():
            out[name] = _to_microbatches(out[name], axis)
    return {'x': out['x'], 'norm_mix_pre': out['norm_mix_pre'], 'w_in': out['w_in'], 'conv_a_w': out['conv_a_w'], 'ssm_conv_w': out['ssm_conv_w'], 'ssm_conv_b': out['ssm_conv_b'], 'dt_bias': out['dt_bias'], 'a_log': out['a_log'], 'd_skip': out['d_skip'], 'conv_out_norm': out['conv_out_norm'], 'ssm_out_norm': out['ssm_out_norm'], 'w_out': out['w_out'], 'norm_mix_post': out['norm_mix_post'], 'norm_mlp_pre': out['norm_mlp_pre'], 'w_up': out['w_up'], 'w_down': out['w_down'], 'norm_mlp_post': out['norm_mlp_post'], 'loss_target': out['loss_target'], 'm_norm_mix_pre': out['m_norm_mix_pre'], 'm_w_in': out['m_w_in'], 'm_conv_a_w': out['m_conv_a_w'], 'm_ssm_conv_w': out['m_ssm_conv_w'], 'm_ssm_conv_b': out['m_ssm_conv_b'], 'm_dt_bias': out['m_dt_bias'], 'm_a_log': out['m_a_log'], 'm_d_skip': out['m_d_skip'], 'm_conv_out_norm': out['m_conv_out_norm'], 'm_ssm_out_norm': out['m_ssm_out_norm'], 'm_w_out': out['m_w_out'], 'm_norm_mix_post': out['m_norm_mix_post'], 'm_norm_mlp_pre': out['m_norm_mlp_pre'], 'm_w_up': out['m_w_up'], 'm_w_down': out['m_w_down'], 'm_norm_mlp_post': out['m_norm_mlp_post'], 'v_norm_mix_pre': out['v_norm_mix_pre'], 'v_w_in': out['v_w_in'], 'v_conv_a_w': out['v_conv_a_w'], 'v_ssm_conv_w': out['v_ssm_conv_w'], 'v_ssm_conv_b': out['v_ssm_conv_b'], 'v_dt_bias': out['v_dt_bias'], 'v_a_log': out['v_a_log'], 'v_d_skip': out['v_d_skip'], 'v_conv_out_norm': out['v_conv_out_norm'], 'v_ssm_out_norm': out['v_ssm_out_norm'], 'v_w_out': out['v_w_out'], 'v_norm_mix_post': out['v_norm_mix_post'], 'v_norm_mlp_pre': out['v_norm_mlp_pre'], 'v_w_up': out['v_w_up'], 'v_w_down': out['v_w_down'], 'v_norm_mlp_post': out['v_norm_mlp_post']}


def _loss(weights, diff, rest, loss_target):
    with _jax.named_scope("forward"):
        args = {**rest, TWIN_DIFF_INPUT: diff, **{k: w.astype(_WEIGHT_DTYPES[k]) for k, w in weights.items()}}
        y = _forward(args)
    with _jax.named_scope("loss_head"):
        err = _jnp.square(y.astype(_jnp.float32) - loss_target)
        return 0.5 * _jnp.sum(_jnp.mean(err, axis=-1)) if err.ndim else 0.5 * err


def _adamw(w, g, m, v):
    m = ADAM_B1 * m + (1.0 - ADAM_B1) * g
    v = ADAM_B2 * v + (1.0 - ADAM_B2) * _jnp.square(g)
    m_hat = m / (1.0 - ADAM_B1 ** ADAM_STEP)
    v_hat = v / (1.0 - ADAM_B2 ** ADAM_STEP)
    delta = -ADAM_LR * (m_hat / (_jnp.sqrt(v_hat) + ADAM_EPS) + ADAM_WD * w)
    return delta, m, v


def reference(x, norm_mix_pre, w_in, conv_a_w, ssm_conv_w, ssm_conv_b, dt_bias, a_log, d_skip, conv_out_norm, ssm_out_norm, w_out, norm_mix_post, norm_mlp_pre, w_up, w_down, norm_mlp_post, loss_target, m_norm_mix_pre, m_w_in, m_conv_a_w, m_ssm_conv_w, m_ssm_conv_b, m_dt_bias, m_a_log, m_d_skip, m_conv_out_norm, m_ssm_out_norm, m_w_out, m_norm_mix_post, m_norm_mlp_pre, m_w_up, m_w_down, m_norm_mlp_post, v_norm_mix_pre, v_w_in, v_conv_a_w, v_ssm_conv_w, v_ssm_conv_b, v_dt_bias, v_a_log, v_d_skip, v_conv_out_norm, v_ssm_out_norm, v_w_out, v_norm_mix_post, v_norm_mlp_pre, v_w_up, v_w_down, v_norm_mlp_post):
    given = dict(x=x, norm_mix_pre=norm_mix_pre, w_in=w_in, conv_a_w=conv_a_w, ssm_conv_w=ssm_conv_w, ssm_conv_b=ssm_conv_b, dt_bias=dt_bias, a_log=a_log, d_skip=d_skip, conv_out_norm=conv_out_norm, ssm_out_norm=ssm_out_norm, w_out=w_out, norm_mix_post=norm_mix_post, norm_mlp_pre=norm_mlp_pre, w_up=w_up, w_down=w_down, norm_mlp_post=norm_mlp_post, loss_target=loss_target, m_norm_mix_pre=m_norm_mix_pre, m_w_in=m_w_in, m_conv_a_w=m_conv_a_w, m_ssm_conv_w=m_ssm_conv_w, m_ssm_conv_b=m_ssm_conv_b, m_dt_bias=m_dt_bias, m_a_log=m_a_log, m_d_skip=m_d_skip, m_conv_out_norm=m_conv_out_norm, m_ssm_out_norm=m_ssm_out_norm, m_w_out=m_w_out, m_norm_mix_post=m_norm_mix_post, m_norm_mlp_pre=m_norm_mlp_pre, m_w_up=m_w_up, m_w_down=m_w_down, m_norm_mlp_post=m_norm_mlp_post, v_norm_mix_pre=v_norm_mix_pre, v_w_in=v_w_in, v_conv_a_w=v_conv_a_w, v_ssm_conv_w=v_ssm_conv_w, v_ssm_conv_b=v_ssm_conv_b, v_dt_bias=v_dt_bias, v_a_log=v_a_log, v_d_skip=v_d_skip, v_conv_out_norm=v_conv_out_norm, v_ssm_out_norm=v_ssm_out_norm, v_w_out=v_w_out, v_norm_mix_post=v_norm_mix_post, v_norm_mlp_pre=v_norm_mlp_pre, v_w_up=v_w_up, v_w_down=v_w_down, v_norm_mlp_post=v_norm_mlp_post)
    weights = {n: given[n] for n in TWIN_WEIGHTS}
    shared = {n: given[n] for n in SHARED_INPUTS}
    per_example = {n: given[n] for n in ['x']}
    grad_fn = _jax.value_and_grad(_loss, argnums=(0, 1))

    def one_microbatch(ex, loss_target):
        ex = dict(ex)
        diff = ex.pop(TWIN_DIFF_INPUT)
        return grad_fn(weights, diff, {**shared, **ex}, loss_target)

    if N_MICROBATCH == 1:
        loss, (grad_w, grad_x) = one_microbatch(per_example, given["loss_target"])
    else:
        def body(carry, xs):
            loss_sum, grad_sum = carry
            l_k, (gw_k, gx_k) = one_microbatch(xs[0], xs[1])
            with _jax.named_scope("update"):
                return (loss_sum + l_k, _jax.tree.map(_jnp.add, grad_sum, gw_k)), gx_k

        init = (_jnp.zeros((), _jnp.float32), _jax.tree.map(_jnp.zeros_like, weights))
        (loss, grad_w), grad_x = _jax.lax.scan(body, init, (per_example, given["loss_target"]))
    with _jax.named_scope("update"):
        delta_w, new_m, new_v = {}, {}, {}
        for n in TWIN_WEIGHTS:
            delta_w[n], new_m[n], new_v[n] = _adamw(weights[n], grad_w[n], given["m_" + n], given["v_" + n])
    return (loss, grad_x, *[grad_w[n] for n in TWIN_WEIGHTS], *[delta_w[n] for n in TWIN_WEIGHTS],
            *[new_m[n] for n in TWIN_WEIGHTS], *[new_v[n] for n in TWIN_WEIGHTS])
```

```python
import functools

import jax
import jax.numpy as jnp
from jax import lax
from jax.experimental import pallas as pl
from jax.experimental.pallas import tpu as pltpu

F32 = jnp.float32
BF16 = jnp.bfloat16
HIGHEST = lax.Precision.HIGHEST
MESH = pl.DeviceIdType.MESH

EPS = 1e-6
HEAD_DIM = 64
D_STATE = 128
SSM_GROUPS = 2
CHUNK = 128
CONV_K = 3
SSM_CONV_K = 4
ADAM_LR = 0.001
ADAM_B1 = 0.9
ADAM_B2 = 0.999
ADAM_EPS = 1e-08
ADAM_WD = 0.01
ADAM_STEP = 10

N_DEV = 8
LANES = 128
SUBLANES = 8
VMEM_LIMIT = 48 * 1024 * 1024
ROW_TILE = 512


def _params(sem):
    return pltpu.CompilerParams(dimension_semantics=sem, vmem_limit_bytes=VMEM_LIMIT)


def _call(body, **kw):
    return pl.pallas_call(body, **kw)


def _pick(n, cap):
    best = None
    for t in range(LANES, min(n, cap) + 1, LANES):
        if n % t == 0:
            best = t
    return best or n


def _pick_rows(n, cap):
    best = None
    for t in range(SUBLANES, min(n, cap) + 1, SUBLANES):
        if n % t == 0:
            best = t
    return best or n


def _sigmoid(x):
    return 1.0 / (1.0 + jnp.exp(-x))


def _softplus(x):
    return jnp.maximum(x, 0.0) + jnp.log1p(jnp.exp(-jnp.abs(x)))


def _rms(x):
    return lax.rsqrt(jnp.mean(x * x, axis=-1, keepdims=True) + EPS)


def _rms_bwd(x, r, g, dy):
    gy = dy * g
    dx = r * gy - x * (r * r * r) * jnp.mean(gy * x, axis=-1, keepdims=True)
    return dx, dy * x * r


def _full(shape):
    return pl.BlockSpec(shape, lambda *_: (0,) * len(shape))


def _mm(a, b, *, name, ta=False, tb=False, out_dtypes=(F32,), epi=None, extras=()):
    m, k = (a.shape[1], a.shape[0]) if ta else a.shape
    n = b.shape[0] if tb else b.shape[1]
    tm, tn, tk = _pick(m, 512), _pick(n, 1024), _pick(k, 1024)
    nk = k // tk
    a_spec = pl.BlockSpec((tk, tm), lambda i, j, kk: (kk, i)) if ta else pl.BlockSpec((tm, tk), lambda i, j, kk: (i, kk))
    b_spec = pl.BlockSpec((tn, tk), lambda i, j, kk: (j, kk)) if tb else pl.BlockSpec((tk, tn), lambda i, j, kk: (kk, j))
    o_spec = pl.BlockSpec((tm, tn), lambda i, j, kk: (i, j))
    dims = (((0 if ta else 1,), (1 if tb else 0,)), ((), ()))
    n_ex = len(extras)

    def body(*refs):
        a_ref, b_ref = refs[:2]
        ex = refs[2:2 + n_ex]
        outs = refs[2 + n_ex:-1]
        acc = refs[-1]
        kk = pl.program_id(2)

        @pl.when(kk == 0)
        def _():
            acc[...] = jnp.zeros_like(acc)

        acc[...] += lax.dot_general(a_ref[...].astype(BF16), b_ref[...].astype(BF16), dims, preferred_element_type=F32)

        @pl.when(kk == nk - 1)
        def _():
            res = (acc[...],) if epi is None else epi(acc[...], *[e[...] for e in ex])
            for o, r in zip(outs, res):
                o[...] = r.astype(o.dtype)

    outs = _call(
        body, name=name, grid=(m // tm, n // tn, nk),
        in_specs=[a_spec, b_spec] + [o_spec] * n_ex,
        out_specs=[o_spec] * len(out_dtypes),
        out_shape=[jax.ShapeDtypeStruct((m, n), dt) for dt in out_dtypes],
        scratch_shapes=[pltpu.VMEM((tm, tn), F32)],
        compiler_params=_params(("parallel", "parallel", "arbitrary")),
    )(a, b, *extras)
    return outs[0] if len(out_dtypes) == 1 else outs


def _epi_relu2(acc):
    r = jnp.maximum(acc, 0.0)
    return r, r * r


def _epi_drelu2(acc, r):
    return (acc * (2.0 * r.astype(F32)),)


def _norm_fwd(x, g, *, name):
    t, d = x.shape
    tt = _pick_rows(t, ROW_TILE)

    def body(x_ref, g_ref, h_ref):
        xv = x_ref[...]
        h_ref[...] = (xv * _rms(xv) * g_ref[...]).astype(BF16)

    row = pl.BlockSpec((tt, d), lambda i: (i, 0))
    return _call(body, name=name, grid=(t // tt,), in_specs=[row, _full((1, d))], out_specs=row,
                 out_shape=jax.ShapeDtypeStruct((t, d), BF16), compiler_params=_params(("parallel",)))(x, g)


def _resid_norm(x, n, g1, g2, *, name):
    t, d = x.shape
    tt = _pick_rows(t, ROW_TILE)

    def body(x_ref, n_ref, g1_ref, g2_ref, xo_ref, h_ref):
        nv = n_ref[...]
        xn = x_ref[...] + nv * _rms(nv) * g1_ref[...]
        xo_ref[...] = xn
        h_ref[...] = (xn * _rms(xn) * g2_ref[...]).astype(BF16)

    row = pl.BlockSpec((tt, d), lambda i: (i, 0))
    return _call(body, name=name, grid=(t // tt,), in_specs=[row, row, _full((1, d)), _full((1, d))],
                 out_specs=[row, row],
                 out_shape=[jax.ShapeDtypeStruct((t, d), F32), jax.ShapeDtypeStruct((t, d), BF16)],
                 compiler_params=_params(("parallel",)))(x, n, g1, g2)


def _loss_fwd_bwd(xf, target, *, name):
    t, d = xf.shape
    tt = _pick_rows(t, ROW_TILE)
    nt = t // tt

    def body(x_ref, t_ref, dy_ref, loss_ref, acc):
        i = pl.program_id(0)

        @pl.when(i == 0)
        def _():
            acc[...] = jnp.zeros_like(acc)

        e = x_ref[...] - t_ref[...]
        dy_ref[...] = e * (1.0 / d)
        acc[...] += jnp.sum(e * e, axis=0, keepdims=True)

        @pl.when(i == nt - 1)
        def _():
            loss_ref[...] = jnp.sum(acc[...], axis=-1, keepdims=True) * (0.5 / d)

    row = pl.BlockSpec((tt, d), lambda i: (i, 0))
    return _call(body, name=name, grid=(nt,), in_specs=[row, row], out_specs=[row, _full((1, 1))],
                 out_shape=[jax.ShapeDtypeStruct((t, d), F32), jax.ShapeDtypeStruct((1, 1), F32)],
                 scratch_shapes=[pltpu.VMEM((1, d), F32)], compiler_params=_params(("arbitrary",)))(xf, target)


def _bwd_norm_pair(xin, dh, dres, n, g_in, g_out, *, name):
    t, d = xin.shape
    tt = _pick_rows(t, ROW_TILE)
    n_dh = len(dh)

    def body(*refs):
        x_ref = refs[0]
        dh_refs = refs[1:1 + n_dh]
        dres_ref, n_ref, gi_ref, go_ref, dx_ref, dn_ref, dgi_ref, dgo_ref = refs[1 + n_dh:]
        i = pl.program_id(0)

        @pl.when(i == 0)
        def _():
            dgi_ref[...] = jnp.zeros_like(dgi_ref)
            dgo_ref[...] = jnp.zeros_like(dgo_ref)

        xv = x_ref[...]
        dhv = dh_refs[0][...].astype(F32)
        for r in dh_refs[1:]:
            dhv = dhv + r[...].astype(F32)
        dxh, dgi = _rms_bwd(xv, _rms(xv), gi_ref[...], dhv)
        dx = dres_ref[...] + dxh
        dx_ref[...] = dx
        dgi_ref[...] += jnp.sum(dgi, axis=0, keepdims=True)
        nv = n_ref[...]
        dn, dgo = _rms_bwd(nv, _rms(nv), go_ref[...], dx)
        dn_ref[...] = dn.astype(BF16)
        dgo_ref[...] += jnp.sum(dgo, axis=0, keepdims=True)

    row = pl.BlockSpec((tt, d), lambda i: (i, 0))
    vec = _full((1, d))
    return _call(body, name=name, grid=(t // tt,), in_specs=[row] * (n_dh + 3) + [vec, vec],
                 out_specs=[row, row, vec, vec],
                 out_shape=[jax.ShapeDtypeStruct((t, d), F32), jax.ShapeDtypeStruct((t, d), BF16),
                            jax.ShapeDtypeStruct((1, d), F32), jax.ShapeDtypeStruct((1, d), F32)],
                 compiler_params=_params(("arbitrary",)))(xin, *dh, dres, n, g_in, g_out)


def _bwd_norm_in(xin, dh, dres, g_in, *, name):
    t, d = xin.shape
    tt = _pick_rows(t, ROW_TILE)
    n_dh = len(dh)

    def body(*refs):
        x_ref = refs[0]
        dh_refs = refs[1:1 + n_dh]
        dres_ref, gi_ref, dx_ref, dgi_ref = refs[1 + n_dh:]
        i = pl.program_id(0)

        @pl.when(i == 0)
        def _():
            dgi_ref[...] = jnp.zeros_like(dgi_ref)

        xv = x_ref[...]
        dhv = dh_refs[0][...].astype(F32)
        for r in dh_refs[1:]:
            dhv = dhv + r[...].astype(F32)
        dxh, dgi = _rms_bwd(xv, _rms(xv), gi_ref[...], dhv)
        dx_ref[...] = dres_ref[...] + dxh
        dgi_ref[...] += jnp.sum(dgi, axis=0, keepdims=True)

    row = pl.BlockSpec((tt, d), lambda i: (i, 0))
    vec = _full((1, d))
    return _call(body, name=name, grid=(t // tt,), in_specs=[row] * (n_dh + 2) + [vec],
                 out_specs=[row, vec],
                 out_shape=[jax.ShapeDtypeStruct((t, d), F32), jax.ShapeDtypeStruct((1, d), F32)],
                 compiler_params=_params(("arbitrary",)))(xin, *dh, dres, g_in)


def _bwd_norm_out(n, g_out, dx, *, name):
    t, d = n.shape
    tt = _pick_rows(t, ROW_TILE)

    def body(n_ref, go_ref, dx_ref, dn_ref, dgo_ref):
        i = pl.program_id(0)

        @pl.when(i == 0)
        def _():
            dgo_ref[...] = jnp.zeros_like(dgo_ref)

        nv = n_ref[...]
        dn, dgo = _rms_bwd(nv, _rms(nv), go_ref[...], dx_ref[...])
        dn_ref[...] = dn.astype(BF16)
        dgo_ref[...] += jnp.sum(dgo, axis=0, keepdims=True)

    row = pl.BlockSpec((tt, d), lambda i: (i, 0))
    vec = _full((1, d))
    return _call(body, name=name, grid=(t // tt,), in_specs=[row, vec, row], out_specs=[row, vec],
                 out_shape=[jax.ShapeDtypeStruct((t, d), BF16), jax.ShapeDtypeStruct((1, d), F32)],
                 compiler_params=_params(("arbitrary",)))(n, g_out, dx)


def _shift_down(cur, halo, s):
    return jnp.concatenate([halo[SUBLANES - s:], cur[:cur.shape[0] - s]], axis=0)


def _shift_up(cur, halo, s):
    return jnp.concatenate([cur[s:], halo[:s]], axis=0)


def _conva_fwd(pa, w, g, *, seq, name):
    t, d3 = pa.shape
    d = d3 // 3
    tt = _pick_rows(seq, ROW_TILE)
    tps = seq // tt

    def body(xa_ref, ca_ref, ba_ref, w_ref, g_ref, ya_ref, v_ref, carry):
        i = pl.program_id(0)

        @pl.when(i % tps == 0)
        def _():
            carry[...] = jnp.zeros_like(carry)

        u = ca_ref[...] * xa_ref[...]
        halo = carry[...]
        wv = w_ref[...]
        v = wv[2:3] * u + wv[1:2] * _shift_down(u, halo, 1) + wv[0:1] * _shift_down(u, halo, 2)
        carry[...] = u[tt - SUBLANES:]
        yp = ba_ref[...] * v
        ya_ref[...] = (yp * _rms(yp) * g_ref[...]).astype(BF16)
        v_ref[...] = v

    col = lambda c: pl.BlockSpec((tt, d), lambda i, c=c: (i, c))
    row = pl.BlockSpec((tt, d), lambda i: (i, 0))
    return _call(body, name=name, grid=(t // tt,),
                 in_specs=[col(0), col(1), col(2), _full((CONV_K, d)), _full((1, d))], out_specs=[row, row],
                 out_shape=[jax.ShapeDtypeStruct((t, d), BF16), jax.ShapeDtypeStruct((t, d), F32)],
                 scratch_shapes=[pltpu.VMEM((SUBLANES, d), F32)],
                 compiler_params=_params(("arbitrary",)))(pa, pa, pa, w, g)


def _conva_bwd(dcat, pa, v, w, g, *, seq, name):
    t, d3 = pa.shape
    d = d3 // 3
    tt = _pick_rows(seq, ROW_TILE)
    tps = seq // tt
    nt = t // tt

    def body(dya_ref, xa_ref, ca_ref, ba_ref, v_ref, w_ref, g_ref, dpa_ref, dw_ref, dg_ref, carry):
        i = pl.program_id(0)

        @pl.when(i == 0)
        def _():
            dw_ref[...] = jnp.zeros_like(dw_ref)
            dg_ref[...] = jnp.zeros_like(dg_ref)

        @pl.when(i % tps == 0)
        def _():
            carry[...] = jnp.zeros_like(carry)

        xa, ca, ba, vv = xa_ref[...], ca_ref[...], ba_ref[...], v_ref[...]
        yp = ba * vv
        dyp, dgt = _rms_bwd(yp, _rms(yp), g_ref[...], dya_ref[...])
        dg_ref[...] += jnp.sum(dgt, axis=0, keepdims=True)
        dv = dyp * ba
        halo = carry[...]
        dv1 = _shift_up(dv, halo, 1)
        dv2 = _shift_up(dv, halo, 2)
        carry[...] = dv[:SUBLANES]
        wv = w_ref[...]
        du = wv[2:3] * dv + wv[1:2] * dv1 + wv[0:1] * dv2
        u = ca * xa
        dw_ref[0:1, :] += jnp.sum(u * dv2, axis=0, keepdims=True)
        dw_ref[1:2, :] += jnp.sum(u * dv1, axis=0, keepdims=True)
        dw_ref[2:3, :] += jnp.sum(u * dv, axis=0, keepdims=True)
        dpa_ref[:, 0:d] = (du * ca).astype(BF16)
        dpa_ref[:, d:2 * d] = (du * xa).astype(BF16)
        dpa_ref[:, 2 * d:3 * d] = (dyp * vv).astype(BF16)

    rcol = lambda c: pl.BlockSpec((tt, d), lambda i, c=c: (nt - 1 - i, c))
    return _call(body, name=name, grid=(nt,),
                 in_specs=[rcol(0), rcol(0), rcol(1), rcol(2), rcol(0), _full((CONV_K, d)), _full((1, d))],
                 out_specs=[pl.BlockSpec((tt, d3), lambda i: (nt - 1 - i, 0)), _full((CONV_K, d)), _full((1, d))],
                 out_shape=[jax.ShapeDtypeStruct((t, d3), BF16), jax.ShapeDtypeStruct((CONV_K, d), F32),
                            jax.ShapeDtypeStruct((1, d), F32)],
                 scratch_shapes=[pltpu.VMEM((SUBLANES, d), F32)],
                 compiler_params=_params(("arbitrary",)))(dcat, pa, pa, pa, v, w, g)


def _convb_fwd(pxbc, w, bias, *, seq, name):
    t, c = pxbc.shape
    tt = _pick_rows(seq, ROW_TILE)
    tps = seq // tt

    def body(p_ref, w_ref, b_ref, o_ref, carry):
        i = pl.program_id(0)

        @pl.when(i % tps == 0)
        def _():
            carry[...] = jnp.zeros_like(carry)

        p = p_ref[...]
        halo = carry[...]
        wv = w_ref[...]
        o = wv[3:4] * p + b_ref[...]
        for s in (1, 2, 3):
            o = o + wv[3 - s:4 - s] * _shift_down(p, halo, s)
        carry[...] = p[tt - SUBLANES:]
        o_ref[...] = o

    row = pl.BlockSpec((tt, c), lambda i: (i, 0))
    return _call(body, name=name, grid=(t // tt,), in_specs=[row, _full((SSM_CONV_K, c)), _full((1, c))],
                 out_specs=row, out_shape=jax.ShapeDtypeStruct((t, c), F32),
                 scratch_shapes=[pltpu.VMEM((SUBLANES, c), F32)],
                 compiler_params=_params(("arbitrary",)))(pxbc, w, bias)


def _convb_bwd(dconv, pxbc, w, *, seq, name):
    t, c = pxbc.shape
    tt = _pick_rows(seq, ROW_TILE)
    tps = seq // tt
    nt = t // tt

    def body(dc_ref, p_ref, w_ref, dp_ref, dw_ref, db_ref, carry):
        i = pl.program_id(0)

        @pl.when(i == 0)
        def _():
            dw_ref[...] = jnp.zeros_like(dw_ref)
            db_ref[...] = jnp.zeros_like(db_ref)

        @pl.when(i % tps == 0)
        def _():
            carry[...] = jnp.zeros_like(carry)

        dc = dc_ref[...]
        p = p_ref[...]
        halo = carry[...]
        wv = w_ref[...]
        dp = wv[3:4] * dc
        dw_ref[3:4, :] += jnp.sum(p * dc, axis=0, keepdims=True)
        for s in (1, 2, 3):
            dcs = _shift_up(dc, halo, s)
            dp = dp + wv[3 - s:4 - s] * dcs
            dw_ref[3 - s:4 - s, :] += jnp.sum(p * dcs, axis=0, keepdims=True)
        carry[...] = dc[:SUBLANES]
        db_ref[...] += jnp.sum(dc, axis=0, keepdims=True)
        dp_ref[...] = dp.astype(BF16)

    rrow = pl.BlockSpec((tt, c), lambda i: (nt - 1 - i, 0))
    return _call(body, name=name, grid=(nt,), in_specs=[rrow, rrow, _full((SSM_CONV_K, c))],
                 out_specs=[rrow, _full((SSM_CONV_K, c)), _full((1, c))],
                 out_shape=[jax.ShapeDtypeStruct((t, c), BF16), jax.ShapeDtypeStruct((SSM_CONV_K, c), F32),
                            jax.ShapeDtypeStruct((1, c), F32)],
                 scratch_shapes=[pltpu.VMEM((SUBLANES, c), F32)],
                 compiler_params=_params(("arbitrary",)))(dconv, pxbc, w)


def _ssd_common(c_ref, pdt_ref, dtb_ref, alog_ref, e_ref, d, h):
    cp = c_ref[...]
    sg = _sigmoid(cp)
    act = cp * sg
    pre = pdt_ref[:, 0:h] + dtb_ref[...]
    dt = _softplus(pre)
    a = -jnp.exp(alog_ref[...])
    adt = dt * a
    row = lax.broadcasted_iota(jnp.int32, (CHUNK, CHUNK), 0)
    col = lax.broadcasted_iota(jnp.int32, (CHUNK, CHUNK), 1)
    tril = row >= col
    cs = jnp.dot(tril.astype(F32), adt, precision=HIGHEST, preferred_element_type=F32)
    ev = e_ref[...]
    dt_l = jnp.dot(dt, ev, precision=HIGHEST, preferred_element_type=F32)
    cs_l = jnp.dot(cs, ev, precision=HIGHEST, preferred_element_type=F32)
    return dict(cp=cp, sg=sg, act=act, pre=pre, dt=dt, a=a, cs=cs, dt_l=dt_l, ecs_l=jnp.exp(cs_l),
                tril=tril, row=row, col=col)


def _dot_nt(a, b):
    return lax.dot_general(a, b, (((1,), (1,)), ((), ())), preferred_element_type=F32)


def _dot_tn(a, b):
    return lax.dot_general(a, b, (((0,), (0,)), ((), ())), preferred_element_type=F32)


def _dot(a, b):
    return jnp.dot(a, b, preferred_element_type=F32)


def _ssd_fwd(cpre, pdt, pz, dtb, alog, dsk_lane, gs, emat, *, nseq, seq, name):
    t, xbc = cpre.shape
    d = pz.shape[1]
    h = d // HEAD_DIM
    npair = h // 2
    ppg = npair // SSM_GROUPS
    nc = seq // CHUNK
    gw = d // SSM_GROUPS
    bc0 = d
    cc0 = d + SSM_GROUPS * D_STATE

    def body(c_ref, pdt_ref, z_ref, dtb_ref, alog_ref, dsk_ref, gs_ref, e_ref, ys_ref, y2_ref, hp_ref, h_ref):
        c = pl.program_id(1)

        @pl.when(c == 0)
        def _():
            h_ref[...] = jnp.zeros_like(h_ref)

        q = _ssd_common(c_ref, pdt_ref, dtb_ref, alog_ref, e_ref, d, h)
        act, cs, ecs_l = q["act"], q["cs"], q["ecs_l"]
        xs = act[:, :d]
        xd = xs * q["dt_l"]
        lo = q["col"] < HEAD_DIM
        ys = []
        for g in range(SSM_GROUPS):
            bg = act[:, bc0 + g * D_STATE: bc0 + (g + 1) * D_STATE]
            cgb = act[:, cc0 + g * D_STATE: cc0 + (g + 1) * D_STATE].astype(BF16)
            s = _dot_nt(cgb, bg.astype(BF16))
            for jj in range(ppg):
                j = g * ppg + jj
                sl = slice(LANES * j, LANES * (j + 1))
                xdb = xd[:, sl].astype(BF16)
                hprev = h_ref[j]
                hp_ref[j] = hprev
                yd, st = [], []
                for hh in (2 * j, 2 * j + 1):
                    csc = cs[:, hh:hh + 1]
                    csb = jnp.broadcast_to(csc, (CHUNK, CHUNK))
                    lm = jnp.exp(jnp.where(q["tril"], csb - csb.T, -jnp.inf))
                    yd.append(_dot((s * lm).astype(BF16), xdb))
                    dte = jnp.exp(cs[CHUNK - 1:CHUNK, hh:hh + 1] - csc)
                    st.append(_dot_tn((bg * dte).astype(BF16), xdb))
                ecs = ecs_l[:, sl]
                yoff = _dot(cgb, hprev.astype(BF16)) * ecs
                h_ref[j] = hprev * ecs[CHUNK - 1:CHUNK] + jnp.where(lo, st[0], st[1])
                ys.append(jnp.where(lo, yd[0], yd[1]) + yoff)
        y = jnp.concatenate(ys, axis=1) + dsk_ref[...] * xs
        y2_ref[...] = y
        zv = z_ref[...]
        y3 = y * (zv * _sigmoid(zv))
        for gi in range(SSM_GROUPS):
            seg = y3[:, gi * gw:(gi + 1) * gw]
            ys_ref[:, gi * gw:(gi + 1) * gw] = (seg * _rms(seg) * gs_ref[:, gi * gw:(gi + 1) * gw]).astype(BF16)

    chunk = lambda w: pl.BlockSpec((CHUNK, w), lambda b, c: (b * nc + c, 0))
    vec = lambda w: pl.BlockSpec((1, w), lambda b, c: (0, 0))
    hp_spec = pl.BlockSpec((None, None, npair, D_STATE, LANES), lambda b, c: (b, c, 0, 0, 0))
    return _call(body, name=name, grid=(nseq, nc),
                 in_specs=[chunk(xbc), chunk(LANES), chunk(d), vec(h), vec(h), vec(d), vec(d),
                           pl.BlockSpec((h, d), lambda b, c: (0, 0))],
                 out_specs=[chunk(d), chunk(d), hp_spec],
                 out_shape=[jax.ShapeDtypeStruct((t, d), BF16), jax.ShapeDtypeStruct((t, d), F32),
                            jax.ShapeDtypeStruct((nseq, nc, npair, D_STATE, LANES), F32)],
                 scratch_shapes=[pltpu.VMEM((npair, D_STATE, LANES), F32)],
                 compiler_params=_params(("arbitrary", "arbitrary")))(cpre, pdt, pz, dtb, alog, dsk_lane, gs, emat)


def _ssd_bwd(cpre, pdt, pz, y2, hprev_all, dcat, dtb, alog, dsk_lane, gs, emat, *, nseq, seq, name):
    t, xbc = cpre.shape
    d = pz.shape[1]
    h = d // HEAD_DIM
    npair = h // 2
    ppg = npair // SSM_GROUPS
    nc = seq // CHUNK
    gw = d // SSM_GROUPS
    bc0 = d
    cc0 = d + SSM_GROUPS * D_STATE

    def body(c_ref, pdt_ref, z_ref, y2_ref, hp_ref, dys_ref, dtb_ref, alog_ref, dsk_ref, gs_ref, e_ref,
             dconv_ref, dz_ref, dpdt_ref, dgs_ref, ddsk_ref, ddtb_ref, dalog_ref, dh_ref):
        b = pl.program_id(0)
        c = pl.program_id(1)

        @pl.when(c == 0)
        def _():
            dh_ref[...] = jnp.zeros_like(dh_ref)

        @pl.when((b == 0) & (c == 0))
        def _():
            dgs_ref[...] = jnp.zeros_like(dgs_ref)
            ddsk_ref[...] = jnp.zeros_like(ddsk_ref)
            ddtb_ref[...] = jnp.zeros_like(ddtb_ref)
            dalog_ref[...] = jnp.zeros_like(dalog_ref)

        q = _ssd_common(c_ref, pdt_ref, dtb_ref, alog_ref, e_ref, d, h)
        cp, sg, act, cs, ecs_l, dt_l, a, dt = q["cp"], q["sg"], q["act"], q["cs"], q["ecs_l"], q["dt_l"], q["a"], q["dt"]
        ev = e_ref[...]
        xs = act[:, :d]
        xd = xs * dt_l
        lo = q["col"] < HEAD_DIM
        row16 = lax.broadcasted_iota(jnp.int32, (CHUNK, h), 0)
        hid = lax.broadcasted_iota(jnp.int32, (1, h), 1)

        zv = z_ref[...]
        sz = _sigmoid(zv)
        siluz = zv * sz
        y2v = y2_ref[...]
        y3 = y2v * siluz
        dysv = dys_ref[...]
        dy3s = []
        for gi in range(SSM_GROUPS):
            gsl = slice(gi * gw, (gi + 1) * gw)
            seg = y3[:, gsl]
            dseg, dgt = _rms_bwd(seg, _rms(seg), gs_ref[:, gsl], dysv[:, gsl])
            dy3s.append(dseg)
            dgs_ref[:, gsl] += jnp.sum(dgt, axis=0, keepdims=True)
        dy3 = jnp.concatenate(dy3s, axis=1)
        dy = dy3 * siluz
        dz_ref[...] = (dy3 * y2v * (sz * (1.0 + zv * (1.0 - sz)))).astype(BF16)
        head_sum = lambda v: lax.dot_general(v, ev, (((1,), (1,)), ((), ())), precision=HIGHEST, preferred_element_type=F32)
        ddsk_ref[...] += jnp.sum(head_sum(dy * xs), axis=0, keepdims=True)

        dcs = jnp.zeros((CHUNK, h), F32)
        dxd_parts, yoff_parts, db_parts, dc_parts = [], [], [], []
        for g in range(SSM_GROUPS):
            bg = act[:, bc0 + g * D_STATE: bc0 + (g + 1) * D_STATE]
            cg = act[:, cc0 + g * D_STATE: cc0 + (g + 1) * D_STATE]
            bgb, cgb = bg.astype(BF16), cg.astype(BF16)
            s = _dot_nt(cgb, bgb)
            ds = jnp.zeros((CHUNK, CHUNK), F32)
            dbg = jnp.zeros((CHUNK, D_STATE), F32)
            dcg = jnp.zeros((CHUNK, D_STATE), F32)
            for jj in range(ppg):
                j = g * ppg + jj
                sl = slice(LANES * j, LANES * (j + 1))
                xdj = xd[:, sl]
                xdb = xdj.astype(BF16)
                dyj = dy[:, sl]
                dyb = dyj.astype(BF16)
                hprev = hp_ref[j]
                hpb = hprev.astype(BF16)
                dhn = dh_ref[j]
                dhb = dhn.astype(BF16)
                ecs = ecs_l[:, sl]
                gmat = (dyj * ecs).astype(BF16)
                yoff_parts.append(_dot(cgb, hpb) * ecs)
                dcg = dcg + _dot_nt(gmat, hpb)
                dh_ref[j] = dhn * ecs[CHUNK - 1:CHUNK] + _dot_tn(cgb, gmat)
                t2 = dhn * hprev
                dxd_h = []
                for idx, hh in enumerate((2 * j, 2 * j + 1)):
                    msk = lo if idx == 0 else jnp.logical_not(lo)
                    onehot = (hid == hh).astype(F32)
                    csc = cs[:, hh:hh + 1]
                    csb = jnp.broadcast_to(csc, (CHUNK, CHUNK))
                    lm = jnp.exp(jnp.where(q["tril"], csb - csb.T, -jnp.inf))
                    m = s * lm
                    mb = m.astype(BF16)
                    cs_last = cs[CHUNK - 1:CHUNK, hh:hh + 1]
                    dte = jnp.exp(cs_last - csc)
                    bwb = (bg * dte).astype(BF16)
                    dxd_s = _dot(bwb, dhb)
                    dbw = _dot_nt(jnp.where(msk, xdj, 0.0).astype(BF16), dhb)
                    dbg = dbg + dbw * dte
                    qv = jnp.sum(dbw * bg, axis=-1, keepdims=True) * dte
                    dm = _dot_nt(jnp.where(msk, dyj, 0.0).astype(BF16), xdb)
                    dxd_d = _dot_tn(mb, dyb)
                    wm = dm * m
                    rc = jnp.sum(wm - wm.T, axis=-1, keepdims=True)
                    ds = ds + dm * lm
                    ddec = jnp.sum(jnp.where(msk, t2, 0.0)) * jnp.exp(cs_last)
                    last = jnp.sum(qv) + ddec
                    dcs = dcs + (rc - qv) * onehot + jnp.where(row16 == CHUNK - 1, last * onehot, 0.0)
                    dxd_h.append(dxd_s + dxd_d)
                dxd_parts.append(jnp.where(lo, dxd_h[0], dxd_h[1]))
            dsb = ds.astype(BF16)
            dc_parts.append(dcg + _dot(dsb, bgb))
            db_parts.append(dbg + _dot_tn(dsb, cgb))
        yoff_all = jnp.concatenate(yoff_parts, axis=1)
        dxd_all = jnp.concatenate(dxd_parts, axis=1)
        dcs = dcs + head_sum(dy * yoff_all)
        triu = (q["col"] >= q["row"]).astype(F32)
        dadt = jnp.dot(triu, dcs, precision=HIGHEST, preferred_element_type=F32)
        ddt = dadt * a + head_sum(dxd_all * xs)
        dalog_ref[...] += jnp.sum(dadt * dt, axis=0, keepdims=True) * a
        dpre = ddt * _sigmoid(q["pre"])
        ddtb_ref[...] += jnp.sum(dpre, axis=0, keepdims=True)
        dpdt_ref[...] = jnp.zeros_like(dpdt_ref)
        dpdt_ref[:, 0:h] = dpre.astype(BF16)
        dxs = dxd_all * dt_l + dy * dsk_ref[...]
        dact = jnp.concatenate([dxs] + db_parts + dc_parts, axis=1)
        dconv_ref[...] = dact * (sg * (1.0 + cp * (1.0 - sg)))

    rchunk = lambda w, cb=0: pl.BlockSpec((CHUNK, w), lambda b, c, cb=cb: (b * nc + nc - 1 - c, cb))
    vec = lambda w: pl.BlockSpec((1, w), lambda b, c: (0, 0))
    hp_spec = pl.BlockSpec((None, None, npair, D_STATE, LANES), lambda b, c: (b, nc - 1 - c, 0, 0, 0))
    return _call(body, name=name, grid=(nseq, nc),
                 in_specs=[rchunk(xbc), rchunk(LANES), rchunk(d), rchunk(d), hp_spec, rchunk(d, 1),
                           vec(h), vec(h), vec(d), vec(d), pl.BlockSpec((h, d), lambda b, c: (0, 0))],
                 out_specs=[rchunk(xbc), rchunk(d), rchunk(LANES), vec(d), vec(h), vec(h), vec(h)],
                 out_shape=[jax.ShapeDtypeStruct((t, xbc), F32), jax.ShapeDtypeStruct((t, d), BF16),
                            jax.ShapeDtypeStruct((t, LANES), BF16), jax.ShapeDtypeStruct((1, d), F32),
                            jax.ShapeDtypeStruct((1, h), F32), jax.ShapeDtypeStruct((1, h), F32),
                            jax.ShapeDtypeStruct((1, h), F32)],
                 scratch_shapes=[pltpu.VMEM((npair, D_STATE, LANES), F32)],
                 compiler_params=_params(("arbitrary", "arbitrary")))(
                     cpre, pdt, pz, y2, hprev_all, dcat, dtb, alog, dsk_lane, gs, emat)


def _sum_adamw(parts, w, m, v, *, name):
    n, r, c = parts.shape
    tr = _pick_rows(r, 256)
    bc1 = 1.0 - ADAM_B1 ** ADAM_STEP
    bc2 = 1.0 - ADAM_B2 ** ADAM_STEP

    def body(p_ref, w_ref, m_ref, v_ref, g_ref, d_ref, mo_ref, vo_ref):
        g = p_ref[0]
        for k in range(1, n):
            g = g + p_ref[k]
        mn = ADAM_B1 * m_ref[...] + (1.0 - ADAM_B1) * g
        vn = ADAM_B2 * v_ref[...] + (1.0 - ADAM_B2) * (g * g)
        g_ref[...] = g
        mo_ref[...] = mn
        vo_ref[...] = vn
        d_ref[...] = -ADAM_LR * ((mn / bc1) / (jnp.sqrt(vn / bc2) + ADAM_EPS) + ADAM_WD * w_ref[...])

    blk = pl.BlockSpec((tr, c), lambda i: (i, 0))
    return _call(body, name=name, grid=(r // tr,),
                 in_specs=[pl.BlockSpec((n, tr, c), lambda i: (0, i, 0)), blk, blk, blk], out_specs=[blk] * 4,
                 out_shape=[jax.ShapeDtypeStruct((r, c), F32)] * 4,
                 compiler_params=_params(("parallel",)))(parts, w, m, v)


def _sum_parts(parts, *, name):
    n, r, c = parts.shape
    tr = _pick_rows(r, 256)

    def body(p_ref, g_ref):
        g = p_ref[0]
        for k in range(1, n):
            g = g + p_ref[k]
        g_ref[...] = g

    return _call(body, name=name, grid=(r // tr,), in_specs=[pl.BlockSpec((n, tr, c), lambda i: (0, i, 0))],
                 out_specs=pl.BlockSpec((tr, c), lambda i: (i, 0)), out_shape=jax.ShapeDtypeStruct((r, c), F32),
                 compiler_params=_params(("parallel",)))(parts)


def _peers():
    x, y, c = lax.axis_index("x"), lax.axis_index("y"), lax.axis_index("c")
    me = 4 * x + 2 * y + c
    out = []
    for k in range(1, N_DEV):
        px = (1 - x) if (k >> 2) & 1 else x
        py = (1 - y) if (k >> 1) & 1 else y
        pc = (1 - c) if k & 1 else c
        out.append(((px, py, pc), 4 * px + 2 * py + pc))
    return me, out


def _exchange(src, *, gather, name):
    shape = src.shape if gather else src.shape[1:]

    def body(s_ref, o_ref, send_sems, recv_sems, local_sem):
        me, peers = _peers()
        mine = pltpu.make_async_copy(s_ref if gather else s_ref.at[me], o_ref.at[me], local_sem)
        mine.start()
        sends = []
        for k, (dev, pid) in enumerate(peers):
            cp = pltpu.make_async_remote_copy(
                src_ref=s_ref if gather else s_ref.at[pid], dst_ref=o_ref.at[me],
                send_sem=send_sems.at[k], recv_sem=recv_sems.at[k], device_id=dev, device_id_type=MESH)
            cp.start()
            sends.append(cp)
        for k, (dev, pid) in enumerate(peers):
            pltpu.make_async_remote_copy(
                src_ref=s_ref if gather else s_ref.at[pid], dst_ref=o_ref.at[pid],
                send_sem=send_sems.at[k], recv_sem=recv_sems.at[k], device_id=dev, device_id_type=MESH).wait_recv()
        for cp in sends:
            cp.wait_send()
        mine.wait()

    any_spec = pl.BlockSpec(memory_space=pl.ANY)
    return _call(body, name=name, in_specs=[any_spec], out_specs=any_spec,
                 out_shape=jax.ShapeDtypeStruct((N_DEV,) + tuple(shape), src.dtype),
                 scratch_shapes=[pltpu.SemaphoreType.DMA((N_DEV - 1,)), pltpu.SemaphoreType.DMA((N_DEV - 1,)),
                                 pltpu.SemaphoreType.DMA(())])(src)


def _pack(arrs):
    flat = jnp.concatenate([a.reshape(-1).astype(F32) for a in arrs])
    pad = (-flat.shape[0]) % (SUBLANES * LANES)
    return jnp.pad(flat, (0, pad)).reshape(-1, LANES)


def _unpack(packed, shapes):
    flat = packed.reshape(-1)
    out, off = [], 0
    for s in shapes:
        n = 1
        for v in s:
            n *= v
        out.append(flat[off:off + n].reshape(s))
        off += n
    return out


SMALL = ("norm_mix_pre", "ssm_conv_b", "dt_bias", "a_log", "d_skip", "conv_out_norm", "ssm_out_norm",
         "norm_mix_post", "norm_mlp_pre", "norm_mlp_post", "conv_a_w", "ssm_conv_w")
BIG = ("w_in", "w_out", "w_up", "w_down")
ORDER = ("norm_mix_pre", "w_in", "conv_a_w", "ssm_conv_w", "ssm_conv_b", "dt_bias", "a_log", "d_skip",
         "conv_out_norm", "ssm_out_norm", "w_out", "norm_mix_post", "norm_mlp_pre", "w_up", "w_down", "norm_mlp_post")


def kernel(x, norm_mix_pre, w_in, conv_a_w, ssm_conv_w, ssm_conv_b, dt_bias, a_log, d_skip, conv_out_norm, ssm_out_norm, w_out, norm_mix_post, norm_mlp_pre, w_up, w_down, norm_mlp_post, loss_target, m_norm_mix_pre, m_w_in, m_conv_a_w, m_ssm_conv_w, m_ssm_conv_b, m_dt_bias, m_a_log, m_d_skip, m_conv_out_norm, m_ssm_out_norm, m_w_out, m_norm_mix_post, m_norm_mlp_pre, m_w_up, m_w_down, m_norm_mlp_post, v_norm_mix_pre, v_w_in, v_conv_a_w, v_ssm_conv_w, v_ssm_conv_b, v_dt_bias, v_a_log, v_d_skip, v_conv_out_norm, v_ssm_out_norm, v_w_out, v_norm_mix_post, v_norm_mlp_pre, v_w_up, v_w_down, v_norm_mlp_post):
    W = dict(norm_mix_pre=norm_mix_pre, w_in=w_in, conv_a_w=conv_a_w, ssm_conv_w=ssm_conv_w, ssm_conv_b=ssm_conv_b,
             dt_bias=dt_bias, a_log=a_log, d_skip=d_skip, conv_out_norm=conv_out_norm, ssm_out_norm=ssm_out_norm,
             w_out=w_out, norm_mix_post=norm_mix_post, norm_mlp_pre=norm_mlp_pre, w_up=w_up, w_down=w_down,
             norm_mlp_post=norm_mlp_post)
    M = dict(norm_mix_pre=m_norm_mix_pre, w_in=m_w_in, conv_a_w=m_conv_a_w, ssm_conv_w=m_ssm_conv_w,
             ssm_conv_b=m_ssm_conv_b, dt_bias=m_dt_bias, a_log=m_a_log, d_skip=m_d_skip,
             conv_out_norm=m_conv_out_norm, ssm_out_norm=m_ssm_out_norm, w_out=m_w_out,
             norm_mix_post=m_norm_mix_post, norm_mlp_pre=m_norm_mlp_pre, w_up=m_w_up, w_down=m_w_down,
             norm_mlp_post=m_norm_mlp_post)
    V = dict(norm_mix_pre=v_norm_mix_pre, w_in=v_w_in, conv_a_w=v_conv_a_w, ssm_conv_w=v_ssm_conv_w,
             ssm_conv_b=v_ssm_conv_b, dt_bias=v_dt_bias, a_log=v_a_log, d_skip=v_d_skip,
             conv_out_norm=v_conv_out_norm, ssm_out_norm=v_ssm_out_norm, w_out=v_w_out,
             norm_mix_post=v_norm_mix_post, norm_mlp_pre=v_norm_mlp_pre, w_up=v_w_up, w_down=v_w_down,
             norm_mlp_post=v_norm_mlp_post)

    nseq, seq, d = x.shape
    t = nseq * seq
    depth = w_in.shape[0]
    h = d // HEAD_DIM
    xbc = d + 2 * SSM_GROUPS * D_STATE
    in_cols = w_in.shape[2] * N_DEV
    d_mix = w_out.shape[1] * N_DEV
    d_ff = w_up.shape[2] * N_DEV
    me = 4 * lax.axis_index("x") + 2 * lax.axis_index("y") + lax.axis_index("c")
    ca_shard = conv_a_w.shape[2]
    sc_shard = ssm_conv_w.shape[2]

    win_g = _exchange(w_in.astype(BF16), gather=True, name="gather_w_in")
    wout_g = _exchange(w_out.astype(BF16), gather=True, name="gather_w_out")
    wup_g = _exchange(w_up.astype(BF16), gather=True, name="gather_w_up")
    wdown_g = _exchange(w_down.astype(BF16), gather=True, name="gather_w_down")
    taps = _pack([conv_a_w, ssm_conv_w])
    taps_g = _exchange(taps, gather=True, name="gather_conv_taps")
    ca_g, sc_g = [], []
    for j in range(N_DEV):
        a_j, s_j = _unpack(taps_g[j], [conv_a_w.shape, ssm_conv_w.shape])
        ca_g.append(a_j)
        sc_g.append(s_j)
    conv_a_full = jnp.concatenate(ca_g, axis=2)
    ssm_conv_full = jnp.concatenate(sc_g, axis=2)

    emat = (lax.broadcasted_iota(jnp.int32, (h, d), 1) // HEAD_DIM == lax.broadcasted_iota(jnp.int32, (h, d), 0)).astype(F32)

    def layer_weights(i):
        win = jnp.transpose(win_g[:, i], (1, 0, 2)).reshape(d, in_cols)
        wdt = jnp.pad(win[:, 4 * d + xbc:], ((0, 0), (0, LANES - h)))
        return dict(
            wa=win[:, :3 * d], wz=win[:, 3 * d:4 * d], wxbc=win[:, 4 * d:4 * d + xbc], wdt=wdt,
            wout=wout_g[:, i].reshape(d_mix, d),
            wup=jnp.transpose(wup_g[:, i], (1, 0, 2)).reshape(d, d_ff),
            wdown=wdown_g[:, i].reshape(d_ff, d))

    vec = lambda name, i: W[name][i].reshape(1, -1)

    xcur = x.reshape(t, d)
    hcur = _norm_fwd(xcur, vec("norm_mix_pre", 0), name="norm_first")
    saved = []
    for i in range(depth):
        lw = layer_weights(i)
        pa = _mm(hcur, lw["wa"], name=f"fwd_proj_a_{i}")
        pz = _mm(hcur, lw["wz"], name=f"fwd_proj_z_{i}")
        pxbc = _mm(hcur, lw["wxbc"], name=f"fwd_proj_xbc_{i}")
        pdt = _mm(hcur, lw["wdt"], name=f"fwd_proj_dt_{i}")
        ya, va = _conva_fwd(pa, conv_a_full[i], vec("conv_out_norm", i), seq=seq, name=f"fwd_conv_a_{i}")
        cpre = _convb_fwd(pxbc, ssm_conv_full[i], vec("ssm_conv_b", i), seq=seq, name=f"fwd_conv_b_{i}")
        dsk_lane = jnp.repeat(W["d_skip"][i], HEAD_DIM).reshape(1, d)
        ys, y2, hprev = _ssd_fwd(cpre, pdt, pz, vec("dt_bias", i), vec("a_log", i), dsk_lane, vec("ssm_out_norm", i),
                                 emat, nseq=nseq, seq=seq, name=f"fwd_ssd_{i}")
        cat = jnp.concatenate([ya, ys], axis=1)
        mix = _mm(cat, lw["wout"], name=f"fwd_out_{i}")
        x1, h2 = _resid_norm(xcur, mix, vec("norm_mix_post", i), vec("norm_mlp_pre", i), name=f"fwd_post_mix_{i}")
        relu_up, f = _mm(h2, lw["wup"], name=f"fwd_up_{i}", out_dtypes=(BF16, BF16), epi=_epi_relu2)
        dn = _mm(f, lw["wdown"], name=f"fwd_down_{i}")
        g_next = vec("norm_mix_pre", i + 1) if i + 1 < depth else vec("norm_mix_pre", 0)
        x2, hnext = _resid_norm(x1, dn, vec("norm_mlp_post", i), g_next, name=f"fwd_post_mlp_{i}")
        saved.append(dict(lw=lw, x0=xcur, h=hcur, pa=pa, pz=pz, pxbc=pxbc, pdt=pdt, va=va, cpre=cpre, y2=y2,
                          hprev=hprev, cat=cat, mix=mix, x1=x1, h2=h2, relu_up=relu_up, f=f, dn=dn,
                          dsk_lane=dsk_lane))
        xcur, hcur = x2, hnext

    dx, loss_part = _loss_fwd_bwd(xcur, loss_target.reshape(t, d), name="loss")
    loss = lax.psum(loss_part[0, 0], ("x", "y", "c"))

    small_grads = {n: [None] * depth for n in SMALL}
    big_out = {n: [None] * depth for n in BIG}
    for i in reversed(range(depth)):
        s = saved[i]
        lw = s["lw"]
        ddn, dg = _bwd_norm_out(s["dn"], vec("norm_mlp_post", i), dx, name=f"bwd_norm_mlp_post_{i}")
        small_grads["norm_mlp_post"][i] = dg
        dup = _mm(ddn, lw["wdown"], tb=True, name=f"bwd_down_dx_{i}", out_dtypes=(BF16,), epi=_epi_drelu2,
                  extras=(s["relu_up"],))
        g_wdown = _mm(s["f"], ddn, ta=True, name=f"bwd_down_dw_{i}")
        dh2 = _mm(dup, lw["wup"], tb=True, name=f"bwd_up_dx_{i}")
        g_wup = _mm(s["h2"], dup, ta=True, name=f"bwd_up_dw_{i}")
        dx1, dmix, dg_pre, dg_post = _bwd_norm_pair(s["x1"], [dh2], dx, s["mix"], vec("norm_mlp_pre", i),
                                                    vec("norm_mix_post", i), name=f"bwd_norm_mix_post_{i}")
        small_grads["norm_mlp_pre"][i] = dg_pre
        small_grads["norm_mix_post"][i] = dg_post
        dcat = _mm(dmix, lw["wout"], tb=True, name=f"bwd_out_dx_{i}")
        g_wout = _mm(s["cat"], dmix, ta=True, name=f"bwd_out_dw_{i}")
        dconv, dz, dpdt, dgs, ddsk, ddtb, dalog = _ssd_bwd(
            s["cpre"], s["pdt"], s["pz"], s["y2"], s["hprev"], dcat, vec("dt_bias", i), vec("a_log", i),
            s["dsk_lane"], vec("ssm_out_norm", i), emat, nseq=nseq, seq=seq, name=f"bwd_ssd_{i}")
        small_grads["ssm_out_norm"][i] = dgs
        small_grads["d_skip"][i] = ddsk
        small_grads["dt_bias"][i] = ddtb
        small_grads["a_log"][i] = dalog
        dpxbc, dscw, dscb = _convb_bwd(dconv, s["pxbc"], ssm_conv_full[i], seq=seq, name=f"bwd_conv_b_{i}")
        small_grads["ssm_conv_w"][i] = dscw
        small_grads["ssm_conv_b"][i] = dscb
        dpa, dcaw, dgca = _conva_bwd(dcat, s["pa"], s["va"], conv_a_full[i], vec("conv_out_norm", i), seq=seq,
                                     name=f"bwd_conv_a_{i}")
        small_grads["conv_a_w"][i] = dcaw
        small_grads["conv_out_norm"][i] = dgca
        dh_parts = [_mm(dpa, lw["wa"], tb=True, name=f"bwd_proj_a_dx_{i}"),
                    _mm(dz, lw["wz"], tb=True, name=f"bwd_proj_z_dx_{i}"),
                    _mm(dpxbc, lw["wxbc"], tb=True, name=f"bwd_proj_xbc_dx_{i}"),
                    _mm(dpdt, lw["wdt"], tb=True, name=f"bwd_proj_dt_dx_{i}")]
        g_win = jnp.concatenate([
            _mm(s["h"], dpa, ta=True, name=f"bwd_proj_a_dw_{i}"),
            _mm(s["h"], dz, ta=True, name=f"bwd_proj_z_dw_{i}"),
            _mm(s["h"], dpxbc, ta=True, name=f"bwd_proj_xbc_dw_{i}"),
            _mm(s["h"], dpdt, ta=True, name=f"bwd_proj_dt_dw_{i}")[:, :h]], axis=1)
        dx, dg_in = _bwd_norm_in(s["x0"], dh_parts, dx1, vec("norm_mix_pre", i), name=f"bwd_norm_mix_pre_{i}")
        small_grads["norm_mix_pre"][i] = dg_in

        blocks = dict(
            w_in=jnp.transpose(g_win.reshape(d, N_DEV, in_cols // N_DEV), (1, 0, 2)),
            w_out=g_wout.reshape(N_DEV, d_mix // N_DEV, d),
            w_up=jnp.transpose(g_wup.reshape(d, N_DEV, d_ff // N_DEV), (1, 0, 2)),
            w_down=g_wdown.reshape(N_DEV, d_ff // N_DEV, d))
        for n in BIG:
            parts = _exchange(blocks[n], gather=False, name=f"scatter_{n}_{i}")
            big_out[n][i] = _sum_adamw(parts, W[n][i], M[n][i], V[n][i], name=f"adamw_{n}_{i}")

    grad_x = dx.reshape(nseq, seq, d)

    small_shapes_full = {n: (depth,) + tuple(small_grads[n][0].shape) for n in SMALL}
    gpack = _pack([jnp.stack(small_grads[n]) for n in SMALL])
    gparts = _exchange(gpack, gather=True, name="allreduce_small")

    def shard_of(n, full):
        if n == "conv_a_w":
            return lax.dynamic_slice_in_dim(full, me * ca_shard, ca_shard, axis=2)
        if n == "ssm_conv_w":
            return lax.dynamic_slice_in_dim(full, me * sc_shard, sc_shard, axis=2)
        return full.reshape(W[n].shape)

    gsum = _sum_parts(gparts, name="sum_small")
    gfull = _unpack(gsum, [small_shapes_full[n] for n in SMALL])
    gsmall = {n: shard_of(n, gf) for n, gf in zip(SMALL, gfull)}
    res = _sum_adamw(_pack([gsmall[n] for n in SMALL])[None], _pack([W[n] for n in SMALL]),
                     _pack([M[n] for n in SMALL]), _pack([V[n] for n in SMALL]), name="adamw_small")
    small_out = [dict(zip(SMALL, _unpack(r, [W[n].shape for n in SMALL]))) for r in res]

    def out_of(kind, n):
        if n in BIG:
            return jnp.stack([big_out[n][i][kind] for i in range(depth)])
        return small_out[kind][n]

    return (loss, grad_x, *[out_of(k, n) for k in range(4) for n in ORDER])
```

```python
import functools

import jax
import jax.numpy as jnp
from jax import lax
from jax.experimental import pallas as pl
from jax.experimental.pallas import tpu as pltpu

F32 = jnp.float32
BF16 = jnp.bfloat16
HIGHEST = lax.Precision.HIGHEST
MESH = pl.DeviceIdType.MESH

EPS = 1e-6
HEAD_DIM = 64
D_STATE = 128
SSM_GROUPS = 2
CHUNK = 128
CONV_K = 3
SSM_CONV_K = 4
ADAM_LR = 0.001
ADAM_B1 = 0.9
ADAM_B2 = 0.999
ADAM_EPS = 1e-08
ADAM_WD = 0.01
ADAM_STEP = 10

N_DEV = 8
LANES = 128
SUBLANES = 8
VMEM_LIMIT = 48 * 1024 * 1024
ROW_TILE = 512


def _params(sem):
    return pltpu.CompilerParams(dimension_semantics=sem, vmem_limit_bytes=VMEM_LIMIT)


def _call(body, **kw):
    return pl.pallas_call(body, **kw)


def _pick(n, cap):
    best = None
    for t in range(LANES, min(n, cap) + 1, LANES):
        if n % t == 0:
            best = t
    return best or n


def _pick_rows(n, cap):
    best = None
    for t in range(SUBLANES, min(n, cap) + 1, SUBLANES):
        if n % t == 0:
            best = t
    return best or n


def _sigmoid(x):
    return 1.0 / (1.0 + jnp.exp(-x))


def _softplus(x):
    return jnp.maximum(x, 0.0) + jnp.log1p(jnp.exp(-jnp.abs(x)))


def _rms(x):
    return lax.rsqrt(jnp.mean(x * x, axis=-1, keepdims=True) + EPS)


def _rms_bwd(x, r, g, dy):
    gy = dy * g
    dx = r * gy - x * (r * r * r) * jnp.mean(gy * x, axis=-1, keepdims=True)
    return dx, dy * x * r


def _full(shape):
    return pl.BlockSpec(shape, lambda *_: (0,) * len(shape))


def _mm(a, b, *, name, ta=False, tb=False, out_dtypes=(F32,), epi=None, extras=()):
    m, k = (a.shape[1], a.shape[0]) if ta else a.shape
    n = b.shape[0] if tb else b.shape[1]
    tm, tn, tk = _pick(m, 512), _pick(n, 1024), _pick(k, 1024)
    nk = k // tk
    a_spec = pl.BlockSpec((tk, tm), lambda i, j, kk: (kk, i)) if ta else pl.BlockSpec((tm, tk), lambda i, j, kk: (i, kk))
    b_spec = pl.BlockSpec((tn, tk), lambda i, j, kk: (j, kk)) if tb else pl.BlockSpec((tk, tn), lambda i, j, kk: (kk, j))
    o_spec = pl.BlockSpec((tm, tn), lambda i, j, kk: (i, j))
    dims = (((0 if ta else 1,), (1 if tb else 0,)), ((), ()))
    n_ex = len(extras)

    def body(*refs):
        a_ref, b_ref = refs[:2]
        ex = refs[2:2 + n_ex]
        outs = refs[2 + n_ex:-1]
        acc = refs[-1]
        kk = pl.program_id(2)

        @pl.when(kk == 0)
        def _():
            acc[...] = jnp.zeros_like(acc)

        acc[...] += lax.dot_general(a_ref[...].astype(BF16), b_ref[...].astype(BF16), dims, preferred_element_type=F32)

        @pl.when(kk == nk - 1)
        def _():
            res = (acc[...],) if epi is None else epi(acc[...], *[e[...] for e in ex])
            for o, r in zip(outs, res):
                o[...] = r.astype(o.dtype)

    outs = _call(
        body, name=name, grid=(m // tm, n // tn, nk),
        in_specs=[a_spec, b_spec] + [o_spec] * n_ex,
        out_specs=[o_spec] * len(out_dtypes),
        out_shape=[jax.ShapeDtypeStruct((m, n), dt) for dt in out_dtypes],
        scratch_shapes=[pltpu.VMEM((tm, tn), F32)],
        compiler_params=_params(("parallel", "parallel", "arbitrary")),
    )(a, b, *extras)
    return outs[0] if len(out_dtypes) == 1 else outs


def _epi_relu2(acc):
    r = jnp.maximum(acc, 0.0)
    return r, r * r


def _epi_drelu2(acc, r):
    return (acc * (2.0 * r.astype(F32)),)


def _norm_fwd(x, g, *, name):
    t, d = x.shape
    tt = _pick_rows(t, ROW_TILE)

    def body(x_ref, g_ref, h_ref):
        xv = x_ref[...]
        h_ref[...] = (xv * _rms(xv) * g_ref[...]).astype(BF16)

    row = pl.BlockSpec((tt, d), lambda i: (i, 0))
    return _call(body, name=name, grid=(t // tt,), in_specs=[row, _full((1, d))], out_specs=row,
                 out_shape=jax.ShapeDtypeStruct((t, d), BF16), compiler_params=_params(("parallel",)))(x, g)


def _resid_norm(x, n, g1, g2, *, name):
    t, d = x.shape
    tt = _pick_rows(t, ROW_TILE)

    def body(x_ref, n_ref, g1_ref, g2_ref, xo_ref, h_ref):
        nv = n_ref[...]
        xn = x_ref[...] + nv * _rms(nv) * g1_ref[...]
        xo_ref[...] = xn
        h_ref[...] = (xn * _rms(xn) * g2_ref[...]).astype(BF16)

    row = pl.BlockSpec((tt, d), lambda i: (i, 0))
    return _call(body, name=name, grid=(t // tt,), in_specs=[row, row, _full((1, d)), _full((1, d))],
                 out_specs=[row, row],
                 out_shape=[jax.ShapeDtypeStruct((t, d), F32), jax.ShapeDtypeStruct((t, d), BF16)],
                 compiler_params=_params(("parallel",)))(x, n, g1, g2)


def _loss_fwd_bwd(xf, target, *, name):
    t, d = xf.shape
    tt = _pick_rows(t, ROW_TILE)
    nt = t // tt

    def body(x_ref, t_ref, dy_ref, loss_ref, acc):
        i = pl.program_id(0)

        @pl.when(i == 0)
        def _():
            acc[...] = jnp.zeros_like(acc)

        e = x_ref[...] - t_ref[...]
        dy_ref[...] = e * (1.0 / d)
        acc[...] += jnp.sum(e * e, axis=0, keepdims=True)

        @pl.when(i == nt - 1)
        def _():
            loss_ref[...] = jnp.sum(acc[...], axis=-1, keepdims=True) * (0.5 / d)

    row = pl.BlockSpec((tt, d), lambda i: (i, 0))
    return _call(body, name=name, grid=(nt,), in_specs=[row, row], out_specs=[row, _full((1, 1))],
                 out_shape=[jax.ShapeDtypeStruct((t, d), F32), jax.ShapeDtypeStruct((1, 1), F32)],
                 scratch_shapes=[pltpu.VMEM((1, d), F32)], compiler_params=_params(("arbitrary",)))(xf, target)


def _bwd_norm_pair(xin, dh, dres, n, g_in, g_out, *, name):
    t, d = xin.shape
    tt = _pick_rows(t, ROW_TILE)
    n_dh = len(dh)

    def body(*refs):
        x_ref = refs[0]
        dh_refs = refs[1:1 + n_dh]
        dres_ref, n_ref, gi_ref, go_ref, dx_ref, dn_ref, dgi_ref, dgo_ref = refs[1 + n_dh:]
        i = pl.program_id(0)

        @pl.when(i == 0)
        def _():
            dgi_ref[...] = jnp.zeros_like(dgi_ref)
            dgo_ref[...] = jnp.zeros_like(dgo_ref)

        xv = x_ref[...]
        dhv = dh_refs[0][...].astype(F32)
        for r in dh_refs[1:]:
            dhv = dhv + r[...].astype(F32)
        dxh, dgi = _rms_bwd(xv, _rms(xv), gi_ref[...], dhv)
        dx = dres_ref[...] + dxh
        dx_ref[...] = dx
        dgi_ref[...] += jnp.sum(dgi, axis=0, keepdims=True)
        nv = n_ref[...]
        dn, dgo = _rms_bwd(nv, _rms(nv), go_ref[...], dx)
        dn_ref[...] = dn.astype(BF16)
        dgo_ref[...] += jnp.sum(dgo, axis=0, keepdims=True)

    row = pl.BlockSpec((tt, d), lambda i: (i, 0))
    vec = _full((1, d))
    return _call(body, name=name, grid=(t // tt,), in_specs=[row] * (n_dh + 3) + [vec, vec],
                 out_specs=[row, row, vec, vec],
                 out_shape=[jax.ShapeDtypeStruct((t, d), F32), jax.ShapeDtypeStruct((t, d), BF16),
                            jax.ShapeDtypeStruct((1, d), F32), jax.ShapeDtypeStruct((1, d), F32)],
                 compiler_params=_params(("arbitrary",)))(xin, *dh, dres, n, g_in, g_out)


def _bwd_norm_in(xin, dh, dres, g_in, *, name):
    t, d = xin.shape
    tt = _pick_rows(t, ROW_TILE)
    n_dh = len(dh)

    def body(*refs):
        x_ref = refs[0]
        dh_refs = refs[1:1 + n_dh]
        dres_ref, gi_ref, dx_ref, dgi_ref = refs[1 + n_dh:]
        i = pl.program_id(0)

        @pl.when(i == 0)
        def _():
            dgi_ref[...] = jnp.zeros_like(dgi_ref)

        xv = x_ref[...]
        dhv = dh_refs[0][...].astype(F32)
        for r in dh_refs[1:]:
            dhv = dhv + r[...].astype(F32)
        dxh, dgi = _rms_bwd(xv, _rms(xv), gi_ref[...], dhv)
        dx_ref[...] = dres_ref[...] + dxh
        dgi_ref[...] += jnp.sum(dgi, axis=0, keepdims=True)

    row = pl.BlockSpec((tt, d), lambda i: (i, 0))
    vec = _full((1, d))
    return _call(body, name=name, grid=(t // tt,), in_specs=[row] * (n_dh + 2) + [vec],
                 out_specs=[row, vec],
                 out_shape=[jax.ShapeDtypeStruct((t, d), F32), jax.ShapeDtypeStruct((1, d), F32)],
                 compiler_params=_params(("arbitrary",)))(xin, *dh, dres, g_in)


def _bwd_norm_out(n, g_out, dx, *, name):
    t, d = n.shape
    tt = _pick_rows(t, ROW_TILE)

    def body(n_ref, go_ref, dx_ref, dn_ref, dgo_ref):
        i = pl.program_id(0)

        @pl.when(i == 0)
        def _():
            dgo_ref[...] = jnp.zeros_like(dgo_ref)

        nv = n_ref[...]
        dn, dgo = _rms_bwd(nv, _rms(nv), go_ref[...], dx_ref[...])
        dn_ref[...] = dn.astype(BF16)
        dgo_ref[...] += jnp.sum(dgo, axis=0, keepdims=True)

    row = pl.BlockSpec((tt, d), lambda i: (i, 0))
    vec = _full((1, d))
    return _call(body, name=name, grid=(t // tt,), in_specs=[row, vec, row], out_specs=[row, vec],
                 out_shape=[jax.ShapeDtypeStruct((t, d), BF16), jax.ShapeDtypeStruct((1, d), F32)],
                 compiler_params=_params(("arbitrary",)))(n, g_out, dx)


def _shift_down(cur, halo, s):
    return jnp.concatenate([halo[SUBLANES - s:], cur[:cur.shape[0] - s]], axis=0)


def _shift_up(cur, halo, s):
    return jnp.concatenate([cur[s:], halo[:s]], axis=0)


def _conva_fwd(pa, w, g, *, seq, name):
    t, d3 = pa.shape
    d = d3 // 3
    tt = _pick_rows(seq, ROW_TILE)
    tps = seq // tt

    def body(xa_ref, ca_ref, ba_ref, w_ref, g_ref, ya_ref, v_ref, carry):
        i = pl.program_id(0)

        @pl.when(i % tps == 0)
        def _():
            carry[...] = jnp.zeros_like(carry)

        u = ca_ref[...] * xa_ref[...]
        halo = carry[...]
        wv = w_ref[...]
        v = wv[2:3] * u + wv[1:2] * _shift_down(u, halo, 1) + wv[0:1] * _shift_down(u, halo, 2)
        carry[...] = u[tt - SUBLANES:]
        yp = ba_ref[...] * v
        ya_ref[...] = (yp * _rms(yp) * g_ref[...]).astype(BF16)
        v_ref[...] = v

    col = lambda c: pl.BlockSpec((tt, d), lambda i, c=c: (i, c))
    row = pl.BlockSpec((tt, d), lambda i: (i, 0))
    return _call(body, name=name, grid=(t // tt,),
                 in_specs=[col(0), col(1), col(2), _full((CONV_K, d)), _full((1, d))], out_specs=[row, row],
                 out_shape=[jax.ShapeDtypeStruct((t, d), BF16), jax.ShapeDtypeStruct((t, d), F32)],
                 scratch_shapes=[pltpu.VMEM((SUBLANES, d), F32)],
                 compiler_params=_params(("arbitrary",)))(pa, pa, pa, w, g)


def _conva_bwd(dcat, pa, v, w, g, *, seq, name):
    t, d3 = pa.shape
    d = d3 // 3
    tt = _pick_rows(seq, ROW_TILE)
    tps = seq // tt
    nt = t // tt

    def body(dya_ref, xa_ref, ca_ref, ba_ref, v_ref, w_ref, g_ref, dpa_ref, dw_ref, dg_ref, carry):
        i = pl.program_id(0)

        @pl.when(i == 0)
        def _():
            dw_ref[...] = jnp.zeros_like(dw_ref)
            dg_ref[...] = jnp.zeros_like(dg_ref)

        @pl.when(i % tps == 0)
        def _():
            carry[...] = jnp.zeros_like(carry)

        xa, ca, ba, vv = xa_ref[...], ca_ref[...], ba_ref[...], v_ref[...]
        yp = ba * vv
        dyp, dgt = _rms_bwd(yp, _rms(yp), g_ref[...], dya_ref[...])
        dg_ref[...] += jnp.sum(dgt, axis=0, keepdims=True)
        dv = dyp * ba
        halo = carry[...]
        dv1 = _shift_up(dv, halo, 1)
        dv2 = _shift_up(dv, halo, 2)
        carry[...] = dv[:SUBLANES]
        wv = w_ref[...]
        du = wv[2:3] * dv + wv[1:2] * dv1 + wv[0:1] * dv2
        u = ca * xa
        dw_ref[0:1, :] += jnp.sum(u * dv2, axis=0, keepdims=True)
        dw_ref[1:2, :] += jnp.sum(u * dv1, axis=0, keepdims=True)
        dw_ref[2:3, :] += jnp.sum(u * dv, axis=0, keepdims=True)
        dpa_ref[:, 0:d] = (du * ca).astype(BF16)
        dpa_ref[:, d:2 * d] = (du * xa).astype(BF16)
        dpa_ref[:, 2 * d:3 * d] = (dyp * vv).astype(BF16)

    rcol = lambda c: pl.BlockSpec((tt, d), lambda i, c=c: (nt - 1 - i, c))
    return _call(body, name=name, grid=(nt,),
                 in_specs=[rcol(0), rcol(0), rcol(1), rcol(2), rcol(0), _full((CONV_K, d)), _full((1, d))],
                 out_specs=[pl.BlockSpec((tt, d3), lambda i: (nt - 1 - i, 0)), _full((CONV_K, d)), _full((1, d))],
                 out_shape=[jax.ShapeDtypeStruct((t, d3), BF16), jax.ShapeDtypeStruct((CONV_K, d), F32),
                            jax.ShapeDtypeStruct((1, d), F32)],
                 scratch_shapes=[pltpu.VMEM((SUBLANES, d), F32)],
                 compiler_params=_params(("arbitrary",)))(dcat, pa, pa, pa, v, w, g)


def _convb_fwd(pxbc, w, bias, *, seq, name):
    t, c = pxbc.shape
    tt = _pick_rows(seq, ROW_TILE)
    tps = seq // tt

    def body(p_ref, w_ref, b_ref, o_ref, carry):
        i = pl.program_id(0)

        @pl.when(i % tps == 0)
        def _():
            carry[...] = jnp.zeros_like(carry)

        p = p_ref[...]
        halo = carry[...]
        wv = w_ref[...]
        o = wv[3:4] * p + b_ref[...]
        for s in (1, 2, 3):
            o = o + wv[3 - s:4 - s] * _shift_down(p, halo, s)
        carry[...] = p[tt - SUBLANES:]
        o_ref[...] = o

    row = pl.BlockSpec((tt, c), lambda i: (i, 0))
    return _call(body, name=name, grid=(t // tt,), in_specs=[row, _full((SSM_CONV_K, c)), _full((1, c))],
                 out_specs=row, out_shape=jax.ShapeDtypeStruct((t, c), F32),
                 scratch_shapes=[pltpu.VMEM((SUBLANES, c), F32)],
                 compiler_params=_params(("arbitrary",)))(pxbc, w, bias)


def _convb_bwd(dconv, pxbc, w, *, seq, name):
    t, c = pxbc.shape
    tt = _pick_rows(seq, ROW_TILE)
    tps = seq // tt
    nt = t // tt

    def body(dc_ref, p_ref, w_ref, dp_ref, dw_ref, db_ref, carry):
        i = pl.program_id(0)

        @pl.when(i == 0)
        def _():
            dw_ref[...] = jnp.zeros_like(dw_ref)
            db_ref[...] = jnp.zeros_like(db_ref)

        @pl.when(i % tps == 0)
        def _():
            carry[...] = jnp.zeros_like(carry)

        dc = dc_ref[...]
        p = p_ref[...]
        halo = carry[...]
        wv = w_ref[...]
        dp = wv[3:4] * dc
        dw_ref[3:4, :] += jnp.sum(p * dc, axis=0, keepdims=True)
        for s in (1, 2, 3):
            dcs = _shift_up(dc, halo, s)
            dp = dp + wv[3 - s:4 - s] * dcs
            dw_ref[3 - s:4 - s, :] += jnp.sum(p * dcs, axis=0, keepdims=True)
        carry[...] = dc[:SUBLANES]
        db_ref[...] += jnp.sum(dc, axis=0, keepdims=True)
        dp_ref[...] = dp.astype(BF16)

    rrow = pl.BlockSpec((tt, c), lambda i: (nt - 1 - i, 0))
    return _call(body, name=name, grid=(nt,), in_specs=[rrow, rrow, _full((SSM_CONV_K, c))],
                 out_specs=[rrow, _full((SSM_CONV_K, c)), _full((1, c))],
                 out_shape=[jax.ShapeDtypeStruct((t, c), BF16), jax.ShapeDtypeStruct((SSM_CONV_K, c), F32),
                            jax.ShapeDtypeStruct((1, c), F32)],
                 scratch_shapes=[pltpu.VMEM((SUBLANES, c), F32)],
                 compiler_params=_params(("arbitrary",)))(dconv, pxbc, w)


def _ssd_common(c_ref, pdt_ref, dtb_ref, alog_ref, e_ref, d, h):
    cp = c_ref[...]
    sg = _sigmoid(cp)
    act = cp * sg
    pre = pdt_ref[:, 0:h] + dtb_ref[...]
    dt = _softplus(pre)
    a = -jnp.exp(alog_ref[...])
    adt = dt * a
    row = lax.broadcasted_iota(jnp.int32, (CHUNK, CHUNK), 0)
    col = lax.broadcasted_iota(jnp.int32, (CHUNK, CHUNK), 1)
    tril = row >= col
    cs = jnp.dot(tril.astype(F32), adt, precision=HIGHEST, preferred_element_type=F32)
    ev = e_ref[...]
    dt_l = jnp.dot(dt, ev, precision=HIGHEST, preferred_element_type=F32)
    cs_l = jnp.dot(cs, ev, precision=HIGHEST, preferred_element_type=F32)
    return dict(cp=cp, sg=sg, act=act, pre=pre, dt=dt, a=a, cs=cs, dt_l=dt_l, ecs_l=jnp.exp(cs_l),
                tril=tril, row=row, col=col)


def _dot_nt(a, b):
    return lax.dot_general(a, b, (((1,), (1,)), ((), ())), preferred_element_type=F32)


def _dot_tn(a, b):
    return lax.dot_general(a, b, (((0,), (0,)), ((), ())), preferred_element_type=F32)


def _dot(a, b):
    return jnp.dot(a, b, preferred_element_type=F32)


def _ssd_fwd(cpre, pdt, pz, dtb, alog, dsk_lane, gs, emat, *, nseq, seq, name):
    t, xbc = cpre.shape
    d = pz.shape[1]
    h = d // HEAD_DIM
    npair = h // 2
    ppg = npair // SSM_GROUPS
    nc = seq // CHUNK
    gw = d // SSM_GROUPS
    bc0 = d
    cc0 = d + SSM_GROUPS * D_STATE

    def body(c_ref, pdt_ref, z_ref, dtb_ref, alog_ref, dsk_ref, gs_ref, e_ref, ys_ref, y2_ref, hp_ref, h_ref):
        c = pl.program_id(1)

        @pl.when(c == 0)
        def _():
            h_ref[...] = jnp.zeros_like(h_ref)

        q = _ssd_common(c_ref, pdt_ref, dtb_ref, alog_ref, e_ref, d, h)
        act, cs, ecs_l = q["act"], q["cs"], q["ecs_l"]
        xs = act[:, :d]
        xd = xs * q["dt_l"]
        lo = q["col"] < HEAD_DIM
        ys = []
        for g in range(SSM_GROUPS):
            bg = act[:, bc0 + g * D_STATE: bc0 + (g + 1) * D_STATE]
            cgb = act[:, cc0 + g * D_STATE: cc0 + (g + 1) * D_STATE].astype(BF16)
            s = _dot_nt(cgb, bg.astype(BF16))
            for jj in range(ppg):
                j = g * ppg + jj
                sl = slice(LANES * j, LANES * (j + 1))
                xdb = xd[:, sl].astype(BF16)
                hprev = h_ref[j]
                hp_ref[j] = hprev
                yd, st = [], []
                for hh in (2 * j, 2 * j + 1):
                    csc = cs[:, hh:hh + 1]
                    csb = jnp.broadcast_to(csc, (CHUNK, CHUNK))
                    lm = jnp.exp(jnp.where(q["tril"], csb - csb.T, -jnp.inf))
                    yd.append(_dot((s * lm).astype(BF16), xdb))
                    dte = jnp.exp(cs[CHUNK - 1:CHUNK, hh:hh + 1] - csc)
                    st.append(_dot_tn((bg * dte).astype(BF16), xdb))
                ecs = ecs_l[:, sl]
                yoff = _dot(cgb, hprev.astype(BF16)) * ecs
                h_ref[j] = hprev * ecs[CHUNK - 1:CHUNK] + jnp.where(lo, st[0], st[1])
                ys.append(jnp.where(lo, yd[0], yd[1]) + yoff)
        y = jnp.concatenate(ys, axis=1) + dsk_ref[...] * xs
        y2_ref[...] = y
        zv = z_ref[...]
        y3 = y * (zv * _sigmoid(zv))
        for gi in range(SSM_GROUPS):
            seg = y3[:, gi * gw:(gi + 1) * gw]
            ys_ref[:, gi * gw:(gi + 1) * gw] = (seg * _rms(seg) * gs_ref[:, gi * gw:(gi + 1) * gw]).astype(BF16)

    chunk = lambda w: pl.BlockSpec((CHUNK, w), lambda b, c: (b * nc + c, 0))
    vec = lambda w: pl.BlockSpec((1, w), lambda b, c: (0, 0))
    hp_spec = pl.BlockSpec((None, None, npair, D_STATE, LANES), lambda b, c: (b, c, 0, 0, 0))
    return _call(body, name=name, grid=(nseq, nc),
                 in_specs=[chunk(xbc), chunk(LANES), chunk(d), vec(h), vec(h), vec(d), vec(d),
                           pl.BlockSpec((h, d), lambda b, c: (0, 0))],
                 out_specs=[chunk(d), chunk(d), hp_spec],
                 out_shape=[jax.ShapeDtypeStruct((t, d), BF16), jax.ShapeDtypeStruct((t, d), F32),
                            jax.ShapeDtypeStruct((nseq, nc, npair, D_STATE, LANES), F32)],
                 scratch_shapes=[pltpu.VMEM((npair, D_STATE, LANES), F32)],
                 compiler_params=_params(("arbitrary", "arbitrary")))(cpre, pdt, pz, dtb, alog, dsk_lane, gs, emat)


def _ssd_bwd(cpre, pdt, pz, y2, hprev_all, dcat, dtb, alog, dsk_lane, gs, emat, *, nseq, seq, name):
    t, xbc = cpre.shape
    d = pz.shape[1]
    h = d // HEAD_DIM
    npair = h // 2
    ppg = npair // SSM_GROUPS
    nc = seq // CHUNK
    gw = d // SSM_GROUPS
    bc0 = d
    cc0 = d + SSM_GROUPS * D_STATE

    def body(c_ref, pdt_ref, z_ref, y2_ref, hp_ref, dys_ref, dtb_ref, alog_ref, dsk_ref, gs_ref, e_ref,
             dconv_ref, dz_ref, dpdt_ref, dgs_ref, ddsk_ref, ddtb_ref, dalog_ref, dh_ref):
        b = pl.program_id(0)
        c = pl.program_id(1)

        @pl.when(c == 0)
        def _():
            dh_ref[...] = jnp.zeros_like(dh_ref)

        @pl.when((b == 0) & (c == 0))
        def _():
            dgs_ref[...] = jnp.zeros_like(dgs_ref)
            ddsk_ref[...] = jnp.zeros_like(ddsk_ref)
            ddtb_ref[...] = jnp.zeros_like(ddtb_ref)
            dalog_ref[...] = jnp.zeros_like(dalog_ref)

        q = _ssd_common(c_ref, pdt_ref, dtb_ref, alog_ref, e_ref, d, h)
        cp, sg, act, cs, ecs_l, dt_l, a, dt = q["cp"], q["sg"], q["act"], q["cs"], q["ecs_l"], q["dt_l"], q["a"], q["dt"]
        ev = e_ref[...]
        xs = act[:, :d]
        xd = xs * dt_l
        lo = q["col"] < HEAD_DIM
        row16 = lax.broadcasted_iota(jnp.int32, (CHUNK, h), 0)
        hid = lax.broadcasted_iota(jnp.int32, (1, h), 1)

        zv = z_ref[...]
        sz = _sigmoid(zv)
        siluz = zv * sz
        y2v = y2_ref[...]
        y3 = y2v * siluz
        dysv = dys_ref[...]
        dy3s = []
        for gi in range(SSM_GROUPS):
            gsl = slice(gi * gw, (gi + 1) * gw)
            seg = y3[:, gsl]
            dseg, dgt = _rms_bwd(seg, _rms(seg), gs_ref[:, gsl], dysv[:, gsl])
            dy3s.append(dseg)
            dgs_ref[:, gsl] += jnp.sum(dgt, axis=0, keepdims=True)
        dy3 = jnp.concatenate(dy3s, axis=1)
        dy = dy3 * siluz
        dz_ref[...] = (dy3 * y2v * (sz * (1.0 + zv * (1.0 - sz)))).astype(BF16)
        head_sum = lambda v: lax.dot_general(v, ev, (((1,), (1,)), ((), ())), precision=HIGHEST, preferred_element_type=F32)
        ddsk_ref[...] += jnp.sum(head_sum(dy * xs), axis=0, keepdims=True)

        dcs = jnp.zeros((CHUNK, h), F32)
        dxd_parts, yoff_parts, db_parts, dc_parts = [], [], [], []
        for g in range(SSM_GROUPS):
            bg = act[:, bc0 + g * D_STATE: bc0 + (g + 1) * D_STATE]
            cg = act[:, cc0 + g * D_STATE: cc0 + (g + 1) * D_STATE]
            bgb, cgb = bg.astype(BF16), cg.astype(BF16)
            s = _dot_nt(cgb, bgb)
            ds = jnp.zeros((CHUNK, CHUNK), F32)
            dbg = jnp.zeros((CHUNK, D_STATE), F32)
            dcg = jnp.zeros((CHUNK, D_STATE), F32)
            for jj in range(ppg):
                j = g * ppg + jj
                sl = slice(LANES * j, LANES * (j + 1))
                xdj = xd[:, sl]
                xdb = xdj.astype(BF16)
                dyj = dy[:, sl]
                dyb = dyj.astype(BF16)
                hprev = hp_ref[j]
                hpb = hprev.astype(BF16)
                dhn = dh_ref[j]
                dhb = dhn.astype(BF16)
                ecs = ecs_l[:, sl]
                gmat = (dyj * ecs).astype(BF16)
                yoff_parts.append(_dot(cgb, hpb) * ecs)
                dcg = dcg + _dot_nt(gmat, hpb)
                dh_ref[j] = dhn * ecs[CHUNK - 1:CHUNK] + _dot_tn(cgb, gmat)
                t2 = dhn * hprev
                dxd_h = []
                for idx, hh in enumerate((2 * j, 2 * j + 1)):
                    msk = lo if idx == 0 else jnp.logical_not(lo)
                    onehot = (hid == hh).astype(F32)
                    csc = cs[:, hh:hh + 1]
                    csb = jnp.broadcast_to(csc, (CHUNK, CHUNK))
                    lm = jnp.exp(jnp.where(q["tril"], csb - csb.T, -jnp.inf))
                    m = s * lm
                    mb = m.astype(BF16)
                    cs_last = cs[CHUNK - 1:CHUNK, hh:hh + 1]
                    dte = jnp.exp(cs_last - csc)
                    bwb = (bg * dte).astype(BF16)
                    dxd_s = _dot(bwb, dhb)
                    dbw = _dot_nt(jnp.where(msk, xdj, 0.0).astype(BF16), dhb)
                    dbg = dbg + dbw * dte
                    qv = jnp.sum(dbw * bg, axis=-1, keepdims=True) * dte
                    dm = _dot_nt(jnp.where(msk, dyj, 0.0).astype(BF16), xdb)
                    dxd_d = _dot_tn(mb, dyb)
                    wm = dm * m
                    rc = jnp.sum(wm - wm.T, axis=-1, keepdims=True)
                    ds = ds + dm * lm
                    ddec = jnp.sum(jnp.where(msk, t2, 0.0)) * jnp.exp(cs_last)
                    last = jnp.sum(qv) + ddec
                    dcs = dcs + (rc - qv) * onehot + jnp.where(row16 == CHUNK - 1, last * onehot, 0.0)
                    dxd_h.append(dxd_s + dxd_d)
                dxd_parts.append(jnp.where(lo, dxd_h[0], dxd_h[1]))
            dsb = ds.astype(BF16)
            dc_parts.append(dcg + _dot(dsb, bgb))
            db_parts.append(dbg + _dot_tn(dsb, cgb))
        yoff_all = jnp.concatenate(yoff_parts, axis=1)
        dxd_all = jnp.concatenate(dxd_parts, axis=1)
        dcs = dcs + head_sum(dy * yoff_all)
        triu = (q["col"] >= q["row"]).astype(F32)
        dadt = jnp.dot(triu, dcs, precision=HIGHEST, preferred_element_type=F32)
        ddt = dadt * a + head_sum(dxd_all * xs)
        dalog_ref[...] += jnp.sum(dadt * dt, axis=0, keepdims=True) * a
        dpre = ddt * _sigmoid(q["pre"])
        ddtb_ref[...] += jnp.sum(dpre, axis=0, keepdims=True)
        dpdt_ref[...] = jnp.zeros_like(dpdt_ref)
        dpdt_ref[:, 0:h] = dpre.astype(BF16)
        dxs = dxd_all * dt_l + dy * dsk_ref[...]
        dact = jnp.concatenate([dxs] + db_parts + dc_parts, axis=1)
        dconv_ref[...] = dact * (sg * (1.0 + cp * (1.0 - sg)))

    rchunk = lambda w, cb=0: pl.BlockSpec((CHUNK, w), lambda b, c, cb=cb: (b * nc + nc - 1 - c, cb))
    vec = lambda w: pl.BlockSpec((1, w), lambda b, c: (0, 0))
    hp_spec = pl.BlockSpec((None, None, npair, D_STATE, LANES), lambda b, c: (b, nc - 1 - c, 0, 0, 0))
    return _call(body, name=name, grid=(nseq, nc),
                 in_specs=[rchunk(xbc), rchunk(LANES), rchunk(d), rchunk(d), hp_spec, rchunk(d, 1),
                           vec(h), vec(h), vec(d), vec(d), pl.BlockSpec((h, d), lambda b, c: (0, 0))],
                 out_specs=[rchunk(xbc), rchunk(d), rchunk(LANES), vec(d), vec(h), vec(h), vec(h)],
                 out_shape=[jax.ShapeDtypeStruct((t, xbc), F32), jax.ShapeDtypeStruct((t, d), BF16),
                            jax.ShapeDtypeStruct((t, LANES), BF16), jax.ShapeDtypeStruct((1, d), F32),
                            jax.ShapeDtypeStruct((1, h), F32), jax.ShapeDtypeStruct((1, h), F32),
                            jax.ShapeDtypeStruct((1, h), F32)],
                 scratch_shapes=[pltpu.VMEM((npair, D_STATE, LANES), F32)],
                 compiler_params=_params(("arbitrary", "arbitrary")))(
                     cpre, pdt, pz, y2, hprev_all, dcat, dtb, alog, dsk_lane, gs, emat)


def _sum_adamw(parts, w, m, v, *, name):
    n, r, c = parts.shape
    tr = _pick_rows(r, 256)
    bc1 = 1.0 - ADAM_B1 ** ADAM_STEP
    bc2 = 1.0 - ADAM_B2 ** ADAM_STEP

    def body(p_ref, w_ref, m_ref, v_ref, g_ref, d_ref, mo_ref, vo_ref):
        g = p_ref[0].astype(F32)
        for k in range(1, n):
            g = g + p_ref[k].astype(F32)
        mn = ADAM_B1 * m_ref[...] + (1.0 - ADAM_B1) * g
        vn = ADAM_B2 * v_ref[...] + (1.0 - ADAM_B2) * (g * g)
        g_ref[...] = g
        mo_ref[...] = mn
        vo_ref[...] = vn
        d_ref[...] = -ADAM_LR * ((mn / bc1) / (jnp.sqrt(vn / bc2) + ADAM_EPS) + ADAM_WD * w_ref[...])

    blk = pl.BlockSpec((tr, c), lambda i: (i, 0))
    return _call(body, name=name, grid=(r // tr,),
                 in_specs=[pl.BlockSpec((n, tr, c), lambda i: (0, i, 0)), blk, blk, blk], out_specs=[blk] * 4,
                 out_shape=[jax.ShapeDtypeStruct((r, c), F32)] * 4,
                 compiler_params=_params(("parallel",)))(parts, w, m, v)


def _sum_parts(parts, *, name):
    n, r, c = parts.shape
    tr = _pick_rows(r, 256)

    def body(p_ref, g_ref):
        g = p_ref[0].astype(F32)
        for k in range(1, n):
            g = g + p_ref[k].astype(F32)
        g_ref[...] = g

    return _call(body, name=name, grid=(r // tr,), in_specs=[pl.BlockSpec((n, tr, c), lambda i: (0, i, 0))],
                 out_specs=pl.BlockSpec((tr, c), lambda i: (i, 0)), out_shape=jax.ShapeDtypeStruct((r, c), F32),
                 compiler_params=_params(("parallel",)))(parts)


def _peers():
    x, y, c = lax.axis_index("x"), lax.axis_index("y"), lax.axis_index("c")
    me = 4 * x + 2 * y + c
    out = []
    for k in range(1, N_DEV):
        px = (1 - x) if (k >> 2) & 1 else x
        py = (1 - y) if (k >> 1) & 1 else y
        pc = (1 - c) if k & 1 else c
        out.append(((px, py, pc), 4 * px + 2 * py + pc))
    return me, out


def _exchange(src, *, gather, name):
    shape = src.shape if gather else src.shape[1:]

    def body(s_ref, o_ref, send_sems, recv_sems, local_sem):
        me, peers = _peers()
        mine = pltpu.make_async_copy(s_ref if gather else s_ref.at[me], o_ref.at[me], local_sem)
        mine.start()
        sends = []
        for k, (dev, pid) in enumerate(peers):
            cp = pltpu.make_async_remote_copy(
                src_ref=s_ref if gather else s_ref.at[pid], dst_ref=o_ref.at[me],
                send_sem=send_sems.at[k], recv_sem=recv_sems.at[k], device_id=dev, device_id_type=MESH)
            cp.start()
            sends.append(cp)
        for k, (dev, pid) in enumerate(peers):
            pltpu.make_async_remote_copy(
                src_ref=s_ref if gather else s_ref.at[pid], dst_ref=o_ref.at[pid],
                send_sem=send_sems.at[k], recv_sem=recv_sems.at[k], device_id=dev, device_id_type=MESH).wait_recv()
        for cp in sends:
            cp.wait_send()
        mine.wait()

    any_spec = pl.BlockSpec(memory_space=pl.ANY)
    return _call(body, name=name, in_specs=[any_spec], out_specs=any_spec,
                 out_shape=jax.ShapeDtypeStruct((N_DEV,) + tuple(shape), src.dtype),
                 scratch_shapes=[pltpu.SemaphoreType.DMA((N_DEV - 1,)), pltpu.SemaphoreType.DMA((N_DEV - 1,)),
                                 pltpu.SemaphoreType.DMA(())])(src)


_HBM = pl.BlockSpec(memory_space=pltpu.HBM)
_SEM = pl.BlockSpec(memory_space=pltpu.SEMAPHORE)
_EFFECT = pltpu.SideEffectType.DATAFLOW_SIDE_EFFECTING


def _split_copies(s_refs, l_refs, send_sems, recv_sems, gather):
    me, peers = _peers()
    local, remote = [], []
    for ti, (s_ref, l_ref) in enumerate(zip(s_refs, l_refs)):
        base = ti * N_DEV
        local.append(pltpu.make_async_copy(s_ref if gather else s_ref.at[me], l_ref.at[me], recv_sems.at[base + N_DEV - 1]))
        for k, (dev, pid) in enumerate(peers):
            remote.append((
                pltpu.make_async_remote_copy(
                    src_ref=s_ref if gather else s_ref.at[pid], dst_ref=l_ref.at[me],
                    send_sem=send_sems.at[base + k], recv_sem=recv_sems.at[base + k], device_id=dev, device_id_type=MESH),
                pltpu.make_async_remote_copy(
                    src_ref=s_ref if gather else s_ref.at[pid], dst_ref=l_ref.at[pid],
                    send_sem=send_sems.at[base + k], recv_sem=recv_sems.at[base + k], device_id=dev, device_id_type=MESH)))
    return local, remote


def _exchange_start(srcs, *, gather, name):
    n = len(srcs)
    srcs = [pltpu.with_memory_space_constraint(s, pltpu.HBM) for s in srcs]
    lands = [pltpu.with_memory_space_constraint(
        lax.empty((N_DEV,) + tuple(s.shape if gather else s.shape[1:]), s.dtype), pltpu.HBM) for s in srcs]

    def body(*refs):
        s_refs, l_refs = refs[:n], refs[n:2 * n]
        send_sems, recv_sems = refs[2 * n], refs[2 * n + 1]
        token = refs[-1]
        local, remote = _split_copies(s_refs, l_refs, send_sems, recv_sems, gather)
        for cp in local:
            cp.start()
        for out_cp, _ in remote:
            out_cp.start()
        token[...] = jnp.zeros_like(token)

    outs = _call(
        body, name=name,
        out_shape=(pltpu.SemaphoreType.DMA((n * N_DEV,)), pltpu.SemaphoreType.DMA((n * N_DEV,)),
                   *[pltpu.HBM(s.shape, s.dtype) for s in srcs], *[pltpu.HBM(l.shape, l.dtype) for l in lands],
                   jax.ShapeDtypeStruct((SUBLANES, LANES), F32)),
        in_specs=[_HBM] * (2 * n), out_specs=(_SEM, _SEM, *[_HBM] * (2 * n), pl.BlockSpec(memory_space=pltpu.VMEM)),
        input_output_aliases={k: k + 2 for k in range(2 * n)},
        compiler_params=pltpu.CompilerParams(has_side_effects=_EFFECT),
    )(*srcs, *lands)
    return dict(n=n, gather=gather, sems=outs[:2], srcs=outs[2:2 + n], lands=outs[2 + n:2 + 2 * n]), outs[-1]


def _exchange_wait(state, after, *, name):
    n, gather = state["n"], state["gather"]
    after = list(after)

    def body(*refs):
        s_refs, l_refs = refs[:n], refs[n:2 * n]
        send_sems, recv_sems = refs[2 * n], refs[2 * n + 1]
        local, remote = _split_copies(s_refs, l_refs, send_sems, recv_sems, gather)
        for out_cp, in_cp in remote:
            out_cp.wait_send()
            in_cp.wait_recv()
        for cp in local:
            cp.wait()

    outs = _call(
        body, name=name,
        out_shape=tuple(pltpu.HBM(a.shape, a.dtype) for a in (*state["srcs"], *state["lands"])),
        in_specs=[_HBM] * (2 * n) + [_SEM, _SEM] + [pl.BlockSpec(memory_space=pl.ANY)] * len(after),
        out_specs=tuple([_HBM] * (2 * n)),
        input_output_aliases={k: k for k in range(2 * n)},
        compiler_params=pltpu.CompilerParams(has_side_effects=_EFFECT),
    )(*state["srcs"], *state["lands"], *state["sems"], *after)
    return outs[n:]


def _pack(arrs):
    flat = jnp.concatenate([a.reshape(-1).astype(F32) for a in arrs])
    pad = (-flat.shape[0]) % (SUBLANES * LANES)
    return jnp.pad(flat, (0, pad)).reshape(-1, LANES)


def _unpack(packed, shapes):
    flat = packed.reshape(-1)
    out, off = [], 0
    for s in shapes:
        n = 1
        for v in s:
            n *= v
        out.append(flat[off:off + n].reshape(s))
        off += n
    return out


SMALL = ("norm_mix_pre", "ssm_conv_b", "dt_bias", "a_log", "d_skip", "conv_out_norm", "ssm_out_norm",
         "norm_mix_post", "norm_mlp_pre", "norm_mlp_post", "conv_a_w", "ssm_conv_w")
BIG = ("w_in", "w_out", "w_up", "w_down")
ORDER = ("norm_mix_pre", "w_in", "conv_a_w", "ssm_conv_w", "ssm_conv_b", "dt_bias", "a_log", "d_skip",
         "conv_out_norm", "ssm_out_norm", "w_out", "norm_mix_post", "norm_mlp_pre", "w_up", "w_down", "norm_mlp_post")


def kernel(x, norm_mix_pre, w_in, conv_a_w, ssm_conv_w, ssm_conv_b, dt_bias, a_log, d_skip, conv_out_norm, ssm_out_norm, w_out, norm_mix_post, norm_mlp_pre, w_up, w_down, norm_mlp_post, loss_target, m_norm_mix_pre, m_w_in, m_conv_a_w, m_ssm_conv_w, m_ssm_conv_b, m_dt_bias, m_a_log, m_d_skip, m_conv_out_norm, m_ssm_out_norm, m_w_out, m_norm_mix_post, m_norm_mlp_pre, m_w_up, m_w_down, m_norm_mlp_post, v_norm_mix_pre, v_w_in, v_conv_a_w, v_ssm_conv_w, v_ssm_conv_b, v_dt_bias, v_a_log, v_d_skip, v_conv_out_norm, v_ssm_out_norm, v_w_out, v_norm_mix_post, v_norm_mlp_pre, v_w_up, v_w_down, v_norm_mlp_post):
    W = dict(norm_mix_pre=norm_mix_pre, w_in=w_in, conv_a_w=conv_a_w, ssm_conv_w=ssm_conv_w, ssm_conv_b=ssm_conv_b,
             dt_bias=dt_bias, a_log=a_log, d_skip=d_skip, conv_out_norm=conv_out_norm, ssm_out_norm=ssm_out_norm,
             w_out=w_out, norm_mix_post=norm_mix_post, norm_mlp_pre=norm_mlp_pre, w_up=w_up, w_down=w_down,
             norm_mlp_post=norm_mlp_post)
    M = dict(norm_mix_pre=m_norm_mix_pre, w_in=m_w_in, conv_a_w=m_conv_a_w, ssm_conv_w=m_ssm_conv_w,
             ssm_conv_b=m_ssm_conv_b, dt_bias=m_dt_bias, a_log=m_a_log, d_skip=m_d_skip,
             conv_out_norm=m_conv_out_norm, ssm_out_norm=m_ssm_out_norm, w_out=m_w_out,
             norm_mix_post=m_norm_mix_post, norm_mlp_pre=m_norm_mlp_pre, w_up=m_w_up, w_down=m_w_down,
             norm_mlp_post=m_norm_mlp_post)
    V = dict(norm_mix_pre=v_norm_mix_pre, w_in=v_w_in, conv_a_w=v_conv_a_w, ssm_conv_w=v_ssm_conv_w,
             ssm_conv_b=v_ssm_conv_b, dt_bias=v_dt_bias, a_log=v_a_log, d_skip=v_d_skip,
             conv_out_norm=v_conv_out_norm, ssm_out_norm=v_ssm_out_norm, w_out=v_w_out,
             norm_mix_post=v_norm_mix_post, norm_mlp_pre=v_norm_mlp_pre, w_up=v_w_up, w_down=v_w_down,
             norm_mlp_post=v_norm_mlp_post)

    nseq, seq, d = x.shape
    t = nseq * seq
    depth = w_in.shape[0]
    h = d // HEAD_DIM
    xbc = d + 2 * SSM_GROUPS * D_STATE
    in_cols = w_in.shape[2] * N_DEV
    d_mix = w_out.shape[1] * N_DEV
    d_ff = w_up.shape[2] * N_DEV
    me = 4 * lax.axis_index("x") + 2 * lax.axis_index("y") + lax.axis_index("c")
    ca_shard = conv_a_w.shape[2]
    sc_shard = ssm_conv_w.shape[2]

    def gather_start(i):
        return _exchange_start([W[n][i].astype(BF16) for n in BIG], gather=True, name=f"gather_start_{i}")

    gstate, token = gather_start(0)
    taps = _pack([conv_a_w, ssm_conv_w])
    taps_g = _exchange(taps, gather=True, name="gather_conv_taps")
    ca_g, sc_g = [], []
    for j in range(N_DEV):
        a_j, s_j = _unpack(taps_g[j], [conv_a_w.shape, ssm_conv_w.shape])
        ca_g.append(a_j)
        sc_g.append(s_j)
    conv_a_full = jnp.concatenate(ca_g, axis=2)
    ssm_conv_full = jnp.concatenate(sc_g, axis=2)

    emat = (lax.broadcasted_iota(jnp.int32, (h, d), 1) // HEAD_DIM == lax.broadcasted_iota(jnp.int32, (h, d), 0)).astype(F32)

    def layer_weights(landed):
        win_g, wout_g, wup_g, wdown_g = landed
        win = jnp.transpose(win_g, (1, 0, 2)).reshape(d, in_cols)
        wdt = jnp.pad(win[:, 4 * d + xbc:], ((0, 0), (0, LANES - h)))
        return dict(
            wa=win[:, :3 * d], wz=win[:, 3 * d:4 * d], wxbc=win[:, 4 * d:4 * d + xbc], wdt=wdt,
            wout=wout_g.reshape(d_mix, d),
            wup=jnp.transpose(wup_g, (1, 0, 2)).reshape(d, d_ff),
            wdown=wdown_g.reshape(d_ff, d))

    vec = lambda name, i: W[name][i].reshape(1, -1)

    xcur = x.reshape(t, d)
    hcur = _norm_fwd(xcur, vec("norm_mix_pre", 0), name="norm_first")
    saved = []
    for i in range(depth):
        cur_state = gstate
        if i + 1 < depth:
            gstate, token = gather_start(i + 1)
        lw = layer_weights(_exchange_wait(cur_state, [hcur, token], name=f"gather_wait_{i}"))
        pa = _mm(hcur, lw["wa"], name=f"fwd_proj_a_{i}")
        pz = _mm(hcur, lw["wz"], name=f"fwd_proj_z_{i}")
        pxbc = _mm(hcur, lw["wxbc"], name=f"fwd_proj_xbc_{i}")
        pdt = _mm(hcur, lw["wdt"], name=f"fwd_proj_dt_{i}")
        ya, va = _conva_fwd(pa, conv_a_full[i], vec("conv_out_norm", i), seq=seq, name=f"fwd_conv_a_{i}")
        cpre = _convb_fwd(pxbc, ssm_conv_full[i], vec("ssm_conv_b", i), seq=seq, name=f"fwd_conv_b_{i}")
        dsk_lane = jnp.repeat(W["d_skip"][i], HEAD_DIM).reshape(1, d)
        ys, y2, hprev = _ssd_fwd(cpre, pdt, pz, vec("dt_bias", i), vec("a_log", i), dsk_lane, vec("ssm_out_norm", i),
                                 emat, nseq=nseq, seq=seq, name=f"fwd_ssd_{i}")
        cat = jnp.concatenate([ya, ys], axis=1)
        mix = _mm(cat, lw["wout"], name=f"fwd_out_{i}")
        x1, h2 = _resid_norm(xcur, mix, vec("norm_mix_post", i), vec("norm_mlp_pre", i), name=f"fwd_post_mix_{i}")
        relu_up, f = _mm(h2, lw["wup"], name=f"fwd_up_{i}", out_dtypes=(BF16, BF16), epi=_epi_relu2)
        dn = _mm(f, lw["wdown"], name=f"fwd_down_{i}")
        g_next = vec("norm_mix_pre", i + 1) if i + 1 < depth else vec("norm_mix_pre", 0)
        x2, hnext = _resid_norm(x1, dn, vec("norm_mlp_post", i), g_next, name=f"fwd_post_mlp_{i}")
        saved.append(dict(lw=lw, x0=xcur, h=hcur, pa=pa, pz=pz, pxbc=pxbc, pdt=pdt, va=va, cpre=cpre, y2=y2,
                          hprev=hprev, cat=cat, mix=mix, x1=x1, h2=h2, relu_up=relu_up, f=f, dn=dn,
                          dsk_lane=dsk_lane))
        xcur, hcur = x2, hnext

    dx, loss_part = _loss_fwd_bwd(xcur, loss_target.reshape(t, d), name="loss")
    loss = lax.psum(loss_part[0, 0], ("x", "y", "c"))

    small_grads = {n: [None] * depth for n in SMALL}
    big_out = {n: [None] * depth for n in BIG}
    def finish(pending, after):
        li, st_a, st_b = pending
        parts = dict(zip(("w_down", "w_up"), _exchange_wait(st_a, after, name=f"scatter_wait_a_{li}")))
        parts.update(zip(("w_out", "w_in"), _exchange_wait(st_b, after, name=f"scatter_wait_b_{li}")))
        for n in BIG:
            big_out[n][li] = _sum_adamw(parts[n], W[n][li], M[n][li], V[n][li], name=f"adamw_{n}_{li}")

    pending = None
    tok = jnp.zeros((1, 1), F32)
    for i in reversed(range(depth)):
        s = saved[i]
        lw = s["lw"]
        ddn, dg = _bwd_norm_out(s["dn"], vec("norm_mlp_post", i) + tok, dx, name=f"bwd_norm_mlp_post_{i}")
        small_grads["norm_mlp_post"][i] = dg
        dup = _mm(ddn, lw["wdown"], tb=True, name=f"bwd_down_dx_{i}", out_dtypes=(BF16,), epi=_epi_drelu2,
                  extras=(s["relu_up"],))
        g_wdown = _mm(s["f"], ddn, ta=True, name=f"bwd_down_dw_{i}", out_dtypes=(BF16,))
        dh2 = _mm(dup, lw["wup"], tb=True, name=f"bwd_up_dx_{i}")
        g_wup = _mm(s["h2"], dup, ta=True, name=f"bwd_up_dw_{i}", out_dtypes=(BF16,))
        st_a, tok_a = _exchange_start(
            [g_wdown.reshape(N_DEV, d_ff // N_DEV, d), jnp.transpose(g_wup.reshape(d, N_DEV, d_ff // N_DEV), (1, 0, 2))],
            gather=False, name=f"scatter_start_a_{i}")
        dx1, dmix, dg_pre, dg_post = _bwd_norm_pair(s["x1"], [dh2], dx, s["mix"], vec("norm_mlp_pre", i) + tok_a[0:1, 0:1],
                                                    vec("norm_mix_post", i), name=f"bwd_norm_mix_post_{i}")
        small_grads["norm_mlp_pre"][i] = dg_pre
        small_grads["norm_mix_post"][i] = dg_post
        dcat = _mm(dmix, lw["wout"], tb=True, name=f"bwd_out_dx_{i}")
        g_wout = _mm(s["cat"], dmix, ta=True, name=f"bwd_out_dw_{i}", out_dtypes=(BF16,))
        dconv, dz, dpdt, dgs, ddsk, ddtb, dalog = _ssd_bwd(
            s["cpre"], s["pdt"], s["pz"], s["y2"], s["hprev"], dcat, vec("dt_bias", i), vec("a_log", i),
            s["dsk_lane"], vec("ssm_out_norm", i), emat, nseq=nseq, seq=seq, name=f"bwd_ssd_{i}")
        small_grads["ssm_out_norm"][i] = dgs
        small_grads["d_skip"][i] = ddsk
        small_grads["dt_bias"][i] = ddtb
        small_grads["a_log"][i] = dalog
        dpxbc, dscw, dscb = _convb_bwd(dconv, s["pxbc"], ssm_conv_full[i], seq=seq, name=f"bwd_conv_b_{i}")
        small_grads["ssm_conv_w"][i] = dscw
        small_grads["ssm_conv_b"][i] = dscb
        dpa, dcaw, dgca = _conva_bwd(dcat, s["pa"], s["va"], conv_a_full[i], vec("conv_out_norm", i), seq=seq,
                                     name=f"bwd_conv_a_{i}")
        small_grads["conv_a_w"][i] = dcaw
        small_grads["conv_out_norm"][i] = dgca
        dh_parts = [_mm(dpa, lw["wa"], tb=True, name=f"bwd_proj_a_dx_{i}"),
                    _mm(dz, lw["wz"], tb=True, name=f"bwd_proj_z_dx_{i}"),
                    _mm(dpxbc, lw["wxbc"], tb=True, name=f"bwd_proj_xbc_dx_{i}"),
                    _mm(dpdt, lw["wdt"], tb=True, name=f"bwd_proj_dt_dx_{i}")]
        g_win = jnp.concatenate([
            _mm(s["h"], dpa, ta=True, name=f"bwd_proj_a_dw_{i}", out_dtypes=(BF16,)),
            _mm(s["h"], dz, ta=True, name=f"bwd_proj_z_dw_{i}", out_dtypes=(BF16,)),
            _mm(s["h"], dpxbc, ta=True, name=f"bwd_proj_xbc_dw_{i}", out_dtypes=(BF16,)),
            _mm(s["h"], dpdt, ta=True, name=f"bwd_proj_dt_dw_{i}", out_dtypes=(BF16,))[:, :h]], axis=1)
        dx, dg_in = _bwd_norm_in(s["x0"], dh_parts, dx1, vec("norm_mix_pre", i), name=f"bwd_norm_mix_pre_{i}")
        small_grads["norm_mix_pre"][i] = dg_in

        st_b, tok_b = _exchange_start(
            [g_wout.reshape(N_DEV, d_mix // N_DEV, d), jnp.transpose(g_win.reshape(d, N_DEV, in_cols // N_DEV), (1, 0, 2))],
            gather=False, name=f"scatter_start_b_{i}")
        if pending is not None:
            finish(pending, [dx, tok_b])
        pending = (i, st_a, st_b)
        tok = tok_b[0:1, 0:1]
    finish(pending, [dx])

    grad_x = dx.reshape(nseq, seq, d)

    small_shapes_full = {n: (depth,) + tuple(small_grads[n][0].shape) for n in SMALL}
    gpack = _pack([jnp.stack(small_grads[n]) for n in SMALL])
    gparts = _exchange(gpack, gather=True, name="allreduce_small")

    def shard_of(n, full):
        if n == "conv_a_w":
            return lax.dynamic_slice_in_dim(full, me * ca_shard, ca_shard, axis=2)
        if n == "ssm_conv_w":
            return lax.dynamic_slice_in_dim(full, me * sc_shard, sc_shard, axis=2)
        return full.reshape(W[n].shape)

    gsum = _sum_parts(gparts, name="sum_small")
    gfull = _unpack(gsum, [small_shapes_full[n] for n in SMALL])
    gsmall = {n: shard_of(n, gf) for n, gf in zip(SMALL, gfull)}
    res = _sum_adamw(_pack([gsmall[n] for n in SMALL])[None], _pack([W[n] for n in SMALL]),
                     _pack([M[n] for n in SMALL]), _pack([V[n] for n in SMALL]), name="adamw_small")
    small_out = [dict(zip(SMALL, _unpack(r, [W[n].shape for n in SMALL]))) for r in res]

    def out_of(kind, n):
        if n in BIG:
            return jnp.stack([big_out[n][i][kind] for i in range(depth)])
        return small_out[kind][n]

    return (loss, grad_x, *[out_of(k, n) for k in range(4) for n in ORDER])
```

```python
import functools

import jax
import jax.numpy as jnp
from jax import lax
from jax.experimental import pallas as pl
from jax.experimental.pallas import tpu as pltpu

F32 = jnp.float32
BF16 = jnp.bfloat16
HIGHEST = lax.Precision.HIGHEST
MESH = pl.DeviceIdType.MESH

EPS = 1e-6
HEAD_DIM = 64
D_STATE = 128
SSM_GROUPS = 2
CHUNK = 128
CONV_K = 3
SSM_CONV_K = 4
ADAM_LR = 0.001
ADAM_B1 = 0.9
ADAM_B2 = 0.999
ADAM_EPS = 1e-08
ADAM_WD = 0.01
ADAM_STEP = 10

N_DEV = 8
LANES = 128
SUBLANES = 8
VMEM_LIMIT = 48 * 1024 * 1024
ROW_TILE = 512
MM_TILE = 1024


def _params(sem):
    return pltpu.CompilerParams(dimension_semantics=sem, vmem_limit_bytes=VMEM_LIMIT)


def _call(body, **kw):
    return pl.pallas_call(body, **kw)


def _pick(n, cap):
    best = None
    for t in range(LANES, min(n, cap) + 1, LANES):
        if n % t == 0:
            best = t
    return best or n


def _pick_rows(n, cap):
    best = None
    for t in range(SUBLANES, min(n, cap) + 1, SUBLANES):
        if n % t == 0:
            best = t
    return best or n


def _sigmoid(x):
    return 1.0 / (1.0 + jnp.exp(-x))


def _softplus(x):
    return jnp.maximum(x, 0.0) + jnp.log1p(jnp.exp(-jnp.abs(x)))


def _rms(x):
    return lax.rsqrt(jnp.mean(x * x, axis=-1, keepdims=True) + EPS)


def _rms_bwd(x, r, g, dy):
    gy = dy * g
    dx = r * gy - x * (r * r * r) * jnp.mean(gy * x, axis=-1, keepdims=True)
    return dx, dy * x * r


def _full(shape):
    return pl.BlockSpec(shape, lambda *_: (0,) * len(shape))


def _mm(a, b, *, name, ta=False, tb=False, out_dtypes=(F32,), epi=None, extras=(), n=None, b_off=0, b_koff=0):
    m, k = (a.shape[1], a.shape[0]) if ta else a.shape
    if n is None:
        n = b.shape[0] if tb else b.shape[1]
    tm, tn, tk = _pick(m, MM_TILE), _pick(n, MM_TILE), _pick(k, MM_TILE)
    while b_off % tn or n % tn:
        tn -= LANES
    while b_koff % tk or k % tk:
        tk -= LANES
    nk = k // tk
    nm, nn = m // tm, n // tn
    jo = b_off // tn
    ko = b_koff // tk
    a_bytes = m * k * a.dtype.itemsize
    b_bytes = n * k * b.dtype.itemsize
    m_outer = a_bytes + nm * b_bytes <= b_bytes + nn * a_bytes
    ij = (lambda g0, g1: (g0, g1)) if m_outer else (lambda g0, g1: (g1, g0))
    grid = (nm, nn, nk) if m_outer else (nn, nm, nk)

    def a_map(g0, g1, kk):
        i, _ = ij(g0, g1)
        return (kk, i) if ta else (i, kk)

    def b_map(g0, g1, kk):
        _, j = ij(g0, g1)
        return (j + jo, kk + ko) if tb else (kk + ko, j + jo)

    def o_map(g0, g1, kk):
        return ij(g0, g1)

    a_spec = pl.BlockSpec((tk, tm) if ta else (tm, tk), a_map)
    b_spec = pl.BlockSpec((tn, tk) if tb else (tk, tn), b_map)
    o_spec = pl.BlockSpec((tm, tn), o_map)
    dims = (((0 if ta else 1,), (1 if tb else 0,)), ((), ()))
    n_ex = len(extras)

    def finish(acc, ex, outs):
        res = (acc,) if epi is None else epi(acc, *[e[...] for e in ex])
        for o, r in zip(outs, res):
            o[...] = r.astype(o.dtype)

    def body_single(*refs):
        a_ref, b_ref = refs[:2]
        acc = lax.dot_general(a_ref[...].astype(BF16), b_ref[...].astype(BF16), dims, preferred_element_type=F32)
        finish(acc, refs[2:2 + n_ex], refs[2 + n_ex:])

    def body_multi(*refs):
        a_ref, b_ref = refs[:2]
        acc = refs[-1]
        kk = pl.program_id(2)

        @pl.when(kk == 0)
        def _():
            acc[...] = jnp.zeros_like(acc)

        acc[...] += lax.dot_general(a_ref[...].astype(BF16), b_ref[...].astype(BF16), dims, preferred_element_type=F32)

        @pl.when(kk == nk - 1)
        def _():
            finish(acc[...], refs[2:2 + n_ex], refs[2 + n_ex:-1])

    outs = _call(
        body_single if nk == 1 else body_multi, name=name, grid=grid,
        in_specs=[a_spec, b_spec] + [o_spec] * n_ex,
        out_specs=[o_spec] * len(out_dtypes),
        out_shape=[jax.ShapeDtypeStruct((m, n), dt) for dt in out_dtypes],
        scratch_shapes=[] if nk == 1 else [pltpu.VMEM((tm, tn), F32)],
        compiler_params=_params(("parallel", "parallel", "arbitrary")),
    )(a, b, *extras)
    return outs[0] if len(out_dtypes) == 1 else outs


def _epi_relu2(acc):
    r = jnp.maximum(acc, 0.0)
    return (r * r,)


def _epi_drelu2(acc, f):
    return (acc * (2.0 * jnp.sqrt(f.astype(F32))),)


def _norm_fwd(x, g, *, name):
    t, d = x.shape
    tt = _pick_rows(t, ROW_TILE)

    def body(x_ref, g_ref, h_ref):
        xv = x_ref[...]
        h_ref[...] = (xv * _rms(xv) * g_ref[...]).astype(BF16)

    row = pl.BlockSpec((tt, d), lambda i: (i, 0))
    return _call(body, name=name, grid=(t // tt,), in_specs=[row, _full((1, d))], out_specs=row,
                 out_shape=jax.ShapeDtypeStruct((t, d), BF16), compiler_params=_params(("parallel",)))(x, g)


def _resid_norm(x, n, g1, g2, *, name):
    t, d = x.shape
    tt = _pick_rows(t, ROW_TILE)

    def body(x_ref, n_ref, g1_ref, g2_ref, xo_ref, h_ref):
        nv = n_ref[...]
        xn = x_ref[...] + nv * _rms(nv) * g1_ref[...]
        xo_ref[...] = xn
        h_ref[...] = (xn * _rms(xn) * g2_ref[...]).astype(BF16)

    row = pl.BlockSpec((tt, d), lambda i: (i, 0))
    return _call(body, name=name, grid=(t // tt,), in_specs=[row, row, _full((1, d)), _full((1, d))],
                 out_specs=[row, row],
                 out_shape=[jax.ShapeDtypeStruct((t, d), F32), jax.ShapeDtypeStruct((t, d), BF16)],
                 compiler_params=_params(("parallel",)))(x, n, g1, g2)


def _loss_fwd_bwd(xf, target, *, name):
    t, d = xf.shape
    tt = _pick_rows(t, ROW_TILE)
    nt = t // tt

    def body(x_ref, t_ref, dy_ref, loss_ref, acc):
        i = pl.program_id(0)

        @pl.when(i == 0)
        def _():
            acc[...] = jnp.zeros_like(acc)

        e = x_ref[...] - t_ref[...]
        dy_ref[...] = e * (1.0 / d)
        acc[...] += jnp.sum(e * e, axis=0, keepdims=True)

        @pl.when(i == nt - 1)
        def _():
            loss_ref[...] = jnp.sum(acc[...], axis=-1, keepdims=True) * (0.5 / d)

    row = pl.BlockSpec((tt, d), lambda i: (i, 0))
    return _call(body, name=name, grid=(nt,), in_specs=[row, row], out_specs=[row, _full((1, 1))],
                 out_shape=[jax.ShapeDtypeStruct((t, d), F32), jax.ShapeDtypeStruct((1, 1), F32)],
                 scratch_shapes=[pltpu.VMEM((1, d), F32)], compiler_params=_params(("arbitrary",)))(xf, target)


def _bwd_norm_pair(xin, dh, dres, n, g_in, g_out, *, name):
    t, d = xin.shape
    tt = _pick_rows(t, ROW_TILE)
    n_dh = len(dh)

    def body(*refs):
        x_ref = refs[0]
        dh_refs = refs[1:1 + n_dh]
        dres_ref, n_ref, gi_ref, go_ref, dx_ref, dn_ref, dgi_ref, dgo_ref = refs[1 + n_dh:]
        i = pl.program_id(0)

        @pl.when(i == 0)
        def _():
            dgi_ref[...] = jnp.zeros_like(dgi_ref)
            dgo_ref[...] = jnp.zeros_like(dgo_ref)

        xv = x_ref[...]
        dhv = dh_refs[0][...].astype(F32)
        for r in dh_refs[1:]:
            dhv = dhv + r[...].astype(F32)
        dxh, dgi = _rms_bwd(xv, _rms(xv), gi_ref[...], dhv)
        dx = dres_ref[...] + dxh
        dx_ref[...] = dx
        dgi_ref[...] += jnp.sum(dgi, axis=0, keepdims=True)
        nv = n_ref[...]
        dn, dgo = _rms_bwd(nv, _rms(nv), go_ref[...], dx)
        dn_ref[...] = dn.astype(BF16)
        dgo_ref[...] += jnp.sum(dgo, axis=0, keepdims=True)

    row = pl.BlockSpec((tt, d), lambda i: (i, 0))
    vec = _full((1, d))
    return _call(body, name=name, grid=(t // tt,), in_specs=[row] * (n_dh + 3) + [vec, vec],
                 out_specs=[row, row, vec, vec],
                 out_shape=[jax.ShapeDtypeStruct((t, d), F32), jax.ShapeDtypeStruct((t, d), BF16),
                            jax.ShapeDtypeStruct((1, d), F32), jax.ShapeDtypeStruct((1, d), F32)],
                 compiler_params=_params(("arbitrary",)))(xin, *dh, dres, n, g_in, g_out)


def _bwd_norm_in(xin, dh, dres, g_in, *, name):
    t, d = xin.shape
    tt = _pick_rows(t, ROW_TILE)
    n_dh = len(dh)

    def body(*refs):
        x_ref = refs[0]
        dh_refs = refs[1:1 + n_dh]
        dres_ref, gi_ref, dx_ref, dgi_ref = refs[1 + n_dh:]
        i = pl.program_id(0)

        @pl.when(i == 0)
        def _():
            dgi_ref[...] = jnp.zeros_like(dgi_ref)

        xv = x_ref[...]
        dhv = dh_refs[0][...].astype(F32)
        for r in dh_refs[1:]:
            dhv = dhv + r[...].astype(F32)
        dxh, dgi = _rms_bwd(xv, _rms(xv), gi_ref[...], dhv)
        dx_ref[...] = dres_ref[...] + dxh
        dgi_ref[...] += jnp.sum(dgi, axis=0, keepdims=True)

    row = pl.BlockSpec((tt, d), lambda i: (i, 0))
    vec = _full((1, d))
    return _call(body, name=name, grid=(t // tt,), in_specs=[row] * (n_dh + 2) + [vec],
                 out_specs=[row, vec],
                 out_shape=[jax.ShapeDtypeStruct((t, d), F32), jax.ShapeDtypeStruct((1, d), F32)],
                 compiler_params=_params(("arbitrary",)))(xin, *dh, dres, g_in)


def _bwd_norm_out(n, g_out, dx, *, name):
    t, d = n.shape
    tt = _pick_rows(t, ROW_TILE)

    def body(n_ref, go_ref, dx_ref, dn_ref, dgo_ref):
        i = pl.program_id(0)

        @pl.when(i == 0)
        def _():
            dgo_ref[...] = jnp.zeros_like(dgo_ref)

        nv = n_ref[...]
        dn, dgo = _rms_bwd(nv, _rms(nv), go_ref[...], dx_ref[...])
        dn_ref[...] = dn.astype(BF16)
        dgo_ref[...] += jnp.sum(dgo, axis=0, keepdims=True)

    row = pl.BlockSpec((tt, d), lambda i: (i, 0))
    vec = _full((1, d))
    return _call(body, name=name, grid=(t // tt,), in_specs=[row, vec, row], out_specs=[row, vec],
                 out_shape=[jax.ShapeDtypeStruct((t, d), BF16), jax.ShapeDtypeStruct((1, d), F32)],
                 compiler_params=_params(("arbitrary",)))(n, g_out, dx)


def _shift_down(cur, halo, s):
    return jnp.concatenate([halo[SUBLANES - s:], cur[:cur.shape[0] - s]], axis=0)


def _shift_up(cur, halo, s):
    return jnp.concatenate([cur[s:], halo[:s]], axis=0)


def _conva_fwd(pa, w, g, *, seq, name):
    t, d3 = pa.shape
    d = d3 // 3
    tt = _pick_rows(seq, ROW_TILE)
    tps = seq // tt

    def body(xa_ref, ca_ref, ba_ref, w_ref, g_ref, ya_ref, v_ref, carry):
        i = pl.program_id(0)

        @pl.when(i % tps == 0)
        def _():
            carry[...] = jnp.zeros_like(carry)

        u = ca_ref[...] * xa_ref[...]
        halo = carry[...]
        wv = w_ref[...]
        v = wv[2:3] * u + wv[1:2] * _shift_down(u, halo, 1) + wv[0:1] * _shift_down(u, halo, 2)
        carry[...] = u[tt - SUBLANES:]
        yp = ba_ref[...] * v
        ya_ref[...] = (yp * _rms(yp) * g_ref[...]).astype(BF16)
        v_ref[...] = v

    col = lambda c: pl.BlockSpec((tt, d), lambda i, c=c: (i, c))
    row = pl.BlockSpec((tt, d), lambda i: (i, 0))
    return _call(body, name=name, grid=(t // tt,),
                 in_specs=[col(0), col(1), col(2), _full((CONV_K, d)), _full((1, d))], out_specs=[row, row],
                 out_shape=[jax.ShapeDtypeStruct((t, d), BF16), jax.ShapeDtypeStruct((t, d), F32)],
                 scratch_shapes=[pltpu.VMEM((SUBLANES, d), F32)],
                 compiler_params=_params(("arbitrary",)))(pa, pa, pa, w, g)


def _conva_bwd(dcat, pa, v, w, g, *, seq, name):
    t, d3 = pa.shape
    d = d3 // 3
    tt = _pick_rows(seq, ROW_TILE)
    tps = seq // tt
    nt = t // tt

    def body(dya_ref, xa_ref, ca_ref, ba_ref, v_ref, w_ref, g_ref, dpa_ref, dw_ref, dg_ref, carry):
        i = pl.program_id(0)

        @pl.when(i == 0)
        def _():
            dw_ref[...] = jnp.zeros_like(dw_ref)
            dg_ref[...] = jnp.zeros_like(dg_ref)

        @pl.when(i % tps == 0)
        def _():
            carry[...] = jnp.zeros_like(carry)

        xa, ca, ba, vv = xa_ref[...], ca_ref[...], ba_ref[...], v_ref[...]
        yp = ba * vv
        dyp, dgt = _rms_bwd(yp, _rms(yp), g_ref[...], dya_ref[...])
        dg_ref[...] += jnp.sum(dgt, axis=0, keepdims=True)
        dv = dyp * ba
        halo = carry[...]
        dv1 = _shift_up(dv, halo, 1)
        dv2 = _shift_up(dv, halo, 2)
        carry[...] = dv[:SUBLANES]
        wv = w_ref[...]
        du = wv[2:3] * dv + wv[1:2] * dv1 + wv[0:1] * dv2
        u = ca * xa
        dw_ref[0:1, :] += jnp.sum(u * dv2, axis=0, keepdims=True)
        dw_ref[1:2, :] += jnp.sum(u * dv1, axis=0, keepdims=True)
        dw_ref[2:3, :] += jnp.sum(u * dv, axis=0, keepdims=True)
        dpa_ref[:, 0:d] = (du * ca).astype(BF16)
        dpa_ref[:, d:2 * d] = (du * xa).astype(BF16)
        dpa_ref[:, 2 * d:3 * d] = (dyp * vv).astype(BF16)

    rcol = lambda c: pl.BlockSpec((tt, d), lambda i, c=c: (nt - 1 - i, c))
    return _call(body, name=name, grid=(nt,),
                 in_specs=[rcol(0), rcol(0), rcol(1), rcol(2), rcol(0), _full((CONV_K, d)), _full((1, d))],
                 out_specs=[pl.BlockSpec((tt, d3), lambda i: (nt - 1 - i, 0)), _full((CONV_K, d)), _full((1, d))],
                 out_shape=[jax.ShapeDtypeStruct((t, d3), BF16), jax.ShapeDtypeStruct((CONV_K, d), F32),
                            jax.ShapeDtypeStruct((1, d), F32)],
                 scratch_shapes=[pltpu.VMEM((SUBLANES, d), F32)],
                 compiler_params=_params(("arbitrary",)))(dcat, pa, pa, pa, v, w, g)


def _convb_fwd(pxbc, w, bias, *, seq, name):
    t, c = pxbc.shape
    tt = _pick_rows(seq, ROW_TILE)
    tps = seq // tt

    def body(p_ref, w_ref, b_ref, o_ref, carry):
        i = pl.program_id(0)

        @pl.when(i % tps == 0)
        def _():
            carry[...] = jnp.zeros_like(carry)

        p = p_ref[...]
        halo = carry[...]
        wv = w_ref[...]
        o = wv[3:4] * p + b_ref[...]
        for s in (1, 2, 3):
            o = o + wv[3 - s:4 - s] * _shift_down(p, halo, s)
        carry[...] = p[tt - SUBLANES:]
        o_ref[...] = o

    row = pl.BlockSpec((tt, c), lambda i: (i, 0))
    return _call(body, name=name, grid=(t // tt,), in_specs=[row, _full((SSM_CONV_K, c)), _full((1, c))],
                 out_specs=row, out_shape=jax.ShapeDtypeStruct((t, c), F32),
                 scratch_shapes=[pltpu.VMEM((SUBLANES, c), F32)],
                 compiler_params=_params(("arbitrary",)))(pxbc, w, bias)


def _convb_bwd(dconv, pxbc, w, *, seq, name):
    t, c = pxbc.shape
    tt = _pick_rows(seq, ROW_TILE)
    tps = seq // tt
    nt = t // tt

    def body(dc_ref, p_ref, w_ref, dp_ref, dw_ref, db_ref, carry):
        i = pl.program_id(0)

        @pl.when(i == 0)
        def _():
            dw_ref[...] = jnp.zeros_like(dw_ref)
            db_ref[...] = jnp.zeros_like(db_ref)

        @pl.when(i % tps == 0)
        def _():
            carry[...] = jnp.zeros_like(carry)

        dc = dc_ref[...]
        p = p_ref[...]
        halo = carry[...]
        wv = w_ref[...]
        dp = wv[3:4] * dc
        dw_ref[3:4, :] += jnp.sum(p * dc, axis=0, keepdims=True)
        for s in (1, 2, 3):
            dcs = _shift_up(dc, halo, s)
            dp = dp + wv[3 - s:4 - s] * dcs
            dw_ref[3 - s:4 - s, :] += jnp.sum(p * dcs, axis=0, keepdims=True)
        carry[...] = dc[:SUBLANES]
        db_ref[...] += jnp.sum(dc, axis=0, keepdims=True)
        dp_ref[...] = dp.astype(BF16)

    rrow = pl.BlockSpec((tt, c), lambda i: (nt - 1 - i, 0))
    return _call(body, name=name, grid=(nt,), in_specs=[rrow, rrow, _full((SSM_CONV_K, c))],
                 out_specs=[rrow, _full((SSM_CONV_K, c)), _full((1, c))],
                 out_shape=[jax.ShapeDtypeStruct((t, c), BF16), jax.ShapeDtypeStruct((SSM_CONV_K, c), F32),
                            jax.ShapeDtypeStruct((1, c), F32)],
                 scratch_shapes=[pltpu.VMEM((SUBLANES, c), F32)],
                 compiler_params=_params(("arbitrary",)))(dconv, pxbc, w)


def _ssd_common(c_ref, pdt_ref, dtb_ref, alog_ref, e_ref, d, h):
    cp = c_ref[...]
    sg = _sigmoid(cp)
    act = cp * sg
    pre = pdt_ref[:, 0:h] + dtb_ref[...]
    dt = _softplus(pre)
    a = -jnp.exp(alog_ref[...])
    adt = dt * a
    row = lax.broadcasted_iota(jnp.int32, (CHUNK, CHUNK), 0)
    col = lax.broadcasted_iota(jnp.int32, (CHUNK, CHUNK), 1)
    tril = row >= col
    cs = jnp.dot(tril.astype(F32), adt, precision=HIGHEST, preferred_element_type=F32)
    ev = e_ref[...]
    dt_l = jnp.dot(dt, ev, precision=HIGHEST, preferred_element_type=F32)
    cs_l = jnp.dot(cs, ev, precision=HIGHEST, preferred_element_type=F32)
    return dict(cp=cp, sg=sg, act=act, pre=pre, dt=dt, a=a, cs=cs, dt_l=dt_l, ecs_l=jnp.exp(cs_l),
                tril=tril, row=row, col=col)


def _dot_nt(a, b):
    return lax.dot_general(a, b, (((1,), (1,)), ((), ())), preferred_element_type=F32)


def _dot_tn(a, b):
    return lax.dot_general(a, b, (((0,), (0,)), ((), ())), preferred_element_type=F32)


def _dot(a, b):
    return jnp.dot(a, b, preferred_element_type=F32)


def _ssd_fwd(cpre, pdt, pz, ya, dtb, alog, dsk_lane, gs, emat, *, nseq, seq, name):
    t, xbc = cpre.shape
    d = pz.shape[1]
    h = d // HEAD_DIM
    npair = h // 2
    ppg = npair // SSM_GROUPS
    nc = seq // CHUNK
    gw = d // SSM_GROUPS
    bc0 = d
    cc0 = d + SSM_GROUPS * D_STATE

    def body(c_ref, pdt_ref, z_ref, ya_ref, dtb_ref, alog_ref, dsk_ref, gs_ref, e_ref, cat_ref, y2_ref, hp_ref, h_ref):
        c = pl.program_id(1)

        @pl.when(c == 0)
        def _():
            h_ref[...] = jnp.zeros_like(h_ref)

        q = _ssd_common(c_ref, pdt_ref, dtb_ref, alog_ref, e_ref, d, h)
        act, cs, ecs_l = q["act"], q["cs"], q["ecs_l"]
        xs = act[:, :d]
        xd = xs * q["dt_l"]
        lo = q["col"] < HEAD_DIM
        ys = []
        for g in range(SSM_GROUPS):
            bg = act[:, bc0 + g * D_STATE: bc0 + (g + 1) * D_STATE]
            cgb = act[:, cc0 + g * D_STATE: cc0 + (g + 1) * D_STATE].astype(BF16)
            s = _dot_nt(cgb, bg.astype(BF16))
            for jj in range(ppg):
                j = g * ppg + jj
                sl = slice(LANES * j, LANES * (j + 1))
                xdb = xd[:, sl].astype(BF16)
                hprev = h_ref[j]
                hp_ref[j] = hprev
                yd, st = [], []
                for hh in (2 * j, 2 * j + 1):
                    csc = cs[:, hh:hh + 1]
                    csb = jnp.broadcast_to(csc, (CHUNK, CHUNK))
                    lm = jnp.exp(jnp.where(q["tril"], csb - csb.T, -jnp.inf))
                    yd.append(_dot((s * lm).astype(BF16), xdb))
                    dte = jnp.exp(cs[CHUNK - 1:CHUNK, hh:hh + 1] - csc)
                    st.append(_dot_tn((bg * dte).astype(BF16), xdb))
                ecs = ecs_l[:, sl]
                yoff = _dot(cgb, hprev.astype(BF16)) * ecs
                h_ref[j] = hprev * ecs[CHUNK - 1:CHUNK] + jnp.where(lo, st[0], st[1])
                ys.append(jnp.where(lo, yd[0], yd[1]) + yoff)
        y = jnp.concatenate(ys, axis=1) + dsk_ref[...] * xs
        y2_ref[...] = y
        zv = z_ref[...]
        y3 = y * (zv * _sigmoid(zv))
        cat_ref[:, 0:d] = ya_ref[...]
        for gi in range(SSM_GROUPS):
            seg = y3[:, gi * gw:(gi + 1) * gw]
            cat_ref[:, d + gi * gw:d + (gi + 1) * gw] = (seg * _rms(seg) * gs_ref[:, gi * gw:(gi + 1) * gw]).astype(BF16)

    chunk = lambda w: pl.BlockSpec((CHUNK, w), lambda b, c: (b * nc + c, 0))
    vec = lambda w: pl.BlockSpec((1, w), lambda b, c: (0, 0))
    hp_spec = pl.BlockSpec((None, None, npair, D_STATE, LANES), lambda b, c: (b, c, 0, 0, 0))
    return _call(body, name=name, grid=(nseq, nc),
                 in_specs=[chunk(xbc), chunk(LANES), chunk(d), chunk(d), vec(h), vec(h), vec(d), vec(d),
                           pl.BlockSpec((h, d), lambda b, c: (0, 0))],
                 out_specs=[chunk(2 * d), chunk(d), hp_spec],
                 out_shape=[jax.ShapeDtypeStruct((t, 2 * d), BF16), jax.ShapeDtypeStruct((t, d), F32),
                            jax.ShapeDtypeStruct((nseq, nc, npair, D_STATE, LANES), F32)],
                 scratch_shapes=[pltpu.VMEM((npair, D_STATE, LANES), F32)],
                 compiler_params=_params(("arbitrary", "arbitrary")))(cpre, pdt, pz, ya, dtb, alog, dsk_lane, gs, emat)


def _ssd_bwd(cpre, pdt, pz, y2, hprev_all, dcat, dtb, alog, dsk_lane, gs, emat, *, nseq, seq, name):
    t, xbc = cpre.shape
    d = pz.shape[1]
    h = d // HEAD_DIM
    npair = h // 2
    ppg = npair // SSM_GROUPS
    nc = seq // CHUNK
    gw = d // SSM_GROUPS
    bc0 = d
    cc0 = d + SSM_GROUPS * D_STATE

    def body(c_ref, pdt_ref, z_ref, y2_ref, hp_ref, dys_ref, dtb_ref, alog_ref, dsk_ref, gs_ref, e_ref,
             dconv_ref, dz_ref, dpdt_ref, dgs_ref, ddsk_ref, ddtb_ref, dalog_ref, dh_ref):
        b = pl.program_id(0)
        c = pl.program_id(1)

        @pl.when(c == 0)
        def _():
            dh_ref[...] = jnp.zeros_like(dh_ref)

        @pl.when((b == 0) & (c == 0))
        def _():
            dgs_ref[...] = jnp.zeros_like(dgs_ref)
            ddsk_ref[...] = jnp.zeros_like(ddsk_ref)
            ddtb_ref[...] = jnp.zeros_like(ddtb_ref)
            dalog_ref[...] = jnp.zeros_like(dalog_ref)

        q = _ssd_common(c_ref, pdt_ref, dtb_ref, alog_ref, e_ref, d, h)
        cp, sg, act, cs, ecs_l, dt_l, a, dt = q["cp"], q["sg"], q["act"], q["cs"], q["ecs_l"], q["dt_l"], q["a"], q["dt"]
        ev = e_ref[...]
        xs = act[:, :d]
        xd = xs * dt_l
        lo = q["col"] < HEAD_DIM
        row16 = lax.broadcasted_iota(jnp.int32, (CHUNK, h), 0)
        hid = lax.broadcasted_iota(jnp.int32, (1, h), 1)

        zv = z_ref[...]
        sz = _sigmoid(zv)
        siluz = zv * sz
        y2v = y2_ref[...]
        y3 = y2v * siluz
        dysv = dys_ref[...]
        dy3s = []
        for gi in range(SSM_GROUPS):
            gsl = slice(gi * gw, (gi + 1) * gw)
            seg = y3[:, gsl]
            dseg, dgt = _rms_bwd(seg, _rms(seg), gs_ref[:, gsl], dysv[:, gsl])
            dy3s.append(dseg)
            dgs_ref[:, gsl] += jnp.sum(dgt, axis=0, keepdims=True)
        dy3 = jnp.concatenate(dy3s, axis=1)
        dy = dy3 * siluz
        dz_ref[...] = (dy3 * y2v * (sz * (1.0 + zv * (1.0 - sz)))).astype(BF16)
        head_sum = lambda v: lax.dot_general(v, ev, (((1,), (1,)), ((), ())), precision=HIGHEST, preferred_element_type=F32)
        ddsk_ref[...] += jnp.sum(head_sum(dy * xs), axis=0, keepdims=True)

        dcs = jnp.zeros((CHUNK, h), F32)
        dxd_parts, yoff_parts, db_parts, dc_parts = [], [], [], []
        for g in range(SSM_GROUPS):
            bg = act[:, bc0 + g * D_STATE: bc0 + (g + 1) * D_STATE]
            cg = act[:, cc0 + g * D_STATE: cc0 + (g + 1) * D_STATE]
            bgb, cgb = bg.astype(BF16), cg.astype(BF16)
            s = _dot_nt(cgb, bgb)
            ds = jnp.zeros((CHUNK, CHUNK), F32)
            dbg = jnp.zeros((CHUNK, D_STATE), F32)
            dcg = jnp.zeros((CHUNK, D_STATE), F32)
            for jj in range(ppg):
                j = g * ppg + jj
                sl = slice(LANES * j, LANES * (j + 1))
                xdj = xd[:, sl]
                xdb = xdj.astype(BF16)
                dyj = dy[:, sl]
                dyb = dyj.astype(BF16)
                hprev = hp_ref[j]
                hpb = hprev.astype(BF16)
                dhn = dh_ref[j]
                dhb = dhn.astype(BF16)
                ecs = ecs_l[:, sl]
                gmat = (dyj * ecs).astype(BF16)
                yoff_parts.append(_dot(cgb, hpb) * ecs)
                dcg = dcg + _dot_nt(gmat, hpb)
                dh_ref[j] = dhn * ecs[CHUNK - 1:CHUNK] + _dot_tn(cgb, gmat)
                t2 = dhn * hprev
                dxd_h = []
                for idx, hh in enumerate((2 * j, 2 * j + 1)):
                    msk = lo if idx == 0 else jnp.logical_not(lo)
                    onehot = (hid == hh).astype(F32)
                    csc = cs[:, hh:hh + 1]
                    csb = jnp.broadcast_to(csc, (CHUNK, CHUNK))
                    lm = jnp.exp(jnp.where(q["tril"], csb - csb.T, -jnp.inf))
                    m = s * lm
                    mb = m.astype(BF16)
                    cs_last = cs[CHUNK - 1:CHUNK, hh:hh + 1]
                    dte = jnp.exp(cs_last - csc)
                    bwb = (bg * dte).astype(BF16)
                    dxd_s = _dot(bwb, dhb)
                    dbw = _dot_nt(jnp.where(msk, xdj, 0.0).astype(BF16), dhb)
                    dbg = dbg + dbw * dte
                    qv = jnp.sum(dbw * bg, axis=-1, keepdims=True) * dte
                    dm = _dot_nt(jnp.where(msk, dyj, 0.0).astype(BF16), xdb)
                    dxd_d = _dot_tn(mb, dyb)
                    wm = dm * m
                    rc = jnp.sum(wm - wm.T, axis=-1, keepdims=True)
                    ds = ds + dm * lm
                    ddec = jnp.sum(jnp.where(msk, t2, 0.0)) * jnp.exp(cs_last)
                    last = jnp.sum(qv) + ddec
                    dcs = dcs + (rc - qv) * onehot + jnp.where(row16 == CHUNK - 1, last * onehot, 0.0)
                    dxd_h.append(dxd_s + dxd_d)
                dxd_parts.append(jnp.where(lo, dxd_h[0], dxd_h[1]))
            dsb = ds.astype(BF16)
            dc_parts.append(dcg + _dot(dsb, bgb))
            db_parts.append(dbg + _dot_tn(dsb, cgb))
        yoff_all = jnp.concatenate(yoff_parts, axis=1)
        dxd_all = jnp.concatenate(dxd_parts, axis=1)
        dcs = dcs + head_sum(dy * yoff_all)
        triu = (q["col"] >= q["row"]).astype(F32)
        dadt = jnp.dot(triu, dcs, precision=HIGHEST, preferred_element_type=F32)
        ddt = dadt * a + head_sum(dxd_all * xs)
        dalog_ref[...] += jnp.sum(dadt * dt, axis=0, keepdims=True) * a
        dpre = ddt * _sigmoid(q["pre"])
        ddtb_ref[...] += jnp.sum(dpre, axis=0, keepdims=True)
        dpdt_ref[...] = jnp.zeros_like(dpdt_ref)
        dpdt_ref[:, 0:h] = dpre.astype(BF16)
        dxs = dxd_all * dt_l + dy * dsk_ref[...]
        dact = jnp.concatenate([dxs] + db_parts + dc_parts, axis=1)
        dconv_ref[...] = dact * (sg * (1.0 + cp * (1.0 - sg)))

    rchunk = lambda w, cb=0: pl.BlockSpec((CHUNK, w), lambda b, c, cb=cb: (b * nc + nc - 1 - c, cb))
    vec = lambda w: pl.BlockSpec((1, w), lambda b, c: (0, 0))
    hp_spec = pl.BlockSpec((None, None, npair, D_STATE, LANES), lambda b, c: (b, nc - 1 - c, 0, 0, 0))
    return _call(body, name=name, grid=(nseq, nc),
                 in_specs=[rchunk(xbc), rchunk(LANES), rchunk(d), rchunk(d), hp_spec, rchunk(d, 1),
                           vec(h), vec(h), vec(d), vec(d), pl.BlockSpec((h, d), lambda b, c: (0, 0))],
                 out_specs=[rchunk(xbc), rchunk(d), rchunk(LANES), vec(d), vec(h), vec(h), vec(h)],
                 out_shape=[jax.ShapeDtypeStruct((t, xbc), F32), jax.ShapeDtypeStruct((t, d), BF16),
                            jax.ShapeDtypeStruct((t, LANES), BF16), jax.ShapeDtypeStruct((1, d), F32),
                            jax.ShapeDtypeStruct((1, h), F32), jax.ShapeDtypeStruct((1, h), F32),
                            jax.ShapeDtypeStruct((1, h), F32)],
                 scratch_shapes=[pltpu.VMEM((npair, D_STATE, LANES), F32)],
                 compiler_params=_params(("arbitrary", "arbitrary")))(
                     cpre, pdt, pz, y2, hprev_all, dcat, dtb, alog, dsk_lane, gs, emat)


def _sum_adamw(parts, w, m, v, *, name, layer=None, outs=None):
    n, r, c = parts.shape
    tr = _pick_rows(r, 256)
    bc1 = 1.0 - ADAM_B1 ** ADAM_STEP
    bc2 = 1.0 - ADAM_B2 ** ADAM_STEP

    def body(p_ref, w_ref, m_ref, v_ref, *rest):
        g_ref, d_ref, mo_ref, vo_ref = rest[-4:]
        g = p_ref[0].astype(F32)
        for k in range(1, n):
            g = g + p_ref[k].astype(F32)
        mn = ADAM_B1 * m_ref[...] + (1.0 - ADAM_B1) * g
        vn = ADAM_B2 * v_ref[...] + (1.0 - ADAM_B2) * (g * g)
        g_ref[...] = g
        mo_ref[...] = mn
        vo_ref[...] = vn
        d_ref[...] = -ADAM_LR * ((mn / bc1) / (jnp.sqrt(vn / bc2) + ADAM_EPS) + ADAM_WD * w_ref[...])

    p_spec = pl.BlockSpec((n, tr, c), lambda i: (0, i, 0))
    if layer is None:
        blk = pl.BlockSpec((tr, c), lambda i: (i, 0))
        return _call(body, name=name, grid=(r // tr,), in_specs=[p_spec, blk, blk, blk], out_specs=[blk] * 4,
                     out_shape=[jax.ShapeDtypeStruct((r, c), F32)] * 4,
                     compiler_params=_params(("parallel",)))(parts, w, m, v)
    blk = pl.BlockSpec((None, tr, c), lambda i: (layer, i, 0))
    if outs is None:
        outs = [lax.empty(w.shape, F32) for _ in range(4)]
    return _call(body, name=name, grid=(r // tr,),
                 in_specs=[p_spec, blk, blk, blk] + [pl.BlockSpec(memory_space=pl.ANY)] * 4, out_specs=[blk] * 4,
                 out_shape=[jax.ShapeDtypeStruct(w.shape, F32)] * 4, input_output_aliases={4 + k: k for k in range(4)},
                 compiler_params=_params(("parallel",)))(parts, w, m, v, *outs)


def _sum_parts(parts, *, name):
    n, r, c = parts.shape
    tr = _pick_rows(r, 256)

    def body(p_ref, g_ref):
        g = p_ref[0].astype(F32)
        for k in range(1, n):
            g = g + p_ref[k].astype(F32)
        g_ref[...] = g

    return _call(body, name=name, grid=(r // tr,), in_specs=[pl.BlockSpec((n, tr, c), lambda i: (0, i, 0))],
                 out_specs=pl.BlockSpec((tr, c), lambda i: (i, 0)), out_shape=jax.ShapeDtypeStruct((r, c), F32),
                 compiler_params=_params(("parallel",)))(parts)


def _peers():
    x, y, c = lax.axis_index("x"), lax.axis_index("y"), lax.axis_index("c")
    me = 4 * x + 2 * y + c
    out = []
    for k in range(1, N_DEV):
        px = (1 - x) if (k >> 2) & 1 else x
        py = (1 - y) if (k >> 1) & 1 else y
        pc = (1 - c) if k & 1 else c
        out.append(((px, py, pc), 4 * px + 2 * py + pc))
    return me, out


def _exchange(src, *, gather, name):
    shape = src.shape if gather else src.shape[1:]

    def body(s_ref, o_ref, send_sems, recv_sems, local_sem):
        me, peers = _peers()
        mine = pltpu.make_async_copy(s_ref if gather else s_ref.at[me], o_ref.at[me], local_sem)
        mine.start()
        sends = []
        for k, (dev, pid) in enumerate(peers):
            cp = pltpu.make_async_remote_copy(
                src_ref=s_ref if gather else s_ref.at[pid], dst_ref=o_ref.at[me],
                send_sem=send_sems.at[k], recv_sem=recv_sems.at[k], device_id=dev, device_id_type=MESH)
            cp.start()
            sends.append(cp)
        for k, (dev, pid) in enumerate(peers):
            pltpu.make_async_remote_copy(
                src_ref=s_ref if gather else s_ref.at[pid], dst_ref=o_ref.at[pid],
                send_sem=send_sems.at[k], recv_sem=recv_sems.at[k], device_id=dev, device_id_type=MESH).wait_recv()
        for cp in sends:
            cp.wait_send()
        mine.wait()

    any_spec = pl.BlockSpec(memory_space=pl.ANY)
    return _call(body, name=name, in_specs=[any_spec], out_specs=any_spec,
                 out_shape=jax.ShapeDtypeStruct((N_DEV,) + tuple(shape), src.dtype),
                 scratch_shapes=[pltpu.SemaphoreType.DMA((N_DEV - 1,)), pltpu.SemaphoreType.DMA((N_DEV - 1,)),
                                 pltpu.SemaphoreType.DMA(())])(src)


_HBM = pl.BlockSpec(memory_space=pltpu.HBM)
_SEM = pl.BlockSpec(memory_space=pltpu.SEMAPHORE)
_EFFECT = pltpu.SideEffectType.DATAFLOW_SIDE_EFFECTING


def _split_copies(s_refs, l_refs, send_sems, recv_sems, gather):
    me, peers = _peers()
    local, remote = [], []
    for ti, (s_ref, l_ref) in enumerate(zip(s_refs, l_refs)):
        base = ti * N_DEV
        local.append(pltpu.make_async_copy(s_ref if gather else s_ref.at[me], l_ref.at[me], recv_sems.at[base + N_DEV - 1]))
        for k, (dev, pid) in enumerate(peers):
            remote.append((
                pltpu.make_async_remote_copy(
                    src_ref=s_ref if gather else s_ref.at[pid], dst_ref=l_ref.at[me],
                    send_sem=send_sems.at[base + k], recv_sem=recv_sems.at[base + k], device_id=dev, device_id_type=MESH),
                pltpu.make_async_remote_copy(
                    src_ref=s_ref if gather else s_ref.at[pid], dst_ref=l_ref.at[pid],
                    send_sem=send_sems.at[base + k], recv_sem=recv_sems.at[base + k], device_id=dev, device_id_type=MESH)))
    return local, remote


def _exchange_start(srcs, *, gather, name):
    n = len(srcs)
    srcs = [pltpu.with_memory_space_constraint(s, pltpu.HBM) for s in srcs]
    lands = [pltpu.with_memory_space_constraint(
        lax.empty((N_DEV,) + tuple(s.shape if gather else s.shape[1:]), s.dtype), pltpu.HBM) for s in srcs]

    def body(*refs):
        s_refs, l_refs = refs[:n], refs[n:2 * n]
        send_sems, recv_sems = refs[2 * n], refs[2 * n + 1]
        token = refs[-1]
        local, remote = _split_copies(s_refs, l_refs, send_sems, recv_sems, gather)
        for cp in local:
            cp.start()
        for out_cp, _ in remote:
            out_cp.start()
        token[...] = jnp.zeros_like(token)

    outs = _call(
        body, name=name,
        out_shape=(pltpu.SemaphoreType.DMA((n * N_DEV,)), pltpu.SemaphoreType.DMA((n * N_DEV,)),
                   *[pltpu.HBM(s.shape, s.dtype) for s in srcs], *[pltpu.HBM(l.shape, l.dtype) for l in lands],
                   jax.ShapeDtypeStruct((SUBLANES, LANES), F32)),
        in_specs=[_HBM] * (2 * n), out_specs=(_SEM, _SEM, *[_HBM] * (2 * n), pl.BlockSpec(memory_space=pltpu.VMEM)),
        input_output_aliases={k: k + 2 for k in range(2 * n)},
        compiler_params=pltpu.CompilerParams(has_side_effects=_EFFECT),
    )(*srcs, *lands)
    return dict(n=n, gather=gather, sems=outs[:2], srcs=outs[2:2 + n], lands=outs[2 + n:2 + 2 * n]), outs[-1]


def _exchange_wait(state, after, *, name):
    n, gather = state["n"], state["gather"]
    after = list(after)

    def body(*refs):
        s_refs, l_refs = refs[:n], refs[n:2 * n]
        send_sems, recv_sems = refs[2 * n], refs[2 * n + 1]
        local, remote = _split_copies(s_refs, l_refs, send_sems, recv_sems, gather)
        for out_cp, in_cp in remote:
            out_cp.wait_send()
            in_cp.wait_recv()
        for cp in local:
            cp.wait()

    outs = _call(
        body, name=name,
        out_shape=tuple(pltpu.HBM(a.shape, a.dtype) for a in (*state["srcs"], *state["lands"])),
        in_specs=[_HBM] * (2 * n) + [_SEM, _SEM] + [pl.BlockSpec(memory_space=pl.ANY)] * len(after),
        out_specs=tuple([_HBM] * (2 * n)),
        input_output_aliases={k: k for k in range(2 * n)},
        compiler_params=pltpu.CompilerParams(has_side_effects=_EFFECT),
    )(*state["srcs"], *state["lands"], *state["sems"], *after)
    return outs[n:]


def _pack(arrs):
    flat = jnp.concatenate([a.reshape(-1).astype(F32) for a in arrs])
    pad = (-flat.shape[0]) % (SUBLANES * LANES)
    return jnp.pad(flat, (0, pad)).reshape(-1, LANES)


def _unpack(packed, shapes):
    flat = packed.reshape(-1)
    out, off = [], 0
    for s in shapes:
        n = 1
        for v in s:
            n *= v
        out.append(flat[off:off + n].reshape(s))
        off += n
    return out


SMALL = ("norm_mix_pre", "ssm_conv_b", "dt_bias", "a_log", "d_skip", "conv_out_norm", "ssm_out_norm",
         "norm_mix_post", "norm_mlp_pre", "norm_mlp_post", "conv_a_w", "ssm_conv_w")
BIG = ("w_in", "w_out", "w_up", "w_down")
ORDER = ("norm_mix_pre", "w_in", "conv_a_w", "ssm_conv_w", "ssm_conv_b", "dt_bias", "a_log", "d_skip",
         "conv_out_norm", "ssm_out_norm", "w_out", "norm_mix_post", "norm_mlp_pre", "w_up", "w_down", "norm_mlp_post")


def kernel(x, norm_mix_pre, w_in, conv_a_w, ssm_conv_w, ssm_conv_b, dt_bias, a_log, d_skip, conv_out_norm, ssm_out_norm, w_out, norm_mix_post, norm_mlp_pre, w_up, w_down, norm_mlp_post, loss_target, m_norm_mix_pre, m_w_in, m_conv_a_w, m_ssm_conv_w, m_ssm_conv_b, m_dt_bias, m_a_log, m_d_skip, m_conv_out_norm, m_ssm_out_norm, m_w_out, m_norm_mix_post, m_norm_mlp_pre, m_w_up, m_w_down, m_norm_mlp_post, v_norm_mix_pre, v_w_in, v_conv_a_w, v_ssm_conv_w, v_ssm_conv_b, v_dt_bias, v_a_log, v_d_skip, v_conv_out_norm, v_ssm_out_norm, v_w_out, v_norm_mix_post, v_norm_mlp_pre, v_w_up, v_w_down, v_norm_mlp_post):
    W = dict(norm_mix_pre=norm_mix_pre, w_in=w_in, conv_a_w=conv_a_w, ssm_conv_w=ssm_conv_w, ssm_conv_b=ssm_conv_b,
             dt_bias=dt_bias, a_log=a_log, d_skip=d_skip, conv_out_norm=conv_out_norm, ssm_out_norm=ssm_out_norm,
             w_out=w_out, norm_mix_post=norm_mix_post, norm_mlp_pre=norm_mlp_pre, w_up=w_up, w_down=w_down,
             norm_mlp_post=norm_mlp_post)
    M = dict(norm_mix_pre=m_norm_mix_pre, w_in=m_w_in, conv_a_w=m_conv_a_w, ssm_conv_w=m_ssm_conv_w,
             ssm_conv_b=m_ssm_conv_b, dt_bias=m_dt_bias, a_log=m_a_log, d_skip=m_d_skip,
             conv_out_norm=m_conv_out_norm, ssm_out_norm=m_ssm_out_norm, w_out=m_w_out,
             norm_mix_post=m_norm_mix_post, norm_mlp_pre=m_norm_mlp_pre, w_up=m_w_up, w_down=m_w_down,
             norm_mlp_post=m_norm_mlp_post)
    V = dict(norm_mix_pre=v_norm_mix_pre, w_in=v_w_in, conv_a_w=v_conv_a_w, ssm_conv_w=v_ssm_conv_w,
             ssm_conv_b=v_ssm_conv_b, dt_bias=v_dt_bias, a_log=v_a_log, d_skip=v_d_skip,
             conv_out_norm=v_conv_out_norm, ssm_out_norm=v_ssm_out_norm, w_out=v_w_out,
             norm_mix_post=v_norm_mix_post, norm_mlp_pre=v_norm_mlp_pre, w_up=v_w_up, w_down=v_w_down,
             norm_mlp_post=v_norm_mlp_post)

    nseq, seq, d = x.shape
    t = nseq * seq
    depth = w_in.shape[0]
    h = d // HEAD_DIM
    xbc = d + 2 * SSM_GROUPS * D_STATE
    in_cols = w_in.shape[2] * N_DEV
    d_mix = w_out.shape[1] * N_DEV
    d_ff = w_up.shape[2] * N_DEV
    me = 4 * lax.axis_index("x") + 2 * lax.axis_index("y") + lax.axis_index("c")
    ca_shard = conv_a_w.shape[2]
    sc_shard = ssm_conv_w.shape[2]

    tap_shapes = [conv_a_w.shape[1:], ssm_conv_w.shape[1:]]

    def gather_start(i):
        st_in, tok_in = _exchange_start([w_in[i].astype(BF16), _pack([conv_a_w[i], ssm_conv_w[i]])], gather=True,
                                        name=f"gather_start_in_{i}")
        st_rest, tok_rest = _exchange_start([W[n][i].astype(BF16) for n in ("w_out", "w_up", "w_down")], gather=True,
                                            name=f"gather_start_rest_{i}")
        return st_in, st_rest, [tok_in, tok_rest]

    emat = (lax.broadcasted_iota(jnp.int32, (h, d), 1) // HEAD_DIM == lax.broadcasted_iota(jnp.int32, (h, d), 0)).astype(F32)
    vec = lambda name, i: W[name][i].reshape(1, -1)

    xcur = x.reshape(t, d)
    hcur = _norm_fwd(xcur, vec("norm_mix_pre", 0), name="norm_first")
    saved = []
    nxt = gather_start(0)
    for i in range(depth):
        st_in, st_rest, tokens = nxt
        if i + 1 < depth:
            nxt = gather_start(i + 1)
            tokens = nxt[2]
        win_g, taps_g = _exchange_wait(st_in, [hcur] + tokens, name=f"gather_wait_in_{i}")
        win = jnp.transpose(win_g, (1, 0, 2)).reshape(d, in_cols)
        wdt = jnp.pad(win[:, 4 * d + xbc:], ((0, 0), (0, LANES - h)))
        taps_j = [_unpack(taps_g[j], tap_shapes) for j in range(N_DEV)]
        conv_a_i = jnp.concatenate([tj[0] for tj in taps_j], axis=1)
        ssm_conv_i = jnp.concatenate([tj[1] for tj in taps_j], axis=1)
        pa = _mm(hcur, win, n=3 * d, name=f"fwd_proj_a_{i}")
        pz = _mm(hcur, win, n=d, b_off=3 * d, name=f"fwd_proj_z_{i}")
        pxbc = _mm(hcur, win, n=xbc, b_off=4 * d, name=f"fwd_proj_xbc_{i}")
        pdt = _mm(hcur, wdt, name=f"fwd_proj_dt_{i}")
        ya, va = _conva_fwd(pa, conv_a_i, vec("conv_out_norm", i), seq=seq, name=f"fwd_conv_a_{i}")
        cpre = _convb_fwd(pxbc, ssm_conv_i, vec("ssm_conv_b", i), seq=seq, name=f"fwd_conv_b_{i}")
        dsk_lane = jnp.repeat(W["d_skip"][i], HEAD_DIM).reshape(1, d)
        cat, y2, hprev = _ssd_fwd(cpre, pdt, pz, ya, vec("dt_bias", i), vec("a_log", i), dsk_lane,
                                  vec("ssm_out_norm", i), emat, nseq=nseq, seq=seq, name=f"fwd_ssd_{i}")
        wout_g, wup_g, wdown_g = _exchange_wait(st_rest, [cat], name=f"gather_wait_rest_{i}")
        lw = dict(win=win, wdt=wdt, wout=wout_g.reshape(d_mix, d),
                  wup=jnp.transpose(wup_g, (1, 0, 2)).reshape(d, d_ff), wdown=wdown_g.reshape(d_ff, d),
                  conv_a=conv_a_i, ssm_conv=ssm_conv_i)
        mix = _mm(cat, lw["wout"], name=f"fwd_out_{i}")
        x1, h2 = _resid_norm(xcur, mix, vec("norm_mix_post", i), vec("norm_mlp_pre", i), name=f"fwd_post_mix_{i}")
        f = _mm(h2, lw["wup"], name=f"fwd_up_{i}", out_dtypes=(BF16,), epi=_epi_relu2)
        dn = _mm(f, lw["wdown"], name=f"fwd_down_{i}")
        g_next = vec("norm_mix_pre", i + 1) if i + 1 < depth else vec("norm_mix_pre", 0)
        x2, hnext = _resid_norm(x1, dn, vec("norm_mlp_post", i), g_next, name=f"fwd_post_mlp_{i}")
        saved.append(dict(lw=lw, x0=xcur, h=hcur, pa=pa, pz=pz, pxbc=pxbc, pdt=pdt, va=va, cpre=cpre, y2=y2,
                          hprev=hprev, cat=cat, mix=mix, x1=x1, h2=h2, f=f, dn=dn, dsk_lane=dsk_lane))
        xcur, hcur = x2, hnext

    dx, loss_part = _loss_fwd_bwd(xcur, loss_target.reshape(t, d), name="loss")
    loss = lax.psum(loss_part[0, 0], ("x", "y", "c"))

    small_grads = {n: [None] * depth for n in SMALL}
    big_out = {n: None for n in BIG}

    def finish(pending, after):
        li, st_a, st_b = pending

        def update(n, parts):
            big_out[n] = _sum_adamw(parts, W[n], M[n], V[n], layer=li, outs=big_out[n], name=f"adamw_{n}_{li}")

        p_down, p_up = _exchange_wait(st_a, after, name=f"scatter_wait_a_{li}")
        update("w_down", p_down)
        update("w_up", p_up)
        p_out, p_in = _exchange_wait(st_b, after + [big_out["w_up"][0]], name=f"scatter_wait_b_{li}")
        update("w_out", p_out)
        update("w_in", p_in)

    pending = None
    tok = jnp.zeros((1, 1), F32)
    for i in reversed(range(depth)):
        s = saved[i]
        lw = s["lw"]
        ddn, dg = _bwd_norm_out(s["dn"], vec("norm_mlp_post", i) + tok, dx, name=f"bwd_norm_mlp_post_{i}")
        small_grads["norm_mlp_post"][i] = dg
        dup = _mm(ddn, lw["wdown"], tb=True, name=f"bwd_down_dx_{i}", out_dtypes=(BF16,), epi=_epi_drelu2,
                  extras=(s["f"],))
        g_wdown = _mm(s["f"], ddn, ta=True, name=f"bwd_down_dw_{i}", out_dtypes=(BF16,))
        dh2 = _mm(dup, lw["wup"], tb=True, name=f"bwd_up_dx_{i}")
        g_wup = _mm(s["h2"], dup, ta=True, name=f"bwd_up_dw_{i}", out_dtypes=(BF16,))
        st_a, tok_a = _exchange_start(
            [g_wdown.reshape(N_DEV, d_ff // N_DEV, d), jnp.transpose(g_wup.reshape(d, N_DEV, d_ff // N_DEV), (1, 0, 2))],
            gather=False, name=f"scatter_start_a_{i}")
        dx1, dmix, dg_pre, dg_post = _bwd_norm_pair(s["x1"], [dh2], dx, s["mix"], vec("norm_mlp_pre", i) + tok_a[0:1, 0:1],
                                                    vec("norm_mix_post", i), name=f"bwd_norm_mix_post_{i}")
        small_grads["norm_mlp_pre"][i] = dg_pre
        small_grads["norm_mix_post"][i] = dg_post
        dcat = _mm(dmix, lw["wout"], tb=True, name=f"bwd_out_dx_{i}")
        g_wout = _mm(s["cat"], dmix, ta=True, name=f"bwd_out_dw_{i}", out_dtypes=(BF16,))
        dconv, dz, dpdt, dgs, ddsk, ddtb, dalog = _ssd_bwd(
            s["cpre"], s["pdt"], s["pz"], s["y2"], s["hprev"], dcat, vec("dt_bias", i), vec("a_log", i),
            s["dsk_lane"], vec("ssm_out_norm", i), emat, nseq=nseq, seq=seq, name=f"bwd_ssd_{i}")
        small_grads["ssm_out_norm"][i] = dgs
        small_grads["d_skip"][i] = ddsk
        small_grads["dt_bias"][i] = ddtb
        small_grads["a_log"][i] = dalog
        dpxbc, dscw, dscb = _convb_bwd(dconv, s["pxbc"], lw["ssm_conv"], seq=seq, name=f"bwd_conv_b_{i}")
        small_grads["ssm_conv_w"][i] = dscw
        small_grads["ssm_conv_b"][i] = dscb
        dpa, dcaw, dgca = _conva_bwd(dcat, s["pa"], s["va"], lw["conv_a"], vec("conv_out_norm", i), seq=seq,
                                     name=f"bwd_conv_a_{i}")
        small_grads["conv_a_w"][i] = dcaw
        small_grads["conv_out_norm"][i] = dgca
        dh_parts = [_mm(dpa, lw["win"], tb=True, name=f"bwd_proj_a_dx_{i}"),
                    _mm(dz, lw["win"], tb=True, b_koff=3 * d, name=f"bwd_proj_z_dx_{i}"),
                    _mm(dpxbc, lw["win"], tb=True, b_koff=4 * d, name=f"bwd_proj_xbc_dx_{i}"),
                    _mm(dpdt, lw["wdt"], tb=True, name=f"bwd_proj_dt_dx_{i}")]
        g_win = jnp.concatenate([
            _mm(s["h"], dpa, ta=True, name=f"bwd_proj_a_dw_{i}", out_dtypes=(BF16,)),
            _mm(s["h"], dz, ta=True, name=f"bwd_proj_z_dw_{i}", out_dtypes=(BF16,)),
            _mm(s["h"], dpxbc, ta=True, name=f"bwd_proj_xbc_dw_{i}", out_dtypes=(BF16,)),
            _mm(s["h"], dpdt, ta=True, name=f"bwd_proj_dt_dw_{i}", out_dtypes=(BF16,))[:, :h]], axis=1)
        dx, dg_in = _bwd_norm_in(s["x0"], dh_parts, dx1, vec("norm_mix_pre", i), name=f"bwd_norm_mix_pre_{i}")
        small_grads["norm_mix_pre"][i] = dg_in

        st_b, tok_b = _exchange_start(
            [g_wout.reshape(N_DEV, d_mix // N_DEV, d), jnp.transpose(g_win.reshape(d, N_DEV, in_cols // N_DEV), (1, 0, 2))],
            gather=False, name=f"scatter_start_b_{i}")
        if pending is not None:
            finish(pending, [dx, tok_b])
        pending = (i, st_a, st_b)
        tok = tok_b[0:1, 0:1]
    finish(pending, [dx])

    grad_x = dx.reshape(nseq, seq, d)

    small_shapes_full = {n: (depth,) + tuple(small_grads[n][0].shape) for n in SMALL}
    gpack = _pack([jnp.stack(small_grads[n]) for n in SMALL])
    gparts = _exchange(gpack, gather=True, name="allreduce_small")

    def shard_of(n, full):
        if n == "conv_a_w":
            return lax.dynamic_slice_in_dim(full, me * ca_shard, ca_shard, axis=2)
        if n == "ssm_conv_w":
            return lax.dynamic_slice_in_dim(full, me * sc_shard, sc_shard, axis=2)
        return full.reshape(W[n].shape)

    gsum = _sum_parts(gparts, name="sum_small")
    gfull = _unpack(gsum, [small_shapes_full[n] for n in SMALL])
    gsmall = {n: shard_of(n, gf) for n, gf in zip(SMALL, gfull)}
    res = _sum_adamw(_pack([gsmall[n] for n in SMALL])[None], _pack([W[n] for n in SMALL]),
                     _pack([M[n] for n in SMALL]), _pack([V[n] for n in SMALL]), name="adamw_small")
    small_out = [dict(zip(SMALL, _unpack(r, [W[n].shape for n in SMALL]))) for r in res]

    def out_of(kind, n):
        return big_out[n][kind] if n in BIG else small_out[kind][n]

    return (loss, grad_x, *[out_of(k, n) for k in range(4) for n in ORDER])
```

```python
import functools

import jax
import jax.numpy as jnp
from jax import lax
from jax.experimental import pallas as pl
from jax.experimental.pallas import tpu as pltpu

F32 = jnp.float32
BF16 = jnp.bfloat16
HIGHEST = lax.Precision.HIGHEST
MESH = pl.DeviceIdType.MESH

EPS = 1e-6
HEAD_DIM = 64
D_STATE = 128
SSM_GROUPS = 2
CHUNK = 128
CONV_K = 3
SSM_CONV_K = 4
ADAM_LR = 0.001
ADAM_B1 = 0.9
ADAM_B2 = 0.999
ADAM_EPS = 1e-08
ADAM_WD = 0.01
ADAM_STEP = 10

N_DEV = 8
LANES = 128
SUBLANES = 8
VMEM_LIMIT = 48 * 1024 * 1024
ROW_TILE = 512
MM_TILE = 1024


def _params(sem):
    return pltpu.CompilerParams(dimension_semantics=sem, vmem_limit_bytes=VMEM_LIMIT)


def _call(body, **kw):
    return pl.pallas_call(body, **kw)


def _pick(n, cap):
    best = None
    for t in range(LANES, min(n, cap) + 1, LANES):
        if n % t == 0:
            best = t
    return best or n


def _pick_rows(n, cap):
    best = None
    for t in range(SUBLANES, min(n, cap) + 1, SUBLANES):
        if n % t == 0:
            best = t
    return best or n


def _sigmoid(x):
    return 1.0 / (1.0 + jnp.exp(-x))


def _softplus(x):
    return jnp.maximum(x, 0.0) + jnp.log1p(jnp.exp(-jnp.abs(x)))


def _rms(x):
    return lax.rsqrt(jnp.mean(x * x, axis=-1, keepdims=True) + EPS)


def _rms_bwd(x, r, g, dy):
    gy = dy * g
    dx = r * gy - x * (r * r * r) * jnp.mean(gy * x, axis=-1, keepdims=True)
    return dx, dy * x * r


def _full(shape):
    return pl.BlockSpec(shape, lambda *_: (0,) * len(shape))


def _mm(a, b, *, name, ta=False, tb=False, out_dtypes=(F32,), epi=None, extras=(), n=None, b_off=0, b_koff=0,
        after=()):
    m, k = (a.shape[1], a.shape[0]) if ta else a.shape
    if n is None:
        n = b.shape[0] if tb else b.shape[1]
    tm, tn, tk = _pick(m, MM_TILE), _pick(n, MM_TILE), _pick(k, MM_TILE)
    while b_off % tn or n % tn:
        tn -= LANES
    while b_koff % tk or k % tk:
        tk -= LANES
    nk = k // tk
    nm, nn = m // tm, n // tn
    jo = b_off // tn
    ko = b_koff // tk
    a_bytes = m * k * a.dtype.itemsize
    b_bytes = n * k * b.dtype.itemsize
    m_outer = a_bytes + nm * b_bytes <= b_bytes + nn * a_bytes
    ij = (lambda g0, g1: (g0, g1)) if m_outer else (lambda g0, g1: (g1, g0))
    grid = (nm, nn, nk) if m_outer else (nn, nm, nk)

    def a_map(g0, g1, kk):
        i, _ = ij(g0, g1)
        return (kk, i) if ta else (i, kk)

    def b_map(g0, g1, kk):
        _, j = ij(g0, g1)
        return (j + jo, kk + ko) if tb else (kk + ko, j + jo)

    def o_map(g0, g1, kk):
        return ij(g0, g1)

    a_spec = pl.BlockSpec((tk, tm) if ta else (tm, tk), a_map)
    b_spec = pl.BlockSpec((tn, tk) if tb else (tk, tn), b_map)
    o_spec = pl.BlockSpec((tm, tn), o_map)
    dims = (((0 if ta else 1,), (1 if tb else 0,)), ((), ()))
    n_ex = len(extras)
    after = list(after)
    o0 = 2 + n_ex + len(after)

    def finish(acc, ex, outs):
        res = (acc,) if epi is None else epi(acc, *[e[...] for e in ex])
        for o, r in zip(outs, res):
            o[...] = r.astype(o.dtype)

    def body_single(*refs):
        a_ref, b_ref = refs[:2]
        acc = lax.dot_general(a_ref[...].astype(BF16), b_ref[...].astype(BF16), dims, preferred_element_type=F32)
        finish(acc, refs[2:2 + n_ex], refs[o0:])

    def body_multi(*refs):
        a_ref, b_ref = refs[:2]
        acc = refs[-1]
        kk = pl.program_id(2)

        @pl.when(kk == 0)
        def _():
            acc[...] = jnp.zeros_like(acc)

        acc[...] += lax.dot_general(a_ref[...].astype(BF16), b_ref[...].astype(BF16), dims, preferred_element_type=F32)

        @pl.when(kk == nk - 1)
        def _():
            finish(acc[...], refs[2:2 + n_ex], refs[o0:-1])

    outs = _call(
        body_single if nk == 1 else body_multi, name=name, grid=grid,
        in_specs=[a_spec, b_spec] + [o_spec] * n_ex + [pl.BlockSpec(memory_space=pl.ANY)] * len(after),
        out_specs=[o_spec] * len(out_dtypes),
        out_shape=[jax.ShapeDtypeStruct((m, n), dt) for dt in out_dtypes],
        scratch_shapes=[] if nk == 1 else [pltpu.VMEM((tm, tn), F32)],
        compiler_params=_params(("parallel", "parallel", "arbitrary")),
    )(a, b, *extras, *after)
    return outs[0] if len(out_dtypes) == 1 else outs


def _epi_relu2(acc):
    r = jnp.maximum(acc, 0.0)
    return (r * r,)


def _epi_drelu2(acc, f):
    return (acc * (2.0 * jnp.sqrt(f.astype(F32))),)


def _norm_fwd(x, g, *, name):
    t, d = x.shape
    tt = _pick_rows(t, ROW_TILE)

    def body(x_ref, g_ref, h_ref):
        xv = x_ref[...]
        h_ref[...] = (xv * _rms(xv) * g_ref[...]).astype(BF16)

    row = pl.BlockSpec((tt, d), lambda i: (i, 0))
    return _call(body, name=name, grid=(t // tt,), in_specs=[row, _full((1, d))], out_specs=row,
                 out_shape=jax.ShapeDtypeStruct((t, d), BF16), compiler_params=_params(("parallel",)))(x, g)


def _resid_norm(x, n, g1, g2, *, name):
    t, d = x.shape
    tt = _pick_rows(t, ROW_TILE)

    def body(x_ref, n_ref, g1_ref, g2_ref, xo_ref, h_ref):
        nv = n_ref[...]
        xn = x_ref[...] + nv * _rms(nv) * g1_ref[...]
        xo_ref[...] = xn
        h_ref[...] = (xn * _rms(xn) * g2_ref[...]).astype(BF16)

    row = pl.BlockSpec((tt, d), lambda i: (i, 0))
    return _call(body, name=name, grid=(t // tt,), in_specs=[row, row, _full((1, d)), _full((1, d))],
                 out_specs=[row, row],
                 out_shape=[jax.ShapeDtypeStruct((t, d), F32), jax.ShapeDtypeStruct((t, d), BF16)],
                 compiler_params=_params(("parallel",)))(x, n, g1, g2)


def _loss_fwd_bwd(xf, target, *, name):
    t, d = xf.shape
    tt = _pick_rows(t, ROW_TILE)
    nt = t // tt

    def body(x_ref, t_ref, dy_ref, loss_ref, acc):
        i = pl.program_id(0)

        @pl.when(i == 0)
        def _():
            acc[...] = jnp.zeros_like(acc)

        e = x_ref[...] - t_ref[...]
        dy_ref[...] = e * (1.0 / d)
        acc[...] += jnp.sum(e * e, axis=0, keepdims=True)

        @pl.when(i == nt - 1)
        def _():
            loss_ref[...] = jnp.sum(acc[...], axis=-1, keepdims=True) * (0.5 / d)

    row = pl.BlockSpec((tt, d), lambda i: (i, 0))
    return _call(body, name=name, grid=(nt,), in_specs=[row, row], out_specs=[row, _full((1, 1))],
                 out_shape=[jax.ShapeDtypeStruct((t, d), F32), jax.ShapeDtypeStruct((1, 1), F32)],
                 scratch_shapes=[pltpu.VMEM((1, d), F32)], compiler_params=_params(("arbitrary",)))(xf, target)


def _bwd_norm_pair(xin, dh, dres, n, g_in, g_out, *, name):
    t, d = xin.shape
    tt = _pick_rows(t, ROW_TILE)
    n_dh = len(dh)

    def body(*refs):
        x_ref = refs[0]
        dh_refs = refs[1:1 + n_dh]
        dres_ref, n_ref, gi_ref, go_ref, dx_ref, dn_ref, dgi_ref, dgo_ref = refs[1 + n_dh:]
        i = pl.program_id(0)

        @pl.when(i == 0)
        def _():
            dgi_ref[...] = jnp.zeros_like(dgi_ref)
            dgo_ref[...] = jnp.zeros_like(dgo_ref)

        xv = x_ref[...]
        dhv = dh_refs[0][...].astype(F32)
        for r in dh_refs[1:]:
            dhv = dhv + r[...].astype(F32)
        dxh, dgi = _rms_bwd(xv, _rms(xv), gi_ref[...], dhv)
        dx = dres_ref[...] + dxh
        dx_ref[...] = dx
        dgi_ref[...] += jnp.sum(dgi, axis=0, keepdims=True)
        nv = n_ref[...]
        dn, dgo = _rms_bwd(nv, _rms(nv), go_ref[...], dx)
        dn_ref[...] = dn.astype(BF16)
        dgo_ref[...] += jnp.sum(dgo, axis=0, keepdims=True)

    row = pl.BlockSpec((tt, d), lambda i: (i, 0))
    vec = _full((1, d))
    return _call(body, name=name, grid=(t // tt,), in_specs=[row] * (n_dh + 3) + [vec, vec],
                 out_specs=[row, row, vec, vec],
                 out_shape=[jax.ShapeDtypeStruct((t, d), F32), jax.ShapeDtypeStruct((t, d), BF16),
                            jax.ShapeDtypeStruct((1, d), F32), jax.ShapeDtypeStruct((1, d), F32)],
                 compiler_params=_params(("arbitrary",)))(xin, *dh, dres, n, g_in, g_out)


def _bwd_norm_in(xin, dh, dres, g_in, *, name):
    t, d = xin.shape
    tt = _pick_rows(t, ROW_TILE)
    n_dh = len(dh)

    def body(*refs):
        x_ref = refs[0]
        dh_refs = refs[1:1 + n_dh]
        dres_ref, gi_ref, dx_ref, dgi_ref = refs[1 + n_dh:]
        i = pl.program_id(0)

        @pl.when(i == 0)
        def _():
            dgi_ref[...] = jnp.zeros_like(dgi_ref)

        xv = x_ref[...]
        dhv = dh_refs[0][...].astype(F32)
        for r in dh_refs[1:]:
            dhv = dhv + r[...].astype(F32)
        dxh, dgi = _rms_bwd(xv, _rms(xv), gi_ref[...], dhv)
        dx_ref[...] = dres_ref[...] + dxh
        dgi_ref[...] += jnp.sum(dgi, axis=0, keepdims=True)

    row = pl.BlockSpec((tt, d), lambda i: (i, 0))
    vec = _full((1, d))
    return _call(body, name=name, grid=(t // tt,), in_specs=[row] * (n_dh + 2) + [vec],
                 out_specs=[row, vec],
                 out_shape=[jax.ShapeDtypeStruct((t, d), F32), jax.ShapeDtypeStruct((1, d), F32)],
                 compiler_params=_params(("arbitrary",)))(xin, *dh, dres, g_in)


def _bwd_norm_out(n, g_out, dx, *, name):
    t, d = n.shape
    tt = _pick_rows(t, ROW_TILE)

    def body(n_ref, go_ref, dx_ref, dn_ref, dgo_ref):
        i = pl.program_id(0)

        @pl.when(i == 0)
        def _():
            dgo_ref[...] = jnp.zeros_like(dgo_ref)

        nv = n_ref[...]
        dn, dgo = _rms_bwd(nv, _rms(nv), go_ref[...], dx_ref[...])
        dn_ref[...] = dn.astype(BF16)
        dgo_ref[...] += jnp.sum(dgo, axis=0, keepdims=True)

    row = pl.BlockSpec((tt, d), lambda i: (i, 0))
    vec = _full((1, d))
    return _call(body, name=name, grid=(t // tt,), in_specs=[row, vec, row], out_specs=[row, vec],
                 out_shape=[jax.ShapeDtypeStruct((t, d), BF16), jax.ShapeDtypeStruct((1, d), F32)],
                 compiler_params=_params(("arbitrary",)))(n, g_out, dx)


def _shift_down(cur, halo, s):
    return jnp.concatenate([halo[SUBLANES - s:], cur[:cur.shape[0] - s]], axis=0)


def _shift_up(cur, halo, s):
    return jnp.concatenate([cur[s:], halo[:s]], axis=0)


def _conva_fwd(pa, w, g, *, seq, name):
    t, d3 = pa.shape
    d = d3 // 3
    tt = _pick_rows(seq, ROW_TILE)
    tps = seq // tt

    def body(xa_ref, ca_ref, ba_ref, w_ref, g_ref, ya_ref, v_ref, carry):
        i = pl.program_id(0)

        @pl.when(i % tps == 0)
        def _():
            carry[...] = jnp.zeros_like(carry)

        u = ca_ref[...] * xa_ref[...]
        halo = carry[...]
        wv = w_ref[...]
        v = wv[2:3] * u + wv[1:2] * _shift_down(u, halo, 1) + wv[0:1] * _shift_down(u, halo, 2)
        carry[...] = u[tt - SUBLANES:]
        yp = ba_ref[...] * v
        ya_ref[...] = (yp * _rms(yp) * g_ref[...]).astype(BF16)
        v_ref[...] = v

    col = lambda c: pl.BlockSpec((tt, d), lambda i, c=c: (i, c))
    row = pl.BlockSpec((tt, d), lambda i: (i, 0))
    return _call(body, name=name, grid=(t // tt,),
                 in_specs=[col(0), col(1), col(2), _full((CONV_K, d)), _full((1, d))], out_specs=[row, row],
                 out_shape=[jax.ShapeDtypeStruct((t, d), BF16), jax.ShapeDtypeStruct((t, d), F32)],
                 scratch_shapes=[pltpu.VMEM((SUBLANES, d), F32)],
                 compiler_params=_params(("arbitrary",)))(pa, pa, pa, w, g)


def _conva_bwd(dcat, pa, v, w, g, *, seq, name):
    t, d3 = pa.shape
    d = d3 // 3
    tt = _pick_rows(seq, ROW_TILE)
    tps = seq // tt
    nt = t // tt

    def body(dya_ref, xa_ref, ca_ref, ba_ref, v_ref, w_ref, g_ref, dpa_ref, dw_ref, dg_ref, carry):
        i = pl.program_id(0)

        @pl.when(i == 0)
        def _():
            dw_ref[...] = jnp.zeros_like(dw_ref)
            dg_ref[...] = jnp.zeros_like(dg_ref)

        @pl.when(i % tps == 0)
        def _():
            carry[...] = jnp.zeros_like(carry)

        xa, ca, ba, vv = xa_ref[...], ca_ref[...], ba_ref[...], v_ref[...]
        yp = ba * vv
        dyp, dgt = _rms_bwd(yp, _rms(yp), g_ref[...], dya_ref[...])
        dg_ref[...] += jnp.sum(dgt, axis=0, keepdims=True)
        dv = dyp * ba
        halo = carry[...]
        dv1 = _shift_up(dv, halo, 1)
        dv2 = _shift_up(dv, halo, 2)
        carry[...] = dv[:SUBLANES]
        wv = w_ref[...]
        du = wv[2:3] * dv + wv[1:2] * dv1 + wv[0:1] * dv2
        u = ca * xa
        dw_ref[0:1, :] += jnp.sum(u * dv2, axis=0, keepdims=True)
        dw_ref[1:2, :] += jnp.sum(u * dv1, axis=0, keepdims=True)
        dw_ref[2:3, :] += jnp.sum(u * dv, axis=0, keepdims=True)
        dpa_ref[:, 0:d] = (du * ca).astype(BF16)
        dpa_ref[:, d:2 * d] = (du * xa).astype(BF16)
        dpa_ref[:, 2 * d:3 * d] = (dyp * vv).astype(BF16)

    rcol = lambda c: pl.BlockSpec((tt, d), lambda i, c=c: (nt - 1 - i, c))
    return _call(body, name=name, grid=(nt,),
                 in_specs=[rcol(0), rcol(0), rcol(1), rcol(2), rcol(0), _full((CONV_K, d)), _full((1, d))],
                 out_specs=[pl.BlockSpec((tt, d3), lambda i: (nt - 1 - i, 0)), _full((CONV_K, d)), _full((1, d))],
                 out_shape=[jax.ShapeDtypeStruct((t, d3), BF16), jax.ShapeDtypeStruct((CONV_K, d), F32),
                            jax.ShapeDtypeStruct((1, d), F32)],
                 scratch_shapes=[pltpu.VMEM((SUBLANES, d), F32)],
                 compiler_params=_params(("arbitrary",)))(dcat, pa, pa, pa, v, w, g)


def _convb_fwd(pxbc, w, bias, *, seq, name):
    t, c = pxbc.shape
    tt = _pick_rows(seq, ROW_TILE)
    tps = seq // tt

    def body(p_ref, w_ref, b_ref, o_ref, carry):
        i = pl.program_id(0)

        @pl.when(i % tps == 0)
        def _():
            carry[...] = jnp.zeros_like(carry)

        p = p_ref[...]
        halo = carry[...]
        wv = w_ref[...]
        o = wv[3:4] * p + b_ref[...]
        for s in (1, 2, 3):
            o = o + wv[3 - s:4 - s] * _shift_down(p, halo, s)
        carry[...] = p[tt - SUBLANES:]
        o_ref[...] = o

    row = pl.BlockSpec((tt, c), lambda i: (i, 0))
    return _call(body, name=name, grid=(t // tt,), in_specs=[row, _full((SSM_CONV_K, c)), _full((1, c))],
                 out_specs=row, out_shape=jax.ShapeDtypeStruct((t, c), F32),
                 scratch_shapes=[pltpu.VMEM((SUBLANES, c), F32)],
                 compiler_params=_params(("arbitrary",)))(pxbc, w, bias)


def _convb_bwd(dconv, pxbc, w, *, seq, name):
    t, c = pxbc.shape
    tt = _pick_rows(seq, ROW_TILE)
    tps = seq // tt
    nt = t // tt

    def body(dc_ref, p_ref, w_ref, dp_ref, dw_ref, db_ref, carry):
        i = pl.program_id(0)

        @pl.when(i == 0)
        def _():
            dw_ref[...] = jnp.zeros_like(dw_ref)
            db_ref[...] = jnp.zeros_like(db_ref)

        @pl.when(i % tps == 0)
        def _():
            carry[...] = jnp.zeros_like(carry)

        dc = dc_ref[...]
        p = p_ref[...]
        halo = carry[...]
        wv = w_ref[...]
        dp = wv[3:4] * dc
        dw_ref[3:4, :] += jnp.sum(p * dc, axis=0, keepdims=True)
        for s in (1, 2, 3):
            dcs = _shift_up(dc, halo, s)
            dp = dp + wv[3 - s:4 - s] * dcs
            dw_ref[3 - s:4 - s, :] += jnp.sum(p * dcs, axis=0, keepdims=True)
        carry[...] = dc[:SUBLANES]
        db_ref[...] += jnp.sum(dc, axis=0, keepdims=True)
        dp_ref[...] = dp.astype(BF16)

    rrow = pl.BlockSpec((tt, c), lambda i: (nt - 1 - i, 0))
    return _call(body, name=name, grid=(nt,), in_specs=[rrow, rrow, _full((SSM_CONV_K, c))],
                 out_specs=[rrow, _full((SSM_CONV_K, c)), _full((1, c))],
                 out_shape=[jax.ShapeDtypeStruct((t, c), BF16), jax.ShapeDtypeStruct((SSM_CONV_K, c), F32),
                            jax.ShapeDtypeStruct((1, c), F32)],
                 scratch_shapes=[pltpu.VMEM((SUBLANES, c), F32)],
                 compiler_params=_params(("arbitrary",)))(dconv, pxbc, w)


def _ssd_common(c_ref, pdt_ref, dtb_ref, alog_ref, h):
    cp = c_ref[...]
    sg = _sigmoid(cp)
    act = cp * sg
    pre = pdt_ref[:, 0:h] + dtb_ref[...]
    dt = _softplus(pre)
    a = -jnp.exp(alog_ref[...])
    adt = dt * a
    row = lax.broadcasted_iota(jnp.int32, (CHUNK, CHUNK), 0)
    col = lax.broadcasted_iota(jnp.int32, (CHUNK, CHUNK), 1)
    tril = row >= col
    cs = jnp.dot(tril.astype(F32), adt, precision=HIGHEST, preferred_element_type=F32)
    return dict(cp=cp, sg=sg, act=act, pre=pre, dt=dt, a=a, cs=cs, tril=tril, row=row, col=col, lo=col < HEAD_DIM)


def _col_bcast(x, hh):
    return jnp.broadcast_to(x[:, hh:hh + 1], (CHUNK, LANES))


def _pair_sums(v, lo):
    return (jnp.sum(jnp.where(lo, v, 0.0), axis=-1, keepdims=True),
            jnp.sum(jnp.where(lo, 0.0, v), axis=-1, keepdims=True))


def _dot_nt(a, b):
    return lax.dot_general(a, b, (((1,), (1,)), ((), ())), preferred_element_type=F32)


def _dot_tn(a, b):
    return lax.dot_general(a, b, (((0,), (0,)), ((), ())), preferred_element_type=F32)


def _dot(a, b):
    return jnp.dot(a, b, preferred_element_type=F32)


def _ssd_fwd(cpre, pdt, pz, ya, dtb, alog, dsk_lane, gs, *, nseq, seq, name):
    t, xbc = cpre.shape
    d = pz.shape[1]
    h = d // HEAD_DIM
    npair = h // 2
    ppg = npair // SSM_GROUPS
    nc = seq // CHUNK
    gw = d // SSM_GROUPS
    bc0 = d
    cc0 = d + SSM_GROUPS * D_STATE

    def body(c_ref, pdt_ref, z_ref, ya_ref, dtb_ref, alog_ref, dsk_ref, gs_ref, cat_ref, y2_ref, hp_ref, h_ref):
        c = pl.program_id(1)

        @pl.when(c == 0)
        def _():
            h_ref[...] = jnp.zeros_like(h_ref)

        q = _ssd_common(c_ref, pdt_ref, dtb_ref, alog_ref, h)
        act, cs, dt, lo = q["act"], q["cs"], q["dt"], q["lo"]
        xs = act[:, :d]
        ys = []
        for g in range(SSM_GROUPS):
            bg = act[:, bc0 + g * D_STATE: bc0 + (g + 1) * D_STATE]
            cgb = act[:, cc0 + g * D_STATE: cc0 + (g + 1) * D_STATE].astype(BF16)
            s = _dot_nt(cgb, bg.astype(BF16))
            for jj in range(ppg):
                j = g * ppg + jj
                sl = slice(LANES * j, LANES * (j + 1))
                xdb = (xs[:, sl] * jnp.where(lo, _col_bcast(dt, 2 * j), _col_bcast(dt, 2 * j + 1))).astype(BF16)
                hprev = h_ref[j]
                hp_ref[j] = hprev
                yd, st, csbs = [], [], []
                for hh in (2 * j, 2 * j + 1):
                    csb = _col_bcast(cs, hh)
                    csbs.append(csb)
                    lm = jnp.exp(jnp.where(q["tril"], csb - csb.T, -jnp.inf))
                    yd.append(_dot((s * lm).astype(BF16), xdb))
                    dte = jnp.exp(cs[CHUNK - 1:CHUNK, hh:hh + 1] - cs[:, hh:hh + 1])
                    st.append(_dot_tn((bg * dte).astype(BF16), xdb))
                ecs = jnp.exp(jnp.where(lo, csbs[0], csbs[1]))
                yoff = _dot(cgb, hprev.astype(BF16)) * ecs
                h_ref[j] = hprev * ecs[CHUNK - 1:CHUNK] + jnp.where(lo, st[0], st[1])
                ys.append(jnp.where(lo, yd[0], yd[1]) + yoff)
        y = jnp.concatenate(ys, axis=1) + dsk_ref[...] * xs
        y2_ref[...] = y
        zv = z_ref[...]
        y3 = y * (zv * _sigmoid(zv))
        cat_ref[:, 0:d] = ya_ref[...]
        for gi in range(SSM_GROUPS):
            seg = y3[:, gi * gw:(gi + 1) * gw]
            cat_ref[:, d + gi * gw:d + (gi + 1) * gw] = (seg * _rms(seg) * gs_ref[:, gi * gw:(gi + 1) * gw]).astype(BF16)

    chunk = lambda w: pl.BlockSpec((CHUNK, w), lambda b, c: (b * nc + c, 0))
    vec = lambda w: pl.BlockSpec((1, w), lambda b, c: (0, 0))
    hp_spec = pl.BlockSpec((None, None, npair, D_STATE, LANES), lambda b, c: (b, c, 0, 0, 0))
    return _call(body, name=name, grid=(nseq, nc),
                 in_specs=[chunk(xbc), chunk(LANES), chunk(d), chunk(d), vec(h), vec(h), vec(d), vec(d)],
                 out_specs=[chunk(2 * d), chunk(d), hp_spec],
                 out_shape=[jax.ShapeDtypeStruct((t, 2 * d), BF16), jax.ShapeDtypeStruct((t, d), F32),
                            jax.ShapeDtypeStruct((nseq, nc, npair, D_STATE, LANES), F32)],
                 scratch_shapes=[pltpu.VMEM((npair, D_STATE, LANES), F32)],
                 compiler_params=_params(("arbitrary", "arbitrary")))(cpre, pdt, pz, ya, dtb, alog, dsk_lane, gs)


def _ssd_bwd(cpre, pdt, pz, y2, hprev_all, dcat, dtb, alog, dsk_lane, gs, *, nseq, seq, name):
    t, xbc = cpre.shape
    d = pz.shape[1]
    h = d // HEAD_DIM
    npair = h // 2
    ppg = npair // SSM_GROUPS
    nc = seq // CHUNK
    gw = d // SSM_GROUPS
    bc0 = d
    cc0 = d + SSM_GROUPS * D_STATE

    def body(c_ref, pdt_ref, z_ref, y2_ref, hp_ref, dys_ref, dtb_ref, alog_ref, dsk_ref, gs_ref,
             dconv_ref, dz_ref, dpdt_ref, dgs_ref, ddsk_ref, ddtb_ref, dalog_ref, dh_ref):
        b = pl.program_id(0)
        c = pl.program_id(1)

        @pl.when(c == 0)
        def _():
            dh_ref[...] = jnp.zeros_like(dh_ref)

        @pl.when((b == 0) & (c == 0))
        def _():
            dgs_ref[...] = jnp.zeros_like(dgs_ref)
            ddsk_ref[...] = jnp.zeros_like(ddsk_ref)
            ddtb_ref[...] = jnp.zeros_like(ddtb_ref)
            dalog_ref[...] = jnp.zeros_like(dalog_ref)

        q = _ssd_common(c_ref, pdt_ref, dtb_ref, alog_ref, h)
        cp, sg, act, cs, a, dt, lo = q["cp"], q["sg"], q["act"], q["cs"], q["a"], q["dt"], q["lo"]
        xs = act[:, :d]
        row16 = lax.broadcasted_iota(jnp.int32, (CHUNK, h), 0)
        hid = lax.broadcasted_iota(jnp.int32, (1, h), 1)

        zv = z_ref[...]
        sz = _sigmoid(zv)
        siluz = zv * sz
        y2v = y2_ref[...]
        y3 = y2v * siluz
        dysv = dys_ref[...]
        dy3s = []
        for gi in range(SSM_GROUPS):
            gsl = slice(gi * gw, (gi + 1) * gw)
            seg = y3[:, gsl]
            dseg, dgt = _rms_bwd(seg, _rms(seg), gs_ref[:, gsl], dysv[:, gsl])
            dy3s.append(dseg)
            dgs_ref[:, gsl] += jnp.sum(dgt, axis=0, keepdims=True)
        dy3 = jnp.concatenate(dy3s, axis=1)
        dy = dy3 * siluz
        dz_ref[...] = (dy3 * y2v * (sz * (1.0 + zv * (1.0 - sz)))).astype(BF16)
        dsk_row = jnp.sum(dy * xs, axis=0, keepdims=True)

        dcs = jnp.zeros((CHUNK, h), F32)
        ddt = jnp.zeros((CHUNK, h), F32)
        ddsk = jnp.zeros((1, h), F32)
        dxs_parts, db_parts, dc_parts = [], [], []
        for g in range(SSM_GROUPS):
            bg = act[:, bc0 + g * D_STATE: bc0 + (g + 1) * D_STATE]
            cg = act[:, cc0 + g * D_STATE: cc0 + (g + 1) * D_STATE]
            bgb, cgb = bg.astype(BF16), cg.astype(BF16)
            s = _dot_nt(cgb, bgb)
            ds = jnp.zeros((CHUNK, CHUNK), F32)
            dbg = jnp.zeros((CHUNK, D_STATE), F32)
            dcg = jnp.zeros((CHUNK, D_STATE), F32)
            for jj in range(ppg):
                j = g * ppg + jj
                sl = slice(LANES * j, LANES * (j + 1))
                heads = (2 * j, 2 * j + 1)
                onehots = [(hid == hh).astype(F32) for hh in heads]
                csbs = [_col_bcast(cs, hh) for hh in heads]
                dt_pair = jnp.where(lo, _col_bcast(dt, heads[0]), _col_bcast(dt, heads[1]))
                xsj = xs[:, sl]
                xdj = xsj * dt_pair
                xdb = xdj.astype(BF16)
                dyj = dy[:, sl]
                dyb = dyj.astype(BF16)
                hprev = hp_ref[j]
                hpb = hprev.astype(BF16)
                dhn = dh_ref[j]
                dhb = dhn.astype(BF16)
                ecs = jnp.exp(jnp.where(lo, csbs[0], csbs[1]))
                gmat = (dyj * ecs).astype(BF16)
                yoff = _dot(cgb, hpb) * ecs
                dcg = dcg + _dot_nt(gmat, hpb)
                dh_ref[j] = dhn * ecs[CHUNK - 1:CHUNK] + _dot_tn(cgb, gmat)
                t2 = dhn * hprev
                dxd_h = []
                for idx, hh in enumerate(heads):
                    msk = lo if idx == 0 else jnp.logical_not(lo)
                    onehot = onehots[idx]
                    csc = cs[:, hh:hh + 1]
                    csb = csbs[idx]
                    lm = jnp.exp(jnp.where(q["tril"], csb - csb.T, -jnp.inf))
                    m = s * lm
                    mb = m.astype(BF16)
                    cs_last = cs[CHUNK - 1:CHUNK, hh:hh + 1]
                    dte = jnp.exp(cs_last - csc)
                    bwb = (bg * dte).astype(BF16)
                    dxd_s = _dot(bwb, dhb)
                    dbw = _dot_nt(jnp.where(msk, xdj, 0.0).astype(BF16), dhb)
                    dbg = dbg + dbw * dte
                    qv = jnp.sum(dbw * bg, axis=-1, keepdims=True) * dte
                    dm = _dot_nt(jnp.where(msk, dyj, 0.0).astype(BF16), xdb)
                    dxd_d = _dot_tn(mb, dyb)
                    wm = dm * m
                    rc = jnp.sum(wm - wm.T, axis=-1, keepdims=True)
                    ds = ds + dm * lm
                    ddec = jnp.sum(jnp.where(msk, t2, 0.0)) * jnp.exp(cs_last)
                    last = jnp.sum(qv) + ddec
                    dcs = dcs + (rc - qv) * onehot + jnp.where(row16 == CHUNK - 1, last * onehot, 0.0)
                    dxd_h.append(dxd_s + dxd_d)
                dxd = jnp.where(lo, dxd_h[0], dxd_h[1])
                off_a, off_b = _pair_sums(dyj * yoff, lo)
                dcs = dcs + off_a * onehots[0] + off_b * onehots[1]
                dt_a, dt_b = _pair_sums(dxd * xsj, lo)
                ddt = ddt + dt_a * onehots[0] + dt_b * onehots[1]
                sk_a, sk_b = _pair_sums(dsk_row[:, sl], lo[0:1])
                ddsk = ddsk + sk_a * onehots[0] + sk_b * onehots[1]
                dxs_parts.append(dxd * dt_pair + dyj * dsk_ref[:, sl])
            dsb = ds.astype(BF16)
            dc_parts.append(dcg + _dot(dsb, bgb))
            db_parts.append(dbg + _dot_tn(dsb, cgb))
        ddsk_ref[...] += ddsk
        triu = (q["col"] >= q["row"]).astype(F32)
        dadt = jnp.dot(triu, dcs, precision=HIGHEST, preferred_element_type=F32)
        ddt = ddt + dadt * a
        dalog_ref[...] += jnp.sum(dadt * dt, axis=0, keepdims=True) * a
        dpre = ddt * _sigmoid(q["pre"])
        ddtb_ref[...] += jnp.sum(dpre, axis=0, keepdims=True)
        dpdt_ref[...] = jnp.zeros_like(dpdt_ref)
        dpdt_ref[:, 0:h] = dpre.astype(BF16)
        dact = jnp.concatenate(dxs_parts + db_parts + dc_parts, axis=1)
        dconv_ref[...] = dact * (sg * (1.0 + cp * (1.0 - sg)))

    rchunk = lambda w, cb=0: pl.BlockSpec((CHUNK, w), lambda b, c, cb=cb: (b * nc + nc - 1 - c, cb))
    vec = lambda w: pl.BlockSpec((1, w), lambda b, c: (0, 0))
    hp_spec = pl.BlockSpec((None, None, npair, D_STATE, LANES), lambda b, c: (b, nc - 1 - c, 0, 0, 0))
    return _call(body, name=name, grid=(nseq, nc),
                 in_specs=[rchunk(xbc), rchunk(LANES), rchunk(d), rchunk(d), hp_spec, rchunk(d, 1),
                           vec(h), vec(h), vec(d), vec(d)],
                 out_specs=[rchunk(xbc), rchunk(d), rchunk(LANES), vec(d), vec(h), vec(h), vec(h)],
                 out_shape=[jax.ShapeDtypeStruct((t, xbc), F32), jax.ShapeDtypeStruct((t, d), BF16),
                            jax.ShapeDtypeStruct((t, LANES), BF16), jax.ShapeDtypeStruct((1, d), F32),
                            jax.ShapeDtypeStruct((1, h), F32), jax.ShapeDtypeStruct((1, h), F32),
                            jax.ShapeDtypeStruct((1, h), F32)],
                 scratch_shapes=[pltpu.VMEM((npair, D_STATE, LANES), F32)],
                 compiler_params=_params(("arbitrary", "arbitrary")))(
                     cpre, pdt, pz, y2, hprev_all, dcat, dtb, alog, dsk_lane, gs)


def _sum_adamw(parts, w, m, v, *, name, layer=None, outs=None):
    n, r, c = parts.shape
    tr = _pick_rows(r, 256)
    bc1 = 1.0 - ADAM_B1 ** ADAM_STEP
    bc2 = 1.0 - ADAM_B2 ** ADAM_STEP

    def body(p_ref, w_ref, m_ref, v_ref, *rest):
        g_ref, d_ref, mo_ref, vo_ref = rest[-4:]
        g = p_ref[0].astype(F32)
        for k in range(1, n):
            g = g + p_ref[k].astype(F32)
        mn = ADAM_B1 * m_ref[...] + (1.0 - ADAM_B1) * g
        vn = ADAM_B2 * v_ref[...] + (1.0 - ADAM_B2) * (g * g)
        g_ref[...] = g
        mo_ref[...] = mn
        vo_ref[...] = vn
        d_ref[...] = -ADAM_LR * ((mn / bc1) / (jnp.sqrt(vn / bc2) + ADAM_EPS) + ADAM_WD * w_ref[...])

    p_spec = pl.BlockSpec((n, tr, c), lambda i: (0, i, 0))
    if layer is None:
        blk = pl.BlockSpec((tr, c), lambda i: (i, 0))
        return _call(body, name=name, grid=(r // tr,), in_specs=[p_spec, blk, blk, blk], out_specs=[blk] * 4,
                     out_shape=[jax.ShapeDtypeStruct((r, c), F32)] * 4,
                     compiler_params=_params(("parallel",)))(parts, w, m, v)
    blk = pl.BlockSpec((None, tr, c), lambda i: (layer, i, 0))
    if outs is None:
        outs = [lax.empty(w.shape, F32) for _ in range(4)]
    return _call(body, name=name, grid=(r // tr,),
                 in_specs=[p_spec, blk, blk, blk] + [pl.BlockSpec(memory_space=pl.ANY)] * 4, out_specs=[blk] * 4,
                 out_shape=[jax.ShapeDtypeStruct(w.shape, F32)] * 4, input_output_aliases={4 + k: k for k in range(4)},
                 compiler_params=_params(("parallel",)))(parts, w, m, v, *outs)


def _sum_parts(parts, *, name):
    n, r, c = parts.shape
    tr = _pick_rows(r, 256)

    def body(p_ref, g_ref):
        g = p_ref[0].astype(F32)
        for k in range(1, n):
            g = g + p_ref[k].astype(F32)
        g_ref[...] = g

    return _call(body, name=name, grid=(r // tr,), in_specs=[pl.BlockSpec((n, tr, c), lambda i: (0, i, 0))],
                 out_specs=pl.BlockSpec((tr, c), lambda i: (i, 0)), out_shape=jax.ShapeDtypeStruct((r, c), F32),
                 compiler_params=_params(("parallel",)))(parts)


def _peers():
    x, y, c = lax.axis_index("x"), lax.axis_index("y"), lax.axis_index("c")
    me = 4 * x + 2 * y + c
    out = []
    for k in range(1, N_DEV):
        px = (1 - x) if (k >> 2) & 1 else x
        py = (1 - y) if (k >> 1) & 1 else y
        pc = (1 - c) if k & 1 else c
        out.append(((px, py, pc), 4 * px + 2 * py + pc))
    return me, out


def _exchange(src, *, gather, name):
    shape = src.shape if gather else src.shape[1:]

    def body(s_ref, o_ref, send_sems, recv_sems, local_sem):
        me, peers = _peers()
        mine = pltpu.make_async_copy(s_ref if gather else s_ref.at[me], o_ref.at[me], local_sem)
        mine.start()
        sends = []
        for k, (dev, pid) in enumerate(peers):
            cp = pltpu.make_async_remote_copy(
                src_ref=s_ref if gather else s_ref.at[pid], dst_ref=o_ref.at[me],
                send_sem=send_sems.at[k], recv_sem=recv_sems.at[k], device_id=dev, device_id_type=MESH)
            cp.start()
            sends.append(cp)
        for k, (dev, pid) in enumerate(peers):
            pltpu.make_async_remote_copy(
                src_ref=s_ref if gather else s_ref.at[pid], dst_ref=o_ref.at[pid],
                send_sem=send_sems.at[k], recv_sem=recv_sems.at[k], device_id=dev, device_id_type=MESH).wait_recv()
        for cp in sends:
            cp.wait_send()
        mine.wait()

    any_spec = pl.BlockSpec(memory_space=pl.ANY)
    return _call(body, name=name, in_specs=[any_spec], out_specs=any_spec,
                 out_shape=jax.ShapeDtypeStruct((N_DEV,) + tuple(shape), src.dtype),
                 scratch_shapes=[pltpu.SemaphoreType.DMA((N_DEV - 1,)), pltpu.SemaphoreType.DMA((N_DEV - 1,)),
                                 pltpu.SemaphoreType.DMA(())])(src)


_HBM = pl.BlockSpec(memory_space=pltpu.HBM)
_SEM = pl.BlockSpec(memory_space=pltpu.SEMAPHORE)
_EFFECT = pltpu.SideEffectType.DATAFLOW_SIDE_EFFECTING


def _split_copies(s_refs, l_refs, send_sems, recv_sems, gather):
    me, peers = _peers()
    local, remote = [], []
    for ti, (s_ref, l_ref) in enumerate(zip(s_refs, l_refs)):
        base = ti * N_DEV
        local.append(pltpu.make_async_copy(s_ref if gather else s_ref.at[me], l_ref.at[me], recv_sems.at[base + N_DEV - 1]))
        for k, (dev, pid) in enumerate(peers):
            remote.append((
                pltpu.make_async_remote_copy(
                    src_ref=s_ref if gather else s_ref.at[pid], dst_ref=l_ref.at[me],
                    send_sem=send_sems.at[base + k], recv_sem=recv_sems.at[base + k], device_id=dev, device_id_type=MESH),
                pltpu.make_async_remote_copy(
                    src_ref=s_ref if gather else s_ref.at[pid], dst_ref=l_ref.at[pid],
                    send_sem=send_sems.at[base + k], recv_sem=recv_sems.at[base + k], device_id=dev, device_id_type=MESH)))
    return local, remote


def _exchange_start(srcs, *, gather, name, after=()):
    n = len(srcs)
    after = list(after)
    srcs = [pltpu.with_memory_space_constraint(s, pltpu.HBM) for s in srcs]
    lands = [pltpu.with_memory_space_constraint(
        lax.empty((N_DEV,) + tuple(s.shape if gather else s.shape[1:]), s.dtype), pltpu.HBM) for s in srcs]

    def body(*refs):
        s_refs, l_refs = refs[:n], refs[n:2 * n]
        outs = refs[2 * n + len(after):]
        send_sems, recv_sems, token = outs[0], outs[1], outs[-1]
        local, remote = _split_copies(s_refs, l_refs, send_sems, recv_sems, gather)
        for cp in local:
            cp.start()
        for out_cp, _ in remote:
            out_cp.start()
        token[...] = jnp.zeros_like(token)

    outs = _call(
        body, name=name,
        out_shape=(pltpu.SemaphoreType.DMA((n * N_DEV,)), pltpu.SemaphoreType.DMA((n * N_DEV,)),
                   *[pltpu.HBM(s.shape, s.dtype) for s in srcs], *[pltpu.HBM(l.shape, l.dtype) for l in lands],
                   jax.ShapeDtypeStruct((SUBLANES, LANES), F32)),
        in_specs=[_HBM] * (2 * n) + [pl.BlockSpec(memory_space=pl.ANY)] * len(after),
        out_specs=(_SEM, _SEM, *[_HBM] * (2 * n), pl.BlockSpec(memory_space=pltpu.VMEM)),
        input_output_aliases={k: k + 2 for k in range(2 * n)},
        compiler_params=pltpu.CompilerParams(has_side_effects=_EFFECT),
    )(*srcs, *lands, *after)
    return dict(n=n, gather=gather, sems=outs[:2], srcs=outs[2:2 + n], lands=outs[2 + n:2 + 2 * n]), outs[-1]


def _exchange_wait(state, after, *, name):
    n, gather = state["n"], state["gather"]
    after = list(after)

    def body(*refs):
        s_refs, l_refs = refs[:n], refs[n:2 * n]
        send_sems, recv_sems = refs[2 * n], refs[2 * n + 1]
        local, remote = _split_copies(s_refs, l_refs, send_sems, recv_sems, gather)
        for out_cp, in_cp in remote:
            out_cp.wait_send()
            in_cp.wait_recv()
        for cp in local:
            cp.wait()

    outs = _call(
        body, name=name,
        out_shape=tuple(pltpu.HBM(a.shape, a.dtype) for a in (*state["srcs"], *state["lands"])),
        in_specs=[_HBM] * (2 * n) + [_SEM, _SEM] + [pl.BlockSpec(memory_space=pl.ANY)] * len(after),
        out_specs=tuple([_HBM] * (2 * n)),
        input_output_aliases={k: k for k in range(2 * n)},
        compiler_params=pltpu.CompilerParams(has_side_effects=_EFFECT),
    )(*state["srcs"], *state["lands"], *state["sems"], *after)
    return outs[n:]


def _pack(arrs):
    flat = jnp.concatenate([a.reshape(-1).astype(F32) for a in arrs])
    pad = (-flat.shape[0]) % (SUBLANES * LANES)
    return jnp.pad(flat, (0, pad)).reshape(-1, LANES)


def _unpack(packed, shapes):
    flat = packed.reshape(-1)
    out, off = [], 0
    for s in shapes:
        n = 1
        for v in s:
            n *= v
        out.append(flat[off:off + n].reshape(s))
        off += n
    return out


SMALL = ("norm_mix_pre", "ssm_conv_b", "dt_bias", "a_log", "d_skip", "conv_out_norm", "ssm_out_norm",
         "norm_mix_post", "norm_mlp_pre", "norm_mlp_post", "conv_a_w", "ssm_conv_w")
BIG = ("w_in", "w_out", "w_up", "w_down")
ORDER = ("norm_mix_pre", "w_in", "conv_a_w", "ssm_conv_w", "ssm_conv_b", "dt_bias", "a_log", "d_skip",
         "conv_out_norm", "ssm_out_norm", "w_out", "norm_mix_post", "norm_mlp_pre", "w_up", "w_down", "norm_mlp_post")


def kernel(x, norm_mix_pre, w_in, conv_a_w, ssm_conv_w, ssm_conv_b, dt_bias, a_log, d_skip, conv_out_norm, ssm_out_norm, w_out, norm_mix_post, norm_mlp_pre, w_up, w_down, norm_mlp_post, loss_target, m_norm_mix_pre, m_w_in, m_conv_a_w, m_ssm_conv_w, m_ssm_conv_b, m_dt_bias, m_a_log, m_d_skip, m_conv_out_norm, m_ssm_out_norm, m_w_out, m_norm_mix_post, m_norm_mlp_pre, m_w_up, m_w_down, m_norm_mlp_post, v_norm_mix_pre, v_w_in, v_conv_a_w, v_ssm_conv_w, v_ssm_conv_b, v_dt_bias, v_a_log, v_d_skip, v_conv_out_norm, v_ssm_out_norm, v_w_out, v_norm_mix_post, v_norm_mlp_pre, v_w_up, v_w_down, v_norm_mlp_post):
    W = dict(norm_mix_pre=norm_mix_pre, w_in=w_in, conv_a_w=conv_a_w, ssm_conv_w=ssm_conv_w, ssm_conv_b=ssm_conv_b,
             dt_bias=dt_bias, a_log=a_log, d_skip=d_skip, conv_out_norm=conv_out_norm, ssm_out_norm=ssm_out_norm,
             w_out=w_out, norm_mix_post=norm_mix_post, norm_mlp_pre=norm_mlp_pre, w_up=w_up, w_down=w_down,
             norm_mlp_post=norm_mlp_post)
    M = dict(norm_mix_pre=m_norm_mix_pre, w_in=m_w_in, conv_a_w=m_conv_a_w, ssm_conv_w=m_ssm_conv_w,
             ssm_conv_b=m_ssm_conv_b, dt_bias=m_dt_bias, a_log=m_a_log, d_skip=m_d_skip,
             conv_out_norm=m_conv_out_norm, ssm_out_norm=m_ssm_out_norm, w_out=m_w_out,
             norm_mix_post=m_norm_mix_post, norm_mlp_pre=m_norm_mlp_pre, w_up=m_w_up, w_down=m_w_down,
             norm_mlp_post=m_norm_mlp_post)
    V = dict(norm_mix_pre=v_norm_mix_pre, w_in=v_w_in, conv_a_w=v_conv_a_w, ssm_conv_w=v_ssm_conv_w,
             ssm_conv_b=v_ssm_conv_b, dt_bias=v_dt_bias, a_log=v_a_log, d_skip=v_d_skip,
             conv_out_norm=v_conv_out_norm, ssm_out_norm=v_ssm_out_norm, w_out=v_w_out,
             norm_mix_post=v_norm_mix_post, norm_mlp_pre=v_norm_mlp_pre, w_up=v_w_up, w_down=v_w_down,
             norm_mlp_post=v_norm_mlp_post)

    nseq, seq, d = x.shape
    t = nseq * seq
    depth = w_in.shape[0]
    h = d // HEAD_DIM
    xbc = d + 2 * SSM_GROUPS * D_STATE
    in_cols = w_in.shape[2] * N_DEV
    d_mix = w_out.shape[1] * N_DEV
    d_ff = w_up.shape[2] * N_DEV
    me = 4 * lax.axis_index("x") + 2 * lax.axis_index("y") + lax.axis_index("c")
    ca_shard = conv_a_w.shape[2]
    sc_shard = ssm_conv_w.shape[2]

    tap_shapes = [conv_a_w.shape[1:], ssm_conv_w.shape[1:]]

    def gather_start(i, after=()):
        st_in, tok_in = _exchange_start([w_in[i].astype(BF16), _pack([conv_a_w[i], ssm_conv_w[i]])], gather=True,
                                        name=f"gather_start_in_{i}", after=after)
        st_rest, tok_rest = _exchange_start([W[n][i].astype(BF16) for n in ("w_out", "w_up", "w_down")], gather=True,
                                            name=f"gather_start_rest_{i}", after=[tok_in])
        return st_in, st_rest, tok_rest

    vec = lambda name, i: W[name][i].reshape(1, -1)

    xcur = x.reshape(t, d)
    hcur = _norm_fwd(xcur, vec("norm_mix_pre", 0), name="norm_first")
    saved = []
    nxt = gather_start(0)
    for i in range(depth):
        st_in, st_rest, tok = nxt
        win_g, taps_g = _exchange_wait(st_in, [hcur, tok], name=f"gather_wait_in_{i}")
        win = jnp.transpose(win_g, (1, 0, 2)).reshape(d, in_cols)
        wdt = jnp.pad(win[:, 4 * d + xbc:], ((0, 0), (0, LANES - h)))
        taps_j = [_unpack(taps_g[j], tap_shapes) for j in range(N_DEV)]
        conv_a_i = jnp.concatenate([tj[0] for tj in taps_j], axis=1)
        ssm_conv_i = jnp.concatenate([tj[1] for tj in taps_j], axis=1)
        pa = _mm(hcur, win, n=3 * d, name=f"fwd_proj_a_{i}")
        pz = _mm(hcur, win, n=d, b_off=3 * d, name=f"fwd_proj_z_{i}")
        pxbc = _mm(hcur, win, n=xbc, b_off=4 * d, name=f"fwd_proj_xbc_{i}")
        pdt = _mm(hcur, wdt, name=f"fwd_proj_dt_{i}")
        ya, va = _conva_fwd(pa, conv_a_i, vec("conv_out_norm", i), seq=seq, name=f"fwd_conv_a_{i}")
        cpre = _convb_fwd(pxbc, ssm_conv_i, vec("ssm_conv_b", i), seq=seq, name=f"fwd_conv_b_{i}")
        dsk_lane = jnp.repeat(W["d_skip"][i], HEAD_DIM).reshape(1, d)
        cat, y2, hprev = _ssd_fwd(cpre, pdt, pz, ya, vec("dt_bias", i), vec("a_log", i), dsk_lane,
                                  vec("ssm_out_norm", i), nseq=nseq, seq=seq, name=f"fwd_ssd_{i}")
        wout_g, wup_g, wdown_g = _exchange_wait(st_rest, [cat], name=f"gather_wait_rest_{i}")
        lw = dict(win=win, wdt=wdt, wout=wout_g.reshape(d_mix, d),
                  wup=jnp.transpose(wup_g, (1, 0, 2)).reshape(d, d_ff), wdown=wdown_g.reshape(d_ff, d),
                  conv_a=conv_a_i, ssm_conv=ssm_conv_i)
        after = []
        if i + 1 < depth:
            nxt = gather_start(i + 1, after=[wout_g])
            after = [nxt[2]]
        mix = _mm(cat, lw["wout"], name=f"fwd_out_{i}", after=after)
        x1, h2 = _resid_norm(xcur, mix, vec("norm_mix_post", i), vec("norm_mlp_pre", i), name=f"fwd_post_mix_{i}")
        f = _mm(h2, lw["wup"], name=f"fwd_up_{i}", out_dtypes=(BF16,), epi=_epi_relu2)
        dn = _mm(f, lw["wdown"], name=f"fwd_down_{i}")
        g_next = vec("norm_mix_pre", i + 1) if i + 1 < depth else vec("norm_mix_pre", 0)
        x2, hnext = _resid_norm(x1, dn, vec("norm_mlp_post", i), g_next, name=f"fwd_post_mlp_{i}")
        saved.append(dict(lw=lw, x0=xcur, h=hcur, pa=pa, pz=pz, pxbc=pxbc, pdt=pdt, va=va, cpre=cpre, y2=y2,
                          hprev=hprev, cat=cat, mix=mix, x1=x1, h2=h2, f=f, dn=dn, dsk_lane=dsk_lane))
        xcur, hcur = x2, hnext

    dx, loss_part = _loss_fwd_bwd(xcur, loss_target.reshape(t, d), name="loss")
    loss = lax.psum(loss_part[0, 0], ("x", "y", "c"))

    small_grads = {n: [None] * depth for n in SMALL}
    big_out = {n: None for n in BIG}

    def finish(pending, after):
        li, st_a, st_b = pending

        def update(n, parts):
            big_out[n] = _sum_adamw(parts, W[n], M[n], V[n], layer=li, outs=big_out[n], name=f"adamw_{n}_{li}")

        p_down, p_up = _exchange_wait(st_a, after, name=f"scatter_wait_a_{li}")
        update("w_down", p_down)
        update("w_up", p_up)
        p_out, p_in = _exchange_wait(st_b, after + [big_out["w_up"][0]], name=f"scatter_wait_b_{li}")
        update("w_out", p_out)
        update("w_in", p_in)

    pending = None
    for i in reversed(range(depth)):
        s = saved[i]
        lw = s["lw"]
        ddn, dg = _bwd_norm_out(s["dn"], vec("norm_mlp_post", i), dx, name=f"bwd_norm_mlp_post_{i}")
        small_grads["norm_mlp_post"][i] = dg
        dup = _mm(ddn, lw["wdown"], tb=True, name=f"bwd_down_dx_{i}", out_dtypes=(BF16,), epi=_epi_drelu2,
                  extras=(s["f"],))
        g_wdown = _mm(s["f"], ddn, ta=True, name=f"bwd_down_dw_{i}", out_dtypes=(BF16,))
        dh2 = _mm(dup, lw["wup"], tb=True, name=f"bwd_up_dx_{i}")
        g_wup = _mm(s["h2"], dup, ta=True, name=f"bwd_up_dw_{i}", out_dtypes=(BF16,))
        st_a, tok_a = _exchange_start(
            [g_wdown.reshape(N_DEV, d_ff // N_DEV, d), jnp.transpose(g_wup.reshape(d, N_DEV, d_ff // N_DEV), (1, 0, 2))],
            gather=False, name=f"scatter_start_a_{i}")
        dx1, dmix, dg_pre, dg_post = _bwd_norm_pair(s["x1"], [dh2], dx, s["mix"], vec("norm_mlp_pre", i) + tok_a[0:1, 0:1],
                                                    vec("norm_mix_post", i), name=f"bwd_norm_mix_post_{i}")
        small_grads["norm_mlp_pre"][i] = dg_pre
        small_grads["norm_mix_post"][i] = dg_post
        dcat = _mm(dmix, lw["wout"], tb=True, name=f"bwd_out_dx_{i}")
        g_wout = _mm(s["cat"], dmix, ta=True, name=f"bwd_out_dw_{i}", out_dtypes=(BF16,))
        dconv, dz, dpdt, dgs, ddsk, ddtb, dalog = _ssd_bwd(
            s["cpre"], s["pdt"], s["pz"], s["y2"], s["hprev"], dcat, vec("dt_bias", i), vec("a_log", i),
            s["dsk_lane"], vec("ssm_out_norm", i), nseq=nseq, seq=seq, name=f"bwd_ssd_{i}")
        small_grads["ssm_out_norm"][i] = dgs
        small_grads["d_skip"][i] = ddsk
        small_grads["dt_bias"][i] = ddtb
        small_grads["a_log"][i] = dalog
        dpxbc, dscw, dscb = _convb_bwd(dconv, s["pxbc"], lw["ssm_conv"], seq=seq, name=f"bwd_conv_b_{i}")
        small_grads["ssm_conv_w"][i] = dscw
        small_grads["ssm_conv_b"][i] = dscb
        dpa, dcaw, dgca = _conva_bwd(dcat, s["pa"], s["va"], lw["conv_a"], vec("conv_out_norm", i), seq=seq,
                                     name=f"bwd_conv_a_{i}")
        small_grads["conv_a_w"][i] = dcaw
        small_grads["conv_out_norm"][i] = dgca
        g_win = jnp.concatenate([
            _mm(s["h"], dpa, ta=True, name=f"bwd_proj_a_dw_{i}", out_dtypes=(BF16,)),
            _mm(s["h"], dz, ta=True, name=f"bwd_proj_z_dw_{i}", out_dtypes=(BF16,)),
            _mm(s["h"], dpxbc, ta=True, name=f"bwd_proj_xbc_dw_{i}", out_dtypes=(BF16,)),
            _mm(s["h"], dpdt, ta=True, name=f"bwd_proj_dt_dw_{i}", out_dtypes=(BF16,))[:, :h]], axis=1)
        st_b, tok_b = _exchange_start(
            [g_wout.reshape(N_DEV, d_mix // N_DEV, d), jnp.transpose(g_win.reshape(d, N_DEV, in_cols // N_DEV), (1, 0, 2))],
            gather=False, name=f"scatter_start_b_{i}")
        dh_parts = [_mm(dpa, lw["win"], tb=True, name=f"bwd_proj_a_dx_{i}", after=[tok_b]),
                    _mm(dz, lw["win"], tb=True, b_koff=3 * d, name=f"bwd_proj_z_dx_{i}", after=[tok_b]),
                    _mm(dpxbc, lw["win"], tb=True, b_koff=4 * d, name=f"bwd_proj_xbc_dx_{i}", after=[tok_b]),
                    _mm(dpdt, lw["wdt"], tb=True, name=f"bwd_proj_dt_dx_{i}", after=[tok_b])]
        dx, dg_in = _bwd_norm_in(s["x0"], dh_parts, dx1, vec("norm_mix_pre", i), name=f"bwd_norm_mix_pre_{i}")
        small_grads["norm_mix_pre"][i] = dg_in
        if pending is not None:
            finish(pending, [dx])
        pending = (i, st_a, st_b)

    grad_x = dx.reshape(nseq, seq, d)

    small_shapes_full = {n: (depth,) + tuple(small_grads[n][0].shape) for n in SMALL}
    gpack = _pack([jnp.stack(small_grads[n]) for n in SMALL])
    gparts = _exchange(gpack, gather=True, name="allreduce_small")

    def shard_of(n, full):
        if n == "conv_a_w":
            return lax.dynamic_slice_in_dim(full, me * ca_shard, ca_shard, axis=2)
        if n == "ssm_conv_w":
            return lax.dynamic_slice_in_dim(full, me * sc_shard, sc_shard, axis=2)
        return full.reshape(W[n].shape)

    gsum = _sum_parts(gparts, name="sum_small")
    gfull = _unpack(gsum, [small_shapes_full[n] for n in SMALL])
    gsmall = {n: shard_of(n, gf) for n, gf in zip(SMALL, gfull)}
    res = _sum_adamw(_pack([gsmall[n] for n in SMALL])[None], _pack([W[n] for n in SMALL]),
                     _pack([M[n] for n in SMALL]), _pack([V[n] for n in SMALL]), name="adamw_small")
    small_out = [dict(zip(SMALL, _unpack(r, [W[n].shape for n in SMALL]))) for r in res]
    finish(pending, [dx, res[0]])

    def out_of(kind, n):
        return big_out[n][kind] if n in BIG else small_out[kind][n]

    return (loss, grad_x, *[out_of(k, n) for k in range(4) for n in ORDER])
```

```python
import functools

import jax
import jax.numpy as jnp
from jax import lax
from jax.experimental import pallas as pl
from jax.experimental.pallas import tpu as pltpu

F32 = jnp.float32
BF16 = jnp.bfloat16
HIGHEST = lax.Precision.HIGHEST
MESH = pl.DeviceIdType.MESH

EPS = 1e-6
HEAD_DIM = 64
D_STATE = 128
SSM_GROUPS = 2
CHUNK = 128
CONV_K = 3
SSM_CONV_K = 4
ADAM_LR = 0.001
ADAM_B1 = 0.9
ADAM_B2 = 0.999
ADAM_EPS = 1e-08
ADAM_WD = 0.01
ADAM_STEP = 10

N_DEV = 8
LANES = 128
SUBLANES = 8
VMEM_LIMIT = 48 * 1024 * 1024
ROW_TILE = 512
MM_TILE = 1024


def _params(sem):
    return pltpu.CompilerParams(dimension_semantics=sem, vmem_limit_bytes=VMEM_LIMIT)


def _call(body, **kw):
    return pl.pallas_call(body, **kw)


def _pick(n, cap):
    best = None
    for t in range(LANES, min(n, cap) + 1, LANES):
        if n % t == 0:
            best = t
    return best or n


def _pick_rows(n, cap):
    best = None
    for t in range(SUBLANES, min(n, cap) + 1, SUBLANES):
        if n % t == 0:
            best = t
    return best or n


def _sigmoid(x):
    return 1.0 / (1.0 + jnp.exp(-x))


def _softplus(x):
    return jnp.maximum(x, 0.0) + jnp.log1p(jnp.exp(-jnp.abs(x)))


def _rms(x):
    return lax.rsqrt(jnp.mean(x * x, axis=-1, keepdims=True) + EPS)


def _rms_bwd(x, r, g, dy):
    gy = dy * g
    dx = r * gy - x * (r * r * r) * jnp.mean(gy * x, axis=-1, keepdims=True)
    return dx, dy * x * r


def _full(shape):
    return pl.BlockSpec(shape, lambda *_: (0,) * len(shape))


def _mm(a, b, *, name, ta=False, tb=False, out_dtypes=(F32,), epi=None, extras=(), n=None, b_off=0, b_koff=0,
        after=()):
    m, k = (a.shape[1], a.shape[0]) if ta else a.shape
    if n is None:
        n = b.shape[0] if tb else b.shape[1]
    tm, tn, tk = _pick(m, MM_TILE), _pick(n, MM_TILE), _pick(k, MM_TILE)
    while b_off % tn or n % tn:
        tn -= LANES
    while b_koff % tk or k % tk:
        tk -= LANES
    nk = k // tk
    nm, nn = m // tm, n // tn
    jo = b_off // tn
    ko = b_koff // tk
    a_bytes = m * k * a.dtype.itemsize
    b_bytes = n * k * b.dtype.itemsize
    m_outer = a_bytes + nm * b_bytes <= b_bytes + nn * a_bytes
    ij = (lambda g0, g1: (g0, g1)) if m_outer else (lambda g0, g1: (g1, g0))
    grid = (nm, nn, nk) if m_outer else (nn, nm, nk)

    def a_map(g0, g1, kk):
        i, _ = ij(g0, g1)
        return (kk, i) if ta else (i, kk)

    def b_map(g0, g1, kk):
        _, j = ij(g0, g1)
        return (j + jo, kk + ko) if tb else (kk + ko, j + jo)

    def o_map(g0, g1, kk):
        return ij(g0, g1)

    a_spec = pl.BlockSpec((tk, tm) if ta else (tm, tk), a_map)
    b_spec = pl.BlockSpec((tn, tk) if tb else (tk, tn), b_map)
    o_spec = pl.BlockSpec((tm, tn), o_map)
    dims = (((0 if ta else 1,), (1 if tb else 0,)), ((), ()))
    n_ex = len(extras)
    after = list(after)
    o0 = 2 + n_ex + len(after)

    def finish(acc, ex, outs):
        res = (acc,) if epi is None else epi(acc, *[e[...] for e in ex])
        for o, r in zip(outs, res):
            o[...] = r.astype(o.dtype)

    def body_single(*refs):
        a_ref, b_ref = refs[:2]
        acc = lax.dot_general(a_ref[...].astype(BF16), b_ref[...].astype(BF16), dims, preferred_element_type=F32)
        finish(acc, refs[2:2 + n_ex], refs[o0:])

    def body_multi(*refs):
        a_ref, b_ref = refs[:2]
        acc = refs[-1]
        kk = pl.program_id(2)

        @pl.when(kk == 0)
        def _():
            acc[...] = jnp.zeros_like(acc)

        acc[...] += lax.dot_general(a_ref[...].astype(BF16), b_ref[...].astype(BF16), dims, preferred_element_type=F32)

        @pl.when(kk == nk - 1)
        def _():
            finish(acc[...], refs[2:2 + n_ex], refs[o0:-1])

    outs = _call(
        body_single if nk == 1 else body_multi, name=name, grid=grid,
        in_specs=[a_spec, b_spec] + [o_spec] * n_ex + [pl.BlockSpec(memory_space=pl.ANY)] * len(after),
        out_specs=[o_spec] * len(out_dtypes),
        out_shape=[jax.ShapeDtypeStruct((m, n), dt) for dt in out_dtypes],
        scratch_shapes=[] if nk == 1 else [pltpu.VMEM((tm, tn), F32)],
        compiler_params=_params(("parallel", "parallel", "arbitrary")),
    )(a, b, *extras, *after)
    return outs[0] if len(out_dtypes) == 1 else outs


def _epi_relu2(acc):
    r = jnp.maximum(acc, 0.0)
    return (r * r,)


def _epi_drelu2(acc, f):
    return (acc * (2.0 * jnp.sqrt(f.astype(F32))),)


def _norm_fwd(x, g, *, name):
    t, d = x.shape
    tt = _pick_rows(t, ROW_TILE)

    def body(x_ref, g_ref, h_ref):
        xv = x_ref[...]
        h_ref[...] = (xv * _rms(xv) * g_ref[...]).astype(BF16)

    row = pl.BlockSpec((tt, d), lambda i: (i, 0))
    return _call(body, name=name, grid=(t // tt,), in_specs=[row, _full((1, d))], out_specs=row,
                 out_shape=jax.ShapeDtypeStruct((t, d), BF16), compiler_params=_params(("parallel",)))(x, g)


def _resid_norm(x, n, g1, g2, *, name):
    t, d = x.shape
    tt = _pick_rows(t, ROW_TILE)

    def body(x_ref, n_ref, g1_ref, g2_ref, xo_ref, h_ref):
        nv = n_ref[...]
        xn = x_ref[...] + nv * _rms(nv) * g1_ref[...]
        xo_ref[...] = xn
        h_ref[...] = (xn * _rms(xn) * g2_ref[...]).astype(BF16)

    row = pl.BlockSpec((tt, d), lambda i: (i, 0))
    return _call(body, name=name, grid=(t // tt,), in_specs=[row, row, _full((1, d)), _full((1, d))],
                 out_specs=[row, row],
                 out_shape=[jax.ShapeDtypeStruct((t, d), F32), jax.ShapeDtypeStruct((t, d), BF16)],
                 compiler_params=_params(("parallel",)))(x, n, g1, g2)


def _loss_fwd_bwd(xf, target, *, name):
    t, d = xf.shape
    tt = _pick_rows(t, ROW_TILE)
    nt = t // tt

    def body(x_ref, t_ref, dy_ref, loss_ref, acc):
        i = pl.program_id(0)

        @pl.when(i == 0)
        def _():
            acc[...] = jnp.zeros_like(acc)

        e = x_ref[...] - t_ref[...]
        dy_ref[...] = e * (1.0 / d)
        acc[...] += jnp.sum(e * e, axis=0, keepdims=True)

        @pl.when(i == nt - 1)
        def _():
            loss_ref[...] = jnp.sum(acc[...], axis=-1, keepdims=True) * (0.5 / d)

    row = pl.BlockSpec((tt, d), lambda i: (i, 0))
    return _call(body, name=name, grid=(nt,), in_specs=[row, row], out_specs=[row, _full((1, 1))],
                 out_shape=[jax.ShapeDtypeStruct((t, d), F32), jax.ShapeDtypeStruct((1, 1), F32)],
                 scratch_shapes=[pltpu.VMEM((1, d), F32)], compiler_params=_params(("arbitrary",)))(xf, target)


def _bwd_norm_pair(xin, dh, dres, n, g_in, g_out, *, name):
    t, d = xin.shape
    tt = _pick_rows(t, ROW_TILE)
    n_dh = len(dh)

    def body(*refs):
        x_ref = refs[0]
        dh_refs = refs[1:1 + n_dh]
        dres_ref, n_ref, gi_ref, go_ref, dx_ref, dn_ref, dgi_ref, dgo_ref = refs[1 + n_dh:]
        i = pl.program_id(0)

        @pl.when(i == 0)
        def _():
            dgi_ref[...] = jnp.zeros_like(dgi_ref)
            dgo_ref[...] = jnp.zeros_like(dgo_ref)

        xv = x_ref[...]
        dhv = dh_refs[0][...].astype(F32)
        for r in dh_refs[1:]:
            dhv = dhv + r[...].astype(F32)
        dxh, dgi = _rms_bwd(xv, _rms(xv), gi_ref[...], dhv)
        dx = dres_ref[...] + dxh
        dx_ref[...] = dx
        dgi_ref[...] += jnp.sum(dgi, axis=0, keepdims=True)
        nv = n_ref[...]
        dn, dgo = _rms_bwd(nv, _rms(nv), go_ref[...], dx)
        dn_ref[...] = dn.astype(BF16)
        dgo_ref[...] += jnp.sum(dgo, axis=0, keepdims=True)

    row = pl.BlockSpec((tt, d), lambda i: (i, 0))
    vec = _full((1, d))
    return _call(body, name=name, grid=(t // tt,), in_specs=[row] * (n_dh + 3) + [vec, vec],
                 out_specs=[row, row, vec, vec],
                 out_shape=[jax.ShapeDtypeStruct((t, d), F32), jax.ShapeDtypeStruct((t, d), BF16),
                            jax.ShapeDtypeStruct((1, d), F32), jax.ShapeDtypeStruct((1, d), F32)],
                 compiler_params=_params(("arbitrary",)))(xin, *dh, dres, n, g_in, g_out)


def _bwd_norm_in(xin, dh, dres, g_in, *, name):
    t, d = xin.shape
    tt = _pick_rows(t, ROW_TILE)
    n_dh = len(dh)

    def body(*refs):
        x_ref = refs[0]
        dh_refs = refs[1:1 + n_dh]
        dres_ref, gi_ref, dx_ref, dgi_ref = refs[1 + n_dh:]
        i = pl.program_id(0)

        @pl.when(i == 0)
        def _():
            dgi_ref[...] = jnp.zeros_like(dgi_ref)

        xv = x_ref[...]
        dhv = dh_refs[0][...].astype(F32)
        for r in dh_refs[1:]:
            dhv = dhv + r[...].astype(F32)
        dxh, dgi = _rms_bwd(xv, _rms(xv), gi_ref[...], dhv)
        dx_ref[...] = dres_ref[...] + dxh
        dgi_ref[...] += jnp.sum(dgi, axis=0, keepdims=True)

    row = pl.BlockSpec((tt, d), lambda i: (i, 0))
    vec = _full((1, d))
    return _call(body, name=name, grid=(t // tt,), in_specs=[row] * (n_dh + 2) + [vec],
                 out_specs=[row, vec],
                 out_shape=[jax.ShapeDtypeStruct((t, d), F32), jax.ShapeDtypeStruct((1, d), F32)],
                 compiler_params=_params(("arbitrary",)))(xin, *dh, dres, g_in)


def _bwd_norm_out(n, g_out, dx, *, name):
    t, d = n.shape
    tt = _pick_rows(t, ROW_TILE)

    def body(n_ref, go_ref, dx_ref, dn_ref, dgo_ref):
        i = pl.program_id(0)

        @pl.when(i == 0)
        def _():
            dgo_ref[...] = jnp.zeros_like(dgo_ref)

        nv = n_ref[...]
        dn, dgo = _rms_bwd(nv, _rms(nv), go_ref[...], dx_ref[...])
        dn_ref[...] = dn.astype(BF16)
        dgo_ref[...] += jnp.sum(dgo, axis=0, keepdims=True)

    row = pl.BlockSpec((tt, d), lambda i: (i, 0))
    vec = _full((1, d))
    return _call(body, name=name, grid=(t // tt,), in_specs=[row, vec, row], out_specs=[row, vec],
                 out_shape=[jax.ShapeDtypeStruct((t, d), BF16), jax.ShapeDtypeStruct((1, d), F32)],
                 compiler_params=_params(("arbitrary",)))(n, g_out, dx)


def _shift_down(cur, halo, s):
    return jnp.concatenate([halo[SUBLANES - s:], cur[:cur.shape[0] - s]], axis=0)


def _shift_up(cur, halo, s):
    return jnp.concatenate([cur[s:], halo[:s]], axis=0)


def _conva_fwd(pa, w, g, *, seq, name):
    t, d3 = pa.shape
    d = d3 // 3
    tt = _pick_rows(seq, ROW_TILE)
    tps = seq // tt

    def body(xa_ref, ca_ref, ba_ref, w_ref, g_ref, ya_ref, v_ref, carry):
        i = pl.program_id(0)

        @pl.when(i % tps == 0)
        def _():
            carry[...] = jnp.zeros_like(carry)

        u = ca_ref[...] * xa_ref[...]
        halo = carry[...]
        wv = w_ref[...]
        v = wv[2:3] * u + wv[1:2] * _shift_down(u, halo, 1) + wv[0:1] * _shift_down(u, halo, 2)
        carry[...] = u[tt - SUBLANES:]
        yp = ba_ref[...] * v
        ya_ref[...] = (yp * _rms(yp) * g_ref[...]).astype(BF16)
        v_ref[...] = v

    col = lambda c: pl.BlockSpec((tt, d), lambda i, c=c: (i, c))
    row = pl.BlockSpec((tt, d), lambda i: (i, 0))
    return _call(body, name=name, grid=(t // tt,),
                 in_specs=[col(0), col(1), col(2), _full((CONV_K, d)), _full((1, d))], out_specs=[row, row],
                 out_shape=[jax.ShapeDtypeStruct((t, d), BF16), jax.ShapeDtypeStruct((t, d), F32)],
                 scratch_shapes=[pltpu.VMEM((SUBLANES, d), F32)],
                 compiler_params=_params(("arbitrary",)))(pa, pa, pa, w, g)


def _conva_bwd(dcat, pa, v, w, g, *, seq, name):
    t, d3 = pa.shape
    d = d3 // 3
    tt = _pick_rows(seq, ROW_TILE)
    tps = seq // tt
    nt = t // tt

    def body(dya_ref, xa_ref, ca_ref, ba_ref, v_ref, w_ref, g_ref, dpa_ref, dw_ref, dg_ref, carry):
        i = pl.program_id(0)

        @pl.when(i == 0)
        def _():
            dw_ref[...] = jnp.zeros_like(dw_ref)
            dg_ref[...] = jnp.zeros_like(dg_ref)

        @pl.when(i % tps == 0)
        def _():
            carry[...] = jnp.zeros_like(carry)

        xa, ca, ba, vv = xa_ref[...], ca_ref[...], ba_ref[...], v_ref[...]
        yp = ba * vv
        dyp, dgt = _rms_bwd(yp, _rms(yp), g_ref[...], dya_ref[...])
        dg_ref[...] += jnp.sum(dgt, axis=0, keepdims=True)
        dv = dyp * ba
        halo = carry[...]
        dv1 = _shift_up(dv, halo, 1)
        dv2 = _shift_up(dv, halo, 2)
        carry[...] = dv[:SUBLANES]
        wv = w_ref[...]
        du = wv[2:3] * dv + wv[1:2] * dv1 + wv[0:1] * dv2
        u = ca * xa
        dw_ref[0:1, :] += jnp.sum(u * dv2, axis=0, keepdims=True)
        dw_ref[1:2, :] += jnp.sum(u * dv1, axis=0, keepdims=True)
        dw_ref[2:3, :] += jnp.sum(u * dv, axis=0, keepdims=True)
        dpa_ref[:, 0:d] = (du * ca).astype(BF16)
        dpa_ref[:, d:2 * d] = (du * xa).astype(BF16)
        dpa_ref[:, 2 * d:3 * d] = (dyp * vv).astype(BF16)

    rcol = lambda c: pl.BlockSpec((tt, d), lambda i, c=c: (nt - 1 - i, c))
    return _call(body, name=name, grid=(nt,),
                 in_specs=[rcol(0), rcol(0), rcol(1), rcol(2), rcol(0), _full((CONV_K, d)), _full((1, d))],
                 out_specs=[pl.BlockSpec((tt, d3), lambda i: (nt - 1 - i, 0)), _full((CONV_K, d)), _full((1, d))],
                 out_shape=[jax.ShapeDtypeStruct((t, d3), BF16), jax.ShapeDtypeStruct((CONV_K, d), F32),
                            jax.ShapeDtypeStruct((1, d), F32)],
                 scratch_shapes=[pltpu.VMEM((SUBLANES, d), F32)],
                 compiler_params=_params(("arbitrary",)))(dcat, pa, pa, pa, v, w, g)


def _convb_fwd(pxbc, w, bias, *, seq, name):
    t, c = pxbc.shape
    tt = _pick_rows(seq, ROW_TILE)
    tps = seq // tt

    def body(p_ref, w_ref, b_ref, o_ref, carry):
        i = pl.program_id(0)

        @pl.when(i % tps == 0)
        def _():
            carry[...] = jnp.zeros_like(carry)

        p = p_ref[...]
        halo = carry[...]
        wv = w_ref[...]
        o = wv[3:4] * p + b_ref[...]
        for s in (1, 2, 3):
            o = o + wv[3 - s:4 - s] * _shift_down(p, halo, s)
        carry[...] = p[tt - SUBLANES:]
        o_ref[...] = o

    row = pl.BlockSpec((tt, c), lambda i: (i, 0))
    return _call(body, name=name, grid=(t // tt,), in_specs=[row, _full((SSM_CONV_K, c)), _full((1, c))],
                 out_specs=row, out_shape=jax.ShapeDtypeStruct((t, c), F32),
                 scratch_shapes=[pltpu.VMEM((SUBLANES, c), F32)],
                 compiler_params=_params(("arbitrary",)))(pxbc, w, bias)


def _convb_bwd(dconv, pxbc, w, *, seq, name):
    t, c = pxbc.shape
    tt = _pick_rows(seq, ROW_TILE)
    tps = seq // tt
    nt = t // tt

    def body(dc_ref, p_ref, w_ref, dp_ref, dw_ref, db_ref, carry):
        i = pl.program_id(0)

        @pl.when(i == 0)
        def _():
            dw_ref[...] = jnp.zeros_like(dw_ref)
            db_ref[...] = jnp.zeros_like(db_ref)

        @pl.when(i % tps == 0)
        def _():
            carry[...] = jnp.zeros_like(carry)

        dc = dc_ref[...]
        p = p_ref[...]
        halo = carry[...]
        wv = w_ref[...]
        dp = wv[3:4] * dc
        dw_ref[3:4, :] += jnp.sum(p * dc, axis=0, keepdims=True)
        for s in (1, 2, 3):
            dcs = _shift_up(dc, halo, s)
            dp = dp + wv[3 - s:4 - s] * dcs
            dw_ref[3 - s:4 - s, :] += jnp.sum(p * dcs, axis=0, keepdims=True)
        carry[...] = dc[:SUBLANES]
        db_ref[...] += jnp.sum(dc, axis=0, keepdims=True)
        dp_ref[...] = dp.astype(BF16)

    rrow = pl.BlockSpec((tt, c), lambda i: (nt - 1 - i, 0))
    return _call(body, name=name, grid=(nt,), in_specs=[rrow, rrow, _full((SSM_CONV_K, c))],
                 out_specs=[rrow, _full((SSM_CONV_K, c)), _full((1, c))],
                 out_shape=[jax.ShapeDtypeStruct((t, c), BF16), jax.ShapeDtypeStruct((SSM_CONV_K, c), F32),
                            jax.ShapeDtypeStruct((1, c), F32)],
                 scratch_shapes=[pltpu.VMEM((SUBLANES, c), F32)],
                 compiler_params=_params(("arbitrary",)))(dconv, pxbc, w)


def _expand_heads(x, ev):
    return jnp.dot(x, ev, precision=HIGHEST, preferred_element_type=F32)


def _head_sums(v, ev):
    return lax.dot_general(v, ev, (((1,), (1,)), ((), ())), precision=HIGHEST, preferred_element_type=F32)


def _ssd_common(c_ref, pdt_ref, dtb_ref, alog_ref, e_ref, h):
    cp = c_ref[...]
    sg = _sigmoid(cp)
    act = cp * sg
    pre = pdt_ref[:, 0:h] + dtb_ref[...]
    dt = _softplus(pre)
    a = -jnp.exp(alog_ref[...])
    adt = dt * a
    row = lax.broadcasted_iota(jnp.int32, (CHUNK, CHUNK), 0)
    col = lax.broadcasted_iota(jnp.int32, (CHUNK, CHUNK), 1)
    tril = row >= col
    cs = jnp.dot(tril.astype(F32), adt, precision=HIGHEST, preferred_element_type=F32)
    ev = e_ref[...]
    dt_l = _expand_heads(dt, ev)
    ecs_l = jnp.exp(_expand_heads(cs, ev))
    return dict(cp=cp, sg=sg, act=act, pre=pre, dt=dt, a=a, cs=cs, dt_l=dt_l, ecs_l=ecs_l,
                tril=tril, row=row, col=col, lo=col < HEAD_DIM)


def _dot_nt(a, b):
    return lax.dot_general(a, b, (((1,), (1,)), ((), ())), preferred_element_type=F32)


def _dot_tn(a, b):
    return lax.dot_general(a, b, (((0,), (0,)), ((), ())), preferred_element_type=F32)


def _dot(a, b):
    return jnp.dot(a, b, preferred_element_type=F32)


def _ssd_fwd(cpre, pdt, pz, ya, dtb, alog, dsk_lane, gs, emat, *, nseq, seq, name):
    t, xbc = cpre.shape
    d = pz.shape[1]
    h = d // HEAD_DIM
    npair = h // 2
    ppg = npair // SSM_GROUPS
    nc = seq // CHUNK
    gw = d // SSM_GROUPS
    bc0 = d
    cc0 = d + SSM_GROUPS * D_STATE

    def body(c_ref, pdt_ref, z_ref, ya_ref, dtb_ref, alog_ref, dsk_ref, gs_ref, e_ref, cat_ref, y2_ref, hp_ref, h_ref):
        c = pl.program_id(1)

        @pl.when(c == 0)
        def _():
            h_ref[...] = jnp.zeros_like(h_ref)

        q = _ssd_common(c_ref, pdt_ref, dtb_ref, alog_ref, e_ref, h)
        act, cs, lo, ecs_l = q["act"], q["cs"], q["lo"], q["ecs_l"]
        xs = act[:, :d]
        xd = xs * q["dt_l"]
        ys = []
        for g in range(SSM_GROUPS):
            bg = act[:, bc0 + g * D_STATE: bc0 + (g + 1) * D_STATE]
            cgb = act[:, cc0 + g * D_STATE: cc0 + (g + 1) * D_STATE].astype(BF16)
            s = _dot_nt(cgb, bg.astype(BF16))
            for jj in range(ppg):
                j = g * ppg + jj
                sl = slice(LANES * j, LANES * (j + 1))
                xdb = xd[:, sl].astype(BF16)
                hprev = h_ref[j]
                hp_ref[j] = hprev
                yd, st = [], []
                for hh in (2 * j, 2 * j + 1):
                    csc = cs[:, hh:hh + 1]
                    csb = jnp.broadcast_to(csc, (CHUNK, CHUNK))
                    lm = jnp.exp(jnp.where(q["tril"], csb - csb.T, -jnp.inf))
                    yd.append(_dot((s * lm).astype(BF16), xdb))
                    dte = jnp.exp(cs[CHUNK - 1:CHUNK, hh:hh + 1] - csc)
                    st.append(_dot_tn((bg * dte).astype(BF16), xdb))
                ecs = ecs_l[:, sl]
                yoff = _dot(cgb, hprev.astype(BF16)) * ecs
                h_ref[j] = hprev * ecs[CHUNK - 1:CHUNK] + jnp.where(lo, st[0], st[1])
                ys.append(jnp.where(lo, yd[0], yd[1]) + yoff)
        y = jnp.concatenate(ys, axis=1) + dsk_ref[...] * xs
        y2_ref[...] = y
        zv = z_ref[...]
        y3 = y * (zv * _sigmoid(zv))
        cat_ref[:, 0:d] = ya_ref[...]
        for gi in range(SSM_GROUPS):
            seg = y3[:, gi * gw:(gi + 1) * gw]
            cat_ref[:, d + gi * gw:d + (gi + 1) * gw] = (seg * _rms(seg) * gs_ref[:, gi * gw:(gi + 1) * gw]).astype(BF16)

    chunk = lambda w: pl.BlockSpec((CHUNK, w), lambda b, c: (b * nc + c, 0))
    vec = lambda w: pl.BlockSpec((1, w), lambda b, c: (0, 0))
    hp_spec = pl.BlockSpec((None, None, npair, D_STATE, LANES), lambda b, c: (b, c, 0, 0, 0))
    return _call(body, name=name, grid=(nseq, nc),
                 in_specs=[chunk(xbc), chunk(LANES), chunk(d), chunk(d), vec(h), vec(h), vec(d), vec(d),
                           pl.BlockSpec((h, d), lambda b, c: (0, 0))],
                 out_specs=[chunk(2 * d), chunk(d), hp_spec],
                 out_shape=[jax.ShapeDtypeStruct((t, 2 * d), BF16), jax.ShapeDtypeStruct((t, d), F32),
                            jax.ShapeDtypeStruct((nseq, nc, npair, D_STATE, LANES), F32)],
                 scratch_shapes=[pltpu.VMEM((npair, D_STATE, LANES), F32)],
                 compiler_params=_params(("arbitrary", "arbitrary")))(cpre, pdt, pz, ya, dtb, alog, dsk_lane, gs, emat)


def _ssd_bwd(cpre, pdt, pz, y2, hprev_all, dcat, dtb, alog, dsk_lane, gs, emat, *, nseq, seq, name):
    t, xbc = cpre.shape
    d = pz.shape[1]
    h = d // HEAD_DIM
    npair = h // 2
    ppg = npair // SSM_GROUPS
    nc = seq // CHUNK
    gw = d // SSM_GROUPS
    bc0 = d
    cc0 = d + SSM_GROUPS * D_STATE

    def body(c_ref, pdt_ref, z_ref, y2_ref, hp_ref, dys_ref, dtb_ref, alog_ref, dsk_ref, gs_ref, e_ref,
             dconv_ref, dz_ref, dpdt_ref, dgs_ref, ddsk_ref, ddtb_ref, dalog_ref, dh_ref):
        b = pl.program_id(0)
        c = pl.program_id(1)

        @pl.when(c == 0)
        def _():
            dh_ref[...] = jnp.zeros_like(dh_ref)

        @pl.when((b == 0) & (c == 0))
        def _():
            dgs_ref[...] = jnp.zeros_like(dgs_ref)
            ddsk_ref[...] = jnp.zeros_like(ddsk_ref)
            ddtb_ref[...] = jnp.zeros_like(ddtb_ref)
            dalog_ref[...] = jnp.zeros_like(dalog_ref)

        q = _ssd_common(c_ref, pdt_ref, dtb_ref, alog_ref, e_ref, h)
        cp, sg, act, cs, a, dt, lo = q["cp"], q["sg"], q["act"], q["cs"], q["a"], q["dt"], q["lo"]
        ecs_l, dt_l = q["ecs_l"], q["dt_l"]
        ev = e_ref[...]
        xs = act[:, :d]
        xd = xs * dt_l
        row16 = lax.broadcasted_iota(jnp.int32, (CHUNK, h), 0)
        hid = lax.broadcasted_iota(jnp.int32, (1, h), 1)

        zv = z_ref[...]
        sz = _sigmoid(zv)
        siluz = zv * sz
        y2v = y2_ref[...]
        y3 = y2v * siluz
        dysv = dys_ref[...]
        dy3s = []
        for gi in range(SSM_GROUPS):
            gsl = slice(gi * gw, (gi + 1) * gw)
            seg = y3[:, gsl]
            dseg, dgt = _rms_bwd(seg, _rms(seg), gs_ref[:, gsl], dysv[:, gsl])
            dy3s.append(dseg)
            dgs_ref[:, gsl] += jnp.sum(dgt, axis=0, keepdims=True)
        dy3 = jnp.concatenate(dy3s, axis=1)
        dy = dy3 * siluz
        dz_ref[...] = (dy3 * y2v * (sz * (1.0 + zv * (1.0 - sz)))).astype(BF16)
        ddsk_ref[...] += jnp.sum(_head_sums(dy * xs, ev), axis=0, keepdims=True)

        dcs = jnp.zeros((CHUNK, h), F32)
        dxd_parts, yoff_parts, db_parts, dc_parts = [], [], [], []
        for g in range(SSM_GROUPS):
            bg = act[:, bc0 + g * D_STATE: bc0 + (g + 1) * D_STATE]
            cg = act[:, cc0 + g * D_STATE: cc0 + (g + 1) * D_STATE]
            bgb, cgb = bg.astype(BF16), cg.astype(BF16)
            s = _dot_nt(cgb, bgb)
            ds = jnp.zeros((CHUNK, CHUNK), F32)
            dbg = jnp.zeros((CHUNK, D_STATE), F32)
            dcg = jnp.zeros((CHUNK, D_STATE), F32)
            for jj in range(ppg):
                j = g * ppg + jj
                sl = slice(LANES * j, LANES * (j + 1))
                xdj = xd[:, sl]
                xdb = xdj.astype(BF16)
                dyj = dy[:, sl]
                dyb = dyj.astype(BF16)
                hprev = hp_ref[j]
                hpb = hprev.astype(BF16)
                dhn = dh_ref[j]
                dhb = dhn.astype(BF16)
                ecs = ecs_l[:, sl]
                gmat = (dyj * ecs).astype(BF16)
                yoff_parts.append(_dot(cgb, hpb) * ecs)
                dcg = dcg + _dot_nt(gmat, hpb)
                dh_ref[j] = dhn * ecs[CHUNK - 1:CHUNK] + _dot_tn(cgb, gmat)
                t2 = dhn * hprev
                dxd_h = []
                for idx, hh in enumerate((2 * j, 2 * j + 1)):
                    msk = lo if idx == 0 else jnp.logical_not(lo)
                    onehot = (hid == hh).astype(F32)
                    csc = cs[:, hh:hh + 1]
                    csb = jnp.broadcast_to(csc, (CHUNK, CHUNK))
                    lm = jnp.exp(jnp.where(q["tril"], csb - csb.T, -jnp.inf))
                    m = s * lm
                    mb = m.astype(BF16)
                    cs_last = cs[CHUNK - 1:CHUNK, hh:hh + 1]
                    dte = jnp.exp(cs_last - csc)
                    bwb = (bg * dte).astype(BF16)
                    dxd_s = _dot(bwb, dhb)
                    dbw = _dot_nt(jnp.where(msk, xdj, 0.0).astype(BF16), dhb)
                    dbg = dbg + dbw * dte
                    qv = jnp.sum(dbw * bg, axis=-1, keepdims=True) * dte
                    dm = _dot_nt(jnp.where(msk, dyj, 0.0).astype(BF16), xdb)
                    dxd_d = _dot_tn(mb, dyb)
                    wm = dm * m
                    rc = jnp.sum(wm - wm.T, axis=-1, keepdims=True)
                    ds = ds + dm * lm
                    ddec = jnp.sum(jnp.where(msk, t2, 0.0)) * jnp.exp(cs_last)
                    last = jnp.sum(qv) + ddec
                    dcs = dcs + (rc - qv) * onehot + jnp.where(row16 == CHUNK - 1, last * onehot, 0.0)
                    dxd_h.append(dxd_s + dxd_d)
                dxd_parts.append(jnp.where(lo, dxd_h[0], dxd_h[1]))
            dsb = ds.astype(BF16)
            dc_parts.append(dcg + _dot(dsb, bgb))
            db_parts.append(dbg + _dot_tn(dsb, cgb))
        yoff_all = jnp.concatenate(yoff_parts, axis=1)
        dxd_all = jnp.concatenate(dxd_parts, axis=1)
        dcs = dcs + _head_sums(dy * yoff_all, ev)
        triu = (q["col"] >= q["row"]).astype(F32)
        dadt = jnp.dot(triu, dcs, precision=HIGHEST, preferred_element_type=F32)
        ddt = dadt * a + _head_sums(dxd_all * xs, ev)
        dalog_ref[...] += jnp.sum(dadt * dt, axis=0, keepdims=True) * a
        dpre = ddt * _sigmoid(q["pre"])
        ddtb_ref[...] += jnp.sum(dpre, axis=0, keepdims=True)
        dpdt_ref[...] = jnp.zeros_like(dpdt_ref)
        dpdt_ref[:, 0:h] = dpre.astype(BF16)
        dxs = dxd_all * dt_l + dy * dsk_ref[...]
        dact = jnp.concatenate([dxs] + db_parts + dc_parts, axis=1)
        dconv_ref[...] = dact * (sg * (1.0 + cp * (1.0 - sg)))

    rchunk = lambda w, cb=0: pl.BlockSpec((CHUNK, w), lambda b, c, cb=cb: (b * nc + nc - 1 - c, cb))
    vec = lambda w: pl.BlockSpec((1, w), lambda b, c: (0, 0))
    hp_spec = pl.BlockSpec((None, None, npair, D_STATE, LANES), lambda b, c: (b, nc - 1 - c, 0, 0, 0))
    return _call(body, name=name, grid=(nseq, nc),
                 in_specs=[rchunk(xbc), rchunk(LANES), rchunk(d), rchunk(d), hp_spec, rchunk(d, 1),
                           vec(h), vec(h), vec(d), vec(d), pl.BlockSpec((h, d), lambda b, c: (0, 0))],
                 out_specs=[rchunk(xbc), rchunk(d), rchunk(LANES), vec(d), vec(h), vec(h), vec(h)],
                 out_shape=[jax.ShapeDtypeStruct((t, xbc), F32), jax.ShapeDtypeStruct((t, d), BF16),
                            jax.ShapeDtypeStruct((t, LANES), BF16), jax.ShapeDtypeStruct((1, d), F32),
                            jax.ShapeDtypeStruct((1, h), F32), jax.ShapeDtypeStruct((1, h), F32),
                            jax.ShapeDtypeStruct((1, h), F32)],
                 scratch_shapes=[pltpu.VMEM((npair, D_STATE, LANES), F32)],
                 compiler_params=_params(("arbitrary", "arbitrary")))(
                     cpre, pdt, pz, y2, hprev_all, dcat, dtb, alog, dsk_lane, gs, emat)


def _sum_adamw(parts, w, m, v, *, name, layer=None, outs=None):
    n, r, c = parts.shape
    tr = _pick_rows(r, 256)
    bc1 = 1.0 - ADAM_B1 ** ADAM_STEP
    bc2 = 1.0 - ADAM_B2 ** ADAM_STEP

    def body(p_ref, w_ref, m_ref, v_ref, *rest):
        g_ref, d_ref, mo_ref, vo_ref = rest[-4:]
        g = p_ref[0].astype(F32)
        for k in range(1, n):
            g = g + p_ref[k].astype(F32)
        mn = ADAM_B1 * m_ref[...] + (1.0 - ADAM_B1) * g
        vn = ADAM_B2 * v_ref[...] + (1.0 - ADAM_B2) * (g * g)
        g_ref[...] = g
        mo_ref[...] = mn
        vo_ref[...] = vn
        d_ref[...] = -ADAM_LR * ((mn / bc1) / (jnp.sqrt(vn / bc2) + ADAM_EPS) + ADAM_WD * w_ref[...])

    p_spec = pl.BlockSpec((n, tr, c), lambda i: (0, i, 0))
    if layer is None:
        blk = pl.BlockSpec((tr, c), lambda i: (i, 0))
        return _call(body, name=name, grid=(r // tr,), in_specs=[p_spec, blk, blk, blk], out_specs=[blk] * 4,
                     out_shape=[jax.ShapeDtypeStruct((r, c), F32)] * 4,
                     compiler_params=_params(("parallel",)))(parts, w, m, v)
    blk = pl.BlockSpec((None, tr, c), lambda i: (layer, i, 0))
    if outs is None:
        outs = [lax.empty(w.shape, F32) for _ in range(4)]
    return _call(body, name=name, grid=(r // tr,),
                 in_specs=[p_spec, blk, blk, blk] + [pl.BlockSpec(memory_space=pl.ANY)] * 4, out_specs=[blk] * 4,
                 out_shape=[jax.ShapeDtypeStruct(w.shape, F32)] * 4, input_output_aliases={4 + k: k for k in range(4)},
                 compiler_params=_params(("parallel",)))(parts, w, m, v, *outs)


def _sum_parts(parts, *, name):
    n, r, c = parts.shape
    tr = _pick_rows(r, 256)

    def body(p_ref, g_ref):
        g = p_ref[0].astype(F32)
        for k in range(1, n):
            g = g + p_ref[k].astype(F32)
        g_ref[...] = g

    return _call(body, name=name, grid=(r // tr,), in_specs=[pl.BlockSpec((n, tr, c), lambda i: (0, i, 0))],
                 out_specs=pl.BlockSpec((tr, c), lambda i: (i, 0)), out_shape=jax.ShapeDtypeStruct((r, c), F32),
                 compiler_params=_params(("parallel",)))(parts)


def _peers():
    x, y, c = lax.axis_index("x"), lax.axis_index("y"), lax.axis_index("c")
    me = 4 * x + 2 * y + c
    out = []
    for k in range(1, N_DEV):
        px = (1 - x) if (k >> 2) & 1 else x
        py = (1 - y) if (k >> 1) & 1 else y
        pc = (1 - c) if k & 1 else c
        out.append(((px, py, pc), 4 * px + 2 * py + pc))
    return me, out


def _exchange(src, *, gather, name):
    shape = src.shape if gather else src.shape[1:]

    def body(s_ref, o_ref, send_sems, recv_sems, local_sem):
        me, peers = _peers()
        mine = pltpu.make_async_copy(s_ref if gather else s_ref.at[me], o_ref.at[me], local_sem)
        mine.start()
        sends = []
        for k, (dev, pid) in enumerate(peers):
            cp = pltpu.make_async_remote_copy(
                src_ref=s_ref if gather else s_ref.at[pid], dst_ref=o_ref.at[me],
                send_sem=send_sems.at[k], recv_sem=recv_sems.at[k], device_id=dev, device_id_type=MESH)
            cp.start()
            sends.append(cp)
        for k, (dev, pid) in enumerate(peers):
            pltpu.make_async_remote_copy(
                src_ref=s_ref if gather else s_ref.at[pid], dst_ref=o_ref.at[pid],
                send_sem=send_sems.at[k], recv_sem=recv_sems.at[k], device_id=dev, device_id_type=MESH).wait_recv()
        for cp in sends:
            cp.wait_send()
        mine.wait()

    any_spec = pl.BlockSpec(memory_space=pl.ANY)
    return _call(body, name=name, in_specs=[any_spec], out_specs=any_spec,
                 out_shape=jax.ShapeDtypeStruct((N_DEV,) + tuple(shape), src.dtype),
                 scratch_shapes=[pltpu.SemaphoreType.DMA((N_DEV - 1,)), pltpu.SemaphoreType.DMA((N_DEV - 1,)),
                                 pltpu.SemaphoreType.DMA(())])(src)


_HBM = pl.BlockSpec(memory_space=pltpu.HBM)
_SEM = pl.BlockSpec(memory_space=pltpu.SEMAPHORE)
_EFFECT = pltpu.SideEffectType.DATAFLOW_SIDE_EFFECTING


def _split_copies(s_refs, l_refs, send_sems, recv_sems, gather, incoming):
    me, peers = _peers()
    local, remote = [], []
    for ti, (s_ref, l_ref) in enumerate(zip(s_refs, l_refs)):
        base = ti * N_DEV
        local.append(pltpu.make_async_copy(s_ref if gather else s_ref.at[me], l_ref.at[me], recv_sems.at[base + N_DEV - 1]))
        for k, (dev, pid) in enumerate(peers):
            sems = dict(send_sem=send_sems.at[base + k], recv_sem=recv_sems.at[base + k], device_id=dev, device_id_type=MESH)
            src = s_ref if gather else s_ref.at[pid]
            remote.append((
                pltpu.make_async_remote_copy(src_ref=src, dst_ref=l_ref.at[me], **sems),
                pltpu.make_async_remote_copy(src_ref=src, dst_ref=l_ref.at[pid], **sems) if incoming else None))
    return local, remote


def _exchange_start(srcs, *, gather, name, after=()):
    n = len(srcs)
    after = list(after)
    srcs = [pltpu.with_memory_space_constraint(s, pltpu.HBM) for s in srcs]
    lands = [pltpu.with_memory_space_constraint(
        lax.empty((N_DEV,) + tuple(s.shape if gather else s.shape[1:]), s.dtype), pltpu.HBM) for s in srcs]

    def body(*refs):
        s_refs, l_refs = refs[:n], refs[n:2 * n]
        outs = refs[2 * n + len(after):]
        send_sems, recv_sems, token = outs[0], outs[1], outs[-1]
        local, remote = _split_copies(s_refs, l_refs, send_sems, recv_sems, gather, incoming=False)
        for cp in local:
            cp.start()
        for out_cp, _ in remote:
            out_cp.start()
        token[...] = jnp.zeros_like(token)

    outs = _call(
        body, name=name,
        out_shape=(pltpu.SemaphoreType.DMA((n * N_DEV,)), pltpu.SemaphoreType.DMA((n * N_DEV,)),
                   *[pltpu.HBM(s.shape, s.dtype) for s in srcs], *[pltpu.HBM(l.shape, l.dtype) for l in lands],
                   jax.ShapeDtypeStruct((SUBLANES, LANES), F32)),
        in_specs=[_HBM] * (2 * n) + [pl.BlockSpec(memory_space=pl.ANY)] * len(after),
        out_specs=(_SEM, _SEM, *[_HBM] * (2 * n), pl.BlockSpec(memory_space=pltpu.VMEM)),
        input_output_aliases={k: k + 2 for k in range(2 * n)},
        compiler_params=pltpu.CompilerParams(has_side_effects=_EFFECT),
    )(*srcs, *lands, *after)
    return dict(n=n, gather=gather, sems=outs[:2], srcs=outs[2:2 + n], lands=outs[2 + n:2 + 2 * n]), outs[-1]


def _exchange_wait(state, after, *, name):
    n, gather = state["n"], state["gather"]
    after = list(after)

    def body(*refs):
        s_refs, l_refs = refs[:n], refs[n:2 * n]
        send_sems, recv_sems = refs[2 * n], refs[2 * n + 1]
        local, remote = _split_copies(s_refs, l_refs, send_sems, recv_sems, gather, incoming=True)
        for out_cp, in_cp in remote:
            out_cp.wait_send()
            in_cp.wait_recv()
        for cp in local:
            cp.wait()

    outs = _call(
        body, name=name,
        out_shape=tuple(pltpu.HBM(a.shape, a.dtype) for a in (*state["srcs"], *state["lands"])),
        in_specs=[_HBM] * (2 * n) + [_SEM, _SEM] + [pl.BlockSpec(memory_space=pl.ANY)] * len(after),
        out_specs=tuple([_HBM] * (2 * n)),
        input_output_aliases={k: k for k in range(2 * n)},
        compiler_params=pltpu.CompilerParams(has_side_effects=_EFFECT),
    )(*state["srcs"], *state["lands"], *state["sems"], *after)
    return outs[n:]


def _pack(arrs):
    flat = jnp.concatenate([a.reshape(-1).astype(F32) for a in arrs])
    pad = (-flat.shape[0]) % (SUBLANES * LANES)
    return jnp.pad(flat, (0, pad)).reshape(-1, LANES)


def _unpack(packed, shapes):
    flat = packed.reshape(-1)
    out, off = [], 0
    for s in shapes:
        n = 1
        for v in s:
            n *= v
        out.append(flat[off:off + n].reshape(s))
        off += n
    return out


SMALL = ("norm_mix_pre", "ssm_conv_b", "dt_bias", "a_log", "d_skip", "conv_out_norm", "ssm_out_norm",
         "norm_mix_post", "norm_mlp_pre", "norm_mlp_post", "conv_a_w", "ssm_conv_w")
BIG = ("w_in", "w_out", "w_up", "w_down")
ORDER = ("norm_mix_pre", "w_in", "conv_a_w", "ssm_conv_w", "ssm_conv_b", "dt_bias", "a_log", "d_skip",
         "conv_out_norm", "ssm_out_norm", "w_out", "norm_mix_post", "norm_mlp_pre", "w_up", "w_down", "norm_mlp_post")


def kernel(x, norm_mix_pre, w_in, conv_a_w, ssm_conv_w, ssm_conv_b, dt_bias, a_log, d_skip, conv_out_norm, ssm_out_norm, w_out, norm_mix_post, norm_mlp_pre, w_up, w_down, norm_mlp_post, loss_target, m_norm_mix_pre, m_w_in, m_conv_a_w, m_ssm_conv_w, m_ssm_conv_b, m_dt_bias, m_a_log, m_d_skip, m_conv_out_norm, m_ssm_out_norm, m_w_out, m_norm_mix_post, m_norm_mlp_pre, m_w_up, m_w_down, m_norm_mlp_post, v_norm_mix_pre, v_w_in, v_conv_a_w, v_ssm_conv_w, v_ssm_conv_b, v_dt_bias, v_a_log, v_d_skip, v_conv_out_norm, v_ssm_out_norm, v_w_out, v_norm_mix_post, v_norm_mlp_pre, v_w_up, v_w_down, v_norm_mlp_post):
    W = dict(norm_mix_pre=norm_mix_pre, w_in=w_in, conv_a_w=conv_a_w, ssm_conv_w=ssm_conv_w, ssm_conv_b=ssm_conv_b,
             dt_bias=dt_bias, a_log=a_log, d_skip=d_skip, conv_out_norm=conv_out_norm, ssm_out_norm=ssm_out_norm,
             w_out=w_out, norm_mix_post=norm_mix_post, norm_mlp_pre=norm_mlp_pre, w_up=w_up, w_down=w_down,
             norm_mlp_post=norm_mlp_post)
    M = dict(norm_mix_pre=m_norm_mix_pre, w_in=m_w_in, conv_a_w=m_conv_a_w, ssm_conv_w=m_ssm_conv_w,
             ssm_conv_b=m_ssm_conv_b, dt_bias=m_dt_bias, a_log=m_a_log, d_skip=m_d_skip,
             conv_out_norm=m_conv_out_norm, ssm_out_norm=m_ssm_out_norm, w_out=m_w_out,
             norm_mix_post=m_norm_mix_post, norm_mlp_pre=m_norm_mlp_pre, w_up=m_w_up, w_down=m_w_down,
             norm_mlp_post=m_norm_mlp_post)
    V = dict(norm_mix_pre=v_norm_mix_pre, w_in=v_w_in, conv_a_w=v_conv_a_w, ssm_conv_w=v_ssm_conv_w,
             ssm_conv_b=v_ssm_conv_b, dt_bias=v_dt_bias, a_log=v_a_log, d_skip=v_d_skip,
             conv_out_norm=v_conv_out_norm, ssm_out_norm=v_ssm_out_norm, w_out=v_w_out,
             norm_mix_post=v_norm_mix_post, norm_mlp_pre=v_norm_mlp_pre, w_up=v_w_up, w_down=v_w_down,
             norm_mlp_post=v_norm_mlp_post)

    nseq, seq, d = x.shape
    t = nseq * seq
    depth = w_in.shape[0]
    h = d // HEAD_DIM
    xbc = d + 2 * SSM_GROUPS * D_STATE
    in_cols = w_in.shape[2] * N_DEV
    d_mix = w_out.shape[1] * N_DEV
    d_ff = w_up.shape[2] * N_DEV
    me = 4 * lax.axis_index("x") + 2 * lax.axis_index("y") + lax.axis_index("c")
    ca_shard = conv_a_w.shape[2]
    sc_shard = ssm_conv_w.shape[2]

    tap_shapes = [conv_a_w.shape[1:], ssm_conv_w.shape[1:]]

    def gather_start(i, after=()):
        st_in, tok_in = _exchange_start([w_in[i].astype(BF16), _pack([conv_a_w[i], ssm_conv_w[i]])], gather=True,
                                        name=f"gather_start_in_{i}", after=after)
        st_rest, tok_rest = _exchange_start([W[n][i].astype(BF16) for n in ("w_out", "w_up", "w_down")], gather=True,
                                            name=f"gather_start_rest_{i}", after=[tok_in])
        return st_in, st_rest, tok_rest

    vec = lambda name, i: W[name][i].reshape(1, -1)
    emat = (lax.broadcasted_iota(jnp.int32, (h, d), 1) // HEAD_DIM == lax.broadcasted_iota(jnp.int32, (h, d), 0)).astype(F32)

    xcur = x.reshape(t, d)
    hcur = _norm_fwd(xcur, vec("norm_mix_pre", 0), name="norm_first")
    saved = []
    nxt = gather_start(0)
    for i in range(depth):
        st_in, st_rest, tok = nxt
        win_g, taps_g = _exchange_wait(st_in, [hcur, tok], name=f"gather_wait_in_{i}")
        win = jnp.transpose(win_g, (1, 0, 2)).reshape(d, in_cols)
        wdt = jnp.pad(win[:, 4 * d + xbc:], ((0, 0), (0, LANES - h)))
        taps_j = [_unpack(taps_g[j], tap_shapes) for j in range(N_DEV)]
        conv_a_i = jnp.concatenate([tj[0] for tj in taps_j], axis=1)
        ssm_conv_i = jnp.concatenate([tj[1] for tj in taps_j], axis=1)
        pa = _mm(hcur, win, n=3 * d, name=f"fwd_proj_a_{i}")
        pz = _mm(hcur, win, n=d, b_off=3 * d, name=f"fwd_proj_z_{i}")
        pxbc = _mm(hcur, win, n=xbc, b_off=4 * d, name=f"fwd_proj_xbc_{i}")
        pdt = _mm(hcur, wdt, name=f"fwd_proj_dt_{i}")
        ya, va = _conva_fwd(pa, conv_a_i, vec("conv_out_norm", i), seq=seq, name=f"fwd_conv_a_{i}")
        cpre = _convb_fwd(pxbc, ssm_conv_i, vec("ssm_conv_b", i), seq=seq, name=f"fwd_conv_b_{i}")
        dsk_lane = jnp.repeat(W["d_skip"][i], HEAD_DIM).reshape(1, d)
        cat, y2, hprev = _ssd_fwd(cpre, pdt, pz, ya, vec("dt_bias", i), vec("a_log", i), dsk_lane,
                                  vec("ssm_out_norm", i), emat, nseq=nseq, seq=seq, name=f"fwd_ssd_{i}")
        wout_g, wup_g, wdown_g = _exchange_wait(st_rest, [cat], name=f"gather_wait_rest_{i}")
        lw = dict(win=win, wdt=wdt, wout=wout_g.reshape(d_mix, d),
                  wup=jnp.transpose(wup_g, (1, 0, 2)).reshape(d, d_ff), wdown=wdown_g.reshape(d_ff, d),
                  conv_a=conv_a_i, ssm_conv=ssm_conv_i)
        after = []
        if i + 1 < depth:
            nxt = gather_start(i + 1, after=[wout_g])
            after = [nxt[2]]
        mix = _mm(cat, lw["wout"], name=f"fwd_out_{i}", after=after)
        x1, h2 = _resid_norm(xcur, mix, vec("norm_mix_post", i), vec("norm_mlp_pre", i), name=f"fwd_post_mix_{i}")
        f = _mm(h2, lw["wup"], name=f"fwd_up_{i}", out_dtypes=(BF16,), epi=_epi_relu2)
        dn = _mm(f, lw["wdown"], name=f"fwd_down_{i}")
        g_next = vec("norm_mix_pre", i + 1) if i + 1 < depth else vec("norm_mix_pre", 0)
        x2, hnext = _resid_norm(x1, dn, vec("norm_mlp_post", i), g_next, name=f"fwd_post_mlp_{i}")
        saved.append(dict(lw=lw, x0=xcur, h=hcur, pa=pa, pz=pz, pxbc=pxbc, pdt=pdt, va=va, cpre=cpre, y2=y2,
                          hprev=hprev, cat=cat, mix=mix, x1=x1, h2=h2, f=f, dn=dn, dsk_lane=dsk_lane))
        xcur, hcur = x2, hnext

    dx, loss_part = _loss_fwd_bwd(xcur, loss_target.reshape(t, d), name="loss")
    loss = lax.psum(loss_part[0, 0], ("x", "y", "c"))

    small_grads = {n: [None] * depth for n in SMALL}
    big_out = {n: None for n in BIG}

    def finish(pending, after):
        li, st_a, st_b = pending

        def update(n, parts):
            big_out[n] = _sum_adamw(parts, W[n], M[n], V[n], layer=li, outs=big_out[n], name=f"adamw_{n}_{li}")

        p_down, p_up = _exchange_wait(st_a, after, name=f"scatter_wait_a_{li}")
        update("w_down", p_down)
        update("w_up", p_up)
        p_out, p_in = _exchange_wait(st_b, after + [big_out["w_up"][0]], name=f"scatter_wait_b_{li}")
        update("w_out", p_out)
        update("w_in", p_in)

    pending = None
    for i in reversed(range(depth)):
        s = saved[i]
        lw = s["lw"]
        ddn, dg = _bwd_norm_out(s["dn"], vec("norm_mlp_post", i), dx, name=f"bwd_norm_mlp_post_{i}")
        small_grads["norm_mlp_post"][i] = dg
        dup = _mm(ddn, lw["wdown"], tb=True, name=f"bwd_down_dx_{i}", out_dtypes=(BF16,), epi=_epi_drelu2,
                  extras=(s["f"],))
        g_wdown = _mm(s["f"], ddn, ta=True, name=f"bwd_down_dw_{i}", out_dtypes=(BF16,))
        dh2 = _mm(dup, lw["wup"], tb=True, name=f"bwd_up_dx_{i}")
        g_wup = _mm(s["h2"], dup, ta=True, name=f"bwd_up_dw_{i}", out_dtypes=(BF16,))
        st_a, tok_a = _exchange_start(
            [g_wdown.reshape(N_DEV, d_ff // N_DEV, d), jnp.transpose(g_wup.reshape(d, N_DEV, d_ff // N_DEV), (1, 0, 2))],
            gather=False, name=f"scatter_start_a_{i}")
        dx1, dmix, dg_pre, dg_post = _bwd_norm_pair(s["x1"], [dh2], dx, s["mix"], vec("norm_mlp_pre", i) + tok_a[0:1, 0:1],
                                                    vec("norm_mix_post", i), name=f"bwd_norm_mix_post_{i}")
        small_grads["norm_mlp_pre"][i] = dg_pre
        small_grads["norm_mix_post"][i] = dg_post
        dcat = _mm(dmix, lw["wout"], tb=True, name=f"bwd_out_dx_{i}")
        g_wout = _mm(s["cat"], dmix, ta=True, name=f"bwd_out_dw_{i}", out_dtypes=(BF16,))
        dconv, dz, dpdt, dgs, ddsk, ddtb, dalog = _ssd_bwd(
            s["cpre"], s["pdt"], s["pz"], s["y2"], s["hprev"], dcat, vec("dt_bias", i), vec("a_log", i),
            s["dsk_lane"], vec("ssm_out_norm", i), emat, nseq=nseq, seq=seq, name=f"bwd_ssd_{i}")
        small_grads["ssm_out_norm"][i] = dgs
        small_grads["d_skip"][i] = ddsk
        small_grads["dt_bias"][i] = ddtb
        small_grads["a_log"][i] = dalog
        dpxbc, dscw, dscb = _convb_bwd(dconv, s["pxbc"], lw["ssm_conv"], seq=seq, name=f"bwd_conv_b_{i}")
        small_grads["ssm_conv_w"][i] = dscw
        small_grads["ssm_conv_b"][i] = dscb
        dpa, dcaw, dgca = _conva_bwd(dcat, s["pa"], s["va"], lw["conv_a"], vec("conv_out_norm", i), seq=seq,
                                     name=f"bwd_conv_a_{i}")
        small_grads["conv_a_w"][i] = dcaw
        small_grads["conv_out_norm"][i] = dgca
        g_win = jnp.concatenate([
            _mm(s["h"], dpa, ta=True, name=f"bwd_proj_a_dw_{i}", out_dtypes=(BF16,)),
            _mm(s["h"], dz, ta=True, name=f"bwd_proj_z_dw_{i}", out_dtypes=(BF16,)),
            _mm(s["h"], dpxbc, ta=True, name=f"bwd_proj_xbc_dw_{i}", out_dtypes=(BF16,)),
            _mm(s["h"], dpdt, ta=True, name=f"bwd_proj_dt_dw_{i}", out_dtypes=(BF16,))[:, :h]], axis=1)
        st_b, tok_b = _exchange_start(
            [g_wout.reshape(N_DEV, d_mix // N_DEV, d), jnp.transpose(g_win.reshape(d, N_DEV, in_cols // N_DEV), (1, 0, 2))],
            gather=False, name=f"scatter_start_b_{i}")
        dh_parts = [_mm(dpa, lw["win"], tb=True, name=f"bwd_proj_a_dx_{i}", after=[tok_b]),
                    _mm(dz, lw["win"], tb=True, b_koff=3 * d, name=f"bwd_proj_z_dx_{i}", after=[tok_b]),
                    _mm(dpxbc, lw["win"], tb=True, b_koff=4 * d, name=f"bwd_proj_xbc_dx_{i}", after=[tok_b]),
                    _mm(dpdt, lw["wdt"], tb=True, name=f"bwd_proj_dt_dx_{i}", after=[tok_b])]
        dx, dg_in = _bwd_norm_in(s["x0"], dh_parts, dx1, vec("norm_mix_pre", i), name=f"bwd_norm_mix_pre_{i}")
        small_grads["norm_mix_pre"][i] = dg_in
        if pending is not None:
            finish(pending, [dx])
        pending = (i, st_a, st_b)

    grad_x = dx.reshape(nseq, seq, d)

    small_shapes_full = {n: (depth,) + tuple(small_grads[n][0].shape) for n in SMALL}
    gpack = _pack([jnp.stack(small_grads[n]) for n in SMALL])
    gparts = _exchange(gpack, gather=True, name="allreduce_small")

    def shard_of(n, full):
        if n == "conv_a_w":
            return lax.dynamic_slice_in_dim(full, me * ca_shard, ca_shard, axis=2)
        if n == "ssm_conv_w":
            return lax.dynamic_slice_in_dim(full, me * sc_shard, sc_shard, axis=2)
        return full.reshape(W[n].shape)

    gsum = _sum_parts(gparts, name="sum_small")
    gfull = _unpack(gsum, [small_shapes_full[n] for n in SMALL])
    gsmall = {n: shard_of(n, gf) for n, gf in zip(SMALL, gfull)}
    res = _sum_adamw(_pack([gsmall[n] for n in SMALL])[None], _pack([W[n] for n in SMALL]),
                     _pack([M[n] for n in SMALL]), _pack([V[n] for n in SMALL]), name="adamw_small")
    small_out = [dict(zip(SMALL, _unpack(r, [W[n].shape for n in SMALL]))) for r in res]
    finish(pending, [dx, res[0]])

    def out_of(kind, n):
        return big_out[n][kind] if n in BIG else small_out[kind][n]

    return (loss, grad_x, *[out_of(k, n) for k in range(4) for n in ORDER])
```

```python
import functools

import jax
import jax.numpy as jnp
from jax import lax
from jax.experimental import pallas as pl
from jax.experimental.pallas import tpu as pltpu

F32 = jnp.float32
BF16 = jnp.bfloat16
HIGHEST = lax.Precision.HIGHEST
MESH = pl.DeviceIdType.MESH

EPS = 1e-6
HEAD_DIM = 64
D_STATE = 128
SSM_GROUPS = 2
CHUNK = 128
CONV_K = 3
SSM_CONV_K = 4
ADAM_LR = 0.001
ADAM_B1 = 0.9
ADAM_B2 = 0.999
ADAM_EPS = 1e-08
ADAM_WD = 0.01
ADAM_STEP = 10

N_DEV = 8
LANES = 128
SUBLANES = 8
VMEM_LIMIT = 48 * 1024 * 1024
ROW_TILE = 512
MM_TILE = 1024


def _params(sem):
    return pltpu.CompilerParams(dimension_semantics=sem, vmem_limit_bytes=VMEM_LIMIT)


def _call(body, **kw):
    return pl.pallas_call(body, **kw)


def _pick(n, cap):
    best = None
    for t in range(LANES, min(n, cap) + 1, LANES):
        if n % t == 0:
            best = t
    return best or n


def _pick_rows(n, cap):
    best = None
    for t in range(SUBLANES, min(n, cap) + 1, SUBLANES):
        if n % t == 0:
            best = t
    return best or n


def _sigmoid(x):
    return 1.0 / (1.0 + jnp.exp(-x))


def _softplus(x):
    return jnp.maximum(x, 0.0) + jnp.log1p(jnp.exp(-jnp.abs(x)))


def _rms(x):
    return lax.rsqrt(jnp.mean(x * x, axis=-1, keepdims=True) + EPS)


def _rms_bwd(x, r, g, dy):
    gy = dy * g
    dx = r * gy - x * (r * r * r) * jnp.mean(gy * x, axis=-1, keepdims=True)
    return dx, dy * x * r


def _full(shape):
    return pl.BlockSpec(shape, lambda *_: (0,) * len(shape))


def _mm(a, b, *, name, ta=False, tb=False, out_dtypes=(F32,), epi=None, extras=(), n=None, b_off=0, b_koff=0,
        after=()):
    m, k = (a.shape[1], a.shape[0]) if ta else a.shape
    if n is None:
        n = b.shape[0] if tb else b.shape[1]
    tm, tn, tk = _pick(m, MM_TILE), _pick(n, MM_TILE), _pick(k, MM_TILE)
    while b_off % tn or n % tn:
        tn -= LANES
    while b_koff % tk or k % tk:
        tk -= LANES
    nk = k // tk
    nm, nn = m // tm, n // tn
    jo = b_off // tn
    ko = b_koff // tk
    a_bytes = m * k * a.dtype.itemsize
    b_bytes = n * k * b.dtype.itemsize
    m_outer = a_bytes + nm * b_bytes <= b_bytes + nn * a_bytes
    ij = (lambda g0, g1: (g0, g1)) if m_outer else (lambda g0, g1: (g1, g0))
    grid = (nm, nn, nk) if m_outer else (nn, nm, nk)

    def a_map(g0, g1, kk):
        i, _ = ij(g0, g1)
        return (kk, i) if ta else (i, kk)

    def b_map(g0, g1, kk):
        _, j = ij(g0, g1)
        return (j + jo, kk + ko) if tb else (kk + ko, j + jo)

    def o_map(g0, g1, kk):
        return ij(g0, g1)

    a_spec = pl.BlockSpec((tk, tm) if ta else (tm, tk), a_map)
    b_spec = pl.BlockSpec((tn, tk) if tb else (tk, tn), b_map)
    o_spec = pl.BlockSpec((tm, tn), o_map)
    dims = (((0 if ta else 1,), (1 if tb else 0,)), ((), ()))
    n_ex = len(extras)
    after = list(after)
    o0 = 2 + n_ex + len(after)

    def finish(acc, ex, outs):
        res = (acc,) if epi is None else epi(acc, *[e[...] for e in ex])
        for o, r in zip(outs, res):
            o[...] = r.astype(o.dtype)

    def body_single(*refs):
        a_ref, b_ref = refs[:2]
        acc = lax.dot_general(a_ref[...].astype(BF16), b_ref[...].astype(BF16), dims, preferred_element_type=F32)
        finish(acc, refs[2:2 + n_ex], refs[o0:])

    def body_multi(*refs):
        a_ref, b_ref = refs[:2]
        acc = refs[-1]
        kk = pl.program_id(2)

        @pl.when(kk == 0)
        def _():
            acc[...] = jnp.zeros_like(acc)

        acc[...] += lax.dot_general(a_ref[...].astype(BF16), b_ref[...].astype(BF16), dims, preferred_element_type=F32)

        @pl.when(kk == nk - 1)
        def _():
            finish(acc[...], refs[2:2 + n_ex], refs[o0:-1])

    outs = _call(
        body_single if nk == 1 else body_multi, name=name, grid=grid,
        in_specs=[a_spec, b_spec] + [o_spec] * n_ex + [pl.BlockSpec(memory_space=pl.ANY)] * len(after),
        out_specs=[o_spec] * len(out_dtypes),
        out_shape=[jax.ShapeDtypeStruct((m, n), dt) for dt in out_dtypes],
        scratch_shapes=[] if nk == 1 else [pltpu.VMEM((tm, tn), F32)],
        compiler_params=_params(("parallel", "parallel", "arbitrary")),
    )(a, b, *extras, *after)
    return outs[0] if len(out_dtypes) == 1 else outs


def _epi_relu2(acc):
    r = jnp.maximum(acc, 0.0)
    return (r * r,)


def _epi_drelu2(acc, f):
    return (acc * (2.0 * jnp.sqrt(f.astype(F32))),)


def _norm_fwd(x, g, *, name):
    t, d = x.shape
    tt = _pick_rows(t, ROW_TILE)

    def body(x_ref, g_ref, h_ref):
        xv = x_ref[...]
        h_ref[...] = (xv * _rms(xv) * g_ref[...]).astype(BF16)

    row = pl.BlockSpec((tt, d), lambda i: (i, 0))
    return _call(body, name=name, grid=(t // tt,), in_specs=[row, _full((1, d))], out_specs=row,
                 out_shape=jax.ShapeDtypeStruct((t, d), BF16), compiler_params=_params(("parallel",)))(x, g)


def _resid_norm(x, n, g1, g2, *, name):
    t, d = x.shape
    tt = _pick_rows(t, ROW_TILE)

    def body(x_ref, n_ref, g1_ref, g2_ref, xo_ref, h_ref):
        nv = n_ref[...]
        xn = x_ref[...] + nv * _rms(nv) * g1_ref[...]
        xo_ref[...] = xn
        h_ref[...] = (xn * _rms(xn) * g2_ref[...]).astype(BF16)

    row = pl.BlockSpec((tt, d), lambda i: (i, 0))
    return _call(body, name=name, grid=(t // tt,), in_specs=[row, row, _full((1, d)), _full((1, d))],
                 out_specs=[row, row],
                 out_shape=[jax.ShapeDtypeStruct((t, d), F32), jax.ShapeDtypeStruct((t, d), BF16)],
                 compiler_params=_params(("parallel",)))(x, n, g1, g2)


def _loss_fwd_bwd(xf, target, *, name):
    t, d = xf.shape
    tt = _pick_rows(t, ROW_TILE)
    nt = t // tt

    def body(x_ref, t_ref, dy_ref, loss_ref, acc):
        i = pl.program_id(0)

        @pl.when(i == 0)
        def _():
            acc[...] = jnp.zeros_like(acc)

        e = x_ref[...] - t_ref[...]
        dy_ref[...] = e * (1.0 / d)
        acc[...] += jnp.sum(e * e, axis=0, keepdims=True)

        @pl.when(i == nt - 1)
        def _():
            loss_ref[...] = jnp.sum(acc[...], axis=-1, keepdims=True) * (0.5 / d)

    row = pl.BlockSpec((tt, d), lambda i: (i, 0))
    return _call(body, name=name, grid=(nt,), in_specs=[row, row], out_specs=[row, _full((1, 1))],
                 out_shape=[jax.ShapeDtypeStruct((t, d), F32), jax.ShapeDtypeStruct((1, 1), F32)],
                 scratch_shapes=[pltpu.VMEM((1, d), F32)], compiler_params=_params(("arbitrary",)))(xf, target)


def _bwd_norm_pair(xin, dh, dres, n, g_in, g_out, *, name):
    t, d = xin.shape
    tt = _pick_rows(t, ROW_TILE)
    n_dh = len(dh)

    def body(*refs):
        x_ref = refs[0]
        dh_refs = refs[1:1 + n_dh]
        dres_ref, n_ref, gi_ref, go_ref, dx_ref, dn_ref, dgi_ref, dgo_ref = refs[1 + n_dh:]
        i = pl.program_id(0)

        @pl.when(i == 0)
        def _():
            dgi_ref[...] = jnp.zeros_like(dgi_ref)
            dgo_ref[...] = jnp.zeros_like(dgo_ref)

        xv = x_ref[...]
        dhv = dh_refs[0][...].astype(F32)
        for r in dh_refs[1:]:
            dhv = dhv + r[...].astype(F32)
        dxh, dgi = _rms_bwd(xv, _rms(xv), gi_ref[...], dhv)
        dx = dres_ref[...] + dxh
        dx_ref[...] = dx
        dgi_ref[...] += jnp.sum(dgi, axis=0, keepdims=True)
        nv = n_ref[...]
        dn, dgo = _rms_bwd(nv, _rms(nv), go_ref[...], dx)
        dn_ref[...] = dn.astype(BF16)
        dgo_ref[...] += jnp.sum(dgo, axis=0, keepdims=True)

    row = pl.BlockSpec((tt, d), lambda i: (i, 0))
    vec = _full((1, d))
    return _call(body, name=name, grid=(t // tt,), in_specs=[row] * (n_dh + 3) + [vec, vec],
                 out_specs=[row, row, vec, vec],
                 out_shape=[jax.ShapeDtypeStruct((t, d), F32), jax.ShapeDtypeStruct((t, d), BF16),
                            jax.ShapeDtypeStruct((1, d), F32), jax.ShapeDtypeStruct((1, d), F32)],
                 compiler_params=_params(("arbitrary",)))(xin, *dh, dres, n, g_in, g_out)


def _bwd_norm_in(xin, dh, dres, g_in, *, name):
    t, d = xin.shape
    tt = _pick_rows(t, ROW_TILE)
    n_dh = len(dh)

    def body(*refs):
        x_ref = refs[0]
        dh_refs = refs[1:1 + n_dh]
        dres_ref, gi_ref, dx_ref, dgi_ref = refs[1 + n_dh:]
        i = pl.program_id(0)

        @pl.when(i == 0)
        def _():
            dgi_ref[...] = jnp.zeros_like(dgi_ref)

        xv = x_ref[...]
        dhv = dh_refs[0][...].astype(F32)
        for r in dh_refs[1:]:
            dhv = dhv + r[...].astype(F32)
        dxh, dgi = _rms_bwd(xv, _rms(xv), gi_ref[...], dhv)
        dx_ref[...] = dres_ref[...] + dxh
        dgi_ref[...] += jnp.sum(dgi, axis=0, keepdims=True)

    row = pl.BlockSpec((tt, d), lambda i: (i, 0))
    vec = _full((1, d))
    return _call(body, name=name, grid=(t // tt,), in_specs=[row] * (n_dh + 2) + [vec],
                 out_specs=[row, vec],
                 out_shape=[jax.ShapeDtypeStruct((t, d), F32), jax.ShapeDtypeStruct((1, d), F32)],
                 compiler_params=_params(("arbitrary",)))(xin, *dh, dres, g_in)


def _bwd_norm_out(n, g_out, dx, *, name):
    t, d = n.shape
    tt = _pick_rows(t, ROW_TILE)

    def body(n_ref, go_ref, dx_ref, dn_ref, dgo_ref):
        i = pl.program_id(0)

        @pl.when(i == 0)
        def _():
            dgo_ref[...] = jnp.zeros_like(dgo_ref)

        nv = n_ref[...]
        dn, dgo = _rms_bwd(nv, _rms(nv), go_ref[...], dx_ref[...])
        dn_ref[...] = dn.astype(BF16)
        dgo_ref[...] += jnp.sum(dgo, axis=0, keepdims=True)

    row = pl.BlockSpec((tt, d), lambda i: (i, 0))
    vec = _full((1, d))
    return _call(body, name=name, grid=(t // tt,), in_specs=[row, vec, row], out_specs=[row, vec],
                 out_shape=[jax.ShapeDtypeStruct((t, d), BF16), jax.ShapeDtypeStruct((1, d), F32)],
                 compiler_params=_params(("arbitrary",)))(n, g_out, dx)


def _shift_down(cur, halo, s):
    return jnp.concatenate([halo[SUBLANES - s:], cur[:cur.shape[0] - s]], axis=0)


def _shift_up(cur, halo, s):
    return jnp.concatenate([cur[s:], halo[:s]], axis=0)


def _conva_fwd(pa, w, g, *, seq, name):
    t, d3 = pa.shape
    d = d3 // 3
    tt = _pick_rows(seq, ROW_TILE)
    tps = seq // tt

    def body(xa_ref, ca_ref, ba_ref, w_ref, g_ref, ya_ref, v_ref, carry):
        i = pl.program_id(0)

        @pl.when(i % tps == 0)
        def _():
            carry[...] = jnp.zeros_like(carry)

        u = ca_ref[...].astype(F32) * xa_ref[...].astype(F32)
        halo = carry[...]
        wv = w_ref[...]
        v = wv[2:3] * u + wv[1:2] * _shift_down(u, halo, 1) + wv[0:1] * _shift_down(u, halo, 2)
        carry[...] = u[tt - SUBLANES:]
        yp = ba_ref[...].astype(F32) * v
        ya_ref[...] = (yp * _rms(yp) * g_ref[...]).astype(BF16)
        v_ref[...] = v.astype(BF16)

    col = lambda c: pl.BlockSpec((tt, d), lambda i, c=c: (i, c))
    row = pl.BlockSpec((tt, d), lambda i: (i, 0))
    return _call(body, name=name, grid=(t // tt,),
                 in_specs=[col(0), col(1), col(2), _full((CONV_K, d)), _full((1, d))], out_specs=[row, row],
                 out_shape=[jax.ShapeDtypeStruct((t, d), BF16), jax.ShapeDtypeStruct((t, d), BF16)],
                 scratch_shapes=[pltpu.VMEM((SUBLANES, d), F32)],
                 compiler_params=_params(("arbitrary",)))(pa, pa, pa, w, g)


def _conva_bwd(dcat, pa, v, w, g, *, seq, name):
    t, d3 = pa.shape
    d = d3 // 3
    tt = _pick_rows(seq, ROW_TILE)
    tps = seq // tt
    nt = t // tt

    def body(dya_ref, xa_ref, ca_ref, ba_ref, v_ref, w_ref, g_ref, dpa_ref, dw_ref, dg_ref, carry):
        i = pl.program_id(0)

        @pl.when(i == 0)
        def _():
            dw_ref[...] = jnp.zeros_like(dw_ref)
            dg_ref[...] = jnp.zeros_like(dg_ref)

        @pl.when(i % tps == 0)
        def _():
            carry[...] = jnp.zeros_like(carry)

        xa, ca, ba, vv = [r[...].astype(F32) for r in (xa_ref, ca_ref, ba_ref, v_ref)]
        yp = ba * vv
        dyp, dgt = _rms_bwd(yp, _rms(yp), g_ref[...], dya_ref[...].astype(F32))
        dg_ref[...] += jnp.sum(dgt, axis=0, keepdims=True)
        dv = dyp * ba
        halo = carry[...]
        dv1 = _shift_up(dv, halo, 1)
        dv2 = _shift_up(dv, halo, 2)
        carry[...] = dv[:SUBLANES]
        wv = w_ref[...]
        du = wv[2:3] * dv + wv[1:2] * dv1 + wv[0:1] * dv2
        u = ca * xa
        dw_ref[0:1, :] += jnp.sum(u * dv2, axis=0, keepdims=True)
        dw_ref[1:2, :] += jnp.sum(u * dv1, axis=0, keepdims=True)
        dw_ref[2:3, :] += jnp.sum(u * dv, axis=0, keepdims=True)
        dpa_ref[:, 0:d] = (du * ca).astype(BF16)
        dpa_ref[:, d:2 * d] = (du * xa).astype(BF16)
        dpa_ref[:, 2 * d:3 * d] = (dyp * vv).astype(BF16)

    rcol = lambda c: pl.BlockSpec((tt, d), lambda i, c=c: (nt - 1 - i, c))
    return _call(body, name=name, grid=(nt,),
                 in_specs=[rcol(0), rcol(0), rcol(1), rcol(2), rcol(0), _full((CONV_K, d)), _full((1, d))],
                 out_specs=[pl.BlockSpec((tt, d3), lambda i: (nt - 1 - i, 0)), _full((CONV_K, d)), _full((1, d))],
                 out_shape=[jax.ShapeDtypeStruct((t, d3), BF16), jax.ShapeDtypeStruct((CONV_K, d), F32),
                            jax.ShapeDtypeStruct((1, d), F32)],
                 scratch_shapes=[pltpu.VMEM((SUBLANES, d), F32)],
                 compiler_params=_params(("arbitrary",)))(dcat, pa, pa, pa, v, w, g)


def _convb_fwd(pxbc, w, bias, *, seq, name):
    t, c = pxbc.shape
    tt = _pick_rows(seq, ROW_TILE)
    tps = seq // tt

    def body(p_ref, w_ref, b_ref, o_ref, carry):
        i = pl.program_id(0)

        @pl.when(i % tps == 0)
        def _():
            carry[...] = jnp.zeros_like(carry)

        p = p_ref[...].astype(F32)
        halo = carry[...]
        wv = w_ref[...]
        o = wv[3:4] * p + b_ref[...]
        for s in (1, 2, 3):
            o = o + wv[3 - s:4 - s] * _shift_down(p, halo, s)
        carry[...] = p[tt - SUBLANES:]
        o_ref[...] = o.astype(BF16)

    row = pl.BlockSpec((tt, c), lambda i: (i, 0))
    return _call(body, name=name, grid=(t // tt,), in_specs=[row, _full((SSM_CONV_K, c)), _full((1, c))],
                 out_specs=row, out_shape=jax.ShapeDtypeStruct((t, c), BF16),
                 scratch_shapes=[pltpu.VMEM((SUBLANES, c), F32)],
                 compiler_params=_params(("arbitrary",)))(pxbc, w, bias)


def _convb_bwd(dconv, pxbc, w, *, seq, name):
    t, c = pxbc.shape
    tt = _pick_rows(seq, ROW_TILE)
    tps = seq // tt
    nt = t // tt

    def body(dc_ref, p_ref, w_ref, dp_ref, dw_ref, db_ref, carry):
        i = pl.program_id(0)

        @pl.when(i == 0)
        def _():
            dw_ref[...] = jnp.zeros_like(dw_ref)
            db_ref[...] = jnp.zeros_like(db_ref)

        @pl.when(i % tps == 0)
        def _():
            carry[...] = jnp.zeros_like(carry)

        dc = dc_ref[...].astype(F32)
        p = p_ref[...].astype(F32)
        halo = carry[...]
        wv = w_ref[...]
        dp = wv[3:4] * dc
        dw_ref[3:4, :] += jnp.sum(p * dc, axis=0, keepdims=True)
        for s in (1, 2, 3):
            dcs = _shift_up(dc, halo, s)
            dp = dp + wv[3 - s:4 - s] * dcs
            dw_ref[3 - s:4 - s, :] += jnp.sum(p * dcs, axis=0, keepdims=True)
        carry[...] = dc[:SUBLANES]
        db_ref[...] += jnp.sum(dc, axis=0, keepdims=True)
        dp_ref[...] = dp.astype(BF16)

    rrow = pl.BlockSpec((tt, c), lambda i: (nt - 1 - i, 0))
    return _call(body, name=name, grid=(nt,), in_specs=[rrow, rrow, _full((SSM_CONV_K, c))],
                 out_specs=[rrow, _full((SSM_CONV_K, c)), _full((1, c))],
                 out_shape=[jax.ShapeDtypeStruct((t, c), BF16), jax.ShapeDtypeStruct((SSM_CONV_K, c), F32),
                            jax.ShapeDtypeStruct((1, c), F32)],
                 scratch_shapes=[pltpu.VMEM((SUBLANES, c), F32)],
                 compiler_params=_params(("arbitrary",)))(dconv, pxbc, w)


def _expand_heads(x, ev):
    return jnp.dot(x, ev, precision=HIGHEST, preferred_element_type=F32)


def _head_sums(v, ev):
    return lax.dot_general(v, ev, (((1,), (1,)), ((), ())), precision=HIGHEST, preferred_element_type=F32)


def _ssd_common(c_ref, pdt_ref, dtb_ref, alog_ref, e_ref, h):
    cp = c_ref[...].astype(F32)
    sg = _sigmoid(cp)
    act = cp * sg
    pre = pdt_ref[:, 0:h] + dtb_ref[...]
    dt = _softplus(pre)
    a = -jnp.exp(alog_ref[...])
    adt = dt * a
    row = lax.broadcasted_iota(jnp.int32, (CHUNK, CHUNK), 0)
    col = lax.broadcasted_iota(jnp.int32, (CHUNK, CHUNK), 1)
    tril = row >= col
    cs = jnp.dot(tril.astype(F32), adt, precision=HIGHEST, preferred_element_type=F32)
    ev = e_ref[...]
    dt_l = _expand_heads(dt, ev)
    ecs_l = jnp.exp(_expand_heads(cs, ev))
    return dict(cp=cp, sg=sg, act=act, pre=pre, dt=dt, a=a, cs=cs, dt_l=dt_l, ecs_l=ecs_l,
                tril=tril, row=row, col=col, lo=col < HEAD_DIM)


def _dot_nt(a, b):
    return lax.dot_general(a, b, (((1,), (1,)), ((), ())), preferred_element_type=F32)


def _dot_tn(a, b):
    return lax.dot_general(a, b, (((0,), (0,)), ((), ())), preferred_element_type=F32)


def _dot(a, b):
    return jnp.dot(a, b, preferred_element_type=F32)


def _ssd_fwd(cpre, pdt, pz, ya, dtb, alog, dsk_lane, gs, emat, *, nseq, seq, name):
    t, xbc = cpre.shape
    d = pz.shape[1]
    h = d // HEAD_DIM
    npair = h // 2
    ppg = npair // SSM_GROUPS
    nc = seq // CHUNK
    gw = d // SSM_GROUPS
    bc0 = d
    cc0 = d + SSM_GROUPS * D_STATE

    def body(c_ref, pdt_ref, z_ref, ya_ref, dtb_ref, alog_ref, dsk_ref, gs_ref, e_ref, cat_ref, y2_ref, hp_ref, h_ref):
        c = pl.program_id(1)

        @pl.when(c == 0)
        def _():
            h_ref[...] = jnp.zeros_like(h_ref)

        q = _ssd_common(c_ref, pdt_ref, dtb_ref, alog_ref, e_ref, h)
        act, cs, lo, ecs_l = q["act"], q["cs"], q["lo"], q["ecs_l"]
        xs = act[:, :d]
        xd = xs * q["dt_l"]
        ys = []
        for g in range(SSM_GROUPS):
            bg = act[:, bc0 + g * D_STATE: bc0 + (g + 1) * D_STATE]
            cgb = act[:, cc0 + g * D_STATE: cc0 + (g + 1) * D_STATE].astype(BF16)
            s = _dot_nt(cgb, bg.astype(BF16))
            for jj in range(ppg):
                j = g * ppg + jj
                sl = slice(LANES * j, LANES * (j + 1))
                xdb = xd[:, sl].astype(BF16)
                hprev = h_ref[j]
                hp_ref[j] = hprev.astype(BF16)
                yd, st = [], []
                for hh in (2 * j, 2 * j + 1):
                    csc = cs[:, hh:hh + 1]
                    csb = jnp.broadcast_to(csc, (CHUNK, CHUNK))
                    lm = jnp.exp(jnp.where(q["tril"], csb - csb.T, -jnp.inf))
                    yd.append(_dot((s * lm).astype(BF16), xdb))
                    dte = jnp.exp(cs[CHUNK - 1:CHUNK, hh:hh + 1] - csc)
                    st.append(_dot_tn((bg * dte).astype(BF16), xdb))
                ecs = ecs_l[:, sl]
                yoff = _dot(cgb, hprev.astype(BF16)) * ecs
                h_ref[j] = hprev * ecs[CHUNK - 1:CHUNK] + jnp.where(lo, st[0], st[1])
                ys.append(jnp.where(lo, yd[0], yd[1]) + yoff)
        y = jnp.concatenate(ys, axis=1) + dsk_ref[...] * xs
        y2_ref[...] = y.astype(BF16)
        zv = z_ref[...].astype(F32)
        y3 = y * (zv * _sigmoid(zv))
        cat_ref[:, 0:d] = ya_ref[...]
        for gi in range(SSM_GROUPS):
            seg = y3[:, gi * gw:(gi + 1) * gw]
            cat_ref[:, d + gi * gw:d + (gi + 1) * gw] = (seg * _rms(seg) * gs_ref[:, gi * gw:(gi + 1) * gw]).astype(BF16)

    chunk = lambda w: pl.BlockSpec((CHUNK, w), lambda b, c: (b * nc + c, 0))
    vec = lambda w: pl.BlockSpec((1, w), lambda b, c: (0, 0))
    hp_spec = pl.BlockSpec((None, None, npair, D_STATE, LANES), lambda b, c: (b, c, 0, 0, 0))
    return _call(body, name=name, grid=(nseq, nc),
                 in_specs=[chunk(xbc), chunk(LANES), chunk(d), chunk(d), vec(h), vec(h), vec(d), vec(d),
                           pl.BlockSpec((h, d), lambda b, c: (0, 0))],
                 out_specs=[chunk(2 * d), chunk(d), hp_spec],
                 out_shape=[jax.ShapeDtypeStruct((t, 2 * d), BF16), jax.ShapeDtypeStruct((t, d), BF16),
                            jax.ShapeDtypeStruct((nseq, nc, npair, D_STATE, LANES), BF16)],
                 scratch_shapes=[pltpu.VMEM((npair, D_STATE, LANES), F32)],
                 compiler_params=_params(("arbitrary", "arbitrary")))(cpre, pdt, pz, ya, dtb, alog, dsk_lane, gs, emat)


def _ssd_bwd(cpre, pdt, pz, y2, hprev_all, dcat, dtb, alog, dsk_lane, gs, emat, *, nseq, seq, name):
    t, xbc = cpre.shape
    d = pz.shape[1]
    h = d // HEAD_DIM
    npair = h // 2
    ppg = npair // SSM_GROUPS
    nc = seq // CHUNK
    gw = d // SSM_GROUPS
    bc0 = d
    cc0 = d + SSM_GROUPS * D_STATE

    def body(c_ref, pdt_ref, z_ref, y2_ref, hp_ref, dys_ref, dtb_ref, alog_ref, dsk_ref, gs_ref, e_ref,
             dconv_ref, dz_ref, dpdt_ref, dgs_ref, ddsk_ref, ddtb_ref, dalog_ref, dh_ref):
        b = pl.program_id(0)
        c = pl.program_id(1)

        @pl.when(c == 0)
        def _():
            dh_ref[...] = jnp.zeros_like(dh_ref)

        @pl.when((b == 0) & (c == 0))
        def _():
            dgs_ref[...] = jnp.zeros_like(dgs_ref)
            ddsk_ref[...] = jnp.zeros_like(ddsk_ref)
            ddtb_ref[...] = jnp.zeros_like(ddtb_ref)
            dalog_ref[...] = jnp.zeros_like(dalog_ref)

        q = _ssd_common(c_ref, pdt_ref, dtb_ref, alog_ref, e_ref, h)
        cp, sg, act, cs, a, dt, lo = q["cp"], q["sg"], q["act"], q["cs"], q["a"], q["dt"], q["lo"]
        ecs_l, dt_l = q["ecs_l"], q["dt_l"]
        ev = e_ref[...]
        xs = act[:, :d]
        xd = xs * dt_l
        row16 = lax.broadcasted_iota(jnp.int32, (CHUNK, h), 0)
        hid = lax.broadcasted_iota(jnp.int32, (1, h), 1)

        zv = z_ref[...].astype(F32)
        sz = _sigmoid(zv)
        siluz = zv * sz
        y2v = y2_ref[...].astype(F32)
        y3 = y2v * siluz
        dysv = dys_ref[...].astype(F32)
        dy3s = []
        for gi in range(SSM_GROUPS):
            gsl = slice(gi * gw, (gi + 1) * gw)
            seg = y3[:, gsl]
            dseg, dgt = _rms_bwd(seg, _rms(seg), gs_ref[:, gsl], dysv[:, gsl])
            dy3s.append(dseg)
            dgs_ref[:, gsl] += jnp.sum(dgt, axis=0, keepdims=True)
        dy3 = jnp.concatenate(dy3s, axis=1)
        dy = dy3 * siluz
        dz_ref[...] = (dy3 * y2v * (sz * (1.0 + zv * (1.0 - sz)))).astype(BF16)
        ddsk_ref[...] += jnp.sum(_head_sums(dy * xs, ev), axis=0, keepdims=True)

        dcs = jnp.zeros((CHUNK, h), F32)
        dxd_parts, yoff_parts, db_parts, dc_parts = [], [], [], []
        for g in range(SSM_GROUPS):
            bg = act[:, bc0 + g * D_STATE: bc0 + (g + 1) * D_STATE]
            cg = act[:, cc0 + g * D_STATE: cc0 + (g + 1) * D_STATE]
            bgb, cgb = bg.astype(BF16), cg.astype(BF16)
            s = _dot_nt(cgb, bgb)
            ds = jnp.zeros((CHUNK, CHUNK), F32)
            dbg = jnp.zeros((CHUNK, D_STATE), F32)
            dcg = jnp.zeros((CHUNK, D_STATE), F32)
            for jj in range(ppg):
                j = g * ppg + jj
                sl = slice(LANES * j, LANES * (j + 1))
                xdj = xd[:, sl]
                xdb = xdj.astype(BF16)
                dyj = dy[:, sl]
                dyb = dyj.astype(BF16)
                hpb = hp_ref[j]
                hprev = hpb.astype(F32)
                dhn = dh_ref[j]
                dhb = dhn.astype(BF16)
                ecs = ecs_l[:, sl]
                gmat = (dyj * ecs).astype(BF16)
                yoff_parts.append(_dot(cgb, hpb) * ecs)
                dcg = dcg + _dot_nt(gmat, hpb)
                dh_ref[j] = dhn * ecs[CHUNK - 1:CHUNK] + _dot_tn(cgb, gmat)
                t2 = dhn * hprev
                dxd_h = []
                for idx, hh in enumerate((2 * j, 2 * j + 1)):
                    msk = lo if idx == 0 else jnp.logical_not(lo)
                    onehot = (hid == hh).astype(F32)
                    csc = cs[:, hh:hh + 1]
                    csb = jnp.broadcast_to(csc, (CHUNK, CHUNK))
                    lm = jnp.exp(jnp.where(q["tril"], csb - csb.T, -jnp.inf))
                    m = s * lm
                    mb = m.astype(BF16)
                    cs_last = cs[CHUNK - 1:CHUNK, hh:hh + 1]
                    dte = jnp.exp(cs_last - csc)
                    bwb = (bg * dte).astype(BF16)
                    dxd_s = _dot(bwb, dhb)
                    dbw = _dot_nt(jnp.where(msk, xdj, 0.0).astype(BF16), dhb)
                    dbg = dbg + dbw * dte
                    qv = jnp.sum(dbw * bg, axis=-1, keepdims=True) * dte
                    dm = _dot_nt(jnp.where(msk, dyj, 0.0).astype(BF16), xdb)
                    dxd_d = _dot_tn(mb, dyb)
                    wm = dm * m
                    rc = jnp.sum(wm - wm.T, axis=-1, keepdims=True)
                    ds = ds + dm * lm
                    ddec = jnp.sum(jnp.where(msk, t2, 0.0)) * jnp.exp(cs_last)
                    last = jnp.sum(qv) + ddec
                    dcs = dcs + (rc - qv) * onehot + jnp.where(row16 == CHUNK - 1, last * onehot, 0.0)
                    dxd_h.append(dxd_s + dxd_d)
                dxd_parts.append(jnp.where(lo, dxd_h[0], dxd_h[1]))
            dsb = ds.astype(BF16)
            dc_parts.append(dcg + _dot(dsb, bgb))
            db_parts.append(dbg + _dot_tn(dsb, cgb))
        yoff_all = jnp.concatenate(yoff_parts, axis=1)
        dxd_all = jnp.concatenate(dxd_parts, axis=1)
        dcs = dcs + _head_sums(dy * yoff_all, ev)
        triu = (q["col"] >= q["row"]).astype(F32)
        dadt = jnp.dot(triu, dcs, precision=HIGHEST, preferred_element_type=F32)
        ddt = dadt * a + _head_sums(dxd_all * xs, ev)
        dalog_ref[...] += jnp.sum(dadt * dt, axis=0, keepdims=True) * a
        dpre = ddt * _sigmoid(q["pre"])
        ddtb_ref[...] += jnp.sum(dpre, axis=0, keepdims=True)
        dpdt_ref[...] = jnp.zeros_like(dpdt_ref)
        dpdt_ref[:, 0:h] = dpre.astype(BF16)
        dxs = dxd_all * dt_l + dy * dsk_ref[...]
        dact = jnp.concatenate([dxs] + db_parts + dc_parts, axis=1)
        dconv_ref[...] = (dact * (sg * (1.0 + cp * (1.0 - sg)))).astype(BF16)

    rchunk = lambda w, cb=0: pl.BlockSpec((CHUNK, w), lambda b, c, cb=cb: (b * nc + nc - 1 - c, cb))
    vec = lambda w: pl.BlockSpec((1, w), lambda b, c: (0, 0))
    hp_spec = pl.BlockSpec((None, None, npair, D_STATE, LANES), lambda b, c: (b, nc - 1 - c, 0, 0, 0))
    return _call(body, name=name, grid=(nseq, nc),
                 in_specs=[rchunk(xbc), rchunk(LANES), rchunk(d), rchunk(d), hp_spec, rchunk(d, 1),
                           vec(h), vec(h), vec(d), vec(d), pl.BlockSpec((h, d), lambda b, c: (0, 0))],
                 out_specs=[rchunk(xbc), rchunk(d), rchunk(LANES), vec(d), vec(h), vec(h), vec(h)],
                 out_shape=[jax.ShapeDtypeStruct((t, xbc), BF16), jax.ShapeDtypeStruct((t, d), BF16),
                            jax.ShapeDtypeStruct((t, LANES), BF16), jax.ShapeDtypeStruct((1, d), F32),
                            jax.ShapeDtypeStruct((1, h), F32), jax.ShapeDtypeStruct((1, h), F32),
                            jax.ShapeDtypeStruct((1, h), F32)],
                 scratch_shapes=[pltpu.VMEM((npair, D_STATE, LANES), F32)],
                 compiler_params=_params(("arbitrary", "arbitrary")))(
                     cpre, pdt, pz, y2, hprev_all, dcat, dtb, alog, dsk_lane, gs, emat)


def _sum_adamw(parts, w, m, v, *, name, layer=None, outs=None):
    n, r, c = parts.shape
    tr = _pick_rows(r, 256)
    bc1 = 1.0 - ADAM_B1 ** ADAM_STEP
    bc2 = 1.0 - ADAM_B2 ** ADAM_STEP

    def body(p_ref, w_ref, m_ref, v_ref, *rest):
        g_ref, d_ref, mo_ref, vo_ref = rest[-4:]
        g = p_ref[0].astype(F32)
        for k in range(1, n):
            g = g + p_ref[k].astype(F32)
        mn = ADAM_B1 * m_ref[...] + (1.0 - ADAM_B1) * g
        vn = ADAM_B2 * v_ref[...] + (1.0 - ADAM_B2) * (g * g)
        g_ref[...] = g
        mo_ref[...] = mn
        vo_ref[...] = vn
        d_ref[...] = -ADAM_LR * ((mn / bc1) / (jnp.sqrt(vn / bc2) + ADAM_EPS) + ADAM_WD * w_ref[...])

    p_spec = pl.BlockSpec((n, tr, c), lambda i: (0, i, 0))
    if layer is None:
        blk = pl.BlockSpec((tr, c), lambda i: (i, 0))
        return _call(body, name=name, grid=(r // tr,), in_specs=[p_spec, blk, blk, blk], out_specs=[blk] * 4,
                     out_shape=[jax.ShapeDtypeStruct((r, c), F32)] * 4,
                     compiler_params=_params(("parallel",)))(parts, w, m, v)
    blk = pl.BlockSpec((None, tr, c), lambda i: (layer, i, 0))
    if outs is None:
        outs = [lax.empty(w.shape, F32) for _ in range(4)]
    return _call(body, name=name, grid=(r // tr,),
                 in_specs=[p_spec, blk, blk, blk] + [pl.BlockSpec(memory_space=pl.ANY)] * 4, out_specs=[blk] * 4,
                 out_shape=[jax.ShapeDtypeStruct(w.shape, F32)] * 4, input_output_aliases={4 + k: k for k in range(4)},
                 compiler_params=_params(("parallel",)))(parts, w, m, v, *outs)


def _assemble_cols(blocks, *, name):
    nb, r, c = blocks.shape
    width = -(-nb * c // LANES) * LANES
    tr = _pick_rows(r, 256)

    def body(b_ref, o_ref):
        pieces = [b_ref[j] for j in range(nb)]
        if width > nb * c:
            pieces.append(jnp.zeros((tr, width - nb * c), blocks.dtype))
        o_ref[...] = jnp.concatenate(pieces, axis=1)

    return _call(body, name=name, grid=(r // tr,), in_specs=[pl.BlockSpec((nb, tr, c), lambda i: (0, i, 0))],
                 out_specs=pl.BlockSpec((tr, width), lambda i: (i, 0)), out_shape=jax.ShapeDtypeStruct((r, width), blocks.dtype),
                 compiler_params=_params(("parallel",)))(blocks)


def _split_cols(pieces, c, *, name):
    r = pieces[0].shape[0]
    tr = _pick_rows(r, 256)
    n_in = len(pieces)

    def body(*refs):
        o_ref = refs[n_in]
        x = jnp.concatenate([p[...] for p in refs[:n_in]], axis=1) if n_in > 1 else refs[0][...]
        for j in range(N_DEV):
            o_ref[j] = x[:, c * j:c * (j + 1)]

    return _call(body, name=name, grid=(r // tr,),
                 in_specs=[pl.BlockSpec((tr, p.shape[1]), lambda i: (i, 0)) for p in pieces],
                 out_specs=pl.BlockSpec((N_DEV, tr, c), lambda i: (0, i, 0)),
                 out_shape=jax.ShapeDtypeStruct((N_DEV, r, c), pieces[0].dtype),
                 compiler_params=_params(("parallel",)))(*pieces)


def _sum_parts(parts, *, name):
    n, r, c = parts.shape
    tr = _pick_rows(r, 256)

    def body(p_ref, g_ref):
        g = p_ref[0].astype(F32)
        for k in range(1, n):
            g = g + p_ref[k].astype(F32)
        g_ref[...] = g

    return _call(body, name=name, grid=(r // tr,), in_specs=[pl.BlockSpec((n, tr, c), lambda i: (0, i, 0))],
                 out_specs=pl.BlockSpec((tr, c), lambda i: (i, 0)), out_shape=jax.ShapeDtypeStruct((r, c), F32),
                 compiler_params=_params(("parallel",)))(parts)


def _peers():
    x, y, c = lax.axis_index("x"), lax.axis_index("y"), lax.axis_index("c")
    me = 4 * x + 2 * y + c
    out = []
    for k in range(1, N_DEV):
        px = (1 - x) if (k >> 2) & 1 else x
        py = (1 - y) if (k >> 1) & 1 else y
        pc = (1 - c) if k & 1 else c
        out.append(((px, py, pc), 4 * px + 2 * py + pc))
    return me, out


def _exchange(src, *, gather, name):
    shape = src.shape if gather else src.shape[1:]

    def body(s_ref, o_ref, send_sems, recv_sems, local_sem):
        me, peers = _peers()
        mine = pltpu.make_async_copy(s_ref if gather else s_ref.at[me], o_ref.at[me], local_sem)
        mine.start()
        sends = []
        for k, (dev, pid) in enumerate(peers):
            cp = pltpu.make_async_remote_copy(
                src_ref=s_ref if gather else s_ref.at[pid], dst_ref=o_ref.at[me],
                send_sem=send_sems.at[k], recv_sem=recv_sems.at[k], device_id=dev, device_id_type=MESH)
            cp.start()
            sends.append(cp)
        for k, (dev, pid) in enumerate(peers):
            pltpu.make_async_remote_copy(
                src_ref=s_ref if gather else s_ref.at[pid], dst_ref=o_ref.at[pid],
                send_sem=send_sems.at[k], recv_sem=recv_sems.at[k], device_id=dev, device_id_type=MESH).wait_recv()
        for cp in sends:
            cp.wait_send()
        mine.wait()

    any_spec = pl.BlockSpec(memory_space=pl.ANY)
    return _call(body, name=name, in_specs=[any_spec], out_specs=any_spec,
                 out_shape=jax.ShapeDtypeStruct((N_DEV,) + tuple(shape), src.dtype),
                 scratch_shapes=[pltpu.SemaphoreType.DMA((N_DEV - 1,)), pltpu.SemaphoreType.DMA((N_DEV - 1,)),
                                 pltpu.SemaphoreType.DMA(())])(src)


_HBM = pl.BlockSpec(memory_space=pltpu.HBM)
_SEM = pl.BlockSpec(memory_space=pltpu.SEMAPHORE)
_EFFECT = pltpu.SideEffectType.DATAFLOW_SIDE_EFFECTING


def _split_copies(s_refs, l_refs, send_sems, recv_sems, gather, incoming):
    me, peers = _peers()
    local, remote = [], []
    for ti, (s_ref, l_ref) in enumerate(zip(s_refs, l_refs)):
        base = ti * N_DEV
        local.append(pltpu.make_async_copy(s_ref if gather else s_ref.at[me], l_ref.at[me], recv_sems.at[base + N_DEV - 1]))
        for k, (dev, pid) in enumerate(peers):
            sems = dict(send_sem=send_sems.at[base + k], recv_sem=recv_sems.at[base + k], device_id=dev, device_id_type=MESH)
            src = s_ref if gather else s_ref.at[pid]
            remote.append((
                pltpu.make_async_remote_copy(src_ref=src, dst_ref=l_ref.at[me], **sems),
                pltpu.make_async_remote_copy(src_ref=src, dst_ref=l_ref.at[pid], **sems) if incoming else None))
    return local, remote


def _exchange_start(srcs, *, gather, name, after=()):
    n = len(srcs)
    after = list(after)
    srcs = [pltpu.with_memory_space_constraint(s, pltpu.HBM) for s in srcs]
    lands = [pltpu.with_memory_space_constraint(
        lax.empty((N_DEV,) + tuple(s.shape if gather else s.shape[1:]), s.dtype), pltpu.HBM) for s in srcs]

    def body(*refs):
        s_refs, l_refs = refs[:n], refs[n:2 * n]
        outs = refs[2 * n + len(after):]
        send_sems, recv_sems, token = outs[0], outs[1], outs[-1]
        local, remote = _split_copies(s_refs, l_refs, send_sems, recv_sems, gather, incoming=False)
        for cp in local:
            cp.start()
        for out_cp, _ in remote:
            out_cp.start()
        token[...] = jnp.zeros_like(token)

    outs = _call(
        body, name=name,
        out_shape=(pltpu.SemaphoreType.DMA((n * N_DEV,)), pltpu.SemaphoreType.DMA((n * N_DEV,)),
                   *[pltpu.HBM(s.shape, s.dtype) for s in srcs], *[pltpu.HBM(l.shape, l.dtype) for l in lands],
                   jax.ShapeDtypeStruct((SUBLANES, LANES), F32)),
        in_specs=[_HBM] * (2 * n) + [pl.BlockSpec(memory_space=pl.ANY)] * len(after),
        out_specs=(_SEM, _SEM, *[_HBM] * (2 * n), pl.BlockSpec(memory_space=pltpu.VMEM)),
        input_output_aliases={k: k + 2 for k in range(2 * n)},
        compiler_params=pltpu.CompilerParams(has_side_effects=_EFFECT),
    )(*srcs, *lands, *after)
    return dict(n=n, gather=gather, sems=outs[:2], srcs=outs[2:2 + n], lands=outs[2 + n:2 + 2 * n]), outs[-1]


def _exchange_wait(state, after, *, name):
    n, gather = state["n"], state["gather"]
    after = list(after)

    def body(*refs):
        s_refs, l_refs = refs[:n], refs[n:2 * n]
        send_sems, recv_sems = refs[2 * n], refs[2 * n + 1]
        local, remote = _split_copies(s_refs, l_refs, send_sems, recv_sems, gather, incoming=True)
        for out_cp, in_cp in remote:
            out_cp.wait_send()
            in_cp.wait_recv()
        for cp in local:
            cp.wait()

    outs = _call(
        body, name=name,
        out_shape=tuple(pltpu.HBM(a.shape, a.dtype) for a in (*state["srcs"], *state["lands"])),
        in_specs=[_HBM] * (2 * n) + [_SEM, _SEM] + [pl.BlockSpec(memory_space=pl.ANY)] * len(after),
        out_specs=tuple([_HBM] * (2 * n)),
        input_output_aliases={k: k for k in range(2 * n)},
        compiler_params=pltpu.CompilerParams(has_side_effects=_EFFECT),
    )(*state["srcs"], *state["lands"], *state["sems"], *after)
    return outs[n:]


def _pack(arrs):
    flat = jnp.concatenate([a.reshape(-1).astype(F32) for a in arrs])
    pad = (-flat.shape[0]) % (SUBLANES * LANES)
    return jnp.pad(flat, (0, pad)).reshape(-1, LANES)


def _unpack(packed, shapes):
    flat = packed.reshape(-1)
    out, off = [], 0
    for s in shapes:
        n = 1
        for v in s:
            n *= v
        out.append(flat[off:off + n].reshape(s))
        off += n
    return out


SMALL = ("norm_mix_pre", "ssm_conv_b", "dt_bias", "a_log", "d_skip", "conv_out_norm", "ssm_out_norm",
         "norm_mix_post", "norm_mlp_pre", "norm_mlp_post", "conv_a_w", "ssm_conv_w")
BIG = ("w_in", "w_out", "w_up", "w_down")
ORDER = ("norm_mix_pre", "w_in", "conv_a_w", "ssm_conv_w", "ssm_conv_b", "dt_bias", "a_log", "d_skip",
         "conv_out_norm", "ssm_out_norm", "w_out", "norm_mix_post", "norm_mlp_pre", "w_up", "w_down", "norm_mlp_post")


def kernel(x, norm_mix_pre, w_in, conv_a_w, ssm_conv_w, ssm_conv_b, dt_bias, a_log, d_skip, conv_out_norm, ssm_out_norm, w_out, norm_mix_post, norm_mlp_pre, w_up, w_down, norm_mlp_post, loss_target, m_norm_mix_pre, m_w_in, m_conv_a_w, m_ssm_conv_w, m_ssm_conv_b, m_dt_bias, m_a_log, m_d_skip, m_conv_out_norm, m_ssm_out_norm, m_w_out, m_norm_mix_post, m_norm_mlp_pre, m_w_up, m_w_down, m_norm_mlp_post, v_norm_mix_pre, v_w_in, v_conv_a_w, v_ssm_conv_w, v_ssm_conv_b, v_dt_bias, v_a_log, v_d_skip, v_conv_out_norm, v_ssm_out_norm, v_w_out, v_norm_mix_post, v_norm_mlp_pre, v_w_up, v_w_down, v_norm_mlp_post):
    W = dict(norm_mix_pre=norm_mix_pre, w_in=w_in, conv_a_w=conv_a_w, ssm_conv_w=ssm_conv_w, ssm_conv_b=ssm_conv_b,
             dt_bias=dt_bias, a_log=a_log, d_skip=d_skip, conv_out_norm=conv_out_norm, ssm_out_norm=ssm_out_norm,
             w_out=w_out, norm_mix_post=norm_mix_post, norm_mlp_pre=norm_mlp_pre, w_up=w_up, w_down=w_down,
             norm_mlp_post=norm_mlp_post)
    M = dict(norm_mix_pre=m_norm_mix_pre, w_in=m_w_in, conv_a_w=m_conv_a_w, ssm_conv_w=m_ssm_conv_w,
             ssm_conv_b=m_ssm_conv_b, dt_bias=m_dt_bias, a_log=m_a_log, d_skip=m_d_skip,
             conv_out_norm=m_conv_out_norm, ssm_out_norm=m_ssm_out_norm, w_out=m_w_out,
             norm_mix_post=m_norm_mix_post, norm_mlp_pre=m_norm_mlp_pre, w_up=m_w_up, w_down=m_w_down,
             norm_mlp_post=m_norm_mlp_post)
    V = dict(norm_mix_pre=v_norm_mix_pre, w_in=v_w_in, conv_a_w=v_conv_a_w, ssm_conv_w=v_ssm_conv_w,
             ssm_conv_b=v_ssm_conv_b, dt_bias=v_dt_bias, a_log=v_a_log, d_skip=v_d_skip,
             conv_out_norm=v_conv_out_norm, ssm_out_norm=v_ssm_out_norm, w_out=v_w_out,
             norm_mix_post=v_norm_mix_post, norm_mlp_pre=v_norm_mlp_pre, w_up=v_w_up, w_down=v_w_down,
             norm_mlp_post=v_norm_mlp_post)

    nseq, seq, d = x.shape
    t = nseq * seq
    depth = w_in.shape[0]
    h = d // HEAD_DIM
    xbc = d + 2 * SSM_GROUPS * D_STATE
    in_cols = w_in.shape[2] * N_DEV
    d_mix = w_out.shape[1] * N_DEV
    d_ff = w_up.shape[2] * N_DEV
    me = 4 * lax.axis_index("x") + 2 * lax.axis_index("y") + lax.axis_index("c")
    ca_shard = conv_a_w.shape[2]
    sc_shard = ssm_conv_w.shape[2]

    tap_shapes = [conv_a_w.shape[1:], ssm_conv_w.shape[1:]]

    def gather_start(i, after=()):
        st_in, tok_in = _exchange_start([w_in[i].astype(BF16), _pack([conv_a_w[i], ssm_conv_w[i]])], gather=True,
                                        name=f"gather_start_in_{i}", after=after)
        st_rest, tok_rest = _exchange_start([W[n][i].astype(BF16) for n in ("w_out", "w_up", "w_down")], gather=True,
                                            name=f"gather_start_rest_{i}", after=[tok_in])
        return st_in, st_rest, tok_rest

    vec = lambda name, i: W[name][i].reshape(1, -1)
    emat = (lax.broadcasted_iota(jnp.int32, (h, d), 1) // HEAD_DIM == lax.broadcasted_iota(jnp.int32, (h, d), 0)).astype(F32)

    xcur = x.reshape(t, d)
    hcur = _norm_fwd(xcur, vec("norm_mix_pre", 0), name="norm_first")
    saved = []
    nxt = gather_start(0)
    for i in range(depth):
        st_in, st_rest, tok = nxt
        win_g, taps_g = _exchange_wait(st_in, [hcur, tok], name=f"gather_wait_in_{i}")
        win = _assemble_cols(win_g, name=f"assemble_w_in_{i}")
        taps_j = [_unpack(taps_g[j], tap_shapes) for j in range(N_DEV)]
        conv_a_i = jnp.concatenate([tj[0] for tj in taps_j], axis=1)
        ssm_conv_i = jnp.concatenate([tj[1] for tj in taps_j], axis=1)
        pa = _mm(hcur, win, n=3 * d, name=f"fwd_proj_a_{i}", out_dtypes=(BF16,))
        pz = _mm(hcur, win, n=d, b_off=3 * d, name=f"fwd_proj_z_{i}", out_dtypes=(BF16,))
        pxbc = _mm(hcur, win, n=xbc, b_off=4 * d, name=f"fwd_proj_xbc_{i}", out_dtypes=(BF16,))
        pdt = _mm(hcur, win, n=LANES, b_off=4 * d + xbc, name=f"fwd_proj_dt_{i}")
        ya, va = _conva_fwd(pa, conv_a_i, vec("conv_out_norm", i), seq=seq, name=f"fwd_conv_a_{i}")
        cpre = _convb_fwd(pxbc, ssm_conv_i, vec("ssm_conv_b", i), seq=seq, name=f"fwd_conv_b_{i}")
        dsk_lane = jnp.repeat(W["d_skip"][i], HEAD_DIM).reshape(1, d)
        cat, y2, hprev = _ssd_fwd(cpre, pdt, pz, ya, vec("dt_bias", i), vec("a_log", i), dsk_lane,
                                  vec("ssm_out_norm", i), emat, nseq=nseq, seq=seq, name=f"fwd_ssd_{i}")
        wout_g, wup_g, wdown_g = _exchange_wait(st_rest, [cat], name=f"gather_wait_rest_{i}")
        lw = dict(win=win, wout=wout_g.reshape(d_mix, d),
                  wup=_assemble_cols(wup_g, name=f"assemble_w_up_{i}"), wdown=wdown_g.reshape(d_ff, d),
                  conv_a=conv_a_i, ssm_conv=ssm_conv_i)
        after = []
        if i + 1 < depth:
            nxt = gather_start(i + 1, after=[wout_g])
            after = [nxt[2]]
        mix = _mm(cat, lw["wout"], name=f"fwd_out_{i}", after=after)
        x1, h2 = _resid_norm(xcur, mix, vec("norm_mix_post", i), vec("norm_mlp_pre", i), name=f"fwd_post_mix_{i}")
        f = _mm(h2, lw["wup"], name=f"fwd_up_{i}", out_dtypes=(BF16,), epi=_epi_relu2)
        dn = _mm(f, lw["wdown"], name=f"fwd_down_{i}")
        g_next = vec("norm_mix_pre", i + 1) if i + 1 < depth else vec("norm_mix_pre", 0)
        x2, hnext = _resid_norm(x1, dn, vec("norm_mlp_post", i), g_next, name=f"fwd_post_mlp_{i}")
        saved.append(dict(lw=lw, x0=xcur, h=hcur, pa=pa, pz=pz, pxbc=pxbc, pdt=pdt, va=va, cpre=cpre, y2=y2,
                          hprev=hprev, cat=cat, mix=mix, x1=x1, h2=h2, f=f, dn=dn, dsk_lane=dsk_lane))
        xcur, hcur = x2, hnext

    dx, loss_part = _loss_fwd_bwd(xcur, loss_target.reshape(t, d), name="loss")
    loss = lax.psum(loss_part[0, 0], ("x", "y", "c"))

    small_grads = {n: [None] * depth for n in SMALL}
    big_out = {n: None for n in BIG}

    def finish(pending, after):
        li, st_a, st_b = pending

        def update(n, parts):
            big_out[n] = _sum_adamw(parts, W[n], M[n], V[n], layer=li, outs=big_out[n], name=f"adamw_{n}_{li}")

        p_down, p_up = _exchange_wait(st_a, after, name=f"scatter_wait_a_{li}")
        update("w_down", p_down)
        update("w_up", p_up)
        p_out, p_in = _exchange_wait(st_b, after + [big_out["w_up"][0]], name=f"scatter_wait_b_{li}")
        update("w_out", p_out)
        update("w_in", p_in)

    pending = None
    for i in reversed(range(depth)):
        s = saved[i]
        lw = s["lw"]
        ddn, dg = _bwd_norm_out(s["dn"], vec("norm_mlp_post", i), dx, name=f"bwd_norm_mlp_post_{i}")
        small_grads["norm_mlp_post"][i] = dg
        dup = _mm(ddn, lw["wdown"], tb=True, name=f"bwd_down_dx_{i}", out_dtypes=(BF16,), epi=_epi_drelu2,
                  extras=(s["f"],))
        g_wdown = _mm(s["f"], ddn, ta=True, name=f"bwd_down_dw_{i}", out_dtypes=(BF16,))
        dh2 = _mm(dup, lw["wup"], tb=True, name=f"bwd_up_dx_{i}", out_dtypes=(BF16,))
        g_wup = _mm(s["h2"], dup, ta=True, name=f"bwd_up_dw_{i}", out_dtypes=(BF16,))
        st_a, tok_a = _exchange_start(
            [g_wdown.reshape(N_DEV, d_ff // N_DEV, d), _split_cols([g_wup], d_ff // N_DEV, name=f"split_g_w_up_{i}")],
            gather=False, name=f"scatter_start_a_{i}")
        dx1, dmix, dg_pre, dg_post = _bwd_norm_pair(s["x1"], [dh2], dx, s["mix"], vec("norm_mlp_pre", i) + tok_a[0:1, 0:1],
                                                    vec("norm_mix_post", i), name=f"bwd_norm_mix_post_{i}")
        small_grads["norm_mlp_pre"][i] = dg_pre
        small_grads["norm_mix_post"][i] = dg_post
        dcat = _mm(dmix, lw["wout"], tb=True, name=f"bwd_out_dx_{i}", out_dtypes=(BF16,))
        g_wout = _mm(s["cat"], dmix, ta=True, name=f"bwd_out_dw_{i}", out_dtypes=(BF16,))
        dconv, dz, dpdt, dgs, ddsk, ddtb, dalog = _ssd_bwd(
            s["cpre"], s["pdt"], s["pz"], s["y2"], s["hprev"], dcat, vec("dt_bias", i), vec("a_log", i),
            s["dsk_lane"], vec("ssm_out_norm", i), emat, nseq=nseq, seq=seq, name=f"bwd_ssd_{i}")
        small_grads["ssm_out_norm"][i] = dgs
        small_grads["d_skip"][i] = ddsk
        small_grads["dt_bias"][i] = ddtb
        small_grads["a_log"][i] = dalog
        dpxbc, dscw, dscb = _convb_bwd(dconv, s["pxbc"], lw["ssm_conv"], seq=seq, name=f"bwd_conv_b_{i}")
        small_grads["ssm_conv_w"][i] = dscw
        small_grads["ssm_conv_b"][i] = dscb
        dpa, dcaw, dgca = _conva_bwd(dcat, s["pa"], s["va"], lw["conv_a"], vec("conv_out_norm", i), seq=seq,
                                     name=f"bwd_conv_a_{i}")
        small_grads["conv_a_w"][i] = dcaw
        small_grads["conv_out_norm"][i] = dgca
        g_win = _split_cols([
            _mm(s["h"], dpa, ta=True, name=f"bwd_proj_a_dw_{i}", out_dtypes=(BF16,)),
            _mm(s["h"], dz, ta=True, name=f"bwd_proj_z_dw_{i}", out_dtypes=(BF16,)),
            _mm(s["h"], dpxbc, ta=True, name=f"bwd_proj_xbc_dw_{i}", out_dtypes=(BF16,)),
            _mm(s["h"], dpdt, ta=True, name=f"bwd_proj_dt_dw_{i}", out_dtypes=(BF16,))],
            in_cols // N_DEV, name=f"split_g_w_in_{i}")
        st_b, tok_b = _exchange_start(
            [g_wout.reshape(N_DEV, d_mix // N_DEV, d), g_win], gather=False, name=f"scatter_start_b_{i}")
        dh_parts = [_mm(dp, lw["win"], tb=True, b_koff=off, name=f"bwd_proj_{nm}_dx_{i}", after=[tok_b], out_dtypes=(BF16,))
                    for nm, dp, off in (("a", dpa, 0), ("z", dz, 3 * d), ("xbc", dpxbc, 4 * d), ("dt", dpdt, 4 * d + xbc))]
        dx, dg_in = _bwd_norm_in(s["x0"], dh_parts, dx1, vec("norm_mix_pre", i), name=f"bwd_norm_mix_pre_{i}")
        small_grads["norm_mix_pre"][i] = dg_in
        if pending is not None:
            finish(pending, [dx])
        pending = (i, st_a, st_b)

    grad_x = dx.reshape(nseq, seq, d)

    small_shapes_full = {n: (depth,) + tuple(small_grads[n][0].shape) for n in SMALL}
    gpack = _pack([jnp.stack(small_grads[n]) for n in SMALL])
    gparts = _exchange(gpack, gather=True, name="allreduce_small")

    def shard_of(n, full):
        if n == "conv_a_w":
            return lax.dynamic_slice_in_dim(full, me * ca_shard, ca_shard, axis=2)
        if n == "ssm_conv_w":
            return lax.dynamic_slice_in_dim(full, me * sc_shard, sc_shard, axis=2)
        return full.reshape(W[n].shape)

    gsum = _sum_parts(gparts, name="sum_small")
    gfull = _unpack(gsum, [small_shapes_full[n] for n in SMALL])
    gsmall = {n: shard_of(n, gf) for n, gf in zip(SMALL, gfull)}
    res = _sum_adamw(_pack([gsmall[n] for n in SMALL])[None], _pack([W[n] for n in SMALL]),
                     _pack([M[n] for n in SMALL]), _pack([V[n] for n in SMALL]), name="adamw_small")
    small_out = [dict(zip(SMALL, _unpack(r, [W[n].shape for n in SMALL]))) for r in res]
    finish(pending, [dx, res[0]])

    def out_of(kind, n):
        return big_out[n][kind] if n in BIG else small_out[kind][n]

    return (loss, grad_x, *[out_of(k, n) for k in range(4) for n in ORDER])
```

```python
import functools

import jax
import jax.numpy as jnp
from jax import lax
from jax.experimental import pallas as pl
from jax.experimental.pallas import tpu as pltpu

F32 = jnp.float32
BF16 = jnp.bfloat16
HIGHEST = lax.Precision.HIGHEST
MESH = pl.DeviceIdType.MESH

EPS = 1e-6
HEAD_DIM = 64
D_STATE = 128
SSM_GROUPS = 2
CHUNK = 128
CONV_K = 3
SSM_CONV_K = 4
ADAM_LR = 0.001
ADAM_B1 = 0.9
ADAM_B2 = 0.999
ADAM_EPS = 1e-08
ADAM_WD = 0.01
ADAM_STEP = 10

N_DEV = 8
LANES = 128
SUBLANES = 8
VMEM_LIMIT = 48 * 1024 * 1024
ROW_TILE = 512
MM_TILE = 1024
MM_TILE_N = 1536


def _params(sem):
    return pltpu.CompilerParams(dimension_semantics=sem, vmem_limit_bytes=VMEM_LIMIT)


def _call(body, **kw):
    return pl.pallas_call(body, **kw)


def _pick(n, cap):
    best = None
    for t in range(LANES, min(n, cap) + 1, LANES):
        if n % t == 0:
            best = t
    return best or n


def _pick_rows(n, cap):
    best = None
    for t in range(SUBLANES, min(n, cap) + 1, SUBLANES):
        if n % t == 0:
            best = t
    return best or n


def _sigmoid(x):
    return 1.0 / (1.0 + jnp.exp(-x))


def _softplus(x):
    return jnp.maximum(x, 0.0) + jnp.log1p(jnp.exp(-jnp.abs(x)))


def _rms(x):
    return lax.rsqrt(jnp.mean(x * x, axis=-1, keepdims=True) + EPS)


def _rms_bwd(x, r, g, dy):
    gy = dy * g
    dx = r * gy - x * (r * r * r) * jnp.mean(gy * x, axis=-1, keepdims=True)
    return dx, dy * x * r


def _full(shape):
    return pl.BlockSpec(shape, lambda *_: (0,) * len(shape))


def _mm(a, b, *, name, ta=False, tb=False, out_dtypes=(F32,), epi=None, extras=(), n=None, b_off=0, b_koff=0,
        after=()):
    m, k = (a.shape[1], a.shape[0]) if ta else a.shape
    if n is None:
        n = b.shape[0] if tb else b.shape[1]
    tm, tn, tk = _pick(m, MM_TILE), _pick(n, MM_TILE_N), _pick(k, MM_TILE)
    while b_off % tn or n % tn:
        tn -= LANES
    while b_koff % tk or k % tk:
        tk -= LANES
    nk = k // tk
    nm, nn = m // tm, n // tn
    jo = b_off // tn
    ko = b_koff // tk
    a_bytes = m * k * a.dtype.itemsize
    b_bytes = n * k * b.dtype.itemsize
    m_outer = a_bytes + nm * b_bytes <= b_bytes + nn * a_bytes
    ij = (lambda g0, g1: (g0, g1)) if m_outer else (lambda g0, g1: (g1, g0))
    grid = (nm, nn, nk) if m_outer else (nn, nm, nk)

    def a_map(g0, g1, kk):
        i, _ = ij(g0, g1)
        return (kk, i) if ta else (i, kk)

    def b_map(g0, g1, kk):
        _, j = ij(g0, g1)
        return (j + jo, kk + ko) if tb else (kk + ko, j + jo)

    def o_map(g0, g1, kk):
        return ij(g0, g1)

    a_spec = pl.BlockSpec((tk, tm) if ta else (tm, tk), a_map)
    b_spec = pl.BlockSpec((tn, tk) if tb else (tk, tn), b_map)
    o_spec = pl.BlockSpec((tm, tn), o_map)
    dims = (((0 if ta else 1,), (1 if tb else 0,)), ((), ()))
    n_ex = len(extras)
    after = list(after)
    o0 = 2 + n_ex + len(after)

    def finish(acc, ex, outs):
        res = (acc,) if epi is None else epi(acc, *[e[...] for e in ex])
        for o, r in zip(outs, res):
            o[...] = r.astype(o.dtype)

    def body_single(*refs):
        a_ref, b_ref = refs[:2]
        acc = lax.dot_general(a_ref[...].astype(BF16), b_ref[...].astype(BF16), dims, preferred_element_type=F32)
        finish(acc, refs[2:2 + n_ex], refs[o0:])

    def body_multi(*refs):
        a_ref, b_ref = refs[:2]
        acc = refs[-1]
        kk = pl.program_id(2)

        @pl.when(kk == 0)
        def _():
            acc[...] = jnp.zeros_like(acc)

        acc[...] += lax.dot_general(a_ref[...].astype(BF16), b_ref[...].astype(BF16), dims, preferred_element_type=F32)

        @pl.when(kk == nk - 1)
        def _():
            finish(acc[...], refs[2:2 + n_ex], refs[o0:-1])

    outs = _call(
        body_single if nk == 1 else body_multi, name=name, grid=grid,
        in_specs=[a_spec, b_spec] + [o_spec] * n_ex + [pl.BlockSpec(memory_space=pl.ANY)] * len(after),
        out_specs=[o_spec] * len(out_dtypes),
        out_shape=[jax.ShapeDtypeStruct((m, n), dt) for dt in out_dtypes],
        scratch_shapes=[] if nk == 1 else [pltpu.VMEM((tm, tn), F32)],
        compiler_params=_params(("parallel", "parallel", "arbitrary")),
    )(a, b, *extras, *after)
    return outs[0] if len(out_dtypes) == 1 else outs


def _epi_relu2(acc):
    r = jnp.maximum(acc, 0.0)
    return (r * r,)


def _epi_drelu2(acc, f):
    return (acc * (2.0 * jnp.sqrt(f.astype(F32))),)


def _norm_fwd(x, g, *, name):
    t, d = x.shape
    tt = _pick_rows(t, ROW_TILE)

    def body(x_ref, g_ref, h_ref):
        xv = x_ref[...]
        h_ref[...] = (xv * _rms(xv) * g_ref[...]).astype(BF16)

    row = pl.BlockSpec((tt, d), lambda i: (i, 0))
    return _call(body, name=name, grid=(t // tt,), in_specs=[row, _full((1, d))], out_specs=row,
                 out_shape=jax.ShapeDtypeStruct((t, d), BF16), compiler_params=_params(("parallel",)))(x, g)


def _resid_norm(x, n, g1, g2, *, name):
    t, d = x.shape
    tt = _pick_rows(t, ROW_TILE)

    def body(x_ref, n_ref, g1_ref, g2_ref, xo_ref, h_ref):
        nv = n_ref[...]
        xn = x_ref[...] + nv * _rms(nv) * g1_ref[...]
        xo_ref[...] = xn
        h_ref[...] = (xn * _rms(xn) * g2_ref[...]).astype(BF16)

    row = pl.BlockSpec((tt, d), lambda i: (i, 0))
    return _call(body, name=name, grid=(t // tt,), in_specs=[row, row, _full((1, d)), _full((1, d))],
                 out_specs=[row, row],
                 out_shape=[jax.ShapeDtypeStruct((t, d), F32), jax.ShapeDtypeStruct((t, d), BF16)],
                 compiler_params=_params(("parallel",)))(x, n, g1, g2)


def _loss_fwd_bwd(xf, target, *, name):
    t, d = xf.shape
    tt = _pick_rows(t, ROW_TILE)
    nt = t // tt

    def body(x_ref, t_ref, dy_ref, loss_ref, acc):
        i = pl.program_id(0)

        @pl.when(i == 0)
        def _():
            acc[...] = jnp.zeros_like(acc)

        e = x_ref[...] - t_ref[...]
        dy_ref[...] = e * (1.0 / d)
        acc[...] += jnp.sum(e * e, axis=0, keepdims=True)

        @pl.when(i == nt - 1)
        def _():
            loss_ref[...] = jnp.sum(acc[...], axis=-1, keepdims=True) * (0.5 / d)

    row = pl.BlockSpec((tt, d), lambda i: (i, 0))
    return _call(body, name=name, grid=(nt,), in_specs=[row, row], out_specs=[row, _full((1, 1))],
                 out_shape=[jax.ShapeDtypeStruct((t, d), F32), jax.ShapeDtypeStruct((1, 1), F32)],
                 scratch_shapes=[pltpu.VMEM((1, d), F32)], compiler_params=_params(("arbitrary",)))(xf, target)


def _bwd_norm_pair(xin, dh, dres, n, g_in, g_out, *, name):
    t, d = xin.shape
    tt = _pick_rows(t, ROW_TILE)
    n_dh = len(dh)

    def body(*refs):
        x_ref = refs[0]
        dh_refs = refs[1:1 + n_dh]
        dres_ref, n_ref, gi_ref, go_ref, dx_ref, dn_ref, dgi_ref, dgo_ref = refs[1 + n_dh:]
        i = pl.program_id(0)

        @pl.when(i == 0)
        def _():
            dgi_ref[...] = jnp.zeros_like(dgi_ref)
            dgo_ref[...] = jnp.zeros_like(dgo_ref)

        xv = x_ref[...]
        dhv = dh_refs[0][...].astype(F32)
        for r in dh_refs[1:]:
            dhv = dhv + r[...].astype(F32)
        dxh, dgi = _rms_bwd(xv, _rms(xv), gi_ref[...], dhv)
        dx = dres_ref[...] + dxh
        dx_ref[...] = dx
        dgi_ref[...] += jnp.sum(dgi, axis=0, keepdims=True)
        nv = n_ref[...]
        dn, dgo = _rms_bwd(nv, _rms(nv), go_ref[...], dx)
        dn_ref[...] = dn.astype(BF16)
        dgo_ref[...] += jnp.sum(dgo, axis=0, keepdims=True)

    row = pl.BlockSpec((tt, d), lambda i: (i, 0))
    vec = _full((1, d))
    return _call(body, name=name, grid=(t // tt,), in_specs=[row] * (n_dh + 3) + [vec, vec],
                 out_specs=[row, row, vec, vec],
                 out_shape=[jax.ShapeDtypeStruct((t, d), F32), jax.ShapeDtypeStruct((t, d), BF16),
                            jax.ShapeDtypeStruct((1, d), F32), jax.ShapeDtypeStruct((1, d), F32)],
                 compiler_params=_params(("arbitrary",)))(xin, *dh, dres, n, g_in, g_out)


def _bwd_norm_in(xin, dh, dres, g_in, *, name):
    t, d = xin.shape
    tt = _pick_rows(t, ROW_TILE)
    n_dh = len(dh)

    def body(*refs):
        x_ref = refs[0]
        dh_refs = refs[1:1 + n_dh]
        dres_ref, gi_ref, dx_ref, dgi_ref = refs[1 + n_dh:]
        i = pl.program_id(0)

        @pl.when(i == 0)
        def _():
            dgi_ref[...] = jnp.zeros_like(dgi_ref)

        xv = x_ref[...]
        dhv = dh_refs[0][...].astype(F32)
        for r in dh_refs[1:]:
            dhv = dhv + r[...].astype(F32)
        dxh, dgi = _rms_bwd(xv, _rms(xv), gi_ref[...], dhv)
        dx_ref[...] = dres_ref[...] + dxh
        dgi_ref[...] += jnp.sum(dgi, axis=0, keepdims=True)

    row = pl.BlockSpec((tt, d), lambda i: (i, 0))
    vec = _full((1, d))
    return _call(body, name=name, grid=(t // tt,), in_specs=[row] * (n_dh + 2) + [vec],
                 out_specs=[row, vec],
                 out_shape=[jax.ShapeDtypeStruct((t, d), F32), jax.ShapeDtypeStruct((1, d), F32)],
                 compiler_params=_params(("arbitrary",)))(xin, *dh, dres, g_in)


def _bwd_norm_out(n, g_out, dx, *, name):
    t, d = n.shape
    tt = _pick_rows(t, ROW_TILE)

    def body(n_ref, go_ref, dx_ref, dn_ref, dgo_ref):
        i = pl.program_id(0)

        @pl.when(i == 0)
        def _():
            dgo_ref[...] = jnp.zeros_like(dgo_ref)

        nv = n_ref[...]
        dn, dgo = _rms_bwd(nv, _rms(nv), go_ref[...], dx_ref[...])
        dn_ref[...] = dn.astype(BF16)
        dgo_ref[...] += jnp.sum(dgo, axis=0, keepdims=True)

    row = pl.BlockSpec((tt, d), lambda i: (i, 0))
    vec = _full((1, d))
    return _call(body, name=name, grid=(t // tt,), in_specs=[row, vec, row], out_specs=[row, vec],
                 out_shape=[jax.ShapeDtypeStruct((t, d), BF16), jax.ShapeDtypeStruct((1, d), F32)],
                 compiler_params=_params(("arbitrary",)))(n, g_out, dx)


def _shift_down(cur, halo, s):
    return jnp.concatenate([halo[SUBLANES - s:], cur[:cur.shape[0] - s]], axis=0)


def _shift_up(cur, halo, s):
    return jnp.concatenate([cur[s:], halo[:s]], axis=0)


def _conva_fwd(pa, w, g, *, d, seq, name):
    t = pa.shape[0]
    tt = _pick_rows(seq, ROW_TILE)
    tps = seq // tt

    def body(xa_ref, ca_ref, ba_ref, w_ref, g_ref, ya_ref, v_ref, carry):
        i = pl.program_id(0)

        @pl.when(i % tps == 0)
        def _():
            carry[...] = jnp.zeros_like(carry)

        u = ca_ref[...].astype(F32) * xa_ref[...].astype(F32)
        halo = carry[...]
        wv = w_ref[...]
        v = wv[2:3] * u + wv[1:2] * _shift_down(u, halo, 1) + wv[0:1] * _shift_down(u, halo, 2)
        carry[...] = u[tt - SUBLANES:]
        yp = ba_ref[...].astype(F32) * v
        ya_ref[...] = (yp * _rms(yp) * g_ref[...]).astype(BF16)
        v_ref[...] = v.astype(BF16)

    col = lambda c: pl.BlockSpec((tt, d), lambda i, c=c: (i, c))
    row = pl.BlockSpec((tt, d), lambda i: (i, 0))
    return _call(body, name=name, grid=(t // tt,),
                 in_specs=[col(0), col(1), col(2), _full((CONV_K, d)), _full((1, d))], out_specs=[row, row],
                 out_shape=[jax.ShapeDtypeStruct((t, d), BF16), jax.ShapeDtypeStruct((t, d), BF16)],
                 scratch_shapes=[pltpu.VMEM((SUBLANES, d), F32)],
                 compiler_params=_params(("arbitrary",)))(pa, pa, pa, w, g)


def _conva_bwd(dcat, pa, v, w, g, *, d, seq, name):
    t, width = pa.shape
    d3 = 3 * d
    tt = _pick_rows(seq, ROW_TILE)
    tps = seq // tt
    nt = t // tt

    def body(dya_ref, xa_ref, ca_ref, ba_ref, v_ref, w_ref, g_ref, dpa_ref, dw_ref, dg_ref, carry):
        i = pl.program_id(0)

        @pl.when(i == 0)
        def _():
            dw_ref[...] = jnp.zeros_like(dw_ref)
            dg_ref[...] = jnp.zeros_like(dg_ref)

        @pl.when(i % tps == 0)
        def _():
            carry[...] = jnp.zeros_like(carry)

        xa, ca, ba, vv = [r[...].astype(F32) for r in (xa_ref, ca_ref, ba_ref, v_ref)]
        yp = ba * vv
        dyp, dgt = _rms_bwd(yp, _rms(yp), g_ref[...], dya_ref[...].astype(F32))
        dg_ref[...] += jnp.sum(dgt, axis=0, keepdims=True)
        dv = dyp * ba
        halo = carry[...]
        dv1 = _shift_up(dv, halo, 1)
        dv2 = _shift_up(dv, halo, 2)
        carry[...] = dv[:SUBLANES]
        wv = w_ref[...]
        du = wv[2:3] * dv + wv[1:2] * dv1 + wv[0:1] * dv2
        u = ca * xa
        dw_ref[0:1, :] += jnp.sum(u * dv2, axis=0, keepdims=True)
        dw_ref[1:2, :] += jnp.sum(u * dv1, axis=0, keepdims=True)
        dw_ref[2:3, :] += jnp.sum(u * dv, axis=0, keepdims=True)
        dpa_ref[:, 0:d] = (du * ca).astype(BF16)
        dpa_ref[:, d:2 * d] = (du * xa).astype(BF16)
        dpa_ref[:, 2 * d:3 * d] = (dyp * vv).astype(BF16)

    rcol = lambda c: pl.BlockSpec((tt, d), lambda i, c=c: (nt - 1 - i, c))
    return _call(body, name=name, grid=(nt,),
                 in_specs=[rcol(0), rcol(0), rcol(1), rcol(2), rcol(0), _full((CONV_K, d)), _full((1, d))],
                 out_specs=[pl.BlockSpec((tt, d3), lambda i: (nt - 1 - i, 0)), _full((CONV_K, d)), _full((1, d))],
                 out_shape=[jax.ShapeDtypeStruct((t, width), BF16), jax.ShapeDtypeStruct((CONV_K, d), F32),
                            jax.ShapeDtypeStruct((1, d), F32)],
                 scratch_shapes=[pltpu.VMEM((SUBLANES, d), F32)],
                 compiler_params=_params(("arbitrary",)))(dcat, pa, pa, pa, v, w, g)


CONV_CH = 512


def _convb_fwd(proj, w, bias, *, col0, seq, name):
    t = proj.shape[0]
    c = w.shape[1]
    cb = _pick(c, CONV_CH)
    assert col0 % cb == 0
    tt = _pick_rows(seq, ROW_TILE)
    tps = seq // tt

    def body(p_ref, w_ref, b_ref, o_ref, carry):
        i = pl.program_id(1)

        @pl.when(i % tps == 0)
        def _():
            carry[...] = jnp.zeros_like(carry)

        p = p_ref[...].astype(F32)
        halo = carry[...]
        wv = w_ref[...]
        o = wv[3:4] * p + b_ref[...]
        for s in (1, 2, 3):
            o = o + wv[3 - s:4 - s] * _shift_down(p, halo, s)
        carry[...] = p[tt - SUBLANES:]
        o_ref[...] = o.astype(BF16)

    return _call(body, name=name, grid=(c // cb, t // tt),
                 in_specs=[pl.BlockSpec((tt, cb), lambda jc, i: (i, col0 // cb + jc)),
                           pl.BlockSpec((SSM_CONV_K, cb), lambda jc, i: (0, jc)), pl.BlockSpec((1, cb), lambda jc, i: (0, jc))],
                 out_specs=pl.BlockSpec((tt, cb), lambda jc, i: (i, jc)), out_shape=jax.ShapeDtypeStruct((t, c), BF16),
                 scratch_shapes=[pltpu.VMEM((SUBLANES, cb), F32)],
                 compiler_params=_params(("arbitrary", "arbitrary")))(proj, w, bias)


def _convb_bwd(dconv, proj, w, dproj, *, col0, seq, name):
    t, c = dconv.shape
    cb = _pick(c, CONV_CH)
    assert col0 % cb == 0
    tt = _pick_rows(seq, ROW_TILE)
    tps = seq // tt
    nt = t // tt

    def body(dc_ref, p_ref, w_ref, dproj_in, dp_ref, dw_ref, db_ref, carry):
        del dproj_in
        i = pl.program_id(1)

        @pl.when(i == 0)
        def _():
            dw_ref[...] = jnp.zeros_like(dw_ref)
            db_ref[...] = jnp.zeros_like(db_ref)

        @pl.when(i % tps == 0)
        def _():
            carry[...] = jnp.zeros_like(carry)

        dc = dc_ref[...].astype(F32)
        p = p_ref[...].astype(F32)
        halo = carry[...]
        wv = w_ref[...]
        dp = wv[3:4] * dc
        dw_ref[3:4, :] += jnp.sum(p * dc, axis=0, keepdims=True)
        for s in (1, 2, 3):
            dcs = _shift_up(dc, halo, s)
            dp = dp + wv[3 - s:4 - s] * dcs
            dw_ref[3 - s:4 - s, :] += jnp.sum(p * dcs, axis=0, keepdims=True)
        carry[...] = dc[:SUBLANES]
        db_ref[...] += jnp.sum(dc, axis=0, keepdims=True)
        dp_ref[...] = dp.astype(BF16)

    win_spec = pl.BlockSpec((tt, cb), lambda jc, i: (nt - 1 - i, col0 // cb + jc))
    taps = pl.BlockSpec((SSM_CONV_K, cb), lambda jc, i: (0, jc))
    return _call(body, name=name, grid=(c // cb, nt),
                 in_specs=[pl.BlockSpec((tt, cb), lambda jc, i: (nt - 1 - i, jc)), win_spec, taps,
                           pl.BlockSpec(memory_space=pl.ANY)],
                 out_specs=[win_spec, taps, pl.BlockSpec((1, cb), lambda jc, i: (0, jc))],
                 out_shape=[jax.ShapeDtypeStruct(dproj.shape, BF16), jax.ShapeDtypeStruct((SSM_CONV_K, c), F32),
                            jax.ShapeDtypeStruct((1, c), F32)],
                 input_output_aliases={3: 0},
                 scratch_shapes=[pltpu.VMEM((SUBLANES, cb), F32)],
                 compiler_params=_params(("arbitrary", "arbitrary")))(dconv, proj, w, dproj)


def _expand_heads(x, ev):
    return jnp.dot(x, ev, precision=HIGHEST, preferred_element_type=F32)


def _head_sums(v, ev):
    return lax.dot_general(v, ev, (((1,), (1,)), ((), ())), precision=HIGHEST, preferred_element_type=F32)


def _ssd_common(c_ref, pdt_ref, dtb_ref, alog_ref, e_ref, h):
    cp = c_ref[...].astype(F32)
    sg = _sigmoid(cp)
    act = cp * sg
    pre = pdt_ref[:, 0:h] + dtb_ref[...]
    dt = _softplus(pre)
    a = -jnp.exp(alog_ref[...])
    adt = dt * a
    row = lax.broadcasted_iota(jnp.int32, (CHUNK, CHUNK), 0)
    col = lax.broadcasted_iota(jnp.int32, (CHUNK, CHUNK), 1)
    tril = row >= col
    cs = jnp.dot(tril.astype(F32), adt, precision=HIGHEST, preferred_element_type=F32)
    ev = e_ref[...]
    dt_l = _expand_heads(dt, ev)
    ecs_l = jnp.exp(_expand_heads(cs, ev))
    return dict(cp=cp, sg=sg, act=act, pre=pre, dt=dt, a=a, cs=cs, dt_l=dt_l, ecs_l=ecs_l,
                tril=tril, row=row, col=col, lo=col < HEAD_DIM)


def _dot_nt(a, b):
    return lax.dot_general(a, b, (((1,), (1,)), ((), ())), preferred_element_type=F32)


def _dot_tn(a, b):
    return lax.dot_general(a, b, (((0,), (0,)), ((), ())), preferred_element_type=F32)


def _dot(a, b):
    return jnp.dot(a, b, preferred_element_type=F32)


def _ssd_fwd(cpre, pdt, pz, ya, dtb, alog, dsk_lane, gs, emat, *, nseq, seq, name):
    t, xbc = cpre.shape
    d = ya.shape[1]
    h = d // HEAD_DIM
    npair = h // 2
    ppg = npair // SSM_GROUPS
    nc = seq // CHUNK
    gw = d // SSM_GROUPS
    bc0 = d
    cc0 = d + SSM_GROUPS * D_STATE

    def body(c_ref, pdt_ref, z_ref, ya_ref, dtb_ref, alog_ref, dsk_ref, gs_ref, e_ref, cat_ref, y2_ref, hp_ref, h_ref):
        c = pl.program_id(1)

        @pl.when(c == 0)
        def _():
            h_ref[...] = jnp.zeros_like(h_ref)

        q = _ssd_common(c_ref, pdt_ref, dtb_ref, alog_ref, e_ref, h)
        act, cs, lo, ecs_l = q["act"], q["cs"], q["lo"], q["ecs_l"]
        xs = act[:, :d]
        xd = xs * q["dt_l"]
        ys = []
        for g in range(SSM_GROUPS):
            bg = act[:, bc0 + g * D_STATE: bc0 + (g + 1) * D_STATE]
            cgb = act[:, cc0 + g * D_STATE: cc0 + (g + 1) * D_STATE].astype(BF16)
            s = _dot_nt(cgb, bg.astype(BF16))
            for jj in range(ppg):
                j = g * ppg + jj
                sl = slice(LANES * j, LANES * (j + 1))
                xdj = xd[:, sl]
                x2 = jnp.concatenate([jnp.where(lo, xdj, 0.0), jnp.where(lo, 0.0, xdj)], axis=0).astype(BF16)
                hprev = h_ref[j]
                hp_ref[j] = hprev.astype(BF16)
                ms, bws = [], []
                for hh in (2 * j, 2 * j + 1):
                    csc = cs[:, hh:hh + 1]
                    csb = jnp.broadcast_to(csc, (CHUNK, CHUNK))
                    ms.append(s * jnp.exp(jnp.where(q["tril"], csb - csb.T, -jnp.inf)))
                    bws.append(bg * jnp.exp(cs[CHUNK - 1:CHUNK, hh:hh + 1] - csc))
                ydiag = _dot(jnp.concatenate(ms, axis=1).astype(BF16), x2)
                st = _dot_tn(jnp.concatenate(bws, axis=0).astype(BF16), x2)
                ecs = ecs_l[:, sl]
                yoff = _dot(cgb, hprev.astype(BF16)) * ecs
                h_ref[j] = hprev * ecs[CHUNK - 1:CHUNK] + st
                ys.append(ydiag + yoff)
        y = jnp.concatenate(ys, axis=1) + dsk_ref[...] * xs
        y2_ref[...] = y.astype(BF16)
        zv = z_ref[...].astype(F32)
        y3 = y * (zv * _sigmoid(zv))
        cat_ref[:, 0:d] = ya_ref[...]
        for gi in range(SSM_GROUPS):
            seg = y3[:, gi * gw:(gi + 1) * gw]
            cat_ref[:, d + gi * gw:d + (gi + 1) * gw] = (seg * _rms(seg) * gs_ref[:, gi * gw:(gi + 1) * gw]).astype(BF16)

    chunk = lambda w, cb=0: pl.BlockSpec((CHUNK, w), lambda b, c, cb=cb: (b * nc + c, cb))
    vec = lambda w: pl.BlockSpec((1, w), lambda b, c: (0, 0))
    hp_spec = pl.BlockSpec((None, None, npair, D_STATE, LANES), lambda b, c: (b, c, 0, 0, 0))
    return _call(body, name=name, grid=(nseq, nc),
                 in_specs=[chunk(xbc), chunk(LANES), chunk(d, 3), chunk(d), vec(h), vec(h), vec(d), vec(d),
                           pl.BlockSpec((h, d), lambda b, c: (0, 0))],
                 out_specs=[chunk(2 * d), chunk(d), hp_spec],
                 out_shape=[jax.ShapeDtypeStruct((t, 2 * d), BF16), jax.ShapeDtypeStruct((t, d), BF16),
                            jax.ShapeDtypeStruct((nseq, nc, npair, D_STATE, LANES), BF16)],
                 scratch_shapes=[pltpu.VMEM((npair, D_STATE, LANES), F32)],
                 compiler_params=_params(("arbitrary", "arbitrary")))(cpre, pdt, pz, ya, dtb, alog, dsk_lane, gs, emat)


def _ssd_bwd(cpre, pdt, pz, y2, hprev_all, dcat, dtb, alog, dsk_lane, gs, emat, dproj, *, nseq, seq, name):
    t, xbc = cpre.shape
    d = y2.shape[1]
    h = d // HEAD_DIM
    npair = h // 2
    ppg = npair // SSM_GROUPS
    nc = seq // CHUNK
    gw = d // SSM_GROUPS
    bc0 = d
    cc0 = d + SSM_GROUPS * D_STATE

    def body(c_ref, pdt_ref, z_ref, y2_ref, hp_ref, dys_ref, dtb_ref, alog_ref, dsk_ref, gs_ref, e_ref, dproj_in,
             dconv_ref, dz_ref, dpdt_ref, dgs_ref, ddsk_ref, ddtb_ref, dalog_ref, dh_ref):
        del dproj_in
        b = pl.program_id(0)
        c = pl.program_id(1)

        @pl.when(c == 0)
        def _():
            dh_ref[...] = jnp.zeros_like(dh_ref)

        @pl.when((b == 0) & (c == 0))
        def _():
            dgs_ref[...] = jnp.zeros_like(dgs_ref)
            ddsk_ref[...] = jnp.zeros_like(ddsk_ref)
            ddtb_ref[...] = jnp.zeros_like(ddtb_ref)
            dalog_ref[...] = jnp.zeros_like(dalog_ref)

        q = _ssd_common(c_ref, pdt_ref, dtb_ref, alog_ref, e_ref, h)
        cp, sg, act, cs, a, dt, lo = q["cp"], q["sg"], q["act"], q["cs"], q["a"], q["dt"], q["lo"]
        ecs_l, dt_l = q["ecs_l"], q["dt_l"]
        ev = e_ref[...]
        xs = act[:, :d]
        xd = xs * dt_l
        row16 = lax.broadcasted_iota(jnp.int32, (CHUNK, h), 0)
        hid = lax.broadcasted_iota(jnp.int32, (1, h), 1)

        zv = z_ref[...].astype(F32)
        sz = _sigmoid(zv)
        siluz = zv * sz
        y2v = y2_ref[...].astype(F32)
        y3 = y2v * siluz
        dysv = dys_ref[...].astype(F32)
        dy3s = []
        for gi in range(SSM_GROUPS):
            gsl = slice(gi * gw, (gi + 1) * gw)
            seg = y3[:, gsl]
            dseg, dgt = _rms_bwd(seg, _rms(seg), gs_ref[:, gsl], dysv[:, gsl])
            dy3s.append(dseg)
            dgs_ref[:, gsl] += jnp.sum(dgt, axis=0, keepdims=True)
        dy3 = jnp.concatenate(dy3s, axis=1)
        dy = dy3 * siluz
        dz_ref[...] = (dy3 * y2v * (sz * (1.0 + zv * (1.0 - sz)))).astype(BF16)
        ddsk_ref[...] += jnp.sum(_head_sums(dy * xs, ev), axis=0, keepdims=True)

        dcs = jnp.zeros((CHUNK, h), F32)
        dxd_parts, yoff_parts, db_parts, dc_parts = [], [], [], []
        for g in range(SSM_GROUPS):
            bg = act[:, bc0 + g * D_STATE: bc0 + (g + 1) * D_STATE]
            cg = act[:, cc0 + g * D_STATE: cc0 + (g + 1) * D_STATE]
            bgb, cgb = bg.astype(BF16), cg.astype(BF16)
            s = _dot_nt(cgb, bgb)
            ds = jnp.zeros((CHUNK, CHUNK), F32)
            dbg = jnp.zeros((CHUNK, D_STATE), F32)
            dcg = jnp.zeros((CHUNK, D_STATE), F32)
            for jj in range(ppg):
                j = g * ppg + jj
                sl = slice(LANES * j, LANES * (j + 1))
                xdj = xd[:, sl]
                xdb = xdj.astype(BF16)
                x2 = jnp.concatenate([jnp.where(lo, xdj, 0.0), jnp.where(lo, 0.0, xdj)], axis=0).astype(BF16)
                dyj = dy[:, sl]
                dy2 = jnp.concatenate([jnp.where(lo, dyj, 0.0), jnp.where(lo, 0.0, dyj)], axis=0).astype(BF16)
                hpb = hp_ref[j]
                hprev = hpb.astype(F32)
                dhn = dh_ref[j]
                dhb = dhn.astype(BF16)
                dh2 = jnp.concatenate([jnp.where(lo, dhn, 0.0), jnp.where(lo, 0.0, dhn)], axis=0).astype(BF16)
                ecs = ecs_l[:, sl]
                gmat = (dyj * ecs).astype(BF16)
                yoff_parts.append(_dot(cgb, hpb) * ecs)
                dcg = dcg + _dot_nt(gmat, hpb)
                dh_ref[j] = dhn * ecs[CHUNK - 1:CHUNK] + _dot_tn(cgb, gmat)
                t2 = dhn * hprev
                dbw2 = _dot_nt(x2, dhb)
                dm2 = _dot_nt(dy2, xdb)
                ms, bws = [], []
                for idx, hh in enumerate((2 * j, 2 * j + 1)):
                    msk = lo if idx == 0 else jnp.logical_not(lo)
                    onehot = (hid == hh).astype(F32)
                    csc = cs[:, hh:hh + 1]
                    csb = jnp.broadcast_to(csc, (CHUNK, CHUNK))
                    lm = jnp.exp(jnp.where(q["tril"], csb - csb.T, -jnp.inf))
                    m = s * lm
                    cs_last = cs[CHUNK - 1:CHUNK, hh:hh + 1]
                    dte = jnp.exp(cs_last - csc)
                    ms.append(m)
                    bws.append(bg * dte)
                    dbw = dbw2[idx * CHUNK:(idx + 1) * CHUNK]
                    dbg = dbg + dbw * dte
                    qv = jnp.sum(dbw * bg, axis=-1, keepdims=True) * dte
                    dm = dm2[idx * CHUNK:(idx + 1) * CHUNK]
                    wm = dm * m
                    rc = jnp.sum(wm - wm.T, axis=-1, keepdims=True)
                    ds = ds + dm * lm
                    ddec = jnp.sum(jnp.where(msk, t2, 0.0)) * jnp.exp(cs_last)
                    last = jnp.sum(qv) + ddec
                    dcs = dcs + (rc - qv) * onehot + jnp.where(row16 == CHUNK - 1, last * onehot, 0.0)
                dxd_s = _dot(jnp.concatenate(bws, axis=1).astype(BF16), dh2)
                dxd_d = _dot_tn(jnp.concatenate(ms, axis=0).astype(BF16), dy2)
                dxd_parts.append(dxd_s + dxd_d)
            dsb = ds.astype(BF16)
            dc_parts.append(dcg + _dot(dsb, bgb))
            db_parts.append(dbg + _dot_tn(dsb, cgb))
        yoff_all = jnp.concatenate(yoff_parts, axis=1)
        dxd_all = jnp.concatenate(dxd_parts, axis=1)
        dcs = dcs + _head_sums(dy * yoff_all, ev)
        triu = (q["col"] >= q["row"]).astype(F32)
        dadt = jnp.dot(triu, dcs, precision=HIGHEST, preferred_element_type=F32)
        ddt = dadt * a + _head_sums(dxd_all * xs, ev)
        dalog_ref[...] += jnp.sum(dadt * dt, axis=0, keepdims=True) * a
        dpre = ddt * _sigmoid(q["pre"])
        ddtb_ref[...] += jnp.sum(dpre, axis=0, keepdims=True)
        dpdt_ref[...] = jnp.zeros_like(dpdt_ref)
        dpdt_ref[:, 0:h] = dpre.astype(BF16)
        dxs = dxd_all * dt_l + dy * dsk_ref[...]
        dact = jnp.concatenate([dxs] + db_parts + dc_parts, axis=1)
        dconv_ref[...] = (dact * (sg * (1.0 + cp * (1.0 - sg)))).astype(BF16)

    rchunk = lambda w, cb=0: pl.BlockSpec((CHUNK, w), lambda b, c, cb=cb: (b * nc + nc - 1 - c, cb))
    vec = lambda w: pl.BlockSpec((1, w), lambda b, c: (0, 0))
    hp_spec = pl.BlockSpec((None, None, npair, D_STATE, LANES), lambda b, c: (b, nc - 1 - c, 0, 0, 0))
    return _call(body, name=name, grid=(nseq, nc),
                 in_specs=[rchunk(xbc), rchunk(LANES), rchunk(d, 3), rchunk(d), hp_spec, rchunk(d, 1),
                           vec(h), vec(h), vec(d), vec(d), pl.BlockSpec((h, d), lambda b, c: (0, 0)),
                           pl.BlockSpec(memory_space=pl.ANY)],
                 out_specs=[rchunk(xbc), rchunk(d, 3), rchunk(LANES), vec(d), vec(h), vec(h), vec(h)],
                 out_shape=[jax.ShapeDtypeStruct((t, xbc), BF16), jax.ShapeDtypeStruct(dproj.shape, BF16),
                            jax.ShapeDtypeStruct((t, LANES), BF16), jax.ShapeDtypeStruct((1, d), F32),
                            jax.ShapeDtypeStruct((1, h), F32), jax.ShapeDtypeStruct((1, h), F32),
                            jax.ShapeDtypeStruct((1, h), F32)],
                 input_output_aliases={11: 1},
                 scratch_shapes=[pltpu.VMEM((npair, D_STATE, LANES), F32)],
                 compiler_params=_params(("arbitrary", "arbitrary")))(
                     cpre, pdt, pz, y2, hprev_all, dcat, dtb, alog, dsk_lane, gs, emat, dproj)


def _sum_adamw(parts, w, m, v, *, name, layer=None, outs=None):
    n, r, c = parts.shape
    tr = _pick_rows(r, 256)
    bc1 = 1.0 - ADAM_B1 ** ADAM_STEP
    bc2 = 1.0 - ADAM_B2 ** ADAM_STEP

    def body(p_ref, w_ref, m_ref, v_ref, *rest):
        g_ref, d_ref, mo_ref, vo_ref = rest[-4:]
        g = p_ref[0].astype(F32)
        for k in range(1, n):
            g = g + p_ref[k].astype(F32)
        mn = ADAM_B1 * m_ref[...] + (1.0 - ADAM_B1) * g
        vn = ADAM_B2 * v_ref[...] + (1.0 - ADAM_B2) * (g * g)
        g_ref[...] = g
        mo_ref[...] = mn
        vo_ref[...] = vn
        d_ref[...] = -ADAM_LR * ((mn / bc1) / (jnp.sqrt(vn / bc2) + ADAM_EPS) + ADAM_WD * w_ref[...])

    p_spec = pl.BlockSpec((n, tr, c), lambda i: (0, i, 0))
    if layer is None:
        blk = pl.BlockSpec((tr, c), lambda i: (i, 0))
        return _call(body, name=name, grid=(r // tr,), in_specs=[p_spec, blk, blk, blk], out_specs=[blk] * 4,
                     out_shape=[jax.ShapeDtypeStruct((r, c), F32)] * 4,
                     compiler_params=_params(("parallel",)))(parts, w, m, v)
    blk = pl.BlockSpec((None, tr, c), lambda i: (layer, i, 0))
    if outs is None:
        outs = [lax.empty(w.shape, F32) for _ in range(4)]
    return _call(body, name=name, grid=(r // tr,),
                 in_specs=[p_spec, blk, blk, blk] + [pl.BlockSpec(memory_space=pl.ANY)] * 4, out_specs=[blk] * 4,
                 out_shape=[jax.ShapeDtypeStruct(w.shape, F32)] * 4, input_output_aliases={4 + k: k for k in range(4)},
                 compiler_params=_params(("parallel",)))(parts, w, m, v, *outs)


def _assemble_cols(blocks, *, name):
    nb, r, c = blocks.shape
    width = -(-nb * c // LANES) * LANES
    tr = _pick_rows(r, 256)

    def body(b_ref, o_ref):
        pieces = [b_ref[j] for j in range(nb)]
        if width > nb * c:
            pieces.append(jnp.zeros((tr, width - nb * c), blocks.dtype))
        o_ref[...] = jnp.concatenate(pieces, axis=1)

    return _call(body, name=name, grid=(r // tr,), in_specs=[pl.BlockSpec((nb, tr, c), lambda i: (0, i, 0))],
                 out_specs=pl.BlockSpec((tr, width), lambda i: (i, 0)), out_shape=jax.ShapeDtypeStruct((r, width), blocks.dtype),
                 compiler_params=_params(("parallel",)))(blocks)


def _split_cols(pieces, c, *, name):
    r = pieces[0].shape[0]
    tr = _pick_rows(r, 256)
    n_in = len(pieces)

    def body(*refs):
        o_ref = refs[n_in]
        x = jnp.concatenate([p[...] for p in refs[:n_in]], axis=1) if n_in > 1 else refs[0][...]
        for j in range(N_DEV):
            o_ref[j] = x[:, c * j:c * (j + 1)]

    return _call(body, name=name, grid=(r // tr,),
                 in_specs=[pl.BlockSpec((tr, p.shape[1]), lambda i: (i, 0)) for p in pieces],
                 out_specs=pl.BlockSpec((N_DEV, tr, c), lambda i: (0, i, 0)),
                 out_shape=jax.ShapeDtypeStruct((N_DEV, r, c), pieces[0].dtype),
                 compiler_params=_params(("parallel",)))(*pieces)


def _sum_parts(parts, *, name):
    n, r, c = parts.shape
    tr = _pick_rows(r, 256)

    def body(p_ref, g_ref):
        g = p_ref[0].astype(F32)
        for k in range(1, n):
            g = g + p_ref[k].astype(F32)
        g_ref[...] = g

    return _call(body, name=name, grid=(r // tr,), in_specs=[pl.BlockSpec((n, tr, c), lambda i: (0, i, 0))],
                 out_specs=pl.BlockSpec((tr, c), lambda i: (i, 0)), out_shape=jax.ShapeDtypeStruct((r, c), F32),
                 compiler_params=_params(("parallel",)))(parts)


def _peers():
    x, y, c = lax.axis_index("x"), lax.axis_index("y"), lax.axis_index("c")
    me = 4 * x + 2 * y + c
    out = []
    for k in range(1, N_DEV):
        px = (1 - x) if (k >> 2) & 1 else x
        py = (1 - y) if (k >> 1) & 1 else y
        pc = (1 - c) if k & 1 else c
        out.append(((px, py, pc), 4 * px + 2 * py + pc))
    return me, out


def _exchange(src, *, gather, name):
    shape = src.shape if gather else src.shape[1:]

    def body(s_ref, o_ref, send_sems, recv_sems, local_sem):
        me, peers = _peers()
        mine = pltpu.make_async_copy(s_ref if gather else s_ref.at[me], o_ref.at[me], local_sem)
        mine.start()
        sends = []
        for k, (dev, pid) in enumerate(peers):
            cp = pltpu.make_async_remote_copy(
                src_ref=s_ref if gather else s_ref.at[pid], dst_ref=o_ref.at[me],
                send_sem=send_sems.at[k], recv_sem=recv_sems.at[k], device_id=dev, device_id_type=MESH)
            cp.start()
            sends.append(cp)
        for k, (dev, pid) in enumerate(peers):
            pltpu.make_async_remote_copy(
                src_ref=s_ref if gather else s_ref.at[pid], dst_ref=o_ref.at[pid],
                send_sem=send_sems.at[k], recv_sem=recv_sems.at[k], device_id=dev, device_id_type=MESH).wait_recv()
        for cp in sends:
            cp.wait_send()
        mine.wait()

    any_spec = pl.BlockSpec(memory_space=pl.ANY)
    return _call(body, name=name, in_specs=[any_spec], out_specs=any_spec,
                 out_shape=jax.ShapeDtypeStruct((N_DEV,) + tuple(shape), src.dtype),
                 scratch_shapes=[pltpu.SemaphoreType.DMA((N_DEV - 1,)), pltpu.SemaphoreType.DMA((N_DEV - 1,)),
                                 pltpu.SemaphoreType.DMA(())])(src)


_HBM = pl.BlockSpec(memory_space=pltpu.HBM)
_SEM = pl.BlockSpec(memory_space=pltpu.SEMAPHORE)
_EFFECT = pltpu.SideEffectType.DATAFLOW_SIDE_EFFECTING


def _split_copies(s_refs, l_refs, send_sems, recv_sems, gather, incoming):
    me, peers = _peers()
    local, remote = [], []
    for ti, (s_ref, l_ref) in enumerate(zip(s_refs, l_refs)):
        base = ti * N_DEV
        local.append(pltpu.make_async_copy(s_ref if gather else s_ref.at[me], l_ref.at[me], recv_sems.at[base + N_DEV - 1]))
        for k, (dev, pid) in enumerate(peers):
            sems = dict(send_sem=send_sems.at[base + k], recv_sem=recv_sems.at[base + k], device_id=dev, device_id_type=MESH)
            src = s_ref if gather else s_ref.at[pid]
            remote.append((
                pltpu.make_async_remote_copy(src_ref=src, dst_ref=l_ref.at[me], **sems),
                pltpu.make_async_remote_copy(src_ref=src, dst_ref=l_ref.at[pid], **sems) if incoming else None))
    return local, remote


def _exchange_start(srcs, *, gather, name, after=()):
    n = len(srcs)
    after = list(after)
    srcs = [pltpu.with_memory_space_constraint(s, pltpu.HBM) for s in srcs]
    lands = [pltpu.with_memory_space_constraint(
        lax.empty((N_DEV,) + tuple(s.shape if gather else s.shape[1:]), s.dtype), pltpu.HBM) for s in srcs]

    def body(*refs):
        s_refs, l_refs = refs[:n], refs[n:2 * n]
        outs = refs[2 * n + len(after):]
        send_sems, recv_sems, token = outs[0], outs[1], outs[-1]
        local, remote = _split_copies(s_refs, l_refs, send_sems, recv_sems, gather, incoming=False)
        for cp in local:
            cp.start()
        for out_cp, _ in remote:
            out_cp.start()
        token[...] = jnp.zeros_like(token)

    outs = _call(
        body, name=name,
        out_shape=(pltpu.SemaphoreType.DMA((n * N_DEV,)), pltpu.SemaphoreType.DMA((n * N_DEV,)),
                   *[pltpu.HBM(s.shape, s.dtype) for s in srcs], *[pltpu.HBM(l.shape, l.dtype) for l in lands],
                   jax.ShapeDtypeStruct((SUBLANES, LANES), F32)),
        in_specs=[_HBM] * (2 * n) + [pl.BlockSpec(memory_space=pl.ANY)] * len(after),
        out_specs=(_SEM, _SEM, *[_HBM] * (2 * n), pl.BlockSpec(memory_space=pltpu.VMEM)),
        input_output_aliases={k: k + 2 for k in range(2 * n)},
        compiler_params=pltpu.CompilerParams(has_side_effects=_EFFECT),
    )(*srcs, *lands, *after)
    return dict(n=n, gather=gather, sems=outs[:2], srcs=outs[2:2 + n], lands=outs[2 + n:2 + 2 * n]), outs[-1]


def _exchange_wait(state, after, *, name):
    n, gather = state["n"], state["gather"]
    after = list(after)

    def body(*refs):
        s_refs, l_refs = refs[:n], refs[n:2 * n]
        send_sems, recv_sems = refs[2 * n], refs[2 * n + 1]
        local, remote = _split_copies(s_refs, l_refs, send_sems, recv_sems, gather, incoming=True)
        for out_cp, in_cp in remote:
            out_cp.wait_send()
            in_cp.wait_recv()
        for cp in local:
            cp.wait()

    outs = _call(
        body, name=name,
        out_shape=tuple(pltpu.HBM(a.shape, a.dtype) for a in (*state["srcs"], *state["lands"])),
        in_specs=[_HBM] * (2 * n) + [_SEM, _SEM] + [pl.BlockSpec(memory_space=pl.ANY)] * len(after),
        out_specs=tuple([_HBM] * (2 * n)),
        input_output_aliases={k: k for k in range(2 * n)},
        compiler_params=pltpu.CompilerParams(has_side_effects=_EFFECT),
    )(*state["srcs"], *state["lands"], *state["sems"], *after)
    return outs[n:]


def _pack(arrs):
    flat = jnp.concatenate([a.reshape(-1).astype(F32) for a in arrs])
    pad = (-flat.shape[0]) % (SUBLANES * LANES)
    return jnp.pad(flat, (0, pad)).reshape(-1, LANES)


def _unpack(packed, shapes):
    flat = packed.reshape(-1)
    out, off = [], 0
    for s in shapes:
        n = 1
        for v in s:
            n *= v
        out.append(flat[off:off + n].reshape(s))
        off += n
    return out


SMALL = ("norm_mix_pre", "ssm_conv_b", "dt_bias", "a_log", "d_skip", "conv_out_norm", "ssm_out_norm",
         "norm_mix_post", "norm_mlp_pre", "norm_mlp_post", "conv_a_w", "ssm_conv_w")
BIG = ("w_in", "w_out", "w_up", "w_down")
ORDER = ("norm_mix_pre", "w_in", "conv_a_w", "ssm_conv_w", "ssm_conv_b", "dt_bias", "a_log", "d_skip",
         "conv_out_norm", "ssm_out_norm", "w_out", "norm_mix_post", "norm_mlp_pre", "w_up", "w_down", "norm_mlp_post")


def kernel(x, norm_mix_pre, w_in, conv_a_w, ssm_conv_w, ssm_conv_b, dt_bias, a_log, d_skip, conv_out_norm, ssm_out_norm, w_out, norm_mix_post, norm_mlp_pre, w_up, w_down, norm_mlp_post, loss_target, m_norm_mix_pre, m_w_in, m_conv_a_w, m_ssm_conv_w, m_ssm_conv_b, m_dt_bias, m_a_log, m_d_skip, m_conv_out_norm, m_ssm_out_norm, m_w_out, m_norm_mix_post, m_norm_mlp_pre, m_w_up, m_w_down, m_norm_mlp_post, v_norm_mix_pre, v_w_in, v_conv_a_w, v_ssm_conv_w, v_ssm_conv_b, v_dt_bias, v_a_log, v_d_skip, v_conv_out_norm, v_ssm_out_norm, v_w_out, v_norm_mix_post, v_norm_mlp_pre, v_w_up, v_w_down, v_norm_mlp_post):
    W = dict(norm_mix_pre=norm_mix_pre, w_in=w_in, conv_a_w=conv_a_w, ssm_conv_w=ssm_conv_w, ssm_conv_b=ssm_conv_b,
             dt_bias=dt_bias, a_log=a_log, d_skip=d_skip, conv_out_norm=conv_out_norm, ssm_out_norm=ssm_out_norm,
             w_out=w_out, norm_mix_post=norm_mix_post, norm_mlp_pre=norm_mlp_pre, w_up=w_up, w_down=w_down,
             norm_mlp_post=norm_mlp_post)
    M = dict(norm_mix_pre=m_norm_mix_pre, w_in=m_w_in, conv_a_w=m_conv_a_w, ssm_conv_w=m_ssm_conv_w,
             ssm_conv_b=m_ssm_conv_b, dt_bias=m_dt_bias, a_log=m_a_log, d_skip=m_d_skip,
             conv_out_norm=m_conv_out_norm, ssm_out_norm=m_ssm_out_norm, w_out=m_w_out,
             norm_mix_post=m_norm_mix_post, norm_mlp_pre=m_norm_mlp_pre, w_up=m_w_up, w_down=m_w_down,
             norm_mlp_post=m_norm_mlp_post)
    V = dict(norm_mix_pre=v_norm_mix_pre, w_in=v_w_in, conv_a_w=v_conv_a_w, ssm_conv_w=v_ssm_conv_w,
             ssm_conv_b=v_ssm_conv_b, dt_bias=v_dt_bias, a_log=v_a_log, d_skip=v_d_skip,
             conv_out_norm=v_conv_out_norm, ssm_out_norm=v_ssm_out_norm, w_out=v_w_out,
             norm_mix_post=v_norm_mix_post, norm_mlp_pre=v_norm_mlp_pre, w_up=v_w_up, w_down=v_w_down,
             norm_mlp_post=v_norm_mlp_post)

    nseq, seq, d = x.shape
    t = nseq * seq
    depth = w_in.shape[0]
    h = d // HEAD_DIM
    xbc = d + 2 * SSM_GROUPS * D_STATE
    in_cols = w_in.shape[2] * N_DEV
    d_mix = w_out.shape[1] * N_DEV
    d_ff = w_up.shape[2] * N_DEV
    me = 4 * lax.axis_index("x") + 2 * lax.axis_index("y") + lax.axis_index("c")
    ca_shard = conv_a_w.shape[2]
    sc_shard = ssm_conv_w.shape[2]

    tap_shapes = [conv_a_w.shape[1:], ssm_conv_w.shape[1:]]

    def gather_start(i, after=()):
        st_in, tok_in = _exchange_start([w_in[i].astype(BF16), _pack([conv_a_w[i], ssm_conv_w[i]])], gather=True,
                                        name=f"gather_start_in_{i}", after=after)
        st_rest, tok_rest = _exchange_start([W[n][i].astype(BF16) for n in ("w_out", "w_up", "w_down")], gather=True,
                                            name=f"gather_start_rest_{i}", after=[tok_in])
        return st_in, st_rest, tok_rest

    vec = lambda name, i: W[name][i].reshape(1, -1)
    emat = (lax.broadcasted_iota(jnp.int32, (h, d), 1) // HEAD_DIM == lax.broadcasted_iota(jnp.int32, (h, d), 0)).astype(F32)

    xcur = x.reshape(t, d)
    hcur = _norm_fwd(xcur, vec("norm_mix_pre", 0), name="norm_first")
    saved = []
    nxt = gather_start(0)
    for i in range(depth):
        st_in, st_rest, tok = nxt
        win_g, taps_g = _exchange_wait(st_in, [hcur, tok], name=f"gather_wait_in_{i}")
        win = _assemble_cols(win_g, name=f"assemble_w_in_{i}")
        taps_j = [_unpack(taps_g[j], tap_shapes) for j in range(N_DEV)]
        conv_a_i = jnp.concatenate([tj[0] for tj in taps_j], axis=1)
        ssm_conv_i = jnp.concatenate([tj[1] for tj in taps_j], axis=1)
        proj = _mm(hcur, win, n=4 * d + xbc, name=f"fwd_proj_{i}", out_dtypes=(BF16,))
        pdt = _mm(hcur, win, n=LANES, b_off=4 * d + xbc, name=f"fwd_proj_dt_{i}")
        ya, va = _conva_fwd(proj, conv_a_i, vec("conv_out_norm", i), d=d, seq=seq, name=f"fwd_conv_a_{i}")
        cpre = _convb_fwd(proj, ssm_conv_i, vec("ssm_conv_b", i), col0=4 * d, seq=seq, name=f"fwd_conv_b_{i}")
        dsk_lane = jnp.repeat(W["d_skip"][i], HEAD_DIM).reshape(1, d)
        cat, y2, hprev = _ssd_fwd(cpre, pdt, proj, ya, vec("dt_bias", i), vec("a_log", i), dsk_lane,
                                  vec("ssm_out_norm", i), emat, nseq=nseq, seq=seq, name=f"fwd_ssd_{i}")
        wout_g, wup_g, wdown_g = _exchange_wait(st_rest, [cat], name=f"gather_wait_rest_{i}")
        lw = dict(win=win, wout=wout_g.reshape(d_mix, d),
                  wup=_assemble_cols(wup_g, name=f"assemble_w_up_{i}"), wdown=wdown_g.reshape(d_ff, d),
                  conv_a=conv_a_i, ssm_conv=ssm_conv_i)
        after = []
        if i + 1 < depth:
            nxt = gather_start(i + 1, after=[wout_g])
            after = [nxt[2]]
        mix = _mm(cat, lw["wout"], name=f"fwd_out_{i}", after=after)
        x1, h2 = _resid_norm(xcur, mix, vec("norm_mix_post", i), vec("norm_mlp_pre", i), name=f"fwd_post_mix_{i}")
        f = _mm(h2, lw["wup"], name=f"fwd_up_{i}", out_dtypes=(BF16,), epi=_epi_relu2)
        dn = _mm(f, lw["wdown"], name=f"fwd_down_{i}")
        g_next = vec("norm_mix_pre", i + 1) if i + 1 < depth else vec("norm_mix_pre", 0)
        x2, hnext = _resid_norm(x1, dn, vec("norm_mlp_post", i), g_next, name=f"fwd_post_mlp_{i}")
        saved.append(dict(lw=lw, x0=xcur, h=hcur, proj=proj, pdt=pdt, va=va, cpre=cpre, y2=y2,
                          hprev=hprev, cat=cat, mix=mix, x1=x1, h2=h2, f=f, dn=dn, dsk_lane=dsk_lane))
        xcur, hcur = x2, hnext

    dx, loss_part = _loss_fwd_bwd(xcur, loss_target.reshape(t, d), name="loss")
    loss = lax.psum(loss_part[0, 0], ("x", "y", "c"))

    small_grads = {n: [None] * depth for n in SMALL}
    big_out = {n: None for n in BIG}

    def finish(pending, after):
        li, st_a, st_b = pending

        def update(n, parts):
            big_out[n] = _sum_adamw(parts, W[n], M[n], V[n], layer=li, outs=big_out[n], name=f"adamw_{n}_{li}")

        p_down, p_up = _exchange_wait(st_a, after, name=f"scatter_wait_a_{li}")
        update("w_down", p_down)
        update("w_up", p_up)
        p_out, p_in = _exchange_wait(st_b, after + [big_out["w_up"][0]], name=f"scatter_wait_b_{li}")
        update("w_out", p_out)
        update("w_in", p_in)

    pending = None
    for i in reversed(range(depth)):
        s = saved[i]
        lw = s["lw"]
        ddn, dg = _bwd_norm_out(s["dn"], vec("norm_mlp_post", i), dx, name=f"bwd_norm_mlp_post_{i}")
        small_grads["norm_mlp_post"][i] = dg
        dup = _mm(ddn, lw["wdown"], tb=True, name=f"bwd_down_dx_{i}", out_dtypes=(BF16,), epi=_epi_drelu2,
                  extras=(s["f"],))
        g_wdown = _mm(s["f"], ddn, ta=True, name=f"bwd_down_dw_{i}", out_dtypes=(BF16,))
        dh2 = _mm(dup, lw["wup"], tb=True, name=f"bwd_up_dx_{i}", out_dtypes=(BF16,))
        g_wup = _mm(s["h2"], dup, ta=True, name=f"bwd_up_dw_{i}", out_dtypes=(BF16,))
        st_a, tok_a = _exchange_start(
            [g_wdown.reshape(N_DEV, d_ff // N_DEV, d), _split_cols([g_wup], d_ff // N_DEV, name=f"split_g_w_up_{i}")],
            gather=False, name=f"scatter_start_a_{i}")
        dx1, dmix, dg_pre, dg_post = _bwd_norm_pair(s["x1"], [dh2], dx, s["mix"], vec("norm_mlp_pre", i) + tok_a[0:1, 0:1],
                                                    vec("norm_mix_post", i), name=f"bwd_norm_mix_post_{i}")
        small_grads["norm_mlp_pre"][i] = dg_pre
        small_grads["norm_mix_post"][i] = dg_post
        dcat = _mm(dmix, lw["wout"], tb=True, name=f"bwd_out_dx_{i}", out_dtypes=(BF16,))
        g_wout = _mm(s["cat"], dmix, ta=True, name=f"bwd_out_dw_{i}", out_dtypes=(BF16,))
        dproj, dcaw, dgca = _conva_bwd(dcat, s["proj"], s["va"], lw["conv_a"], vec("conv_out_norm", i), d=d, seq=seq,
                                       name=f"bwd_conv_a_{i}")
        small_grads["conv_a_w"][i] = dcaw
        small_grads["conv_out_norm"][i] = dgca
        dconv, dproj, dpdt, dgs, ddsk, ddtb, dalog = _ssd_bwd(
            s["cpre"], s["pdt"], s["proj"], s["y2"], s["hprev"], dcat, vec("dt_bias", i), vec("a_log", i),
            s["dsk_lane"], vec("ssm_out_norm", i), emat, dproj, nseq=nseq, seq=seq, name=f"bwd_ssd_{i}")
        small_grads["ssm_out_norm"][i] = dgs
        small_grads["d_skip"][i] = ddsk
        small_grads["dt_bias"][i] = ddtb
        small_grads["a_log"][i] = dalog
        dproj, dscw, dscb = _convb_bwd(dconv, s["proj"], lw["ssm_conv"], dproj, col0=4 * d, seq=seq,
                                       name=f"bwd_conv_b_{i}")
        small_grads["ssm_conv_w"][i] = dscw
        small_grads["ssm_conv_b"][i] = dscb
        g_win = _split_cols([
            _mm(s["h"], dproj, ta=True, name=f"bwd_proj_dw_{i}", out_dtypes=(BF16,)),
            _mm(s["h"], dpdt, ta=True, name=f"bwd_proj_dt_dw_{i}", out_dtypes=(BF16,))],
            in_cols // N_DEV, name=f"split_g_w_in_{i}")
        st_b, tok_b = _exchange_start(
            [g_wout.reshape(N_DEV, d_mix // N_DEV, d), g_win], gather=False, name=f"scatter_start_b_{i}")
        dh_parts = [_mm(dp, lw["win"], tb=True, b_koff=off, name=f"bwd_proj_{nm}dx_{i}", after=[tok_b], out_dtypes=(BF16,))
                    for nm, dp, off in (("", dproj, 0), ("dt_", dpdt, 4 * d + xbc))]
        dx, dg_in = _bwd_norm_in(s["x0"], dh_parts, dx1, vec("norm_mix_pre", i), name=f"bwd_norm_mix_pre_{i}")
        small_grads["norm_mix_pre"][i] = dg_in
        if pending is not None:
            finish(pending, [dx])
        pending = (i, st_a, st_b)

    grad_x = dx.reshape(nseq, seq, d)

    small_shapes_full = {n: (depth,) + tuple(small_grads[n][0].shape) for n in SMALL}
    gpack = _pack([jnp.stack(small_grads[n]) for n in SMALL])
    gparts = _exchange(gpack, gather=True, name="allreduce_small")

    def shard_of(n, full):
        if n == "conv_a_w":
            return lax.dynamic_slice_in_dim(full, me * ca_shard, ca_shard, axis=2)
        if n == "ssm_conv_w":
            return lax.dynamic_slice_in_dim(full, me * sc_shard, sc_shard, axis=2)
        return full.reshape(W[n].shape)

    gsum = _sum_parts(gparts, name="sum_small")
    gfull = _unpack(gsum, [small_shapes_full[n] for n in SMALL])
    gsmall = {n: shard_of(n, gf) for n, gf in zip(SMALL, gfull)}
    res = _sum_adamw(_pack([gsmall[n] for n in SMALL])[None], _pack([W[n] for n in SMALL]),
                     _pack([M[n] for n in SMALL]), _pack([V[n] for n in SMALL]), name="adamw_small")
    small_out = [dict(zip(SMALL, _unpack(r, [W[n].shape for n in SMALL]))) for r in res]
    finish(pending, [dx, res[0]])

    def out_of(kind, n):
        return big_out[n][kind] if n in BIG else small_out[kind][n]

    return (loss, grad_x, *[out_of(k, n) for k in range(4) for n in ORDER])
```

```python
import functools

import jax
import jax.numpy as jnp
from jax import lax
from jax.experimental import pallas as pl
from jax.experimental.pallas import tpu as pltpu

F32 = jnp.float32
BF16 = jnp.bfloat16
HIGHEST = lax.Precision.HIGHEST
MESH = pl.DeviceIdType.MESH

EPS = 1e-6
HEAD_DIM = 64
D_STATE = 128
SSM_GROUPS = 2
CHUNK = 128
CONV_K = 3
SSM_CONV_K = 4
ADAM_LR = 0.001
ADAM_B1 = 0.9
ADAM_B2 = 0.999
ADAM_EPS = 1e-08
ADAM_WD = 0.01
ADAM_STEP = 10

N_DEV = 8
LANES = 128
SUBLANES = 8
VMEM_LIMIT = 48 * 1024 * 1024
ROW_TILE = 512
MM_TILE = 1024
MM_TILE_N = 1536


def _params(sem):
    return pltpu.CompilerParams(dimension_semantics=sem, vmem_limit_bytes=VMEM_LIMIT)


def _call(body, **kw):
    return pl.pallas_call(body, **kw)


def _pick(n, cap):
    best = None
    for t in range(LANES, min(n, cap) + 1, LANES):
        if n % t == 0:
            best = t
    return best or n


def _pick_rows(n, cap):
    best = None
    for t in range(SUBLANES, min(n, cap) + 1, SUBLANES):
        if n % t == 0:
            best = t
    return best or n


def _sigmoid(x):
    return 1.0 / (1.0 + jnp.exp(-x))


def _softplus(x):
    return jnp.maximum(x, 0.0) + jnp.log1p(jnp.exp(-jnp.abs(x)))


def _rms(x):
    return lax.rsqrt(jnp.mean(x * x, axis=-1, keepdims=True) + EPS)


def _rms_bwd(x, r, g, dy):
    gy = dy * g
    dx = r * gy - x * (r * r * r) * jnp.mean(gy * x, axis=-1, keepdims=True)
    return dx, dy * x * r


def _full(shape):
    return pl.BlockSpec(shape, lambda *_: (0,) * len(shape))


def _mm(a, b, *, name, ta=False, tb=False, out_dtypes=(F32,), epi=None, extras=(), n=None, b_off=0, b_koff=0,
        after=()):
    m, k = (a.shape[1], a.shape[0]) if ta else a.shape
    if n is None:
        n = b.shape[0] if tb else b.shape[1]
    tm, tn, tk = _pick(m, MM_TILE), _pick(n, MM_TILE_N), _pick(k, MM_TILE)
    while b_off % tn or n % tn:
        tn -= LANES
    if b_koff == 0 and k > MM_TILE:
        tk = _pick(k, MM_TILE_N)
    while b_koff % tk or k % tk:
        tk -= LANES
    nk = k // tk
    nm, nn = m // tm, n // tn
    jo = b_off // tn
    ko = b_koff // tk
    a_bytes = m * k * a.dtype.itemsize
    b_bytes = n * k * b.dtype.itemsize
    m_outer = a_bytes + nm * b_bytes <= b_bytes + nn * a_bytes
    ij = (lambda g0, g1: (g0, g1)) if m_outer else (lambda g0, g1: (g1, g0))
    grid = (nm, nn, nk) if m_outer else (nn, nm, nk)

    def a_map(g0, g1, kk):
        i, _ = ij(g0, g1)
        return (kk, i) if ta else (i, kk)

    def b_map(g0, g1, kk):
        _, j = ij(g0, g1)
        return (j + jo, kk + ko) if tb else (kk + ko, j + jo)

    def o_map(g0, g1, kk):
        return ij(g0, g1)

    a_spec = pl.BlockSpec((tk, tm) if ta else (tm, tk), a_map)
    b_spec = pl.BlockSpec((tn, tk) if tb else (tk, tn), b_map)
    o_spec = pl.BlockSpec((tm, tn), o_map)
    dims = (((0 if ta else 1,), (1 if tb else 0,)), ((), ()))
    n_ex = len(extras)
    after = list(after)
    o0 = 2 + n_ex + len(after)

    def finish(acc, ex, outs):
        res = (acc,) if epi is None else epi(acc, *[e[...] for e in ex])
        for o, r in zip(outs, res):
            o[...] = r.astype(o.dtype)

    def body_single(*refs):
        a_ref, b_ref = refs[:2]
        acc = lax.dot_general(a_ref[...].astype(BF16), b_ref[...].astype(BF16), dims, preferred_element_type=F32)
        finish(acc, refs[2:2 + n_ex], refs[o0:])

    def body_multi(*refs):
        a_ref, b_ref = refs[:2]
        acc = refs[-1]
        kk = pl.program_id(2)

        @pl.when(kk == 0)
        def _():
            acc[...] = jnp.zeros_like(acc)

        acc[...] += lax.dot_general(a_ref[...].astype(BF16), b_ref[...].astype(BF16), dims, preferred_element_type=F32)

        @pl.when(kk == nk - 1)
        def _():
            finish(acc[...], refs[2:2 + n_ex], refs[o0:-1])

    outs = _call(
        body_single if nk == 1 else body_multi, name=name, grid=grid,
        in_specs=[a_spec, b_spec] + [o_spec] * n_ex + [pl.BlockSpec(memory_space=pl.ANY)] * len(after),
        out_specs=[o_spec] * len(out_dtypes),
        out_shape=[jax.ShapeDtypeStruct((m, n), dt) for dt in out_dtypes],
        scratch_shapes=[] if nk == 1 else [pltpu.VMEM((tm, tn), F32)],
        compiler_params=_params(("parallel", "parallel", "arbitrary")),
    )(a, b, *extras, *after)
    return outs[0] if len(out_dtypes) == 1 else outs


def _epi_relu2(acc):
    r = jnp.maximum(acc, 0.0)
    return (r * r,)


def _epi_drelu2(acc, f):
    return (acc * (2.0 * jnp.sqrt(f.astype(F32))),)


def _norm_fwd(x, g, *, name):
    t, d = x.shape
    tt = _pick_rows(t, ROW_TILE)

    def body(x_ref, g_ref, h_ref):
        xv = x_ref[...]
        h_ref[...] = (xv * _rms(xv) * g_ref[...]).astype(BF16)

    row = pl.BlockSpec((tt, d), lambda i: (i, 0))
    return _call(body, name=name, grid=(t // tt,), in_specs=[row, _full((1, d))], out_specs=row,
                 out_shape=jax.ShapeDtypeStruct((t, d), BF16), compiler_params=_params(("parallel",)))(x, g)


def _resid_norm(x, n, g1, g2, *, name):
    t, d = x.shape
    tt = _pick_rows(t, ROW_TILE)

    def body(x_ref, n_ref, g1_ref, g2_ref, xo_ref, h_ref):
        nv = n_ref[...]
        xn = x_ref[...] + nv * _rms(nv) * g1_ref[...]
        xo_ref[...] = xn
        h_ref[...] = (xn * _rms(xn) * g2_ref[...]).astype(BF16)

    row = pl.BlockSpec((tt, d), lambda i: (i, 0))
    return _call(body, name=name, grid=(t // tt,), in_specs=[row, row, _full((1, d)), _full((1, d))],
                 out_specs=[row, row],
                 out_shape=[jax.ShapeDtypeStruct((t, d), F32), jax.ShapeDtypeStruct((t, d), BF16)],
                 compiler_params=_params(("parallel",)))(x, n, g1, g2)


def _loss_fwd_bwd(xf, target, *, name):
    t, d = xf.shape
    tt = _pick_rows(t, ROW_TILE)
    nt = t // tt

    def body(x_ref, t_ref, dy_ref, loss_ref, acc):
        i = pl.program_id(0)

        @pl.when(i == 0)
        def _():
            acc[...] = jnp.zeros_like(acc)

        e = x_ref[...] - t_ref[...]
        dy_ref[...] = e * (1.0 / d)
        acc[...] += jnp.sum(e * e, axis=0, keepdims=True)

        @pl.when(i == nt - 1)
        def _():
            loss_ref[...] = jnp.sum(acc[...], axis=-1, keepdims=True) * (0.5 / d)

    row = pl.BlockSpec((tt, d), lambda i: (i, 0))
    return _call(body, name=name, grid=(nt,), in_specs=[row, row], out_specs=[row, _full((1, 1))],
                 out_shape=[jax.ShapeDtypeStruct((t, d), F32), jax.ShapeDtypeStruct((1, 1), F32)],
                 scratch_shapes=[pltpu.VMEM((1, d), F32)], compiler_params=_params(("arbitrary",)))(xf, target)


def _bwd_norm_pair(xin, dh, dres, n, g_in, g_out, *, name):
    t, d = xin.shape
    tt = _pick_rows(t, ROW_TILE)
    n_dh = len(dh)

    def body(*refs):
        x_ref = refs[0]
        dh_refs = refs[1:1 + n_dh]
        dres_ref, n_ref, gi_ref, go_ref, dx_ref, dn_ref, dgi_ref, dgo_ref = refs[1 + n_dh:]
        i = pl.program_id(0)

        @pl.when(i == 0)
        def _():
            dgi_ref[...] = jnp.zeros_like(dgi_ref)
            dgo_ref[...] = jnp.zeros_like(dgo_ref)

        xv = x_ref[...]
        dhv = dh_refs[0][...].astype(F32)
        for r in dh_refs[1:]:
            dhv = dhv + r[...].astype(F32)
        dxh, dgi = _rms_bwd(xv, _rms(xv), gi_ref[...], dhv)
        dx = dres_ref[...] + dxh
        dx_ref[...] = dx
        dgi_ref[...] += jnp.sum(dgi, axis=0, keepdims=True)
        nv = n_ref[...]
        dn, dgo = _rms_bwd(nv, _rms(nv), go_ref[...], dx)
        dn_ref[...] = dn.astype(BF16)
        dgo_ref[...] += jnp.sum(dgo, axis=0, keepdims=True)

    row = pl.BlockSpec((tt, d), lambda i: (i, 0))
    vec = _full((1, d))
    return _call(body, name=name, grid=(t // tt,), in_specs=[row] * (n_dh + 3) + [vec, vec],
                 out_specs=[row, row, vec, vec],
                 out_shape=[jax.ShapeDtypeStruct((t, d), F32), jax.ShapeDtypeStruct((t, d), BF16),
                            jax.ShapeDtypeStruct((1, d), F32), jax.ShapeDtypeStruct((1, d), F32)],
                 compiler_params=_params(("arbitrary",)))(xin, *dh, dres, n, g_in, g_out)


def _bwd_norm_in(xin, dh, dres, g_in, *, name):
    t, d = xin.shape
    tt = _pick_rows(t, ROW_TILE)
    n_dh = len(dh)

    def body(*refs):
        x_ref = refs[0]
        dh_refs = refs[1:1 + n_dh]
        dres_ref, gi_ref, dx_ref, dgi_ref = refs[1 + n_dh:]
        i = pl.program_id(0)

        @pl.when(i == 0)
        def _():
            dgi_ref[...] = jnp.zeros_like(dgi_ref)

        xv = x_ref[...]
        dhv = dh_refs[0][...].astype(F32)
        for r in dh_refs[1:]:
            dhv = dhv + r[...].astype(F32)
        dxh, dgi = _rms_bwd(xv, _rms(xv), gi_ref[...], dhv)
        dx_ref[...] = dres_ref[...] + dxh
        dgi_ref[...] += jnp.sum(dgi, axis=0, keepdims=True)

    row = pl.BlockSpec((tt, d), lambda i: (i, 0))
    vec = _full((1, d))
    return _call(body, name=name, grid=(t // tt,), in_specs=[row] * (n_dh + 2) + [vec],
                 out_specs=[row, vec],
                 out_shape=[jax.ShapeDtypeStruct((t, d), F32), jax.ShapeDtypeStruct((1, d), F32)],
                 compiler_params=_params(("arbitrary",)))(xin, *dh, dres, g_in)


def _bwd_norm_out(n, g_out, dx, *, name):
    t, d = n.shape
    tt = _pick_rows(t, ROW_TILE)

    def body(n_ref, go_ref, dx_ref, dn_ref, dgo_ref):
        i = pl.program_id(0)

        @pl.when(i == 0)
        def _():
            dgo_ref[...] = jnp.zeros_like(dgo_ref)

        nv = n_ref[...]
        dn, dgo = _rms_bwd(nv, _rms(nv), go_ref[...], dx_ref[...])
        dn_ref[...] = dn.astype(BF16)
        dgo_ref[...] += jnp.sum(dgo, axis=0, keepdims=True)

    row = pl.BlockSpec((tt, d), lambda i: (i, 0))
    vec = _full((1, d))
    return _call(body, name=name, grid=(t // tt,), in_specs=[row, vec, row], out_specs=[row, vec],
                 out_shape=[jax.ShapeDtypeStruct((t, d), BF16), jax.ShapeDtypeStruct((1, d), F32)],
                 compiler_params=_params(("arbitrary",)))(n, g_out, dx)


def _shift_down(cur, halo, s):
    return jnp.concatenate([halo[SUBLANES - s:], cur[:cur.shape[0] - s]], axis=0)


def _shift_up(cur, halo, s):
    return jnp.concatenate([cur[s:], halo[:s]], axis=0)


def _conva_fwd(pa, w, g, *, d, seq, name):
    t = pa.shape[0]
    tt = _pick_rows(seq, ROW_TILE)
    tps = seq // tt

    def body(xa_ref, ca_ref, ba_ref, w_ref, g_ref, ya_ref, v_ref, carry):
        i = pl.program_id(0)

        @pl.when(i % tps == 0)
        def _():
            carry[...] = jnp.zeros_like(carry)

        u = ca_ref[...].astype(F32) * xa_ref[...].astype(F32)
        halo = carry[...]
        wv = w_ref[...]
        v = wv[2:3] * u + wv[1:2] * _shift_down(u, halo, 1) + wv[0:1] * _shift_down(u, halo, 2)
        carry[...] = u[tt - SUBLANES:]
        yp = ba_ref[...].astype(F32) * v
        ya_ref[...] = (yp * _rms(yp) * g_ref[...]).astype(BF16)
        v_ref[...] = v.astype(BF16)

    col = lambda c: pl.BlockSpec((tt, d), lambda i, c=c: (i, c))
    row = pl.BlockSpec((tt, d), lambda i: (i, 0))
    return _call(body, name=name, grid=(t // tt,),
                 in_specs=[col(0), col(1), col(2), _full((CONV_K, d)), _full((1, d))], out_specs=[row, row],
                 out_shape=[jax.ShapeDtypeStruct((t, d), BF16), jax.ShapeDtypeStruct((t, d), BF16)],
                 scratch_shapes=[pltpu.VMEM((SUBLANES, d), F32)],
                 compiler_params=_params(("arbitrary",)))(pa, pa, pa, w, g)


def _conva_bwd(dcat, pa, v, w, g, *, d, seq, name):
    t, width = pa.shape
    d3 = 3 * d
    tt = _pick_rows(seq, ROW_TILE)
    tps = seq // tt
    nt = t // tt

    def body(dya_ref, xa_ref, ca_ref, ba_ref, v_ref, w_ref, g_ref, dpa_ref, dw_ref, dg_ref, carry):
        i = pl.program_id(0)

        @pl.when(i == 0)
        def _():
            dw_ref[...] = jnp.zeros_like(dw_ref)
            dg_ref[...] = jnp.zeros_like(dg_ref)

        @pl.when(i % tps == 0)
        def _():
            carry[...] = jnp.zeros_like(carry)

        xa, ca, ba, vv = [r[...].astype(F32) for r in (xa_ref, ca_ref, ba_ref, v_ref)]
        yp = ba * vv
        dyp, dgt = _rms_bwd(yp, _rms(yp), g_ref[...], dya_ref[...].astype(F32))
        dg_ref[...] += jnp.sum(dgt, axis=0, keepdims=True)
        dv = dyp * ba
        halo = carry[...]
        dv1 = _shift_up(dv, halo, 1)
        dv2 = _shift_up(dv, halo, 2)
        carry[...] = dv[:SUBLANES]
        wv = w_ref[...]
        du = wv[2:3] * dv + wv[1:2] * dv1 + wv[0:1] * dv2
        u = ca * xa
        dw_ref[0:1, :] += jnp.sum(u * dv2, axis=0, keepdims=True)
        dw_ref[1:2, :] += jnp.sum(u * dv1, axis=0, keepdims=True)
        dw_ref[2:3, :] += jnp.sum(u * dv, axis=0, keepdims=True)
        dpa_ref[:, 0:d] = (du * ca).astype(BF16)
        dpa_ref[:, d:2 * d] = (du * xa).astype(BF16)
        dpa_ref[:, 2 * d:3 * d] = (dyp * vv).astype(BF16)

    rcol = lambda c: pl.BlockSpec((tt, d), lambda i, c=c: (nt - 1 - i, c))
    return _call(body, name=name, grid=(nt,),
                 in_specs=[rcol(0), rcol(0), rcol(1), rcol(2), rcol(0), _full((CONV_K, d)), _full((1, d))],
                 out_specs=[pl.BlockSpec((tt, d3), lambda i: (nt - 1 - i, 0)), _full((CONV_K, d)), _full((1, d))],
                 out_shape=[jax.ShapeDtypeStruct((t, width), BF16), jax.ShapeDtypeStruct((CONV_K, d), F32),
                            jax.ShapeDtypeStruct((1, d), F32)],
                 scratch_shapes=[pltpu.VMEM((SUBLANES, d), F32)],
                 compiler_params=_params(("arbitrary",)))(dcat, pa, pa, pa, v, w, g)


CONV_CH = 512


def _convb_fwd(proj, w, bias, *, col0, seq, name):
    t = proj.shape[0]
    c = w.shape[1]
    cb = _pick(c, CONV_CH)
    assert col0 % cb == 0
    tt = _pick_rows(seq, 2 * ROW_TILE)
    tps = seq // tt

    def body(p_ref, w_ref, b_ref, o_ref, carry):
        i = pl.program_id(1)

        @pl.when(i % tps == 0)
        def _():
            carry[...] = jnp.zeros_like(carry)

        p = p_ref[...].astype(F32)
        halo = carry[...]
        wv = w_ref[...]
        o = wv[3:4] * p + b_ref[...]
        for s in (1, 2, 3):
            o = o + wv[3 - s:4 - s] * _shift_down(p, halo, s)
        carry[...] = p[tt - SUBLANES:]
        o_ref[...] = o.astype(BF16)

    return _call(body, name=name, grid=(c // cb, t // tt),
                 in_specs=[pl.BlockSpec((tt, cb), lambda jc, i: (i, col0 // cb + jc)),
                           pl.BlockSpec((SSM_CONV_K, cb), lambda jc, i: (0, jc)), pl.BlockSpec((1, cb), lambda jc, i: (0, jc))],
                 out_specs=pl.BlockSpec((tt, cb), lambda jc, i: (i, jc)), out_shape=jax.ShapeDtypeStruct((t, c), BF16),
                 scratch_shapes=[pltpu.VMEM((SUBLANES, cb), F32)],
                 compiler_params=_params(("arbitrary", "arbitrary")))(proj, w, bias)


def _convb_bwd(dconv, proj, w, dproj, *, col0, seq, name):
    t, c = dconv.shape
    cb = _pick(c, CONV_CH)
    assert col0 % cb == 0
    tt = _pick_rows(seq, 2 * ROW_TILE)
    tps = seq // tt
    nt = t // tt

    def body(dc_ref, p_ref, w_ref, dproj_in, dp_ref, dw_ref, db_ref, carry):
        del dproj_in
        i = pl.program_id(1)

        @pl.when(i == 0)
        def _():
            dw_ref[...] = jnp.zeros_like(dw_ref)
            db_ref[...] = jnp.zeros_like(db_ref)

        @pl.when(i % tps == 0)
        def _():
            carry[...] = jnp.zeros_like(carry)

        dc = dc_ref[...].astype(F32)
        p = p_ref[...].astype(F32)
        halo = carry[...]
        wv = w_ref[...]
        dp = wv[3:4] * dc
        dw_ref[3:4, :] += jnp.sum(p * dc, axis=0, keepdims=True)
        for s in (1, 2, 3):
            dcs = _shift_up(dc, halo, s)
            dp = dp + wv[3 - s:4 - s] * dcs
            dw_ref[3 - s:4 - s, :] += jnp.sum(p * dcs, axis=0, keepdims=True)
        carry[...] = dc[:SUBLANES]
        db_ref[...] += jnp.sum(dc, axis=0, keepdims=True)
        dp_ref[...] = dp.astype(BF16)

    win_spec = pl.BlockSpec((tt, cb), lambda jc, i: (nt - 1 - i, col0 // cb + jc))
    taps = pl.BlockSpec((SSM_CONV_K, cb), lambda jc, i: (0, jc))
    return _call(body, name=name, grid=(c // cb, nt),
                 in_specs=[pl.BlockSpec((tt, cb), lambda jc, i: (nt - 1 - i, jc)), win_spec, taps,
                           pl.BlockSpec(memory_space=pl.ANY)],
                 out_specs=[win_spec, taps, pl.BlockSpec((1, cb), lambda jc, i: (0, jc))],
                 out_shape=[jax.ShapeDtypeStruct(dproj.shape, BF16), jax.ShapeDtypeStruct((SSM_CONV_K, c), F32),
                            jax.ShapeDtypeStruct((1, c), F32)],
                 input_output_aliases={3: 0},
                 scratch_shapes=[pltpu.VMEM((SUBLANES, cb), F32)],
                 compiler_params=_params(("arbitrary", "arbitrary")))(dconv, proj, w, dproj)


def _expand_heads(x, ev):
    return jnp.dot(x, ev, precision=HIGHEST, preferred_element_type=F32)


def _head_sums(v, ev):
    return lax.dot_general(v, ev, (((1,), (1,)), ((), ())), precision=HIGHEST, preferred_element_type=F32)


def _ssd_common(c_ref, pdt_ref, dtb_ref, alog_ref, e_ref, h):
    cp = c_ref[...].astype(F32)
    sg = _sigmoid(cp)
    act = cp * sg
    pre = pdt_ref[:, 0:h] + dtb_ref[...]
    dt = _softplus(pre)
    a = -jnp.exp(alog_ref[...])
    adt = dt * a
    row = lax.broadcasted_iota(jnp.int32, (CHUNK, CHUNK), 0)
    col = lax.broadcasted_iota(jnp.int32, (CHUNK, CHUNK), 1)
    tril = row >= col
    cs = jnp.dot(tril.astype(F32), adt, precision=HIGHEST, preferred_element_type=F32)
    ev = e_ref[...]
    dt_l = _expand_heads(dt, ev)
    ecs_l = jnp.exp(_expand_heads(cs, ev))
    return dict(cp=cp, sg=sg, act=act, pre=pre, dt=dt, a=a, cs=cs, dt_l=dt_l, ecs_l=ecs_l,
                tril=tril, row=row, col=col, lo=col < HEAD_DIM)


def _dot_nt(a, b):
    return lax.dot_general(a, b, (((1,), (1,)), ((), ())), preferred_element_type=F32)


def _dot_tn(a, b):
    return lax.dot_general(a, b, (((0,), (0,)), ((), ())), preferred_element_type=F32)


def _dot(a, b):
    return jnp.dot(a, b, preferred_element_type=F32)


def _ssd_fwd(cpre, pdt, pz, ya, dtb, alog, dsk_lane, gs, emat, *, nseq, seq, name):
    t, xbc = cpre.shape
    d = ya.shape[1]
    h = d // HEAD_DIM
    npair = h // 2
    ppg = npair // SSM_GROUPS
    nc = seq // CHUNK
    gw = d // SSM_GROUPS
    bc0 = d
    cc0 = d + SSM_GROUPS * D_STATE

    def body(c_ref, pdt_ref, z_ref, ya_ref, dtb_ref, alog_ref, dsk_ref, gs_ref, e_ref, cat_ref, y2_ref, hp_ref, h_ref):
        c = pl.program_id(1)

        @pl.when(c == 0)
        def _():
            h_ref[...] = jnp.zeros_like(h_ref)

        q = _ssd_common(c_ref, pdt_ref, dtb_ref, alog_ref, e_ref, h)
        act, cs, lo, ecs_l = q["act"], q["cs"], q["lo"], q["ecs_l"]
        xs = act[:, :d]
        xd = xs * q["dt_l"]
        ys = []
        for g in range(SSM_GROUPS):
            bg = act[:, bc0 + g * D_STATE: bc0 + (g + 1) * D_STATE]
            cgb = act[:, cc0 + g * D_STATE: cc0 + (g + 1) * D_STATE].astype(BF16)
            s = _dot_nt(cgb, bg.astype(BF16))
            for jj in range(ppg):
                j = g * ppg + jj
                sl = slice(LANES * j, LANES * (j + 1))
                xdj = xd[:, sl]
                x2 = jnp.concatenate([jnp.where(lo, xdj, 0.0), jnp.where(lo, 0.0, xdj)], axis=0).astype(BF16)
                hprev = h_ref[j]
                hp_ref[j] = hprev.astype(BF16)
                ms, bws = [], []
                for hh in (2 * j, 2 * j + 1):
                    csc = cs[:, hh:hh + 1]
                    csb = jnp.broadcast_to(csc, (CHUNK, CHUNK))
                    ms.append(s * jnp.exp(jnp.where(q["tril"], csb - csb.T, -jnp.inf)))
                    bws.append(bg * jnp.exp(cs[CHUNK - 1:CHUNK, hh:hh + 1] - csc))
                ydiag = _dot(jnp.concatenate(ms, axis=1).astype(BF16), x2)
                st = _dot_tn(jnp.concatenate(bws, axis=0).astype(BF16), x2)
                ecs = ecs_l[:, sl]
                yoff = _dot(cgb, hprev.astype(BF16)) * ecs
                h_ref[j] = hprev * ecs[CHUNK - 1:CHUNK] + st
                ys.append(ydiag + yoff)
        y = jnp.concatenate(ys, axis=1) + dsk_ref[...] * xs
        y2_ref[...] = y.astype(BF16)
        zv = z_ref[...].astype(F32)
        y3 = y * (zv * _sigmoid(zv))
        cat_ref[:, 0:d] = ya_ref[...]
        for gi in range(SSM_GROUPS):
            seg = y3[:, gi * gw:(gi + 1) * gw]
            cat_ref[:, d + gi * gw:d + (gi + 1) * gw] = (seg * _rms(seg) * gs_ref[:, gi * gw:(gi + 1) * gw]).astype(BF16)

    chunk = lambda w, cb=0: pl.BlockSpec((CHUNK, w), lambda b, c, cb=cb: (b * nc + c, cb))
    vec = lambda w: pl.BlockSpec((1, w), lambda b, c: (0, 0))
    hp_spec = pl.BlockSpec((None, None, npair, D_STATE, LANES), lambda b, c: (b, c, 0, 0, 0))
    return _call(body, name=name, grid=(nseq, nc),
                 in_specs=[chunk(xbc), chunk(LANES), chunk(d, 3), chunk(d), vec(h), vec(h), vec(d), vec(d),
                           pl.BlockSpec((h, d), lambda b, c: (0, 0))],
                 out_specs=[chunk(2 * d), chunk(d), hp_spec],
                 out_shape=[jax.ShapeDtypeStruct((t, 2 * d), BF16), jax.ShapeDtypeStruct((t, d), BF16),
                            jax.ShapeDtypeStruct((nseq, nc, npair, D_STATE, LANES), BF16)],
                 scratch_shapes=[pltpu.VMEM((npair, D_STATE, LANES), F32)],
                 compiler_params=_params(("arbitrary", "arbitrary")))(cpre, pdt, pz, ya, dtb, alog, dsk_lane, gs, emat)


def _ssd_bwd(cpre, pdt, pz, y2, hprev_all, dcat, dtb, alog, dsk_lane, gs, emat, dproj, *, nseq, seq, name):
    t, xbc = cpre.shape
    d = y2.shape[1]
    h = d // HEAD_DIM
    npair = h // 2
    ppg = npair // SSM_GROUPS
    nc = seq // CHUNK
    gw = d // SSM_GROUPS
    bc0 = d
    cc0 = d + SSM_GROUPS * D_STATE

    def body(c_ref, pdt_ref, z_ref, y2_ref, hp_ref, dys_ref, dtb_ref, alog_ref, dsk_ref, gs_ref, e_ref, dproj_in,
             dconv_ref, dz_ref, dpdt_ref, dgs_ref, ddsk_ref, ddtb_ref, dalog_ref, dh_ref):
        del dproj_in
        b = pl.program_id(0)
        c = pl.program_id(1)

        @pl.when(c == 0)
        def _():
            dh_ref[...] = jnp.zeros_like(dh_ref)

        @pl.when((b == 0) & (c == 0))
        def _():
            dgs_ref[...] = jnp.zeros_like(dgs_ref)
            ddsk_ref[...] = jnp.zeros_like(ddsk_ref)
            ddtb_ref[...] = jnp.zeros_like(ddtb_ref)
            dalog_ref[...] = jnp.zeros_like(dalog_ref)

        q = _ssd_common(c_ref, pdt_ref, dtb_ref, alog_ref, e_ref, h)
        cp, sg, act, cs, a, dt, lo = q["cp"], q["sg"], q["act"], q["cs"], q["a"], q["dt"], q["lo"]
        ecs_l, dt_l = q["ecs_l"], q["dt_l"]
        ev = e_ref[...]
        xs = act[:, :d]
        xd = xs * dt_l
        row16 = lax.broadcasted_iota(jnp.int32, (CHUNK, h), 0)
        hid = lax.broadcasted_iota(jnp.int32, (1, h), 1)

        zv = z_ref[...].astype(F32)
        sz = _sigmoid(zv)
        siluz = zv * sz
        y2v = y2_ref[...].astype(F32)
        y3 = y2v * siluz
        dysv = dys_ref[...].astype(F32)
        dy3s = []
        for gi in range(SSM_GROUPS):
            gsl = slice(gi * gw, (gi + 1) * gw)
            seg = y3[:, gsl]
            dseg, dgt = _rms_bwd(seg, _rms(seg), gs_ref[:, gsl], dysv[:, gsl])
            dy3s.append(dseg)
            dgs_ref[:, gsl] += jnp.sum(dgt, axis=0, keepdims=True)
        dy3 = jnp.concatenate(dy3s, axis=1)
        dy = dy3 * siluz
        dz_ref[...] = (dy3 * y2v * (sz * (1.0 + zv * (1.0 - sz)))).astype(BF16)
        ddsk_ref[...] += jnp.sum(_head_sums(dy * xs, ev), axis=0, keepdims=True)

        dcs = jnp.zeros((CHUNK, h), F32)
        dxd_parts, yoff_parts, db_parts, dc_parts = [], [], [], []
        for g in range(SSM_GROUPS):
            bg = act[:, bc0 + g * D_STATE: bc0 + (g + 1) * D_STATE]
            cg = act[:, cc0 + g * D_STATE: cc0 + (g + 1) * D_STATE]
            bgb, cgb = bg.astype(BF16), cg.astype(BF16)
            s = _dot_nt(cgb, bgb)
            ds = jnp.zeros((CHUNK, CHUNK), F32)
            dbg = jnp.zeros((CHUNK, D_STATE), F32)
            dcg = jnp.zeros((CHUNK, D_STATE), F32)
            for jj in range(ppg):
                j = g * ppg + jj
                sl = slice(LANES * j, LANES * (j + 1))
                xdj = xd[:, sl]
                xdb = xdj.astype(BF16)
                x2 = jnp.concatenate([jnp.where(lo, xdj, 0.0), jnp.where(lo, 0.0, xdj)], axis=0).astype(BF16)
                dyj = dy[:, sl]
                dy2 = jnp.concatenate([jnp.where(lo, dyj, 0.0), jnp.where(lo, 0.0, dyj)], axis=0).astype(BF16)
                hpb = hp_ref[j]
                hprev = hpb.astype(F32)
                dhn = dh_ref[j]
                dhb = dhn.astype(BF16)
                dh2 = jnp.concatenate([jnp.where(lo, dhn, 0.0), jnp.where(lo, 0.0, dhn)], axis=0).astype(BF16)
                ecs = ecs_l[:, sl]
                gmat = (dyj * ecs).astype(BF16)
                yoff_parts.append(_dot(cgb, hpb) * ecs)
                dcg = dcg + _dot_nt(gmat, hpb)
                dh_ref[j] = dhn * ecs[CHUNK - 1:CHUNK] + _dot_tn(cgb, gmat)
                t2 = dhn * hprev
                dbw2 = _dot_nt(x2, dhb)
                dm2 = _dot_nt(dy2, xdb)
                ms, bws = [], []
                for idx, hh in enumerate((2 * j, 2 * j + 1)):
                    msk = lo if idx == 0 else jnp.logical_not(lo)
                    onehot = (hid == hh).astype(F32)
                    csc = cs[:, hh:hh + 1]
                    csb = jnp.broadcast_to(csc, (CHUNK, CHUNK))
                    lm = jnp.exp(jnp.where(q["tril"], csb - csb.T, -jnp.inf))
                    m = s * lm
                    cs_last = cs[CHUNK - 1:CHUNK, hh:hh + 1]
                    dte = jnp.exp(cs_last - csc)
                    ms.append(m)
                    bws.append(bg * dte)
                    dbw = dbw2[idx * CHUNK:(idx + 1) * CHUNK]
                    dbg = dbg + dbw * dte
                    qv = jnp.sum(dbw * bg, axis=-1, keepdims=True) * dte
                    dm = dm2[idx * CHUNK:(idx + 1) * CHUNK]
                    wm = dm * m
                    rc = jnp.sum(wm - wm.T, axis=-1, keepdims=True)
                    ds = ds + dm * lm
                    ddec = jnp.sum(jnp.where(msk, t2, 0.0)) * jnp.exp(cs_last)
                    last = jnp.sum(qv) + ddec
                    dcs = dcs + (rc - qv) * onehot + jnp.where(row16 == CHUNK - 1, last * onehot, 0.0)
                dxd_s = _dot(jnp.concatenate(bws, axis=1).astype(BF16), dh2)
                dxd_d = _dot_tn(jnp.concatenate(ms, axis=0).astype(BF16), dy2)
                dxd_parts.append(dxd_s + dxd_d)
            dsb = ds.astype(BF16)
            dc_parts.append(dcg + _dot(dsb, bgb))
            db_parts.append(dbg + _dot_tn(dsb, cgb))
        yoff_all = jnp.concatenate(yoff_parts, axis=1)
        dxd_all = jnp.concatenate(dxd_parts, axis=1)
        dcs = dcs + _head_sums(dy * yoff_all, ev)
        triu = (q["col"] >= q["row"]).astype(F32)
        dadt = jnp.dot(triu, dcs, precision=HIGHEST, preferred_element_type=F32)
        ddt = dadt * a + _head_sums(dxd_all * xs, ev)
        dalog_ref[...] += jnp.sum(dadt * dt, axis=0, keepdims=True) * a
        dpre = ddt * _sigmoid(q["pre"])
        ddtb_ref[...] += jnp.sum(dpre, axis=0, keepdims=True)
        dpdt_ref[...] = jnp.zeros_like(dpdt_ref)
        dpdt_ref[:, 0:h] = dpre.astype(BF16)
        dxs = dxd_all * dt_l + dy * dsk_ref[...]
        dact = jnp.concatenate([dxs] + db_parts + dc_parts, axis=1)
        dconv_ref[...] = (dact * (sg * (1.0 + cp * (1.0 - sg)))).astype(BF16)

    rchunk = lambda w, cb=0: pl.BlockSpec((CHUNK, w), lambda b, c, cb=cb: (b * nc + nc - 1 - c, cb))
    vec = lambda w: pl.BlockSpec((1, w), lambda b, c: (0, 0))
    hp_spec = pl.BlockSpec((None, None, npair, D_STATE, LANES), lambda b, c: (b, nc - 1 - c, 0, 0, 0))
    return _call(body, name=name, grid=(nseq, nc),
                 in_specs=[rchunk(xbc), rchunk(LANES), rchunk(d, 3), rchunk(d), hp_spec, rchunk(d, 1),
                           vec(h), vec(h), vec(d), vec(d), pl.BlockSpec((h, d), lambda b, c: (0, 0)),
                           pl.BlockSpec(memory_space=pl.ANY)],
                 out_specs=[rchunk(xbc), rchunk(d, 3), rchunk(LANES), vec(d), vec(h), vec(h), vec(h)],
                 out_shape=[jax.ShapeDtypeStruct((t, xbc), BF16), jax.ShapeDtypeStruct(dproj.shape, BF16),
                            jax.ShapeDtypeStruct((t, LANES), BF16), jax.ShapeDtypeStruct((1, d), F32),
                            jax.ShapeDtypeStruct((1, h), F32), jax.ShapeDtypeStruct((1, h), F32),
                            jax.ShapeDtypeStruct((1, h), F32)],
                 input_output_aliases={11: 1},
                 scratch_shapes=[pltpu.VMEM((npair, D_STATE, LANES), F32)],
                 compiler_params=_params(("arbitrary", "arbitrary")))(
                     cpre, pdt, pz, y2, hprev_all, dcat, dtb, alog, dsk_lane, gs, emat, dproj)


def _sum_adamw(parts, w, m, v, *, name, layer=None, outs=None):
    n, r, c = parts.shape
    tr = _pick_rows(r, 256)
    bc1 = 1.0 - ADAM_B1 ** ADAM_STEP
    bc2 = 1.0 - ADAM_B2 ** ADAM_STEP

    def body(p_ref, w_ref, m_ref, v_ref, *rest):
        g_ref, d_ref, mo_ref, vo_ref = rest[-4:]
        g = p_ref[0].astype(F32)
        for k in range(1, n):
            g = g + p_ref[k].astype(F32)
        mn = ADAM_B1 * m_ref[...] + (1.0 - ADAM_B1) * g
        vn = ADAM_B2 * v_ref[...] + (1.0 - ADAM_B2) * (g * g)
        g_ref[...] = g
        mo_ref[...] = mn
        vo_ref[...] = vn
        d_ref[...] = -ADAM_LR * ((mn / bc1) / (jnp.sqrt(vn / bc2) + ADAM_EPS) + ADAM_WD * w_ref[...])

    p_spec = pl.BlockSpec((n, tr, c), lambda i: (0, i, 0))
    if layer is None:
        blk = pl.BlockSpec((tr, c), lambda i: (i, 0))
        return _call(body, name=name, grid=(r // tr,), in_specs=[p_spec, blk, blk, blk], out_specs=[blk] * 4,
                     out_shape=[jax.ShapeDtypeStruct((r, c), F32)] * 4,
                     compiler_params=_params(("parallel",)))(parts, w, m, v)
    blk = pl.BlockSpec((None, tr, c), lambda i: (layer, i, 0))
    if outs is None:
        outs = [lax.empty(w.shape, F32) for _ in range(4)]
    return _call(body, name=name, grid=(r // tr,),
                 in_specs=[p_spec, blk, blk, blk] + [pl.BlockSpec(memory_space=pl.ANY)] * 4, out_specs=[blk] * 4,
                 out_shape=[jax.ShapeDtypeStruct(w.shape, F32)] * 4, input_output_aliases={4 + k: k for k in range(4)},
                 compiler_params=_params(("parallel",)))(parts, w, m, v, *outs)


def _assemble_cols(blocks, *, name):
    nb, r, c = blocks.shape
    width = -(-nb * c // LANES) * LANES
    tr = _pick_rows(r, 256)

    def body(b_ref, o_ref):
        pieces = [b_ref[j] for j in range(nb)]
        if width > nb * c:
            pieces.append(jnp.zeros((tr, width - nb * c), blocks.dtype))
        o_ref[...] = jnp.concatenate(pieces, axis=1)

    return _call(body, name=name, grid=(r // tr,), in_specs=[pl.BlockSpec((nb, tr, c), lambda i: (0, i, 0))],
                 out_specs=pl.BlockSpec((tr, width), lambda i: (i, 0)), out_shape=jax.ShapeDtypeStruct((r, width), blocks.dtype),
                 compiler_params=_params(("parallel",)))(blocks)


def _split_cols(pieces, c, *, name):
    r = pieces[0].shape[0]
    tr = _pick_rows(r, 256)
    n_in = len(pieces)

    def body(*refs):
        o_ref = refs[n_in]
        x = jnp.concatenate([p[...] for p in refs[:n_in]], axis=1) if n_in > 1 else refs[0][...]
        for j in range(N_DEV):
            o_ref[j] = x[:, c * j:c * (j + 1)]

    return _call(body, name=name, grid=(r // tr,),
                 in_specs=[pl.BlockSpec((tr, p.shape[1]), lambda i: (i, 0)) for p in pieces],
                 out_specs=pl.BlockSpec((N_DEV, tr, c), lambda i: (0, i, 0)),
                 out_shape=jax.ShapeDtypeStruct((N_DEV, r, c), pieces[0].dtype),
                 compiler_params=_params(("parallel",)))(*pieces)


def _sum_parts(parts, *, name):
    n, r, c = parts.shape
    tr = _pick_rows(r, 256)

    def body(p_ref, g_ref):
        g = p_ref[0].astype(F32)
        for k in range(1, n):
            g = g + p_ref[k].astype(F32)
        g_ref[...] = g

    return _call(body, name=name, grid=(r // tr,), in_specs=[pl.BlockSpec((n, tr, c), lambda i: (0, i, 0))],
                 out_specs=pl.BlockSpec((tr, c), lambda i: (i, 0)), out_shape=jax.ShapeDtypeStruct((r, c), F32),
                 compiler_params=_params(("parallel",)))(parts)


def _peers():
    x, y, c = lax.axis_index("x"), lax.axis_index("y"), lax.axis_index("c")
    me = 4 * x + 2 * y + c
    out = []
    for k in range(1, N_DEV):
        px = (1 - x) if (k >> 2) & 1 else x
        py = (1 - y) if (k >> 1) & 1 else y
        pc = (1 - c) if k & 1 else c
        out.append(((px, py, pc), 4 * px + 2 * py + pc))
    return me, out


def _exchange(src, *, gather, name):
    shape = src.shape if gather else src.shape[1:]

    def body(s_ref, o_ref, send_sems, recv_sems, local_sem):
        me, peers = _peers()
        mine = pltpu.make_async_copy(s_ref if gather else s_ref.at[me], o_ref.at[me], local_sem)
        mine.start()
        sends = []
        for k, (dev, pid) in enumerate(peers):
            cp = pltpu.make_async_remote_copy(
                src_ref=s_ref if gather else s_ref.at[pid], dst_ref=o_ref.at[me],
                send_sem=send_sems.at[k], recv_sem=recv_sems.at[k], device_id=dev, device_id_type=MESH)
            cp.start()
            sends.append(cp)
        for k, (dev, pid) in enumerate(peers):
            pltpu.make_async_remote_copy(
                src_ref=s_ref if gather else s_ref.at[pid], dst_ref=o_ref.at[pid],
                send_sem=send_sems.at[k], recv_sem=recv_sems.at[k], device_id=dev, device_id_type=MESH).wait_recv()
        for cp in sends:
            cp.wait_send()
        mine.wait()

    any_spec = pl.BlockSpec(memory_space=pl.ANY)
    return _call(body, name=name, in_specs=[any_spec], out_specs=any_spec,
                 out_shape=jax.ShapeDtypeStruct((N_DEV,) + tuple(shape), src.dtype),
                 scratch_shapes=[pltpu.SemaphoreType.DMA((N_DEV - 1,)), pltpu.SemaphoreType.DMA((N_DEV - 1,)),
                                 pltpu.SemaphoreType.DMA(())])(src)


_HBM = pl.BlockSpec(memory_space=pltpu.HBM)
_SEM = pl.BlockSpec(memory_space=pltpu.SEMAPHORE)
_EFFECT = pltpu.SideEffectType.DATAFLOW_SIDE_EFFECTING


def _split_copies(s_refs, l_refs, send_sems, recv_sems, gather, incoming):
    me, peers = _peers()
    local, remote = [], []
    for ti, (s_ref, l_ref) in enumerate(zip(s_refs, l_refs)):
        base = ti * N_DEV
        local.append(pltpu.make_async_copy(s_ref if gather else s_ref.at[me], l_ref.at[me], recv_sems.at[base + N_DEV - 1]))
        for k, (dev, pid) in enumerate(peers):
            sems = dict(send_sem=send_sems.at[base + k], recv_sem=recv_sems.at[base + k], device_id=dev, device_id_type=MESH)
            src = s_ref if gather else s_ref.at[pid]
            remote.append((
                pltpu.make_async_remote_copy(src_ref=src, dst_ref=l_ref.at[me], **sems),
                pltpu.make_async_remote_copy(src_ref=src, dst_ref=l_ref.at[pid], **sems) if incoming else None))
    return local, remote


def _exchange_start(srcs, *, gather, name, after=()):
    n = len(srcs)
    after = list(after)
    srcs = [pltpu.with_memory_space_constraint(s, pltpu.HBM) for s in srcs]
    lands = [pltpu.with_memory_space_constraint(
        lax.empty((N_DEV,) + tuple(s.shape if gather else s.shape[1:]), s.dtype), pltpu.HBM) for s in srcs]

    def body(*refs):
        s_refs, l_refs = refs[:n], refs[n:2 * n]
        outs = refs[2 * n + len(after):]
        send_sems, recv_sems, token = outs[0], outs[1], outs[-1]
        local, remote = _split_copies(s_refs, l_refs, send_sems, recv_sems, gather, incoming=False)
        for cp in local:
            cp.start()
        for out_cp, _ in remote:
            out_cp.start()
        token[...] = jnp.zeros_like(token)

    outs = _call(
        body, name=name,
        out_shape=(pltpu.SemaphoreType.DMA((n * N_DEV,)), pltpu.SemaphoreType.DMA((n * N_DEV,)),
                   *[pltpu.HBM(s.shape, s.dtype) for s in srcs], *[pltpu.HBM(l.shape, l.dtype) for l in lands],
                   jax.ShapeDtypeStruct((SUBLANES, LANES), F32)),
        in_specs=[_HBM] * (2 * n) + [pl.BlockSpec(memory_space=pl.ANY)] * len(after),
        out_specs=(_SEM, _SEM, *[_HBM] * (2 * n), pl.BlockSpec(memory_space=pltpu.VMEM)),
        input_output_aliases={k: k + 2 for k in range(2 * n)},
        compiler_params=pltpu.CompilerParams(has_side_effects=_EFFECT),
    )(*srcs, *lands, *after)
    return dict(n=n, gather=gather, sems=outs[:2], srcs=outs[2:2 + n], lands=outs[2 + n:2 + 2 * n]), outs[-1]


def _exchange_wait(state, after, *, name):
    n, gather = state["n"], state["gather"]
    after = list(after)

    def body(*refs):
        s_refs, l_refs = refs[:n], refs[n:2 * n]
        send_sems, recv_sems = refs[2 * n], refs[2 * n + 1]
        local, remote = _split_copies(s_refs, l_refs, send_sems, recv_sems, gather, incoming=True)
        for out_cp, in_cp in remote:
            out_cp.wait_send()
            in_cp.wait_recv()
        for cp in local:
            cp.wait()

    outs = _call(
        body, name=name,
        out_shape=tuple(pltpu.HBM(a.shape, a.dtype) for a in (*state["srcs"], *state["lands"])),
        in_specs=[_HBM] * (2 * n) + [_SEM, _SEM] + [pl.BlockSpec(memory_space=pl.ANY)] * len(after),
        out_specs=tuple([_HBM] * (2 * n)),
        input_output_aliases={k: k for k in range(2 * n)},
        compiler_params=pltpu.CompilerParams(has_side_effects=_EFFECT),
    )(*state["srcs"], *state["lands"], *state["sems"], *after)
    return outs[n:]


def _pack(arrs):
    flat = jnp.concatenate([a.reshape(-1).astype(F32) for a in arrs])
    pad = (-flat.shape[0]) % (SUBLANES * LANES)
    return jnp.pad(flat, (0, pad)).reshape(-1, LANES)


def _unpack(packed, shapes):
    flat = packed.reshape(-1)
    out, off = [], 0
    for s in shapes:
        n = 1
        for v in s:
            n *= v
        out.append(flat[off:off + n].reshape(s))
        off += n
    return out


SMALL = ("norm_mix_pre", "ssm_conv_b", "dt_bias", "a_log", "d_skip", "conv_out_norm", "ssm_out_norm",
         "norm_mix_post", "norm_mlp_pre", "norm_mlp_post", "conv_a_w", "ssm_conv_w")
BIG = ("w_in", "w_out", "w_up", "w_down")
ORDER = ("norm_mix_pre", "w_in", "conv_a_w", "ssm_conv_w", "ssm_conv_b", "dt_bias", "a_log", "d_skip",
         "conv_out_norm", "ssm_out_norm", "w_out", "norm_mix_post", "norm_mlp_pre", "w_up", "w_down", "norm_mlp_post")


def kernel(x, norm_mix_pre, w_in, conv_a_w, ssm_conv_w, ssm_conv_b, dt_bias, a_log, d_skip, conv_out_norm, ssm_out_norm, w_out, norm_mix_post, norm_mlp_pre, w_up, w_down, norm_mlp_post, loss_target, m_norm_mix_pre, m_w_in, m_conv_a_w, m_ssm_conv_w, m_ssm_conv_b, m_dt_bias, m_a_log, m_d_skip, m_conv_out_norm, m_ssm_out_norm, m_w_out, m_norm_mix_post, m_norm_mlp_pre, m_w_up, m_w_down, m_norm_mlp_post, v_norm_mix_pre, v_w_in, v_conv_a_w, v_ssm_conv_w, v_ssm_conv_b, v_dt_bias, v_a_log, v_d_skip, v_conv_out_norm, v_ssm_out_norm, v_w_out, v_norm_mix_post, v_norm_mlp_pre, v_w_up, v_w_down, v_norm_mlp_post):
    W = dict(norm_mix_pre=norm_mix_pre, w_in=w_in, conv_a_w=conv_a_w, ssm_conv_w=ssm_conv_w, ssm_conv_b=ssm_conv_b,
             dt_bias=dt_bias, a_log=a_log, d_skip=d_skip, conv_out_norm=conv_out_norm, ssm_out_norm=ssm_out_norm,
             w_out=w_out, norm_mix_post=norm_mix_post, norm_mlp_pre=norm_mlp_pre, w_up=w_up, w_down=w_down,
             norm_mlp_post=norm_mlp_post)
    M = dict(norm_mix_pre=m_norm_mix_pre, w_in=m_w_in, conv_a_w=m_conv_a_w, ssm_conv_w=m_ssm_conv_w,
             ssm_conv_b=m_ssm_conv_b, dt_bias=m_dt_bias, a_log=m_a_log, d_skip=m_d_skip,
             conv_out_norm=m_conv_out_norm, ssm_out_norm=m_ssm_out_norm, w_out=m_w_out,
             norm_mix_post=m_norm_mix_post, norm_mlp_pre=m_norm_mlp_pre, w_up=m_w_up, w_down=m_w_down,
             norm_mlp_post=m_norm_mlp_post)
    V = dict(norm_mix_pre=v_norm_mix_pre, w_in=v_w_in, conv_a_w=v_conv_a_w, ssm_conv_w=v_ssm_conv_w,
             ssm_conv_b=v_ssm_conv_b, dt_bias=v_dt_bias, a_log=v_a_log, d_skip=v_d_skip,
             conv_out_norm=v_conv_out_norm, ssm_out_norm=v_ssm_out_norm, w_out=v_w_out,
             norm_mix_post=v_norm_mix_post, norm_mlp_pre=v_norm_mlp_pre, w_up=v_w_up, w_down=v_w_down,
             norm_mlp_post=v_norm_mlp_post)

    nseq, seq, d = x.shape
    t = nseq * seq
    depth = w_in.shape[0]
    h = d // HEAD_DIM
    xbc = d + 2 * SSM_GROUPS * D_STATE
    in_cols = w_in.shape[2] * N_DEV
    d_mix = w_out.shape[1] * N_DEV
    d_ff = w_up.shape[2] * N_DEV
    me = 4 * lax.axis_index("x") + 2 * lax.axis_index("y") + lax.axis_index("c")
    ca_shard = conv_a_w.shape[2]
    sc_shard = ssm_conv_w.shape[2]

    tap_shapes = [conv_a_w.shape[1:], ssm_conv_w.shape[1:]]

    def gather_start(i, after=()):
        st_in, tok_in = _exchange_start([w_in[i].astype(BF16), _pack([conv_a_w[i], ssm_conv_w[i]])], gather=True,
                                        name=f"gather_start_in_{i}", after=after)
        st_rest, tok_rest = _exchange_start([W[n][i].astype(BF16) for n in ("w_out", "w_up", "w_down")], gather=True,
                                            name=f"gather_start_rest_{i}", after=[tok_in])
        return st_in, st_rest, tok_rest

    vec = lambda name, i: W[name][i].reshape(1, -1)
    emat = (lax.broadcasted_iota(jnp.int32, (h, d), 1) // HEAD_DIM == lax.broadcasted_iota(jnp.int32, (h, d), 0)).astype(F32)

    xcur = x.reshape(t, d)
    hcur = _norm_fwd(xcur, vec("norm_mix_pre", 0), name="norm_first")
    saved = []
    nxt = gather_start(0)
    for i in range(depth):
        st_in, st_rest, tok = nxt
        win_g, taps_g = _exchange_wait(st_in, [hcur, tok], name=f"gather_wait_in_{i}")
        win = _assemble_cols(win_g, name=f"assemble_w_in_{i}")
        taps_j = [_unpack(taps_g[j], tap_shapes) for j in range(N_DEV)]
        conv_a_i = jnp.concatenate([tj[0] for tj in taps_j], axis=1)
        ssm_conv_i = jnp.concatenate([tj[1] for tj in taps_j], axis=1)
        proj = _mm(hcur, win, n=4 * d + xbc, name=f"fwd_proj_{i}", out_dtypes=(BF16,))
        pdt = _mm(hcur, win, n=LANES, b_off=4 * d + xbc, name=f"fwd_proj_dt_{i}")
        ya, va = _conva_fwd(proj, conv_a_i, vec("conv_out_norm", i), d=d, seq=seq, name=f"fwd_conv_a_{i}")
        cpre = _convb_fwd(proj, ssm_conv_i, vec("ssm_conv_b", i), col0=4 * d, seq=seq, name=f"fwd_conv_b_{i}")
        dsk_lane = jnp.repeat(W["d_skip"][i], HEAD_DIM).reshape(1, d)
        cat, y2, hprev = _ssd_fwd(cpre, pdt, proj, ya, vec("dt_bias", i), vec("a_log", i), dsk_lane,
                                  vec("ssm_out_norm", i), emat, nseq=nseq, seq=seq, name=f"fwd_ssd_{i}")
        wout_g, wup_g, wdown_g = _exchange_wait(st_rest, [cat], name=f"gather_wait_rest_{i}")
        lw = dict(win=win, wout=wout_g.reshape(d_mix, d),
                  wup=_assemble_cols(wup_g, name=f"assemble_w_up_{i}"), wdown=wdown_g.reshape(d_ff, d),
                  conv_a=conv_a_i, ssm_conv=ssm_conv_i)
        after = []
        if i + 1 < depth:
            nxt = gather_start(i + 1, after=[wout_g])
            after = [nxt[2]]
        mix = _mm(cat, lw["wout"], name=f"fwd_out_{i}", after=after)
        x1, h2 = _resid_norm(xcur, mix, vec("norm_mix_post", i), vec("norm_mlp_pre", i), name=f"fwd_post_mix_{i}")
        f = _mm(h2, lw["wup"], name=f"fwd_up_{i}", out_dtypes=(BF16,), epi=_epi_relu2)
        dn = _mm(f, lw["wdown"], name=f"fwd_down_{i}")
        g_next = vec("norm_mix_pre", i + 1) if i + 1 < depth else vec("norm_mix_pre", 0)
        x2, hnext = _resid_norm(x1, dn, vec("norm_mlp_post", i), g_next, name=f"fwd_post_mlp_{i}")
        saved.append(dict(lw=lw, x0=xcur, h=hcur, proj=proj, pdt=pdt, va=va, cpre=cpre, y2=y2,
                          hprev=hprev, cat=cat, mix=mix, x1=x1, h2=h2, f=f, dn=dn, dsk_lane=dsk_lane))
        xcur, hcur = x2, hnext

    dx, loss_part = _loss_fwd_bwd(xcur, loss_target.reshape(t, d), name="loss")
    loss = lax.psum(loss_part[0, 0], ("x", "y", "c"))

    small_grads = {n: [None] * depth for n in SMALL}
    big_out = {n: None for n in BIG}

    def finish(pending, after):
        li, st_a, st_b = pending

        def update(n, parts):
            big_out[n] = _sum_adamw(parts, W[n], M[n], V[n], layer=li, outs=big_out[n], name=f"adamw_{n}_{li}")

        p_down, p_up = _exchange_wait(st_a, after, name=f"scatter_wait_a_{li}")
        update("w_down", p_down)
        update("w_up", p_up)
        p_out, p_in = _exchange_wait(st_b, after + [big_out["w_up"][0]], name=f"scatter_wait_b_{li}")
        update("w_out", p_out)
        update("w_in", p_in)

    pending = None
    for i in reversed(range(depth)):
        s = saved[i]
        lw = s["lw"]
        ddn, dg = _bwd_norm_out(s["dn"], vec("norm_mlp_post", i), dx, name=f"bwd_norm_mlp_post_{i}")
        small_grads["norm_mlp_post"][i] = dg
        dup = _mm(ddn, lw["wdown"], tb=True, name=f"bwd_down_dx_{i}", out_dtypes=(BF16,), epi=_epi_drelu2,
                  extras=(s["f"],))
        g_wdown = _mm(s["f"], ddn, ta=True, name=f"bwd_down_dw_{i}", out_dtypes=(BF16,))
        dh2 = _mm(dup, lw["wup"], tb=True, name=f"bwd_up_dx_{i}", out_dtypes=(BF16,))
        g_wup = _mm(s["h2"], dup, ta=True, name=f"bwd_up_dw_{i}", out_dtypes=(BF16,))
        st_a, tok_a = _exchange_start(
            [g_wdown.reshape(N_DEV, d_ff // N_DEV, d), _split_cols([g_wup], d_ff // N_DEV, name=f"split_g_w_up_{i}")],
            gather=False, name=f"scatter_start_a_{i}")
        dx1, dmix, dg_pre, dg_post = _bwd_norm_pair(s["x1"], [dh2], dx, s["mix"], vec("norm_mlp_pre", i) + tok_a[0:1, 0:1],
                                                    vec("norm_mix_post", i), name=f"bwd_norm_mix_post_{i}")
        small_grads["norm_mlp_pre"][i] = dg_pre
        small_grads["norm_mix_post"][i] = dg_post
        dcat = _mm(dmix, lw["wout"], tb=True, name=f"bwd_out_dx_{i}", out_dtypes=(BF16,))
        g_wout = _mm(s["cat"], dmix, ta=True, name=f"bwd_out_dw_{i}", out_dtypes=(BF16,))
        dproj, dcaw, dgca = _conva_bwd(dcat, s["proj"], s["va"], lw["conv_a"], vec("conv_out_norm", i), d=d, seq=seq,
                                       name=f"bwd_conv_a_{i}")
        small_grads["conv_a_w"][i] = dcaw
        small_grads["conv_out_norm"][i] = dgca
        dconv, dproj, dpdt, dgs, ddsk, ddtb, dalog = _ssd_bwd(
            s["cpre"], s["pdt"], s["proj"], s["y2"], s["hprev"], dcat, vec("dt_bias", i), vec("a_log", i),
            s["dsk_lane"], vec("ssm_out_norm", i), emat, dproj, nseq=nseq, seq=seq, name=f"bwd_ssd_{i}")
        small_grads["ssm_out_norm"][i] = dgs
        small_grads["d_skip"][i] = ddsk
        small_grads["dt_bias"][i] = ddtb
        small_grads["a_log"][i] = dalog
        dproj, dscw, dscb = _convb_bwd(dconv, s["proj"], lw["ssm_conv"], dproj, col0=4 * d, seq=seq,
                                       name=f"bwd_conv_b_{i}")
        small_grads["ssm_conv_w"][i] = dscw
        small_grads["ssm_conv_b"][i] = dscb
        g_win = _split_cols([
            _mm(s["h"], dproj, ta=True, name=f"bwd_proj_dw_{i}", out_dtypes=(BF16,)),
            _mm(s["h"], dpdt, ta=True, name=f"bwd_proj_dt_dw_{i}", out_dtypes=(BF16,))],
            in_cols // N_DEV, name=f"split_g_w_in_{i}")
        st_b, tok_b = _exchange_start(
            [g_wout.reshape(N_DEV, d_mix // N_DEV, d), g_win], gather=False, name=f"scatter_start_b_{i}")
        dh_parts = [_mm(dp, lw["win"], tb=True, b_koff=off, name=f"bwd_proj_{nm}dx_{i}", after=[tok_b], out_dtypes=(BF16,))
                    for nm, dp, off in (("", dproj, 0), ("dt_", dpdt, 4 * d + xbc))]
        dx, dg_in = _bwd_norm_in(s["x0"], dh_parts, dx1, vec("norm_mix_pre", i), name=f"bwd_norm_mix_pre_{i}")
        small_grads["norm_mix_pre"][i] = dg_in
        if pending is not None:
            finish(pending, [dx])
        pending = (i, st_a, st_b)

    grad_x = dx.reshape(nseq, seq, d)

    small_shapes_full = {n: (depth,) + tuple(small_grads[n][0].shape) for n in SMALL}
    gpack = _pack([jnp.stack(small_grads[n]) for n in SMALL])
    st_small, tok_small = _exchange_start([gpack], gather=True, name="allreduce_small_start")
    finish(pending, [dx, tok_small])
    gparts, = _exchange_wait(st_small, [big_out["w_in"][0]], name="allreduce_small_wait")

    def shard_of(n, full):
        if n == "conv_a_w":
            return lax.dynamic_slice_in_dim(full, me * ca_shard, ca_shard, axis=2)
        if n == "ssm_conv_w":
            return lax.dynamic_slice_in_dim(full, me * sc_shard, sc_shard, axis=2)
        return full.reshape(W[n].shape)

    gsum = _sum_parts(gparts, name="sum_small")
    gfull = _unpack(gsum, [small_shapes_full[n] for n in SMALL])
    gsmall = {n: shard_of(n, gf) for n, gf in zip(SMALL, gfull)}
    res = _sum_adamw(_pack([gsmall[n] for n in SMALL])[None], _pack([W[n] for n in SMALL]),
                     _pack([M[n] for n in SMALL]), _pack([V[n] for n in SMALL]), name="adamw_small")
    small_out = [dict(zip(SMALL, _unpack(r, [W[n].shape for n in SMALL]))) for r in res]

    def out_of(kind, n):
        return big_out[n][kind] if n in BIG else small_out[kind][n]

    return (loss, grad_x, *[out_of(k, n) for k in range(4) for n in ORDER])
```

```python
import functools

import jax
import jax.numpy as jnp
from jax import lax
from jax.experimental import pallas as pl
from jax.experimental.pallas import tpu as pltpu

F32 = jnp.float32
BF16 = jnp.bfloat16
HIGHEST = lax.Precision.HIGHEST
MESH = pl.DeviceIdType.MESH

EPS = 1e-6
HEAD_DIM = 64
D_STATE = 128
SSM_GROUPS = 2
CHUNK = 128
CONV_K = 3
SSM_CONV_K = 4
ADAM_LR = 0.001
ADAM_B1 = 0.9
ADAM_B2 = 0.999
ADAM_EPS = 1e-08
ADAM_WD = 0.01
ADAM_STEP = 10

N_DEV = 8
LANES = 128
SUBLANES = 8
VMEM_LIMIT = 48 * 1024 * 1024
ROW_TILE = 512
MM_TILE = 1024
MM_TILE_N = 1536
FUSED_ROWS = 512


def _params(sem):
    return pltpu.CompilerParams(dimension_semantics=sem, vmem_limit_bytes=VMEM_LIMIT)


def _call(body, **kw):
    return pl.pallas_call(body, **kw)


def _pick(n, cap):
    best = None
    for t in range(LANES, min(n, cap) + 1, LANES):
        if n % t == 0:
            best = t
    return best or n


def _pick_rows(n, cap):
    best = None
    for t in range(SUBLANES, min(n, cap) + 1, SUBLANES):
        if n % t == 0:
            best = t
    return best or n


def _sigmoid(x):
    return 1.0 / (1.0 + jnp.exp(-x))


def _softplus(x):
    return jnp.maximum(x, 0.0) + jnp.log1p(jnp.exp(-jnp.abs(x)))


def _rms(x):
    return lax.rsqrt(jnp.mean(x * x, axis=-1, keepdims=True) + EPS)


def _rms_bwd(x, r, g, dy):
    gy = dy * g
    dx = r * gy - x * (r * r * r) * jnp.mean(gy * x, axis=-1, keepdims=True)
    return dx, dy * x * r


def _full(shape):
    return pl.BlockSpec(shape, lambda *_: (0,) * len(shape))


def _mm(a, b, *, name, ta=False, tb=False, out_dtypes=(F32,), epi=None, extras=(), n=None, b_off=0, b_koff=0,
        after=(), vecs=(), n_sums=0, tm_cap=MM_TILE):
    m, k = (a.shape[1], a.shape[0]) if ta else a.shape
    if n is None:
        n = b.shape[0] if tb else b.shape[1]
    tm, tn, tk = _pick(m, tm_cap), _pick(n, MM_TILE_N), _pick(k, MM_TILE)
    while b_off % tn or n % tn:
        tn -= LANES
    if b_koff == 0 and k > MM_TILE:
        tk = _pick(k, MM_TILE_N)
    while b_koff % tk or k % tk:
        tk -= LANES
    nk = k // tk
    nm, nn = m // tm, n // tn
    jo = b_off // tn
    ko = b_koff // tk
    a_bytes = m * k * a.dtype.itemsize
    b_bytes = n * k * b.dtype.itemsize
    m_outer = a_bytes + nm * b_bytes <= b_bytes + nn * a_bytes
    ij = (lambda g0, g1: (g0, g1)) if m_outer else (lambda g0, g1: (g1, g0))
    grid = (nm, nn, nk) if m_outer else (nn, nm, nk)

    def a_map(g0, g1, kk):
        i, _ = ij(g0, g1)
        return (kk, i) if ta else (i, kk)

    def b_map(g0, g1, kk):
        _, j = ij(g0, g1)
        return (j + jo, kk + ko) if tb else (kk + ko, j + jo)

    def o_map(g0, g1, kk):
        return ij(g0, g1)

    a_spec = pl.BlockSpec((tk, tm) if ta else (tm, tk), a_map)
    b_spec = pl.BlockSpec((tn, tk) if tb else (tk, tn), b_map)
    o_spec = pl.BlockSpec((tm, tn), o_map)
    dims = (((0 if ta else 1,), (1 if tb else 0,)), ((), ()))
    n_ex = len(extras) + len(vecs)
    after = list(after)
    o0 = 2 + n_ex + len(after)
    n_out = len(out_dtypes)
    assert n_sums == 0 or nn == 1

    def finish(acc, ex, outs):
        res = (acc,) if epi is None else epi(acc, *[e[...] for e in ex])
        for o, r in zip(outs[:n_out], res[:n_out]):
            o[...] = r.astype(o.dtype)
        for o, r in zip(outs[n_out:], res[n_out:]):
            o[...] += jnp.sum(r, axis=0, keepdims=True)

    def zero_sums(outs, kk):
        if n_sums:
            @pl.when((pl.program_id(0) == 0) & (pl.program_id(1) == 0) & (kk == 0))
            def _():
                for o in outs[n_out:]:
                    o[...] = jnp.zeros_like(o)

    def body_single(*refs):
        a_ref, b_ref = refs[:2]
        zero_sums(refs[o0:], 0)
        acc = lax.dot_general(a_ref[...].astype(BF16), b_ref[...].astype(BF16), dims, preferred_element_type=F32)
        finish(acc, refs[2:2 + n_ex], refs[o0:])

    def body_multi(*refs):
        a_ref, b_ref = refs[:2]
        acc = refs[-1]
        kk = pl.program_id(2)
        zero_sums(refs[o0:-1], kk)

        @pl.when(kk == 0)
        def _():
            acc[...] = jnp.zeros_like(acc)

        acc[...] += lax.dot_general(a_ref[...].astype(BF16), b_ref[...].astype(BF16), dims, preferred_element_type=F32)

        @pl.when(kk == nk - 1)
        def _():
            finish(acc[...], refs[2:2 + n_ex], refs[o0:-1])

    v_spec = pl.BlockSpec((1, tn), lambda g0, g1, kk: (0, ij(g0, g1)[1]))
    outs = _call(
        body_single if nk == 1 else body_multi, name=name, grid=grid,
        in_specs=([a_spec, b_spec] + [o_spec] * len(extras) + [v_spec] * len(vecs)
                  + [pl.BlockSpec(memory_space=pl.ANY)] * len(after)),
        out_specs=[o_spec] * n_out + [v_spec] * n_sums,
        out_shape=[jax.ShapeDtypeStruct((m, n), dt) for dt in out_dtypes] + [jax.ShapeDtypeStruct((1, n), F32)] * n_sums,
        scratch_shapes=[] if nk == 1 else [pltpu.VMEM((tm, tn), F32)],
        compiler_params=_params(("parallel", "parallel", "arbitrary") if n_sums == 0 else ("arbitrary",) * 3),
    )(a, b, *extras, *vecs, *after)
    return outs[0] if len(outs) == 1 else outs


def _epi_resid_norm(acc, x, g_res, g_next):
    xn = x + acc * _rms(acc) * g_res
    return xn, xn * _rms(xn) * g_next, acc


def _epi_bwd_norm_pair(acc, x, dres, n, g_in, g_out):
    dxh, dgi = _rms_bwd(x, _rms(x), g_in, acc)
    dx = dres + dxh
    nv = n.astype(F32)
    dn, dgo = _rms_bwd(nv, _rms(nv), g_out, dx)
    return dx, dn, dgi, dgo


def _epi_bwd_norm_in(acc, x, dres, dh_more, g_in):
    dxh, dgi = _rms_bwd(x, _rms(x), g_in, acc + dh_more.astype(F32))
    return dres + dxh, dgi


def _epi_relu2(acc):
    r = jnp.maximum(acc, 0.0)
    return (r * r,)


def _epi_drelu2(acc, f):
    return (acc * (2.0 * jnp.sqrt(f.astype(F32))),)


def _norm_fwd(x, g, *, name):
    t, d = x.shape
    tt = _pick_rows(t, ROW_TILE)

    def body(x_ref, g_ref, h_ref):
        xv = x_ref[...]
        h_ref[...] = (xv * _rms(xv) * g_ref[...]).astype(BF16)

    row = pl.BlockSpec((tt, d), lambda i: (i, 0))
    return _call(body, name=name, grid=(t // tt,), in_specs=[row, _full((1, d))], out_specs=row,
                 out_shape=jax.ShapeDtypeStruct((t, d), BF16), compiler_params=_params(("parallel",)))(x, g)


def _resid_norm(x, n, g1, g2, *, name):
    t, d = x.shape
    tt = _pick_rows(t, ROW_TILE)

    def body(x_ref, n_ref, g1_ref, g2_ref, xo_ref, h_ref):
        nv = n_ref[...].astype(F32)
        xn = x_ref[...] + nv * _rms(nv) * g1_ref[...]
        xo_ref[...] = xn
        h_ref[...] = (xn * _rms(xn) * g2_ref[...]).astype(BF16)

    row = pl.BlockSpec((tt, d), lambda i: (i, 0))
    return _call(body, name=name, grid=(t // tt,), in_specs=[row, row, _full((1, d)), _full((1, d))],
                 out_specs=[row, row],
                 out_shape=[jax.ShapeDtypeStruct((t, d), F32), jax.ShapeDtypeStruct((t, d), BF16)],
                 compiler_params=_params(("parallel",)))(x, n, g1, g2)


def _loss_fwd_bwd(xf, target, *, name):
    t, d = xf.shape
    tt = _pick_rows(t, ROW_TILE)
    nt = t // tt

    def body(x_ref, t_ref, dy_ref, loss_ref, acc):
        i = pl.program_id(0)

        @pl.when(i == 0)
        def _():
            acc[...] = jnp.zeros_like(acc)

        e = x_ref[...] - t_ref[...]
        dy_ref[...] = e * (1.0 / d)
        acc[...] += jnp.sum(e * e, axis=0, keepdims=True)

        @pl.when(i == nt - 1)
        def _():
            loss_ref[...] = jnp.sum(acc[...], axis=-1, keepdims=True) * (0.5 / d)

    row = pl.BlockSpec((tt, d), lambda i: (i, 0))
    return _call(body, name=name, grid=(nt,), in_specs=[row, row], out_specs=[row, _full((1, 1))],
                 out_shape=[jax.ShapeDtypeStruct((t, d), F32), jax.ShapeDtypeStruct((1, 1), F32)],
                 scratch_shapes=[pltpu.VMEM((1, d), F32)], compiler_params=_params(("arbitrary",)))(xf, target)


def _bwd_norm_pair(xin, dh, dres, n, g_in, g_out, *, name):
    t, d = xin.shape
    tt = _pick_rows(t, ROW_TILE)
    n_dh = len(dh)

    def body(*refs):
        x_ref = refs[0]
        dh_refs = refs[1:1 + n_dh]
        dres_ref, n_ref, gi_ref, go_ref, dx_ref, dn_ref, dgi_ref, dgo_ref = refs[1 + n_dh:]
        i = pl.program_id(0)

        @pl.when(i == 0)
        def _():
            dgi_ref[...] = jnp.zeros_like(dgi_ref)
            dgo_ref[...] = jnp.zeros_like(dgo_ref)

        xv = x_ref[...]
        dhv = dh_refs[0][...].astype(F32)
        for r in dh_refs[1:]:
            dhv = dhv + r[...].astype(F32)
        dxh, dgi = _rms_bwd(xv, _rms(xv), gi_ref[...], dhv)
        dx = dres_ref[...] + dxh
        dx_ref[...] = dx
        dgi_ref[...] += jnp.sum(dgi, axis=0, keepdims=True)
        nv = n_ref[...].astype(F32)
        dn, dgo = _rms_bwd(nv, _rms(nv), go_ref[...], dx)
        dn_ref[...] = dn.astype(BF16)
        dgo_ref[...] += jnp.sum(dgo, axis=0, keepdims=True)

    row = pl.BlockSpec((tt, d), lambda i: (i, 0))
    vec = _full((1, d))
    return _call(body, name=name, grid=(t // tt,), in_specs=[row] * (n_dh + 3) + [vec, vec],
                 out_specs=[row, row, vec, vec],
                 out_shape=[jax.ShapeDtypeStruct((t, d), F32), jax.ShapeDtypeStruct((t, d), BF16),
                            jax.ShapeDtypeStruct((1, d), F32), jax.ShapeDtypeStruct((1, d), F32)],
                 compiler_params=_params(("arbitrary",)))(xin, *dh, dres, n, g_in, g_out)


def _bwd_norm_in(xin, dh, dres, g_in, *, name):
    t, d = xin.shape
    tt = _pick_rows(t, ROW_TILE)
    n_dh = len(dh)

    def body(*refs):
        x_ref = refs[0]
        dh_refs = refs[1:1 + n_dh]
        dres_ref, gi_ref, dx_ref, dgi_ref = refs[1 + n_dh:]
        i = pl.program_id(0)

        @pl.when(i == 0)
        def _():
            dgi_ref[...] = jnp.zeros_like(dgi_ref)

        xv = x_ref[...]
        dhv = dh_refs[0][...].astype(F32)
        for r in dh_refs[1:]:
            dhv = dhv + r[...].astype(F32)
        dxh, dgi = _rms_bwd(xv, _rms(xv), gi_ref[...], dhv)
        dx_ref[...] = dres_ref[...] + dxh
        dgi_ref[...] += jnp.sum(dgi, axis=0, keepdims=True)

    row = pl.BlockSpec((tt, d), lambda i: (i, 0))
    vec = _full((1, d))
    return _call(body, name=name, grid=(t // tt,), in_specs=[row] * (n_dh + 2) + [vec],
                 out_specs=[row, vec],
                 out_shape=[jax.ShapeDtypeStruct((t, d), F32), jax.ShapeDtypeStruct((1, d), F32)],
                 compiler_params=_params(("arbitrary",)))(xin, *dh, dres, g_in)


def _bwd_norm_out(n, g_out, dx, *, name):
    t, d = n.shape
    tt = _pick_rows(t, ROW_TILE)

    def body(n_ref, go_ref, dx_ref, dn_ref, dgo_ref):
        i = pl.program_id(0)

        @pl.when(i == 0)
        def _():
            dgo_ref[...] = jnp.zeros_like(dgo_ref)

        nv = n_ref[...].astype(F32)
        dn, dgo = _rms_bwd(nv, _rms(nv), go_ref[...], dx_ref[...])
        dn_ref[...] = dn.astype(BF16)
        dgo_ref[...] += jnp.sum(dgo, axis=0, keepdims=True)

    row = pl.BlockSpec((tt, d), lambda i: (i, 0))
    vec = _full((1, d))
    return _call(body, name=name, grid=(t // tt,), in_specs=[row, vec, row], out_specs=[row, vec],
                 out_shape=[jax.ShapeDtypeStruct((t, d), BF16), jax.ShapeDtypeStruct((1, d), F32)],
                 compiler_params=_params(("arbitrary",)))(n, g_out, dx)


def _shift_down(cur, halo, s):
    return jnp.concatenate([halo[SUBLANES - s:], cur[:cur.shape[0] - s]], axis=0)


def _shift_up(cur, halo, s):
    return jnp.concatenate([cur[s:], halo[:s]], axis=0)


def _conva_fwd(pa, w, g, *, d, seq, name):
    t = pa.shape[0]
    tt = _pick_rows(seq, ROW_TILE)
    tps = seq // tt

    def body(xa_ref, ca_ref, ba_ref, w_ref, g_ref, ya_ref, v_ref, carry):
        i = pl.program_id(0)

        @pl.when(i % tps == 0)
        def _():
            carry[...] = jnp.zeros_like(carry)

        u = ca_ref[...].astype(F32) * xa_ref[...].astype(F32)
        halo = carry[...]
        wv = w_ref[...]
        v = wv[2:3] * u + wv[1:2] * _shift_down(u, halo, 1) + wv[0:1] * _shift_down(u, halo, 2)
        carry[...] = u[tt - SUBLANES:]
        yp = ba_ref[...].astype(F32) * v
        ya_ref[...] = (yp * _rms(yp) * g_ref[...]).astype(BF16)
        v_ref[...] = v.astype(BF16)

    col = lambda c: pl.BlockSpec((tt, d), lambda i, c=c: (i, c))
    row = pl.BlockSpec((tt, d), lambda i: (i, 0))
    return _call(body, name=name, grid=(t // tt,),
                 in_specs=[col(0), col(1), col(2), _full((CONV_K, d)), _full((1, d))], out_specs=[row, row],
                 out_shape=[jax.ShapeDtypeStruct((t, d), BF16), jax.ShapeDtypeStruct((t, d), BF16)],
                 scratch_shapes=[pltpu.VMEM((SUBLANES, d), F32)],
                 compiler_params=_params(("arbitrary",)))(pa, pa, pa, w, g)


def _conva_bwd(dcat, pa, v, w, g, *, d, seq, name):
    t, width = pa.shape
    d3 = 3 * d
    tt = _pick_rows(seq, ROW_TILE)
    tps = seq // tt
    nt = t // tt

    def body(dya_ref, xa_ref, ca_ref, ba_ref, v_ref, w_ref, g_ref, dpa_ref, dw_ref, dg_ref, carry):
        i = pl.program_id(0)

        @pl.when(i == 0)
        def _():
            dw_ref[...] = jnp.zeros_like(dw_ref)
            dg_ref[...] = jnp.zeros_like(dg_ref)

        @pl.when(i % tps == 0)
        def _():
            carry[...] = jnp.zeros_like(carry)

        xa, ca, ba, vv = [r[...].astype(F32) for r in (xa_ref, ca_ref, ba_ref, v_ref)]
        yp = ba * vv
        dyp, dgt = _rms_bwd(yp, _rms(yp), g_ref[...], dya_ref[...].astype(F32))
        dg_ref[...] += jnp.sum(dgt, axis=0, keepdims=True)
        dv = dyp * ba
        halo = carry[...]
        dv1 = _shift_up(dv, halo, 1)
        dv2 = _shift_up(dv, halo, 2)
        carry[...] = dv[:SUBLANES]
        wv = w_ref[...]
        du = wv[2:3] * dv + wv[1:2] * dv1 + wv[0:1] * dv2
        u = ca * xa
        dw_ref[0:1, :] += jnp.sum(u * dv2, axis=0, keepdims=True)
        dw_ref[1:2, :] += jnp.sum(u * dv1, axis=0, keepdims=True)
        dw_ref[2:3, :] += jnp.sum(u * dv, axis=0, keepdims=True)
        dpa_ref[:, 0:d] = (du * ca).astype(BF16)
        dpa_ref[:, d:2 * d] = (du * xa).astype(BF16)
        dpa_ref[:, 2 * d:3 * d] = (dyp * vv).astype(BF16)

    rcol = lambda c: pl.BlockSpec((tt, d), lambda i, c=c: (nt - 1 - i, c))
    return _call(body, name=name, grid=(nt,),
                 in_specs=[rcol(0), rcol(0), rcol(1), rcol(2), rcol(0), _full((CONV_K, d)), _full((1, d))],
                 out_specs=[pl.BlockSpec((tt, d3), lambda i: (nt - 1 - i, 0)), _full((CONV_K, d)), _full((1, d))],
                 out_shape=[jax.ShapeDtypeStruct((t, width), BF16), jax.ShapeDtypeStruct((CONV_K, d), F32),
                            jax.ShapeDtypeStruct((1, d), F32)],
                 scratch_shapes=[pltpu.VMEM((SUBLANES, d), F32)],
                 compiler_params=_params(("arbitrary",)))(dcat, pa, pa, pa, v, w, g)


CONV_CH = 512


def _convb_fwd(proj, w, bias, *, col0, seq, name):
    t = proj.shape[0]
    c = w.shape[1]
    cb = _pick(c, CONV_CH)
    assert col0 % cb == 0
    tt = _pick_rows(seq, 2 * ROW_TILE)
    tps = seq // tt

    def body(p_ref, w_ref, b_ref, o_ref, carry):
        i = pl.program_id(1)

        @pl.when(i % tps == 0)
        def _():
            carry[...] = jnp.zeros_like(carry)

        p = p_ref[...].astype(F32)
        halo = carry[...]
        wv = w_ref[...]
        o = wv[3:4] * p + b_ref[...]
        for s in (1, 2, 3):
            o = o + wv[3 - s:4 - s] * _shift_down(p, halo, s)
        carry[...] = p[tt - SUBLANES:]
        o_ref[...] = o.astype(BF16)

    return _call(body, name=name, grid=(c // cb, t // tt),
                 in_specs=[pl.BlockSpec((tt, cb), lambda jc, i: (i, col0 // cb + jc)),
                           pl.BlockSpec((SSM_CONV_K, cb), lambda jc, i: (0, jc)), pl.BlockSpec((1, cb), lambda jc, i: (0, jc))],
                 out_specs=pl.BlockSpec((tt, cb), lambda jc, i: (i, jc)), out_shape=jax.ShapeDtypeStruct((t, c), BF16),
                 scratch_shapes=[pltpu.VMEM((SUBLANES, cb), F32)],
                 compiler_params=_params(("arbitrary", "arbitrary")))(proj, w, bias)


def _convb_bwd(dconv, proj, w, dproj, *, col0, seq, name):
    t, c = dconv.shape
    cb = _pick(c, CONV_CH)
    assert col0 % cb == 0
    tt = _pick_rows(seq, 2 * ROW_TILE)
    tps = seq // tt
    nt = t // tt

    def body(dc_ref, p_ref, w_ref, dproj_in, dp_ref, dw_ref, db_ref, carry):
        del dproj_in
        i = pl.program_id(1)

        @pl.when(i == 0)
        def _():
            dw_ref[...] = jnp.zeros_like(dw_ref)
            db_ref[...] = jnp.zeros_like(db_ref)

        @pl.when(i % tps == 0)
        def _():
            carry[...] = jnp.zeros_like(carry)

        dc = dc_ref[...].astype(F32)
        p = p_ref[...].astype(F32)
        halo = carry[...]
        wv = w_ref[...]
        dp = wv[3:4] * dc
        dw_ref[3:4, :] += jnp.sum(p * dc, axis=0, keepdims=True)
        for s in (1, 2, 3):
            dcs = _shift_up(dc, halo, s)
            dp = dp + wv[3 - s:4 - s] * dcs
            dw_ref[3 - s:4 - s, :] += jnp.sum(p * dcs, axis=0, keepdims=True)
        carry[...] = dc[:SUBLANES]
        db_ref[...] += jnp.sum(dc, axis=0, keepdims=True)
        dp_ref[...] = dp.astype(BF16)

    win_spec = pl.BlockSpec((tt, cb), lambda jc, i: (nt - 1 - i, col0 // cb + jc))
    taps = pl.BlockSpec((SSM_CONV_K, cb), lambda jc, i: (0, jc))
    return _call(body, name=name, grid=(c // cb, nt),
                 in_specs=[pl.BlockSpec((tt, cb), lambda jc, i: (nt - 1 - i, jc)), win_spec, taps,
                           pl.BlockSpec(memory_space=pl.ANY)],
                 out_specs=[win_spec, taps, pl.BlockSpec((1, cb), lambda jc, i: (0, jc))],
                 out_shape=[jax.ShapeDtypeStruct(dproj.shape, BF16), jax.ShapeDtypeStruct((SSM_CONV_K, c), F32),
                            jax.ShapeDtypeStruct((1, c), F32)],
                 input_output_aliases={3: 0},
                 scratch_shapes=[pltpu.VMEM((SUBLANES, cb), F32)],
                 compiler_params=_params(("arbitrary", "arbitrary")))(dconv, proj, w, dproj)


def _expand_heads(x, ev):
    return jnp.dot(x, ev, precision=HIGHEST, preferred_element_type=F32)


def _head_sums(v, ev):
    return lax.dot_general(v, ev, (((1,), (1,)), ((), ())), precision=HIGHEST, preferred_element_type=F32)


def _ssd_common(c_ref, pdt_ref, dtb_ref, alog_ref, e_ref, h):
    cp = c_ref[...].astype(F32)
    sg = _sigmoid(cp)
    act = cp * sg
    pre = pdt_ref[:, 0:h] + dtb_ref[...]
    dt = _softplus(pre)
    a = -jnp.exp(alog_ref[...])
    adt = dt * a
    row = lax.broadcasted_iota(jnp.int32, (CHUNK, CHUNK), 0)
    col = lax.broadcasted_iota(jnp.int32, (CHUNK, CHUNK), 1)
    tril = row >= col
    cs = jnp.dot(tril.astype(F32), adt, precision=HIGHEST, preferred_element_type=F32)
    ev = e_ref[...]
    dt_l = _expand_heads(dt, ev)
    ecs_l = jnp.exp(_expand_heads(cs, ev))
    return dict(cp=cp, sg=sg, act=act, pre=pre, dt=dt, a=a, cs=cs, dt_l=dt_l, ecs_l=ecs_l,
                tril=tril, row=row, col=col, lo=col < HEAD_DIM)


def _dot_nt(a, b):
    return lax.dot_general(a, b, (((1,), (1,)), ((), ())), preferred_element_type=F32)


def _dot_tn(a, b):
    return lax.dot_general(a, b, (((0,), (0,)), ((), ())), preferred_element_type=F32)


def _dot(a, b):
    return jnp.dot(a, b, preferred_element_type=F32)


def _ssd_fwd(cpre, pdt, pz, ya, dtb, alog, dsk_lane, gs, emat, *, nseq, seq, name):
    t, xbc = cpre.shape
    d = ya.shape[1]
    h = d // HEAD_DIM
    npair = h // 2
    ppg = npair // SSM_GROUPS
    nc = seq // CHUNK
    gw = d // SSM_GROUPS
    bc0 = d
    cc0 = d + SSM_GROUPS * D_STATE

    def body(c_ref, pdt_ref, z_ref, ya_ref, dtb_ref, alog_ref, dsk_ref, gs_ref, e_ref, cat_ref, y2_ref, hp_ref, h_ref):
        @pl.when(pl.program_id(0) == 0)
        def _():
            h_ref[...] = jnp.zeros_like(h_ref)

        for sq in range(nseq):
            one_seq(c_ref.at[sq], pdt_ref.at[sq], z_ref.at[sq], ya_ref.at[sq], dtb_ref, alog_ref, dsk_ref, gs_ref, e_ref,
                    cat_ref.at[sq], y2_ref.at[sq], hp_ref.at[sq], h_ref.at[sq])

    def one_seq(c_ref, pdt_ref, z_ref, ya_ref, dtb_ref, alog_ref, dsk_ref, gs_ref, e_ref, cat_ref, y2_ref, hp_ref, h_ref):
        q = _ssd_common(c_ref, pdt_ref, dtb_ref, alog_ref, e_ref, h)
        act, cs, lo, ecs_l = q["act"], q["cs"], q["lo"], q["ecs_l"]
        xs = act[:, :d]
        xd = xs * q["dt_l"]
        ys = []
        for g in range(SSM_GROUPS):
            bg = act[:, bc0 + g * D_STATE: bc0 + (g + 1) * D_STATE]
            cgb = act[:, cc0 + g * D_STATE: cc0 + (g + 1) * D_STATE].astype(BF16)
            s = _dot_nt(cgb, bg.astype(BF16))
            for jj in range(ppg):
                j = g * ppg + jj
                sl = slice(LANES * j, LANES * (j + 1))
                xdj = xd[:, sl]
                x2 = jnp.concatenate([jnp.where(lo, xdj, 0.0), jnp.where(lo, 0.0, xdj)], axis=0).astype(BF16)
                hprev = h_ref[j]
                hp_ref[j] = hprev.astype(BF16)
                ms, bws = [], []
                for hh in (2 * j, 2 * j + 1):
                    csc = cs[:, hh:hh + 1]
                    csb = jnp.broadcast_to(csc, (CHUNK, CHUNK))
                    ms.append(s * jnp.exp(jnp.where(q["tril"], csb - csb.T, -jnp.inf)))
                    bws.append(bg * jnp.exp(cs[CHUNK - 1:CHUNK, hh:hh + 1] - csc))
                ydiag = _dot(jnp.concatenate(ms, axis=1).astype(BF16), x2)
                st = _dot_tn(jnp.concatenate(bws, axis=0).astype(BF16), x2)
                ecs = ecs_l[:, sl]
                yoff = _dot(cgb, hprev.astype(BF16)) * ecs
                h_ref[j] = hprev * ecs[CHUNK - 1:CHUNK] + st
                ys.append(ydiag + yoff)
        y = jnp.concatenate(ys, axis=1) + dsk_ref[...] * xs
        y2_ref[...] = y.astype(BF16)
        zv = z_ref[...].astype(F32)
        y3 = y * (zv * _sigmoid(zv))
        cat_ref[:, 0:d] = ya_ref[...]
        for gi in range(SSM_GROUPS):
            seg = y3[:, gi * gw:(gi + 1) * gw]
            cat_ref[:, d + gi * gw:d + (gi + 1) * gw] = (seg * _rms(seg) * gs_ref[:, gi * gw:(gi + 1) * gw]).astype(BF16)

    chunk = lambda w, cb=0: pl.BlockSpec((nseq, CHUNK, w), lambda c, cb=cb: (0, c, cb))
    vec = lambda w: pl.BlockSpec((1, w), lambda c: (0, 0))
    hp_spec = pl.BlockSpec((nseq, None, npair, D_STATE, LANES), lambda c: (0, c, 0, 0, 0))
    per_seq = lambda a: a.reshape(nseq, seq, a.shape[1])
    cat, y2, hp = _call(
        body, name=name, grid=(nc,),
        in_specs=[chunk(xbc), chunk(LANES), chunk(d, 3), chunk(d), vec(h), vec(h), vec(d), vec(d),
                  pl.BlockSpec((h, d), lambda c: (0, 0))],
        out_specs=[chunk(2 * d), chunk(d), hp_spec],
        out_shape=[jax.ShapeDtypeStruct((nseq, seq, 2 * d), BF16), jax.ShapeDtypeStruct((nseq, seq, d), BF16),
                   jax.ShapeDtypeStruct((nseq, nc, npair, D_STATE, LANES), BF16)],
        scratch_shapes=[pltpu.VMEM((nseq, npair, D_STATE, LANES), F32)],
        compiler_params=_params(("arbitrary",)))(
            per_seq(cpre), per_seq(pdt), per_seq(pz), per_seq(ya), dtb, alog, dsk_lane, gs, emat)
    return cat.reshape(t, 2 * d), y2.reshape(t, d), hp


def _ssd_bwd(cpre, pdt, pz, y2, hprev_all, dcat, dtb, alog, dsk_lane, gs, emat, dproj, *, nseq, seq, name):
    t, xbc = cpre.shape
    d = y2.shape[1]
    h = d // HEAD_DIM
    npair = h // 2
    ppg = npair // SSM_GROUPS
    nc = seq // CHUNK
    gw = d // SSM_GROUPS
    bc0 = d
    cc0 = d + SSM_GROUPS * D_STATE

    def body(c_ref, pdt_ref, z_ref, y2_ref, hp_ref, dys_ref, dtb_ref, alog_ref, dsk_ref, gs_ref, e_ref, dproj_in,
             dconv_ref, dz_ref, dpdt_ref, dgs_ref, ddsk_ref, ddtb_ref, dalog_ref, dh_ref):
        del dproj_in
        b = pl.program_id(0)
        c = pl.program_id(1)

        @pl.when(c == 0)
        def _():
            dh_ref[...] = jnp.zeros_like(dh_ref)

        @pl.when((b == 0) & (c == 0))
        def _():
            dgs_ref[...] = jnp.zeros_like(dgs_ref)
            ddsk_ref[...] = jnp.zeros_like(ddsk_ref)
            ddtb_ref[...] = jnp.zeros_like(ddtb_ref)
            dalog_ref[...] = jnp.zeros_like(dalog_ref)

        q = _ssd_common(c_ref, pdt_ref, dtb_ref, alog_ref, e_ref, h)
        cp, sg, act, cs, a, dt, lo = q["cp"], q["sg"], q["act"], q["cs"], q["a"], q["dt"], q["lo"]
        ecs_l, dt_l = q["ecs_l"], q["dt_l"]
        ev = e_ref[...]
        xs = act[:, :d]
        xd = xs * dt_l
        row16 = lax.broadcasted_iota(jnp.int32, (CHUNK, h), 0)
        hid = lax.broadcasted_iota(jnp.int32, (1, h), 1)

        zv = z_ref[...].astype(F32)
        sz = _sigmoid(zv)
        siluz = zv * sz
        y2v = y2_ref[...].astype(F32)
        y3 = y2v * siluz
        dysv = dys_ref[...].astype(F32)
        dy3s = []
        for gi in range(SSM_GROUPS):
            gsl = slice(gi * gw, (gi + 1) * gw)
            seg = y3[:, gsl]
            dseg, dgt = _rms_bwd(seg, _rms(seg), gs_ref[:, gsl], dysv[:, gsl])
            dy3s.append(dseg)
            dgs_ref[:, gsl] += jnp.sum(dgt, axis=0, keepdims=True)
        dy3 = jnp.concatenate(dy3s, axis=1)
        dy = dy3 * siluz
        dz_ref[...] = (dy3 * y2v * (sz * (1.0 + zv * (1.0 - sz)))).astype(BF16)
        ddsk_ref[...] += jnp.sum(_head_sums(dy * xs, ev), axis=0, keepdims=True)

        dcs = jnp.zeros((CHUNK, h), F32)
        dxd_parts, yoff_parts, db_parts, dc_parts = [], [], [], []
        for g in range(SSM_GROUPS):
            bg = act[:, bc0 + g * D_STATE: bc0 + (g + 1) * D_STATE]
            cg = act[:, cc0 + g * D_STATE: cc0 + (g + 1) * D_STATE]
            bgb, cgb = bg.astype(BF16), cg.astype(BF16)
            s = _dot_nt(cgb, bgb)
            ds = jnp.zeros((CHUNK, CHUNK), F32)
            dbg = jnp.zeros((CHUNK, D_STATE), F32)
            dcg = jnp.zeros((CHUNK, D_STATE), F32)
            for jj in range(ppg):
                j = g * ppg + jj
                sl = slice(LANES * j, LANES * (j + 1))
                xdj = xd[:, sl]
                xdb = xdj.astype(BF16)
                x2 = jnp.concatenate([jnp.where(lo, xdj, 0.0), jnp.where(lo, 0.0, xdj)], axis=0).astype(BF16)
                dyj = dy[:, sl]
                dy2 = jnp.concatenate([jnp.where(lo, dyj, 0.0), jnp.where(lo, 0.0, dyj)], axis=0).astype(BF16)
                hpb = hp_ref[j]
                hprev = hpb.astype(F32)
                dhn = dh_ref[j]
                dhb = dhn.astype(BF16)
                dh2 = jnp.concatenate([jnp.where(lo, dhn, 0.0), jnp.where(lo, 0.0, dhn)], axis=0).astype(BF16)
                ecs = ecs_l[:, sl]
                gmat = (dyj * ecs).astype(BF16)
                yoff_parts.append(_dot(cgb, hpb) * ecs)
                dcg = dcg + _dot_nt(gmat, hpb)
                dh_ref[j] = dhn * ecs[CHUNK - 1:CHUNK] + _dot_tn(cgb, gmat)
                t2 = dhn * hprev
                dbw2 = _dot_nt(x2, dhb)
                dm2 = _dot_nt(dy2, xdb)
                ms, bws = [], []
                for idx, hh in enumerate((2 * j, 2 * j + 1)):
                    msk = lo if idx == 0 else jnp.logical_not(lo)
                    onehot = (hid == hh).astype(F32)
                    csc = cs[:, hh:hh + 1]
                    csb = jnp.broadcast_to(csc, (CHUNK, CHUNK))
                    lm = jnp.exp(jnp.where(q["tril"], csb - csb.T, -jnp.inf))
                    m = s * lm
                    cs_last = cs[CHUNK - 1:CHUNK, hh:hh + 1]
                    dte = jnp.exp(cs_last - csc)
                    ms.append(m)
                    bws.append(bg * dte)
                    dbw = dbw2[idx * CHUNK:(idx + 1) * CHUNK]
                    dbg = dbg + dbw * dte
                    qv = jnp.sum(dbw * bg, axis=-1, keepdims=True) * dte
                    dm = dm2[idx * CHUNK:(idx + 1) * CHUNK]
                    wm = dm * m
                    rc = jnp.sum(wm - wm.T, axis=-1, keepdims=True)
                    ds = ds + dm * lm
                    ddec = jnp.sum(jnp.where(msk, t2, 0.0)) * jnp.exp(cs_last)
                    last = jnp.sum(qv) + ddec
                    dcs = dcs + (rc - qv) * onehot + jnp.where(row16 == CHUNK - 1, last * onehot, 0.0)
                dxd_s = _dot(jnp.concatenate(bws, axis=1).astype(BF16), dh2)
                dxd_d = _dot_tn(jnp.concatenate(ms, axis=0).astype(BF16), dy2)
                dxd_parts.append(dxd_s + dxd_d)
            dsb = ds.astype(BF16)
            dc_parts.append(dcg + _dot(dsb, bgb))
            db_parts.append(dbg + _dot_tn(dsb, cgb))
        yoff_all = jnp.concatenate(yoff_parts, axis=1)
        dxd_all = jnp.concatenate(dxd_parts, axis=1)
        dcs = dcs + _head_sums(dy * yoff_all, ev)
        triu = (q["col"] >= q["row"]).astype(F32)
        dadt = jnp.dot(triu, dcs, precision=HIGHEST, preferred_element_type=F32)
        ddt = dadt * a + _head_sums(dxd_all * xs, ev)
        dalog_ref[...] += jnp.sum(dadt * dt, axis=0, keepdims=True) * a
        dpre = ddt * _sigmoid(q["pre"])
        ddtb_ref[...] += jnp.sum(dpre, axis=0, keepdims=True)
        dpdt_ref[...] = jnp.zeros_like(dpdt_ref)
        dpdt_ref[:, 0:h] = dpre.astype(BF16)
        dxs = dxd_all * dt_l + dy * dsk_ref[...]
        dact = jnp.concatenate([dxs] + db_parts + dc_parts, axis=1)
        dconv_ref[...] = (dact * (sg * (1.0 + cp * (1.0 - sg)))).astype(BF16)

    rchunk = lambda w, cb=0: pl.BlockSpec((CHUNK, w), lambda b, c, cb=cb: (b * nc + nc - 1 - c, cb))
    vec = lambda w: pl.BlockSpec((1, w), lambda b, c: (0, 0))
    hp_spec = pl.BlockSpec((None, None, npair, D_STATE, LANES), lambda b, c: (b, nc - 1 - c, 0, 0, 0))
    return _call(body, name=name, grid=(nseq, nc),
                 in_specs=[rchunk(xbc), rchunk(LANES), rchunk(d, 3), rchunk(d), hp_spec, rchunk(d, 1),
                           vec(h), vec(h), vec(d), vec(d), pl.BlockSpec((h, d), lambda b, c: (0, 0)),
                           pl.BlockSpec(memory_space=pl.ANY)],
                 out_specs=[rchunk(xbc), rchunk(d, 3), rchunk(LANES), vec(d), vec(h), vec(h), vec(h)],
                 out_shape=[jax.ShapeDtypeStruct((t, xbc), BF16), jax.ShapeDtypeStruct(dproj.shape, BF16),
                            jax.ShapeDtypeStruct((t, LANES), BF16), jax.ShapeDtypeStruct((1, d), F32),
                            jax.ShapeDtypeStruct((1, h), F32), jax.ShapeDtypeStruct((1, h), F32),
                            jax.ShapeDtypeStruct((1, h), F32)],
                 input_output_aliases={11: 1},
                 scratch_shapes=[pltpu.VMEM((npair, D_STATE, LANES), F32)],
                 compiler_params=_params(("arbitrary", "arbitrary")))(
                     cpre, pdt, pz, y2, hprev_all, dcat, dtb, alog, dsk_lane, gs, emat, dproj)


def _sum_adamw(parts, w, m, v, *, name, layer=None, outs=None):
    n, r, c = parts.shape
    tr = _pick_rows(r, 256)
    bc1 = 1.0 - ADAM_B1 ** ADAM_STEP
    bc2 = 1.0 - ADAM_B2 ** ADAM_STEP

    def body(p_ref, w_ref, m_ref, v_ref, *rest):
        g_ref, d_ref, mo_ref, vo_ref = rest[-4:]
        g = p_ref[0].astype(F32)
        for k in range(1, n):
            g = g + p_ref[k].astype(F32)
        mn = ADAM_B1 * m_ref[...] + (1.0 - ADAM_B1) * g
        vn = ADAM_B2 * v_ref[...] + (1.0 - ADAM_B2) * (g * g)
        g_ref[...] = g
        mo_ref[...] = mn
        vo_ref[...] = vn
        d_ref[...] = -ADAM_LR * ((mn / bc1) / (jnp.sqrt(vn / bc2) + ADAM_EPS) + ADAM_WD * w_ref[...])

    p_spec = pl.BlockSpec((n, tr, c), lambda i: (0, i, 0))
    if layer is None:
        blk = pl.BlockSpec((tr, c), lambda i: (i, 0))
        return _call(body, name=name, grid=(r // tr,), in_specs=[p_spec, blk, blk, blk], out_specs=[blk] * 4,
                     out_shape=[jax.ShapeDtypeStruct((r, c), F32)] * 4,
                     compiler_params=_params(("parallel",)))(parts, w, m, v)
    blk = pl.BlockSpec((None, tr, c), lambda i: (layer, i, 0))
    if outs is None:
        outs = [lax.empty(w.shape, F32) for _ in range(4)]
    return _call(body, name=name, grid=(r // tr,),
                 in_specs=[p_spec, blk, blk, blk] + [pl.BlockSpec(memory_space=pl.ANY)] * 4, out_specs=[blk] * 4,
                 out_shape=[jax.ShapeDtypeStruct(w.shape, F32)] * 4, input_output_aliases={4 + k: k for k in range(4)},
                 compiler_params=_params(("parallel",)))(parts, w, m, v, *outs)


def _assemble_cols(blocks, *, name):
    nb, r, c = blocks.shape
    width = -(-nb * c // LANES) * LANES
    tr = _pick_rows(r, 256)

    def body(b_ref, o_ref):
        pieces = [b_ref[j] for j in range(nb)]
        if width > nb * c:
            pieces.append(jnp.zeros((tr, width - nb * c), blocks.dtype))
        o_ref[...] = jnp.concatenate(pieces, axis=1)

    return _call(body, name=name, grid=(r // tr,), in_specs=[pl.BlockSpec((nb, tr, c), lambda i: (0, i, 0))],
                 out_specs=pl.BlockSpec((tr, width), lambda i: (i, 0)), out_shape=jax.ShapeDtypeStruct((r, width), blocks.dtype),
                 compiler_params=_params(("parallel",)))(blocks)


def _split_cols(pieces, c, *, name):
    r = pieces[0].shape[0]
    tr = _pick_rows(r, 256)
    n_in = len(pieces)

    def body(*refs):
        o_ref = refs[n_in]
        x = jnp.concatenate([p[...] for p in refs[:n_in]], axis=1) if n_in > 1 else refs[0][...]
        for j in range(N_DEV):
            o_ref[j] = x[:, c * j:c * (j + 1)]

    return _call(body, name=name, grid=(r // tr,),
                 in_specs=[pl.BlockSpec((tr, p.shape[1]), lambda i: (i, 0)) for p in pieces],
                 out_specs=pl.BlockSpec((N_DEV, tr, c), lambda i: (0, i, 0)),
                 out_shape=jax.ShapeDtypeStruct((N_DEV, r, c), pieces[0].dtype),
                 compiler_params=_params(("parallel",)))(*pieces)


def _sum_parts(parts, *, name):
    n, r, c = parts.shape
    tr = _pick_rows(r, 256)

    def body(p_ref, g_ref):
        g = p_ref[0].astype(F32)
        for k in range(1, n):
            g = g + p_ref[k].astype(F32)
        g_ref[...] = g

    return _call(body, name=name, grid=(r // tr,), in_specs=[pl.BlockSpec((n, tr, c), lambda i: (0, i, 0))],
                 out_specs=pl.BlockSpec((tr, c), lambda i: (i, 0)), out_shape=jax.ShapeDtypeStruct((r, c), F32),
                 compiler_params=_params(("parallel",)))(parts)


def _peers():
    x, y, c = lax.axis_index("x"), lax.axis_index("y"), lax.axis_index("c")
    me = 4 * x + 2 * y + c
    out = []
    for k in range(1, N_DEV):
        px = (1 - x) if (k >> 2) & 1 else x
        py = (1 - y) if (k >> 1) & 1 else y
        pc = (1 - c) if k & 1 else c
        out.append(((px, py, pc), 4 * px + 2 * py + pc))
    return me, out


def _exchange(src, *, gather, name):
    shape = src.shape if gather else src.shape[1:]

    def body(s_ref, o_ref, send_sems, recv_sems, local_sem):
        me, peers = _peers()
        mine = pltpu.make_async_copy(s_ref if gather else s_ref.at[me], o_ref.at[me], local_sem)
        mine.start()
        sends = []
        for k, (dev, pid) in enumerate(peers):
            cp = pltpu.make_async_remote_copy(
                src_ref=s_ref if gather else s_ref.at[pid], dst_ref=o_ref.at[me],
                send_sem=send_sems.at[k], recv_sem=recv_sems.at[k], device_id=dev, device_id_type=MESH)
            cp.start()
            sends.append(cp)
        for k, (dev, pid) in enumerate(peers):
            pltpu.make_async_remote_copy(
                src_ref=s_ref if gather else s_ref.at[pid], dst_ref=o_ref.at[pid],
                send_sem=send_sems.at[k], recv_sem=recv_sems.at[k], device_id=dev, device_id_type=MESH).wait_recv()
        for cp in sends:
            cp.wait_send()
        mine.wait()

    any_spec = pl.BlockSpec(memory_space=pl.ANY)
    return _call(body, name=name, in_specs=[any_spec], out_specs=any_spec,
                 out_shape=jax.ShapeDtypeStruct((N_DEV,) + tuple(shape), src.dtype),
                 scratch_shapes=[pltpu.SemaphoreType.DMA((N_DEV - 1,)), pltpu.SemaphoreType.DMA((N_DEV - 1,)),
                                 pltpu.SemaphoreType.DMA(())])(src)


_HBM = pl.BlockSpec(memory_space=pltpu.HBM)
_SEM = pl.BlockSpec(memory_space=pltpu.SEMAPHORE)
_EFFECT = pltpu.SideEffectType.DATAFLOW_SIDE_EFFECTING


def _split_copies(s_refs, l_refs, send_sems, recv_sems, gather, incoming):
    me, peers = _peers()
    local, remote = [], []
    for ti, (s_ref, l_ref) in enumerate(zip(s_refs, l_refs)):
        base = ti * N_DEV
        local.append(pltpu.make_async_copy(s_ref if gather else s_ref.at[me], l_ref.at[me], recv_sems.at[base + N_DEV - 1]))
        for k, (dev, pid) in enumerate(peers):
            sems = dict(send_sem=send_sems.at[base + k], recv_sem=recv_sems.at[base + k], device_id=dev, device_id_type=MESH)
            src = s_ref if gather else s_ref.at[pid]
            remote.append((
                pltpu.make_async_remote_copy(src_ref=src, dst_ref=l_ref.at[me], **sems),
                pltpu.make_async_remote_copy(src_ref=src, dst_ref=l_ref.at[pid], **sems) if incoming else None))
    return local, remote


def _exchange_start(srcs, *, gather, name, after=()):
    n = len(srcs)
    after = list(after)
    srcs = [pltpu.with_memory_space_constraint(s, pltpu.HBM) for s in srcs]
    lands = [pltpu.with_memory_space_constraint(
        lax.empty((N_DEV,) + tuple(s.shape if gather else s.shape[1:]), s.dtype), pltpu.HBM) for s in srcs]

    def body(*refs):
        s_refs, l_refs = refs[:n], refs[n:2 * n]
        outs = refs[2 * n + len(after):]
        send_sems, recv_sems, token = outs[0], outs[1], outs[-1]
        local, remote = _split_copies(s_refs, l_refs, send_sems, recv_sems, gather, incoming=False)
        for cp in local:
            cp.start()
        for out_cp, _ in remote:
            out_cp.start()
        token[...] = jnp.zeros_like(token)

    outs = _call(
        body, name=name,
        out_shape=(pltpu.SemaphoreType.DMA((n * N_DEV,)), pltpu.SemaphoreType.DMA((n * N_DEV,)),
                   *[pltpu.HBM(s.shape, s.dtype) for s in srcs], *[pltpu.HBM(l.shape, l.dtype) for l in lands],
                   jax.ShapeDtypeStruct((SUBLANES, LANES), F32)),
        in_specs=[_HBM] * (2 * n) + [pl.BlockSpec(memory_space=pl.ANY)] * len(after),
        out_specs=(_SEM, _SEM, *[_HBM] * (2 * n), pl.BlockSpec(memory_space=pltpu.VMEM)),
        input_output_aliases={k: k + 2 for k in range(2 * n)},
        compiler_params=pltpu.CompilerParams(has_side_effects=_EFFECT),
    )(*srcs, *lands, *after)
    return dict(n=n, gather=gather, sems=outs[:2], srcs=outs[2:2 + n], lands=outs[2 + n:2 + 2 * n]), outs[-1]


def _exchange_wait(state, after, *, name):
    n, gather = state["n"], state["gather"]
    after = list(after)

    def body(*refs):
        s_refs, l_refs = refs[:n], refs[n:2 * n]
        send_sems, recv_sems = refs[2 * n], refs[2 * n + 1]
        local, remote = _split_copies(s_refs, l_refs, send_sems, recv_sems, gather, incoming=True)
        for out_cp, in_cp in remote:
            out_cp.wait_send()
            in_cp.wait_recv()
        for cp in local:
            cp.wait()

    outs = _call(
        body, name=name,
        out_shape=tuple(pltpu.HBM(a.shape, a.dtype) for a in (*state["srcs"], *state["lands"])),
        in_specs=[_HBM] * (2 * n) + [_SEM, _SEM] + [pl.BlockSpec(memory_space=pl.ANY)] * len(after),
        out_specs=tuple([_HBM] * (2 * n)),
        input_output_aliases={k: k for k in range(2 * n)},
        compiler_params=pltpu.CompilerParams(has_side_effects=_EFFECT),
    )(*state["srcs"], *state["lands"], *state["sems"], *after)
    return outs[n:]


def _pack(arrs):
    flat = jnp.concatenate([a.reshape(-1).astype(F32) for a in arrs])
    pad = (-flat.shape[0]) % (SUBLANES * LANES)
    return jnp.pad(flat, (0, pad)).reshape(-1, LANES)


def _unpack(packed, shapes):
    flat = packed.reshape(-1)
    out, off = [], 0
    for s in shapes:
        n = 1
        for v in s:
            n *= v
        out.append(flat[off:off + n].reshape(s))
        off += n
    return out


SMALL = ("norm_mix_pre", "ssm_conv_b", "dt_bias", "a_log", "d_skip", "conv_out_norm", "ssm_out_norm",
         "norm_mix_post", "norm_mlp_pre", "norm_mlp_post", "conv_a_w", "ssm_conv_w")
BIG = ("w_in", "w_out", "w_up", "w_down")
ORDER = ("norm_mix_pre", "w_in", "conv_a_w", "ssm_conv_w", "ssm_conv_b", "dt_bias", "a_log", "d_skip",
         "conv_out_norm", "ssm_out_norm", "w_out", "norm_mix_post", "norm_mlp_pre", "w_up", "w_down", "norm_mlp_post")


def kernel(x, norm_mix_pre, w_in, conv_a_w, ssm_conv_w, ssm_conv_b, dt_bias, a_log, d_skip, conv_out_norm, ssm_out_norm, w_out, norm_mix_post, norm_mlp_pre, w_up, w_down, norm_mlp_post, loss_target, m_norm_mix_pre, m_w_in, m_conv_a_w, m_ssm_conv_w, m_ssm_conv_b, m_dt_bias, m_a_log, m_d_skip, m_conv_out_norm, m_ssm_out_norm, m_w_out, m_norm_mix_post, m_norm_mlp_pre, m_w_up, m_w_down, m_norm_mlp_post, v_norm_mix_pre, v_w_in, v_conv_a_w, v_ssm_conv_w, v_ssm_conv_b, v_dt_bias, v_a_log, v_d_skip, v_conv_out_norm, v_ssm_out_norm, v_w_out, v_norm_mix_post, v_norm_mlp_pre, v_w_up, v_w_down, v_norm_mlp_post):
    W = dict(norm_mix_pre=norm_mix_pre, w_in=w_in, conv_a_w=conv_a_w, ssm_conv_w=ssm_conv_w, ssm_conv_b=ssm_conv_b,
             dt_bias=dt_bias, a_log=a_log, d_skip=d_skip, conv_out_norm=conv_out_norm, ssm_out_norm=ssm_out_norm,
             w_out=w_out, norm_mix_post=norm_mix_post, norm_mlp_pre=norm_mlp_pre, w_up=w_up, w_down=w_down,
             norm_mlp_post=norm_mlp_post)
    M = dict(norm_mix_pre=m_norm_mix_pre, w_in=m_w_in, conv_a_w=m_conv_a_w, ssm_conv_w=m_ssm_conv_w,
             ssm_conv_b=m_ssm_conv_b, dt_bias=m_dt_bias, a_log=m_a_log, d_skip=m_d_skip,
             conv_out_norm=m_conv_out_norm, ssm_out_norm=m_ssm_out_norm, w_out=m_w_out,
             norm_mix_post=m_norm_mix_post, norm_mlp_pre=m_norm_mlp_pre, w_up=m_w_up, w_down=m_w_down,
             norm_mlp_post=m_norm_mlp_post)
    V = dict(norm_mix_pre=v_norm_mix_pre, w_in=v_w_in, conv_a_w=v_conv_a_w, ssm_conv_w=v_ssm_conv_w,
             ssm_conv_b=v_ssm_conv_b, dt_bias=v_dt_bias, a_log=v_a_log, d_skip=v_d_skip,
             conv_out_norm=v_conv_out_norm, ssm_out_norm=v_ssm_out_norm, w_out=v_w_out,
             norm_mix_post=v_norm_mix_post, norm_mlp_pre=v_norm_mlp_pre, w_up=v_w_up, w_down=v_w_down,
             norm_mlp_post=v_norm_mlp_post)

    nseq, seq, d = x.shape
    t = nseq * seq
    depth = w_in.shape[0]
    h = d // HEAD_DIM
    xbc = d + 2 * SSM_GROUPS * D_STATE
    in_cols = w_in.shape[2] * N_DEV
    d_mix = w_out.shape[1] * N_DEV
    d_ff = w_up.shape[2] * N_DEV
    me = 4 * lax.axis_index("x") + 2 * lax.axis_index("y") + lax.axis_index("c")
    ca_shard = conv_a_w.shape[2]
    sc_shard = ssm_conv_w.shape[2]

    tap_shapes = [conv_a_w.shape[1:], ssm_conv_w.shape[1:]]

    def gather_start(i, after=()):
        st_in, tok_in = _exchange_start([w_in[i].astype(BF16), _pack([conv_a_w[i], ssm_conv_w[i]])], gather=True,
                                        name=f"gather_start_in_{i}", after=after)
        st_rest, tok_rest = _exchange_start([W[n][i].astype(BF16) for n in ("w_out", "w_up", "w_down")], gather=True,
                                            name=f"gather_start_rest_{i}", after=[tok_in])
        return st_in, st_rest, tok_rest

    vec = lambda name, i: W[name][i].reshape(1, -1)
    emat = (lax.broadcasted_iota(jnp.int32, (h, d), 1) // HEAD_DIM == lax.broadcasted_iota(jnp.int32, (h, d), 0)).astype(F32)

    xcur = x.reshape(t, d)
    hcur = _norm_fwd(xcur, vec("norm_mix_pre", 0), name="norm_first")
    saved = []
    nxt = gather_start(0)
    for i in range(depth):
        st_in, st_rest, tok = nxt
        win_g, taps_g = _exchange_wait(st_in, [hcur, tok], name=f"gather_wait_in_{i}")
        win = _assemble_cols(win_g, name=f"assemble_w_in_{i}")
        taps_j = [_unpack(taps_g[j], tap_shapes) for j in range(N_DEV)]
        conv_a_i = jnp.concatenate([tj[0] for tj in taps_j], axis=1)
        ssm_conv_i = jnp.concatenate([tj[1] for tj in taps_j], axis=1)
        proj = _mm(hcur, win, n=4 * d + xbc, name=f"fwd_proj_{i}", out_dtypes=(BF16,))
        pdt = _mm(hcur, win, n=LANES, b_off=4 * d + xbc, name=f"fwd_proj_dt_{i}")
        ya, va = _conva_fwd(proj, conv_a_i, vec("conv_out_norm", i), d=d, seq=seq, name=f"fwd_conv_a_{i}")
        cpre = _convb_fwd(proj, ssm_conv_i, vec("ssm_conv_b", i), col0=4 * d, seq=seq, name=f"fwd_conv_b_{i}")
        dsk_lane = jnp.repeat(W["d_skip"][i], HEAD_DIM).reshape(1, d)
        cat, y2, hprev = _ssd_fwd(cpre, pdt, proj, ya, vec("dt_bias", i), vec("a_log", i), dsk_lane,
                                  vec("ssm_out_norm", i), emat, nseq=nseq, seq=seq, name=f"fwd_ssd_{i}")
        wout_g, wup_g, wdown_g = _exchange_wait(st_rest, [cat], name=f"gather_wait_rest_{i}")
        lw = dict(win=win, wout=wout_g.reshape(d_mix, d),
                  wup=_assemble_cols(wup_g, name=f"assemble_w_up_{i}"), wdown=wdown_g.reshape(d_ff, d),
                  conv_a=conv_a_i, ssm_conv=ssm_conv_i)
        after = []
        if i + 1 < depth:
            nxt = gather_start(i + 1, after=[wout_g])
            after = [nxt[2]]
        x1, h2, mix = _mm(cat, lw["wout"], name=f"fwd_out_{i}", after=after, out_dtypes=(F32, BF16, BF16),
                          epi=_epi_resid_norm, extras=(xcur,), vecs=(vec("norm_mix_post", i), vec("norm_mlp_pre", i)),
                          tm_cap=FUSED_ROWS)
        f = _mm(h2, lw["wup"], name=f"fwd_up_{i}", out_dtypes=(BF16,), epi=_epi_relu2)
        g_next = vec("norm_mix_pre", i + 1) if i + 1 < depth else vec("norm_mix_pre", 0)
        x2, hnext, dn = _mm(f, lw["wdown"], name=f"fwd_down_{i}", out_dtypes=(F32, BF16, BF16),
                            epi=_epi_resid_norm, extras=(x1,), vecs=(vec("norm_mlp_post", i), g_next), tm_cap=FUSED_ROWS)
        saved.append(dict(lw=lw, x0=xcur, h=hcur, proj=proj, pdt=pdt, va=va, cpre=cpre, y2=y2,
                          hprev=hprev, cat=cat, mix=mix, x1=x1, h2=h2, f=f, dn=dn, dsk_lane=dsk_lane))
        xcur, hcur = x2, hnext

    dx, loss_part = _loss_fwd_bwd(xcur, loss_target.reshape(t, d), name="loss")
    loss = lax.psum(loss_part[0, 0], ("x", "y", "c"))

    small_grads = {n: [None] * depth for n in SMALL}
    big_out = {n: None for n in BIG}

    def finish(pending, after):
        li, st_a, st_b = pending

        def update(n, parts):
            big_out[n] = _sum_adamw(parts, W[n], M[n], V[n], layer=li, outs=big_out[n], name=f"adamw_{n}_{li}")

        p_down, p_up = _exchange_wait(st_a, after, name=f"scatter_wait_a_{li}")
        update("w_down", p_down)
        update("w_up", p_up)
        p_out, p_in = _exchange_wait(st_b, after + [big_out["w_up"][0]], name=f"scatter_wait_b_{li}")
        update("w_out", p_out)
        update("w_in", p_in)

    pending = None
    for i in reversed(range(depth)):
        s = saved[i]
        lw = s["lw"]
        ddn, dg = _bwd_norm_out(s["dn"], vec("norm_mlp_post", i), dx, name=f"bwd_norm_mlp_post_{i}")
        small_grads["norm_mlp_post"][i] = dg
        dup = _mm(ddn, lw["wdown"], tb=True, name=f"bwd_down_dx_{i}", out_dtypes=(BF16,), epi=_epi_drelu2,
                  extras=(s["f"],))
        g_wdown = _mm(s["f"], ddn, ta=True, name=f"bwd_down_dw_{i}", out_dtypes=(BF16,))
        g_wup = _mm(s["h2"], dup, ta=True, name=f"bwd_up_dw_{i}", out_dtypes=(BF16,))
        st_a, tok_a = _exchange_start(
            [g_wdown.reshape(N_DEV, d_ff // N_DEV, d), _split_cols([g_wup], d_ff // N_DEV, name=f"split_g_w_up_{i}")],
            gather=False, name=f"scatter_start_a_{i}")
        dx1, dmix, dg_pre, dg_post = _mm(
            dup, lw["wup"], tb=True, name=f"bwd_up_dx_{i}", out_dtypes=(F32, BF16), epi=_epi_bwd_norm_pair,
            extras=(s["x1"], dx, s["mix"]), vecs=(vec("norm_mlp_pre", i), vec("norm_mix_post", i)), n_sums=2,
            after=[tok_a], tm_cap=FUSED_ROWS)
        small_grads["norm_mlp_pre"][i] = dg_pre
        small_grads["norm_mix_post"][i] = dg_post
        dcat = _mm(dmix, lw["wout"], tb=True, name=f"bwd_out_dx_{i}", out_dtypes=(BF16,))
        g_wout = _mm(s["cat"], dmix, ta=True, name=f"bwd_out_dw_{i}", out_dtypes=(BF16,))
        dproj, dcaw, dgca = _conva_bwd(dcat, s["proj"], s["va"], lw["conv_a"], vec("conv_out_norm", i), d=d, seq=seq,
                                       name=f"bwd_conv_a_{i}")
        small_grads["conv_a_w"][i] = dcaw
        small_grads["conv_out_norm"][i] = dgca
        dconv, dproj, dpdt, dgs, ddsk, ddtb, dalog = _ssd_bwd(
            s["cpre"], s["pdt"], s["proj"], s["y2"], s["hprev"], dcat, vec("dt_bias", i), vec("a_log", i),
            s["dsk_lane"], vec("ssm_out_norm", i), emat, dproj, nseq=nseq, seq=seq, name=f"bwd_ssd_{i}")
        small_grads["ssm_out_norm"][i] = dgs
        small_grads["d_skip"][i] = ddsk
        small_grads["dt_bias"][i] = ddtb
        small_grads["a_log"][i] = dalog
        dproj, dscw, dscb = _convb_bwd(dconv, s["proj"], lw["ssm_conv"], dproj, col0=4 * d, seq=seq,
                                       name=f"bwd_conv_b_{i}")
        small_grads["ssm_conv_w"][i] = dscw
        small_grads["ssm_conv_b"][i] = dscb
        g_win = _split_cols([
            _mm(s["h"], dproj, ta=True, name=f"bwd_proj_dw_{i}", out_dtypes=(BF16,)),
            _mm(s["h"], dpdt, ta=True, name=f"bwd_proj_dt_dw_{i}", out_dtypes=(BF16,))],
            in_cols // N_DEV, name=f"split_g_w_in_{i}")
        st_b, tok_b = _exchange_start(
            [g_wout.reshape(N_DEV, d_mix // N_DEV, d), g_win], gather=False, name=f"scatter_start_b_{i}")
        dh_dt = _mm(dpdt, lw["win"], tb=True, b_koff=4 * d + xbc, name=f"bwd_proj_dt_dx_{i}", after=[tok_b],
                    out_dtypes=(BF16,))
        dx, dg_in = _mm(dproj, lw["win"], tb=True, name=f"bwd_proj_dx_{i}", out_dtypes=(F32,), epi=_epi_bwd_norm_in,
                        extras=(s["x0"], dx1, dh_dt), vecs=(vec("norm_mix_pre", i),), n_sums=1, after=[tok_b],
                        tm_cap=FUSED_ROWS)
        small_grads["norm_mix_pre"][i] = dg_in
        if pending is not None:
            finish(pending, [dx])
        pending = (i, st_a, st_b)

    grad_x = dx.reshape(nseq, seq, d)

    small_shapes_full = {n: (depth,) + tuple(small_grads[n][0].shape) for n in SMALL}
    gpack = _pack([jnp.stack(small_grads[n]) for n in SMALL])
    st_small, tok_small = _exchange_start([gpack], gather=True, name="allreduce_small_start")
    finish(pending, [dx, tok_small])
    gparts, = _exchange_wait(st_small, [big_out["w_in"][0]], name="allreduce_small_wait")

    def shard_of(n, full):
        if n == "conv_a_w":
            return lax.dynamic_slice_in_dim(full, me * ca_shard, ca_shard, axis=2)
        if n == "ssm_conv_w":
            return lax.dynamic_slice_in_dim(full, me * sc_shard, sc_shard, axis=2)
        return full.reshape(W[n].shape)

    gsum = _sum_parts(gparts, name="sum_small")
    gfull = _unpack(gsum, [small_shapes_full[n] for n in SMALL])
    gsmall = {n: shard_of(n, gf) for n, gf in zip(SMALL, gfull)}
    res = _sum_adamw(_pack([gsmall[n] for n in SMALL])[None], _pack([W[n] for n in SMALL]),
                     _pack([M[n] for n in SMALL]), _pack([V[n] for n in SMALL]), name="adamw_small")
    small_out = [dict(zip(SMALL, _unpack(r, [W[n].shape for n in SMALL]))) for r in res]

    def out_of(kind, n):
        return big_out[n][kind] if n in BIG else small_out[kind][n]

    return (loss, grad_x, *[out_of(k, n) for k in range(4) for n in ORDER])
```

```python
import functools

import jax
import jax.numpy as jnp
from jax import lax
from jax.experimental import pallas as pl
from jax.experimental.pallas import tpu as pltpu

F32 = jnp.float32
BF16 = jnp.bfloat16
HIGHEST = lax.Precision.HIGHEST
MESH = pl.DeviceIdType.MESH

EPS = 1e-6
HEAD_DIM = 64
D_STATE = 128
SSM_GROUPS = 2
CHUNK = 128
CONV_K = 3
SSM_CONV_K = 4
ADAM_LR = 0.001
ADAM_B1 = 0.9
ADAM_B2 = 0.999
ADAM_EPS = 1e-08
ADAM_WD = 0.01
ADAM_STEP = 10

N_DEV = 8
LANES = 128
SUBLANES = 8
VMEM_LIMIT = 48 * 1024 * 1024
ROW_TILE = 512
MM_TILE = 1024
MM_TILE_N = 1536
FUSED_ROWS = 512


def _params(sem):
    return pltpu.CompilerParams(dimension_semantics=sem, vmem_limit_bytes=VMEM_LIMIT)


def _call(body, **kw):
    return pl.pallas_call(body, **kw)


def _pick(n, cap):
    best = None
    for t in range(LANES, min(n, cap) + 1, LANES):
        if n % t == 0:
            best = t
    return best or n


def _pick_rows(n, cap):
    best = None
    for t in range(SUBLANES, min(n, cap) + 1, SUBLANES):
        if n % t == 0:
            best = t
    return best or n


def _sigmoid(x):
    return 1.0 / (1.0 + jnp.exp(-x))


def _softplus(x):
    return jnp.maximum(x, 0.0) + jnp.log1p(jnp.exp(-jnp.abs(x)))


def _rms(x):
    return lax.rsqrt(jnp.mean(x * x, axis=-1, keepdims=True) + EPS)


def _rms_bwd(x, r, g, dy):
    gy = dy * g
    dx = r * gy - x * (r * r * r) * jnp.mean(gy * x, axis=-1, keepdims=True)
    return dx, dy * x * r


def _full(shape):
    return pl.BlockSpec(shape, lambda *_: (0,) * len(shape))


def _mm(a, b, *, name, ta=False, tb=False, out_dtypes=(F32,), epi=None, extras=(), n=None, b_off=0, b_koff=0,
        after=(), vecs=(), n_sums=0, tm_cap=MM_TILE):
    m, k = (a.shape[1], a.shape[0]) if ta else a.shape
    if n is None:
        n = b.shape[0] if tb else b.shape[1]
    tm, tn, tk = _pick(m, tm_cap), _pick(n, MM_TILE_N), _pick(k, MM_TILE)
    while b_off % tn or n % tn:
        tn -= LANES
    if b_koff == 0 and k > MM_TILE:
        tk = _pick(k, MM_TILE_N)
    while b_koff % tk or k % tk:
        tk -= LANES
    nk = k // tk
    nm, nn = m // tm, n // tn
    jo = b_off // tn
    ko = b_koff // tk
    a_bytes = m * k * a.dtype.itemsize
    b_bytes = n * k * b.dtype.itemsize
    m_outer = a_bytes + nm * b_bytes <= b_bytes + nn * a_bytes
    ij = (lambda g0, g1: (g0, g1)) if m_outer else (lambda g0, g1: (g1, g0))
    grid = (nm, nn, nk) if m_outer else (nn, nm, nk)

    def a_map(g0, g1, kk):
        i, _ = ij(g0, g1)
        return (kk, i) if ta else (i, kk)

    def b_map(g0, g1, kk):
        _, j = ij(g0, g1)
        return (j + jo, kk + ko) if tb else (kk + ko, j + jo)

    def o_map(g0, g1, kk):
        return ij(g0, g1)

    a_spec = pl.BlockSpec((tk, tm) if ta else (tm, tk), a_map)
    b_spec = pl.BlockSpec((tn, tk) if tb else (tk, tn), b_map)
    o_spec = pl.BlockSpec((tm, tn), o_map)
    dims = (((0 if ta else 1,), (1 if tb else 0,)), ((), ()))
    n_ex = len(extras) + len(vecs)
    after = list(after)
    o0 = 2 + n_ex + len(after)
    n_out = len(out_dtypes)
    assert n_sums == 0 or nn == 1

    def finish(acc, ex, outs):
        res = (acc,) if epi is None else epi(acc, *[e[...] for e in ex])
        for o, r in zip(outs[:n_out], res[:n_out]):
            o[...] = r.astype(o.dtype)
        for o, r in zip(outs[n_out:], res[n_out:]):
            o[...] += jnp.sum(r, axis=0, keepdims=True)

    def zero_sums(outs, kk):
        if n_sums:
            @pl.when((pl.program_id(0) == 0) & (pl.program_id(1) == 0) & (kk == 0))
            def _():
                for o in outs[n_out:]:
                    o[...] = jnp.zeros_like(o)

    def body_single(*refs):
        a_ref, b_ref = refs[:2]
        zero_sums(refs[o0:], 0)
        acc = lax.dot_general(a_ref[...].astype(BF16), b_ref[...].astype(BF16), dims, preferred_element_type=F32)
        finish(acc, refs[2:2 + n_ex], refs[o0:])

    def body_multi(*refs):
        a_ref, b_ref = refs[:2]
        acc = refs[-1]
        kk = pl.program_id(2)
        zero_sums(refs[o0:-1], kk)

        @pl.when(kk == 0)
        def _():
            acc[...] = jnp.zeros_like(acc)

        acc[...] += lax.dot_general(a_ref[...].astype(BF16), b_ref[...].astype(BF16), dims, preferred_element_type=F32)

        @pl.when(kk == nk - 1)
        def _():
            finish(acc[...], refs[2:2 + n_ex], refs[o0:-1])

    v_spec = pl.BlockSpec((1, tn), lambda g0, g1, kk: (0, ij(g0, g1)[1]))
    outs = _call(
        body_single if nk == 1 else body_multi, name=name, grid=grid,
        in_specs=([a_spec, b_spec] + [o_spec] * len(extras) + [v_spec] * len(vecs)
                  + [pl.BlockSpec(memory_space=pl.ANY)] * len(after)),
        out_specs=[o_spec] * n_out + [v_spec] * n_sums,
        out_shape=[jax.ShapeDtypeStruct((m, n), dt) for dt in out_dtypes] + [jax.ShapeDtypeStruct((1, n), F32)] * n_sums,
        scratch_shapes=[] if nk == 1 else [pltpu.VMEM((tm, tn), F32)],
        compiler_params=_params(("parallel", "parallel", "arbitrary") if n_sums == 0 else ("arbitrary",) * 3),
    )(a, b, *extras, *vecs, *after)
    return outs[0] if len(outs) == 1 else outs


def _epi_resid_norm(acc, x, g_res, g_next):
    xn = x + acc * _rms(acc) * g_res
    return xn, xn * _rms(xn) * g_next, acc


def _epi_bwd_norm_pair(acc, x, dres, n, g_in, g_out):
    dxh, dgi = _rms_bwd(x, _rms(x), g_in, acc)
    dx = dres + dxh
    nv = n.astype(F32)
    dn, dgo = _rms_bwd(nv, _rms(nv), g_out, dx)
    return dx, dn, dgi, dgo


def _epi_bwd_norm_in(acc, x, dres, dh_more, g_in):
    dxh, dgi = _rms_bwd(x, _rms(x), g_in, acc + dh_more.astype(F32))
    return dres + dxh, dgi


def _epi_relu2(acc):
    r = jnp.maximum(acc, 0.0)
    return (r * r,)


def _epi_drelu2(acc, f):
    return (acc * (2.0 * jnp.sqrt(f.astype(F32))),)


def _norm_fwd(x, g, *, name):
    t, d = x.shape
    tt = _pick_rows(t, ROW_TILE)

    def body(x_ref, g_ref, h_ref):
        xv = x_ref[...]
        h_ref[...] = (xv * _rms(xv) * g_ref[...]).astype(BF16)

    row = pl.BlockSpec((tt, d), lambda i: (i, 0))
    return _call(body, name=name, grid=(t // tt,), in_specs=[row, _full((1, d))], out_specs=row,
                 out_shape=jax.ShapeDtypeStruct((t, d), BF16), compiler_params=_params(("parallel",)))(x, g)


def _resid_norm(x, n, g1, g2, *, name):
    t, d = x.shape
    tt = _pick_rows(t, ROW_TILE)

    def body(x_ref, n_ref, g1_ref, g2_ref, xo_ref, h_ref):
        nv = n_ref[...].astype(F32)
        xn = x_ref[...] + nv * _rms(nv) * g1_ref[...]
        xo_ref[...] = xn
        h_ref[...] = (xn * _rms(xn) * g2_ref[...]).astype(BF16)

    row = pl.BlockSpec((tt, d), lambda i: (i, 0))
    return _call(body, name=name, grid=(t // tt,), in_specs=[row, row, _full((1, d)), _full((1, d))],
                 out_specs=[row, row],
                 out_shape=[jax.ShapeDtypeStruct((t, d), F32), jax.ShapeDtypeStruct((t, d), BF16)],
                 compiler_params=_params(("parallel",)))(x, n, g1, g2)


def _loss_fwd_bwd(xf, target, *, name):
    t, d = xf.shape
    tt = _pick_rows(t, ROW_TILE)
    nt = t // tt

    def body(x_ref, t_ref, dy_ref, loss_ref, acc):
        i = pl.program_id(0)

        @pl.when(i == 0)
        def _():
            acc[...] = jnp.zeros_like(acc)

        e = x_ref[...] - t_ref[...]
        dy_ref[...] = e * (1.0 / d)
        acc[...] += jnp.sum(e * e, axis=0, keepdims=True)

        @pl.when(i == nt - 1)
        def _():
            loss_ref[...] = jnp.sum(acc[...], axis=-1, keepdims=True) * (0.5 / d)

    row = pl.BlockSpec((tt, d), lambda i: (i, 0))
    return _call(body, name=name, grid=(nt,), in_specs=[row, row], out_specs=[row, _full((1, 1))],
                 out_shape=[jax.ShapeDtypeStruct((t, d), F32), jax.ShapeDtypeStruct((1, 1), F32)],
                 scratch_shapes=[pltpu.VMEM((1, d), F32)], compiler_params=_params(("arbitrary",)))(xf, target)


def _bwd_norm_pair(xin, dh, dres, n, g_in, g_out, *, name):
    t, d = xin.shape
    tt = _pick_rows(t, ROW_TILE)
    n_dh = len(dh)

    def body(*refs):
        x_ref = refs[0]
        dh_refs = refs[1:1 + n_dh]
        dres_ref, n_ref, gi_ref, go_ref, dx_ref, dn_ref, dgi_ref, dgo_ref = refs[1 + n_dh:]
        i = pl.program_id(0)

        @pl.when(i == 0)
        def _():
            dgi_ref[...] = jnp.zeros_like(dgi_ref)
            dgo_ref[...] = jnp.zeros_like(dgo_ref)

        xv = x_ref[...]
        dhv = dh_refs[0][...].astype(F32)
        for r in dh_refs[1:]:
            dhv = dhv + r[...].astype(F32)
        dxh, dgi = _rms_bwd(xv, _rms(xv), gi_ref[...], dhv)
        dx = dres_ref[...] + dxh
        dx_ref[...] = dx
        dgi_ref[...] += jnp.sum(dgi, axis=0, keepdims=True)
        nv = n_ref[...].astype(F32)
        dn, dgo = _rms_bwd(nv, _rms(nv), go_ref[...], dx)
        dn_ref[...] = dn.astype(BF16)
        dgo_ref[...] += jnp.sum(dgo, axis=0, keepdims=True)

    row = pl.BlockSpec((tt, d), lambda i: (i, 0))
    vec = _full((1, d))
    return _call(body, name=name, grid=(t // tt,), in_specs=[row] * (n_dh + 3) + [vec, vec],
                 out_specs=[row, row, vec, vec],
                 out_shape=[jax.ShapeDtypeStruct((t, d), F32), jax.ShapeDtypeStruct((t, d), BF16),
                            jax.ShapeDtypeStruct((1, d), F32), jax.ShapeDtypeStruct((1, d), F32)],
                 compiler_params=_params(("arbitrary",)))(xin, *dh, dres, n, g_in, g_out)


def _bwd_norm_in(xin, dh, dres, g_in, *, name):
    t, d = xin.shape
    tt = _pick_rows(t, ROW_TILE)
    n_dh = len(dh)

    def body(*refs):
        x_ref = refs[0]
        dh_refs = refs[1:1 + n_dh]
        dres_ref, gi_ref, dx_ref, dgi_ref = refs[1 + n_dh:]
        i = pl.program_id(0)

        @pl.when(i == 0)
        def _():
            dgi_ref[...] = jnp.zeros_like(dgi_ref)

        xv = x_ref[...]
        dhv = dh_refs[0][...].astype(F32)
        for r in dh_refs[1:]:
            dhv = dhv + r[...].astype(F32)
        dxh, dgi = _rms_bwd(xv, _rms(xv), gi_ref[...], dhv)
        dx_ref[...] = dres_ref[...] + dxh
        dgi_ref[...] += jnp.sum(dgi, axis=0, keepdims=True)

    row = pl.BlockSpec((tt, d), lambda i: (i, 0))
    vec = _full((1, d))
    return _call(body, name=name, grid=(t // tt,), in_specs=[row] * (n_dh + 2) + [vec],
                 out_specs=[row, vec],
                 out_shape=[jax.ShapeDtypeStruct((t, d), F32), jax.ShapeDtypeStruct((1, d), F32)],
                 compiler_params=_params(("arbitrary",)))(xin, *dh, dres, g_in)


def _bwd_norm_out(n, g_out, dx, *, name):
    t, d = n.shape
    tt = _pick_rows(t, ROW_TILE)

    def body(n_ref, go_ref, dx_ref, dn_ref, dgo_ref):
        i = pl.program_id(0)

        @pl.when(i == 0)
        def _():
            dgo_ref[...] = jnp.zeros_like(dgo_ref)

        nv = n_ref[...].astype(F32)
        dn, dgo = _rms_bwd(nv, _rms(nv), go_ref[...], dx_ref[...])
        dn_ref[...] = dn.astype(BF16)
        dgo_ref[...] += jnp.sum(dgo, axis=0, keepdims=True)

    row = pl.BlockSpec((tt, d), lambda i: (i, 0))
    vec = _full((1, d))
    return _call(body, name=name, grid=(t // tt,), in_specs=[row, vec, row], out_specs=[row, vec],
                 out_shape=[jax.ShapeDtypeStruct((t, d), BF16), jax.ShapeDtypeStruct((1, d), F32)],
                 compiler_params=_params(("arbitrary",)))(n, g_out, dx)


def _shift_down(cur, halo, s):
    return jnp.concatenate([halo[SUBLANES - s:], cur[:cur.shape[0] - s]], axis=0)


def _shift_up(cur, halo, s):
    return jnp.concatenate([cur[s:], halo[:s]], axis=0)


def _conva_fwd(pa, w, g, *, d, seq, name):
    t = pa.shape[0]
    tt = _pick_rows(seq, ROW_TILE)
    tps = seq // tt

    def body(xa_ref, ca_ref, ba_ref, w_ref, g_ref, ya_ref, v_ref, carry):
        i = pl.program_id(0)

        @pl.when(i % tps == 0)
        def _():
            carry[...] = jnp.zeros_like(carry)

        u = ca_ref[...].astype(F32) * xa_ref[...].astype(F32)
        halo = carry[...]
        wv = w_ref[...]
        v = wv[2:3] * u + wv[1:2] * _shift_down(u, halo, 1) + wv[0:1] * _shift_down(u, halo, 2)
        carry[...] = u[tt - SUBLANES:]
        yp = ba_ref[...].astype(F32) * v
        ya_ref[...] = (yp * _rms(yp) * g_ref[...]).astype(BF16)
        v_ref[...] = v.astype(BF16)

    col = lambda c: pl.BlockSpec((tt, d), lambda i, c=c: (i, c))
    row = pl.BlockSpec((tt, d), lambda i: (i, 0))
    return _call(body, name=name, grid=(t // tt,),
                 in_specs=[col(0), col(1), col(2), _full((CONV_K, d)), _full((1, d))], out_specs=[row, row],
                 out_shape=[jax.ShapeDtypeStruct((t, d), BF16), jax.ShapeDtypeStruct((t, d), BF16)],
                 scratch_shapes=[pltpu.VMEM((SUBLANES, d), F32)],
                 compiler_params=_params(("arbitrary",)))(pa, pa, pa, w, g)


def _conva_bwd(dcat, pa, v, w, g, *, d, seq, name):
    t, width = pa.shape
    d3 = 3 * d
    tt = _pick_rows(seq, ROW_TILE)
    tps = seq // tt
    nt = t // tt

    def body(dya_ref, xa_ref, ca_ref, ba_ref, v_ref, w_ref, g_ref, dpa_ref, dw_ref, dg_ref, carry):
        i = pl.program_id(0)

        @pl.when(i == 0)
        def _():
            dw_ref[...] = jnp.zeros_like(dw_ref)
            dg_ref[...] = jnp.zeros_like(dg_ref)

        @pl.when(i % tps == 0)
        def _():
            carry[...] = jnp.zeros_like(carry)

        xa, ca, ba, vv = [r[...].astype(F32) for r in (xa_ref, ca_ref, ba_ref, v_ref)]
        yp = ba * vv
        dyp, dgt = _rms_bwd(yp, _rms(yp), g_ref[...], dya_ref[...].astype(F32))
        dg_ref[...] += jnp.sum(dgt, axis=0, keepdims=True)
        dv = dyp * ba
        halo = carry[...]
        dv1 = _shift_up(dv, halo, 1)
        dv2 = _shift_up(dv, halo, 2)
        carry[...] = dv[:SUBLANES]
        wv = w_ref[...]
        du = wv[2:3] * dv + wv[1:2] * dv1 + wv[0:1] * dv2
        u = ca * xa
        dw_ref[0:1, :] += jnp.sum(u * dv2, axis=0, keepdims=True)
        dw_ref[1:2, :] += jnp.sum(u * dv1, axis=0, keepdims=True)
        dw_ref[2:3, :] += jnp.sum(u * dv, axis=0, keepdims=True)
        dpa_ref[:, 0:d] = (du * ca).astype(BF16)
        dpa_ref[:, d:2 * d] = (du * xa).astype(BF16)
        dpa_ref[:, 2 * d:3 * d] = (dyp * vv).astype(BF16)

    rcol = lambda c: pl.BlockSpec((tt, d), lambda i, c=c: (nt - 1 - i, c))
    return _call(body, name=name, grid=(nt,),
                 in_specs=[rcol(0), rcol(0), rcol(1), rcol(2), rcol(0), _full((CONV_K, d)), _full((1, d))],
                 out_specs=[pl.BlockSpec((tt, d3), lambda i: (nt - 1 - i, 0)), _full((CONV_K, d)), _full((1, d))],
                 out_shape=[jax.ShapeDtypeStruct((t, width), BF16), jax.ShapeDtypeStruct((CONV_K, d), F32),
                            jax.ShapeDtypeStruct((1, d), F32)],
                 scratch_shapes=[pltpu.VMEM((SUBLANES, d), F32)],
                 compiler_params=_params(("arbitrary",)))(dcat, pa, pa, pa, v, w, g)


CONV_CH = 512


def _convb_fwd(proj, w, bias, *, col0, seq, name):
    t = proj.shape[0]
    c = w.shape[1]
    cb = _pick(c, CONV_CH)
    assert col0 % cb == 0
    tt = _pick_rows(seq, 2 * ROW_TILE)
    tps = seq // tt

    def body(p_ref, w_ref, b_ref, o_ref, carry):
        i = pl.program_id(1)

        @pl.when(i % tps == 0)
        def _():
            carry[...] = jnp.zeros_like(carry)

        p = p_ref[...].astype(F32)
        halo = carry[...]
        wv = w_ref[...]
        o = wv[3:4] * p + b_ref[...]
        for s in (1, 2, 3):
            o = o + wv[3 - s:4 - s] * _shift_down(p, halo, s)
        carry[...] = p[tt - SUBLANES:]
        o_ref[...] = o.astype(BF16)

    return _call(body, name=name, grid=(c // cb, t // tt),
                 in_specs=[pl.BlockSpec((tt, cb), lambda jc, i: (i, col0 // cb + jc)),
                           pl.BlockSpec((SSM_CONV_K, cb), lambda jc, i: (0, jc)), pl.BlockSpec((1, cb), lambda jc, i: (0, jc))],
                 out_specs=pl.BlockSpec((tt, cb), lambda jc, i: (i, jc)), out_shape=jax.ShapeDtypeStruct((t, c), BF16),
                 scratch_shapes=[pltpu.VMEM((SUBLANES, cb), F32)],
                 compiler_params=_params(("arbitrary", "arbitrary")))(proj, w, bias)


def _convb_bwd(dconv, proj, w, dproj, *, col0, seq, name):
    t, c = dconv.shape
    cb = _pick(c, CONV_CH)
    assert col0 % cb == 0
    tt = _pick_rows(seq, 2 * ROW_TILE)
    tps = seq // tt
    nt = t // tt

    def body(dc_ref, p_ref, w_ref, dproj_in, dp_ref, dw_ref, db_ref, carry):
        del dproj_in
        i = pl.program_id(1)

        @pl.when(i == 0)
        def _():
            dw_ref[...] = jnp.zeros_like(dw_ref)
            db_ref[...] = jnp.zeros_like(db_ref)

        @pl.when(i % tps == 0)
        def _():
            carry[...] = jnp.zeros_like(carry)

        dc = dc_ref[...].astype(F32)
        p = p_ref[...].astype(F32)
        halo = carry[...]
        wv = w_ref[...]
        dp = wv[3:4] * dc
        dw_ref[3:4, :] += jnp.sum(p * dc, axis=0, keepdims=True)
        for s in (1, 2, 3):
            dcs = _shift_up(dc, halo, s)
            dp = dp + wv[3 - s:4 - s] * dcs
            dw_ref[3 - s:4 - s, :] += jnp.sum(p * dcs, axis=0, keepdims=True)
        carry[...] = dc[:SUBLANES]
        db_ref[...] += jnp.sum(dc, axis=0, keepdims=True)
        dp_ref[...] = dp.astype(BF16)

    win_spec = pl.BlockSpec((tt, cb), lambda jc, i: (nt - 1 - i, col0 // cb + jc))
    taps = pl.BlockSpec((SSM_CONV_K, cb), lambda jc, i: (0, jc))
    return _call(body, name=name, grid=(c // cb, nt),
                 in_specs=[pl.BlockSpec((tt, cb), lambda jc, i: (nt - 1 - i, jc)), win_spec, taps,
                           pl.BlockSpec(memory_space=pl.ANY)],
                 out_specs=[win_spec, taps, pl.BlockSpec((1, cb), lambda jc, i: (0, jc))],
                 out_shape=[jax.ShapeDtypeStruct(dproj.shape, BF16), jax.ShapeDtypeStruct((SSM_CONV_K, c), F32),
                            jax.ShapeDtypeStruct((1, c), F32)],
                 input_output_aliases={3: 0},
                 scratch_shapes=[pltpu.VMEM((SUBLANES, cb), F32)],
                 compiler_params=_params(("arbitrary", "arbitrary")))(dconv, proj, w, dproj)


def _expand_heads(x, ev):
    return jnp.dot(x, ev, precision=HIGHEST, preferred_element_type=F32)


def _head_sums(v, ev):
    return lax.dot_general(v, ev, (((1,), (1,)), ((), ())), precision=HIGHEST, preferred_element_type=F32)


def _ssd_common(c_ref, pdt_ref, dtb_ref, alog_ref, e_ref, h):
    cp = c_ref[...].astype(F32)
    sg = _sigmoid(cp)
    act = cp * sg
    pre = pdt_ref[:, 0:h] + dtb_ref[...]
    dt = _softplus(pre)
    a = -jnp.exp(alog_ref[...])
    adt = dt * a
    row = lax.broadcasted_iota(jnp.int32, (CHUNK, CHUNK), 0)
    col = lax.broadcasted_iota(jnp.int32, (CHUNK, CHUNK), 1)
    tril = row >= col
    cs = jnp.dot(tril.astype(F32), adt, precision=HIGHEST, preferred_element_type=F32)
    ev = e_ref[...]
    dt_l = _expand_heads(dt, ev)
    ecs_l = jnp.exp(_expand_heads(cs, ev))
    return dict(cp=cp, sg=sg, act=act, pre=pre, dt=dt, a=a, cs=cs, dt_l=dt_l, ecs_l=ecs_l,
                tril=tril, row=row, col=col, lo=col < HEAD_DIM)


def _dot_nt(a, b):
    return lax.dot_general(a, b, (((1,), (1,)), ((), ())), preferred_element_type=F32)


def _dot_tn(a, b):
    return lax.dot_general(a, b, (((0,), (0,)), ((), ())), preferred_element_type=F32)


def _dot(a, b):
    return jnp.dot(a, b, preferred_element_type=F32)


def _ssd_fwd(cpre, pdt, pz, ya, dtb, alog, dsk_lane, gs, emat, *, nseq, seq, name):
    t, xbc = cpre.shape
    d = ya.shape[1]
    h = d // HEAD_DIM
    npair = h // 2
    ppg = npair // SSM_GROUPS
    nc = seq // CHUNK
    gw = d // SSM_GROUPS
    bc0 = d
    cc0 = d + SSM_GROUPS * D_STATE

    def body(c_ref, pdt_ref, z_ref, ya_ref, dtb_ref, alog_ref, dsk_ref, gs_ref, e_ref, cat_ref, y2_ref, hp_ref, h_ref):
        @pl.when(pl.program_id(0) == 0)
        def _():
            h_ref[...] = jnp.zeros_like(h_ref)

        for sq in range(nseq):
            one_seq(c_ref.at[sq], pdt_ref.at[sq], z_ref.at[sq], ya_ref.at[sq], dtb_ref, alog_ref, dsk_ref, gs_ref, e_ref,
                    cat_ref.at[sq], y2_ref.at[sq], hp_ref.at[sq], h_ref.at[sq])

    def one_seq(c_ref, pdt_ref, z_ref, ya_ref, dtb_ref, alog_ref, dsk_ref, gs_ref, e_ref, cat_ref, y2_ref, hp_ref, h_ref):
        q = _ssd_common(c_ref, pdt_ref, dtb_ref, alog_ref, e_ref, h)
        act, cs, lo, ecs_l = q["act"], q["cs"], q["lo"], q["ecs_l"]
        xs = act[:, :d]
        xd = xs * q["dt_l"]
        ys = []
        for g in range(SSM_GROUPS):
            bg = act[:, bc0 + g * D_STATE: bc0 + (g + 1) * D_STATE]
            cgb = act[:, cc0 + g * D_STATE: cc0 + (g + 1) * D_STATE].astype(BF16)
            s = _dot_nt(cgb, bg.astype(BF16))
            for jj in range(ppg):
                j = g * ppg + jj
                sl = slice(LANES * j, LANES * (j + 1))
                xdj = xd[:, sl]
                x2 = jnp.concatenate([jnp.where(lo, xdj, 0.0), jnp.where(lo, 0.0, xdj)], axis=0).astype(BF16)
                hprev = h_ref[j]
                hp_ref[j] = hprev.astype(BF16)
                ms, bws = [], []
                for hh in (2 * j, 2 * j + 1):
                    csc = cs[:, hh:hh + 1]
                    csb = jnp.broadcast_to(csc, (CHUNK, CHUNK))
                    ms.append(s * jnp.exp(jnp.where(q["tril"], csb - csb.T, -jnp.inf)))
                    bws.append(bg * jnp.exp(cs[CHUNK - 1:CHUNK, hh:hh + 1] - csc))
                ydiag = _dot(jnp.concatenate(ms, axis=1).astype(BF16), x2)
                st = _dot_tn(jnp.concatenate(bws, axis=0).astype(BF16), x2)
                ecs = ecs_l[:, sl]
                yoff = _dot(cgb, hprev.astype(BF16)) * ecs
                h_ref[j] = hprev * ecs[CHUNK - 1:CHUNK] + st
                ys.append(ydiag + yoff)
        y = jnp.concatenate(ys, axis=1) + dsk_ref[...] * xs
        y2_ref[...] = y.astype(BF16)
        zv = z_ref[...].astype(F32)
        y3 = y * (zv * _sigmoid(zv))
        cat_ref[:, 0:d] = ya_ref[...]
        for gi in range(SSM_GROUPS):
            seg = y3[:, gi * gw:(gi + 1) * gw]
            cat_ref[:, d + gi * gw:d + (gi + 1) * gw] = (seg * _rms(seg) * gs_ref[:, gi * gw:(gi + 1) * gw]).astype(BF16)

    chunk = lambda w, cb=0: pl.BlockSpec((nseq, CHUNK, w), lambda c, cb=cb: (0, c, cb))
    vec = lambda w: pl.BlockSpec((1, w), lambda c: (0, 0))
    hp_spec = pl.BlockSpec((nseq, None, npair, D_STATE, LANES), lambda c: (0, c, 0, 0, 0))
    per_seq = lambda a: a.reshape(nseq, seq, a.shape[1])
    cat, y2, hp = _call(
        body, name=name, grid=(nc,),
        in_specs=[chunk(xbc), chunk(LANES), chunk(d, 3), chunk(d), vec(h), vec(h), vec(d), vec(d),
                  pl.BlockSpec((h, d), lambda c: (0, 0))],
        out_specs=[chunk(2 * d), chunk(d), hp_spec],
        out_shape=[jax.ShapeDtypeStruct((nseq, seq, 2 * d), BF16), jax.ShapeDtypeStruct((nseq, seq, d), BF16),
                   jax.ShapeDtypeStruct((nseq, nc, npair, D_STATE, LANES), BF16)],
        scratch_shapes=[pltpu.VMEM((nseq, npair, D_STATE, LANES), F32)],
        compiler_params=_params(("arbitrary",)))(
            per_seq(cpre), per_seq(pdt), per_seq(pz), per_seq(ya), dtb, alog, dsk_lane, gs, emat)
    return cat.reshape(t, 2 * d), y2.reshape(t, d), hp


def _ssd_bwd(cpre, pdt, pz, y2, hprev_all, dcat, dtb, alog, dsk_lane, gs, emat, dproj, *, nseq, seq, name):
    t, xbc = cpre.shape
    d = y2.shape[1]
    h = d // HEAD_DIM
    npair = h // 2
    ppg = npair // SSM_GROUPS
    nc = seq // CHUNK
    gw = d // SSM_GROUPS
    bc0 = d
    cc0 = d + SSM_GROUPS * D_STATE

    def body(c_ref, pdt_ref, z_ref, y2_ref, hp_ref, dys_ref, dtb_ref, alog_ref, dsk_ref, gs_ref, e_ref, dproj_in,
             dconv_ref, dz_ref, dpdt_ref, dgs_ref, ddsk_ref, ddtb_ref, dalog_ref, dh_ref):
        del dproj_in
        b = pl.program_id(0)
        c = pl.program_id(1)

        @pl.when(c == 0)
        def _():
            dh_ref[...] = jnp.zeros_like(dh_ref)

        @pl.when((b == 0) & (c == 0))
        def _():
            dgs_ref[...] = jnp.zeros_like(dgs_ref)
            ddsk_ref[...] = jnp.zeros_like(ddsk_ref)
            ddtb_ref[...] = jnp.zeros_like(ddtb_ref)
            dalog_ref[...] = jnp.zeros_like(dalog_ref)

        q = _ssd_common(c_ref, pdt_ref, dtb_ref, alog_ref, e_ref, h)
        cp, sg, act, cs, a, dt, lo = q["cp"], q["sg"], q["act"], q["cs"], q["a"], q["dt"], q["lo"]
        ecs_l, dt_l = q["ecs_l"], q["dt_l"]
        ev = e_ref[...]
        xs = act[:, :d]
        xd = xs * dt_l
        row16 = lax.broadcasted_iota(jnp.int32, (CHUNK, h), 0)
        hid = lax.broadcasted_iota(jnp.int32, (1, h), 1)

        zv = z_ref[...].astype(F32)
        sz = _sigmoid(zv)
        siluz = zv * sz
        y2v = y2_ref[...].astype(F32)
        y3 = y2v * siluz
        dysv = dys_ref[...].astype(F32)
        dy3s = []
        for gi in range(SSM_GROUPS):
            gsl = slice(gi * gw, (gi + 1) * gw)
            seg = y3[:, gsl]
            dseg, dgt = _rms_bwd(seg, _rms(seg), gs_ref[:, gsl], dysv[:, gsl])
            dy3s.append(dseg)
            dgs_ref[:, gsl] += jnp.sum(dgt, axis=0, keepdims=True)
        dy3 = jnp.concatenate(dy3s, axis=1)
        dy = dy3 * siluz
        dz_ref[...] = (dy3 * y2v * (sz * (1.0 + zv * (1.0 - sz)))).astype(BF16)
        ddsk_ref[...] += jnp.sum(_head_sums(dy * xs, ev), axis=0, keepdims=True)

        dcs = jnp.zeros((CHUNK, h), F32)
        dxd_parts, yoff_parts, db_parts, dc_parts = [], [], [], []
        for g in range(SSM_GROUPS):
            bg = act[:, bc0 + g * D_STATE: bc0 + (g + 1) * D_STATE]
            cg = act[:, cc0 + g * D_STATE: cc0 + (g + 1) * D_STATE]
            bgb, cgb = bg.astype(BF16), cg.astype(BF16)
            s = _dot_nt(cgb, bgb)
            ds = jnp.zeros((CHUNK, CHUNK), F32)
            dbg = jnp.zeros((CHUNK, D_STATE), F32)
            dcg = jnp.zeros((CHUNK, D_STATE), F32)
            for jj in range(ppg):
                j = g * ppg + jj
                sl = slice(LANES * j, LANES * (j + 1))
                xdj = xd[:, sl]
                xdb = xdj.astype(BF16)
                x2 = jnp.concatenate([jnp.where(lo, xdj, 0.0), jnp.where(lo, 0.0, xdj)], axis=0).astype(BF16)
                dyj = dy[:, sl]
                dy2 = jnp.concatenate([jnp.where(lo, dyj, 0.0), jnp.where(lo, 0.0, dyj)], axis=0).astype(BF16)
                hpb = hp_ref[j]
                hprev = hpb.astype(F32)
                dhn = dh_ref[j]
                dhb = dhn.astype(BF16)
                dh2 = jnp.concatenate([jnp.where(lo, dhn, 0.0), jnp.where(lo, 0.0, dhn)], axis=0).astype(BF16)
                ecs = ecs_l[:, sl]
                gmat = (dyj * ecs).astype(BF16)
                yoff_parts.append(_dot(cgb, hpb) * ecs)
                dcg = dcg + _dot_nt(gmat, hpb)
                dh_ref[j] = dhn * ecs[CHUNK - 1:CHUNK] + _dot_tn(cgb, gmat)
                t2 = dhn * hprev
                dbw2 = _dot_nt(x2, dhb)
                dm2 = _dot_nt(dy2, xdb)
                ms, bws = [], []
                for idx, hh in enumerate((2 * j, 2 * j + 1)):
                    msk = lo if idx == 0 else jnp.logical_not(lo)
                    onehot = (hid == hh).astype(F32)
                    csc = cs[:, hh:hh + 1]
                    csb = jnp.broadcast_to(csc, (CHUNK, CHUNK))
                    lm = jnp.exp(jnp.where(q["tril"], csb - csb.T, -jnp.inf))
                    m = s * lm
                    cs_last = cs[CHUNK - 1:CHUNK, hh:hh + 1]
                    dte = jnp.exp(cs_last - csc)
                    ms.append(m)
                    bws.append(bg * dte)
                    dbw = dbw2[idx * CHUNK:(idx + 1) * CHUNK]
                    dbg = dbg + dbw * dte
                    qv = jnp.sum(dbw * bg, axis=-1, keepdims=True) * dte
                    dm = dm2[idx * CHUNK:(idx + 1) * CHUNK]
                    wm = dm * m
                    rc = jnp.sum(wm - wm.T, axis=-1, keepdims=True)
                    ds = ds + dm * lm
                    ddec = jnp.sum(jnp.where(msk, t2, 0.0)) * jnp.exp(cs_last)
                    last = jnp.sum(qv) + ddec
                    dcs = dcs + (rc - qv) * onehot + jnp.where(row16 == CHUNK - 1, last * onehot, 0.0)
                dxd_s = _dot(jnp.concatenate(bws, axis=1).astype(BF16), dh2)
                dxd_d = _dot_tn(jnp.concatenate(ms, axis=0).astype(BF16), dy2)
                dxd_parts.append(dxd_s + dxd_d)
            dsb = ds.astype(BF16)
            dc_parts.append(dcg + _dot(dsb, bgb))
            db_parts.append(dbg + _dot_tn(dsb, cgb))
        yoff_all = jnp.concatenate(yoff_parts, axis=1)
        dxd_all = jnp.concatenate(dxd_parts, axis=1)
        dcs = dcs + _head_sums(dy * yoff_all, ev)
        triu = (q["col"] >= q["row"]).astype(F32)
        dadt = jnp.dot(triu, dcs, precision=HIGHEST, preferred_element_type=F32)
        ddt = dadt * a + _head_sums(dxd_all * xs, ev)
        dalog_ref[...] += jnp.sum(dadt * dt, axis=0, keepdims=True) * a
        dpre = ddt * _sigmoid(q["pre"])
        ddtb_ref[...] += jnp.sum(dpre, axis=0, keepdims=True)
        dpdt_ref[...] = jnp.zeros_like(dpdt_ref)
        dpdt_ref[:, 0:h] = dpre.astype(BF16)
        dxs = dxd_all * dt_l + dy * dsk_ref[...]
        dact = jnp.concatenate([dxs] + db_parts + dc_parts, axis=1)
        dconv_ref[...] = (dact * (sg * (1.0 + cp * (1.0 - sg)))).astype(BF16)

    rchunk = lambda w, cb=0: pl.BlockSpec((CHUNK, w), lambda b, c, cb=cb: (b * nc + nc - 1 - c, cb))
    vec = lambda w: pl.BlockSpec((1, w), lambda b, c: (0, 0))
    hp_spec = pl.BlockSpec((None, None, npair, D_STATE, LANES), lambda b, c: (b, nc - 1 - c, 0, 0, 0))
    return _call(body, name=name, grid=(nseq, nc),
                 in_specs=[rchunk(xbc), rchunk(LANES), rchunk(d, 3), rchunk(d), hp_spec, rchunk(d, 1),
                           vec(h), vec(h), vec(d), vec(d), pl.BlockSpec((h, d), lambda b, c: (0, 0)),
                           pl.BlockSpec(memory_space=pl.ANY)],
                 out_specs=[rchunk(xbc), rchunk(d, 3), rchunk(LANES), vec(d), vec(h), vec(h), vec(h)],
                 out_shape=[jax.ShapeDtypeStruct((t, xbc), BF16), jax.ShapeDtypeStruct(dproj.shape, BF16),
                            jax.ShapeDtypeStruct((t, LANES), BF16), jax.ShapeDtypeStruct((1, d), F32),
                            jax.ShapeDtypeStruct((1, h), F32), jax.ShapeDtypeStruct((1, h), F32),
                            jax.ShapeDtypeStruct((1, h), F32)],
                 input_output_aliases={11: 1},
                 scratch_shapes=[pltpu.VMEM((npair, D_STATE, LANES), F32)],
                 compiler_params=_params(("arbitrary", "arbitrary")))(
                     cpre, pdt, pz, y2, hprev_all, dcat, dtb, alog, dsk_lane, gs, emat, dproj)


def _sum_adamw(parts, w, m, v, *, name, layer=None, outs=None):
    n, r, c = parts.shape
    tr = _pick_rows(r, 256)
    bc1 = 1.0 - ADAM_B1 ** ADAM_STEP
    bc2 = 1.0 - ADAM_B2 ** ADAM_STEP

    def body(p_ref, w_ref, m_ref, v_ref, *rest):
        g_ref, d_ref, mo_ref, vo_ref = rest[-4:]
        g = p_ref[0].astype(F32)
        for k in range(1, n):
            g = g + p_ref[k].astype(F32)
        mn = ADAM_B1 * m_ref[...] + (1.0 - ADAM_B1) * g
        vn = ADAM_B2 * v_ref[...] + (1.0 - ADAM_B2) * (g * g)
        g_ref[...] = g
        mo_ref[...] = mn
        vo_ref[...] = vn
        d_ref[...] = -ADAM_LR * ((mn / bc1) / (jnp.sqrt(vn / bc2) + ADAM_EPS) + ADAM_WD * w_ref[...])

    p_spec = pl.BlockSpec((n, tr, c), lambda i: (0, i, 0))
    if layer is None:
        blk = pl.BlockSpec((tr, c), lambda i: (i, 0))
        return _call(body, name=name, grid=(r // tr,), in_specs=[p_spec, blk, blk, blk], out_specs=[blk] * 4,
                     out_shape=[jax.ShapeDtypeStruct((r, c), F32)] * 4,
                     compiler_params=_params(("parallel",)))(parts, w, m, v)
    blk = pl.BlockSpec((None, tr, c), lambda i: (layer, i, 0))
    if outs is None:
        outs = [lax.empty(w.shape, F32) for _ in range(4)]
    return _call(body, name=name, grid=(r // tr,),
                 in_specs=[p_spec, blk, blk, blk] + [pl.BlockSpec(memory_space=pl.ANY)] * 4, out_specs=[blk] * 4,
                 out_shape=[jax.ShapeDtypeStruct(w.shape, F32)] * 4, input_output_aliases={4 + k: k for k in range(4)},
                 compiler_params=_params(("parallel",)))(parts, w, m, v, *outs)


def _assemble_cols(blocks, *, name):
    nb, r, c = blocks.shape
    width = -(-nb * c // LANES) * LANES
    tr = _pick_rows(r, 256)

    def body(b_ref, o_ref):
        pieces = [b_ref[j] for j in range(nb)]
        if width > nb * c:
            pieces.append(jnp.zeros((tr, width - nb * c), blocks.dtype))
        o_ref[...] = jnp.concatenate(pieces, axis=1)

    return _call(body, name=name, grid=(r // tr,), in_specs=[pl.BlockSpec((nb, tr, c), lambda i: (0, i, 0))],
                 out_specs=pl.BlockSpec((tr, width), lambda i: (i, 0)), out_shape=jax.ShapeDtypeStruct((r, width), blocks.dtype),
                 compiler_params=_params(("parallel",)))(blocks)


def _split_cols(pieces, c, *, name):
    r = pieces[0].shape[0]
    tr = _pick_rows(r, 256)
    n_in = len(pieces)

    def body(*refs):
        o_ref = refs[n_in]
        x = jnp.concatenate([p[...] for p in refs[:n_in]], axis=1) if n_in > 1 else refs[0][...]
        for j in range(N_DEV):
            o_ref[j] = x[:, c * j:c * (j + 1)]

    return _call(body, name=name, grid=(r // tr,),
                 in_specs=[pl.BlockSpec((tr, p.shape[1]), lambda i: (i, 0)) for p in pieces],
                 out_specs=pl.BlockSpec((N_DEV, tr, c), lambda i: (0, i, 0)),
                 out_shape=jax.ShapeDtypeStruct((N_DEV, r, c), pieces[0].dtype),
                 compiler_params=_params(("parallel",)))(*pieces)


def _sum_parts(parts, *, name):
    n, r, c = parts.shape
    tr = _pick_rows(r, 256)

    def body(p_ref, g_ref):
        g = p_ref[0].astype(F32)
        for k in range(1, n):
            g = g + p_ref[k].astype(F32)
        g_ref[...] = g

    return _call(body, name=name, grid=(r // tr,), in_specs=[pl.BlockSpec((n, tr, c), lambda i: (0, i, 0))],
                 out_specs=pl.BlockSpec((tr, c), lambda i: (i, 0)), out_shape=jax.ShapeDtypeStruct((r, c), F32),
                 compiler_params=_params(("parallel",)))(parts)


def _peers():
    x, y, c = lax.axis_index("x"), lax.axis_index("y"), lax.axis_index("c")
    me = 4 * x + 2 * y + c
    out = []
    for k in range(1, N_DEV):
        px = (1 - x) if (k >> 2) & 1 else x
        py = (1 - y) if (k >> 1) & 1 else y
        pc = (1 - c) if k & 1 else c
        out.append(((px, py, pc), 4 * px + 2 * py + pc))
    return me, out


def _exchange(src, *, gather, name):
    shape = src.shape if gather else src.shape[1:]

    def body(s_ref, o_ref, send_sems, recv_sems, local_sem):
        me, peers = _peers()
        mine = pltpu.make_async_copy(s_ref if gather else s_ref.at[me], o_ref.at[me], local_sem)
        mine.start()
        sends = []
        for k, (dev, pid) in enumerate(peers):
            cp = pltpu.make_async_remote_copy(
                src_ref=s_ref if gather else s_ref.at[pid], dst_ref=o_ref.at[me],
                send_sem=send_sems.at[k], recv_sem=recv_sems.at[k], device_id=dev, device_id_type=MESH)
            cp.start()
            sends.append(cp)
        for k, (dev, pid) in enumerate(peers):
            pltpu.make_async_remote_copy(
                src_ref=s_ref if gather else s_ref.at[pid], dst_ref=o_ref.at[pid],
                send_sem=send_sems.at[k], recv_sem=recv_sems.at[k], device_id=dev, device_id_type=MESH).wait_recv()
        for cp in sends:
            cp.wait_send()
        mine.wait()

    any_spec = pl.BlockSpec(memory_space=pl.ANY)
    return _call(body, name=name, in_specs=[any_spec], out_specs=any_spec,
                 out_shape=jax.ShapeDtypeStruct((N_DEV,) + tuple(shape), src.dtype),
                 scratch_shapes=[pltpu.SemaphoreType.DMA((N_DEV - 1,)), pltpu.SemaphoreType.DMA((N_DEV - 1,)),
                                 pltpu.SemaphoreType.DMA(())])(src)


_HBM = pl.BlockSpec(memory_space=pltpu.HBM)
_SEM = pl.BlockSpec(memory_space=pltpu.SEMAPHORE)
_EFFECT = pltpu.SideEffectType.DATAFLOW_SIDE_EFFECTING


def _split_copies(s_refs, l_refs, send_sems, recv_sems, gather, incoming):
    me, peers = _peers()
    local, remote = [], []
    for ti, (s_ref, l_ref) in enumerate(zip(s_refs, l_refs)):
        base = ti * N_DEV
        local.append(pltpu.make_async_copy(s_ref if gather else s_ref.at[me], l_ref.at[me], recv_sems.at[base + N_DEV - 1]))
        for k, (dev, pid) in enumerate(peers):
            sems = dict(send_sem=send_sems.at[base + k], recv_sem=recv_sems.at[base + k], device_id=dev, device_id_type=MESH)
            src = s_ref if gather else s_ref.at[pid]
            remote.append((
                pltpu.make_async_remote_copy(src_ref=src, dst_ref=l_ref.at[me], **sems),
                pltpu.make_async_remote_copy(src_ref=src, dst_ref=l_ref.at[pid], **sems) if incoming else None))
    return local, remote


def _exchange_start(srcs, *, gather, name, after=()):
    n = len(srcs)
    after = list(after)
    srcs = [pltpu.with_memory_space_constraint(s, pltpu.HBM) for s in srcs]
    lands = [pltpu.with_memory_space_constraint(
        lax.empty((N_DEV,) + tuple(s.shape if gather else s.shape[1:]), s.dtype), pltpu.HBM) for s in srcs]

    def body(*refs):
        s_refs, l_refs = refs[:n], refs[n:2 * n]
        outs = refs[2 * n + len(after):]
        send_sems, recv_sems, token = outs[0], outs[1], outs[-1]
        local, remote = _split_copies(s_refs, l_refs, send_sems, recv_sems, gather, incoming=False)
        for cp in local:
            cp.start()
        for out_cp, _ in remote:
            out_cp.start()
        token[...] = jnp.zeros_like(token)

    outs = _call(
        body, name=name,
        out_shape=(pltpu.SemaphoreType.DMA((n * N_DEV,)), pltpu.SemaphoreType.DMA((n * N_DEV,)),
                   *[pltpu.HBM(s.shape, s.dtype) for s in srcs], *[pltpu.HBM(l.shape, l.dtype) for l in lands],
                   jax.ShapeDtypeStruct((SUBLANES, LANES), F32)),
        in_specs=[_HBM] * (2 * n) + [pl.BlockSpec(memory_space=pl.ANY)] * len(after),
        out_specs=(_SEM, _SEM, *[_HBM] * (2 * n), pl.BlockSpec(memory_space=pltpu.VMEM)),
        input_output_aliases={k: k + 2 for k in range(2 * n)},
        compiler_params=pltpu.CompilerParams(has_side_effects=_EFFECT),
    )(*srcs, *lands, *after)
    return dict(n=n, gather=gather, sems=outs[:2], srcs=outs[2:2 + n], lands=outs[2 + n:2 + 2 * n]), outs[-1]


def _exchange_wait(state, after, *, name):
    n, gather = state["n"], state["gather"]
    after = list(after)

    def body(*refs):
        s_refs, l_refs = refs[:n], refs[n:2 * n]
        send_sems, recv_sems = refs[2 * n], refs[2 * n + 1]
        local, remote = _split_copies(s_refs, l_refs, send_sems, recv_sems, gather, incoming=True)
        for out_cp, in_cp in remote:
            out_cp.wait_send()
            in_cp.wait_recv()
        for cp in local:
            cp.wait()

    outs = _call(
        body, name=name,
        out_shape=tuple(pltpu.HBM(a.shape, a.dtype) for a in (*state["srcs"], *state["lands"])),
        in_specs=[_HBM] * (2 * n) + [_SEM, _SEM] + [pl.BlockSpec(memory_space=pl.ANY)] * len(after),
        out_specs=tuple([_HBM] * (2 * n)),
        input_output_aliases={k: k for k in range(2 * n)},
        compiler_params=pltpu.CompilerParams(has_side_effects=_EFFECT),
    )(*state["srcs"], *state["lands"], *state["sems"], *after)
    return outs[n:]


def _pack(arrs):
    flat = jnp.concatenate([a.reshape(-1).astype(F32) for a in arrs])
    pad = (-flat.shape[0]) % (SUBLANES * LANES)
    return jnp.pad(flat, (0, pad)).reshape(-1, LANES)


def _unpack(packed, shapes):
    flat = packed.reshape(-1)
    out, off = [], 0
    for s in shapes:
        n = 1
        for v in s:
            n *= v
        out.append(flat[off:off + n].reshape(s))
        off += n
    return out


SMALL = ("norm_mix_pre", "ssm_conv_b", "dt_bias", "a_log", "d_skip", "conv_out_norm", "ssm_out_norm",
         "norm_mix_post", "norm_mlp_pre", "norm_mlp_post", "conv_a_w", "ssm_conv_w")
BIG = ("w_in", "w_out", "w_up", "w_down")
ORDER = ("norm_mix_pre", "w_in", "conv_a_w", "ssm_conv_w", "ssm_conv_b", "dt_bias", "a_log", "d_skip",
         "conv_out_norm", "ssm_out_norm", "w_out", "norm_mix_post", "norm_mlp_pre", "w_up", "w_down", "norm_mlp_post")


def kernel(x, norm_mix_pre, w_in, conv_a_w, ssm_conv_w, ssm_conv_b, dt_bias, a_log, d_skip, conv_out_norm, ssm_out_norm, w_out, norm_mix_post, norm_mlp_pre, w_up, w_down, norm_mlp_post, loss_target, m_norm_mix_pre, m_w_in, m_conv_a_w, m_ssm_conv_w, m_ssm_conv_b, m_dt_bias, m_a_log, m_d_skip, m_conv_out_norm, m_ssm_out_norm, m_w_out, m_norm_mix_post, m_norm_mlp_pre, m_w_up, m_w_down, m_norm_mlp_post, v_norm_mix_pre, v_w_in, v_conv_a_w, v_ssm_conv_w, v_ssm_conv_b, v_dt_bias, v_a_log, v_d_skip, v_conv_out_norm, v_ssm_out_norm, v_w_out, v_norm_mix_post, v_norm_mlp_pre, v_w_up, v_w_down, v_norm_mlp_post):
    W = dict(norm_mix_pre=norm_mix_pre, w_in=w_in, conv_a_w=conv_a_w, ssm_conv_w=ssm_conv_w, ssm_conv_b=ssm_conv_b,
             dt_bias=dt_bias, a_log=a_log, d_skip=d_skip, conv_out_norm=conv_out_norm, ssm_out_norm=ssm_out_norm,
             w_out=w_out, norm_mix_post=norm_mix_post, norm_mlp_pre=norm_mlp_pre, w_up=w_up, w_down=w_down,
             norm_mlp_post=norm_mlp_post)
    M = dict(norm_mix_pre=m_norm_mix_pre, w_in=m_w_in, conv_a_w=m_conv_a_w, ssm_conv_w=m_ssm_conv_w,
             ssm_conv_b=m_ssm_conv_b, dt_bias=m_dt_bias, a_log=m_a_log, d_skip=m_d_skip,
             conv_out_norm=m_conv_out_norm, ssm_out_norm=m_ssm_out_norm, w_out=m_w_out,
             norm_mix_post=m_norm_mix_post, norm_mlp_pre=m_norm_mlp_pre, w_up=m_w_up, w_down=m_w_down,
             norm_mlp_post=m_norm_mlp_post)
    V = dict(norm_mix_pre=v_norm_mix_pre, w_in=v_w_in, conv_a_w=v_conv_a_w, ssm_conv_w=v_ssm_conv_w,
             ssm_conv_b=v_ssm_conv_b, dt_bias=v_dt_bias, a_log=v_a_log, d_skip=v_d_skip,
             conv_out_norm=v_conv_out_norm, ssm_out_norm=v_ssm_out_norm, w_out=v_w_out,
             norm_mix_post=v_norm_mix_post, norm_mlp_pre=v_norm_mlp_pre, w_up=v_w_up, w_down=v_w_down,
             norm_mlp_post=v_norm_mlp_post)

    nseq, seq, d = x.shape
    t = nseq * seq
    depth = w_in.shape[0]
    h = d // HEAD_DIM
    xbc = d + 2 * SSM_GROUPS * D_STATE
    in_cols = w_in.shape[2] * N_DEV
    d_mix = w_out.shape[1] * N_DEV
    d_ff = w_up.shape[2] * N_DEV
    me = 4 * lax.axis_index("x") + 2 * lax.axis_index("y") + lax.axis_index("c")
    ca_shard = conv_a_w.shape[2]
    sc_shard = ssm_conv_w.shape[2]

    tap_shapes = [conv_a_w.shape[1:], ssm_conv_w.shape[1:]]

    def gather_start(i, after=()):
        st_in, tok_in = _exchange_start([w_in[i].astype(BF16), _pack([conv_a_w[i], ssm_conv_w[i]])], gather=True,
                                        name=f"gather_start_in_{i}", after=after)
        st_rest, tok_rest = _exchange_start([W[n][i].astype(BF16) for n in ("w_out", "w_up", "w_down")], gather=True,
                                            name=f"gather_start_rest_{i}", after=[tok_in])
        return st_in, st_rest, tok_rest

    vec = lambda name, i: W[name][i].reshape(1, -1)
    emat = (lax.broadcasted_iota(jnp.int32, (h, d), 1) // HEAD_DIM == lax.broadcasted_iota(jnp.int32, (h, d), 0)).astype(F32)

    xcur = x.reshape(t, d)
    hcur = _norm_fwd(xcur, vec("norm_mix_pre", 0), name="norm_first")
    saved = []
    nxt = gather_start(0)
    for i in range(depth):
        st_in, st_rest, tok = nxt
        win_g, taps_g = _exchange_wait(st_in, [hcur, tok], name=f"gather_wait_in_{i}")
        win = _assemble_cols(win_g, name=f"assemble_w_in_{i}")
        taps_j = [_unpack(taps_g[j], tap_shapes) for j in range(N_DEV)]
        conv_a_i = jnp.concatenate([tj[0] for tj in taps_j], axis=1)
        ssm_conv_i = jnp.concatenate([tj[1] for tj in taps_j], axis=1)
        proj = _mm(hcur, win, n=4 * d + xbc, name=f"fwd_proj_{i}", out_dtypes=(BF16,))
        pdt = _mm(hcur, win, n=LANES, b_off=4 * d + xbc, name=f"fwd_proj_dt_{i}")
        ya, va = _conva_fwd(proj, conv_a_i, vec("conv_out_norm", i), d=d, seq=seq, name=f"fwd_conv_a_{i}")
        cpre = _convb_fwd(proj, ssm_conv_i, vec("ssm_conv_b", i), col0=4 * d, seq=seq, name=f"fwd_conv_b_{i}")
        dsk_lane = jnp.repeat(W["d_skip"][i], HEAD_DIM).reshape(1, d)
        cat, y2, hprev = _ssd_fwd(cpre, pdt, proj, ya, vec("dt_bias", i), vec("a_log", i), dsk_lane,
                                  vec("ssm_out_norm", i), emat, nseq=nseq, seq=seq, name=f"fwd_ssd_{i}")
        wout_g, wup_g, wdown_g = _exchange_wait(st_rest, [cat], name=f"gather_wait_rest_{i}")
        lw = dict(win=win, wout=wout_g.reshape(d_mix, d),
                  wup=_assemble_cols(wup_g, name=f"assemble_w_up_{i}"), wdown=wdown_g.reshape(d_ff, d),
                  conv_a=conv_a_i, ssm_conv=ssm_conv_i)
        after = []
        if i + 1 < depth:
            nxt = gather_start(i + 1, after=[wout_g])
            after = [nxt[2]]
        x1, h2, mix = _mm(cat, lw["wout"], name=f"fwd_out_{i}", after=after, out_dtypes=(F32, BF16, BF16),
                          epi=_epi_resid_norm, extras=(xcur,), vecs=(vec("norm_mix_post", i), vec("norm_mlp_pre", i)),
                          tm_cap=FUSED_ROWS)
        f = _mm(h2, lw["wup"], name=f"fwd_up_{i}", out_dtypes=(BF16,), epi=_epi_relu2)
        g_next = vec("norm_mix_pre", i + 1) if i + 1 < depth else vec("norm_mix_pre", 0)
        dn = _mm(f, lw["wdown"], name=f"fwd_down_{i}")
        x2, hnext = _resid_norm(x1, dn, vec("norm_mlp_post", i), g_next, name=f"fwd_post_mlp_{i}")
        saved.append(dict(lw=lw, x0=xcur, h=hcur, proj=proj, pdt=pdt, va=va, cpre=cpre, y2=y2,
                          hprev=hprev, cat=cat, mix=mix, x1=x1, h2=h2, f=f, dn=dn, dsk_lane=dsk_lane))
        xcur, hcur = x2, hnext

    dx, loss_part = _loss_fwd_bwd(xcur, loss_target.reshape(t, d), name="loss")
    loss = lax.psum(loss_part[0, 0], ("x", "y", "c"))

    small_grads = {n: [None] * depth for n in SMALL}
    big_out = {n: None for n in BIG}

    def finish(pending, after):
        li, st_a, st_b = pending

        def update(n, parts):
            big_out[n] = _sum_adamw(parts, W[n], M[n], V[n], layer=li, outs=big_out[n], name=f"adamw_{n}_{li}")

        p_down, p_up = _exchange_wait(st_a, after, name=f"scatter_wait_a_{li}")
        update("w_down", p_down)
        update("w_up", p_up)
        p_out, p_in = _exchange_wait(st_b, after + [big_out["w_up"][0]], name=f"scatter_wait_b_{li}")
        update("w_out", p_out)
        update("w_in", p_in)

    pending = None
    for i in reversed(range(depth)):
        s = saved[i]
        lw = s["lw"]
        ddn, dg = _bwd_norm_out(s["dn"], vec("norm_mlp_post", i), dx, name=f"bwd_norm_mlp_post_{i}")
        small_grads["norm_mlp_post"][i] = dg
        dup = _mm(ddn, lw["wdown"], tb=True, name=f"bwd_down_dx_{i}", out_dtypes=(BF16,), epi=_epi_drelu2,
                  extras=(s["f"],))
        g_wdown = _mm(s["f"], ddn, ta=True, name=f"bwd_down_dw_{i}", out_dtypes=(BF16,))
        dh2 = _mm(dup, lw["wup"], tb=True, name=f"bwd_up_dx_{i}", out_dtypes=(BF16,))
        g_wup = _mm(s["h2"], dup, ta=True, name=f"bwd_up_dw_{i}", out_dtypes=(BF16,))
        st_a, tok_a = _exchange_start(
            [g_wdown.reshape(N_DEV, d_ff // N_DEV, d), _split_cols([g_wup], d_ff // N_DEV, name=f"split_g_w_up_{i}")],
            gather=False, name=f"scatter_start_a_{i}")
        dx1, dmix, dg_pre, dg_post = _bwd_norm_pair(s["x1"], [dh2], dx, s["mix"], vec("norm_mlp_pre", i) + tok_a[0:1, 0:1],
                                                    vec("norm_mix_post", i), name=f"bwd_norm_mix_post_{i}")
        small_grads["norm_mlp_pre"][i] = dg_pre
        small_grads["norm_mix_post"][i] = dg_post
        dcat = _mm(dmix, lw["wout"], tb=True, name=f"bwd_out_dx_{i}", out_dtypes=(BF16,))
        g_wout = _mm(s["cat"], dmix, ta=True, name=f"bwd_out_dw_{i}", out_dtypes=(BF16,))
        dproj, dcaw, dgca = _conva_bwd(dcat, s["proj"], s["va"], lw["conv_a"], vec("conv_out_norm", i), d=d, seq=seq,
                                       name=f"bwd_conv_a_{i}")
        small_grads["conv_a_w"][i] = dcaw
        small_grads["conv_out_norm"][i] = dgca
        dconv, dproj, dpdt, dgs, ddsk, ddtb, dalog = _ssd_bwd(
            s["cpre"], s["pdt"], s["proj"], s["y2"], s["hprev"], dcat, vec("dt_bias", i), vec("a_log", i),
            s["dsk_lane"], vec("ssm_out_norm", i), emat, dproj, nseq=nseq, seq=seq, name=f"bwd_ssd_{i}")
        small_grads["ssm_out_norm"][i] = dgs
        small_grads["d_skip"][i] = ddsk
        small_grads["dt_bias"][i] = ddtb
        small_grads["a_log"][i] = dalog
        dproj, dscw, dscb = _convb_bwd(dconv, s["proj"], lw["ssm_conv"], dproj, col0=4 * d, seq=seq,
                                       name=f"bwd_conv_b_{i}")
        small_grads["ssm_conv_w"][i] = dscw
        small_grads["ssm_conv_b"][i] = dscb
        g_win = _split_cols([
            _mm(s["h"], dproj, ta=True, name=f"bwd_proj_dw_{i}", out_dtypes=(BF16,)),
            _mm(s["h"], dpdt, ta=True, name=f"bwd_proj_dt_dw_{i}", out_dtypes=(BF16,))],
            in_cols // N_DEV, name=f"split_g_w_in_{i}")
        st_b, tok_b = _exchange_start(
            [g_wout.reshape(N_DEV, d_mix // N_DEV, d), g_win], gather=False, name=f"scatter_start_b_{i}")
        dh_parts = [_mm(dp, lw["win"], tb=True, b_koff=off, name=f"bwd_proj_{nm}dx_{i}", after=[tok_b], out_dtypes=(BF16,))
                    for nm, dp, off in (("", dproj, 0), ("dt_", dpdt, 4 * d + xbc))]
        dx, dg_in = _bwd_norm_in(s["x0"], dh_parts, dx1, vec("norm_mix_pre", i), name=f"bwd_norm_mix_pre_{i}")
        small_grads["norm_mix_pre"][i] = dg_in
        if pending is not None:
            finish(pending, [dx])
        pending = (i, st_a, st_b)

    grad_x = dx.reshape(nseq, seq, d)

    small_shapes_full = {n: (depth,) + tuple(small_grads[n][0].shape) for n in SMALL}
    gpack = _pack([jnp.stack(small_grads[n]) for n in SMALL])
    st_small, tok_small = _exchange_start([gpack], gather=True, name="allreduce_small_start")
    finish(pending, [dx, tok_small])
    gparts, = _exchange_wait(st_small, [big_out["w_in"][0]], name="allreduce_small_wait")

    def shard_of(n, full):
        if n == "conv_a_w":
            return lax.dynamic_slice_in_dim(full, me * ca_shard, ca_shard, axis=2)
        if n == "ssm_conv_w":
            return lax.dynamic_slice_in_dim(full, me * sc_shard, sc_shard, axis=2)
        return full.reshape(W[n].shape)

    gsum = _sum_parts(gparts, name="sum_small")
    gfull = _unpack(gsum, [small_shapes_full[n] for n in SMALL])
    gsmall = {n: shard_of(n, gf) for n, gf in zip(SMALL, gfull)}
    res = _sum_adamw(_pack([gsmall[n] for n in SMALL])[None], _pack([W[n] for n in SMALL]),
                     _pack([M[n] for n in SMALL]), _pack([V[n] for n in SMALL]), name="adamw_small")
    small_out = [dict(zip(SMALL, _unpack(r, [W[n].shape for n in SMALL]))) for r in res]

    def out_of(kind, n):
        return big_out[n][kind] if n in BIG else small_out[kind][n]

    return (loss, grad_x, *[out_of(k, n) for k in range(4) for n in ORDER])
```

```python
import functools

import jax
import jax.numpy as jnp
from jax import lax
from jax.experimental import pallas as pl
from jax.experimental.pallas import tpu as pltpu

F32 = jnp.float32
BF16 = jnp.bfloat16
HIGHEST = lax.Precision.HIGHEST
MESH = pl.DeviceIdType.MESH

EPS = 1e-6
HEAD_DIM = 64
D_STATE = 128
SSM_GROUPS = 2
CHUNK = 128
CONV_K = 3
SSM_CONV_K = 4
ADAM_LR = 0.001
ADAM_B1 = 0.9
ADAM_B2 = 0.999
ADAM_EPS = 1e-08
ADAM_WD = 0.01
ADAM_STEP = 10

N_DEV = 8
LANES = 128
SUBLANES = 8
VMEM_LIMIT = 48 * 1024 * 1024
ROW_TILE = 512
MM_TILE = 1024
MM_TILE_N = 1536
FUSED_ROWS = 512


def _params(sem):
    return pltpu.CompilerParams(dimension_semantics=sem, vmem_limit_bytes=VMEM_LIMIT)


def _call(body, **kw):
    return pl.pallas_call(body, **kw)


def _pick(n, cap):
    best = None
    for t in range(LANES, min(n, cap) + 1, LANES):
        if n % t == 0:
            best = t
    return best or n


def _pick_rows(n, cap):
    best = None
    for t in range(SUBLANES, min(n, cap) + 1, SUBLANES):
        if n % t == 0:
            best = t
    return best or n


def _sigmoid(x):
    return 1.0 / (1.0 + jnp.exp(-x))


def _softplus(x):
    return jnp.maximum(x, 0.0) + jnp.log1p(jnp.exp(-jnp.abs(x)))


def _rms(x):
    return lax.rsqrt(jnp.mean(x * x, axis=-1, keepdims=True) + EPS)


def _rms_bwd(x, r, g, dy):
    gy = dy * g
    dx = r * gy - x * (r * r * r) * jnp.mean(gy * x, axis=-1, keepdims=True)
    return dx, dy * x * r


def _full(shape):
    return pl.BlockSpec(shape, lambda *_: (0,) * len(shape))


def _mm(a, b, *, name, ta=False, tb=False, out_dtypes=(F32,), epi=None, extras=(), n=None, b_off=0, b_koff=0,
        after=(), vecs=(), n_sums=0, tm_cap=MM_TILE):
    m, k = (a.shape[1], a.shape[0]) if ta else a.shape
    if n is None:
        n = b.shape[0] if tb else b.shape[1]
    tm, tn, tk = _pick(m, tm_cap), _pick(n, MM_TILE_N), _pick(k, MM_TILE)
    while b_off % tn or n % tn:
        tn -= LANES
    if b_koff == 0 and k > MM_TILE:
        tk = _pick(k, MM_TILE_N)
    while b_koff % tk or k % tk:
        tk -= LANES
    nk = k // tk
    nm, nn = m // tm, n // tn
    jo = b_off // tn
    ko = b_koff // tk
    a_bytes = m * k * a.dtype.itemsize
    b_bytes = n * k * b.dtype.itemsize
    m_outer = a_bytes + nm * b_bytes <= b_bytes + nn * a_bytes
    ij = (lambda g0, g1: (g0, g1)) if m_outer else (lambda g0, g1: (g1, g0))
    grid = (nm, nn, nk) if m_outer else (nn, nm, nk)

    def a_map(g0, g1, kk):
        i, _ = ij(g0, g1)
        return (kk, i) if ta else (i, kk)

    def b_map(g0, g1, kk):
        _, j = ij(g0, g1)
        return (j + jo, kk + ko) if tb else (kk + ko, j + jo)

    def o_map(g0, g1, kk):
        return ij(g0, g1)

    a_spec = pl.BlockSpec((tk, tm) if ta else (tm, tk), a_map)
    b_spec = pl.BlockSpec((tn, tk) if tb else (tk, tn), b_map)
    o_spec = pl.BlockSpec((tm, tn), o_map)
    dims = (((0 if ta else 1,), (1 if tb else 0,)), ((), ()))
    n_ex = len(extras) + len(vecs)
    after = list(after)
    o0 = 2 + n_ex + len(after)
    n_out = len(out_dtypes)
    assert n_sums == 0 or nn == 1

    def finish(acc, ex, outs):
        res = (acc,) if epi is None else epi(acc, *[e[...] for e in ex])
        for o, r in zip(outs[:n_out], res[:n_out]):
            o[...] = r.astype(o.dtype)
        for o, r in zip(outs[n_out:], res[n_out:]):
            o[...] += jnp.sum(r, axis=0, keepdims=True)

    def zero_sums(outs, kk):
        if n_sums:
            @pl.when((pl.program_id(0) == 0) & (pl.program_id(1) == 0) & (kk == 0))
            def _():
                for o in outs[n_out:]:
                    o[...] = jnp.zeros_like(o)

    def body_single(*refs):
        a_ref, b_ref = refs[:2]
        zero_sums(refs[o0:], 0)
        acc = lax.dot_general(a_ref[...].astype(BF16), b_ref[...].astype(BF16), dims, preferred_element_type=F32)
        finish(acc, refs[2:2 + n_ex], refs[o0:])

    def body_multi(*refs):
        a_ref, b_ref = refs[:2]
        acc = refs[-1]
        kk = pl.program_id(2)
        zero_sums(refs[o0:-1], kk)

        @pl.when(kk == 0)
        def _():
            acc[...] = jnp.zeros_like(acc)

        acc[...] += lax.dot_general(a_ref[...].astype(BF16), b_ref[...].astype(BF16), dims, preferred_element_type=F32)

        @pl.when(kk == nk - 1)
        def _():
            finish(acc[...], refs[2:2 + n_ex], refs[o0:-1])

    v_spec = pl.BlockSpec((1, tn), lambda g0, g1, kk: (0, ij(g0, g1)[1]))
    outs = _call(
        body_single if nk == 1 else body_multi, name=name, grid=grid,
        in_specs=([a_spec, b_spec] + [o_spec] * len(extras) + [v_spec] * len(vecs)
                  + [pl.BlockSpec(memory_space=pl.ANY)] * len(after)),
        out_specs=[o_spec] * n_out + [v_spec] * n_sums,
        out_shape=[jax.ShapeDtypeStruct((m, n), dt) for dt in out_dtypes] + [jax.ShapeDtypeStruct((1, n), F32)] * n_sums,
        scratch_shapes=[] if nk == 1 else [pltpu.VMEM((tm, tn), F32)],
        compiler_params=_params(("parallel", "parallel", "arbitrary") if n_sums == 0 else ("arbitrary",) * 3),
    )(a, b, *extras, *vecs, *after)
    return outs[0] if len(outs) == 1 else outs


def _epi_resid_norm(acc, x, g_res, g_next):
    xn = x + acc * _rms(acc) * g_res
    return xn, xn * _rms(xn) * g_next, acc


def _epi_bwd_norm_pair(acc, x, dres, n, g_in, g_out):
    dxh, dgi = _rms_bwd(x, _rms(x), g_in, acc)
    dx = dres + dxh
    nv = n.astype(F32)
    dn, dgo = _rms_bwd(nv, _rms(nv), g_out, dx)
    return dx, dn, dgi, dgo


def _epi_bwd_norm_in(acc, x, dres, dh_more, g_in):
    dxh, dgi = _rms_bwd(x, _rms(x), g_in, acc + dh_more.astype(F32))
    return dres + dxh, dgi


def _epi_relu2(acc):
    r = jnp.maximum(acc, 0.0)
    return (r * r,)


def _epi_drelu2(acc, f):
    return (acc * (2.0 * jnp.sqrt(f.astype(F32))),)


def _norm_fwd(x, g, *, name):
    t, d = x.shape
    tt = _pick_rows(t, ROW_TILE)

    def body(x_ref, g_ref, h_ref):
        xv = x_ref[...]
        h_ref[...] = (xv * _rms(xv) * g_ref[...]).astype(BF16)

    row = pl.BlockSpec((tt, d), lambda i: (i, 0))
    return _call(body, name=name, grid=(t // tt,), in_specs=[row, _full((1, d))], out_specs=row,
                 out_shape=jax.ShapeDtypeStruct((t, d), BF16), compiler_params=_params(("parallel",)))(x, g)


def _resid_norm(x, n, g1, g2, *, name):
    t, d = x.shape
    tt = _pick_rows(t, ROW_TILE)

    def body(x_ref, n_ref, g1_ref, g2_ref, xo_ref, h_ref):
        nv = n_ref[...].astype(F32)
        xn = x_ref[...] + nv * _rms(nv) * g1_ref[...]
        xo_ref[...] = xn
        h_ref[...] = (xn * _rms(xn) * g2_ref[...]).astype(BF16)

    row = pl.BlockSpec((tt, d), lambda i: (i, 0))
    return _call(body, name=name, grid=(t // tt,), in_specs=[row, row, _full((1, d)), _full((1, d))],
                 out_specs=[row, row],
                 out_shape=[jax.ShapeDtypeStruct((t, d), F32), jax.ShapeDtypeStruct((t, d), BF16)],
                 compiler_params=_params(("parallel",)))(x, n, g1, g2)


def _loss_fwd_bwd(xf, target, *, name):
    t, d = xf.shape
    tt = _pick_rows(t, ROW_TILE)
    nt = t // tt

    def body(x_ref, t_ref, dy_ref, loss_ref, acc):
        i = pl.program_id(0)

        @pl.when(i == 0)
        def _():
            acc[...] = jnp.zeros_like(acc)

        e = x_ref[...] - t_ref[...]
        dy_ref[...] = e * (1.0 / d)
        acc[...] += jnp.sum(e * e, axis=0, keepdims=True)

        @pl.when(i == nt - 1)
        def _():
            loss_ref[...] = jnp.sum(acc[...], axis=-1, keepdims=True) * (0.5 / d)

    row = pl.BlockSpec((tt, d), lambda i: (i, 0))
    return _call(body, name=name, grid=(nt,), in_specs=[row, row], out_specs=[row, _full((1, 1))],
                 out_shape=[jax.ShapeDtypeStruct((t, d), F32), jax.ShapeDtypeStruct((1, 1), F32)],
                 scratch_shapes=[pltpu.VMEM((1, d), F32)], compiler_params=_params(("arbitrary",)))(xf, target)


def _bwd_norm_pair(xin, dh, dres, n, g_in, g_out, *, name):
    t, d = xin.shape
    tt = _pick_rows(t, ROW_TILE)
    n_dh = len(dh)

    def body(*refs):
        x_ref = refs[0]
        dh_refs = refs[1:1 + n_dh]
        dres_ref, n_ref, gi_ref, go_ref, dx_ref, dn_ref, dgi_ref, dgo_ref = refs[1 + n_dh:]
        i = pl.program_id(0)

        @pl.when(i == 0)
        def _():
            dgi_ref[...] = jnp.zeros_like(dgi_ref)
            dgo_ref[...] = jnp.zeros_like(dgo_ref)

        xv = x_ref[...]
        dhv = dh_refs[0][...].astype(F32)
        for r in dh_refs[1:]:
            dhv = dhv + r[...].astype(F32)
        dxh, dgi = _rms_bwd(xv, _rms(xv), gi_ref[...], dhv)
        dx = dres_ref[...] + dxh
        dx_ref[...] = dx
        dgi_ref[...] += jnp.sum(dgi, axis=0, keepdims=True)
        nv = n_ref[...].astype(F32)
        dn, dgo = _rms_bwd(nv, _rms(nv), go_ref[...], dx)
        dn_ref[...] = dn.astype(BF16)
        dgo_ref[...] += jnp.sum(dgo, axis=0, keepdims=True)

    row = pl.BlockSpec((tt, d), lambda i: (i, 0))
    vec = _full((1, d))
    return _call(body, name=name, grid=(t // tt,), in_specs=[row] * (n_dh + 3) + [vec, vec],
                 out_specs=[row, row, vec, vec],
                 out_shape=[jax.ShapeDtypeStruct((t, d), F32), jax.ShapeDtypeStruct((t, d), BF16),
                            jax.ShapeDtypeStruct((1, d), F32), jax.ShapeDtypeStruct((1, d), F32)],
                 compiler_params=_params(("arbitrary",)))(xin, *dh, dres, n, g_in, g_out)


def _bwd_norm_in(xin, dh, dres, g_in, *, name):
    t, d = xin.shape
    tt = _pick_rows(t, ROW_TILE)
    n_dh = len(dh)

    def body(*refs):
        x_ref = refs[0]
        dh_refs = refs[1:1 + n_dh]
        dres_ref, gi_ref, dx_ref, dgi_ref = refs[1 + n_dh:]
        i = pl.program_id(0)

        @pl.when(i == 0)
        def _():
            dgi_ref[...] = jnp.zeros_like(dgi_ref)

        xv = x_ref[...]
        dhv = dh_refs[0][...].astype(F32)
        for r in dh_refs[1:]:
            dhv = dhv + r[...].astype(F32)
        dxh, dgi = _rms_bwd(xv, _rms(xv), gi_ref[...], dhv)
        dx_ref[...] = dres_ref[...] + dxh
        dgi_ref[...] += jnp.sum(dgi, axis=0, keepdims=True)

    row = pl.BlockSpec((tt, d), lambda i: (i, 0))
    vec = _full((1, d))
    return _call(body, name=name, grid=(t // tt,), in_specs=[row] * (n_dh + 2) + [vec],
                 out_specs=[row, vec],
                 out_shape=[jax.ShapeDtypeStruct((t, d), F32), jax.ShapeDtypeStruct((1, d), F32)],
                 compiler_params=_params(("arbitrary",)))(xin, *dh, dres, g_in)


def _bwd_norm_out(n, g_out, dx, *, name):
    t, d = n.shape
    tt = _pick_rows(t, ROW_TILE)

    def body(n_ref, go_ref, dx_ref, dn_ref, dgo_ref):
        i = pl.program_id(0)

        @pl.when(i == 0)
        def _():
            dgo_ref[...] = jnp.zeros_like(dgo_ref)

        nv = n_ref[...].astype(F32)
        dn, dgo = _rms_bwd(nv, _rms(nv), go_ref[...], dx_ref[...])
        dn_ref[...] = dn.astype(BF16)
        dgo_ref[...] += jnp.sum(dgo, axis=0, keepdims=True)

    row = pl.BlockSpec((tt, d), lambda i: (i, 0))
    vec = _full((1, d))
    return _call(body, name=name, grid=(t // tt,), in_specs=[row, vec, row], out_specs=[row, vec],
                 out_shape=[jax.ShapeDtypeStruct((t, d), BF16), jax.ShapeDtypeStruct((1, d), F32)],
                 compiler_params=_params(("arbitrary",)))(n, g_out, dx)


def _shift_down(cur, halo, s):
    return jnp.concatenate([halo[SUBLANES - s:], cur[:cur.shape[0] - s]], axis=0)


def _shift_up(cur, halo, s):
    return jnp.concatenate([cur[s:], halo[:s]], axis=0)


def _conva_fwd(pa, w, g, *, d, seq, name):
    t = pa.shape[0]
    tt = _pick_rows(seq, ROW_TILE)
    tps = seq // tt

    def body(xa_ref, ca_ref, ba_ref, w_ref, g_ref, ya_ref, v_ref, carry):
        i = pl.program_id(0)

        @pl.when(i % tps == 0)
        def _():
            carry[...] = jnp.zeros_like(carry)

        u = ca_ref[...].astype(F32) * xa_ref[...].astype(F32)
        halo = carry[...]
        wv = w_ref[...]
        v = wv[2:3] * u + wv[1:2] * _shift_down(u, halo, 1) + wv[0:1] * _shift_down(u, halo, 2)
        carry[...] = u[tt - SUBLANES:]
        yp = ba_ref[...].astype(F32) * v
        ya_ref[...] = (yp * _rms(yp) * g_ref[...]).astype(BF16)
        v_ref[...] = v.astype(BF16)

    col = lambda c: pl.BlockSpec((tt, d), lambda i, c=c: (i, c))
    row = pl.BlockSpec((tt, d), lambda i: (i, 0))
    return _call(body, name=name, grid=(t // tt,),
                 in_specs=[col(0), col(1), col(2), _full((CONV_K, d)), _full((1, d))], out_specs=[row, row],
                 out_shape=[jax.ShapeDtypeStruct((t, d), BF16), jax.ShapeDtypeStruct((t, d), BF16)],
                 scratch_shapes=[pltpu.VMEM((SUBLANES, d), F32)],
                 compiler_params=_params(("arbitrary",)))(pa, pa, pa, w, g)


def _conva_bwd(dcat, pa, v, w, g, *, d, seq, name):
    t, width = pa.shape
    d3 = 3 * d
    tt = _pick_rows(seq, ROW_TILE)
    tps = seq // tt
    nt = t // tt

    def body(dya_ref, xa_ref, ca_ref, ba_ref, v_ref, w_ref, g_ref, dpa_ref, dw_ref, dg_ref, carry):
        i = pl.program_id(0)

        @pl.when(i == 0)
        def _():
            dw_ref[...] = jnp.zeros_like(dw_ref)
            dg_ref[...] = jnp.zeros_like(dg_ref)

        @pl.when(i % tps == 0)
        def _():
            carry[...] = jnp.zeros_like(carry)

        xa, ca, ba, vv = [r[...].astype(F32) for r in (xa_ref, ca_ref, ba_ref, v_ref)]
        yp = ba * vv
        dyp, dgt = _rms_bwd(yp, _rms(yp), g_ref[...], dya_ref[...].astype(F32))
        dg_ref[...] += jnp.sum(dgt, axis=0, keepdims=True)
        dv = dyp * ba
        halo = carry[...]
        dv1 = _shift_up(dv, halo, 1)
        dv2 = _shift_up(dv, halo, 2)
        carry[...] = dv[:SUBLANES]
        wv = w_ref[...]
        du = wv[2:3] * dv + wv[1:2] * dv1 + wv[0:1] * dv2
        u = ca * xa
        dw_ref[0:1, :] += jnp.sum(u * dv2, axis=0, keepdims=True)
        dw_ref[1:2, :] += jnp.sum(u * dv1, axis=0, keepdims=True)
        dw_ref[2:3, :] += jnp.sum(u * dv, axis=0, keepdims=True)
        dpa_ref[:, 0:d] = (du * ca).astype(BF16)
        dpa_ref[:, d:2 * d] = (du * xa).astype(BF16)
        dpa_ref[:, 2 * d:3 * d] = (dyp * vv).astype(BF16)

    rcol = lambda c: pl.BlockSpec((tt, d), lambda i, c=c: (nt - 1 - i, c))
    return _call(body, name=name, grid=(nt,),
                 in_specs=[rcol(0), rcol(0), rcol(1), rcol(2), rcol(0), _full((CONV_K, d)), _full((1, d))],
                 out_specs=[pl.BlockSpec((tt, d3), lambda i: (nt - 1 - i, 0)), _full((CONV_K, d)), _full((1, d))],
                 out_shape=[jax.ShapeDtypeStruct((t, width), BF16), jax.ShapeDtypeStruct((CONV_K, d), F32),
                            jax.ShapeDtypeStruct((1, d), F32)],
                 scratch_shapes=[pltpu.VMEM((SUBLANES, d), F32)],
                 compiler_params=_params(("arbitrary",)))(dcat, pa, pa, pa, v, w, g)


CONV_CH = 512


def _convb_fwd(proj, w, bias, *, col0, seq, name):
    t = proj.shape[0]
    c = w.shape[1]
    cb = _pick(c, CONV_CH)
    assert col0 % cb == 0
    tt = _pick_rows(seq, 2 * ROW_TILE)
    tps = seq // tt

    def body(p_ref, w_ref, b_ref, o_ref, carry):
        i = pl.program_id(1)

        @pl.when(i % tps == 0)
        def _():
            carry[...] = jnp.zeros_like(carry)

        p = p_ref[...].astype(F32)
        halo = carry[...]
        wv = w_ref[...]
        o = wv[3:4] * p + b_ref[...]
        for s in (1, 2, 3):
            o = o + wv[3 - s:4 - s] * _shift_down(p, halo, s)
        carry[...] = p[tt - SUBLANES:]
        o_ref[...] = o.astype(BF16)

    return _call(body, name=name, grid=(c // cb, t // tt),
                 in_specs=[pl.BlockSpec((tt, cb), lambda jc, i: (i, col0 // cb + jc)),
                           pl.BlockSpec((SSM_CONV_K, cb), lambda jc, i: (0, jc)), pl.BlockSpec((1, cb), lambda jc, i: (0, jc))],
                 out_specs=pl.BlockSpec((tt, cb), lambda jc, i: (i, jc)), out_shape=jax.ShapeDtypeStruct((t, c), BF16),
                 scratch_shapes=[pltpu.VMEM((SUBLANES, cb), F32)],
                 compiler_params=_params(("arbitrary", "arbitrary")))(proj, w, bias)


def _convb_bwd(dconv, proj, w, dproj, *, col0, seq, name):
    t, c = dconv.shape
    cb = _pick(c, CONV_CH)
    assert col0 % cb == 0
    tt = _pick_rows(seq, 2 * ROW_TILE)
    tps = seq // tt
    nt = t // tt

    def body(dc_ref, p_ref, w_ref, dproj_in, dp_ref, dw_ref, db_ref, carry):
        del dproj_in
        i = pl.program_id(1)

        @pl.when(i == 0)
        def _():
            dw_ref[...] = jnp.zeros_like(dw_ref)
            db_ref[...] = jnp.zeros_like(db_ref)

        @pl.when(i % tps == 0)
        def _():
            carry[...] = jnp.zeros_like(carry)

        dc = dc_ref[...].astype(F32)
        p = p_ref[...].astype(F32)
        halo = carry[...]
        wv = w_ref[...]
        dp = wv[3:4] * dc
        dw_ref[3:4, :] += jnp.sum(p * dc, axis=0, keepdims=True)
        for s in (1, 2, 3):
            dcs = _shift_up(dc, halo, s)
            dp = dp + wv[3 - s:4 - s] * dcs
            dw_ref[3 - s:4 - s, :] += jnp.sum(p * dcs, axis=0, keepdims=True)
        carry[...] = dc[:SUBLANES]
        db_ref[...] += jnp.sum(dc, axis=0, keepdims=True)
        dp_ref[...] = dp.astype(BF16)

    win_spec = pl.BlockSpec((tt, cb), lambda jc, i: (nt - 1 - i, col0 // cb + jc))
    taps = pl.BlockSpec((SSM_CONV_K, cb), lambda jc, i: (0, jc))
    return _call(body, name=name, grid=(c // cb, nt),
                 in_specs=[pl.BlockSpec((tt, cb), lambda jc, i: (nt - 1 - i, jc)), win_spec, taps,
                           pl.BlockSpec(memory_space=pl.ANY)],
                 out_specs=[win_spec, taps, pl.BlockSpec((1, cb), lambda jc, i: (0, jc))],
                 out_shape=[jax.ShapeDtypeStruct(dproj.shape, BF16), jax.ShapeDtypeStruct((SSM_CONV_K, c), F32),
                            jax.ShapeDtypeStruct((1, c), F32)],
                 input_output_aliases={3: 0},
                 scratch_shapes=[pltpu.VMEM((SUBLANES, cb), F32)],
                 compiler_params=_params(("arbitrary", "arbitrary")))(dconv, proj, w, dproj)


def _expand_heads(x, ev):
    return jnp.dot(x, ev, precision=HIGHEST, preferred_element_type=F32)


def _head_sums(v, ev):
    return lax.dot_general(v, ev, (((1,), (1,)), ((), ())), precision=HIGHEST, preferred_element_type=F32)


def _ssd_common(c_ref, pdt_ref, dtb_ref, alog_ref, e_ref, h):
    cp = c_ref[...].astype(F32)
    sg = _sigmoid(cp)
    act = cp * sg
    pre = pdt_ref[:, 0:h] + dtb_ref[...]
    dt = _softplus(pre)
    a = -jnp.exp(alog_ref[...])
    adt = dt * a
    row = lax.broadcasted_iota(jnp.int32, (CHUNK, CHUNK), 0)
    col = lax.broadcasted_iota(jnp.int32, (CHUNK, CHUNK), 1)
    tril = row >= col
    cs = jnp.dot(tril.astype(F32), adt, precision=HIGHEST, preferred_element_type=F32)
    cs_t = lax.dot_general(adt, (col >= row).astype(F32), (((0,), (0,)), ((), ())), precision=HIGHEST,
                           preferred_element_type=F32)
    ev = e_ref[...]
    dt_l = _expand_heads(dt, ev)
    ecs_l = jnp.exp(_expand_heads(cs, ev))
    return dict(cp=cp, sg=sg, act=act, pre=pre, dt=dt, a=a, cs=cs, cs_t=cs_t, dt_l=dt_l, ecs_l=ecs_l,
                tril=tril, row=row, col=col, lo=col < HEAD_DIM)


def _dot_nt(a, b):
    return lax.dot_general(a, b, (((1,), (1,)), ((), ())), preferred_element_type=F32)


def _dot_tn(a, b):
    return lax.dot_general(a, b, (((0,), (0,)), ((), ())), preferred_element_type=F32)


def _dot(a, b):
    return jnp.dot(a, b, preferred_element_type=F32)


def _ssd_fwd(cpre, pdt, pz, ya, dtb, alog, dsk_lane, gs, emat, *, nseq, seq, name):
    t, xbc = cpre.shape
    d = ya.shape[1]
    h = d // HEAD_DIM
    npair = h // 2
    ppg = npair // SSM_GROUPS
    nc = seq // CHUNK
    gw = d // SSM_GROUPS
    bc0 = d
    cc0 = d + SSM_GROUPS * D_STATE

    def body(c_ref, pdt_ref, z_ref, ya_ref, dtb_ref, alog_ref, dsk_ref, gs_ref, e_ref, cat_ref, y2_ref, hp_ref, h_ref):
        @pl.when(pl.program_id(0) == 0)
        def _():
            h_ref[...] = jnp.zeros_like(h_ref)

        for sq in range(nseq):
            one_seq(c_ref.at[sq], pdt_ref.at[sq], z_ref.at[sq], ya_ref.at[sq], dtb_ref, alog_ref, dsk_ref, gs_ref, e_ref,
                    cat_ref.at[sq], y2_ref.at[sq], hp_ref.at[sq], h_ref.at[sq])

    def one_seq(c_ref, pdt_ref, z_ref, ya_ref, dtb_ref, alog_ref, dsk_ref, gs_ref, e_ref, cat_ref, y2_ref, hp_ref, h_ref):
        q = _ssd_common(c_ref, pdt_ref, dtb_ref, alog_ref, e_ref, h)
        act, cs, lo, ecs_l = q["act"], q["cs"], q["lo"], q["ecs_l"]
        xs = act[:, :d]
        xd = xs * q["dt_l"]
        ys = []
        for g in range(SSM_GROUPS):
            bg = act[:, bc0 + g * D_STATE: bc0 + (g + 1) * D_STATE]
            cgb = act[:, cc0 + g * D_STATE: cc0 + (g + 1) * D_STATE].astype(BF16)
            s = _dot_nt(cgb, bg.astype(BF16))
            for jj in range(ppg):
                j = g * ppg + jj
                sl = slice(LANES * j, LANES * (j + 1))
                xdj = xd[:, sl]
                x2 = jnp.concatenate([jnp.where(lo, xdj, 0.0), jnp.where(lo, 0.0, xdj)], axis=0).astype(BF16)
                hprev = h_ref[j]
                hp_ref[j] = hprev.astype(BF16)
                ms, bws = [], []
                for hh in (2 * j, 2 * j + 1):
                    csc = cs[:, hh:hh + 1]
                    seg = jnp.broadcast_to(csc, (CHUNK, CHUNK)) - jnp.broadcast_to(q["cs_t"][hh:hh + 1, :], (CHUNK, CHUNK))
                    ms.append(s * jnp.exp(jnp.where(q["tril"], seg, -jnp.inf)))
                    bws.append(bg * jnp.exp(cs[CHUNK - 1:CHUNK, hh:hh + 1] - csc))
                ydiag = _dot(jnp.concatenate(ms, axis=1).astype(BF16), x2)
                st = _dot_tn(jnp.concatenate(bws, axis=0).astype(BF16), x2)
                ecs = ecs_l[:, sl]
                yoff = _dot(cgb, hprev.astype(BF16)) * ecs
                h_ref[j] = hprev * ecs[CHUNK - 1:CHUNK] + st
                ys.append(ydiag + yoff)
        y = jnp.concatenate(ys, axis=1) + dsk_ref[...] * xs
        y2_ref[...] = y.astype(BF16)
        zv = z_ref[...].astype(F32)
        y3 = y * (zv * _sigmoid(zv))
        cat_ref[:, 0:d] = ya_ref[...]
        for gi in range(SSM_GROUPS):
            seg = y3[:, gi * gw:(gi + 1) * gw]
            cat_ref[:, d + gi * gw:d + (gi + 1) * gw] = (seg * _rms(seg) * gs_ref[:, gi * gw:(gi + 1) * gw]).astype(BF16)

    chunk = lambda w, cb=0: pl.BlockSpec((nseq, CHUNK, w), lambda c, cb=cb: (0, c, cb))
    vec = lambda w: pl.BlockSpec((1, w), lambda c: (0, 0))
    hp_spec = pl.BlockSpec((nseq, None, npair, D_STATE, LANES), lambda c: (0, c, 0, 0, 0))
    per_seq = lambda a: a.reshape(nseq, seq, a.shape[1])
    cat, y2, hp = _call(
        body, name=name, grid=(nc,),
        in_specs=[chunk(xbc), chunk(LANES), chunk(d, 3), chunk(d), vec(h), vec(h), vec(d), vec(d),
                  pl.BlockSpec((h, d), lambda c: (0, 0))],
        out_specs=[chunk(2 * d), chunk(d), hp_spec],
        out_shape=[jax.ShapeDtypeStruct((nseq, seq, 2 * d), BF16), jax.ShapeDtypeStruct((nseq, seq, d), BF16),
                   jax.ShapeDtypeStruct((nseq, nc, npair, D_STATE, LANES), BF16)],
        scratch_shapes=[pltpu.VMEM((nseq, npair, D_STATE, LANES), F32)],
        compiler_params=_params(("arbitrary",)))(
            per_seq(cpre), per_seq(pdt), per_seq(pz), per_seq(ya), dtb, alog, dsk_lane, gs, emat)
    return cat.reshape(t, 2 * d), y2.reshape(t, d), hp


def _ssd_bwd(cpre, pdt, pz, y2, hprev_all, dcat, dtb, alog, dsk_lane, gs, emat, dproj, *, nseq, seq, name):
    t, xbc = cpre.shape
    d = y2.shape[1]
    h = d // HEAD_DIM
    npair = h // 2
    ppg = npair // SSM_GROUPS
    nc = seq // CHUNK
    gw = d // SSM_GROUPS
    bc0 = d
    cc0 = d + SSM_GROUPS * D_STATE

    def body(c_ref, pdt_ref, z_ref, y2_ref, hp_ref, dys_ref, dtb_ref, alog_ref, dsk_ref, gs_ref, e_ref, dproj_in,
             dconv_ref, dz_ref, dpdt_ref, dgs_ref, ddsk_ref, ddtb_ref, dalog_ref, dh_ref):
        del dproj_in
        b = pl.program_id(0)
        c = pl.program_id(1)

        @pl.when(c == 0)
        def _():
            dh_ref[...] = jnp.zeros_like(dh_ref)

        @pl.when((b == 0) & (c == 0))
        def _():
            dgs_ref[...] = jnp.zeros_like(dgs_ref)
            ddsk_ref[...] = jnp.zeros_like(ddsk_ref)
            ddtb_ref[...] = jnp.zeros_like(ddtb_ref)
            dalog_ref[...] = jnp.zeros_like(dalog_ref)

        q = _ssd_common(c_ref, pdt_ref, dtb_ref, alog_ref, e_ref, h)
        cp, sg, act, cs, a, dt, lo = q["cp"], q["sg"], q["act"], q["cs"], q["a"], q["dt"], q["lo"]
        ecs_l, dt_l = q["ecs_l"], q["dt_l"]
        ev = e_ref[...]
        xs = act[:, :d]
        xd = xs * dt_l
        row16 = lax.broadcasted_iota(jnp.int32, (CHUNK, h), 0)
        hid = lax.broadcasted_iota(jnp.int32, (1, h), 1)
        hid_t = lax.broadcasted_iota(jnp.int32, (h, 1), 0)

        zv = z_ref[...].astype(F32)
        sz = _sigmoid(zv)
        siluz = zv * sz
        y2v = y2_ref[...].astype(F32)
        y3 = y2v * siluz
        dysv = dys_ref[...].astype(F32)
        dy3s = []
        for gi in range(SSM_GROUPS):
            gsl = slice(gi * gw, (gi + 1) * gw)
            seg = y3[:, gsl]
            dseg, dgt = _rms_bwd(seg, _rms(seg), gs_ref[:, gsl], dysv[:, gsl])
            dy3s.append(dseg)
            dgs_ref[:, gsl] += jnp.sum(dgt, axis=0, keepdims=True)
        dy3 = jnp.concatenate(dy3s, axis=1)
        dy = dy3 * siluz
        dz_ref[...] = (dy3 * y2v * (sz * (1.0 + zv * (1.0 - sz)))).astype(BF16)
        ddsk_ref[...] += jnp.sum(_head_sums(dy * xs, ev), axis=0, keepdims=True)

        dcs = jnp.zeros((CHUNK, h), F32)
        dcs_t = jnp.zeros((h, CHUNK), F32)
        dxd_parts, yoff_parts, db_parts, dc_parts = [], [], [], []
        for g in range(SSM_GROUPS):
            bg = act[:, bc0 + g * D_STATE: bc0 + (g + 1) * D_STATE]
            cg = act[:, cc0 + g * D_STATE: cc0 + (g + 1) * D_STATE]
            bgb, cgb = bg.astype(BF16), cg.astype(BF16)
            s = _dot_nt(cgb, bgb)
            ds = jnp.zeros((CHUNK, CHUNK), F32)
            dbg = jnp.zeros((CHUNK, D_STATE), F32)
            dcg = jnp.zeros((CHUNK, D_STATE), F32)
            for jj in range(ppg):
                j = g * ppg + jj
                sl = slice(LANES * j, LANES * (j + 1))
                xdj = xd[:, sl]
                xdb = xdj.astype(BF16)
                x2 = jnp.concatenate([jnp.where(lo, xdj, 0.0), jnp.where(lo, 0.0, xdj)], axis=0).astype(BF16)
                dyj = dy[:, sl]
                dy2 = jnp.concatenate([jnp.where(lo, dyj, 0.0), jnp.where(lo, 0.0, dyj)], axis=0).astype(BF16)
                hpb = hp_ref[j]
                hprev = hpb.astype(F32)
                dhn = dh_ref[j]
                dhb = dhn.astype(BF16)
                dh2 = jnp.concatenate([jnp.where(lo, dhn, 0.0), jnp.where(lo, 0.0, dhn)], axis=0).astype(BF16)
                ecs = ecs_l[:, sl]
                gmat = (dyj * ecs).astype(BF16)
                yoff_parts.append(_dot(cgb, hpb) * ecs)
                dcg = dcg + _dot_nt(gmat, hpb)
                dh_ref[j] = dhn * ecs[CHUNK - 1:CHUNK] + _dot_tn(cgb, gmat)
                t2 = dhn * hprev
                dbw2 = _dot_nt(x2, dhb)
                dm2 = _dot_nt(dy2, xdb)
                ms, bws = [], []
                for idx, hh in enumerate((2 * j, 2 * j + 1)):
                    msk = lo if idx == 0 else jnp.logical_not(lo)
                    onehot = (hid == hh).astype(F32)
                    csc = cs[:, hh:hh + 1]
                    seg = jnp.broadcast_to(csc, (CHUNK, CHUNK)) - jnp.broadcast_to(q["cs_t"][hh:hh + 1, :], (CHUNK, CHUNK))
                    lm = jnp.exp(jnp.where(q["tril"], seg, -jnp.inf))
                    m = s * lm
                    cs_last = cs[CHUNK - 1:CHUNK, hh:hh + 1]
                    dte = jnp.exp(cs_last - csc)
                    ms.append(m)
                    bws.append(bg * dte)
                    dbw = dbw2[idx * CHUNK:(idx + 1) * CHUNK]
                    dbg = dbg + dbw * dte
                    qv = jnp.sum(dbw * bg, axis=-1, keepdims=True) * dte
                    dm = dm2[idx * CHUNK:(idx + 1) * CHUNK]
                    wm = dm * m
                    rc = jnp.sum(wm, axis=-1, keepdims=True)
                    dcs_t = dcs_t - (hid_t == hh).astype(F32) * jnp.sum(wm, axis=0, keepdims=True)
                    ds = ds + dm * lm
                    ddec = jnp.sum(jnp.where(msk, t2, 0.0)) * jnp.exp(cs_last)
                    last = jnp.sum(qv) + ddec
                    dcs = dcs + (rc - qv) * onehot + jnp.where(row16 == CHUNK - 1, last * onehot, 0.0)
                dxd_s = _dot(jnp.concatenate(bws, axis=1).astype(BF16), dh2)
                dxd_d = _dot_tn(jnp.concatenate(ms, axis=0).astype(BF16), dy2)
                dxd_parts.append(dxd_s + dxd_d)
            dsb = ds.astype(BF16)
            dc_parts.append(dcg + _dot(dsb, bgb))
            db_parts.append(dbg + _dot_tn(dsb, cgb))
        yoff_all = jnp.concatenate(yoff_parts, axis=1)
        dxd_all = jnp.concatenate(dxd_parts, axis=1)
        dcs = dcs + _head_sums(dy * yoff_all, ev)
        triu = (q["col"] >= q["row"]).astype(F32)
        dadt = (jnp.dot(triu, dcs, precision=HIGHEST, preferred_element_type=F32)
                + lax.dot_general(triu, dcs_t, (((1,), (1,)), ((), ())), precision=HIGHEST, preferred_element_type=F32))
        ddt = dadt * a + _head_sums(dxd_all * xs, ev)
        dalog_ref[...] += jnp.sum(dadt * dt, axis=0, keepdims=True) * a
        dpre = ddt * _sigmoid(q["pre"])
        ddtb_ref[...] += jnp.sum(dpre, axis=0, keepdims=True)
        dpdt_ref[...] = jnp.zeros_like(dpdt_ref)
        dpdt_ref[:, 0:h] = dpre.astype(BF16)
        dxs = dxd_all * dt_l + dy * dsk_ref[...]
        dact = jnp.concatenate([dxs] + db_parts + dc_parts, axis=1)
        dconv_ref[...] = (dact * (sg * (1.0 + cp * (1.0 - sg)))).astype(BF16)

    rchunk = lambda w, cb=0: pl.BlockSpec((CHUNK, w), lambda b, c, cb=cb: (b * nc + nc - 1 - c, cb))
    vec = lambda w: pl.BlockSpec((1, w), lambda b, c: (0, 0))
    hp_spec = pl.BlockSpec((None, None, npair, D_STATE, LANES), lambda b, c: (b, nc - 1 - c, 0, 0, 0))
    return _call(body, name=name, grid=(nseq, nc),
                 in_specs=[rchunk(xbc), rchunk(LANES), rchunk(d, 3), rchunk(d), hp_spec, rchunk(d, 1),
                           vec(h), vec(h), vec(d), vec(d), pl.BlockSpec((h, d), lambda b, c: (0, 0)),
                           pl.BlockSpec(memory_space=pl.ANY)],
                 out_specs=[rchunk(xbc), rchunk(d, 3), rchunk(LANES), vec(d), vec(h), vec(h), vec(h)],
                 out_shape=[jax.ShapeDtypeStruct((t, xbc), BF16), jax.ShapeDtypeStruct(dproj.shape, BF16),
                            jax.ShapeDtypeStruct((t, LANES), BF16), jax.ShapeDtypeStruct((1, d), F32),
                            jax.ShapeDtypeStruct((1, h), F32), jax.ShapeDtypeStruct((1, h), F32),
                            jax.ShapeDtypeStruct((1, h), F32)],
                 input_output_aliases={11: 1},
                 scratch_shapes=[pltpu.VMEM((npair, D_STATE, LANES), F32)],
                 compiler_params=_params(("arbitrary", "arbitrary")))(
                     cpre, pdt, pz, y2, hprev_all, dcat, dtb, alog, dsk_lane, gs, emat, dproj)


def _sum_adamw(parts, w, m, v, *, name, layer=None, outs=None):
    n, r, c = parts.shape
    tr = _pick_rows(r, 256)
    bc1 = 1.0 - ADAM_B1 ** ADAM_STEP
    bc2 = 1.0 - ADAM_B2 ** ADAM_STEP

    def body(p_ref, w_ref, m_ref, v_ref, *rest):
        g_ref, d_ref, mo_ref, vo_ref = rest[-4:]
        g = p_ref[0].astype(F32)
        for k in range(1, n):
            g = g + p_ref[k].astype(F32)
        mn = ADAM_B1 * m_ref[...] + (1.0 - ADAM_B1) * g
        vn = ADAM_B2 * v_ref[...] + (1.0 - ADAM_B2) * (g * g)
        g_ref[...] = g
        mo_ref[...] = mn
        vo_ref[...] = vn
        d_ref[...] = -ADAM_LR * ((mn / bc1) / (jnp.sqrt(vn / bc2) + ADAM_EPS) + ADAM_WD * w_ref[...])

    p_spec = pl.BlockSpec((n, tr, c), lambda i: (0, i, 0))
    if layer is None:
        blk = pl.BlockSpec((tr, c), lambda i: (i, 0))
        return _call(body, name=name, grid=(r // tr,), in_specs=[p_spec, blk, blk, blk], out_specs=[blk] * 4,
                     out_shape=[jax.ShapeDtypeStruct((r, c), F32)] * 4,
                     compiler_params=_params(("parallel",)))(parts, w, m, v)
    blk = pl.BlockSpec((None, tr, c), lambda i: (layer, i, 0))
    if outs is None:
        outs = [lax.empty(w.shape, F32) for _ in range(4)]
    return _call(body, name=name, grid=(r // tr,),
                 in_specs=[p_spec, blk, blk, blk] + [pl.BlockSpec(memory_space=pl.ANY)] * 4, out_specs=[blk] * 4,
                 out_shape=[jax.ShapeDtypeStruct(w.shape, F32)] * 4, input_output_aliases={4 + k: k for k in range(4)},
                 compiler_params=_params(("parallel",)))(parts, w, m, v, *outs)


def _assemble_cols(blocks, *, name):
    nb, r, c = blocks.shape
    width = -(-nb * c // LANES) * LANES
    tr = _pick_rows(r, 256)

    def body(b_ref, o_ref):
        pieces = [b_ref[j] for j in range(nb)]
        if width > nb * c:
            pieces.append(jnp.zeros((tr, width - nb * c), blocks.dtype))
        o_ref[...] = jnp.concatenate(pieces, axis=1)

    return _call(body, name=name, grid=(r // tr,), in_specs=[pl.BlockSpec((nb, tr, c), lambda i: (0, i, 0))],
                 out_specs=pl.BlockSpec((tr, width), lambda i: (i, 0)), out_shape=jax.ShapeDtypeStruct((r, width), blocks.dtype),
                 compiler_params=_params(("parallel",)))(blocks)


def _split_cols(pieces, c, *, name):
    r = pieces[0].shape[0]
    tr = _pick_rows(r, 256)
    n_in = len(pieces)

    def body(*refs):
        o_ref = refs[n_in]
        x = jnp.concatenate([p[...] for p in refs[:n_in]], axis=1) if n_in > 1 else refs[0][...]
        for j in range(N_DEV):
            o_ref[j] = x[:, c * j:c * (j + 1)]

    return _call(body, name=name, grid=(r // tr,),
                 in_specs=[pl.BlockSpec((tr, p.shape[1]), lambda i: (i, 0)) for p in pieces],
                 out_specs=pl.BlockSpec((N_DEV, tr, c), lambda i: (0, i, 0)),
                 out_shape=jax.ShapeDtypeStruct((N_DEV, r, c), pieces[0].dtype),
                 compiler_params=_params(("parallel",)))(*pieces)


def _sum_parts(parts, *, name):
    n, r, c = parts.shape
    tr = _pick_rows(r, 256)

    def body(p_ref, g_ref):
        g = p_ref[0].astype(F32)
        for k in range(1, n):
            g = g + p_ref[k].astype(F32)
        g_ref[...] = g

    return _call(body, name=name, grid=(r // tr,), in_specs=[pl.BlockSpec((n, tr, c), lambda i: (0, i, 0))],
                 out_specs=pl.BlockSpec((tr, c), lambda i: (i, 0)), out_shape=jax.ShapeDtypeStruct((r, c), F32),
                 compiler_params=_params(("parallel",)))(parts)


def _peers():
    x, y, c = lax.axis_index("x"), lax.axis_index("y"), lax.axis_index("c")
    me = 4 * x + 2 * y + c
    out = []
    for k in range(1, N_DEV):
        px = (1 - x) if (k >> 2) & 1 else x
        py = (1 - y) if (k >> 1) & 1 else y
        pc = (1 - c) if k & 1 else c
        out.append(((px, py, pc), 4 * px + 2 * py + pc))
    return me, out


def _exchange(src, *, gather, name):
    shape = src.shape if gather else src.shape[1:]

    def body(s_ref, o_ref, send_sems, recv_sems, local_sem):
        me, peers = _peers()
        mine = pltpu.make_async_copy(s_ref if gather else s_ref.at[me], o_ref.at[me], local_sem)
        mine.start()
        sends = []
        for k, (dev, pid) in enumerate(peers):
            cp = pltpu.make_async_remote_copy(
                src_ref=s_ref if gather else s_ref.at[pid], dst_ref=o_ref.at[me],
                send_sem=send_sems.at[k], recv_sem=recv_sems.at[k], device_id=dev, device_id_type=MESH)
            cp.start()
            sends.append(cp)
        for k, (dev, pid) in enumerate(peers):
            pltpu.make_async_remote_copy(
                src_ref=s_ref if gather else s_ref.at[pid], dst_ref=o_ref.at[pid],
                send_sem=send_sems.at[k], recv_sem=recv_sems.at[k], device_id=dev, device_id_type=MESH).wait_recv()
        for cp in sends:
            cp.wait_send()
        mine.wait()

    any_spec = pl.BlockSpec(memory_space=pl.ANY)
    return _call(body, name=name, in_specs=[any_spec], out_specs=any_spec,
                 out_shape=jax.ShapeDtypeStruct((N_DEV,) + tuple(shape), src.dtype),
                 scratch_shapes=[pltpu.SemaphoreType.DMA((N_DEV - 1,)), pltpu.SemaphoreType.DMA((N_DEV - 1,)),
                                 pltpu.SemaphoreType.DMA(())])(src)


_HBM = pl.BlockSpec(memory_space=pltpu.HBM)
_SEM = pl.BlockSpec(memory_space=pltpu.SEMAPHORE)
_EFFECT = pltpu.SideEffectType.DATAFLOW_SIDE_EFFECTING


def _split_copies(s_refs, l_refs, send_sems, recv_sems, gather, incoming):
    me, peers = _peers()
    local, remote = [], []
    for ti, (s_ref, l_ref) in enumerate(zip(s_refs, l_refs)):
        base = ti * N_DEV
        local.append(pltpu.make_async_copy(s_ref if gather else s_ref.at[me], l_ref.at[me], recv_sems.at[base + N_DEV - 1]))
        for k, (dev, pid) in enumerate(peers):
            sems = dict(send_sem=send_sems.at[base + k], recv_sem=recv_sems.at[base + k], device_id=dev, device_id_type=MESH)
            src = s_ref if gather else s_ref.at[pid]
            remote.append((
                pltpu.make_async_remote_copy(src_ref=src, dst_ref=l_ref.at[me], **sems),
                pltpu.make_async_remote_copy(src_ref=src, dst_ref=l_ref.at[pid], **sems) if incoming else None))
    return local, remote


def _exchange_start(srcs, *, gather, name, after=()):
    n = len(srcs)
    after = list(after)
    srcs = [pltpu.with_memory_space_constraint(s, pltpu.HBM) for s in srcs]
    lands = [pltpu.with_memory_space_constraint(
        lax.empty((N_DEV,) + tuple(s.shape if gather else s.shape[1:]), s.dtype), pltpu.HBM) for s in srcs]

    def body(*refs):
        s_refs, l_refs = refs[:n], refs[n:2 * n]
        outs = refs[2 * n + len(after):]
        send_sems, recv_sems, token = outs[0], outs[1], outs[-1]
        local, remote = _split_copies(s_refs, l_refs, send_sems, recv_sems, gather, incoming=False)
        for cp in local:
            cp.start()
        for out_cp, _ in remote:
            out_cp.start()
        token[...] = jnp.zeros_like(token)

    outs = _call(
        body, name=name,
        out_shape=(pltpu.SemaphoreType.DMA((n * N_DEV,)), pltpu.SemaphoreType.DMA((n * N_DEV,)),
                   *[pltpu.HBM(s.shape, s.dtype) for s in srcs], *[pltpu.HBM(l.shape, l.dtype) for l in lands],
                   jax.ShapeDtypeStruct((SUBLANES, LANES), F32)),
        in_specs=[_HBM] * (2 * n) + [pl.BlockSpec(memory_space=pl.ANY)] * len(after),
        out_specs=(_SEM, _SEM, *[_HBM] * (2 * n), pl.BlockSpec(memory_space=pltpu.VMEM)),
        input_output_aliases={k: k + 2 for k in range(2 * n)},
        compiler_params=pltpu.CompilerParams(has_side_effects=_EFFECT),
    )(*srcs, *lands, *after)
    return dict(n=n, gather=gather, sems=outs[:2], srcs=outs[2:2 + n], lands=outs[2 + n:2 + 2 * n]), outs[-1]


def _exchange_wait(state, after, *, name):
    n, gather = state["n"], state["gather"]
    after = list(after)

    def body(*refs):
        s_refs, l_refs = refs[:n], refs[n:2 * n]
        send_sems, recv_sems = refs[2 * n], refs[2 * n + 1]
        local, remote = _split_copies(s_refs, l_refs, send_sems, recv_sems, gather, incoming=True)
        for out_cp, in_cp in remote:
            out_cp.wait_send()
            in_cp.wait_recv()
        for cp in local:
            cp.wait()

    outs = _call(
        body, name=name,
        out_shape=tuple(pltpu.HBM(a.shape, a.dtype) for a in (*state["srcs"], *state["lands"])),
        in_specs=[_HBM] * (2 * n) + [_SEM, _SEM] + [pl.BlockSpec(memory_space=pl.ANY)] * len(after),
        out_specs=tuple([_HBM] * (2 * n)),
        input_output_aliases={k: k for k in range(2 * n)},
        compiler_params=pltpu.CompilerParams(has_side_effects=_EFFECT),
    )(*state["srcs"], *state["lands"], *state["sems"], *after)
    return outs[n:]


def _pack(arrs):
    flat = jnp.concatenate([a.reshape(-1).astype(F32) for a in arrs])
    pad = (-flat.shape[0]) % (SUBLANES * LANES)
    return jnp.pad(flat, (0, pad)).reshape(-1, LANES)


def _unpack(packed, shapes):
    flat = packed.reshape(-1)
    out, off = [], 0
    for s in shapes:
        n = 1
        for v in s:
            n *= v
        out.append(flat[off:off + n].reshape(s))
        off += n
    return out


SMALL = ("norm_mix_pre", "ssm_conv_b", "dt_bias", "a_log", "d_skip", "conv_out_norm", "ssm_out_norm",
         "norm_mix_post", "norm_mlp_pre", "norm_mlp_post", "conv_a_w", "ssm_conv_w")
BIG = ("w_in", "w_out", "w_up", "w_down")
ORDER = ("norm_mix_pre", "w_in", "conv_a_w", "ssm_conv_w", "ssm_conv_b", "dt_bias", "a_log", "d_skip",
         "conv_out_norm", "ssm_out_norm", "w_out", "norm_mix_post", "norm_mlp_pre", "w_up", "w_down", "norm_mlp_post")


def kernel(x, norm_mix_pre, w_in, conv_a_w, ssm_conv_w, ssm_conv_b, dt_bias, a_log, d_skip, conv_out_norm, ssm_out_norm, w_out, norm_mix_post, norm_mlp_pre, w_up, w_down, norm_mlp_post, loss_target, m_norm_mix_pre, m_w_in, m_conv_a_w, m_ssm_conv_w, m_ssm_conv_b, m_dt_bias, m_a_log, m_d_skip, m_conv_out_norm, m_ssm_out_norm, m_w_out, m_norm_mix_post, m_norm_mlp_pre, m_w_up, m_w_down, m_norm_mlp_post, v_norm_mix_pre, v_w_in, v_conv_a_w, v_ssm_conv_w, v_ssm_conv_b, v_dt_bias, v_a_log, v_d_skip, v_conv_out_norm, v_ssm_out_norm, v_w_out, v_norm_mix_post, v_norm_mlp_pre, v_w_up, v_w_down, v_norm_mlp_post):
    W = dict(norm_mix_pre=norm_mix_pre, w_in=w_in, conv_a_w=conv_a_w, ssm_conv_w=ssm_conv_w, ssm_conv_b=ssm_conv_b,
             dt_bias=dt_bias, a_log=a_log, d_skip=d_skip, conv_out_norm=conv_out_norm, ssm_out_norm=ssm_out_norm,
             w_out=w_out, norm_mix_post=norm_mix_post, norm_mlp_pre=norm_mlp_pre, w_up=w_up, w_down=w_down,
             norm_mlp_post=norm_mlp_post)
    M = dict(norm_mix_pre=m_norm_mix_pre, w_in=m_w_in, conv_a_w=m_conv_a_w, ssm_conv_w=m_ssm_conv_w,
             ssm_conv_b=m_ssm_conv_b, dt_bias=m_dt_bias, a_log=m_a_log, d_skip=m_d_skip,
             conv_out_norm=m_conv_out_norm, ssm_out_norm=m_ssm_out_norm, w_out=m_w_out,
             norm_mix_post=m_norm_mix_post, norm_mlp_pre=m_norm_mlp_pre, w_up=m_w_up, w_down=m_w_down,
             norm_mlp_post=m_norm_mlp_post)
    V = dict(norm_mix_pre=v_norm_mix_pre, w_in=v_w_in, conv_a_w=v_conv_a_w, ssm_conv_w=v_ssm_conv_w,
             ssm_conv_b=v_ssm_conv_b, dt_bias=v_dt_bias, a_log=v_a_log, d_skip=v_d_skip,
             conv_out_norm=v_conv_out_norm, ssm_out_norm=v_ssm_out_norm, w_out=v_w_out,
             norm_mix_post=v_norm_mix_post, norm_mlp_pre=v_norm_mlp_pre, w_up=v_w_up, w_down=v_w_down,
             norm_mlp_post=v_norm_mlp_post)

    nseq, seq, d = x.shape
    t = nseq * seq
    depth = w_in.shape[0]
    h = d // HEAD_DIM
    xbc = d + 2 * SSM_GROUPS * D_STATE
    in_cols = w_in.shape[2] * N_DEV
    d_mix = w_out.shape[1] * N_DEV
    d_ff = w_up.shape[2] * N_DEV
    me = 4 * lax.axis_index("x") + 2 * lax.axis_index("y") + lax.axis_index("c")
    ca_shard = conv_a_w.shape[2]
    sc_shard = ssm_conv_w.shape[2]

    tap_shapes = [conv_a_w.shape[1:], ssm_conv_w.shape[1:]]

    def gather_start(i, after=()):
        st_in, tok_in = _exchange_start([w_in[i].astype(BF16), _pack([conv_a_w[i], ssm_conv_w[i]])], gather=True,
                                        name=f"gather_start_in_{i}", after=after)
        st_rest, tok_rest = _exchange_start([W[n][i].astype(BF16) for n in ("w_out", "w_up", "w_down")], gather=True,
                                            name=f"gather_start_rest_{i}", after=[tok_in])
        return st_in, st_rest, tok_rest

    vec = lambda name, i: W[name][i].reshape(1, -1)
    emat = (lax.broadcasted_iota(jnp.int32, (h, d), 1) // HEAD_DIM == lax.broadcasted_iota(jnp.int32, (h, d), 0)).astype(F32)

    xcur = x.reshape(t, d)
    hcur = _norm_fwd(xcur, vec("norm_mix_pre", 0), name="norm_first")
    saved = []
    nxt = gather_start(0)
    for i in range(depth):
        st_in, st_rest, tok = nxt
        win_g, taps_g = _exchange_wait(st_in, [hcur, tok], name=f"gather_wait_in_{i}")
        win = _assemble_cols(win_g, name=f"assemble_w_in_{i}")
        taps_j = [_unpack(taps_g[j], tap_shapes) for j in range(N_DEV)]
        conv_a_i = jnp.concatenate([tj[0] for tj in taps_j], axis=1)
        ssm_conv_i = jnp.concatenate([tj[1] for tj in taps_j], axis=1)
        proj = _mm(hcur, win, n=4 * d + xbc, name=f"fwd_proj_{i}", out_dtypes=(BF16,))
        pdt = _mm(hcur, win, n=LANES, b_off=4 * d + xbc, name=f"fwd_proj_dt_{i}")
        ya, va = _conva_fwd(proj, conv_a_i, vec("conv_out_norm", i), d=d, seq=seq, name=f"fwd_conv_a_{i}")
        cpre = _convb_fwd(proj, ssm_conv_i, vec("ssm_conv_b", i), col0=4 * d, seq=seq, name=f"fwd_conv_b_{i}")
        dsk_lane = jnp.repeat(W["d_skip"][i], HEAD_DIM).reshape(1, d)
        cat, y2, hprev = _ssd_fwd(cpre, pdt, proj, ya, vec("dt_bias", i), vec("a_log", i), dsk_lane,
                                  vec("ssm_out_norm", i), emat, nseq=nseq, seq=seq, name=f"fwd_ssd_{i}")
        wout_g, wup_g, wdown_g = _exchange_wait(st_rest, [cat], name=f"gather_wait_rest_{i}")
        lw = dict(win=win, wout=wout_g.reshape(d_mix, d),
                  wup=_assemble_cols(wup_g, name=f"assemble_w_up_{i}"), wdown=wdown_g.reshape(d_ff, d),
                  conv_a=conv_a_i, ssm_conv=ssm_conv_i)
        after = []
        if i + 1 < depth:
            nxt = gather_start(i + 1, after=[wout_g])
            after = [nxt[2]]
        x1, h2, mix = _mm(cat, lw["wout"], name=f"fwd_out_{i}", after=after, out_dtypes=(F32, BF16, BF16),
                          epi=_epi_resid_norm, extras=(xcur,), vecs=(vec("norm_mix_post", i), vec("norm_mlp_pre", i)),
                          tm_cap=FUSED_ROWS)
        f = _mm(h2, lw["wup"], name=f"fwd_up_{i}", out_dtypes=(BF16,), epi=_epi_relu2)
        g_next = vec("norm_mix_pre", i + 1) if i + 1 < depth else vec("norm_mix_pre", 0)
        dn = _mm(f, lw["wdown"], name=f"fwd_down_{i}")
        x2, hnext = _resid_norm(x1, dn, vec("norm_mlp_post", i), g_next, name=f"fwd_post_mlp_{i}")
        saved.append(dict(lw=lw, x0=xcur, h=hcur, proj=proj, pdt=pdt, va=va, cpre=cpre, y2=y2,
                          hprev=hprev, cat=cat, mix=mix, x1=x1, h2=h2, f=f, dn=dn, dsk_lane=dsk_lane))
        xcur, hcur = x2, hnext

    dx, loss_part = _loss_fwd_bwd(xcur, loss_target.reshape(t, d), name="loss")
    loss = lax.psum(loss_part[0, 0], ("x", "y", "c"))

    small_grads = {n: [None] * depth for n in SMALL}
    big_out = {n: None for n in BIG}

    def finish(pending, after):
        li, st_a, st_b = pending

        def update(n, parts):
            big_out[n] = _sum_adamw(parts, W[n], M[n], V[n], layer=li, outs=big_out[n], name=f"adamw_{n}_{li}")

        p_down, p_up = _exchange_wait(st_a, after, name=f"scatter_wait_a_{li}")
        update("w_down", p_down)
        update("w_up", p_up)
        p_out, p_in = _exchange_wait(st_b, after + [big_out["w_up"][0]], name=f"scatter_wait_b_{li}")
        update("w_out", p_out)
        update("w_in", p_in)

    pending = None
    for i in reversed(range(depth)):
        s = saved[i]
        lw = s["lw"]
        ddn, dg = _bwd_norm_out(s["dn"], vec("norm_mlp_post", i), dx, name=f"bwd_norm_mlp_post_{i}")
        small_grads["norm_mlp_post"][i] = dg
        dup = _mm(ddn, lw["wdown"], tb=True, name=f"bwd_down_dx_{i}", out_dtypes=(BF16,), epi=_epi_drelu2,
                  extras=(s["f"],))
        g_wdown = _mm(s["f"], ddn, ta=True, name=f"bwd_down_dw_{i}", out_dtypes=(BF16,))
        dh2 = _mm(dup, lw["wup"], tb=True, name=f"bwd_up_dx_{i}", out_dtypes=(BF16,))
        g_wup = _mm(s["h2"], dup, ta=True, name=f"bwd_up_dw_{i}", out_dtypes=(BF16,))
        st_a, tok_a = _exchange_start(
            [g_wdown.reshape(N_DEV, d_ff // N_DEV, d), _split_cols([g_wup], d_ff // N_DEV, name=f"split_g_w_up_{i}")],
            gather=False, name=f"scatter_start_a_{i}")
        dx1, dmix, dg_pre, dg_post = _bwd_norm_pair(s["x1"], [dh2], dx, s["mix"], vec("norm_mlp_pre", i) + tok_a[0:1, 0:1],
                                                    vec("norm_mix_post", i), name=f"bwd_norm_mix_post_{i}")
        small_grads["norm_mlp_pre"][i] = dg_pre
        small_grads["norm_mix_post"][i] = dg_post
        dcat = _mm(dmix, lw["wout"], tb=True, name=f"bwd_out_dx_{i}", out_dtypes=(BF16,))
        g_wout = _mm(s["cat"], dmix, ta=True, name=f"bwd_out_dw_{i}", out_dtypes=(BF16,))
        dproj, dcaw, dgca = _conva_bwd(dcat, s["proj"], s["va"], lw["conv_a"], vec("conv_out_norm", i), d=d, seq=seq,
                                       name=f"bwd_conv_a_{i}")
        small_grads["conv_a_w"][i] = dcaw
        small_grads["conv_out_norm"][i] = dgca
        dconv, dproj, dpdt, dgs, ddsk, ddtb, dalog = _ssd_bwd(
            s["cpre"], s["pdt"], s["proj"], s["y2"], s["hprev"], dcat, vec("dt_bias", i), vec("a_log", i),
            s["dsk_lane"], vec("ssm_out_norm", i), emat, dproj, nseq=nseq, seq=seq, name=f"bwd_ssd_{i}")
        small_grads["ssm_out_norm"][i] = dgs
        small_grads["d_skip"][i] = ddsk
        small_grads["dt_bias"][i] = ddtb
        small_grads["a_log"][i] = dalog
        dproj, dscw, dscb = _convb_bwd(dconv, s["proj"], lw["ssm_conv"], dproj, col0=4 * d, seq=seq,
                                       name=f"bwd_conv_b_{i}")
        small_grads["ssm_conv_w"][i] = dscw
        small_grads["ssm_conv_b"][i] = dscb
        g_win = _split_cols([
            _mm(s["h"], dproj, ta=True, name=f"bwd_proj_dw_{i}", out_dtypes=(BF16,)),
            _mm(s["h"], dpdt, ta=True, name=f"bwd_proj_dt_dw_{i}", out_dtypes=(BF16,))],
            in_cols // N_DEV, name=f"split_g_w_in_{i}")
        st_b, tok_b = _exchange_start(
            [g_wout.reshape(N_DEV, d_mix // N_DEV, d), g_win], gather=False, name=f"scatter_start_b_{i}")
        dh_parts = [_mm(dp, lw["win"], tb=True, b_koff=off, name=f"bwd_proj_{nm}dx_{i}", after=[tok_b], out_dtypes=(BF16,))
                    for nm, dp, off in (("", dproj, 0), ("dt_", dpdt, 4 * d + xbc))]
        dx, dg_in = _bwd_norm_in(s["x0"], dh_parts, dx1, vec("norm_mix_pre", i), name=f"bwd_norm_mix_pre_{i}")
        small_grads["norm_mix_pre"][i] = dg_in
        if pending is not None:
            finish(pending, [dx])
        pending = (i, st_a, st_b)

    grad_x = dx.reshape(nseq, seq, d)

    small_shapes_full = {n: (depth,) + tuple(small_grads[n][0].shape) for n in SMALL}
    gpack = _pack([jnp.stack(small_grads[n]) for n in SMALL])
    st_small, tok_small = _exchange_start([gpack], gather=True, name="allreduce_small_start")
    finish(pending, [dx, tok_small])
    gparts, = _exchange_wait(st_small, [big_out["w_in"][0]], name="allreduce_small_wait")

    def shard_of(n, full):
        if n == "conv_a_w":
            return lax.dynamic_slice_in_dim(full, me * ca_shard, ca_shard, axis=2)
        if n == "ssm_conv_w":
            return lax.dynamic_slice_in_dim(full, me * sc_shard, sc_shard, axis=2)
        return full.reshape(W[n].shape)

    gsum = _sum_parts(gparts, name="sum_small")
    gfull = _unpack(gsum, [small_shapes_full[n] for n in SMALL])
    gsmall = {n: shard_of(n, gf) for n, gf in zip(SMALL, gfull)}
    res = _sum_adamw(_pack([gsmall[n] for n in SMALL])[None], _pack([W[n] for n in SMALL]),
                     _pack([M[n] for n in SMALL]), _pack([V[n] for n in SMALL]), name="adamw_small")
    small_out = [dict(zip(SMALL, _unpack(r, [W[n].shape for n in SMALL]))) for r in res]

    def out_of(kind, n):
        return big_out[n][kind] if n in BIG else small_out[kind][n]

    return (loss, grad_x, *[out_of(k, n) for k in range(4) for n in ORDER])
```

```python
import functools

import jax
import jax.numpy as jnp
from jax import lax
from jax.experimental import pallas as pl
from jax.experimental.pallas import tpu as pltpu

F32 = jnp.float32
BF16 = jnp.bfloat16
HIGHEST = lax.Precision.HIGHEST
MESH = pl.DeviceIdType.MESH

EPS = 1e-6
HEAD_DIM = 64
D_STATE = 128
SSM_GROUPS = 2
CHUNK = 128
CONV_K = 3
SSM_CONV_K = 4
ADAM_LR = 0.001
ADAM_B1 = 0.9
ADAM_B2 = 0.999
ADAM_EPS = 1e-08
ADAM_WD = 0.01
ADAM_STEP = 10

N_DEV = 8
LANES = 128
SUBLANES = 8
VMEM_LIMIT = 48 * 1024 * 1024
ROW_TILE = 512
MM_TILE = 1024
MM_TILE_N = 1536
FUSED_ROWS = 512


def _params(sem):
    return pltpu.CompilerParams(dimension_semantics=sem, vmem_limit_bytes=VMEM_LIMIT)


def _call(body, **kw):
    return pl.pallas_call(body, **kw)


def _pick(n, cap):
    best = None
    for t in range(LANES, min(n, cap) + 1, LANES):
        if n % t == 0:
            best = t
    return best or n


def _pick_rows(n, cap):
    best = None
    for t in range(SUBLANES, min(n, cap) + 1, SUBLANES):
        if n % t == 0:
            best = t
    return best or n


def _sigmoid(x):
    return 1.0 / (1.0 + jnp.exp(-x))


def _softplus(x):
    return jnp.maximum(x, 0.0) + jnp.log1p(jnp.exp(-jnp.abs(x)))


def _rms(x):
    return lax.rsqrt(jnp.mean(x * x, axis=-1, keepdims=True) + EPS)


def _rms_bwd(x, r, g, dy):
    gy = dy * g
    dx = r * gy - x * (r * r * r) * jnp.mean(gy * x, axis=-1, keepdims=True)
    return dx, dy * x * r


def _full(shape):
    return pl.BlockSpec(shape, lambda *_: (0,) * len(shape))


def _mm(a, b, *, name, ta=False, tb=False, out_dtypes=(F32,), epi=None, extras=(), n=None, b_off=0, b_koff=0,
        after=(), vecs=(), n_sums=0, tm_cap=MM_TILE):
    m, k = (a.shape[1], a.shape[0]) if ta else a.shape
    if n is None:
        n = b.shape[0] if tb else b.shape[1]
    tm, tn, tk = _pick(m, tm_cap), _pick(n, MM_TILE_N), _pick(k, MM_TILE)
    while b_off % tn or n % tn:
        tn -= LANES
    if b_koff == 0 and k > MM_TILE:
        tk = _pick(k, MM_TILE_N)
    while b_koff % tk or k % tk:
        tk -= LANES
    nk = k // tk
    nm, nn = m // tm, n // tn
    jo = b_off // tn
    ko = b_koff // tk
    a_bytes = m * k * a.dtype.itemsize
    b_bytes = n * k * b.dtype.itemsize
    m_outer = a_bytes + nm * b_bytes <= b_bytes + nn * a_bytes
    ij = (lambda g0, g1: (g0, g1)) if m_outer else (lambda g0, g1: (g1, g0))
    grid = (nm, nn, nk) if m_outer else (nn, nm, nk)

    def a_map(g0, g1, kk):
        i, _ = ij(g0, g1)
        return (kk, i) if ta else (i, kk)

    def b_map(g0, g1, kk):
        _, j = ij(g0, g1)
        return (j + jo, kk + ko) if tb else (kk + ko, j + jo)

    def o_map(g0, g1, kk):
        return ij(g0, g1)

    a_spec = pl.BlockSpec((tk, tm) if ta else (tm, tk), a_map)
    b_spec = pl.BlockSpec((tn, tk) if tb else (tk, tn), b_map)
    o_spec = pl.BlockSpec((tm, tn), o_map)
    dims = (((0 if ta else 1,), (1 if tb else 0,)), ((), ()))
    n_ex = len(extras) + len(vecs)
    after = list(after)
    o0 = 2 + n_ex + len(after)
    n_out = len(out_dtypes)
    assert n_sums == 0 or nn == 1

    def finish(acc, ex, outs):
        res = (acc,) if epi is None else epi(acc, *[e[...] for e in ex])
        for o, r in zip(outs[:n_out], res[:n_out]):
            o[...] = r.astype(o.dtype)
        for o, r in zip(outs[n_out:], res[n_out:]):
            o[...] += jnp.sum(r, axis=0, keepdims=True)

    def zero_sums(outs, kk):
        if n_sums:
            @pl.when((pl.program_id(0) == 0) & (pl.program_id(1) == 0) & (kk == 0))
            def _():
                for o in outs[n_out:]:
                    o[...] = jnp.zeros_like(o)

    def body_single(*refs):
        a_ref, b_ref = refs[:2]
        zero_sums(refs[o0:], 0)
        acc = lax.dot_general(a_ref[...].astype(BF16), b_ref[...].astype(BF16), dims, preferred_element_type=F32)
        finish(acc, refs[2:2 + n_ex], refs[o0:])

    def body_multi(*refs):
        a_ref, b_ref = refs[:2]
        acc = refs[-1]
        kk = pl.program_id(2)
        zero_sums(refs[o0:-1], kk)

        @pl.when(kk == 0)
        def _():
            acc[...] = jnp.zeros_like(acc)

        acc[...] += lax.dot_general(a_ref[...].astype(BF16), b_ref[...].astype(BF16), dims, preferred_element_type=F32)

        @pl.when(kk == nk - 1)
        def _():
            finish(acc[...], refs[2:2 + n_ex], refs[o0:-1])

    v_spec = pl.BlockSpec((1, tn), lambda g0, g1, kk: (0, ij(g0, g1)[1]))
    outs = _call(
        body_single if nk == 1 else body_multi, name=name, grid=grid,
        in_specs=([a_spec, b_spec] + [o_spec] * len(extras) + [v_spec] * len(vecs)
                  + [pl.BlockSpec(memory_space=pl.ANY)] * len(after)),
        out_specs=[o_spec] * n_out + [v_spec] * n_sums,
        out_shape=[jax.ShapeDtypeStruct((m, n), dt) for dt in out_dtypes] + [jax.ShapeDtypeStruct((1, n), F32)] * n_sums,
        scratch_shapes=[] if nk == 1 else [pltpu.VMEM((tm, tn), F32)],
        compiler_params=_params(("parallel", "parallel", "arbitrary") if n_sums == 0 else ("arbitrary",) * 3),
    )(a, b, *extras, *vecs, *after)
    return outs[0] if len(outs) == 1 else outs


def _epi_resid_norm(acc, x, g_res, g_next):
    xn = x + acc * _rms(acc) * g_res
    return xn, xn * _rms(xn) * g_next, acc


def _epi_bwd_norm_pair(acc, x, dres, n, g_in, g_out):
    dxh, dgi = _rms_bwd(x, _rms(x), g_in, acc)
    dx = dres + dxh
    nv = n.astype(F32)
    dn, dgo = _rms_bwd(nv, _rms(nv), g_out, dx)
    return dx, dn, dgi, dgo


def _epi_bwd_norm_in(acc, x, dres, dh_more, g_in):
    dxh, dgi = _rms_bwd(x, _rms(x), g_in, acc + dh_more.astype(F32))
    return dres + dxh, dgi


def _epi_relu2(acc):
    r = jnp.maximum(acc, 0.0)
    return (r * r,)


def _epi_drelu2(acc, f):
    return (acc * (2.0 * jnp.sqrt(f.astype(F32))),)


def _norm_fwd(x, g, *, name):
    t, d = x.shape
    tt = _pick_rows(t, ROW_TILE)

    def body(x_ref, g_ref, h_ref):
        xv = x_ref[...]
        h_ref[...] = (xv * _rms(xv) * g_ref[...]).astype(BF16)

    row = pl.BlockSpec((tt, d), lambda i: (i, 0))
    return _call(body, name=name, grid=(t // tt,), in_specs=[row, _full((1, d))], out_specs=row,
                 out_shape=jax.ShapeDtypeStruct((t, d), BF16), compiler_params=_params(("parallel",)))(x, g)


def _resid_norm(x, n, g1, g2, *, name):
    t, d = x.shape
    tt = _pick_rows(t, ROW_TILE)

    def body(x_ref, n_ref, g1_ref, g2_ref, xo_ref, h_ref):
        nv = n_ref[...].astype(F32)
        xn = x_ref[...] + nv * _rms(nv) * g1_ref[...]
        xo_ref[...] = xn
        h_ref[...] = (xn * _rms(xn) * g2_ref[...]).astype(BF16)

    row = pl.BlockSpec((tt, d), lambda i: (i, 0))
    return _call(body, name=name, grid=(t // tt,), in_specs=[row, row, _full((1, d)), _full((1, d))],
                 out_specs=[row, row],
                 out_shape=[jax.ShapeDtypeStruct((t, d), F32), jax.ShapeDtypeStruct((t, d), BF16)],
                 compiler_params=_params(("parallel",)))(x, n, g1, g2)


def _loss_fwd_bwd(xf, target, *, name):
    t, d = xf.shape
    tt = _pick_rows(t, ROW_TILE)
    nt = t // tt

    def body(x_ref, t_ref, dy_ref, loss_ref, acc):
        i = pl.program_id(0)

        @pl.when(i == 0)
        def _():
            acc[...] = jnp.zeros_like(acc)

        e = x_ref[...] - t_ref[...]
        dy_ref[...] = e * (1.0 / d)
        acc[...] += jnp.sum(e * e, axis=0, keepdims=True)

        @pl.when(i == nt - 1)
        def _():
            loss_ref[...] = jnp.sum(acc[...], axis=-1, keepdims=True) * (0.5 / d)

    row = pl.BlockSpec((tt, d), lambda i: (i, 0))
    return _call(body, name=name, grid=(nt,), in_specs=[row, row], out_specs=[row, _full((1, 1))],
                 out_shape=[jax.ShapeDtypeStruct((t, d), F32), jax.ShapeDtypeStruct((1, 1), F32)],
                 scratch_shapes=[pltpu.VMEM((1, d), F32)], compiler_params=_params(("arbitrary",)))(xf, target)


def _bwd_norm_pair(xin, dh, dres, n, g_in, g_out, *, name):
    t, d = xin.shape
    tt = _pick_rows(t, ROW_TILE)
    n_dh = len(dh)

    def body(*refs):
        x_ref = refs[0]
        dh_refs = refs[1:1 + n_dh]
        dres_ref, n_ref, gi_ref, go_ref, dx_ref, dn_ref, dgi_ref, dgo_ref = refs[1 + n_dh:]
        i = pl.program_id(0)

        @pl.when(i == 0)
        def _():
            dgi_ref[...] = jnp.zeros_like(dgi_ref)
            dgo_ref[...] = jnp.zeros_like(dgo_ref)

        xv = x_ref[...]
        dhv = dh_refs[0][...].astype(F32)
        for r in dh_refs[1:]:
            dhv = dhv + r[...].astype(F32)
        dxh, dgi = _rms_bwd(xv, _rms(xv), gi_ref[...], dhv)
        dx = dres_ref[...] + dxh
        dx_ref[...] = dx
        dgi_ref[...] += jnp.sum(dgi, axis=0, keepdims=True)
        nv = n_ref[...].astype(F32)
        dn, dgo = _rms_bwd(nv, _rms(nv), go_ref[...], dx)
        dn_ref[...] = dn.astype(BF16)
        dgo_ref[...] += jnp.sum(dgo, axis=0, keepdims=True)

    row = pl.BlockSpec((tt, d), lambda i: (i, 0))
    vec = _full((1, d))
    return _call(body, name=name, grid=(t // tt,), in_specs=[row] * (n_dh + 3) + [vec, vec],
                 out_specs=[row, row, vec, vec],
                 out_shape=[jax.ShapeDtypeStruct((t, d), F32), jax.ShapeDtypeStruct((t, d), BF16),
                            jax.ShapeDtypeStruct((1, d), F32), jax.ShapeDtypeStruct((1, d), F32)],
                 compiler_params=_params(("arbitrary",)))(xin, *dh, dres, n, g_in, g_out)


def _bwd_norm_in(xin, dh, dres, g_in, *, name):
    t, d = xin.shape
    tt = _pick_rows(t, ROW_TILE)
    n_dh = len(dh)

    def body(*refs):
        x_ref = refs[0]
        dh_refs = refs[1:1 + n_dh]
        dres_ref, gi_ref, dx_ref, dgi_ref = refs[1 + n_dh:]
        i = pl.program_id(0)

        @pl.when(i == 0)
        def _():
            dgi_ref[...] = jnp.zeros_like(dgi_ref)

        xv = x_ref[...]
        dhv = dh_refs[0][...].astype(F32)
        for r in dh_refs[1:]:
            dhv = dhv + r[...].astype(F32)
        dxh, dgi = _rms_bwd(xv, _rms(xv), gi_ref[...], dhv)
        dx_ref[...] = dres_ref[...] + dxh
        dgi_ref[...] += jnp.sum(dgi, axis=0, keepdims=True)

    row = pl.BlockSpec((tt, d), lambda i: (i, 0))
    vec = _full((1, d))
    return _call(body, name=name, grid=(t // tt,), in_specs=[row] * (n_dh + 2) + [vec],
                 out_specs=[row, vec],
                 out_shape=[jax.ShapeDtypeStruct((t, d), F32), jax.ShapeDtypeStruct((1, d), F32)],
                 compiler_params=_params(("arbitrary",)))(xin, *dh, dres, g_in)


def _bwd_norm_out(n, g_out, dx, *, name):
    t, d = n.shape
    tt = _pick_rows(t, ROW_TILE)

    def body(n_ref, go_ref, dx_ref, dn_ref, dgo_ref):
        i = pl.program_id(0)

        @pl.when(i == 0)
        def _():
            dgo_ref[...] = jnp.zeros_like(dgo_ref)

        nv = n_ref[...].astype(F32)
        dn, dgo = _rms_bwd(nv, _rms(nv), go_ref[...], dx_ref[...])
        dn_ref[...] = dn.astype(BF16)
        dgo_ref[...] += jnp.sum(dgo, axis=0, keepdims=True)

    row = pl.BlockSpec((tt, d), lambda i: (i, 0))
    vec = _full((1, d))
    return _call(body, name=name, grid=(t // tt,), in_specs=[row, vec, row], out_specs=[row, vec],
                 out_shape=[jax.ShapeDtypeStruct((t, d), BF16), jax.ShapeDtypeStruct((1, d), F32)],
                 compiler_params=_params(("arbitrary",)))(n, g_out, dx)


def _shift_down(cur, halo, s):
    return jnp.concatenate([halo[SUBLANES - s:], cur[:cur.shape[0] - s]], axis=0)


def _shift_up(cur, halo, s):
    return jnp.concatenate([cur[s:], halo[:s]], axis=0)


def _conva_fwd(pa, w, g, *, d, seq, name):
    t = pa.shape[0]
    tt = _pick_rows(seq, ROW_TILE)
    tps = seq // tt

    def body(xa_ref, ca_ref, ba_ref, w_ref, g_ref, ya_ref, v_ref, carry):
        i = pl.program_id(0)

        @pl.when(i % tps == 0)
        def _():
            carry[...] = jnp.zeros_like(carry)

        u = ca_ref[...].astype(F32) * xa_ref[...].astype(F32)
        halo = carry[...]
        wv = w_ref[...]
        v = wv[2:3] * u + wv[1:2] * _shift_down(u, halo, 1) + wv[0:1] * _shift_down(u, halo, 2)
        carry[...] = u[tt - SUBLANES:]
        yp = ba_ref[...].astype(F32) * v
        ya_ref[...] = (yp * _rms(yp) * g_ref[...]).astype(BF16)
        v_ref[...] = v.astype(BF16)

    col = lambda c: pl.BlockSpec((tt, d), lambda i, c=c: (i, c))
    row = pl.BlockSpec((tt, d), lambda i: (i, 0))
    return _call(body, name=name, grid=(t // tt,),
                 in_specs=[col(0), col(1), col(2), _full((CONV_K, d)), _full((1, d))], out_specs=[row, row],
                 out_shape=[jax.ShapeDtypeStruct((t, d), BF16), jax.ShapeDtypeStruct((t, d), BF16)],
                 scratch_shapes=[pltpu.VMEM((SUBLANES, d), F32)],
                 compiler_params=_params(("arbitrary",)))(pa, pa, pa, w, g)


def _conva_bwd(dcat, pa, v, w, g, *, d, seq, name):
    t, width = pa.shape
    d3 = 3 * d
    tt = _pick_rows(seq, ROW_TILE)
    tps = seq // tt
    nt = t // tt

    def body(dya_ref, xa_ref, ca_ref, ba_ref, v_ref, w_ref, g_ref, dpa_ref, dw_ref, dg_ref, carry):
        i = pl.program_id(0)

        @pl.when(i == 0)
        def _():
            dw_ref[...] = jnp.zeros_like(dw_ref)
            dg_ref[...] = jnp.zeros_like(dg_ref)

        @pl.when(i % tps == 0)
        def _():
            carry[...] = jnp.zeros_like(carry)

        xa, ca, ba, vv = [r[...].astype(F32) for r in (xa_ref, ca_ref, ba_ref, v_ref)]
        yp = ba * vv
        dyp, dgt = _rms_bwd(yp, _rms(yp), g_ref[...], dya_ref[...].astype(F32))
        dg_ref[...] += jnp.sum(dgt, axis=0, keepdims=True)
        dv = dyp * ba
        halo = carry[...]
        dv1 = _shift_up(dv, halo, 1)
        dv2 = _shift_up(dv, halo, 2)
        carry[...] = dv[:SUBLANES]
        wv = w_ref[...]
        du = wv[2:3] * dv + wv[1:2] * dv1 + wv[0:1] * dv2
        u = ca * xa
        dw_ref[0:1, :] += jnp.sum(u * dv2, axis=0, keepdims=True)
        dw_ref[1:2, :] += jnp.sum(u * dv1, axis=0, keepdims=True)
        dw_ref[2:3, :] += jnp.sum(u * dv, axis=0, keepdims=True)
        dpa_ref[:, 0:d] = (du * ca).astype(BF16)
        dpa_ref[:, d:2 * d] = (du * xa).astype(BF16)
        dpa_ref[:, 2 * d:3 * d] = (dyp * vv).astype(BF16)

    rcol = lambda c: pl.BlockSpec((tt, d), lambda i, c=c: (nt - 1 - i, c))
    return _call(body, name=name, grid=(nt,),
                 in_specs=[rcol(0), rcol(0), rcol(1), rcol(2), rcol(0), _full((CONV_K, d)), _full((1, d))],
                 out_specs=[pl.BlockSpec((tt, d3), lambda i: (nt - 1 - i, 0)), _full((CONV_K, d)), _full((1, d))],
                 out_shape=[jax.ShapeDtypeStruct((t, width), BF16), jax.ShapeDtypeStruct((CONV_K, d), F32),
                            jax.ShapeDtypeStruct((1, d), F32)],
                 scratch_shapes=[pltpu.VMEM((SUBLANES, d), F32)],
                 compiler_params=_params(("arbitrary",)))(dcat, pa, pa, pa, v, w, g)


CONV_CH = 512


def _convb_fwd(proj, w, bias, *, col0, seq, name):
    t = proj.shape[0]
    c = w.shape[1]
    cb = _pick(c, CONV_CH)
    assert col0 % cb == 0
    tt = _pick_rows(seq, 2 * ROW_TILE)
    tps = seq // tt

    def body(p_ref, w_ref, b_ref, o_ref, carry):
        i = pl.program_id(1)

        @pl.when(i % tps == 0)
        def _():
            carry[...] = jnp.zeros_like(carry)

        p = p_ref[...].astype(F32)
        halo = carry[...]
        wv = w_ref[...]
        o = wv[3:4] * p + b_ref[...]
        for s in (1, 2, 3):
            o = o + wv[3 - s:4 - s] * _shift_down(p, halo, s)
        carry[...] = p[tt - SUBLANES:]
        o_ref[...] = o.astype(BF16)

    return _call(body, name=name, grid=(c // cb, t // tt),
                 in_specs=[pl.BlockSpec((tt, cb), lambda jc, i: (i, col0 // cb + jc)),
                           pl.BlockSpec((SSM_CONV_K, cb), lambda jc, i: (0, jc)), pl.BlockSpec((1, cb), lambda jc, i: (0, jc))],
                 out_specs=pl.BlockSpec((tt, cb), lambda jc, i: (i, jc)), out_shape=jax.ShapeDtypeStruct((t, c), BF16),
                 scratch_shapes=[pltpu.VMEM((SUBLANES, cb), F32)],
                 compiler_params=_params(("arbitrary", "arbitrary")))(proj, w, bias)


def _convb_bwd(dconv, proj, w, dproj, *, col0, seq, name):
    t, c = dconv.shape
    cb = _pick(c, CONV_CH)
    assert col0 % cb == 0
    tt = _pick_rows(seq, 2 * ROW_TILE)
    tps = seq // tt
    nt = t // tt

    def body(dc_ref, p_ref, w_ref, dproj_in, dp_ref, dw_ref, db_ref, carry):
        del dproj_in
        i = pl.program_id(1)

        @pl.when(i == 0)
        def _():
            dw_ref[...] = jnp.zeros_like(dw_ref)
            db_ref[...] = jnp.zeros_like(db_ref)

        @pl.when(i % tps == 0)
        def _():
            carry[...] = jnp.zeros_like(carry)

        dc = dc_ref[...].astype(F32)
        p = p_ref[...].astype(F32)
        halo = carry[...]
        wv = w_ref[...]
        dp = wv[3:4] * dc
        dw_ref[3:4, :] += jnp.sum(p * dc, axis=0, keepdims=True)
        for s in (1, 2, 3):
            dcs = _shift_up(dc, halo, s)
            dp = dp + wv[3 - s:4 - s] * dcs
            dw_ref[3 - s:4 - s, :] += jnp.sum(p * dcs, axis=0, keepdims=True)
        carry[...] = dc[:SUBLANES]
        db_ref[...] += jnp.sum(dc, axis=0, keepdims=True)
        dp_ref[...] = dp.astype(BF16)

    win_spec = pl.BlockSpec((tt, cb), lambda jc, i: (nt - 1 - i, col0 // cb + jc))
    taps = pl.BlockSpec((SSM_CONV_K, cb), lambda jc, i: (0, jc))
    return _call(body, name=name, grid=(c // cb, nt),
                 in_specs=[pl.BlockSpec((tt, cb), lambda jc, i: (nt - 1 - i, jc)), win_spec, taps,
                           pl.BlockSpec(memory_space=pl.ANY)],
                 out_specs=[win_spec, taps, pl.BlockSpec((1, cb), lambda jc, i: (0, jc))],
                 out_shape=[jax.ShapeDtypeStruct(dproj.shape, BF16), jax.ShapeDtypeStruct((SSM_CONV_K, c), F32),
                            jax.ShapeDtypeStruct((1, c), F32)],
                 input_output_aliases={3: 0},
                 scratch_shapes=[pltpu.VMEM((SUBLANES, cb), F32)],
                 compiler_params=_params(("arbitrary", "arbitrary")))(dconv, proj, w, dproj)


def _expand_heads(x, ev):
    return jnp.dot(x, ev, precision=HIGHEST, preferred_element_type=F32)


def _head_sums(v, ev):
    return lax.dot_general(v, ev, (((1,), (1,)), ((), ())), precision=HIGHEST, preferred_element_type=F32)


def _ssd_common(c_ref, pdt_ref, dtb_ref, alog_ref, e_ref, h):
    cp = c_ref[...].astype(F32)
    sg = _sigmoid(cp)
    act = cp * sg
    pre = pdt_ref[:, 0:h] + dtb_ref[...]
    dt = _softplus(pre)
    a = -jnp.exp(alog_ref[...])
    adt = dt * a
    row = lax.broadcasted_iota(jnp.int32, (CHUNK, CHUNK), 0)
    col = lax.broadcasted_iota(jnp.int32, (CHUNK, CHUNK), 1)
    tril = row >= col
    cs = jnp.dot(tril.astype(F32), adt, precision=HIGHEST, preferred_element_type=F32)
    cs_t = lax.dot_general(adt, (col >= row).astype(F32), (((0,), (0,)), ((), ())), precision=HIGHEST,
                           preferred_element_type=F32)
    ev = e_ref[...]
    dt_l = _expand_heads(dt, ev)
    ecs_l = jnp.exp(_expand_heads(cs, ev))
    return dict(cp=cp, sg=sg, act=act, pre=pre, dt=dt, a=a, cs=cs, cs_t=cs_t, dt_l=dt_l, ecs_l=ecs_l,
                tril=tril, row=row, col=col, lo=col < HEAD_DIM)


def _dot_nt(a, b):
    return lax.dot_general(a, b, (((1,), (1,)), ((), ())), preferred_element_type=F32)


def _dot_tn(a, b):
    return lax.dot_general(a, b, (((0,), (0,)), ((), ())), preferred_element_type=F32)


def _dot(a, b):
    return jnp.dot(a, b, preferred_element_type=F32)


def _ssd_fwd(cpre, pdt, pz, ya, dtb, alog, dsk_lane, gs, emat, *, nseq, seq, name):
    t, xbc = cpre.shape
    d = ya.shape[1]
    h = d // HEAD_DIM
    npair = h // 2
    ppg = npair // SSM_GROUPS
    nc = seq // CHUNK
    gw = d // SSM_GROUPS
    bc0 = d
    cc0 = d + SSM_GROUPS * D_STATE

    def body(c_ref, pdt_ref, z_ref, ya_ref, dtb_ref, alog_ref, dsk_ref, gs_ref, e_ref, cat_ref, y2_ref, hp_ref, h_ref):
        @pl.when(pl.program_id(0) == 0)
        def _():
            h_ref[...] = jnp.zeros_like(h_ref)

        for sq in range(nseq):
            one_seq(c_ref.at[sq], pdt_ref.at[sq], z_ref.at[sq], ya_ref.at[sq], dtb_ref, alog_ref, dsk_ref, gs_ref, e_ref,
                    cat_ref.at[sq], y2_ref.at[sq], hp_ref.at[sq], h_ref.at[sq])

    def one_seq(c_ref, pdt_ref, z_ref, ya_ref, dtb_ref, alog_ref, dsk_ref, gs_ref, e_ref, cat_ref, y2_ref, hp_ref, h_ref):
        q = _ssd_common(c_ref, pdt_ref, dtb_ref, alog_ref, e_ref, h)
        act, cs, lo, ecs_l = q["act"], q["cs"], q["lo"], q["ecs_l"]
        xs = act[:, :d]
        xd = xs * q["dt_l"]
        ys = []
        for g in range(SSM_GROUPS):
            bg = act[:, bc0 + g * D_STATE: bc0 + (g + 1) * D_STATE]
            cgb = act[:, cc0 + g * D_STATE: cc0 + (g + 1) * D_STATE].astype(BF16)
            s = _dot_nt(cgb, bg.astype(BF16))
            bg_t = bg.T
            for jj in range(ppg):
                j = g * ppg + jj
                sl = slice(LANES * j, LANES * (j + 1))
                xdj = xd[:, sl]
                x2 = jnp.concatenate([jnp.where(lo, xdj, 0.0), jnp.where(lo, 0.0, xdj)], axis=0).astype(BF16)
                hprev = h_ref[j]
                hp_ref[j] = hprev.astype(BF16)
                ms, bws_t = [], []
                for hh in (2 * j, 2 * j + 1):
                    csc = cs[:, hh:hh + 1]
                    cs_row = q["cs_t"][hh:hh + 1, :]
                    seg = jnp.broadcast_to(csc, (CHUNK, CHUNK)) - jnp.broadcast_to(cs_row, (CHUNK, CHUNK))
                    ms.append(s * jnp.exp(jnp.where(q["tril"], seg, -jnp.inf)))
                    bws_t.append(bg_t * jnp.exp(cs_row[:, CHUNK - 1:CHUNK] - cs_row))
                ydiag = _dot(jnp.concatenate(ms, axis=1).astype(BF16), x2)
                st = _dot(jnp.concatenate(bws_t, axis=1).astype(BF16), x2)
                ecs = ecs_l[:, sl]
                yoff = _dot(cgb, hprev.astype(BF16)) * ecs
                h_ref[j] = hprev * ecs[CHUNK - 1:CHUNK] + st
                ys.append(ydiag + yoff)
        y = jnp.concatenate(ys, axis=1) + dsk_ref[...] * xs
        y2_ref[...] = y.astype(BF16)
        zv = z_ref[...].astype(F32)
        y3 = y * (zv * _sigmoid(zv))
        cat_ref[:, 0:d] = ya_ref[...]
        for gi in range(SSM_GROUPS):
            seg = y3[:, gi * gw:(gi + 1) * gw]
            cat_ref[:, d + gi * gw:d + (gi + 1) * gw] = (seg * _rms(seg) * gs_ref[:, gi * gw:(gi + 1) * gw]).astype(BF16)

    chunk = lambda w, cb=0: pl.BlockSpec((nseq, CHUNK, w), lambda c, cb=cb: (0, c, cb))
    vec = lambda w: pl.BlockSpec((1, w), lambda c: (0, 0))
    hp_spec = pl.BlockSpec((nseq, None, npair, D_STATE, LANES), lambda c: (0, c, 0, 0, 0))
    per_seq = lambda a: a.reshape(nseq, seq, a.shape[1])
    cat, y2, hp = _call(
        body, name=name, grid=(nc,),
        in_specs=[chunk(xbc), chunk(LANES), chunk(d, 3), chunk(d), vec(h), vec(h), vec(d), vec(d),
                  pl.BlockSpec((h, d), lambda c: (0, 0))],
        out_specs=[chunk(2 * d), chunk(d), hp_spec],
        out_shape=[jax.ShapeDtypeStruct((nseq, seq, 2 * d), BF16), jax.ShapeDtypeStruct((nseq, seq, d), BF16),
                   jax.ShapeDtypeStruct((nseq, nc, npair, D_STATE, LANES), BF16)],
        scratch_shapes=[pltpu.VMEM((nseq, npair, D_STATE, LANES), F32)],
        compiler_params=_params(("arbitrary",)))(
            per_seq(cpre), per_seq(pdt), per_seq(pz), per_seq(ya), dtb, alog, dsk_lane, gs, emat)
    return cat.reshape(t, 2 * d), y2.reshape(t, d), hp


def _ssd_bwd(cpre, pdt, pz, y2, hprev_all, dcat, dtb, alog, dsk_lane, gs, emat, dproj, *, nseq, seq, name):
    t, xbc = cpre.shape
    d = y2.shape[1]
    h = d // HEAD_DIM
    npair = h // 2
    ppg = npair // SSM_GROUPS
    nc = seq // CHUNK
    gw = d // SSM_GROUPS
    bc0 = d
    cc0 = d + SSM_GROUPS * D_STATE

    def body(c_ref, pdt_ref, z_ref, y2_ref, hp_ref, dys_ref, dtb_ref, alog_ref, dsk_ref, gs_ref, e_ref, dproj_in,
             dconv_ref, dz_ref, dpdt_ref, dgs_ref, ddsk_ref, ddtb_ref, dalog_ref, dh_ref):
        del dproj_in
        b = pl.program_id(0)
        c = pl.program_id(1)

        @pl.when(c == 0)
        def _():
            dh_ref[...] = jnp.zeros_like(dh_ref)

        @pl.when((b == 0) & (c == 0))
        def _():
            dgs_ref[...] = jnp.zeros_like(dgs_ref)
            ddsk_ref[...] = jnp.zeros_like(ddsk_ref)
            ddtb_ref[...] = jnp.zeros_like(ddtb_ref)
            dalog_ref[...] = jnp.zeros_like(dalog_ref)

        q = _ssd_common(c_ref, pdt_ref, dtb_ref, alog_ref, e_ref, h)
        cp, sg, act, cs, a, dt, lo = q["cp"], q["sg"], q["act"], q["cs"], q["a"], q["dt"], q["lo"]
        ecs_l, dt_l = q["ecs_l"], q["dt_l"]
        ev = e_ref[...]
        xs = act[:, :d]
        xd = xs * dt_l
        row16 = lax.broadcasted_iota(jnp.int32, (CHUNK, h), 0)
        hid = lax.broadcasted_iota(jnp.int32, (1, h), 1)
        hid_t = lax.broadcasted_iota(jnp.int32, (h, 1), 0)

        zv = z_ref[...].astype(F32)
        sz = _sigmoid(zv)
        siluz = zv * sz
        y2v = y2_ref[...].astype(F32)
        y3 = y2v * siluz
        dysv = dys_ref[...].astype(F32)
        dy3s = []
        for gi in range(SSM_GROUPS):
            gsl = slice(gi * gw, (gi + 1) * gw)
            seg = y3[:, gsl]
            dseg, dgt = _rms_bwd(seg, _rms(seg), gs_ref[:, gsl], dysv[:, gsl])
            dy3s.append(dseg)
            dgs_ref[:, gsl] += jnp.sum(dgt, axis=0, keepdims=True)
        dy3 = jnp.concatenate(dy3s, axis=1)
        dy = dy3 * siluz
        dz_ref[...] = (dy3 * y2v * (sz * (1.0 + zv * (1.0 - sz)))).astype(BF16)
        ddsk_ref[...] += jnp.sum(_head_sums(dy * xs, ev), axis=0, keepdims=True)

        dcs = jnp.zeros((CHUNK, h), F32)
        dcs_t = jnp.zeros((h, CHUNK), F32)
        dxd_parts, yoff_parts, db_parts, dc_parts = [], [], [], []
        for g in range(SSM_GROUPS):
            bg = act[:, bc0 + g * D_STATE: bc0 + (g + 1) * D_STATE]
            cg = act[:, cc0 + g * D_STATE: cc0 + (g + 1) * D_STATE]
            bgb, cgb = bg.astype(BF16), cg.astype(BF16)
            cgb_t = cg.T.astype(BF16)
            s = _dot_nt(cgb, bgb)
            ds = jnp.zeros((CHUNK, CHUNK), F32)
            dbg = jnp.zeros((CHUNK, D_STATE), F32)
            dcg = jnp.zeros((CHUNK, D_STATE), F32)
            for jj in range(ppg):
                j = g * ppg + jj
                sl = slice(LANES * j, LANES * (j + 1))
                xdj = xd[:, sl]
                xdb = xdj.astype(BF16)
                x2 = jnp.concatenate([jnp.where(lo, xdj, 0.0), jnp.where(lo, 0.0, xdj)], axis=0).astype(BF16)
                dyj = dy[:, sl]
                dy2 = jnp.concatenate([jnp.where(lo, dyj, 0.0), jnp.where(lo, 0.0, dyj)], axis=0).astype(BF16)
                hpb = hp_ref[j]
                hprev = hpb.astype(F32)
                dhn = dh_ref[j]
                dhb = dhn.astype(BF16)
                dh2 = jnp.concatenate([jnp.where(lo, dhn, 0.0), jnp.where(lo, 0.0, dhn)], axis=0).astype(BF16)
                ecs = ecs_l[:, sl]
                gmat = (dyj * ecs).astype(BF16)
                yoff_parts.append(_dot(cgb, hpb) * ecs)
                dcg = dcg + _dot_nt(gmat, hpb)
                dh_ref[j] = dhn * ecs[CHUNK - 1:CHUNK] + _dot(cgb_t, gmat)
                t2 = dhn * hprev
                dbw2 = _dot_nt(x2, dhb)
                dm2 = _dot_nt(dy2, xdb)
                ms, bws = [], []
                for idx, hh in enumerate((2 * j, 2 * j + 1)):
                    msk = lo if idx == 0 else jnp.logical_not(lo)
                    onehot = (hid == hh).astype(F32)
                    csc = cs[:, hh:hh + 1]
                    seg = jnp.broadcast_to(csc, (CHUNK, CHUNK)) - jnp.broadcast_to(q["cs_t"][hh:hh + 1, :], (CHUNK, CHUNK))
                    lm = jnp.exp(jnp.where(q["tril"], seg, -jnp.inf))
                    m = s * lm
                    cs_last = cs[CHUNK - 1:CHUNK, hh:hh + 1]
                    dte = jnp.exp(cs_last - csc)
                    ms.append(m)
                    bws.append(bg * dte)
                    dbw = dbw2[idx * CHUNK:(idx + 1) * CHUNK]
                    dbg = dbg + dbw * dte
                    qv = jnp.sum(dbw * bg, axis=-1, keepdims=True) * dte
                    dm = dm2[idx * CHUNK:(idx + 1) * CHUNK]
                    wm = dm * m
                    rc = jnp.sum(wm, axis=-1, keepdims=True)
                    dcs_t = dcs_t - (hid_t == hh).astype(F32) * jnp.sum(wm, axis=0, keepdims=True)
                    ds = ds + dm * lm
                    ddec = jnp.sum(jnp.where(msk, t2, 0.0)) * jnp.exp(cs_last)
                    last = jnp.sum(qv) + ddec
                    dcs = dcs + (rc - qv) * onehot + jnp.where(row16 == CHUNK - 1, last * onehot, 0.0)
                dxd_s = _dot(jnp.concatenate(bws, axis=1).astype(BF16), dh2)
                dxd_d = _dot_tn(jnp.concatenate(ms, axis=0).astype(BF16), dy2)
                dxd_parts.append(dxd_s + dxd_d)
            dsb = ds.astype(BF16)
            dc_parts.append(dcg + _dot(dsb, bgb))
            db_parts.append(dbg + _dot_tn(dsb, cgb))
        yoff_all = jnp.concatenate(yoff_parts, axis=1)
        dxd_all = jnp.concatenate(dxd_parts, axis=1)
        dcs = dcs + _head_sums(dy * yoff_all, ev)
        triu = (q["col"] >= q["row"]).astype(F32)
        dadt = (jnp.dot(triu, dcs, precision=HIGHEST, preferred_element_type=F32)
                + lax.dot_general(triu, dcs_t, (((1,), (1,)), ((), ())), precision=HIGHEST, preferred_element_type=F32))
        ddt = dadt * a + _head_sums(dxd_all * xs, ev)
        dalog_ref[...] += jnp.sum(dadt * dt, axis=0, keepdims=True) * a
        dpre = ddt * _sigmoid(q["pre"])
        ddtb_ref[...] += jnp.sum(dpre, axis=0, keepdims=True)
        dpdt_ref[...] = jnp.zeros_like(dpdt_ref)
        dpdt_ref[:, 0:h] = dpre.astype(BF16)
        dxs = dxd_all * dt_l + dy * dsk_ref[...]
        dact = jnp.concatenate([dxs] + db_parts + dc_parts, axis=1)
        dconv_ref[...] = (dact * (sg * (1.0 + cp * (1.0 - sg)))).astype(BF16)

    rchunk = lambda w, cb=0: pl.BlockSpec((CHUNK, w), lambda b, c, cb=cb: (b * nc + nc - 1 - c, cb))
    vec = lambda w: pl.BlockSpec((1, w), lambda b, c: (0, 0))
    hp_spec = pl.BlockSpec((None, None, npair, D_STATE, LANES), lambda b, c: (b, nc - 1 - c, 0, 0, 0))
    return _call(body, name=name, grid=(nseq, nc),
                 in_specs=[rchunk(xbc), rchunk(LANES), rchunk(d, 3), rchunk(d), hp_spec, rchunk(d, 1),
                           vec(h), vec(h), vec(d), vec(d), pl.BlockSpec((h, d), lambda b, c: (0, 0)),
                           pl.BlockSpec(memory_space=pl.ANY)],
                 out_specs=[rchunk(xbc), rchunk(d, 3), rchunk(LANES), vec(d), vec(h), vec(h), vec(h)],
                 out_shape=[jax.ShapeDtypeStruct((t, xbc), BF16), jax.ShapeDtypeStruct(dproj.shape, BF16),
                            jax.ShapeDtypeStruct((t, LANES), BF16), jax.ShapeDtypeStruct((1, d), F32),
                            jax.ShapeDtypeStruct((1, h), F32), jax.ShapeDtypeStruct((1, h), F32),
                            jax.ShapeDtypeStruct((1, h), F32)],
                 input_output_aliases={11: 1},
                 scratch_shapes=[pltpu.VMEM((npair, D_STATE, LANES), F32)],
                 compiler_params=_params(("arbitrary", "arbitrary")))(
                     cpre, pdt, pz, y2, hprev_all, dcat, dtb, alog, dsk_lane, gs, emat, dproj)


def _sum_adamw(parts, w, m, v, *, name, layer=None, outs=None):
    n, r, c = parts.shape
    tr = _pick_rows(r, 256)
    bc1 = 1.0 - ADAM_B1 ** ADAM_STEP
    bc2 = 1.0 - ADAM_B2 ** ADAM_STEP

    def body(p_ref, w_ref, m_ref, v_ref, *rest):
        g_ref, d_ref, mo_ref, vo_ref = rest[-4:]
        g = p_ref[0].astype(F32)
        for k in range(1, n):
            g = g + p_ref[k].astype(F32)
        mn = ADAM_B1 * m_ref[...] + (1.0 - ADAM_B1) * g
        vn = ADAM_B2 * v_ref[...] + (1.0 - ADAM_B2) * (g * g)
        g_ref[...] = g
        mo_ref[...] = mn
        vo_ref[...] = vn
        d_ref[...] = -ADAM_LR * ((mn / bc1) / (jnp.sqrt(vn / bc2) + ADAM_EPS) + ADAM_WD * w_ref[...])

    p_spec = pl.BlockSpec((n, tr, c), lambda i: (0, i, 0))
    if layer is None:
        blk = pl.BlockSpec((tr, c), lambda i: (i, 0))
        return _call(body, name=name, grid=(r // tr,), in_specs=[p_spec, blk, blk, blk], out_specs=[blk] * 4,
                     out_shape=[jax.ShapeDtypeStruct((r, c), F32)] * 4,
                     compiler_params=_params(("parallel",)))(parts, w, m, v)
    blk = pl.BlockSpec((None, tr, c), lambda i: (layer, i, 0))
    if outs is None:
        outs = [lax.empty(w.shape, F32) for _ in range(4)]
    return _call(body, name=name, grid=(r // tr,),
                 in_specs=[p_spec, blk, blk, blk] + [pl.BlockSpec(memory_space=pl.ANY)] * 4, out_specs=[blk] * 4,
                 out_shape=[jax.ShapeDtypeStruct(w.shape, F32)] * 4, input_output_aliases={4 + k: k for k in range(4)},
                 compiler_params=_params(("parallel",)))(parts, w, m, v, *outs)


def _assemble_cols(blocks, *, name):
    nb, r, c = blocks.shape
    width = -(-nb * c // LANES) * LANES
    tr = _pick_rows(r, 256)

    def body(b_ref, o_ref):
        pieces = [b_ref[j] for j in range(nb)]
        if width > nb * c:
            pieces.append(jnp.zeros((tr, width - nb * c), blocks.dtype))
        o_ref[...] = jnp.concatenate(pieces, axis=1)

    return _call(body, name=name, grid=(r // tr,), in_specs=[pl.BlockSpec((nb, tr, c), lambda i: (0, i, 0))],
                 out_specs=pl.BlockSpec((tr, width), lambda i: (i, 0)), out_shape=jax.ShapeDtypeStruct((r, width), blocks.dtype),
                 compiler_params=_params(("parallel",)))(blocks)


def _split_cols(pieces, c, *, name):
    r = pieces[0].shape[0]
    tr = _pick_rows(r, 256)
    n_in = len(pieces)

    def body(*refs):
        o_ref = refs[n_in]
        x = jnp.concatenate([p[...] for p in refs[:n_in]], axis=1) if n_in > 1 else refs[0][...]
        for j in range(N_DEV):
            o_ref[j] = x[:, c * j:c * (j + 1)]

    return _call(body, name=name, grid=(r // tr,),
                 in_specs=[pl.BlockSpec((tr, p.shape[1]), lambda i: (i, 0)) for p in pieces],
                 out_specs=pl.BlockSpec((N_DEV, tr, c), lambda i: (0, i, 0)),
                 out_shape=jax.ShapeDtypeStruct((N_DEV, r, c), pieces[0].dtype),
                 compiler_params=_params(("parallel",)))(*pieces)


def _sum_parts(parts, *, name):
    n, r, c = parts.shape
    tr = _pick_rows(r, 256)

    def body(p_ref, g_ref):
        g = p_ref[0].astype(F32)
        for k in range(1, n):
            g = g + p_ref[k].astype(F32)
        g_ref[...] = g

    return _call(body, name=name, grid=(r // tr,), in_specs=[pl.BlockSpec((n, tr, c), lambda i: (0, i, 0))],
                 out_specs=pl.BlockSpec((tr, c), lambda i: (i, 0)), out_shape=jax.ShapeDtypeStruct((r, c), F32),
                 compiler_params=_params(("parallel",)))(parts)


def _peers():
    x, y, c = lax.axis_index("x"), lax.axis_index("y"), lax.axis_index("c")
    me = 4 * x + 2 * y + c
    out = []
    for k in range(1, N_DEV):
        px = (1 - x) if (k >> 2) & 1 else x
        py = (1 - y) if (k >> 1) & 1 else y
        pc = (1 - c) if k & 1 else c
        out.append(((px, py, pc), 4 * px + 2 * py + pc))
    return me, out


def _exchange(src, *, gather, name):
    shape = src.shape if gather else src.shape[1:]

    def body(s_ref, o_ref, send_sems, recv_sems, local_sem):
        me, peers = _peers()
        mine = pltpu.make_async_copy(s_ref if gather else s_ref.at[me], o_ref.at[me], local_sem)
        mine.start()
        sends = []
        for k, (dev, pid) in enumerate(peers):
            cp = pltpu.make_async_remote_copy(
                src_ref=s_ref if gather else s_ref.at[pid], dst_ref=o_ref.at[me],
                send_sem=send_sems.at[k], recv_sem=recv_sems.at[k], device_id=dev, device_id_type=MESH)
            cp.start()
            sends.append(cp)
        for k, (dev, pid) in enumerate(peers):
            pltpu.make_async_remote_copy(
                src_ref=s_ref if gather else s_ref.at[pid], dst_ref=o_ref.at[pid],
                send_sem=send_sems.at[k], recv_sem=recv_sems.at[k], device_id=dev, device_id_type=MESH).wait_recv()
        for cp in sends:
            cp.wait_send()
        mine.wait()

    any_spec = pl.BlockSpec(memory_space=pl.ANY)
    return _call(body, name=name, in_specs=[any_spec], out_specs=any_spec,
                 out_shape=jax.ShapeDtypeStruct((N_DEV,) + tuple(shape), src.dtype),
                 scratch_shapes=[pltpu.SemaphoreType.DMA((N_DEV - 1,)), pltpu.SemaphoreType.DMA((N_DEV - 1,)),
                                 pltpu.SemaphoreType.DMA(())])(src)


_HBM = pl.BlockSpec(memory_space=pltpu.HBM)
_SEM = pl.BlockSpec(memory_space=pltpu.SEMAPHORE)
_EFFECT = pltpu.SideEffectType.DATAFLOW_SIDE_EFFECTING


def _split_copies(s_refs, l_refs, send_sems, recv_sems, gather, incoming):
    me, peers = _peers()
    local, remote = [], []
    for ti, (s_ref, l_ref) in enumerate(zip(s_refs, l_refs)):
        base = ti * N_DEV
        local.append(pltpu.make_async_copy(s_ref if gather else s_ref.at[me], l_ref.at[me], recv_sems.at[base + N_DEV - 1]))
        for k, (dev, pid) in enumerate(peers):
            sems = dict(send_sem=send_sems.at[base + k], recv_sem=recv_sems.at[base + k], device_id=dev, device_id_type=MESH)
            src = s_ref if gather else s_ref.at[pid]
            remote.append((
                pltpu.make_async_remote_copy(src_ref=src, dst_ref=l_ref.at[me], **sems),
                pltpu.make_async_remote_copy(src_ref=src, dst_ref=l_ref.at[pid], **sems) if incoming else None))
    return local, remote


def _exchange_start(srcs, *, gather, name, after=()):
    n = len(srcs)
    after = list(after)
    srcs = [pltpu.with_memory_space_constraint(s, pltpu.HBM) for s in srcs]
    lands = [pltpu.with_memory_space_constraint(
        lax.empty((N_DEV,) + tuple(s.shape if gather else s.shape[1:]), s.dtype), pltpu.HBM) for s in srcs]

    def body(*refs):
        s_refs, l_refs = refs[:n], refs[n:2 * n]
        outs = refs[2 * n + len(after):]
        send_sems, recv_sems, token = outs[0], outs[1], outs[-1]
        local, remote = _split_copies(s_refs, l_refs, send_sems, recv_sems, gather, incoming=False)
        for cp in local:
            cp.start()
        for out_cp, _ in remote:
            out_cp.start()
        token[...] = jnp.zeros_like(token)

    outs = _call(
        body, name=name,
        out_shape=(pltpu.SemaphoreType.DMA((n * N_DEV,)), pltpu.SemaphoreType.DMA((n * N_DEV,)),
                   *[pltpu.HBM(s.shape, s.dtype) for s in srcs], *[pltpu.HBM(l.shape, l.dtype) for l in lands],
                   jax.ShapeDtypeStruct((SUBLANES, LANES), F32)),
        in_specs=[_HBM] * (2 * n) + [pl.BlockSpec(memory_space=pl.ANY)] * len(after),
        out_specs=(_SEM, _SEM, *[_HBM] * (2 * n), pl.BlockSpec(memory_space=pltpu.VMEM)),
        input_output_aliases={k: k + 2 for k in range(2 * n)},
        compiler_params=pltpu.CompilerParams(has_side_effects=_EFFECT),
    )(*srcs, *lands, *after)
    return dict(n=n, gather=gather, sems=outs[:2], srcs=outs[2:2 + n], lands=outs[2 + n:2 + 2 * n]), outs[-1]


def _exchange_wait(state, after, *, name):
    n, gather = state["n"], state["gather"]
    after = list(after)

    def body(*refs):
        s_refs, l_refs = refs[:n], refs[n:2 * n]
        send_sems, recv_sems = refs[2 * n], refs[2 * n + 1]
        local, remote = _split_copies(s_refs, l_refs, send_sems, recv_sems, gather, incoming=True)
        for out_cp, in_cp in remote:
            out_cp.wait_send()
            in_cp.wait_recv()
        for cp in local:
            cp.wait()

    outs = _call(
        body, name=name,
        out_shape=tuple(pltpu.HBM(a.shape, a.dtype) for a in (*state["srcs"], *state["lands"])),
        in_specs=[_HBM] * (2 * n) + [_SEM, _SEM] + [pl.BlockSpec(memory_space=pl.ANY)] * len(after),
        out_specs=tuple([_HBM] * (2 * n)),
        input_output_aliases={k: k for k in range(2 * n)},
        compiler_params=pltpu.CompilerParams(has_side_effects=_EFFECT),
    )(*state["srcs"], *state["lands"], *state["sems"], *after)
    return outs[n:]


def _pack(arrs):
    flat = jnp.concatenate([a.reshape(-1).astype(F32) for a in arrs])
    pad = (-flat.shape[0]) % (SUBLANES * LANES)
    return jnp.pad(flat, (0, pad)).reshape(-1, LANES)


def _unpack(packed, shapes):
    flat = packed.reshape(-1)
    out, off = [], 0
    for s in shapes:
        n = 1
        for v in s:
            n *= v
        out.append(flat[off:off + n].reshape(s))
        off += n
    return out


SMALL = ("norm_mix_pre", "ssm_conv_b", "dt_bias", "a_log", "d_skip", "conv_out_norm", "ssm_out_norm",
         "norm_mix_post", "norm_mlp_pre", "norm_mlp_post", "conv_a_w", "ssm_conv_w")
BIG = ("w_in", "w_out", "w_up", "w_down")
ORDER = ("norm_mix_pre", "w_in", "conv_a_w", "ssm_conv_w", "ssm_conv_b", "dt_bias", "a_log", "d_skip",
         "conv_out_norm", "ssm_out_norm", "w_out", "norm_mix_post", "norm_mlp_pre", "w_up", "w_down", "norm_mlp_post")


def kernel(x, norm_mix_pre, w_in, conv_a_w, ssm_conv_w, ssm_conv_b, dt_bias, a_log, d_skip, conv_out_norm, ssm_out_norm, w_out, norm_mix_post, norm_mlp_pre, w_up, w_down, norm_mlp_post, loss_target, m_norm_mix_pre, m_w_in, m_conv_a_w, m_ssm_conv_w, m_ssm_conv_b, m_dt_bias, m_a_log, m_d_skip, m_conv_out_norm, m_ssm_out_norm, m_w_out, m_norm_mix_post, m_norm_mlp_pre, m_w_up, m_w_down, m_norm_mlp_post, v_norm_mix_pre, v_w_in, v_conv_a_w, v_ssm_conv_w, v_ssm_conv_b, v_dt_bias, v_a_log, v_d_skip, v_conv_out_norm, v_ssm_out_norm, v_w_out, v_norm_mix_post, v_norm_mlp_pre, v_w_up, v_w_down, v_norm_mlp_post):
    W = dict(norm_mix_pre=norm_mix_pre, w_in=w_in, conv_a_w=conv_a_w, ssm_conv_w=ssm_conv_w, ssm_conv_b=ssm_conv_b,
             dt_bias=dt_bias, a_log=a_log, d_skip=d_skip, conv_out_norm=conv_out_norm, ssm_out_norm=ssm_out_norm,
             w_out=w_out, norm_mix_post=norm_mix_post, norm_mlp_pre=norm_mlp_pre, w_up=w_up, w_down=w_down,
             norm_mlp_post=norm_mlp_post)
    M = dict(norm_mix_pre=m_norm_mix_pre, w_in=m_w_in, conv_a_w=m_conv_a_w, ssm_conv_w=m_ssm_conv_w,
             ssm_conv_b=m_ssm_conv_b, dt_bias=m_dt_bias, a_log=m_a_log, d_skip=m_d_skip,
             conv_out_norm=m_conv_out_norm, ssm_out_norm=m_ssm_out_norm, w_out=m_w_out,
             norm_mix_post=m_norm_mix_post, norm_mlp_pre=m_norm_mlp_pre, w_up=m_w_up, w_down=m_w_down,
             norm_mlp_post=m_norm_mlp_post)
    V = dict(norm_mix_pre=v_norm_mix_pre, w_in=v_w_in, conv_a_w=v_conv_a_w, ssm_conv_w=v_ssm_conv_w,
             ssm_conv_b=v_ssm_conv_b, dt_bias=v_dt_bias, a_log=v_a_log, d_skip=v_d_skip,
             conv_out_norm=v_conv_out_norm, ssm_out_norm=v_ssm_out_norm, w_out=v_w_out,
             norm_mix_post=v_norm_mix_post, norm_mlp_pre=v_norm_mlp_pre, w_up=v_w_up, w_down=v_w_down,
             norm_mlp_post=v_norm_mlp_post)

    nseq, seq, d = x.shape
    t = nseq * seq
    depth = w_in.shape[0]
    h = d // HEAD_DIM
    xbc = d + 2 * SSM_GROUPS * D_STATE
    in_cols = w_in.shape[2] * N_DEV
    d_mix = w_out.shape[1] * N_DEV
    d_ff = w_up.shape[2] * N_DEV
    me = 4 * lax.axis_index("x") + 2 * lax.axis_index("y") + lax.axis_index("c")
    ca_shard = conv_a_w.shape[2]
    sc_shard = ssm_conv_w.shape[2]

    tap_shapes = [conv_a_w.shape[1:], ssm_conv_w.shape[1:]]

    def gather_start(i, after=()):
        st_in, tok_in = _exchange_start([w_in[i].astype(BF16), _pack([conv_a_w[i], ssm_conv_w[i]])], gather=True,
                                        name=f"gather_start_in_{i}", after=after)
        st_rest, tok_rest = _exchange_start([W[n][i].astype(BF16) for n in ("w_out", "w_up", "w_down")], gather=True,
                                            name=f"gather_start_rest_{i}", after=[tok_in])
        return st_in, st_rest, tok_rest

    vec = lambda name, i: W[name][i].reshape(1, -1)
    emat = (lax.broadcasted_iota(jnp.int32, (h, d), 1) // HEAD_DIM == lax.broadcasted_iota(jnp.int32, (h, d), 0)).astype(F32)

    xcur = x.reshape(t, d)
    hcur = _norm_fwd(xcur, vec("norm_mix_pre", 0), name="norm_first")
    saved = []
    nxt = gather_start(0)
    for i in range(depth):
        st_in, st_rest, tok = nxt
        win_g, taps_g = _exchange_wait(st_in, [hcur, tok], name=f"gather_wait_in_{i}")
        win = _assemble_cols(win_g, name=f"assemble_w_in_{i}")
        taps_j = [_unpack(taps_g[j], tap_shapes) for j in range(N_DEV)]
        conv_a_i = jnp.concatenate([tj[0] for tj in taps_j], axis=1)
        ssm_conv_i = jnp.concatenate([tj[1] for tj in taps_j], axis=1)
        proj = _mm(hcur, win, n=4 * d + xbc, name=f"fwd_proj_{i}", out_dtypes=(BF16,))
        pdt = _mm(hcur, win, n=LANES, b_off=4 * d + xbc, name=f"fwd_proj_dt_{i}")
        ya, va = _conva_fwd(proj, conv_a_i, vec("conv_out_norm", i), d=d, seq=seq, name=f"fwd_conv_a_{i}")
        cpre = _convb_fwd(proj, ssm_conv_i, vec("ssm_conv_b", i), col0=4 * d, seq=seq, name=f"fwd_conv_b_{i}")
        dsk_lane = jnp.repeat(W["d_skip"][i], HEAD_DIM).reshape(1, d)
        cat, y2, hprev = _ssd_fwd(cpre, pdt, proj, ya, vec("dt_bias", i), vec("a_log", i), dsk_lane,
                                  vec("ssm_out_norm", i), emat, nseq=nseq, seq=seq, name=f"fwd_ssd_{i}")
        wout_g, wup_g, wdown_g = _exchange_wait(st_rest, [cat], name=f"gather_wait_rest_{i}")
        lw = dict(win=win, wout=wout_g.reshape(d_mix, d),
                  wup=_assemble_cols(wup_g, name=f"assemble_w_up_{i}"), wdown=wdown_g.reshape(d_ff, d),
                  conv_a=conv_a_i, ssm_conv=ssm_conv_i)
        after = []
        if i + 1 < depth:
            nxt = gather_start(i + 1, after=[wout_g])
            after = [nxt[2]]
        x1, h2, mix = _mm(cat, lw["wout"], name=f"fwd_out_{i}", after=after, out_dtypes=(F32, BF16, BF16),
                          epi=_epi_resid_norm, extras=(xcur,), vecs=(vec("norm_mix_post", i), vec("norm_mlp_pre", i)),
                          tm_cap=FUSED_ROWS)
        f = _mm(h2, lw["wup"], name=f"fwd_up_{i}", out_dtypes=(BF16,), epi=_epi_relu2)
        g_next = vec("norm_mix_pre", i + 1) if i + 1 < depth else vec("norm_mix_pre", 0)
        dn = _mm(f, lw["wdown"], name=f"fwd_down_{i}")
        x2, hnext = _resid_norm(x1, dn, vec("norm_mlp_post", i), g_next, name=f"fwd_post_mlp_{i}")
        saved.append(dict(lw=lw, x0=xcur, h=hcur, proj=proj, pdt=pdt, va=va, cpre=cpre, y2=y2,
                          hprev=hprev, cat=cat, mix=mix, x1=x1, h2=h2, f=f, dn=dn, dsk_lane=dsk_lane))
        xcur, hcur = x2, hnext

    dx, loss_part = _loss_fwd_bwd(xcur, loss_target.reshape(t, d), name="loss")
    loss = lax.psum(loss_part[0, 0], ("x", "y", "c"))

    small_grads = {n: [None] * depth for n in SMALL}
    big_out = {n: None for n in BIG}

    def finish(pending, after):
        li, st_a, st_b = pending

        def update(n, parts):
            big_out[n] = _sum_adamw(parts, W[n], M[n], V[n], layer=li, outs=big_out[n], name=f"adamw_{n}_{li}")

        p_down, p_up = _exchange_wait(st_a, after, name=f"scatter_wait_a_{li}")
        update("w_down", p_down)
        update("w_up", p_up)
        p_out, p_in = _exchange_wait(st_b, after + [big_out["w_up"][0]], name=f"scatter_wait_b_{li}")
        update("w_out", p_out)
        update("w_in", p_in)

    pending = None
    for i in reversed(range(depth)):
        s = saved[i]
        lw = s["lw"]
        ddn, dg = _bwd_norm_out(s["dn"], vec("norm_mlp_post", i), dx, name=f"bwd_norm_mlp_post_{i}")
        small_grads["norm_mlp_post"][i] = dg
        dup = _mm(ddn, lw["wdown"], tb=True, name=f"bwd_down_dx_{i}", out_dtypes=(BF16,), epi=_epi_drelu2,
                  extras=(s["f"],))
        g_wdown = _mm(s["f"], ddn, ta=True, name=f"bwd_down_dw_{i}", out_dtypes=(BF16,))
        dh2 = _mm(dup, lw["wup"], tb=True, name=f"bwd_up_dx_{i}", out_dtypes=(BF16,))
        g_wup = _mm(s["h2"], dup, ta=True, name=f"bwd_up_dw_{i}", out_dtypes=(BF16,))
        st_a, tok_a = _exchange_start(
            [g_wdown.reshape(N_DEV, d_ff // N_DEV, d), _split_cols([g_wup], d_ff // N_DEV, name=f"split_g_w_up_{i}")],
            gather=False, name=f"scatter_start_a_{i}")
        dx1, dmix, dg_pre, dg_post = _bwd_norm_pair(s["x1"], [dh2], dx, s["mix"], vec("norm_mlp_pre", i) + tok_a[0:1, 0:1],
                                                    vec("norm_mix_post", i), name=f"bwd_norm_mix_post_{i}")
        small_grads["norm_mlp_pre"][i] = dg_pre
        small_grads["norm_mix_post"][i] = dg_post
        dcat = _mm(dmix, lw["wout"], tb=True, name=f"bwd_out_dx_{i}", out_dtypes=(BF16,))
        g_wout = _mm(s["cat"], dmix, ta=True, name=f"bwd_out_dw_{i}", out_dtypes=(BF16,))
        dproj, dcaw, dgca = _conva_bwd(dcat, s["proj"], s["va"], lw["conv_a"], vec("conv_out_norm", i), d=d, seq=seq,
                                       name=f"bwd_conv_a_{i}")
        small_grads["conv_a_w"][i] = dcaw
        small_grads["conv_out_norm"][i] = dgca
        dconv, dproj, dpdt, dgs, ddsk, ddtb, dalog = _ssd_bwd(
            s["cpre"], s["pdt"], s["proj"], s["y2"], s["hprev"], dcat, vec("dt_bias", i), vec("a_log", i),
            s["dsk_lane"], vec("ssm_out_norm", i), emat, dproj, nseq=nseq, seq=seq, name=f"bwd_ssd_{i}")
        small_grads["ssm_out_norm"][i] = dgs
        small_grads["d_skip"][i] = ddsk
        small_grads["dt_bias"][i] = ddtb
        small_grads["a_log"][i] = dalog
        dproj, dscw, dscb = _convb_bwd(dconv, s["proj"], lw["ssm_conv"], dproj, col0=4 * d, seq=seq,
                                       name=f"bwd_conv_b_{i}")
        small_grads["ssm_conv_w"][i] = dscw
        small_grads["ssm_conv_b"][i] = dscb
        g_win = _split_cols([
            _mm(s["h"], dproj, ta=True, name=f"bwd_proj_dw_{i}", out_dtypes=(BF16,)),
            _mm(s["h"], dpdt, ta=True, name=f"bwd_proj_dt_dw_{i}", out_dtypes=(BF16,))],
            in_cols // N_DEV, name=f"split_g_w_in_{i}")
        st_b, tok_b = _exchange_start(
            [g_wout.reshape(N_DEV, d_mix // N_DEV, d), g_win], gather=False, name=f"scatter_start_b_{i}")
        dh_parts = [_mm(dp, lw["win"], tb=True, b_koff=off, name=f"bwd_proj_{nm}dx_{i}", after=[tok_b], out_dtypes=(BF16,))
                    for nm, dp, off in (("", dproj, 0), ("dt_", dpdt, 4 * d + xbc))]
        dx, dg_in = _bwd_norm_in(s["x0"], dh_parts, dx1, vec("norm_mix_pre", i), name=f"bwd_norm_mix_pre_{i}")
        small_grads["norm_mix_pre"][i] = dg_in
        if pending is not None:
            finish(pending, [dx])
        pending = (i, st_a, st_b)

    grad_x = dx.reshape(nseq, seq, d)

    small_shapes_full = {n: (depth,) + tuple(small_grads[n][0].shape) for n in SMALL}
    gpack = _pack([jnp.stack(small_grads[n]) for n in SMALL])
    st_small, tok_small = _exchange_start([gpack], gather=True, name="allreduce_small_start")
    finish(pending, [dx, tok_small])
    gparts, = _exchange_wait(st_small, [big_out["w_in"][0]], name="allreduce_small_wait")

    def shard_of(n, full):
        if n == "conv_a_w":
            return lax.dynamic_slice_in_dim(full, me * ca_shard, ca_shard, axis=2)
        if n == "ssm_conv_w":
            return lax.dynamic_slice_in_dim(full, me * sc_shard, sc_shard, axis=2)
        return full.reshape(W[n].shape)

    gsum = _sum_parts(gparts, name="sum_small")
    gfull = _unpack(gsum, [small_shapes_full[n] for n in SMALL])
    gsmall = {n: shard_of(n, gf) for n, gf in zip(SMALL, gfull)}
    res = _sum_adamw(_pack([gsmall[n] for n in SMALL])[None], _pack([W[n] for n in SMALL]),
                     _pack([M[n] for n in SMALL]), _pack([V[n] for n in SMALL]), name="adamw_small")
    small_out = [dict(zip(SMALL, _unpack(r, [W[n].shape for n in SMALL]))) for r in res]

    def out_of(kind, n):
        return big_out[n][kind] if n in BIG else small_out[kind][n]

    return (loss, grad_x, *[out_of(k, n) for k in range(4) for n in ORDER])
```

```python
import functools

import jax
import jax.numpy as jnp
from jax import lax
from jax.experimental import pallas as pl
from jax.experimental.pallas import tpu as pltpu

F32 = jnp.float32
BF16 = jnp.bfloat16
HIGHEST = lax.Precision.HIGHEST
MESH = pl.DeviceIdType.MESH

EPS = 1e-6
HEAD_DIM = 64
D_STATE = 128
SSM_GROUPS = 2
CHUNK = 128
CONV_K = 3
SSM_CONV_K = 4
ADAM_LR = 0.001
ADAM_B1 = 0.9
ADAM_B2 = 0.999
ADAM_EPS = 1e-08
ADAM_WD = 0.01
ADAM_STEP = 10

N_DEV = 8
LANES = 128
SUBLANES = 8
VMEM_LIMIT = 48 * 1024 * 1024
ROW_TILE = 512
MM_TILE = 1024
MM_TILE_N = 1536
FUSED_ROWS = 512


def _params(sem):
    return pltpu.CompilerParams(dimension_semantics=sem, vmem_limit_bytes=VMEM_LIMIT)


def _call(body, **kw):
    return pl.pallas_call(body, **kw)


def _pick(n, cap):
    best = None
    for t in range(LANES, min(n, cap) + 1, LANES):
        if n % t == 0:
            best = t
    return best or n


def _pick_rows(n, cap):
    best = None
    for t in range(SUBLANES, min(n, cap) + 1, SUBLANES):
        if n % t == 0:
            best = t
    return best or n


def _sigmoid(x):
    return 1.0 / (1.0 + jnp.exp(-x))


def _softplus(x):
    return jnp.maximum(x, 0.0) + jnp.log1p(jnp.exp(-jnp.abs(x)))


def _rms(x):
    return lax.rsqrt(jnp.mean(x * x, axis=-1, keepdims=True) + EPS)


def _rms_bwd(x, r, g, dy):
    gy = dy * g
    dx = r * gy - x * (r * r * r) * jnp.mean(gy * x, axis=-1, keepdims=True)
    return dx, dy * x * r


def _full(shape):
    return pl.BlockSpec(shape, lambda *_: (0,) * len(shape))


def _mm(a, b, *, name, ta=False, tb=False, out_dtypes=(F32,), epi=None, extras=(), n=None, b_off=0, b_koff=0,
        after=(), vecs=(), n_sums=0, tm_cap=MM_TILE):
    m, k = (a.shape[1], a.shape[0]) if ta else a.shape
    if n is None:
        n = b.shape[0] if tb else b.shape[1]
    tm, tn, tk = _pick(m, tm_cap), _pick(n, MM_TILE_N), _pick(k, MM_TILE)
    while b_off % tn or n % tn:
        tn -= LANES
    if b_koff == 0 and k > MM_TILE:
        tk = _pick(k, MM_TILE_N)
    while b_koff % tk or k % tk:
        tk -= LANES
    nk = k // tk
    nm, nn = m // tm, n // tn
    jo = b_off // tn
    ko = b_koff // tk
    a_bytes = m * k * a.dtype.itemsize
    b_bytes = n * k * b.dtype.itemsize
    m_outer = a_bytes + nm * b_bytes <= b_bytes + nn * a_bytes
    ij = (lambda g0, g1: (g0, g1)) if m_outer else (lambda g0, g1: (g1, g0))
    grid = (nm, nn, nk) if m_outer else (nn, nm, nk)

    def a_map(g0, g1, kk):
        i, _ = ij(g0, g1)
        return (kk, i) if ta else (i, kk)

    def b_map(g0, g1, kk):
        _, j = ij(g0, g1)
        return (j + jo, kk + ko) if tb else (kk + ko, j + jo)

    def o_map(g0, g1, kk):
        return ij(g0, g1)

    a_spec = pl.BlockSpec((tk, tm) if ta else (tm, tk), a_map)
    b_spec = pl.BlockSpec((tn, tk) if tb else (tk, tn), b_map)
    o_spec = pl.BlockSpec((tm, tn), o_map)
    dims = (((0 if ta else 1,), (1 if tb else 0,)), ((), ()))
    n_ex = len(extras) + len(vecs)
    after = list(after)
    o0 = 2 + n_ex + len(after)
    n_out = len(out_dtypes)
    assert n_sums == 0 or nn == 1

    def finish(acc, ex, outs):
        res = (acc,) if epi is None else epi(acc, *[e[...] for e in ex])
        for o, r in zip(outs[:n_out], res[:n_out]):
            o[...] = r.astype(o.dtype)
        for o, r in zip(outs[n_out:], res[n_out:]):
            o[...] += jnp.sum(r, axis=0, keepdims=True)

    def zero_sums(outs, kk):
        if n_sums:
            @pl.when((pl.program_id(0) == 0) & (pl.program_id(1) == 0) & (kk == 0))
            def _():
                for o in outs[n_out:]:
                    o[...] = jnp.zeros_like(o)

    def body_single(*refs):
        a_ref, b_ref = refs[:2]
        zero_sums(refs[o0:], 0)
        acc = lax.dot_general(a_ref[...].astype(BF16), b_ref[...].astype(BF16), dims, preferred_element_type=F32)
        finish(acc, refs[2:2 + n_ex], refs[o0:])

    def body_multi(*refs):
        a_ref, b_ref = refs[:2]
        acc = refs[-1]
        kk = pl.program_id(2)
        zero_sums(refs[o0:-1], kk)

        @pl.when(kk == 0)
        def _():
            acc[...] = jnp.zeros_like(acc)

        acc[...] += lax.dot_general(a_ref[...].astype(BF16), b_ref[...].astype(BF16), dims, preferred_element_type=F32)

        @pl.when(kk == nk - 1)
        def _():
            finish(acc[...], refs[2:2 + n_ex], refs[o0:-1])

    v_spec = pl.BlockSpec((1, tn), lambda g0, g1, kk: (0, ij(g0, g1)[1]))
    outs = _call(
        body_single if nk == 1 else body_multi, name=name, grid=grid,
        in_specs=([a_spec, b_spec] + [o_spec] * len(extras) + [v_spec] * len(vecs)
                  + [pl.BlockSpec(memory_space=pl.ANY)] * len(after)),
        out_specs=[o_spec] * n_out + [v_spec] * n_sums,
        out_shape=[jax.ShapeDtypeStruct((m, n), dt) for dt in out_dtypes] + [jax.ShapeDtypeStruct((1, n), F32)] * n_sums,
        scratch_shapes=[] if nk == 1 else [pltpu.VMEM((tm, tn), F32)],
        compiler_params=_params(("parallel", "parallel", "arbitrary") if n_sums == 0 else ("arbitrary",) * 3),
    )(a, b, *extras, *vecs, *after)
    return outs[0] if len(outs) == 1 else outs


def _epi_resid_norm(acc, x, g_res, g_next):
    xn = x + acc * _rms(acc) * g_res
    return xn, xn * _rms(xn) * g_next, acc


def _epi_bwd_norm_pair(acc, x, dres, n, g_in, g_out):
    dxh, dgi = _rms_bwd(x, _rms(x), g_in, acc)
    dx = dres + dxh
    nv = n.astype(F32)
    dn, dgo = _rms_bwd(nv, _rms(nv), g_out, dx)
    return dx, dn, dgi, dgo


def _epi_bwd_norm_in(acc, x, dres, dh_more, g_in):
    dxh, dgi = _rms_bwd(x, _rms(x), g_in, acc + dh_more.astype(F32))
    return dres + dxh, dgi


def _epi_relu2(acc):
    r = jnp.maximum(acc, 0.0)
    return (r * r,)


def _epi_drelu2(acc, f):
    return (acc * (2.0 * jnp.sqrt(f.astype(F32))),)


def _norm_fwd(x, g, *, name):
    t, d = x.shape
    tt = _pick_rows(t, ROW_TILE)

    def body(x_ref, g_ref, h_ref):
        xv = x_ref[...]
        h_ref[...] = (xv * _rms(xv) * g_ref[...]).astype(BF16)

    row = pl.BlockSpec((tt, d), lambda i: (i, 0))
    return _call(body, name=name, grid=(t // tt,), in_specs=[row, _full((1, d))], out_specs=row,
                 out_shape=jax.ShapeDtypeStruct((t, d), BF16), compiler_params=_params(("parallel",)))(x, g)


def _resid_norm(x, n, g1, g2, *, name):
    t, d = x.shape
    tt = _pick_rows(t, ROW_TILE)

    def body(x_ref, n_ref, g1_ref, g2_ref, xo_ref, h_ref):
        nv = n_ref[...].astype(F32)
        xn = x_ref[...] + nv * _rms(nv) * g1_ref[...]
        xo_ref[...] = xn
        h_ref[...] = (xn * _rms(xn) * g2_ref[...]).astype(BF16)

    row = pl.BlockSpec((tt, d), lambda i: (i, 0))
    return _call(body, name=name, grid=(t // tt,), in_specs=[row, row, _full((1, d)), _full((1, d))],
                 out_specs=[row, row],
                 out_shape=[jax.ShapeDtypeStruct((t, d), F32), jax.ShapeDtypeStruct((t, d), BF16)],
                 compiler_params=_params(("parallel",)))(x, n, g1, g2)


def _loss_fwd_bwd(xf, target, *, name):
    t, d = xf.shape
    tt = _pick_rows(t, ROW_TILE)
    nt = t // tt

    def body(x_ref, t_ref, dy_ref, loss_ref, acc):
        i = pl.program_id(0)

        @pl.when(i == 0)
        def _():
            acc[...] = jnp.zeros_like(acc)

        e = x_ref[...] - t_ref[...]
        dy_ref[...] = e * (1.0 / d)
        acc[...] += jnp.sum(e * e, axis=0, keepdims=True)

        @pl.when(i == nt - 1)
        def _():
            loss_ref[...] = jnp.sum(acc[...], axis=-1, keepdims=True) * (0.5 / d)

    row = pl.BlockSpec((tt, d), lambda i: (i, 0))
    return _call(body, name=name, grid=(nt,), in_specs=[row, row], out_specs=[row, _full((1, 1))],
                 out_shape=[jax.ShapeDtypeStruct((t, d), F32), jax.ShapeDtypeStruct((1, 1), F32)],
                 scratch_shapes=[pltpu.VMEM((1, d), F32)], compiler_params=_params(("arbitrary",)))(xf, target)


def _bwd_norm_pair(xin, dh, dres, n, g_in, g_out, *, name):
    t, d = xin.shape
    tt = _pick_rows(t, ROW_TILE)
    n_dh = len(dh)

    def body(*refs):
        x_ref = refs[0]
        dh_refs = refs[1:1 + n_dh]
        dres_ref, n_ref, gi_ref, go_ref, dx_ref, dn_ref, dgi_ref, dgo_ref = refs[1 + n_dh:]
        i = pl.program_id(0)

        @pl.when(i == 0)
        def _():
            dgi_ref[...] = jnp.zeros_like(dgi_ref)
            dgo_ref[...] = jnp.zeros_like(dgo_ref)

        xv = x_ref[...]
        dhv = dh_refs[0][...].astype(F32)
        for r in dh_refs[1:]:
            dhv = dhv + r[...].astype(F32)
        dxh, dgi = _rms_bwd(xv, _rms(xv), gi_ref[...], dhv)
        dx = dres_ref[...] + dxh
        dx_ref[...] = dx
        dgi_ref[...] += jnp.sum(dgi, axis=0, keepdims=True)
        nv = n_ref[...].astype(F32)
        dn, dgo = _rms_bwd(nv, _rms(nv), go_ref[...], dx)
        dn_ref[...] = dn.astype(BF16)
        dgo_ref[...] += jnp.sum(dgo, axis=0, keepdims=True)

    row = pl.BlockSpec((tt, d), lambda i: (i, 0))
    vec = _full((1, d))
    return _call(body, name=name, grid=(t // tt,), in_specs=[row] * (n_dh + 3) + [vec, vec],
                 out_specs=[row, row, vec, vec],
                 out_shape=[jax.ShapeDtypeStruct((t, d), F32), jax.ShapeDtypeStruct((t, d), BF16),
                            jax.ShapeDtypeStruct((1, d), F32), jax.ShapeDtypeStruct((1, d), F32)],
                 compiler_params=_params(("arbitrary",)))(xin, *dh, dres, n, g_in, g_out)


def _bwd_norm_in(xin, dh, dres, g_in, *, name):
    t, d = xin.shape
    tt = _pick_rows(t, ROW_TILE)
    n_dh = len(dh)

    def body(*refs):
        x_ref = refs[0]
        dh_refs = refs[1:1 + n_dh]
        dres_ref, gi_ref, dx_ref, dgi_ref = refs[1 + n_dh:]
        i = pl.program_id(0)

        @pl.when(i == 0)
        def _():
            dgi_ref[...] = jnp.zeros_like(dgi_ref)

        xv = x_ref[...]
        dhv = dh_refs[0][...].astype(F32)
        for r in dh_refs[1:]:
            dhv = dhv + r[...].astype(F32)
        dxh, dgi = _rms_bwd(xv, _rms(xv), gi_ref[...], dhv)
        dx_ref[...] = dres_ref[...] + dxh
        dgi_ref[...] += jnp.sum(dgi, axis=0, keepdims=True)

    row = pl.BlockSpec((tt, d), lambda i: (i, 0))
    vec = _full((1, d))
    return _call(body, name=name, grid=(t // tt,), in_specs=[row] * (n_dh + 2) + [vec],
                 out_specs=[row, vec],
                 out_shape=[jax.ShapeDtypeStruct((t, d), F32), jax.ShapeDtypeStruct((1, d), F32)],
                 compiler_params=_params(("arbitrary",)))(xin, *dh, dres, g_in)


def _bwd_norm_out(n, g_out, dx, *, name):
    t, d = n.shape
    tt = _pick_rows(t, ROW_TILE)

    def body(n_ref, go_ref, dx_ref, dn_ref, dgo_ref):
        i = pl.program_id(0)

        @pl.when(i == 0)
        def _():
            dgo_ref[...] = jnp.zeros_like(dgo_ref)

        nv = n_ref[...].astype(F32)
        dn, dgo = _rms_bwd(nv, _rms(nv), go_ref[...], dx_ref[...])
        dn_ref[...] = dn.astype(BF16)
        dgo_ref[...] += jnp.sum(dgo, axis=0, keepdims=True)

    row = pl.BlockSpec((tt, d), lambda i: (i, 0))
    vec = _full((1, d))
    return _call(body, name=name, grid=(t // tt,), in_specs=[row, vec, row], out_specs=[row, vec],
                 out_shape=[jax.ShapeDtypeStruct((t, d), BF16), jax.ShapeDtypeStruct((1, d), F32)],
                 compiler_params=_params(("arbitrary",)))(n, g_out, dx)


def _shift_down(cur, halo, s):
    return jnp.concatenate([halo[SUBLANES - s:], cur[:cur.shape[0] - s]], axis=0)


def _shift_up(cur, halo, s):
    return jnp.concatenate([cur[s:], halo[:s]], axis=0)


def _conva_fwd(pa, w, g, *, d, seq, name):
    t = pa.shape[0]
    tt = _pick_rows(seq, ROW_TILE)
    tps = seq // tt

    def body(xa_ref, ca_ref, ba_ref, w_ref, g_ref, ya_ref, v_ref, carry):
        i = pl.program_id(0)

        @pl.when(i % tps == 0)
        def _():
            carry[...] = jnp.zeros_like(carry)

        u = ca_ref[...].astype(F32) * xa_ref[...].astype(F32)
        halo = carry[...]
        wv = w_ref[...]
        v = wv[2:3] * u + wv[1:2] * _shift_down(u, halo, 1) + wv[0:1] * _shift_down(u, halo, 2)
        carry[...] = u[tt - SUBLANES:]
        yp = ba_ref[...].astype(F32) * v
        ya_ref[...] = (yp * _rms(yp) * g_ref[...]).astype(BF16)
        v_ref[...] = v.astype(BF16)

    col = lambda c: pl.BlockSpec((tt, d), lambda i, c=c: (i, c))
    row = pl.BlockSpec((tt, d), lambda i: (i, 0))
    return _call(body, name=name, grid=(t // tt,),
                 in_specs=[col(0), col(1), col(2), _full((CONV_K, d)), _full((1, d))], out_specs=[row, row],
                 out_shape=[jax.ShapeDtypeStruct((t, d), BF16), jax.ShapeDtypeStruct((t, d), BF16)],
                 scratch_shapes=[pltpu.VMEM((SUBLANES, d), F32)],
                 compiler_params=_params(("arbitrary",)))(pa, pa, pa, w, g)


def _conva_bwd(dcat, pa, v, w, g, *, d, seq, name):
    t, width = pa.shape
    d3 = 3 * d
    tt = _pick_rows(seq, ROW_TILE)
    tps = seq // tt
    nt = t // tt

    def body(dya_ref, xa_ref, ca_ref, ba_ref, v_ref, w_ref, g_ref, dpa_ref, dw_ref, dg_ref, carry):
        i = pl.program_id(0)

        @pl.when(i == 0)
        def _():
            dw_ref[...] = jnp.zeros_like(dw_ref)
            dg_ref[...] = jnp.zeros_like(dg_ref)

        @pl.when(i % tps == 0)
        def _():
            carry[...] = jnp.zeros_like(carry)

        xa, ca, ba, vv = [r[...].astype(F32) for r in (xa_ref, ca_ref, ba_ref, v_ref)]
        yp = ba * vv
        dyp, dgt = _rms_bwd(yp, _rms(yp), g_ref[...], dya_ref[...].astype(F32))
        dg_ref[...] += jnp.sum(dgt, axis=0, keepdims=True)
        dv = dyp * ba
        halo = carry[...]
        dv1 = _shift_up(dv, halo, 1)
        dv2 = _shift_up(dv, halo, 2)
        carry[...] = dv[:SUBLANES]
        wv = w_ref[...]
        du = wv[2:3] * dv + wv[1:2] * dv1 + wv[0:1] * dv2
        u = ca * xa
        dw_ref[0:1, :] += jnp.sum(u * dv2, axis=0, keepdims=True)
        dw_ref[1:2, :] += jnp.sum(u * dv1, axis=0, keepdims=True)
        dw_ref[2:3, :] += jnp.sum(u * dv, axis=0, keepdims=True)
        dpa_ref[:, 0:d] = (du * ca).astype(BF16)
        dpa_ref[:, d:2 * d] = (du * xa).astype(BF16)
        dpa_ref[:, 2 * d:3 * d] = (dyp * vv).astype(BF16)

    rcol = lambda c: pl.BlockSpec((tt, d), lambda i, c=c: (nt - 1 - i, c))
    return _call(body, name=name, grid=(nt,),
                 in_specs=[rcol(0), rcol(0), rcol(1), rcol(2), rcol(0), _full((CONV_K, d)), _full((1, d))],
                 out_specs=[pl.BlockSpec((tt, d3), lambda i: (nt - 1 - i, 0)), _full((CONV_K, d)), _full((1, d))],
                 out_shape=[jax.ShapeDtypeStruct((t, width), BF16), jax.ShapeDtypeStruct((CONV_K, d), F32),
                            jax.ShapeDtypeStruct((1, d), F32)],
                 scratch_shapes=[pltpu.VMEM((SUBLANES, d), F32)],
                 compiler_params=_params(("arbitrary",)))(dcat, pa, pa, pa, v, w, g)


CONV_CH = 512


def _convb_fwd(proj, w, bias, *, col0, seq, name):
    t = proj.shape[0]
    c = w.shape[1]
    cb = _pick(c, CONV_CH)
    assert col0 % cb == 0
    tt = _pick_rows(seq, 2 * ROW_TILE)
    tps = seq // tt

    def body(p_ref, w_ref, b_ref, o_ref, carry):
        i = pl.program_id(1)

        @pl.when(i % tps == 0)
        def _():
            carry[...] = jnp.zeros_like(carry)

        p = p_ref[...].astype(F32)
        halo = carry[...]
        wv = w_ref[...]
        o = wv[3:4] * p + b_ref[...]
        for s in (1, 2, 3):
            o = o + wv[3 - s:4 - s] * _shift_down(p, halo, s)
        carry[...] = p[tt - SUBLANES:]
        o_ref[...] = o.astype(BF16)

    return _call(body, name=name, grid=(c // cb, t // tt),
                 in_specs=[pl.BlockSpec((tt, cb), lambda jc, i: (i, col0 // cb + jc)),
                           pl.BlockSpec((SSM_CONV_K, cb), lambda jc, i: (0, jc)), pl.BlockSpec((1, cb), lambda jc, i: (0, jc))],
                 out_specs=pl.BlockSpec((tt, cb), lambda jc, i: (i, jc)), out_shape=jax.ShapeDtypeStruct((t, c), BF16),
                 scratch_shapes=[pltpu.VMEM((SUBLANES, cb), F32)],
                 compiler_params=_params(("arbitrary", "arbitrary")))(proj, w, bias)


def _convb_bwd(dconv, proj, w, dproj, *, col0, seq, name):
    t, c = dconv.shape
    cb = _pick(c, CONV_CH)
    assert col0 % cb == 0
    tt = _pick_rows(seq, 2 * ROW_TILE)
    tps = seq // tt
    nt = t // tt

    def body(dc_ref, p_ref, w_ref, dproj_in, dp_ref, dw_ref, db_ref, carry):
        del dproj_in
        i = pl.program_id(1)

        @pl.when(i == 0)
        def _():
            dw_ref[...] = jnp.zeros_like(dw_ref)
            db_ref[...] = jnp.zeros_like(db_ref)

        @pl.when(i % tps == 0)
        def _():
            carry[...] = jnp.zeros_like(carry)

        dc = dc_ref[...].astype(F32)
        p = p_ref[...].astype(F32)
        halo = carry[...]
        wv = w_ref[...]
        dp = wv[3:4] * dc
        dw_ref[3:4, :] += jnp.sum(p * dc, axis=0, keepdims=True)
        for s in (1, 2, 3):
            dcs = _shift_up(dc, halo, s)
            dp = dp + wv[3 - s:4 - s] * dcs
            dw_ref[3 - s:4 - s, :] += jnp.sum(p * dcs, axis=0, keepdims=True)
        carry[...] = dc[:SUBLANES]
        db_ref[...] += jnp.sum(dc, axis=0, keepdims=True)
        dp_ref[...] = dp.astype(BF16)

    win_spec = pl.BlockSpec((tt, cb), lambda jc, i: (nt - 1 - i, col0 // cb + jc))
    taps = pl.BlockSpec((SSM_CONV_K, cb), lambda jc, i: (0, jc))
    return _call(body, name=name, grid=(c // cb, nt),
                 in_specs=[pl.BlockSpec((tt, cb), lambda jc, i: (nt - 1 - i, jc)), win_spec, taps,
                           pl.BlockSpec(memory_space=pl.ANY)],
                 out_specs=[win_spec, taps, pl.BlockSpec((1, cb), lambda jc, i: (0, jc))],
                 out_shape=[jax.ShapeDtypeStruct(dproj.shape, BF16), jax.ShapeDtypeStruct((SSM_CONV_K, c), F32),
                            jax.ShapeDtypeStruct((1, c), F32)],
                 input_output_aliases={3: 0},
                 scratch_shapes=[pltpu.VMEM((SUBLANES, cb), F32)],
                 compiler_params=_params(("arbitrary", "arbitrary")))(dconv, proj, w, dproj)


def _expand_heads(x, ev):
    return jnp.dot(x, ev, precision=HIGHEST, preferred_element_type=F32)


def _head_sums(v, ev):
    return lax.dot_general(v, ev, (((1,), (1,)), ((), ())), precision=HIGHEST, preferred_element_type=F32)


def _ssd_common(c_ref, pdt_ref, dtb_ref, alog_ref, e_ref, h):
    cp = c_ref[...].astype(F32)
    sg = _sigmoid(cp)
    act = cp * sg
    pre = pdt_ref[:, 0:h] + dtb_ref[...]
    dt = _softplus(pre)
    a = -jnp.exp(alog_ref[...])
    adt = dt * a
    row = lax.broadcasted_iota(jnp.int32, (CHUNK, CHUNK), 0)
    col = lax.broadcasted_iota(jnp.int32, (CHUNK, CHUNK), 1)
    tril = row >= col
    cs = jnp.dot(tril.astype(F32), adt, precision=HIGHEST, preferred_element_type=F32)
    cs_t = lax.dot_general(adt, (col >= row).astype(F32), (((0,), (0,)), ((), ())), precision=HIGHEST,
                           preferred_element_type=F32)
    ev = e_ref[...]
    dt_l = _expand_heads(dt, ev)
    ecs_l = jnp.exp(_expand_heads(cs, ev))
    return dict(cp=cp, sg=sg, act=act, pre=pre, dt=dt, a=a, cs=cs, cs_t=cs_t, dt_l=dt_l, ecs_l=ecs_l,
                tril=tril, row=row, col=col, lo=col < HEAD_DIM)


def _dot_nt(a, b):
    return lax.dot_general(a, b, (((1,), (1,)), ((), ())), preferred_element_type=F32)


def _dot_tn(a, b):
    return lax.dot_general(a, b, (((0,), (0,)), ((), ())), preferred_element_type=F32)


def _dot(a, b):
    return jnp.dot(a, b, preferred_element_type=F32)


def _ssd_fwd(cpre, pdt, pz, ya, dtb, alog, dsk_lane, gs, emat, *, nseq, seq, name):
    t, xbc = cpre.shape
    d = ya.shape[1]
    h = d // HEAD_DIM
    npair = h // 2
    ppg = npair // SSM_GROUPS
    nc = seq // CHUNK
    gw = d // SSM_GROUPS
    bc0 = d
    cc0 = d + SSM_GROUPS * D_STATE

    def body(c_ref, pdt_ref, z_ref, ya_ref, dtb_ref, alog_ref, dsk_ref, gs_ref, e_ref, cat_ref, y2_ref, hp_ref, h_ref):
        @pl.when(pl.program_id(0) == 0)
        def _():
            h_ref[...] = jnp.zeros_like(h_ref)

        for sq in range(nseq):
            one_seq(c_ref.at[sq], pdt_ref.at[sq], z_ref.at[sq], ya_ref.at[sq], dtb_ref, alog_ref, dsk_ref, gs_ref, e_ref,
                    cat_ref.at[sq], y2_ref.at[sq], hp_ref.at[sq], h_ref.at[sq])

    def one_seq(c_ref, pdt_ref, z_ref, ya_ref, dtb_ref, alog_ref, dsk_ref, gs_ref, e_ref, cat_ref, y2_ref, hp_ref, h_ref):
        q = _ssd_common(c_ref, pdt_ref, dtb_ref, alog_ref, e_ref, h)
        act, cs, lo, ecs_l = q["act"], q["cs"], q["lo"], q["ecs_l"]
        xs = act[:, :d]
        xd = xs * q["dt_l"]
        ys = []
        for g in range(SSM_GROUPS):
            bg = act[:, bc0 + g * D_STATE: bc0 + (g + 1) * D_STATE]
            cgb = act[:, cc0 + g * D_STATE: cc0 + (g + 1) * D_STATE].astype(BF16)
            s = _dot_nt(cgb, bg.astype(BF16))
            bg_t = bg.T
            for jj in range(ppg):
                j = g * ppg + jj
                sl = slice(LANES * j, LANES * (j + 1))
                xdj = xd[:, sl]
                x2 = jnp.concatenate([jnp.where(lo, xdj, 0.0), jnp.where(lo, 0.0, xdj)], axis=0).astype(BF16)
                hprev = h_ref[j]
                hp_ref[j] = hprev.astype(BF16)
                ms, bws_t = [], []
                for hh in (2 * j, 2 * j + 1):
                    csc = cs[:, hh:hh + 1]
                    cs_row = q["cs_t"][hh:hh + 1, :]
                    seg = jnp.broadcast_to(csc, (CHUNK, CHUNK)) - jnp.broadcast_to(cs_row, (CHUNK, CHUNK))
                    ms.append(s * jnp.exp(jnp.where(q["tril"], seg, -jnp.inf)))
                    bws_t.append(bg_t * jnp.exp(cs_row[:, CHUNK - 1:CHUNK] - cs_row))
                ydiag = _dot(jnp.concatenate(ms, axis=1).astype(BF16), x2)
                st = _dot(jnp.concatenate(bws_t, axis=1).astype(BF16), x2)
                ecs = ecs_l[:, sl]
                yoff = _dot(cgb, hprev.astype(BF16)) * ecs
                h_ref[j] = hprev * ecs[CHUNK - 1:CHUNK] + st
                ys.append(ydiag + yoff)
        y = jnp.concatenate(ys, axis=1) + dsk_ref[...] * xs
        y2_ref[...] = y.astype(BF16)
        zv = z_ref[...].astype(F32)
        y3 = y * (zv * _sigmoid(zv))
        cat_ref[:, 0:d] = ya_ref[...]
        for gi in range(SSM_GROUPS):
            seg = y3[:, gi * gw:(gi + 1) * gw]
            cat_ref[:, d + gi * gw:d + (gi + 1) * gw] = (seg * _rms(seg) * gs_ref[:, gi * gw:(gi + 1) * gw]).astype(BF16)

    chunk = lambda w, cb=0: pl.BlockSpec((nseq, CHUNK, w), lambda c, cb=cb: (0, c, cb))
    vec = lambda w: pl.BlockSpec((1, w), lambda c: (0, 0))
    hp_spec = pl.BlockSpec((nseq, None, npair, D_STATE, LANES), lambda c: (0, c, 0, 0, 0))
    per_seq = lambda a: a.reshape(nseq, seq, a.shape[1])
    cat, y2, hp = _call(
        body, name=name, grid=(nc,),
        in_specs=[chunk(xbc), chunk(LANES), chunk(d, 3), chunk(d), vec(h), vec(h), vec(d), vec(d),
                  pl.BlockSpec((h, d), lambda c: (0, 0))],
        out_specs=[chunk(2 * d), chunk(d), hp_spec],
        out_shape=[jax.ShapeDtypeStruct((nseq, seq, 2 * d), BF16), jax.ShapeDtypeStruct((nseq, seq, d), BF16),
                   jax.ShapeDtypeStruct((nseq, nc, npair, D_STATE, LANES), BF16)],
        scratch_shapes=[pltpu.VMEM((nseq, npair, D_STATE, LANES), F32)],
        compiler_params=_params(("arbitrary",)))(
            per_seq(cpre), per_seq(pdt), per_seq(pz), per_seq(ya), dtb, alog, dsk_lane, gs, emat)
    return cat.reshape(t, 2 * d), y2.reshape(t, d), hp


def _ssd_bwd(cpre, pdt, pz, y2, hprev_all, dcat, dtb, alog, dsk_lane, gs, emat, dproj, *, nseq, seq, name):
    t, xbc = cpre.shape
    d = y2.shape[1]
    h = d // HEAD_DIM
    npair = h // 2
    ppg = npair // SSM_GROUPS
    nc = seq // CHUNK
    gw = d // SSM_GROUPS
    bc0 = d
    cc0 = d + SSM_GROUPS * D_STATE

    def body(c_ref, pdt_ref, z_ref, y2_ref, hp_ref, dys_ref, dtb_ref, alog_ref, dsk_ref, gs_ref, e_ref, dproj_in,
             dconv_ref, dz_ref, dpdt_ref, dgs_ref, ddsk_ref, ddtb_ref, dalog_ref, dh_ref):
        del dproj_in
        b = pl.program_id(0)
        c = pl.program_id(1)

        @pl.when(c == 0)
        def _():
            dh_ref[...] = jnp.zeros_like(dh_ref)

        @pl.when((b == 0) & (c == 0))
        def _():
            dgs_ref[...] = jnp.zeros_like(dgs_ref)
            ddsk_ref[...] = jnp.zeros_like(ddsk_ref)
            ddtb_ref[...] = jnp.zeros_like(ddtb_ref)
            dalog_ref[...] = jnp.zeros_like(dalog_ref)

        q = _ssd_common(c_ref, pdt_ref, dtb_ref, alog_ref, e_ref, h)
        cp, sg, act, cs, a, dt, lo = q["cp"], q["sg"], q["act"], q["cs"], q["a"], q["dt"], q["lo"]
        ecs_l, dt_l = q["ecs_l"], q["dt_l"]
        ev = e_ref[...]
        xs = act[:, :d]
        xd = xs * dt_l
        row16 = lax.broadcasted_iota(jnp.int32, (CHUNK, h), 0)
        hid = lax.broadcasted_iota(jnp.int32, (1, h), 1)
        hid_t = lax.broadcasted_iota(jnp.int32, (h, 1), 0)

        zv = z_ref[...].astype(F32)
        sz = _sigmoid(zv)
        siluz = zv * sz
        y2v = y2_ref[...].astype(F32)
        y3 = y2v * siluz
        dysv = dys_ref[...].astype(F32)
        dy3s = []
        for gi in range(SSM_GROUPS):
            gsl = slice(gi * gw, (gi + 1) * gw)
            seg = y3[:, gsl]
            dseg, dgt = _rms_bwd(seg, _rms(seg), gs_ref[:, gsl], dysv[:, gsl])
            dy3s.append(dseg)
            dgs_ref[:, gsl] += jnp.sum(dgt, axis=0, keepdims=True)
        dy3 = jnp.concatenate(dy3s, axis=1)
        dy = dy3 * siluz
        dz_ref[...] = (dy3 * y2v * (sz * (1.0 + zv * (1.0 - sz)))).astype(BF16)
        ddsk_ref[...] += jnp.sum(_head_sums(dy * xs, ev), axis=0, keepdims=True)

        dcs = jnp.zeros((CHUNK, h), F32)
        dcs_t = jnp.zeros((h, CHUNK), F32)
        dxd_parts, yoff_parts, db_parts, dc_parts = [], [], [], []
        for g in range(SSM_GROUPS):
            bg = act[:, bc0 + g * D_STATE: bc0 + (g + 1) * D_STATE]
            cg = act[:, cc0 + g * D_STATE: cc0 + (g + 1) * D_STATE]
            bgb, cgb = bg.astype(BF16), cg.astype(BF16)
            cgb_t = cg.T.astype(BF16)
            s = _dot_nt(cgb, bgb)
            ds = jnp.zeros((CHUNK, CHUNK), F32)
            dbg = jnp.zeros((CHUNK, D_STATE), F32)
            dcg = jnp.zeros((CHUNK, D_STATE), F32)
            for jj in range(ppg):
                j = g * ppg + jj
                sl = slice(LANES * j, LANES * (j + 1))
                xdj = xd[:, sl]
                xdb = xdj.astype(BF16)
                x2 = jnp.concatenate([jnp.where(lo, xdj, 0.0), jnp.where(lo, 0.0, xdj)], axis=0).astype(BF16)
                dyj = dy[:, sl]
                dy2 = jnp.concatenate([jnp.where(lo, dyj, 0.0), jnp.where(lo, 0.0, dyj)], axis=0).astype(BF16)
                hpb = hp_ref[j]
                hprev = hpb.astype(F32)
                dhn = dh_ref[j]
                dhb = dhn.astype(BF16)
                dh2 = jnp.concatenate([jnp.where(lo, dhn, 0.0), jnp.where(lo, 0.0, dhn)], axis=0).astype(BF16)
                ecs = ecs_l[:, sl]
                gmat = (dyj * ecs).astype(BF16)
                yoff_parts.append(_dot(cgb, hpb) * ecs)
                dcg = dcg + _dot_nt(gmat, hpb)
                dh_ref[j] = dhn * ecs[CHUNK - 1:CHUNK] + _dot(cgb_t, gmat)
                t2 = dhn * hprev
                dbw2 = _dot_nt(x2, dhb)
                dm2 = _dot_nt(dy2, xdb)
                ms, bws = [], []
                for idx, hh in enumerate((2 * j, 2 * j + 1)):
                    msk = lo if idx == 0 else jnp.logical_not(lo)
                    onehot = (hid == hh).astype(F32)
                    csc = cs[:, hh:hh + 1]
                    seg = jnp.broadcast_to(csc, (CHUNK, CHUNK)) - jnp.broadcast_to(q["cs_t"][hh:hh + 1, :], (CHUNK, CHUNK))
                    lm = jnp.exp(jnp.where(q["tril"], seg, -jnp.inf))
                    m = s * lm
                    cs_last = cs[CHUNK - 1:CHUNK, hh:hh + 1]
                    dte = jnp.exp(cs_last - csc)
                    ms.append(m)
                    bws.append(bg * dte)
                    dbw = dbw2[idx * CHUNK:(idx + 1) * CHUNK]
                    dbg = dbg + dbw * dte
                    qv = jnp.sum(dbw * bg, axis=-1, keepdims=True) * dte
                    dm = dm2[idx * CHUNK:(idx + 1) * CHUNK]
                    wm = dm * m
                    rc = jnp.sum(wm, axis=-1, keepdims=True)
                    dcs_t = dcs_t - (hid_t == hh).astype(F32) * jnp.sum(wm, axis=0, keepdims=True)
                    ds = ds + dm * lm
                    ddec = jnp.sum(jnp.where(msk, t2, 0.0)) * jnp.exp(cs_last)
                    last = jnp.sum(qv) + ddec
                    dcs = dcs + (rc - qv) * onehot + jnp.where(row16 == CHUNK - 1, last * onehot, 0.0)
                dxd_s = _dot(jnp.concatenate(bws, axis=1).astype(BF16), dh2)
                dxd_d = _dot_tn(jnp.concatenate(ms, axis=0).astype(BF16), dy2)
                dxd_parts.append(dxd_s + dxd_d)
            dsb = ds.astype(BF16)
            dc_parts.append(dcg + _dot(dsb, bgb))
            db_parts.append(dbg + _dot_tn(dsb, cgb))
        yoff_all = jnp.concatenate(yoff_parts, axis=1)
        dxd_all = jnp.concatenate(dxd_parts, axis=1)
        dcs = dcs + _head_sums(dy * yoff_all, ev)
        triu = (q["col"] >= q["row"]).astype(F32)
        dadt = (jnp.dot(triu, dcs, precision=HIGHEST, preferred_element_type=F32)
                + lax.dot_general(triu, dcs_t, (((1,), (1,)), ((), ())), precision=HIGHEST, preferred_element_type=F32))
        ddt = dadt * a + _head_sums(dxd_all * xs, ev)
        dalog_ref[...] += jnp.sum(dadt * dt, axis=0, keepdims=True) * a
        dpre = ddt * _sigmoid(q["pre"])
        ddtb_ref[...] += jnp.sum(dpre, axis=0, keepdims=True)
        dpdt_ref[...] = jnp.zeros_like(dpdt_ref)
        dpdt_ref[:, 0:h] = dpre.astype(BF16)
        dxs = dxd_all * dt_l + dy * dsk_ref[...]
        dact = jnp.concatenate([dxs] + db_parts + dc_parts, axis=1)
        dconv_ref[...] = (dact * (sg * (1.0 + cp * (1.0 - sg)))).astype(BF16)

    rchunk = lambda w, cb=0: pl.BlockSpec((CHUNK, w), lambda b, c, cb=cb: (b * nc + nc - 1 - c, cb))
    vec = lambda w: pl.BlockSpec((1, w), lambda b, c: (0, 0))
    hp_spec = pl.BlockSpec((None, None, npair, D_STATE, LANES), lambda b, c: (b, nc - 1 - c, 0, 0, 0))
    return _call(body, name=name, grid=(nseq, nc),
                 in_specs=[rchunk(xbc), rchunk(LANES), rchunk(d, 3), rchunk(d), hp_spec, rchunk(d, 1),
                           vec(h), vec(h), vec(d), vec(d), pl.BlockSpec((h, d), lambda b, c: (0, 0)),
                           pl.BlockSpec(memory_space=pl.ANY)],
                 out_specs=[rchunk(xbc), rchunk(d, 3), rchunk(LANES), vec(d), vec(h), vec(h), vec(h)],
                 out_shape=[jax.ShapeDtypeStruct((t, xbc), BF16), jax.ShapeDtypeStruct(dproj.shape, BF16),
                            jax.ShapeDtypeStruct((t, LANES), BF16), jax.ShapeDtypeStruct((1, d), F32),
                            jax.ShapeDtypeStruct((1, h), F32), jax.ShapeDtypeStruct((1, h), F32),
                            jax.ShapeDtypeStruct((1, h), F32)],
                 input_output_aliases={11: 1},
                 scratch_shapes=[pltpu.VMEM((npair, D_STATE, LANES), F32)],
                 compiler_params=_params(("arbitrary", "arbitrary")))(
                     cpre, pdt, pz, y2, hprev_all, dcat, dtb, alog, dsk_lane, gs, emat, dproj)


def _sum_adamw(parts, w, m, v, *, name, layer=None, outs=None):
    n, r, c = parts.shape
    tr = _pick_rows(r, 256)
    bc1 = 1.0 - ADAM_B1 ** ADAM_STEP
    bc2 = 1.0 - ADAM_B2 ** ADAM_STEP

    def body(p_ref, w_ref, m_ref, v_ref, *rest):
        g_ref, d_ref, mo_ref, vo_ref = rest[-4:]
        g = p_ref[0].astype(F32)
        for k in range(1, n):
            g = g + p_ref[k].astype(F32)
        mn = ADAM_B1 * m_ref[...] + (1.0 - ADAM_B1) * g
        vn = ADAM_B2 * v_ref[...] + (1.0 - ADAM_B2) * (g * g)
        g_ref[...] = g
        mo_ref[...] = mn
        vo_ref[...] = vn
        d_ref[...] = -ADAM_LR * ((mn / bc1) / (jnp.sqrt(vn / bc2) + ADAM_EPS) + ADAM_WD * w_ref[...])

    p_spec = pl.BlockSpec((n, tr, c), lambda i: (0, i, 0))
    if layer is None:
        blk = pl.BlockSpec((tr, c), lambda i: (i, 0))
        return _call(body, name=name, grid=(r // tr,), in_specs=[p_spec, blk, blk, blk], out_specs=[blk] * 4,
                     out_shape=[jax.ShapeDtypeStruct((r, c), F32)] * 4,
                     compiler_params=_params(("parallel",)))(parts, w, m, v)
    blk = pl.BlockSpec((None, tr, c), lambda i: (layer, i, 0))
    if outs is None:
        outs = [lax.empty(w.shape, F32) for _ in range(4)]
    return _call(body, name=name, grid=(r // tr,),
                 in_specs=[p_spec, blk, blk, blk] + [pl.BlockSpec(memory_space=pl.ANY)] * 4, out_specs=[blk] * 4,
                 out_shape=[jax.ShapeDtypeStruct(w.shape, F32)] * 4, input_output_aliases={4 + k: k for k in range(4)},
                 compiler_params=_params(("parallel",)))(parts, w, m, v, *outs)


def _assemble_cols(blocks, *, name):
    nb, r, c = blocks.shape
    width = -(-nb * c // LANES) * LANES
    tr = _pick_rows(r, 256)

    def body(b_ref, o_ref):
        pieces = [b_ref[j] for j in range(nb)]
        if width > nb * c:
            pieces.append(jnp.zeros((tr, width - nb * c), blocks.dtype))
        o_ref[...] = jnp.concatenate(pieces, axis=1)

    return _call(body, name=name, grid=(r // tr,), in_specs=[pl.BlockSpec((nb, tr, c), lambda i: (0, i, 0))],
                 out_specs=pl.BlockSpec((tr, width), lambda i: (i, 0)), out_shape=jax.ShapeDtypeStruct((r, width), blocks.dtype),
                 compiler_params=_params(("parallel",)))(blocks)


def _split_cols(pieces, c, *, name):
    r = pieces[0].shape[0]
    tr = _pick_rows(r, 256)
    n_in = len(pieces)

    def body(*refs):
        o_ref = refs[n_in]
        x = jnp.concatenate([p[...] for p in refs[:n_in]], axis=1) if n_in > 1 else refs[0][...]
        for j in range(N_DEV):
            o_ref[j] = x[:, c * j:c * (j + 1)]

    return _call(body, name=name, grid=(r // tr,),
                 in_specs=[pl.BlockSpec((tr, p.shape[1]), lambda i: (i, 0)) for p in pieces],
                 out_specs=pl.BlockSpec((N_DEV, tr, c), lambda i: (0, i, 0)),
                 out_shape=jax.ShapeDtypeStruct((N_DEV, r, c), pieces[0].dtype),
                 compiler_params=_params(("parallel",)))(*pieces)


def _sum_parts(parts, *, name):
    n, r, c = parts.shape
    tr = _pick_rows(r, 256)

    def body(p_ref, g_ref):
        g = p_ref[0].astype(F32)
        for k in range(1, n):
            g = g + p_ref[k].astype(F32)
        g_ref[...] = g

    return _call(body, name=name, grid=(r // tr,), in_specs=[pl.BlockSpec((n, tr, c), lambda i: (0, i, 0))],
                 out_specs=pl.BlockSpec((tr, c), lambda i: (i, 0)), out_shape=jax.ShapeDtypeStruct((r, c), F32),
                 compiler_params=_params(("parallel",)))(parts)


def _peers():
    x, y, c = lax.axis_index("x"), lax.axis_index("y"), lax.axis_index("c")
    me = 4 * x + 2 * y + c
    out = []
    for k in range(1, N_DEV):
        px = (1 - x) if (k >> 2) & 1 else x
        py = (1 - y) if (k >> 1) & 1 else y
        pc = (1 - c) if k & 1 else c
        out.append(((px, py, pc), 4 * px + 2 * py + pc))
    return me, out


def _exchange(src, *, gather, name):
    shape = src.shape if gather else src.shape[1:]

    def body(s_ref, o_ref, send_sems, recv_sems, local_sem):
        me, peers = _peers()
        mine = pltpu.make_async_copy(s_ref if gather else s_ref.at[me], o_ref.at[me], local_sem)
        mine.start()
        sends = []
        for k, (dev, pid) in enumerate(peers):
            cp = pltpu.make_async_remote_copy(
                src_ref=s_ref if gather else s_ref.at[pid], dst_ref=o_ref.at[me],
                send_sem=send_sems.at[k], recv_sem=recv_sems.at[k], device_id=dev, device_id_type=MESH)
            cp.start()
            sends.append(cp)
        for k, (dev, pid) in enumerate(peers):
            pltpu.make_async_remote_copy(
                src_ref=s_ref if gather else s_ref.at[pid], dst_ref=o_ref.at[pid],
                send_sem=send_sems.at[k], recv_sem=recv_sems.at[k], device_id=dev, device_id_type=MESH).wait_recv()
        for cp in sends:
            cp.wait_send()
        mine.wait()

    any_spec = pl.BlockSpec(memory_space=pl.ANY)
    return _call(body, name=name, in_specs=[any_spec], out_specs=any_spec,
                 out_shape=jax.ShapeDtypeStruct((N_DEV,) + tuple(shape), src.dtype),
                 scratch_shapes=[pltpu.SemaphoreType.DMA((N_DEV - 1,)), pltpu.SemaphoreType.DMA((N_DEV - 1,)),
                                 pltpu.SemaphoreType.DMA(())])(src)


_HBM = pl.BlockSpec(memory_space=pltpu.HBM)
_SEM = pl.BlockSpec(memory_space=pltpu.SEMAPHORE)
_EFFECT = pltpu.SideEffectType.DATAFLOW_SIDE_EFFECTING


ALL_PEERS = tuple(range(1, N_DEV))
SAME_CORE_PEERS = (2, 4, 6)
SIBLING = 1


def _split_copies(s_refs, l_refs, send_sems, recv_sems, gather, incoming, ks=ALL_PEERS):
    me, peers = _peers()
    local, remote = [], []
    for ti, (s_ref, l_ref) in enumerate(zip(s_refs, l_refs)):
        base = ti * N_DEV
        local.append(pltpu.make_async_copy(s_ref if gather else s_ref.at[me], l_ref.at[me], recv_sems.at[base + N_DEV - 1]))
        for k, (dev, pid) in enumerate(peers):
            if k + 1 not in ks:
                continue
            sems = dict(send_sem=send_sems.at[base + k], recv_sem=recv_sems.at[base + k], device_id=dev, device_id_type=MESH)
            src = s_ref if gather else s_ref.at[pid]
            remote.append((
                pltpu.make_async_remote_copy(src_ref=src, dst_ref=l_ref.at[me], **sems),
                pltpu.make_async_remote_copy(src_ref=src, dst_ref=l_ref.at[pid], **sems) if incoming else None))
    return local, remote


def _exchange_start(srcs, *, gather, name, after=(), ks=ALL_PEERS):
    n = len(srcs)
    after = list(after)
    srcs = [pltpu.with_memory_space_constraint(s, pltpu.HBM) for s in srcs]
    lands = [pltpu.with_memory_space_constraint(
        lax.empty((N_DEV,) + tuple(s.shape if gather else s.shape[1:]), s.dtype), pltpu.HBM) for s in srcs]

    def body(*refs):
        s_refs, l_refs = refs[:n], refs[n:2 * n]
        outs = refs[2 * n + len(after):]
        send_sems, recv_sems, token = outs[0], outs[1], outs[-1]
        local, remote = _split_copies(s_refs, l_refs, send_sems, recv_sems, gather, incoming=False, ks=ks)
        for cp in local:
            cp.start()
        for out_cp, _ in remote:
            out_cp.start()
        token[...] = jnp.zeros_like(token)

    outs = _call(
        body, name=name,
        out_shape=(pltpu.SemaphoreType.DMA((n * N_DEV,)), pltpu.SemaphoreType.DMA((n * N_DEV,)),
                   *[pltpu.HBM(s.shape, s.dtype) for s in srcs], *[pltpu.HBM(l.shape, l.dtype) for l in lands],
                   jax.ShapeDtypeStruct((SUBLANES, LANES), F32)),
        in_specs=[_HBM] * (2 * n) + [pl.BlockSpec(memory_space=pl.ANY)] * len(after),
        out_specs=(_SEM, _SEM, *[_HBM] * (2 * n), pl.BlockSpec(memory_space=pltpu.VMEM)),
        input_output_aliases={k: k + 2 for k in range(2 * n)},
        compiler_params=pltpu.CompilerParams(has_side_effects=_EFFECT),
    )(*srcs, *lands, *after)
    return dict(n=n, gather=gather, ks=ks, sems=outs[:2], srcs=outs[2:2 + n], lands=outs[2 + n:2 + 2 * n]), outs[-1]


def _exchange_wait(state, after, *, name):
    n, gather, ks = state["n"], state["gather"], state["ks"]
    after = list(after)

    def body(*refs):
        s_refs, l_refs = refs[:n], refs[n:2 * n]
        send_sems, recv_sems = refs[2 * n], refs[2 * n + 1]
        local, remote = _split_copies(s_refs, l_refs, send_sems, recv_sems, gather, incoming=True, ks=ks)
        for out_cp, in_cp in remote:
            out_cp.wait_send()
            in_cp.wait_recv()
        for cp in local:
            cp.wait()

    outs = _call(
        body, name=name,
        out_shape=tuple(pltpu.HBM(a.shape, a.dtype) for a in (*state["srcs"], *state["lands"])),
        in_specs=[_HBM] * (2 * n) + [_SEM, _SEM] + [pl.BlockSpec(memory_space=pl.ANY)] * len(after),
        out_specs=tuple([_HBM] * (2 * n)),
        input_output_aliases={k: k for k in range(2 * n)},
        compiler_params=pltpu.CompilerParams(has_side_effects=_EFFECT),
    )(*state["srcs"], *state["lands"], *state["sems"], *after)
    return outs[n:]


def _sibling_copies(l_refs, send_sems, recv_sems, incoming):
    x, y, c = lax.axis_index("x"), lax.axis_index("y"), lax.axis_index("c")
    out = []
    for ti, l_ref in enumerate(l_refs):
        for q in range(4):
            px = (1 - x) if q & 2 else x
            py = (1 - y) if q & 1 else y
            mine, theirs = 4 * px + 2 * py + c, 4 * px + 2 * py + (1 - c)
            sems = dict(send_sem=send_sems.at[4 * ti + q], recv_sem=recv_sems.at[4 * ti + q],
                        device_id=(x, y, 1 - c), device_id_type=MESH)
            out.append((
                pltpu.make_async_remote_copy(src_ref=l_ref.at[mine], dst_ref=l_ref.at[mine], **sems),
                pltpu.make_async_remote_copy(src_ref=l_ref.at[mine], dst_ref=l_ref.at[theirs], **sems) if incoming else None))
    return out


def _sibling_start(lands, *, name, after=()):
    n = len(lands)
    after = list(after)
    lands = [pltpu.with_memory_space_constraint(l, pltpu.HBM) for l in lands]

    def body(*refs):
        l_refs = refs[:n]
        outs = refs[n + len(after):]
        for out_cp, _ in _sibling_copies(l_refs, outs[0], outs[1], incoming=False):
            out_cp.start()
        outs[-1][...] = jnp.zeros_like(outs[-1])

    outs = _call(
        body, name=name,
        out_shape=(pltpu.SemaphoreType.DMA((4 * n,)), pltpu.SemaphoreType.DMA((4 * n,)),
                   *[pltpu.HBM(l.shape, l.dtype) for l in lands], jax.ShapeDtypeStruct((SUBLANES, LANES), F32)),
        in_specs=[_HBM] * n + [pl.BlockSpec(memory_space=pl.ANY)] * len(after),
        out_specs=(_SEM, _SEM, *[_HBM] * n, pl.BlockSpec(memory_space=pltpu.VMEM)),
        input_output_aliases={k: k + 2 for k in range(n)},
        compiler_params=pltpu.CompilerParams(has_side_effects=_EFFECT),
    )(*lands, *after)
    return dict(n=n, sems=outs[:2], lands=outs[2:2 + n]), outs[-1]


def _sibling_wait(state, after, *, name):
    n = state["n"]
    after = list(after)

    def body(*refs):
        l_refs = refs[:n]
        for out_cp, in_cp in _sibling_copies(l_refs, refs[n], refs[n + 1], incoming=True):
            out_cp.wait_send()
            in_cp.wait_recv()

    return _call(
        body, name=name,
        out_shape=tuple(pltpu.HBM(a.shape, a.dtype) for a in state["lands"]),
        in_specs=[_HBM] * n + [_SEM, _SEM] + [pl.BlockSpec(memory_space=pl.ANY)] * len(after),
        out_specs=tuple([_HBM] * n), input_output_aliases={k: k for k in range(n)},
        compiler_params=pltpu.CompilerParams(has_side_effects=_EFFECT),
    )(*state["lands"], *state["sems"], *after)


def _pack(arrs):
    flat = jnp.concatenate([a.reshape(-1).astype(F32) for a in arrs])
    pad = (-flat.shape[0]) % (SUBLANES * LANES)
    return jnp.pad(flat, (0, pad)).reshape(-1, LANES)


def _unpack(packed, shapes):
    flat = packed.reshape(-1)
    out, off = [], 0
    for s in shapes:
        n = 1
        for v in s:
            n *= v
        out.append(flat[off:off + n].reshape(s))
        off += n
    return out


SMALL = ("norm_mix_pre", "ssm_conv_b", "dt_bias", "a_log", "d_skip", "conv_out_norm", "ssm_out_norm",
         "norm_mix_post", "norm_mlp_pre", "norm_mlp_post", "conv_a_w", "ssm_conv_w")
BIG = ("w_in", "w_out", "w_up", "w_down")
ORDER = ("norm_mix_pre", "w_in", "conv_a_w", "ssm_conv_w", "ssm_conv_b", "dt_bias", "a_log", "d_skip",
         "conv_out_norm", "ssm_out_norm", "w_out", "norm_mix_post", "norm_mlp_pre", "w_up", "w_down", "norm_mlp_post")


def kernel(x, norm_mix_pre, w_in, conv_a_w, ssm_conv_w, ssm_conv_b, dt_bias, a_log, d_skip, conv_out_norm, ssm_out_norm, w_out, norm_mix_post, norm_mlp_pre, w_up, w_down, norm_mlp_post, loss_target, m_norm_mix_pre, m_w_in, m_conv_a_w, m_ssm_conv_w, m_ssm_conv_b, m_dt_bias, m_a_log, m_d_skip, m_conv_out_norm, m_ssm_out_norm, m_w_out, m_norm_mix_post, m_norm_mlp_pre, m_w_up, m_w_down, m_norm_mlp_post, v_norm_mix_pre, v_w_in, v_conv_a_w, v_ssm_conv_w, v_ssm_conv_b, v_dt_bias, v_a_log, v_d_skip, v_conv_out_norm, v_ssm_out_norm, v_w_out, v_norm_mix_post, v_norm_mlp_pre, v_w_up, v_w_down, v_norm_mlp_post):
    W = dict(norm_mix_pre=norm_mix_pre, w_in=w_in, conv_a_w=conv_a_w, ssm_conv_w=ssm_conv_w, ssm_conv_b=ssm_conv_b,
             dt_bias=dt_bias, a_log=a_log, d_skip=d_skip, conv_out_norm=conv_out_norm, ssm_out_norm=ssm_out_norm,
             w_out=w_out, norm_mix_post=norm_mix_post, norm_mlp_pre=norm_mlp_pre, w_up=w_up, w_down=w_down,
             norm_mlp_post=norm_mlp_post)
    M = dict(norm_mix_pre=m_norm_mix_pre, w_in=m_w_in, conv_a_w=m_conv_a_w, ssm_conv_w=m_ssm_conv_w,
             ssm_conv_b=m_ssm_conv_b, dt_bias=m_dt_bias, a_log=m_a_log, d_skip=m_d_skip,
             conv_out_norm=m_conv_out_norm, ssm_out_norm=m_ssm_out_norm, w_out=m_w_out,
             norm_mix_post=m_norm_mix_post, norm_mlp_pre=m_norm_mlp_pre, w_up=m_w_up, w_down=m_w_down,
             norm_mlp_post=m_norm_mlp_post)
    V = dict(norm_mix_pre=v_norm_mix_pre, w_in=v_w_in, conv_a_w=v_conv_a_w, ssm_conv_w=v_ssm_conv_w,
             ssm_conv_b=v_ssm_conv_b, dt_bias=v_dt_bias, a_log=v_a_log, d_skip=v_d_skip,
             conv_out_norm=v_conv_out_norm, ssm_out_norm=v_ssm_out_norm, w_out=v_w_out,
             norm_mix_post=v_norm_mix_post, norm_mlp_pre=v_norm_mlp_pre, w_up=v_w_up, w_down=v_w_down,
             norm_mlp_post=v_norm_mlp_post)

    nseq, seq, d = x.shape
    t = nseq * seq
    depth = w_in.shape[0]
    h = d // HEAD_DIM
    xbc = d + 2 * SSM_GROUPS * D_STATE
    in_cols = w_in.shape[2] * N_DEV
    d_mix = w_out.shape[1] * N_DEV
    d_ff = w_up.shape[2] * N_DEV
    me = 4 * lax.axis_index("x") + 2 * lax.axis_index("y") + lax.axis_index("c")
    ca_shard = conv_a_w.shape[2]
    sc_shard = ssm_conv_w.shape[2]

    tap_shapes = [conv_a_w.shape[1:], ssm_conv_w.shape[1:]]

    def gather_start(i, after=()):
        ks = SAME_CORE_PEERS if i == 0 else ALL_PEERS
        st_in, tok_in = _exchange_start([w_in[i].astype(BF16), _pack([conv_a_w[i], ssm_conv_w[i]])], gather=True,
                                        name=f"gather_start_in_{i}", after=after, ks=ks)
        st_rest, tok_rest = _exchange_start([W[n][i].astype(BF16) for n in ("w_out", "w_up", "w_down")], gather=True,
                                            name=f"gather_start_rest_{i}", after=[tok_in], ks=ks)
        return st_in, st_rest, tok_rest

    vec = lambda name, i: W[name][i].reshape(1, -1)
    emat = (lax.broadcasted_iota(jnp.int32, (h, d), 1) // HEAD_DIM == lax.broadcasted_iota(jnp.int32, (h, d), 0)).astype(F32)

    xcur = x.reshape(t, d)
    hcur = _norm_fwd(xcur, vec("norm_mix_pre", 0), name="norm_first")
    saved = []
    nxt = gather_start(0)
    for i in range(depth):
        st_in, st_rest, tok = nxt
        landed = _exchange_wait(st_in, [hcur, tok], name=f"gather_wait_in_{i}")
        if i == 0:
            st_sib, tok_sib = _sibling_start(landed, name="gather_sibling_start_in_0")
            landed = _sibling_wait(st_sib, [tok_sib], name="gather_sibling_wait_in_0")
        win_g, taps_g = landed
        win = _assemble_cols(win_g, name=f"assemble_w_in_{i}")
        taps_j = [_unpack(taps_g[j], tap_shapes) for j in range(N_DEV)]
        conv_a_i = jnp.concatenate([tj[0] for tj in taps_j], axis=1)
        ssm_conv_i = jnp.concatenate([tj[1] for tj in taps_j], axis=1)
        proj = _mm(hcur, win, n=4 * d + xbc, name=f"fwd_proj_{i}", out_dtypes=(BF16,))
        pdt = _mm(hcur, win, n=LANES, b_off=4 * d + xbc, name=f"fwd_proj_dt_{i}")
        ya, va = _conva_fwd(proj, conv_a_i, vec("conv_out_norm", i), d=d, seq=seq, name=f"fwd_conv_a_{i}")
        cpre = _convb_fwd(proj, ssm_conv_i, vec("ssm_conv_b", i), col0=4 * d, seq=seq, name=f"fwd_conv_b_{i}")
        dsk_lane = jnp.repeat(W["d_skip"][i], HEAD_DIM).reshape(1, d)
        dtb = vec("dt_bias", i)
        if i == 0:
            st_sib, tok_sib = _sibling_start(_exchange_wait(st_rest, [cpre], name="gather_wait_rest_0"),
                                             name="gather_sibling_start_rest_0")
            dtb = dtb + tok_sib[0:1, 0:1]
        cat, y2, hprev = _ssd_fwd(cpre, pdt, proj, ya, dtb, vec("a_log", i), dsk_lane,
                                  vec("ssm_out_norm", i), emat, nseq=nseq, seq=seq, name=f"fwd_ssd_{i}")
        if i == 0:
            wout_g, wup_g, wdown_g = _sibling_wait(st_sib, [cat], name="gather_sibling_wait_rest_0")
        else:
            wout_g, wup_g, wdown_g = _exchange_wait(st_rest, [cat], name=f"gather_wait_rest_{i}")
        lw = dict(win=win, wout=wout_g.reshape(d_mix, d),
                  wup=_assemble_cols(wup_g, name=f"assemble_w_up_{i}"), wdown=wdown_g.reshape(d_ff, d),
                  conv_a=conv_a_i, ssm_conv=ssm_conv_i)
        after = []
        if i + 1 < depth:
            nxt = gather_start(i + 1, after=[wout_g])
            after = [nxt[2]]
        x1, h2, mix = _mm(cat, lw["wout"], name=f"fwd_out_{i}", after=after, out_dtypes=(F32, BF16, BF16),
                          epi=_epi_resid_norm, extras=(xcur,), vecs=(vec("norm_mix_post", i), vec("norm_mlp_pre", i)),
                          tm_cap=FUSED_ROWS)
        f = _mm(h2, lw["wup"], name=f"fwd_up_{i}", out_dtypes=(BF16,), epi=_epi_relu2)
        g_next = vec("norm_mix_pre", i + 1) if i + 1 < depth else vec("norm_mix_pre", 0)
        dn = _mm(f, lw["wdown"], name=f"fwd_down_{i}")
        x2, hnext = _resid_norm(x1, dn, vec("norm_mlp_post", i), g_next, name=f"fwd_post_mlp_{i}")
        saved.append(dict(lw=lw, x0=xcur, h=hcur, proj=proj, pdt=pdt, va=va, cpre=cpre, y2=y2,
                          hprev=hprev, cat=cat, mix=mix, x1=x1, h2=h2, f=f, dn=dn, dsk_lane=dsk_lane))
        xcur, hcur = x2, hnext

    dx, loss_part = _loss_fwd_bwd(xcur, loss_target.reshape(t, d), name="loss")
    loss = lax.psum(loss_part[0, 0], ("x", "y", "c"))

    small_grads = {n: [None] * depth for n in SMALL}
    big_out = {n: None for n in BIG}

    def finish(pending, after):
        li, st_a, st_b = pending

        def update(n, parts):
            big_out[n] = _sum_adamw(parts, W[n], M[n], V[n], layer=li, outs=big_out[n], name=f"adamw_{n}_{li}")

        p_down, p_up = _exchange_wait(st_a, after, name=f"scatter_wait_a_{li}")
        update("w_down", p_down)
        update("w_up", p_up)
        p_out, p_in = _exchange_wait(st_b, after + [big_out["w_up"][0]], name=f"scatter_wait_b_{li}")
        update("w_out", p_out)
        update("w_in", p_in)

    pending = None
    for i in reversed(range(depth)):
        s = saved[i]
        lw = s["lw"]
        ddn, dg = _bwd_norm_out(s["dn"], vec("norm_mlp_post", i), dx, name=f"bwd_norm_mlp_post_{i}")
        small_grads["norm_mlp_post"][i] = dg
        dup = _mm(ddn, lw["wdown"], tb=True, name=f"bwd_down_dx_{i}", out_dtypes=(BF16,), epi=_epi_drelu2,
                  extras=(s["f"],))
        g_wdown = _mm(s["f"], ddn, ta=True, name=f"bwd_down_dw_{i}", out_dtypes=(BF16,))
        dh2 = _mm(dup, lw["wup"], tb=True, name=f"bwd_up_dx_{i}", out_dtypes=(BF16,))
        g_wup = _mm(s["h2"], dup, ta=True, name=f"bwd_up_dw_{i}", out_dtypes=(BF16,))
        st_a, tok_a = _exchange_start(
            [g_wdown.reshape(N_DEV, d_ff // N_DEV, d), _split_cols([g_wup], d_ff // N_DEV, name=f"split_g_w_up_{i}")],
            gather=False, name=f"scatter_start_a_{i}")
        dx1, dmix, dg_pre, dg_post = _bwd_norm_pair(s["x1"], [dh2], dx, s["mix"], vec("norm_mlp_pre", i) + tok_a[0:1, 0:1],
                                                    vec("norm_mix_post", i), name=f"bwd_norm_mix_post_{i}")
        small_grads["norm_mlp_pre"][i] = dg_pre
        small_grads["norm_mix_post"][i] = dg_post
        dcat = _mm(dmix, lw["wout"], tb=True, name=f"bwd_out_dx_{i}", out_dtypes=(BF16,))
        g_wout = _mm(s["cat"], dmix, ta=True, name=f"bwd_out_dw_{i}", out_dtypes=(BF16,))
        dproj, dcaw, dgca = _conva_bwd(dcat, s["proj"], s["va"], lw["conv_a"], vec("conv_out_norm", i), d=d, seq=seq,
                                       name=f"bwd_conv_a_{i}")
        small_grads["conv_a_w"][i] = dcaw
        small_grads["conv_out_norm"][i] = dgca
        dconv, dproj, dpdt, dgs, ddsk, ddtb, dalog = _ssd_bwd(
            s["cpre"], s["pdt"], s["proj"], s["y2"], s["hprev"], dcat, vec("dt_bias", i), vec("a_log", i),
            s["dsk_lane"], vec("ssm_out_norm", i), emat, dproj, nseq=nseq, seq=seq, name=f"bwd_ssd_{i}")
        small_grads["ssm_out_norm"][i] = dgs
        small_grads["d_skip"][i] = ddsk
        small_grads["dt_bias"][i] = ddtb
        small_grads["a_log"][i] = dalog
        dproj, dscw, dscb = _convb_bwd(dconv, s["proj"], lw["ssm_conv"], dproj, col0=4 * d, seq=seq,
                                       name=f"bwd_conv_b_{i}")
        small_grads["ssm_conv_w"][i] = dscw
        small_grads["ssm_conv_b"][i] = dscb
        g_win = _split_cols([
            _mm(s["h"], dproj, ta=True, name=f"bwd_proj_dw_{i}", out_dtypes=(BF16,)),
            _mm(s["h"], dpdt, ta=True, name=f"bwd_proj_dt_dw_{i}", out_dtypes=(BF16,))],
            in_cols // N_DEV, name=f"split_g_w_in_{i}")
        st_b, tok_b = _exchange_start(
            [g_wout.reshape(N_DEV, d_mix // N_DEV, d), g_win], gather=False, name=f"scatter_start_b_{i}")
        dh_parts = [_mm(dp, lw["win"], tb=True, b_koff=off, name=f"bwd_proj_{nm}dx_{i}", after=[tok_b], out_dtypes=(BF16,))
                    for nm, dp, off in (("", dproj, 0), ("dt_", dpdt, 4 * d + xbc))]
        dx, dg_in = _bwd_norm_in(s["x0"], dh_parts, dx1, vec("norm_mix_pre", i), name=f"bwd_norm_mix_pre_{i}")
        small_grads["norm_mix_pre"][i] = dg_in
        if pending is not None:
            finish(pending, [dx])
        pending = (i, st_a, st_b)

    grad_x = dx.reshape(nseq, seq, d)

    small_shapes_full = {n: (depth,) + tuple(small_grads[n][0].shape) for n in SMALL}
    gpack = _pack([jnp.stack(small_grads[n]) for n in SMALL])
    st_small, tok_small = _exchange_start([gpack], gather=True, name="allreduce_small_start")
    finish(pending, [dx, tok_small])
    gparts, = _exchange_wait(st_small, [big_out["w_in"][0]], name="allreduce_small_wait")

    def shard_of(n, full):
        if n == "conv_a_w":
            return lax.dynamic_slice_in_dim(full, me * ca_shard, ca_shard, axis=2)
        if n == "ssm_conv_w":
            return lax.dynamic_slice_in_dim(full, me * sc_shard, sc_shard, axis=2)
        return full.reshape(W[n].shape)

    gsum = _sum_parts(gparts, name="sum_small")
    gfull = _unpack(gsum, [small_shapes_full[n] for n in SMALL])
    gsmall = {n: shard_of(n, gf) for n, gf in zip(SMALL, gfull)}
    res = _sum_adamw(_pack([gsmall[n] for n in SMALL])[None], _pack([W[n] for n in SMALL]),
                     _pack([M[n] for n in SMALL]), _pack([V[n] for n in SMALL]), name="adamw_small")
    small_out = [dict(zip(SMALL, _unpack(r, [W[n].shape for n in SMALL]))) for r in res]

    def out_of(kind, n):
        return big_out[n][kind] if n in BIG else small_out[kind][n]

    return (loss, grad_x, *[out_of(k, n) for k in range(4) for n in ORDER])
```

```python
import jax
import jax.numpy as jnp
from jax import lax
from jax.experimental import pallas as pl
from jax.experimental.pallas import tpu as pltpu

F32 = jnp.float32
BF16 = jnp.bfloat16
HIGHEST = lax.Precision.HIGHEST
MESH = pl.DeviceIdType.MESH

EPS = 1e-6
HEAD_DIM = 64
D_STATE = 128
SSM_GROUPS = 2
CHUNK = 128
CONV_K = 3
SSM_CONV_K = 4
ADAM_LR = 0.001
ADAM_B1 = 0.9
ADAM_B2 = 0.999
ADAM_EPS = 1e-08
ADAM_WD = 0.01
ADAM_STEP = 10

N_DEV = 8
LANES = 128
SUBLANES = 8
VMEM_LIMIT = 48 * 1024 * 1024
ROW_TILE = 512
MM_TILE = 1024
MM_TILE_N = 1536
FUSED_ROWS = 512


def _params(sem):
    return pltpu.CompilerParams(dimension_semantics=sem, vmem_limit_bytes=VMEM_LIMIT)


def _call(body, **kw):
    return pl.pallas_call(body, **kw)


def _pick(n, cap):
    best = None
    for t in range(LANES, min(n, cap) + 1, LANES):
        if n % t == 0:
            best = t
    return best or n


def _pick_rows(n, cap):
    best = None
    for t in range(SUBLANES, min(n, cap) + 1, SUBLANES):
        if n % t == 0:
            best = t
    return best or n


def _sigmoid(x):
    return 1.0 / (1.0 + jnp.exp(-x))


def _softplus(x):
    return jnp.maximum(x, 0.0) + jnp.log1p(jnp.exp(-jnp.abs(x)))


def _rms(x):
    return lax.rsqrt(jnp.mean(x * x, axis=-1, keepdims=True) + EPS)


def _rms_bwd(x, r, g, dy):
    gy = dy * g
    dx = r * gy - x * (r * r * r) * jnp.mean(gy * x, axis=-1, keepdims=True)
    return dx, dy * x * r


def _full(shape):
    return pl.BlockSpec(shape, lambda *_: (0,) * len(shape))


def _mm(a, b, *, name, ta=False, tb=False, out_dtypes=(F32,), epi=None, extras=(), n=None, b_off=0, b_koff=0,
        after=(), vecs=(), tm_cap=MM_TILE):
    m, k = (a.shape[1], a.shape[0]) if ta else a.shape
    if n is None:
        n = b.shape[0] if tb else b.shape[1]
    tm, tn, tk = _pick(m, tm_cap), _pick(n, MM_TILE_N), _pick(k, MM_TILE)
    while b_off % tn or n % tn:
        tn -= LANES
    if b_koff == 0 and k > MM_TILE:
        tk = _pick(k, MM_TILE_N)
    while b_koff % tk or k % tk:
        tk -= LANES
    nk = k // tk
    nm, nn = m // tm, n // tn
    jo = b_off // tn
    ko = b_koff // tk
    a_bytes = m * k * a.dtype.itemsize
    b_bytes = n * k * b.dtype.itemsize
    m_outer = a_bytes + nm * b_bytes <= b_bytes + nn * a_bytes
    ij = (lambda g0, g1: (g0, g1)) if m_outer else (lambda g0, g1: (g1, g0))
    grid = (nm, nn, nk) if m_outer else (nn, nm, nk)

    def a_map(g0, g1, kk):
        i, _ = ij(g0, g1)
        return (kk, i) if ta else (i, kk)

    def b_map(g0, g1, kk):
        _, j = ij(g0, g1)
        return (j + jo, kk + ko) if tb else (kk + ko, j + jo)

    def o_map(g0, g1, kk):
        return ij(g0, g1)

    a_spec = pl.BlockSpec((tk, tm) if ta else (tm, tk), a_map)
    b_spec = pl.BlockSpec((tn, tk) if tb else (tk, tn), b_map)
    o_spec = pl.BlockSpec((tm, tn), o_map)
    dims = (((0 if ta else 1,), (1 if tb else 0,)), ((), ()))
    n_ex = len(extras) + len(vecs)
    after = list(after)
    o0 = 2 + n_ex + len(after)

    def finish(acc, ex, outs):
        res = (acc,) if epi is None else epi(acc, *[e[...] for e in ex])
        for o, r in zip(outs, res):
            o[...] = r.astype(o.dtype)

    def body_single(*refs):
        a_ref, b_ref = refs[:2]
        acc = lax.dot_general(a_ref[...].astype(BF16), b_ref[...].astype(BF16), dims, preferred_element_type=F32)
        finish(acc, refs[2:2 + n_ex], refs[o0:])

    def body_multi(*refs):
        a_ref, b_ref = refs[:2]
        acc = refs[-1]
        kk = pl.program_id(2)

        @pl.when(kk == 0)
        def _():
            acc[...] = jnp.zeros_like(acc)

        acc[...] += lax.dot_general(a_ref[...].astype(BF16), b_ref[...].astype(BF16), dims, preferred_element_type=F32)

        @pl.when(kk == nk - 1)
        def _():
            finish(acc[...], refs[2:2 + n_ex], refs[o0:-1])

    v_spec = pl.BlockSpec((1, tn), lambda g0, g1, kk: (0, ij(g0, g1)[1]))
    outs = _call(
        body_single if nk == 1 else body_multi, name=name, grid=grid,
        in_specs=([a_spec, b_spec] + [o_spec] * len(extras) + [v_spec] * len(vecs)
                  + [pl.BlockSpec(memory_space=pl.ANY)] * len(after)),
        out_specs=[o_spec] * len(out_dtypes),
        out_shape=[jax.ShapeDtypeStruct((m, n), dt) for dt in out_dtypes],
        scratch_shapes=[] if nk == 1 else [pltpu.VMEM((tm, tn), F32)],
        compiler_params=_params(("parallel", "parallel", "arbitrary")),
    )(a, b, *extras, *vecs, *after)
    return outs[0] if len(outs) == 1 else outs


def _epi_resid_norm(acc, x, g_res, g_next):
    xn = x + acc * _rms(acc) * g_res
    return xn, xn * _rms(xn) * g_next, acc


def _epi_relu2(acc):
    r = jnp.maximum(acc, 0.0)
    return (r * r,)


def _epi_drelu2(acc, f):
    return (acc * (2.0 * jnp.sqrt(f.astype(F32))),)


def _norm_fwd(x, g, *, name):
    t, d = x.shape
    tt = _pick_rows(t, ROW_TILE)

    def body(x_ref, g_ref, h_ref):
        xv = x_ref[...]
        h_ref[...] = (xv * _rms(xv) * g_ref[...]).astype(BF16)

    row = pl.BlockSpec((tt, d), lambda i: (i, 0))
    return _call(body, name=name, grid=(t // tt,), in_specs=[row, _full((1, d))], out_specs=row,
                 out_shape=jax.ShapeDtypeStruct((t, d), BF16), compiler_params=_params(("parallel",)))(x, g)


def _resid_norm(x, n, g1, g2, *, name):
    t, d = x.shape
    tt = _pick_rows(t, ROW_TILE)

    def body(x_ref, n_ref, g1_ref, g2_ref, xo_ref, h_ref):
        nv = n_ref[...].astype(F32)
        xn = x_ref[...] + nv * _rms(nv) * g1_ref[...]
        xo_ref[...] = xn
        h_ref[...] = (xn * _rms(xn) * g2_ref[...]).astype(BF16)

    row = pl.BlockSpec((tt, d), lambda i: (i, 0))
    return _call(body, name=name, grid=(t // tt,), in_specs=[row, row, _full((1, d)), _full((1, d))],
                 out_specs=[row, row],
                 out_shape=[jax.ShapeDtypeStruct((t, d), F32), jax.ShapeDtypeStruct((t, d), BF16)],
                 compiler_params=_params(("parallel",)))(x, n, g1, g2)


def _loss_fwd_bwd(xf, target, *, name):
    t, d = xf.shape
    tt = _pick_rows(t, ROW_TILE)
    nt = t // tt

    def body(x_ref, t_ref, dy_ref, loss_ref, acc):
        i = pl.program_id(0)

        @pl.when(i == 0)
        def _():
            acc[...] = jnp.zeros_like(acc)

        e = x_ref[...] - t_ref[...]
        dy_ref[...] = e * (1.0 / d)
        acc[...] += jnp.sum(e * e, axis=0, keepdims=True)

        @pl.when(i == nt - 1)
        def _():
            loss_ref[...] = jnp.sum(acc[...], axis=-1, keepdims=True) * (0.5 / d)

    row = pl.BlockSpec((tt, d), lambda i: (i, 0))
    return _call(body, name=name, grid=(nt,), in_specs=[row, row], out_specs=[row, _full((1, 1))],
                 out_shape=[jax.ShapeDtypeStruct((t, d), F32), jax.ShapeDtypeStruct((1, 1), F32)],
                 scratch_shapes=[pltpu.VMEM((1, d), F32)], compiler_params=_params(("arbitrary",)))(xf, target)


def _bwd_norm_pair(xin, dh, dres, n, g_in, g_out, *, name):
    t, d = xin.shape
    tt = _pick_rows(t, ROW_TILE)
    n_dh = len(dh)

    def body(*refs):
        x_ref = refs[0]
        dh_refs = refs[1:1 + n_dh]
        dres_ref, n_ref, gi_ref, go_ref, dx_ref, dn_ref, dgi_ref, dgo_ref = refs[1 + n_dh:]
        i = pl.program_id(0)

        @pl.when(i == 0)
        def _():
            dgi_ref[...] = jnp.zeros_like(dgi_ref)
            dgo_ref[...] = jnp.zeros_like(dgo_ref)

        xv = x_ref[...]
        dhv = dh_refs[0][...].astype(F32)
        for r in dh_refs[1:]:
            dhv = dhv + r[...].astype(F32)
        dxh, dgi = _rms_bwd(xv, _rms(xv), gi_ref[...], dhv)
        dx = dres_ref[...] + dxh
        dx_ref[...] = dx
        dgi_ref[...] += jnp.sum(dgi, axis=0, keepdims=True)
        nv = n_ref[...].astype(F32)
        dn, dgo = _rms_bwd(nv, _rms(nv), go_ref[...], dx)
        dn_ref[...] = dn.astype(BF16)
        dgo_ref[...] += jnp.sum(dgo, axis=0, keepdims=True)

    row = pl.BlockSpec((tt, d), lambda i: (i, 0))
    vec = _full((1, d))
    return _call(body, name=name, grid=(t // tt,), in_specs=[row] * (n_dh + 3) + [vec, vec],
                 out_specs=[row, row, vec, vec],
                 out_shape=[jax.ShapeDtypeStruct((t, d), F32), jax.ShapeDtypeStruct((t, d), BF16),
                            jax.ShapeDtypeStruct((1, d), F32), jax.ShapeDtypeStruct((1, d), F32)],
                 compiler_params=_params(("arbitrary",)))(xin, *dh, dres, n, g_in, g_out)


def _bwd_norm_in(xin, dh, dres, g_in, *, name):
    t, d = xin.shape
    tt = _pick_rows(t, ROW_TILE)
    n_dh = len(dh)

    def body(*refs):
        x_ref = refs[0]
        dh_refs = refs[1:1 + n_dh]
        dres_ref, gi_ref, dx_ref, dgi_ref = refs[1 + n_dh:]
        i = pl.program_id(0)

        @pl.when(i == 0)
        def _():
            dgi_ref[...] = jnp.zeros_like(dgi_ref)

        xv = x_ref[...]
        dhv = dh_refs[0][...].astype(F32)
        for r in dh_refs[1:]:
            dhv = dhv + r[...].astype(F32)
        dxh, dgi = _rms_bwd(xv, _rms(xv), gi_ref[...], dhv)
        dx_ref[...] = dres_ref[...] + dxh
        dgi_ref[...] += jnp.sum(dgi, axis=0, keepdims=True)

    row = pl.BlockSpec((tt, d), lambda i: (i, 0))
    vec = _full((1, d))
    return _call(body, name=name, grid=(t // tt,), in_specs=[row] * (n_dh + 2) + [vec],
                 out_specs=[row, vec],
                 out_shape=[jax.ShapeDtypeStruct((t, d), F32), jax.ShapeDtypeStruct((1, d), F32)],
                 compiler_params=_params(("arbitrary",)))(xin, *dh, dres, g_in)


def _bwd_norm_out(n, g_out, dx, *, name):
    t, d = n.shape
    tt = _pick_rows(t, ROW_TILE)

    def body(n_ref, go_ref, dx_ref, dn_ref, dgo_ref):
        i = pl.program_id(0)

        @pl.when(i == 0)
        def _():
            dgo_ref[...] = jnp.zeros_like(dgo_ref)

        nv = n_ref[...].astype(F32)
        dn, dgo = _rms_bwd(nv, _rms(nv), go_ref[...], dx_ref[...])
        dn_ref[...] = dn.astype(BF16)
        dgo_ref[...] += jnp.sum(dgo, axis=0, keepdims=True)

    row = pl.BlockSpec((tt, d), lambda i: (i, 0))
    vec = _full((1, d))
    return _call(body, name=name, grid=(t // tt,), in_specs=[row, vec, row], out_specs=[row, vec],
                 out_shape=[jax.ShapeDtypeStruct((t, d), BF16), jax.ShapeDtypeStruct((1, d), F32)],
                 compiler_params=_params(("arbitrary",)))(n, g_out, dx)


def _shift_down(cur, halo, s):
    return jnp.concatenate([halo[SUBLANES - s:], cur[:cur.shape[0] - s]], axis=0)


def _shift_up(cur, halo, s):
    return jnp.concatenate([cur[s:], halo[:s]], axis=0)


def _conva_fwd(pa, w, g, *, d, seq, name):
    t = pa.shape[0]
    tt = _pick_rows(seq, ROW_TILE)
    tps = seq // tt

    def body(xa_ref, ca_ref, ba_ref, w_ref, g_ref, ya_ref, v_ref, carry):
        i = pl.program_id(0)

        @pl.when(i % tps == 0)
        def _():
            carry[...] = jnp.zeros_like(carry)

        u = ca_ref[...].astype(F32) * xa_ref[...].astype(F32)
        halo = carry[...]
        wv = w_ref[...]
        v = wv[2:3] * u + wv[1:2] * _shift_down(u, halo, 1) + wv[0:1] * _shift_down(u, halo, 2)
        carry[...] = u[tt - SUBLANES:]
        yp = ba_ref[...].astype(F32) * v
        ya_ref[...] = (yp * _rms(yp) * g_ref[...]).astype(BF16)
        v_ref[...] = v.astype(BF16)

    col = lambda c: pl.BlockSpec((tt, d), lambda i, c=c: (i, c))
    row = pl.BlockSpec((tt, d), lambda i: (i, 0))
    return _call(body, name=name, grid=(t // tt,),
                 in_specs=[col(0), col(1), col(2), _full((CONV_K, d)), _full((1, d))], out_specs=[row, row],
                 out_shape=[jax.ShapeDtypeStruct((t, d), BF16), jax.ShapeDtypeStruct((t, d), BF16)],
                 scratch_shapes=[pltpu.VMEM((SUBLANES, d), F32)],
                 compiler_params=_params(("arbitrary",)))(pa, pa, pa, w, g)


def _conva_bwd(dcat, pa, v, w, g, *, d, seq, name):
    t, width = pa.shape
    d3 = 3 * d
    tt = _pick_rows(seq, ROW_TILE)
    tps = seq // tt
    nt = t // tt

    def body(dya_ref, xa_ref, ca_ref, ba_ref, v_ref, w_ref, g_ref, dpa_ref, dw_ref, dg_ref, carry):
        i = pl.program_id(0)

        @pl.when(i == 0)
        def _():
            dw_ref[...] = jnp.zeros_like(dw_ref)
            dg_ref[...] = jnp.zeros_like(dg_ref)

        @pl.when(i % tps == 0)
        def _():
            carry[...] = jnp.zeros_like(carry)

        xa, ca, ba, vv = [r[...].astype(F32) for r in (xa_ref, ca_ref, ba_ref, v_ref)]
        yp = ba * vv
        dyp, dgt = _rms_bwd(yp, _rms(yp), g_ref[...], dya_ref[...].astype(F32))
        dg_ref[...] += jnp.sum(dgt, axis=0, keepdims=True)
        dv = dyp * ba
        halo = carry[...]
        dv1 = _shift_up(dv, halo, 1)
        dv2 = _shift_up(dv, halo, 2)
        carry[...] = dv[:SUBLANES]
        wv = w_ref[...]
        du = wv[2:3] * dv + wv[1:2] * dv1 + wv[0:1] * dv2
        u = ca * xa
        dw_ref[0:1, :] += jnp.sum(u * dv2, axis=0, keepdims=True)
        dw_ref[1:2, :] += jnp.sum(u * dv1, axis=0, keepdims=True)
        dw_ref[2:3, :] += jnp.sum(u * dv, axis=0, keepdims=True)
        dpa_ref[:, 0:d] = (du * ca).astype(BF16)
        dpa_ref[:, d:2 * d] = (du * xa).astype(BF16)
        dpa_ref[:, 2 * d:3 * d] = (dyp * vv).astype(BF16)

    rcol = lambda c: pl.BlockSpec((tt, d), lambda i, c=c: (nt - 1 - i, c))
    return _call(body, name=name, grid=(nt,),
                 in_specs=[rcol(0), rcol(0), rcol(1), rcol(2), rcol(0), _full((CONV_K, d)), _full((1, d))],
                 out_specs=[pl.BlockSpec((tt, d3), lambda i: (nt - 1 - i, 0)), _full((CONV_K, d)), _full((1, d))],
                 out_shape=[jax.ShapeDtypeStruct((t, width), BF16), jax.ShapeDtypeStruct((CONV_K, d), F32),
                            jax.ShapeDtypeStruct((1, d), F32)],
                 scratch_shapes=[pltpu.VMEM((SUBLANES, d), F32)],
                 compiler_params=_params(("arbitrary",)))(dcat, pa, pa, pa, v, w, g)


CONV_CH = 512


def _convb_fwd(proj, w, bias, *, col0, seq, name):
    t = proj.shape[0]
    c = w.shape[1]
    cb = _pick(c, CONV_CH)
    assert col0 % cb == 0
    tt = _pick_rows(seq, 2 * ROW_TILE)
    tps = seq // tt

    def body(p_ref, w_ref, b_ref, o_ref, carry):
        i = pl.program_id(1)

        @pl.when(i % tps == 0)
        def _():
            carry[...] = jnp.zeros_like(carry)

        p = p_ref[...].astype(F32)
        halo = carry[...]
        wv = w_ref[...]
        o = wv[3:4] * p + b_ref[...]
        for s in (1, 2, 3):
            o = o + wv[3 - s:4 - s] * _shift_down(p, halo, s)
        carry[...] = p[tt - SUBLANES:]
        o_ref[...] = o.astype(BF16)

    return _call(body, name=name, grid=(c // cb, t // tt),
                 in_specs=[pl.BlockSpec((tt, cb), lambda jc, i: (i, col0 // cb + jc)),
                           pl.BlockSpec((SSM_CONV_K, cb), lambda jc, i: (0, jc)), pl.BlockSpec((1, cb), lambda jc, i: (0, jc))],
                 out_specs=pl.BlockSpec((tt, cb), lambda jc, i: (i, jc)), out_shape=jax.ShapeDtypeStruct((t, c), BF16),
                 scratch_shapes=[pltpu.VMEM((SUBLANES, cb), F32)],
                 compiler_params=_params(("arbitrary", "arbitrary")))(proj, w, bias)


def _convb_bwd(dconv, proj, w, dproj, *, col0, seq, name):
    t, c = dconv.shape
    cb = _pick(c, CONV_CH)
    assert col0 % cb == 0
    tt = _pick_rows(seq, 2 * ROW_TILE)
    tps = seq // tt
    nt = t // tt

    def body(dc_ref, p_ref, w_ref, dproj_in, dp_ref, dw_ref, db_ref, carry):
        del dproj_in
        i = pl.program_id(1)

        @pl.when(i == 0)
        def _():
            dw_ref[...] = jnp.zeros_like(dw_ref)
            db_ref[...] = jnp.zeros_like(db_ref)

        @pl.when(i % tps == 0)
        def _():
            carry[...] = jnp.zeros_like(carry)

        dc = dc_ref[...].astype(F32)
        p = p_ref[...].astype(F32)
        halo = carry[...]
        wv = w_ref[...]
        dp = wv[3:4] * dc
        dw_ref[3:4, :] += jnp.sum(p * dc, axis=0, keepdims=True)
        for s in (1, 2, 3):
            dcs = _shift_up(dc, halo, s)
            dp = dp + wv[3 - s:4 - s] * dcs
            dw_ref[3 - s:4 - s, :] += jnp.sum(p * dcs, axis=0, keepdims=True)
        carry[...] = dc[:SUBLANES]
        db_ref[...] += jnp.sum(dc, axis=0, keepdims=True)
        dp_ref[...] = dp.astype(BF16)

    win_spec = pl.BlockSpec((tt, cb), lambda jc, i: (nt - 1 - i, col0 // cb + jc))
    taps = pl.BlockSpec((SSM_CONV_K, cb), lambda jc, i: (0, jc))
    return _call(body, name=name, grid=(c // cb, nt),
                 in_specs=[pl.BlockSpec((tt, cb), lambda jc, i: (nt - 1 - i, jc)), win_spec, taps,
                           pl.BlockSpec(memory_space=pl.ANY)],
                 out_specs=[win_spec, taps, pl.BlockSpec((1, cb), lambda jc, i: (0, jc))],
                 out_shape=[jax.ShapeDtypeStruct(dproj.shape, BF16), jax.ShapeDtypeStruct((SSM_CONV_K, c), F32),
                            jax.ShapeDtypeStruct((1, c), F32)],
                 input_output_aliases={3: 0},
                 scratch_shapes=[pltpu.VMEM((SUBLANES, cb), F32)],
                 compiler_params=_params(("arbitrary", "arbitrary")))(dconv, proj, w, dproj)


def _expand_heads(x, ev):
    return jnp.dot(x, ev, precision=HIGHEST, preferred_element_type=F32)


def _head_sums(v, ev):
    return lax.dot_general(v, ev, (((1,), (1,)), ((), ())), precision=HIGHEST, preferred_element_type=F32)


def _ssd_common(c_ref, pdt_ref, dtb_ref, alog_ref, e_ref, h):
    cp = c_ref[...].astype(F32)
    sg = _sigmoid(cp)
    act = cp * sg
    pre = pdt_ref[:, 0:h] + dtb_ref[...]
    dt = _softplus(pre)
    a = -jnp.exp(alog_ref[...])
    adt = dt * a
    row = lax.broadcasted_iota(jnp.int32, (CHUNK, CHUNK), 0)
    col = lax.broadcasted_iota(jnp.int32, (CHUNK, CHUNK), 1)
    tril = row >= col
    cs = jnp.dot(tril.astype(F32), adt, precision=HIGHEST, preferred_element_type=F32)
    cs_t = lax.dot_general(adt, (col >= row).astype(F32), (((0,), (0,)), ((), ())), precision=HIGHEST,
                           preferred_element_type=F32)
    ev = e_ref[...]
    dt_l = _expand_heads(dt, ev)
    ecs_l = jnp.exp(_expand_heads(cs, ev))
    return dict(cp=cp, sg=sg, act=act, pre=pre, dt=dt, a=a, cs=cs, cs_t=cs_t, dt_l=dt_l, ecs_l=ecs_l,
                tril=tril, row=row, col=col, lo=col < HEAD_DIM)


def _dot_nt(a, b):
    return lax.dot_general(a, b, (((1,), (1,)), ((), ())), preferred_element_type=F32)


def _dot_tn(a, b):
    return lax.dot_general(a, b, (((0,), (0,)), ((), ())), preferred_element_type=F32)


def _dot(a, b):
    return jnp.dot(a, b, preferred_element_type=F32)


def _ssd_fwd(cpre, pdt, pz, ya, dtb, alog, dsk_lane, gs, emat, *, nseq, seq, name):
    t, xbc = cpre.shape
    d = ya.shape[1]
    h = d // HEAD_DIM
    npair = h // 2
    ppg = npair // SSM_GROUPS
    nc = seq // CHUNK
    gw = d // SSM_GROUPS
    bc0 = d
    cc0 = d + SSM_GROUPS * D_STATE

    def body(c_ref, pdt_ref, z_ref, ya_ref, dtb_ref, alog_ref, dsk_ref, gs_ref, e_ref, cat_ref, y2_ref, hp_ref, h_ref):
        @pl.when(pl.program_id(0) == 0)
        def _():
            h_ref[...] = jnp.zeros_like(h_ref)

        for sq in range(nseq):
            one_seq(c_ref.at[sq], pdt_ref.at[sq], z_ref.at[sq], ya_ref.at[sq], dtb_ref, alog_ref, dsk_ref, gs_ref, e_ref,
                    cat_ref.at[sq], y2_ref.at[sq], hp_ref.at[sq], h_ref.at[sq])

    def one_seq(c_ref, pdt_ref, z_ref, ya_ref, dtb_ref, alog_ref, dsk_ref, gs_ref, e_ref, cat_ref, y2_ref, hp_ref, h_ref):
        q = _ssd_common(c_ref, pdt_ref, dtb_ref, alog_ref, e_ref, h)
        act, cs, lo, ecs_l = q["act"], q["cs"], q["lo"], q["ecs_l"]
        xs = act[:, :d]
        xd = xs * q["dt_l"]
        ys = []
        for g in range(SSM_GROUPS):
            bg = act[:, bc0 + g * D_STATE: bc0 + (g + 1) * D_STATE]
            cgb = act[:, cc0 + g * D_STATE: cc0 + (g + 1) * D_STATE].astype(BF16)
            s = _dot_nt(cgb, bg.astype(BF16))
            bg_t = bg.T
            for jj in range(ppg):
                j = g * ppg + jj
                sl = slice(LANES * j, LANES * (j + 1))
                xdj = xd[:, sl]
                x2 = jnp.concatenate([jnp.where(lo, xdj, 0.0), jnp.where(lo, 0.0, xdj)], axis=0).astype(BF16)
                hprev = h_ref[j]
                hp_ref[j] = hprev.astype(BF16)
                ms, bws_t = [], []
                for hh in (2 * j, 2 * j + 1):
                    csc = cs[:, hh:hh + 1]
                    cs_row = q["cs_t"][hh:hh + 1, :]
                    seg = jnp.broadcast_to(csc, (CHUNK, CHUNK)) - jnp.broadcast_to(cs_row, (CHUNK, CHUNK))
                    ms.append(s * jnp.exp(jnp.where(q["tril"], seg, -jnp.inf)))
                    bws_t.append(bg_t * jnp.exp(cs_row[:, CHUNK - 1:CHUNK] - cs_row))
                ydiag = _dot(jnp.concatenate(ms, axis=1).astype(BF16), x2)
                st = _dot(jnp.concatenate(bws_t, axis=1).astype(BF16), x2)
                ecs = ecs_l[:, sl]
                yoff = _dot(cgb, hprev.astype(BF16)) * ecs
                h_ref[j] = hprev * ecs[CHUNK - 1:CHUNK] + st
                ys.append(ydiag + yoff)
        y = jnp.concatenate(ys, axis=1) + dsk_ref[...] * xs
        y2_ref[...] = y.astype(BF16)
        zv = z_ref[...].astype(F32)
        y3 = y * (zv * _sigmoid(zv))
        cat_ref[:, 0:d] = ya_ref[...]
        for gi in range(SSM_GROUPS):
            seg = y3[:, gi * gw:(gi + 1) * gw]
            cat_ref[:, d + gi * gw:d + (gi + 1) * gw] = (seg * _rms(seg) * gs_ref[:, gi * gw:(gi + 1) * gw]).astype(BF16)

    chunk = lambda w, cb=0: pl.BlockSpec((nseq, CHUNK, w), lambda c, cb=cb: (0, c, cb))
    vec = lambda w: pl.BlockSpec((1, w), lambda c: (0, 0))
    hp_spec = pl.BlockSpec((nseq, None, npair, D_STATE, LANES), lambda c: (0, c, 0, 0, 0))
    per_seq = lambda a: a.reshape(nseq, seq, a.shape[1])
    cat, y2, hp = _call(
        body, name=name, grid=(nc,),
        in_specs=[chunk(xbc), chunk(LANES), chunk(d, 3), chunk(d), vec(h), vec(h), vec(d), vec(d),
                  pl.BlockSpec((h, d), lambda c: (0, 0))],
        out_specs=[chunk(2 * d), chunk(d), hp_spec],
        out_shape=[jax.ShapeDtypeStruct((nseq, seq, 2 * d), BF16), jax.ShapeDtypeStruct((nseq, seq, d), BF16),
                   jax.ShapeDtypeStruct((nseq, nc, npair, D_STATE, LANES), BF16)],
        scratch_shapes=[pltpu.VMEM((nseq, npair, D_STATE, LANES), F32)],
        compiler_params=_params(("arbitrary",)))(
            per_seq(cpre), per_seq(pdt), per_seq(pz), per_seq(ya), dtb, alog, dsk_lane, gs, emat)
    return cat.reshape(t, 2 * d), y2.reshape(t, d), hp


def _ssd_bwd(cpre, pdt, pz, y2, hprev_all, dcat, dtb, alog, dsk_lane, gs, emat, dproj, *, nseq, seq, name):
    t, xbc = cpre.shape
    d = y2.shape[1]
    h = d // HEAD_DIM
    npair = h // 2
    ppg = npair // SSM_GROUPS
    nc = seq // CHUNK
    gw = d // SSM_GROUPS
    bc0 = d
    cc0 = d + SSM_GROUPS * D_STATE

    def body(c_ref, pdt_ref, z_ref, y2_ref, hp_ref, dys_ref, dtb_ref, alog_ref, dsk_ref, gs_ref, e_ref, dproj_in,
             dconv_ref, dz_ref, dpdt_ref, dgs_ref, ddsk_ref, ddtb_ref, dalog_ref, dh_ref):
        del dproj_in
        b = pl.program_id(0)
        c = pl.program_id(1)

        @pl.when(c == 0)
        def _():
            dh_ref[...] = jnp.zeros_like(dh_ref)

        @pl.when((b == 0) & (c == 0))
        def _():
            dgs_ref[...] = jnp.zeros_like(dgs_ref)
            ddsk_ref[...] = jnp.zeros_like(ddsk_ref)
            ddtb_ref[...] = jnp.zeros_like(ddtb_ref)
            dalog_ref[...] = jnp.zeros_like(dalog_ref)

        q = _ssd_common(c_ref, pdt_ref, dtb_ref, alog_ref, e_ref, h)
        cp, sg, act, cs, a, dt, lo = q["cp"], q["sg"], q["act"], q["cs"], q["a"], q["dt"], q["lo"]
        ecs_l, dt_l = q["ecs_l"], q["dt_l"]
        ev = e_ref[...]
        xs = act[:, :d]
        xd = xs * dt_l
        row16 = lax.broadcasted_iota(jnp.int32, (CHUNK, h), 0)
        hid = lax.broadcasted_iota(jnp.int32, (1, h), 1)
        hid_t = lax.broadcasted_iota(jnp.int32, (h, 1), 0)

        zv = z_ref[...].astype(F32)
        sz = _sigmoid(zv)
        siluz = zv * sz
        y2v = y2_ref[...].astype(F32)
        y3 = y2v * siluz
        dysv = dys_ref[...].astype(F32)
        dy3s = []
        for gi in range(SSM_GROUPS):
            gsl = slice(gi * gw, (gi + 1) * gw)
            seg = y3[:, gsl]
            dseg, dgt = _rms_bwd(seg, _rms(seg), gs_ref[:, gsl], dysv[:, gsl])
            dy3s.append(dseg)
            dgs_ref[:, gsl] += jnp.sum(dgt, axis=0, keepdims=True)
        dy3 = jnp.concatenate(dy3s, axis=1)
        dy = dy3 * siluz
        dz_ref[...] = (dy3 * y2v * (sz * (1.0 + zv * (1.0 - sz)))).astype(BF16)
        ddsk_ref[...] += jnp.sum(_head_sums(dy * xs, ev), axis=0, keepdims=True)

        dcs = jnp.zeros((CHUNK, h), F32)
        dcs_t = jnp.zeros((h, CHUNK), F32)
        dxd_parts, yoff_parts, db_parts, dc_parts = [], [], [], []
        for g in range(SSM_GROUPS):
            bg = act[:, bc0 + g * D_STATE: bc0 + (g + 1) * D_STATE]
            cg = act[:, cc0 + g * D_STATE: cc0 + (g + 1) * D_STATE]
            bgb, cgb = bg.astype(BF16), cg.astype(BF16)
            cgb_t = cg.T.astype(BF16)
            s = _dot_nt(cgb, bgb)
            ds = jnp.zeros((CHUNK, CHUNK), F32)
            dbg = jnp.zeros((CHUNK, D_STATE), F32)
            dcg = jnp.zeros((CHUNK, D_STATE), F32)
            for jj in range(ppg):
                j = g * ppg + jj
                sl = slice(LANES * j, LANES * (j + 1))
                xdj = xd[:, sl]
                xdb = xdj.astype(BF16)
                x2 = jnp.concatenate([jnp.where(lo, xdj, 0.0), jnp.where(lo, 0.0, xdj)], axis=0).astype(BF16)
                dyj = dy[:, sl]
                dy2 = jnp.concatenate([jnp.where(lo, dyj, 0.0), jnp.where(lo, 0.0, dyj)], axis=0).astype(BF16)
                hpb = hp_ref[j]
                hprev = hpb.astype(F32)
                dhn = dh_ref[j]
                dhb = dhn.astype(BF16)
                dh2 = jnp.concatenate([jnp.where(lo, dhn, 0.0), jnp.where(lo, 0.0, dhn)], axis=0).astype(BF16)
                ecs = ecs_l[:, sl]
                gmat = (dyj * ecs).astype(BF16)
                yoff_parts.append(_dot(cgb, hpb) * ecs)
                dcg = dcg + _dot_nt(gmat, hpb)
                dh_ref[j] = dhn * ecs[CHUNK - 1:CHUNK] + _dot(cgb_t, gmat)
                t2 = dhn * hprev
                dbw2 = _dot_nt(x2, dhb)
                dm2 = _dot_nt(dy2, xdb)
                ms, bws = [], []
                for idx, hh in enumerate((2 * j, 2 * j + 1)):
                    msk = lo if idx == 0 else jnp.logical_not(lo)
                    onehot = (hid == hh).astype(F32)
                    csc = cs[:, hh:hh + 1]
                    seg = jnp.broadcast_to(csc, (CHUNK, CHUNK)) - jnp.broadcast_to(q["cs_t"][hh:hh + 1, :], (CHUNK, CHUNK))
                    lm = jnp.exp(jnp.where(q["tril"], seg, -jnp.inf))
                    m = s * lm
                    cs_last = cs[CHUNK - 1:CHUNK, hh:hh + 1]
                    dte = jnp.exp(cs_last - csc)
                    ms.append(m)
                    bws.append(bg * dte)
                    dbw = dbw2[idx * CHUNK:(idx + 1) * CHUNK]
                    dbg = dbg + dbw * dte
                    qv = jnp.sum(dbw * bg, axis=-1, keepdims=True) * dte
                    dm = dm2[idx * CHUNK:(idx + 1) * CHUNK]
                    wm = dm * m
                    rc = jnp.sum(wm, axis=-1, keepdims=True)
                    dcs_t = dcs_t - (hid_t == hh).astype(F32) * jnp.sum(wm, axis=0, keepdims=True)
                    ds = ds + dm * lm
                    ddec = jnp.sum(jnp.where(msk, t2, 0.0)) * jnp.exp(cs_last)
                    last = jnp.sum(qv) + ddec
                    dcs = dcs + (rc - qv) * onehot + jnp.where(row16 == CHUNK - 1, last * onehot, 0.0)
                dxd_s = _dot(jnp.concatenate(bws, axis=1).astype(BF16), dh2)
                dxd_d = _dot_tn(jnp.concatenate(ms, axis=0).astype(BF16), dy2)
                dxd_parts.append(dxd_s + dxd_d)
            dsb = ds.astype(BF16)
            dc_parts.append(dcg + _dot(dsb, bgb))
            db_parts.append(dbg + _dot_tn(dsb, cgb))
        yoff_all = jnp.concatenate(yoff_parts, axis=1)
        dxd_all = jnp.concatenate(dxd_parts, axis=1)
        dcs = dcs + _head_sums(dy * yoff_all, ev)
        triu = (q["col"] >= q["row"]).astype(F32)
        dadt = (jnp.dot(triu, dcs, precision=HIGHEST, preferred_element_type=F32)
                + lax.dot_general(triu, dcs_t, (((1,), (1,)), ((), ())), precision=HIGHEST, preferred_element_type=F32))
        ddt = dadt * a + _head_sums(dxd_all * xs, ev)
        dalog_ref[...] += jnp.sum(dadt * dt, axis=0, keepdims=True) * a
        dpre = ddt * _sigmoid(q["pre"])
        ddtb_ref[...] += jnp.sum(dpre, axis=0, keepdims=True)
        dpdt_ref[...] = jnp.zeros_like(dpdt_ref)
        dpdt_ref[:, 0:h] = dpre.astype(BF16)
        dxs = dxd_all * dt_l + dy * dsk_ref[...]
        dact = jnp.concatenate([dxs] + db_parts + dc_parts, axis=1)
        dconv_ref[...] = (dact * (sg * (1.0 + cp * (1.0 - sg)))).astype(BF16)

    rchunk = lambda w, cb=0: pl.BlockSpec((CHUNK, w), lambda b, c, cb=cb: (b * nc + nc - 1 - c, cb))
    vec = lambda w: pl.BlockSpec((1, w), lambda b, c: (0, 0))
    hp_spec = pl.BlockSpec((None, None, npair, D_STATE, LANES), lambda b, c: (b, nc - 1 - c, 0, 0, 0))
    return _call(body, name=name, grid=(nseq, nc),
                 in_specs=[rchunk(xbc), rchunk(LANES), rchunk(d, 3), rchunk(d), hp_spec, rchunk(d, 1),
                           vec(h), vec(h), vec(d), vec(d), pl.BlockSpec((h, d), lambda b, c: (0, 0)),
                           pl.BlockSpec(memory_space=pl.ANY)],
                 out_specs=[rchunk(xbc), rchunk(d, 3), rchunk(LANES), vec(d), vec(h), vec(h), vec(h)],
                 out_shape=[jax.ShapeDtypeStruct((t, xbc), BF16), jax.ShapeDtypeStruct(dproj.shape, BF16),
                            jax.ShapeDtypeStruct((t, LANES), BF16), jax.ShapeDtypeStruct((1, d), F32),
                            jax.ShapeDtypeStruct((1, h), F32), jax.ShapeDtypeStruct((1, h), F32),
                            jax.ShapeDtypeStruct((1, h), F32)],
                 input_output_aliases={11: 1},
                 scratch_shapes=[pltpu.VMEM((npair, D_STATE, LANES), F32)],
                 compiler_params=_params(("arbitrary", "arbitrary")))(
                     cpre, pdt, pz, y2, hprev_all, dcat, dtb, alog, dsk_lane, gs, emat, dproj)


def _sum_adamw(parts, w, m, v, *, name, layer=None, outs=None):
    n, r, c = parts.shape
    tr = _pick_rows(r, 256)
    bc1 = 1.0 - ADAM_B1 ** ADAM_STEP
    bc2 = 1.0 - ADAM_B2 ** ADAM_STEP

    def body(p_ref, w_ref, m_ref, v_ref, *rest):
        g_ref, d_ref, mo_ref, vo_ref = rest[-4:]
        g = p_ref[0].astype(F32)
        for k in range(1, n):
            g = g + p_ref[k].astype(F32)
        mn = ADAM_B1 * m_ref[...] + (1.0 - ADAM_B1) * g
        vn = ADAM_B2 * v_ref[...] + (1.0 - ADAM_B2) * (g * g)
        g_ref[...] = g
        mo_ref[...] = mn
        vo_ref[...] = vn
        d_ref[...] = -ADAM_LR * ((mn / bc1) / (jnp.sqrt(vn / bc2) + ADAM_EPS) + ADAM_WD * w_ref[...])

    p_spec = pl.BlockSpec((n, tr, c), lambda i: (0, i, 0))
    if layer is None:
        blk = pl.BlockSpec((tr, c), lambda i: (i, 0))
        return _call(body, name=name, grid=(r // tr,), in_specs=[p_spec, blk, blk, blk], out_specs=[blk] * 4,
                     out_shape=[jax.ShapeDtypeStruct((r, c), F32)] * 4,
                     compiler_params=_params(("parallel",)))(parts, w, m, v)
    blk = pl.BlockSpec((None, tr, c), lambda i: (layer, i, 0))
    if outs is None:
        outs = [lax.empty(w.shape, F32) for _ in range(4)]
    return _call(body, name=name, grid=(r // tr,),
                 in_specs=[p_spec, blk, blk, blk] + [pl.BlockSpec(memory_space=pl.ANY)] * 4, out_specs=[blk] * 4,
                 out_shape=[jax.ShapeDtypeStruct(w.shape, F32)] * 4, input_output_aliases={4 + k: k for k in range(4)},
                 compiler_params=_params(("parallel",)))(parts, w, m, v, *outs)


def _assemble_cols(blocks, *, name):
    nb, r, c = blocks.shape
    width = -(-nb * c // LANES) * LANES
    tr = _pick_rows(r, 256)

    def body(b_ref, o_ref):
        pieces = [b_ref[j] for j in range(nb)]
        if width > nb * c:
            pieces.append(jnp.zeros((tr, width - nb * c), blocks.dtype))
        o_ref[...] = jnp.concatenate(pieces, axis=1)

    return _call(body, name=name, grid=(r // tr,), in_specs=[pl.BlockSpec((nb, tr, c), lambda i: (0, i, 0))],
                 out_specs=pl.BlockSpec((tr, width), lambda i: (i, 0)), out_shape=jax.ShapeDtypeStruct((r, width), blocks.dtype),
                 compiler_params=_params(("parallel",)))(blocks)


def _split_cols(pieces, c, *, name):
    r = pieces[0].shape[0]
    tr = _pick_rows(r, 256)
    n_in = len(pieces)

    def body(*refs):
        o_ref = refs[n_in]
        x = jnp.concatenate([p[...] for p in refs[:n_in]], axis=1) if n_in > 1 else refs[0][...]
        for j in range(N_DEV):
            o_ref[j] = x[:, c * j:c * (j + 1)]

    return _call(body, name=name, grid=(r // tr,),
                 in_specs=[pl.BlockSpec((tr, p.shape[1]), lambda i: (i, 0)) for p in pieces],
                 out_specs=pl.BlockSpec((N_DEV, tr, c), lambda i: (0, i, 0)),
                 out_shape=jax.ShapeDtypeStruct((N_DEV, r, c), pieces[0].dtype),
                 compiler_params=_params(("parallel",)))(*pieces)


def _sum_parts(parts, *, name):
    n, r, c = parts.shape
    tr = _pick_rows(r, 256)

    def body(p_ref, g_ref):
        g = p_ref[0].astype(F32)
        for k in range(1, n):
            g = g + p_ref[k].astype(F32)
        g_ref[...] = g

    return _call(body, name=name, grid=(r // tr,), in_specs=[pl.BlockSpec((n, tr, c), lambda i: (0, i, 0))],
                 out_specs=pl.BlockSpec((tr, c), lambda i: (i, 0)), out_shape=jax.ShapeDtypeStruct((r, c), F32),
                 compiler_params=_params(("parallel",)))(parts)


def _peers():
    x, y, c = lax.axis_index("x"), lax.axis_index("y"), lax.axis_index("c")
    me = 4 * x + 2 * y + c
    out = []
    for k in range(1, N_DEV):
        px = (1 - x) if (k >> 2) & 1 else x
        py = (1 - y) if (k >> 1) & 1 else y
        pc = (1 - c) if k & 1 else c
        out.append(((px, py, pc), 4 * px + 2 * py + pc))
    return me, out


_HBM = pl.BlockSpec(memory_space=pltpu.HBM)
_SEM = pl.BlockSpec(memory_space=pltpu.SEMAPHORE)
_EFFECT = pltpu.SideEffectType.DATAFLOW_SIDE_EFFECTING


ALL_PEERS = tuple(range(1, N_DEV))
SAME_CORE_PEERS = (2, 4, 6)


def _split_copies(s_refs, l_refs, send_sems, recv_sems, gather, incoming, ks=ALL_PEERS):
    me, peers = _peers()
    local, remote = [], []
    for ti, (s_ref, l_ref) in enumerate(zip(s_refs, l_refs)):
        base = ti * N_DEV
        local.append(pltpu.make_async_copy(s_ref if gather else s_ref.at[me], l_ref.at[me], recv_sems.at[base + N_DEV - 1]))
        for k, (dev, pid) in enumerate(peers):
            if k + 1 not in ks:
                continue
            sems = dict(send_sem=send_sems.at[base + k], recv_sem=recv_sems.at[base + k], device_id=dev, device_id_type=MESH)
            src = s_ref if gather else s_ref.at[pid]
            remote.append((
                pltpu.make_async_remote_copy(src_ref=src, dst_ref=l_ref.at[me], **sems),
                pltpu.make_async_remote_copy(src_ref=src, dst_ref=l_ref.at[pid], **sems) if incoming else None))
    return local, remote


def _exchange_start(srcs, *, gather, name, after=(), ks=ALL_PEERS):
    n = len(srcs)
    after = list(after)
    srcs = [pltpu.with_memory_space_constraint(s, pltpu.HBM) for s in srcs]
    lands = [pltpu.with_memory_space_constraint(
        lax.empty((N_DEV,) + tuple(s.shape if gather else s.shape[1:]), s.dtype), pltpu.HBM) for s in srcs]

    def body(*refs):
        s_refs, l_refs = refs[:n], refs[n:2 * n]
        outs = refs[2 * n + len(after):]
        send_sems, recv_sems, token = outs[0], outs[1], outs[-1]
        local, remote = _split_copies(s_refs, l_refs, send_sems, recv_sems, gather, incoming=False, ks=ks)
        for cp in local:
            cp.start()
        for out_cp, _ in remote:
            out_cp.start()
        token[...] = jnp.zeros_like(token)

    outs = _call(
        body, name=name,
        out_shape=(pltpu.SemaphoreType.DMA((n * N_DEV,)), pltpu.SemaphoreType.DMA((n * N_DEV,)),
                   *[pltpu.HBM(s.shape, s.dtype) for s in srcs], *[pltpu.HBM(l.shape, l.dtype) for l in lands],
                   jax.ShapeDtypeStruct((SUBLANES, LANES), F32)),
        in_specs=[_HBM] * (2 * n) + [pl.BlockSpec(memory_space=pl.ANY)] * len(after),
        out_specs=(_SEM, _SEM, *[_HBM] * (2 * n), pl.BlockSpec(memory_space=pltpu.VMEM)),
        input_output_aliases={k: k + 2 for k in range(2 * n)},
        compiler_params=pltpu.CompilerParams(has_side_effects=_EFFECT),
    )(*srcs, *lands, *after)
    return dict(n=n, gather=gather, ks=ks, sems=outs[:2], srcs=outs[2:2 + n], lands=outs[2 + n:2 + 2 * n]), outs[-1]


def _exchange_wait(state, after, *, name):
    n, gather, ks = state["n"], state["gather"], state["ks"]
    after = list(after)

    def body(*refs):
        s_refs, l_refs = refs[:n], refs[n:2 * n]
        send_sems, recv_sems = refs[2 * n], refs[2 * n + 1]
        local, remote = _split_copies(s_refs, l_refs, send_sems, recv_sems, gather, incoming=True, ks=ks)
        for out_cp, in_cp in remote:
            out_cp.wait_send()
            in_cp.wait_recv()
        for cp in local:
            cp.wait()

    outs = _call(
        body, name=name,
        out_shape=tuple(pltpu.HBM(a.shape, a.dtype) for a in (*state["srcs"], *state["lands"])),
        in_specs=[_HBM] * (2 * n) + [_SEM, _SEM] + [pl.BlockSpec(memory_space=pl.ANY)] * len(after),
        out_specs=tuple([_HBM] * (2 * n)),
        input_output_aliases={k: k for k in range(2 * n)},
        compiler_params=pltpu.CompilerParams(has_side_effects=_EFFECT),
    )(*state["srcs"], *state["lands"], *state["sems"], *after)
    return outs[n:]


def _sibling_copies(l_refs, send_sems, recv_sems, incoming):
    x, y, c = lax.axis_index("x"), lax.axis_index("y"), lax.axis_index("c")
    out = []
    for ti, l_ref in enumerate(l_refs):
        for q in range(4):
            px = (1 - x) if q & 2 else x
            py = (1 - y) if q & 1 else y
            mine, theirs = 4 * px + 2 * py + c, 4 * px + 2 * py + (1 - c)
            sems = dict(send_sem=send_sems.at[4 * ti + q], recv_sem=recv_sems.at[4 * ti + q],
                        device_id=(x, y, 1 - c), device_id_type=MESH)
            out.append((
                pltpu.make_async_remote_copy(src_ref=l_ref.at[mine], dst_ref=l_ref.at[mine], **sems),
                pltpu.make_async_remote_copy(src_ref=l_ref.at[mine], dst_ref=l_ref.at[theirs], **sems) if incoming else None))
    return out


def _sibling_start(lands, *, name, after=()):
    n = len(lands)
    after = list(after)
    lands = [pltpu.with_memory_space_constraint(l, pltpu.HBM) for l in lands]

    def body(*refs):
        l_refs = refs[:n]
        outs = refs[n + len(after):]
        for out_cp, _ in _sibling_copies(l_refs, outs[0], outs[1], incoming=False):
            out_cp.start()
        outs[-1][...] = jnp.zeros_like(outs[-1])

    outs = _call(
        body, name=name,
        out_shape=(pltpu.SemaphoreType.DMA((4 * n,)), pltpu.SemaphoreType.DMA((4 * n,)),
                   *[pltpu.HBM(l.shape, l.dtype) for l in lands], jax.ShapeDtypeStruct((SUBLANES, LANES), F32)),
        in_specs=[_HBM] * n + [pl.BlockSpec(memory_space=pl.ANY)] * len(after),
        out_specs=(_SEM, _SEM, *[_HBM] * n, pl.BlockSpec(memory_space=pltpu.VMEM)),
        input_output_aliases={k: k + 2 for k in range(n)},
        compiler_params=pltpu.CompilerParams(has_side_effects=_EFFECT),
    )(*lands, *after)
    return dict(n=n, sems=outs[:2], lands=outs[2:2 + n]), outs[-1]


def _sibling_wait(state, after, *, name):
    n = state["n"]
    after = list(after)

    def body(*refs):
        l_refs = refs[:n]
        for out_cp, in_cp in _sibling_copies(l_refs, refs[n], refs[n + 1], incoming=True):
            out_cp.wait_send()
            in_cp.wait_recv()

    return _call(
        body, name=name,
        out_shape=tuple(pltpu.HBM(a.shape, a.dtype) for a in state["lands"]),
        in_specs=[_HBM] * n + [_SEM, _SEM] + [pl.BlockSpec(memory_space=pl.ANY)] * len(after),
        out_specs=tuple([_HBM] * n), input_output_aliases={k: k for k in range(n)},
        compiler_params=pltpu.CompilerParams(has_side_effects=_EFFECT),
    )(*state["lands"], *state["sems"], *after)


def _pack(arrs):
    flat = jnp.concatenate([a.reshape(-1).astype(F32) for a in arrs])
    pad = (-flat.shape[0]) % (SUBLANES * LANES)
    return jnp.pad(flat, (0, pad)).reshape(-1, LANES)


def _unpack(packed, shapes):
    flat = packed.reshape(-1)
    out, off = [], 0
    for s in shapes:
        n = 1
        for v in s:
            n *= v
        out.append(flat[off:off + n].reshape(s))
        off += n
    return out


SMALL = ("norm_mix_pre", "ssm_conv_b", "dt_bias", "a_log", "d_skip", "conv_out_norm", "ssm_out_norm",
         "norm_mix_post", "norm_mlp_pre", "norm_mlp_post", "conv_a_w", "ssm_conv_w")
BIG = ("w_in", "w_out", "w_up", "w_down")
ORDER = ("norm_mix_pre", "w_in", "conv_a_w", "ssm_conv_w", "ssm_conv_b", "dt_bias", "a_log", "d_skip",
         "conv_out_norm", "ssm_out_norm", "w_out", "norm_mix_post", "norm_mlp_pre", "w_up", "w_down", "norm_mlp_post")


def kernel(x, norm_mix_pre, w_in, conv_a_w, ssm_conv_w, ssm_conv_b, dt_bias, a_log, d_skip, conv_out_norm, ssm_out_norm, w_out, norm_mix_post, norm_mlp_pre, w_up, w_down, norm_mlp_post, loss_target, m_norm_mix_pre, m_w_in, m_conv_a_w, m_ssm_conv_w, m_ssm_conv_b, m_dt_bias, m_a_log, m_d_skip, m_conv_out_norm, m_ssm_out_norm, m_w_out, m_norm_mix_post, m_norm_mlp_pre, m_w_up, m_w_down, m_norm_mlp_post, v_norm_mix_pre, v_w_in, v_conv_a_w, v_ssm_conv_w, v_ssm_conv_b, v_dt_bias, v_a_log, v_d_skip, v_conv_out_norm, v_ssm_out_norm, v_w_out, v_norm_mix_post, v_norm_mlp_pre, v_w_up, v_w_down, v_norm_mlp_post):
    W = dict(norm_mix_pre=norm_mix_pre, w_in=w_in, conv_a_w=conv_a_w, ssm_conv_w=ssm_conv_w, ssm_conv_b=ssm_conv_b,
             dt_bias=dt_bias, a_log=a_log, d_skip=d_skip, conv_out_norm=conv_out_norm, ssm_out_norm=ssm_out_norm,
             w_out=w_out, norm_mix_post=norm_mix_post, norm_mlp_pre=norm_mlp_pre, w_up=w_up, w_down=w_down,
             norm_mlp_post=norm_mlp_post)
    M = dict(norm_mix_pre=m_norm_mix_pre, w_in=m_w_in, conv_a_w=m_conv_a_w, ssm_conv_w=m_ssm_conv_w,
             ssm_conv_b=m_ssm_conv_b, dt_bias=m_dt_bias, a_log=m_a_log, d_skip=m_d_skip,
             conv_out_norm=m_conv_out_norm, ssm_out_norm=m_ssm_out_norm, w_out=m_w_out,
             norm_mix_post=m_norm_mix_post, norm_mlp_pre=m_norm_mlp_pre, w_up=m_w_up, w_down=m_w_down,
             norm_mlp_post=m_norm_mlp_post)
    V = dict(norm_mix_pre=v_norm_mix_pre, w_in=v_w_in, conv_a_w=v_conv_a_w, ssm_conv_w=v_ssm_conv_w,
             ssm_conv_b=v_ssm_conv_b, dt_bias=v_dt_bias, a_log=v_a_log, d_skip=v_d_skip,
             conv_out_norm=v_conv_out_norm, ssm_out_norm=v_ssm_out_norm, w_out=v_w_out,
             norm_mix_post=v_norm_mix_post, norm_mlp_pre=v_norm_mlp_pre, w_up=v_w_up, w_down=v_w_down,
             norm_mlp_post=v_norm_mlp_post)

    nseq, seq, d = x.shape
    t = nseq * seq
    depth = w_in.shape[0]
    h = d // HEAD_DIM
    xbc = d + 2 * SSM_GROUPS * D_STATE
    in_cols = w_in.shape[2] * N_DEV
    d_mix = w_out.shape[1] * N_DEV
    d_ff = w_up.shape[2] * N_DEV
    me = 4 * lax.axis_index("x") + 2 * lax.axis_index("y") + lax.axis_index("c")
    ca_shard = conv_a_w.shape[2]
    sc_shard = ssm_conv_w.shape[2]

    tap_shapes = [conv_a_w.shape[1:], ssm_conv_w.shape[1:]]

    def gather_start(i, after=()):
        ks = SAME_CORE_PEERS if i == 0 else ALL_PEERS
        st_in, tok_in = _exchange_start([w_in[i].astype(BF16), _pack([conv_a_w[i], ssm_conv_w[i]])], gather=True,
                                        name=f"gather_start_in_{i}", after=after, ks=ks)
        st_rest, tok_rest = _exchange_start([W[n][i].astype(BF16) for n in ("w_out", "w_up", "w_down")], gather=True,
                                            name=f"gather_start_rest_{i}", after=[tok_in], ks=ks)
        return st_in, st_rest, tok_rest

    vec = lambda name, i: W[name][i].reshape(1, -1)
    emat = (lax.broadcasted_iota(jnp.int32, (h, d), 1) // HEAD_DIM == lax.broadcasted_iota(jnp.int32, (h, d), 0)).astype(F32)

    xcur = x.reshape(t, d)
    hcur = _norm_fwd(xcur, vec("norm_mix_pre", 0), name="norm_first")
    saved = []
    nxt = gather_start(0)
    for i in range(depth):
        st_in, st_rest, tok = nxt
        landed = _exchange_wait(st_in, [hcur, tok], name=f"gather_wait_in_{i}")
        if i == 0:
            st_sib, tok_sib = _sibling_start(landed, name="gather_sibling_start_in_0")
            landed = _sibling_wait(st_sib, [tok_sib], name="gather_sibling_wait_in_0")
        win_g, taps_g = landed
        win = _assemble_cols(win_g, name=f"assemble_w_in_{i}")
        taps_j = [_unpack(taps_g[j], tap_shapes) for j in range(N_DEV)]
        conv_a_i = jnp.concatenate([tj[0] for tj in taps_j], axis=1)
        ssm_conv_i = jnp.concatenate([tj[1] for tj in taps_j], axis=1)
        proj = _mm(hcur, win, n=4 * d + xbc, name=f"fwd_proj_{i}", out_dtypes=(BF16,))
        pdt = _mm(hcur, win, n=LANES, b_off=4 * d + xbc, name=f"fwd_proj_dt_{i}")
        ya, va = _conva_fwd(proj, conv_a_i, vec("conv_out_norm", i), d=d, seq=seq, name=f"fwd_conv_a_{i}")
        cpre = _convb_fwd(proj, ssm_conv_i, vec("ssm_conv_b", i), col0=4 * d, seq=seq, name=f"fwd_conv_b_{i}")
        dsk_lane = jnp.repeat(W["d_skip"][i], HEAD_DIM).reshape(1, d)
        dtb = vec("dt_bias", i)
        if i == 0:
            st_sib, tok_sib = _sibling_start(_exchange_wait(st_rest, [cpre], name="gather_wait_rest_0"),
                                             name="gather_sibling_start_rest_0")
            dtb = dtb + tok_sib[0:1, 0:1]
        cat, y2, hprev = _ssd_fwd(cpre, pdt, proj, ya, dtb, vec("a_log", i), dsk_lane,
                                  vec("ssm_out_norm", i), emat, nseq=nseq, seq=seq, name=f"fwd_ssd_{i}")
        if i == 0:
            wout_g, wup_g, wdown_g = _sibling_wait(st_sib, [cat], name="gather_sibling_wait_rest_0")
        else:
            wout_g, wup_g, wdown_g = _exchange_wait(st_rest, [cat], name=f"gather_wait_rest_{i}")
        lw = dict(win=win, wout=wout_g.reshape(d_mix, d),
                  wup=_assemble_cols(wup_g, name=f"assemble_w_up_{i}"), wdown=wdown_g.reshape(d_ff, d),
                  conv_a=conv_a_i, ssm_conv=ssm_conv_i)
        after = []
        if i + 1 < depth:
            nxt = gather_start(i + 1, after=[wout_g])
            after = [nxt[2]]
        x1, h2, mix = _mm(cat, lw["wout"], name=f"fwd_out_{i}", after=after, out_dtypes=(F32, BF16, BF16),
                          epi=_epi_resid_norm, extras=(xcur,), vecs=(vec("norm_mix_post", i), vec("norm_mlp_pre", i)),
                          tm_cap=FUSED_ROWS)
        f = _mm(h2, lw["wup"], name=f"fwd_up_{i}", out_dtypes=(BF16,), epi=_epi_relu2)
        g_next = vec("norm_mix_pre", i + 1) if i + 1 < depth else vec("norm_mix_pre", 0)
        dn = _mm(f, lw["wdown"], name=f"fwd_down_{i}")
        x2, hnext = _resid_norm(x1, dn, vec("norm_mlp_post", i), g_next, name=f"fwd_post_mlp_{i}")
        saved.append(dict(lw=lw, x0=xcur, h=hcur, proj=proj, pdt=pdt, va=va, cpre=cpre, y2=y2,
                          hprev=hprev, cat=cat, mix=mix, x1=x1, h2=h2, f=f, dn=dn, dsk_lane=dsk_lane))
        xcur, hcur = x2, hnext

    dx, loss_part = _loss_fwd_bwd(xcur, loss_target.reshape(t, d), name="loss")
    loss = lax.psum(loss_part[0, 0], ("x", "y", "c"))

    small_grads = {n: [None] * depth for n in SMALL}
    big_out = {n: None for n in BIG}

    def finish(pending, after):
        li, st_a, st_b = pending

        def update(n, parts):
            big_out[n] = _sum_adamw(parts, W[n], M[n], V[n], layer=li, outs=big_out[n], name=f"adamw_{n}_{li}")

        p_down, p_up, p_out = _exchange_wait(st_a, after, name=f"scatter_wait_a_{li}")
        update("w_down", p_down)
        update("w_up", p_up)
        update("w_out", p_out)
        p_in, = _exchange_wait(st_b, after + [big_out["w_out"][0]], name=f"scatter_wait_b_{li}")
        update("w_in", p_in)

    pending = None
    for i in reversed(range(depth)):
        s = saved[i]
        lw = s["lw"]
        ddn, dg = _bwd_norm_out(s["dn"], vec("norm_mlp_post", i), dx, name=f"bwd_norm_mlp_post_{i}")
        small_grads["norm_mlp_post"][i] = dg
        dup = _mm(ddn, lw["wdown"], tb=True, name=f"bwd_down_dx_{i}", out_dtypes=(BF16,), epi=_epi_drelu2,
                  extras=(s["f"],))
        g_wdown = _mm(s["f"], ddn, ta=True, name=f"bwd_down_dw_{i}", out_dtypes=(BF16,))
        dh2 = _mm(dup, lw["wup"], tb=True, name=f"bwd_up_dx_{i}", out_dtypes=(BF16,))
        g_wup = _mm(s["h2"], dup, ta=True, name=f"bwd_up_dw_{i}", out_dtypes=(BF16,))
        dx1, dmix, dg_pre, dg_post = _bwd_norm_pair(s["x1"], [dh2], dx, s["mix"], vec("norm_mlp_pre", i),
                                                    vec("norm_mix_post", i), name=f"bwd_norm_mix_post_{i}")
        small_grads["norm_mlp_pre"][i] = dg_pre
        small_grads["norm_mix_post"][i] = dg_post
        dcat = _mm(dmix, lw["wout"], tb=True, name=f"bwd_out_dx_{i}", out_dtypes=(BF16,))
        g_wout = _mm(s["cat"], dmix, ta=True, name=f"bwd_out_dw_{i}", out_dtypes=(BF16,))
        st_a, tok_a = _exchange_start(
            [g_wdown.reshape(N_DEV, d_ff // N_DEV, d), _split_cols([g_wup], d_ff // N_DEV, name=f"split_g_w_up_{i}"),
             g_wout.reshape(N_DEV, d_mix // N_DEV, d)], gather=False, name=f"scatter_start_a_{i}")
        dproj, dcaw, dgca = _conva_bwd(dcat, s["proj"], s["va"], lw["conv_a"],
                                       vec("conv_out_norm", i) + tok_a[0:1, 0:1], d=d, seq=seq, name=f"bwd_conv_a_{i}")
        small_grads["conv_a_w"][i] = dcaw
        small_grads["conv_out_norm"][i] = dgca
        dconv, dproj, dpdt, dgs, ddsk, ddtb, dalog = _ssd_bwd(
            s["cpre"], s["pdt"], s["proj"], s["y2"], s["hprev"], dcat, vec("dt_bias", i), vec("a_log", i),
            s["dsk_lane"], vec("ssm_out_norm", i), emat, dproj, nseq=nseq, seq=seq, name=f"bwd_ssd_{i}")
        small_grads["ssm_out_norm"][i] = dgs
        small_grads["d_skip"][i] = ddsk
        small_grads["dt_bias"][i] = ddtb
        small_grads["a_log"][i] = dalog
        dproj, dscw, dscb = _convb_bwd(dconv, s["proj"], lw["ssm_conv"], dproj, col0=4 * d, seq=seq,
                                       name=f"bwd_conv_b_{i}")
        small_grads["ssm_conv_w"][i] = dscw
        small_grads["ssm_conv_b"][i] = dscb
        g_win = _split_cols([
            _mm(s["h"], dproj, ta=True, name=f"bwd_proj_dw_{i}", out_dtypes=(BF16,)),
            _mm(s["h"], dpdt, ta=True, name=f"bwd_proj_dt_dw_{i}", out_dtypes=(BF16,))],
            in_cols // N_DEV, name=f"split_g_w_in_{i}")
        st_b, tok_b = _exchange_start([g_win], gather=False, name=f"scatter_start_b_{i}")
        dh_parts = [_mm(dp, lw["win"], tb=True, b_koff=off, name=f"bwd_proj_{nm}dx_{i}", after=[tok_b], out_dtypes=(BF16,))
                    for nm, dp, off in (("", dproj, 0), ("dt_", dpdt, 4 * d + xbc))]
        dx, dg_in = _bwd_norm_in(s["x0"], dh_parts, dx1, vec("norm_mix_pre", i), name=f"bwd_norm_mix_pre_{i}")
        small_grads["norm_mix_pre"][i] = dg_in
        if pending is not None:
            finish(pending, [dx])
        pending = (i, st_a, st_b)

    grad_x = dx.reshape(nseq, seq, d)

    small_shapes_full = {n: (depth,) + tuple(small_grads[n][0].shape) for n in SMALL}
    gpack = _pack([jnp.stack(small_grads[n]) for n in SMALL])
    st_small, tok_small = _exchange_start([gpack], gather=True, name="allreduce_small_start")
    finish(pending, [dx, tok_small])
    gparts, = _exchange_wait(st_small, [big_out["w_in"][0]], name="allreduce_small_wait")

    def shard_of(n, full):
        if n == "conv_a_w":
            return lax.dynamic_slice_in_dim(full, me * ca_shard, ca_shard, axis=2)
        if n == "ssm_conv_w":
            return lax.dynamic_slice_in_dim(full, me * sc_shard, sc_shard, axis=2)
        return full.reshape(W[n].shape)

    gsum = _sum_parts(gparts, name="sum_small")
    gfull = _unpack(gsum, [small_shapes_full[n] for n in SMALL])
    gsmall = {n: shard_of(n, gf) for n, gf in zip(SMALL, gfull)}
    res = _sum_adamw(_pack([gsmall[n] for n in SMALL])[None], _pack([W[n] for n in SMALL]),
                     _pack([M[n] for n in SMALL]), _pack([V[n] for n in SMALL]), name="adamw_small")
    small_out = [dict(zip(SMALL, _unpack(r, [W[n].shape for n in SMALL]))) for r in res]

    def out_of(kind, n):
        return big_out[n][kind] if n in BIG else small_out[kind][n]

    return (loss, grad_x, *[out_of(k, n) for k in range(4) for n in ORDER])
```

```python
import jax
import jax.numpy as jnp
from jax import lax
from jax.experimental import pallas as pl
from jax.experimental.pallas import tpu as pltpu

F32 = jnp.float32
BF16 = jnp.bfloat16
HIGHEST = lax.Precision.HIGHEST
MESH = pl.DeviceIdType.MESH

EPS = 1e-6
HEAD_DIM = 64
D_STATE = 128
SSM_GROUPS = 2
CHUNK = 128
CONV_K = 3
SSM_CONV_K = 4
ADAM_LR = 0.001
ADAM_B1 = 0.9
ADAM_B2 = 0.999
ADAM_EPS = 1e-08
ADAM_WD = 0.01
ADAM_STEP = 10

N_DEV = 8
LANES = 128
SUBLANES = 8
VMEM_LIMIT = 48 * 1024 * 1024
ROW_TILE = 512
MM_TILE = 1024
MM_TILE_N = 2816
MM_VMEM_BUDGET = 40 * 1024 * 1024
FUSED_ROWS = 512


def _params(sem):
    return pltpu.CompilerParams(dimension_semantics=sem, vmem_limit_bytes=VMEM_LIMIT)


def _call(body, **kw):
    return pl.pallas_call(body, **kw)


def _pick(n, cap):
    best = None
    for t in range(LANES, min(n, cap) + 1, LANES):
        if n % t == 0:
            best = t
    return best or n


def _pick_rows(n, cap):
    best = None
    for t in range(SUBLANES, min(n, cap) + 1, SUBLANES):
        if n % t == 0:
            best = t
    return best or n


def _sigmoid(x):
    return 1.0 / (1.0 + jnp.exp(-x))


def _softplus(x):
    return jnp.maximum(x, 0.0) + jnp.log1p(jnp.exp(-jnp.abs(x)))


def _rms(x):
    return lax.rsqrt(jnp.mean(x * x, axis=-1, keepdims=True) + EPS)


def _rms_bwd(x, r, g, dy):
    gy = dy * g
    dx = r * gy - x * (r * r * r) * jnp.mean(gy * x, axis=-1, keepdims=True)
    return dx, dy * x * r


def _full(shape):
    return pl.BlockSpec(shape, lambda *_: (0,) * len(shape))


def _mm(a, b, *, name, ta=False, tb=False, out_dtypes=(F32,), epi=None, extras=(), n=None, b_off=0, b_koff=0,
        after=(), vecs=(), tm_cap=MM_TILE):
    m, k = (a.shape[1], a.shape[0]) if ta else a.shape
    if n is None:
        n = b.shape[0] if tb else b.shape[1]
    tm, tn, tk = _pick(m, tm_cap), _pick(n, MM_TILE_N), _pick(k, MM_TILE)
    while b_off % tn or n % tn:
        tn -= LANES
    if b_koff == 0 and k > MM_TILE:
        tk = _pick(k, MM_TILE_N)
    while b_koff % tk or k % tk:
        tk -= LANES

    def vmem_bytes(tk_):
        per_out = sum(jnp.dtype(dt).itemsize for dt in out_dtypes) + sum(e.dtype.itemsize for e in extras)
        return 2 * tk_ * (tm * a.dtype.itemsize + tn * b.dtype.itemsize) + tm * tn * (2 * per_out + 4)

    while vmem_bytes(tk) > MM_VMEM_BUDGET and tk % (2 * LANES) == 0 and not b_koff % (tk // 2):
        tk //= 2
    nk = k // tk
    nm, nn = m // tm, n // tn
    jo = b_off // tn
    ko = b_koff // tk
    a_bytes = m * k * a.dtype.itemsize
    b_bytes = n * k * b.dtype.itemsize
    m_outer = a_bytes + nm * b_bytes <= b_bytes + nn * a_bytes
    ij = (lambda g0, g1: (g0, g1)) if m_outer else (lambda g0, g1: (g1, g0))
    grid = (nm, nn, nk) if m_outer else (nn, nm, nk)

    def a_map(g0, g1, kk):
        i, _ = ij(g0, g1)
        return (kk, i) if ta else (i, kk)

    def b_map(g0, g1, kk):
        _, j = ij(g0, g1)
        return (j + jo, kk + ko) if tb else (kk + ko, j + jo)

    def o_map(g0, g1, kk):
        return ij(g0, g1)

    a_spec = pl.BlockSpec((tk, tm) if ta else (tm, tk), a_map)
    b_spec = pl.BlockSpec((tn, tk) if tb else (tk, tn), b_map)
    o_spec = pl.BlockSpec((tm, tn), o_map)
    dims = (((0 if ta else 1,), (1 if tb else 0,)), ((), ()))
    n_ex = len(extras) + len(vecs)
    after = list(after)
    o0 = 2 + n_ex + len(after)

    def finish(acc, ex, outs):
        res = (acc,) if epi is None else epi(acc, *[e[...] for e in ex])
        for o, r in zip(outs, res):
            o[...] = r.astype(o.dtype)

    def body_single(*refs):
        a_ref, b_ref = refs[:2]
        acc = lax.dot_general(a_ref[...].astype(BF16), b_ref[...].astype(BF16), dims, preferred_element_type=F32)
        finish(acc, refs[2:2 + n_ex], refs[o0:])

    def body_multi(*refs):
        a_ref, b_ref = refs[:2]
        acc = refs[-1]
        kk = pl.program_id(2)

        @pl.when(kk == 0)
        def _():
            acc[...] = jnp.zeros_like(acc)

        acc[...] += lax.dot_general(a_ref[...].astype(BF16), b_ref[...].astype(BF16), dims, preferred_element_type=F32)

        @pl.when(kk == nk - 1)
        def _():
            finish(acc[...], refs[2:2 + n_ex], refs[o0:-1])

    v_spec = pl.BlockSpec((1, tn), lambda g0, g1, kk: (0, ij(g0, g1)[1]))
    outs = _call(
        body_single if nk == 1 else body_multi, name=name, grid=grid,
        in_specs=([a_spec, b_spec] + [o_spec] * len(extras) + [v_spec] * len(vecs)
                  + [pl.BlockSpec(memory_space=pl.ANY)] * len(after)),
        out_specs=[o_spec] * len(out_dtypes),
        out_shape=[jax.ShapeDtypeStruct((m, n), dt) for dt in out_dtypes],
        scratch_shapes=[] if nk == 1 else [pltpu.VMEM((tm, tn), F32)],
        compiler_params=_params(("parallel", "parallel", "arbitrary")),
    )(a, b, *extras, *vecs, *after)
    return outs[0] if len(outs) == 1 else outs


def _epi_resid_norm(acc, x, g_res, g_next):
    xn = x + acc * _rms(acc) * g_res
    return xn, xn * _rms(xn) * g_next, acc


def _epi_relu2(acc):
    r = jnp.maximum(acc, 0.0)
    return (r * r,)


def _epi_drelu2(acc, f):
    return (acc * (2.0 * jnp.sqrt(f.astype(F32))),)


def _norm_fwd(x, g, *, name):
    t, d = x.shape
    tt = _pick_rows(t, ROW_TILE)

    def body(x_ref, g_ref, h_ref):
        xv = x_ref[...]
        h_ref[...] = (xv * _rms(xv) * g_ref[...]).astype(BF16)

    row = pl.BlockSpec((tt, d), lambda i: (i, 0))
    return _call(body, name=name, grid=(t // tt,), in_specs=[row, _full((1, d))], out_specs=row,
                 out_shape=jax.ShapeDtypeStruct((t, d), BF16), compiler_params=_params(("parallel",)))(x, g)


def _resid_norm(x, n, g1, g2, *, name):
    t, d = x.shape
    tt = _pick_rows(t, ROW_TILE)

    def body(x_ref, n_ref, g1_ref, g2_ref, xo_ref, h_ref):
        nv = n_ref[...].astype(F32)
        xn = x_ref[...] + nv * _rms(nv) * g1_ref[...]
        xo_ref[...] = xn
        h_ref[...] = (xn * _rms(xn) * g2_ref[...]).astype(BF16)

    row = pl.BlockSpec((tt, d), lambda i: (i, 0))
    return _call(body, name=name, grid=(t // tt,), in_specs=[row, row, _full((1, d)), _full((1, d))],
                 out_specs=[row, row],
                 out_shape=[jax.ShapeDtypeStruct((t, d), F32), jax.ShapeDtypeStruct((t, d), BF16)],
                 compiler_params=_params(("parallel",)))(x, n, g1, g2)


def _loss_fwd_bwd(xf, target, *, name):
    t, d = xf.shape
    tt = _pick_rows(t, ROW_TILE)
    nt = t // tt

    def body(x_ref, t_ref, dy_ref, loss_ref, acc):
        i = pl.program_id(0)

        @pl.when(i == 0)
        def _():
            acc[...] = jnp.zeros_like(acc)

        e = x_ref[...] - t_ref[...]
        dy_ref[...] = e * (1.0 / d)
        acc[...] += jnp.sum(e * e, axis=0, keepdims=True)

        @pl.when(i == nt - 1)
        def _():
            loss_ref[...] = jnp.sum(acc[...], axis=-1, keepdims=True) * (0.5 / d)

    row = pl.BlockSpec((tt, d), lambda i: (i, 0))
    return _call(body, name=name, grid=(nt,), in_specs=[row, row], out_specs=[row, _full((1, 1))],
                 out_shape=[jax.ShapeDtypeStruct((t, d), F32), jax.ShapeDtypeStruct((1, 1), F32)],
                 scratch_shapes=[pltpu.VMEM((1, d), F32)], compiler_params=_params(("arbitrary",)))(xf, target)


def _bwd_norm_pair(xin, dh, dres, n, g_in, g_out, *, name):
    t, d = xin.shape
    tt = _pick_rows(t, ROW_TILE)
    n_dh = len(dh)

    def body(*refs):
        x_ref = refs[0]
        dh_refs = refs[1:1 + n_dh]
        dres_ref, n_ref, gi_ref, go_ref, dx_ref, dn_ref, dgi_ref, dgo_ref = refs[1 + n_dh:]
        i = pl.program_id(0)

        @pl.when(i == 0)
        def _():
            dgi_ref[...] = jnp.zeros_like(dgi_ref)
            dgo_ref[...] = jnp.zeros_like(dgo_ref)

        xv = x_ref[...]
        dhv = dh_refs[0][...].astype(F32)
        for r in dh_refs[1:]:
            dhv = dhv + r[...].astype(F32)
        dxh, dgi = _rms_bwd(xv, _rms(xv), gi_ref[...], dhv)
        dx = dres_ref[...] + dxh
        dx_ref[...] = dx
        dgi_ref[...] += jnp.sum(dgi, axis=0, keepdims=True)
        nv = n_ref[...].astype(F32)
        dn, dgo = _rms_bwd(nv, _rms(nv), go_ref[...], dx)
        dn_ref[...] = dn.astype(BF16)
        dgo_ref[...] += jnp.sum(dgo, axis=0, keepdims=True)

    row = pl.BlockSpec((tt, d), lambda i: (i, 0))
    vec = _full((1, d))
    return _call(body, name=name, grid=(t // tt,), in_specs=[row] * (n_dh + 3) + [vec, vec],
                 out_specs=[row, row, vec, vec],
                 out_shape=[jax.ShapeDtypeStruct((t, d), F32), jax.ShapeDtypeStruct((t, d), BF16),
                            jax.ShapeDtypeStruct((1, d), F32), jax.ShapeDtypeStruct((1, d), F32)],
                 compiler_params=_params(("arbitrary",)))(xin, *dh, dres, n, g_in, g_out)


def _bwd_norm_in(xin, dh, dres, g_in, *, name):
    t, d = xin.shape
    tt = _pick_rows(t, ROW_TILE)
    n_dh = len(dh)

    def body(*refs):
        x_ref = refs[0]
        dh_refs = refs[1:1 + n_dh]
        dres_ref, gi_ref, dx_ref, dgi_ref = refs[1 + n_dh:]
        i = pl.program_id(0)

        @pl.when(i == 0)
        def _():
            dgi_ref[...] = jnp.zeros_like(dgi_ref)

        xv = x_ref[...]
        dhv = dh_refs[0][...].astype(F32)
        for r in dh_refs[1:]:
            dhv = dhv + r[...].astype(F32)
        dxh, dgi = _rms_bwd(xv, _rms(xv), gi_ref[...], dhv)
        dx_ref[...] = dres_ref[...] + dxh
        dgi_ref[...] += jnp.sum(dgi, axis=0, keepdims=True)

    row = pl.BlockSpec((tt, d), lambda i: (i, 0))
    vec = _full((1, d))
    return _call(body, name=name, grid=(t // tt,), in_specs=[row] * (n_dh + 2) + [vec],
                 out_specs=[row, vec],
                 out_shape=[jax.ShapeDtypeStruct((t, d), F32), jax.ShapeDtypeStruct((1, d), F32)],
                 compiler_params=_params(("arbitrary",)))(xin, *dh, dres, g_in)


def _bwd_norm_out(n, g_out, dx, *, name):
    t, d = n.shape
    tt = _pick_rows(t, ROW_TILE)

    def body(n_ref, go_ref, dx_ref, dn_ref, dgo_ref):
        i = pl.program_id(0)

        @pl.when(i == 0)
        def _():
            dgo_ref[...] = jnp.zeros_like(dgo_ref)

        nv = n_ref[...].astype(F32)
        dn, dgo = _rms_bwd(nv, _rms(nv), go_ref[...], dx_ref[...])
        dn_ref[...] = dn.astype(BF16)
        dgo_ref[...] += jnp.sum(dgo, axis=0, keepdims=True)

    row = pl.BlockSpec((tt, d), lambda i: (i, 0))
    vec = _full((1, d))
    return _call(body, name=name, grid=(t // tt,), in_specs=[row, vec, row], out_specs=[row, vec],
                 out_shape=[jax.ShapeDtypeStruct((t, d), BF16), jax.ShapeDtypeStruct((1, d), F32)],
                 compiler_params=_params(("arbitrary",)))(n, g_out, dx)


def _shift_down(cur, halo, s):
    return jnp.concatenate([halo[SUBLANES - s:], cur[:cur.shape[0] - s]], axis=0)


def _shift_up(cur, halo, s):
    return jnp.concatenate([cur[s:], halo[:s]], axis=0)


def _conva_fwd(pa, w, g, *, d, seq, name):
    t = pa.shape[0]
    tt = _pick_rows(seq, ROW_TILE)
    tps = seq // tt

    def body(xa_ref, ca_ref, ba_ref, w_ref, g_ref, ya_ref, v_ref, carry):
        i = pl.program_id(0)

        @pl.when(i % tps == 0)
        def _():
            carry[...] = jnp.zeros_like(carry)

        u = ca_ref[...].astype(F32) * xa_ref[...].astype(F32)
        halo = carry[...]
        wv = w_ref[...]
        v = wv[2:3] * u + wv[1:2] * _shift_down(u, halo, 1) + wv[0:1] * _shift_down(u, halo, 2)
        carry[...] = u[tt - SUBLANES:]
        yp = ba_ref[...].astype(F32) * v
        ya_ref[...] = (yp * _rms(yp) * g_ref[...]).astype(BF16)
        v_ref[...] = v.astype(BF16)

    col = lambda c: pl.BlockSpec((tt, d), lambda i, c=c: (i, c))
    row = pl.BlockSpec((tt, d), lambda i: (i, 0))
    return _call(body, name=name, grid=(t // tt,),
                 in_specs=[col(0), col(1), col(2), _full((CONV_K, d)), _full((1, d))], out_specs=[row, row],
                 out_shape=[jax.ShapeDtypeStruct((t, d), BF16), jax.ShapeDtypeStruct((t, d), BF16)],
                 scratch_shapes=[pltpu.VMEM((SUBLANES, d), F32)],
                 compiler_params=_params(("arbitrary",)))(pa, pa, pa, w, g)


def _conva_bwd(dcat, pa, v, w, g, *, d, seq, name):
    t, width = pa.shape
    d3 = 3 * d
    tt = _pick_rows(seq, ROW_TILE)
    tps = seq // tt
    nt = t // tt

    def body(dya_ref, xa_ref, ca_ref, ba_ref, v_ref, w_ref, g_ref, dpa_ref, dw_ref, dg_ref, carry):
        i = pl.program_id(0)

        @pl.when(i == 0)
        def _():
            dw_ref[...] = jnp.zeros_like(dw_ref)
            dg_ref[...] = jnp.zeros_like(dg_ref)

        @pl.when(i % tps == 0)
        def _():
            carry[...] = jnp.zeros_like(carry)

        xa, ca, ba, vv = [r[...].astype(F32) for r in (xa_ref, ca_ref, ba_ref, v_ref)]
        yp = ba * vv
        dyp, dgt = _rms_bwd(yp, _rms(yp), g_ref[...], dya_ref[...].astype(F32))
        dg_ref[...] += jnp.sum(dgt, axis=0, keepdims=True)
        dv = dyp * ba
        halo = carry[...]
        dv1 = _shift_up(dv, halo, 1)
        dv2 = _shift_up(dv, halo, 2)
        carry[...] = dv[:SUBLANES]
        wv = w_ref[...]
        du = wv[2:3] * dv + wv[1:2] * dv1 + wv[0:1] * dv2
        u = ca * xa
        dw_ref[0:1, :] += jnp.sum(u * dv2, axis=0, keepdims=True)
        dw_ref[1:2, :] += jnp.sum(u * dv1, axis=0, keepdims=True)
        dw_ref[2:3, :] += jnp.sum(u * dv, axis=0, keepdims=True)
        dpa_ref[:, 0:d] = (du * ca).astype(BF16)
        dpa_ref[:, d:2 * d] = (du * xa).astype(BF16)
        dpa_ref[:, 2 * d:3 * d] = (dyp * vv).astype(BF16)

    rcol = lambda c: pl.BlockSpec((tt, d), lambda i, c=c: (nt - 1 - i, c))
    return _call(body, name=name, grid=(nt,),
                 in_specs=[rcol(0), rcol(0), rcol(1), rcol(2), rcol(0), _full((CONV_K, d)), _full((1, d))],
                 out_specs=[pl.BlockSpec((tt, d3), lambda i: (nt - 1 - i, 0)), _full((CONV_K, d)), _full((1, d))],
                 out_shape=[jax.ShapeDtypeStruct((t, width), BF16), jax.ShapeDtypeStruct((CONV_K, d), F32),
                            jax.ShapeDtypeStruct((1, d), F32)],
                 scratch_shapes=[pltpu.VMEM((SUBLANES, d), F32)],
                 compiler_params=_params(("arbitrary",)))(dcat, pa, pa, pa, v, w, g)


CONV_CH = 512


def _convb_fwd(proj, w, bias, *, col0, seq, name):
    t = proj.shape[0]
    c = w.shape[1]
    cb = _pick(c, CONV_CH)
    assert col0 % cb == 0
    tt = _pick_rows(seq, 2 * ROW_TILE)
    tps = seq // tt

    def body(p_ref, w_ref, b_ref, o_ref, carry):
        i = pl.program_id(1)

        @pl.when(i % tps == 0)
        def _():
            carry[...] = jnp.zeros_like(carry)

        p = p_ref[...].astype(F32)
        halo = carry[...]
        wv = w_ref[...]
        o = wv[3:4] * p + b_ref[...]
        for s in (1, 2, 3):
            o = o + wv[3 - s:4 - s] * _shift_down(p, halo, s)
        carry[...] = p[tt - SUBLANES:]
        o_ref[...] = o.astype(BF16)

    return _call(body, name=name, grid=(c // cb, t // tt),
                 in_specs=[pl.BlockSpec((tt, cb), lambda jc, i: (i, col0 // cb + jc)),
                           pl.BlockSpec((SSM_CONV_K, cb), lambda jc, i: (0, jc)), pl.BlockSpec((1, cb), lambda jc, i: (0, jc))],
                 out_specs=pl.BlockSpec((tt, cb), lambda jc, i: (i, jc)), out_shape=jax.ShapeDtypeStruct((t, c), BF16),
                 scratch_shapes=[pltpu.VMEM((SUBLANES, cb), F32)],
                 compiler_params=_params(("arbitrary", "arbitrary")))(proj, w, bias)


def _convb_bwd(dconv, proj, w, dproj, *, col0, seq, name):
    t, c = dconv.shape
    cb = _pick(c, CONV_CH)
    assert col0 % cb == 0
    tt = _pick_rows(seq, 2 * ROW_TILE)
    tps = seq // tt
    nt = t // tt

    def body(dc_ref, p_ref, w_ref, dproj_in, dp_ref, dw_ref, db_ref, carry):
        del dproj_in
        i = pl.program_id(1)

        @pl.when(i == 0)
        def _():
            dw_ref[...] = jnp.zeros_like(dw_ref)
            db_ref[...] = jnp.zeros_like(db_ref)

        @pl.when(i % tps == 0)
        def _():
            carry[...] = jnp.zeros_like(carry)

        dc = dc_ref[...].astype(F32)
        p = p_ref[...].astype(F32)
        halo = carry[...]
        wv = w_ref[...]
        dp = wv[3:4] * dc
        dw_ref[3:4, :] += jnp.sum(p * dc, axis=0, keepdims=True)
        for s in (1, 2, 3):
            dcs = _shift_up(dc, halo, s)
            dp = dp + wv[3 - s:4 - s] * dcs
            dw_ref[3 - s:4 - s, :] += jnp.sum(p * dcs, axis=0, keepdims=True)
        carry[...] = dc[:SUBLANES]
        db_ref[...] += jnp.sum(dc, axis=0, keepdims=True)
        dp_ref[...] = dp.astype(BF16)

    win_spec = pl.BlockSpec((tt, cb), lambda jc, i: (nt - 1 - i, col0 // cb + jc))
    taps = pl.BlockSpec((SSM_CONV_K, cb), lambda jc, i: (0, jc))
    return _call(body, name=name, grid=(c // cb, nt),
                 in_specs=[pl.BlockSpec((tt, cb), lambda jc, i: (nt - 1 - i, jc)), win_spec, taps,
                           pl.BlockSpec(memory_space=pl.ANY)],
                 out_specs=[win_spec, taps, pl.BlockSpec((1, cb), lambda jc, i: (0, jc))],
                 out_shape=[jax.ShapeDtypeStruct(dproj.shape, BF16), jax.ShapeDtypeStruct((SSM_CONV_K, c), F32),
                            jax.ShapeDtypeStruct((1, c), F32)],
                 input_output_aliases={3: 0},
                 scratch_shapes=[pltpu.VMEM((SUBLANES, cb), F32)],
                 compiler_params=_params(("arbitrary", "arbitrary")))(dconv, proj, w, dproj)


def _expand_heads(x, ev):
    return jnp.dot(x, ev, precision=HIGHEST, preferred_element_type=F32)


def _head_sums(v, ev):
    return lax.dot_general(v, ev, (((1,), (1,)), ((), ())), precision=HIGHEST, preferred_element_type=F32)


def _ssd_common(c_ref, pdt_ref, dtb_ref, alog_ref, e_ref, h):
    cp = c_ref[...].astype(F32)
    sg = _sigmoid(cp)
    act = cp * sg
    pre = pdt_ref[:, 0:h] + dtb_ref[...]
    dt = _softplus(pre)
    a = -jnp.exp(alog_ref[...])
    adt = dt * a
    row = lax.broadcasted_iota(jnp.int32, (CHUNK, CHUNK), 0)
    col = lax.broadcasted_iota(jnp.int32, (CHUNK, CHUNK), 1)
    tril = row >= col
    cs = jnp.dot(tril.astype(F32), adt, precision=HIGHEST, preferred_element_type=F32)
    cs_t = lax.dot_general(adt, (col >= row).astype(F32), (((0,), (0,)), ((), ())), precision=HIGHEST,
                           preferred_element_type=F32)
    ev = e_ref[...]
    dt_l = _expand_heads(dt, ev)
    ecs_l = jnp.exp(_expand_heads(cs, ev))
    return dict(cp=cp, sg=sg, act=act, pre=pre, dt=dt, a=a, cs=cs, cs_t=cs_t, dt_l=dt_l, ecs_l=ecs_l,
                tril=tril, row=row, col=col, lo=col < HEAD_DIM)


def _dot_nt(a, b):
    return lax.dot_general(a, b, (((1,), (1,)), ((), ())), preferred_element_type=F32)


def _dot_tn(a, b):
    return lax.dot_general(a, b, (((0,), (0,)), ((), ())), preferred_element_type=F32)


def _dot(a, b):
    return jnp.dot(a, b, preferred_element_type=F32)


def _ssd_fwd(cpre, pdt, pz, ya, dtb, alog, dsk_lane, gs, emat, *, nseq, seq, name):
    t, xbc = cpre.shape
    d = ya.shape[1]
    h = d // HEAD_DIM
    npair = h // 2
    ppg = npair // SSM_GROUPS
    nc = seq // CHUNK
    gw = d // SSM_GROUPS
    bc0 = d
    cc0 = d + SSM_GROUPS * D_STATE

    def body(c_ref, pdt_ref, z_ref, ya_ref, dtb_ref, alog_ref, dsk_ref, gs_ref, e_ref, cat_ref, y2_ref, hp_ref, h_ref):
        @pl.when(pl.program_id(0) == 0)
        def _():
            h_ref[...] = jnp.zeros_like(h_ref)

        for sq in range(nseq):
            one_seq(c_ref.at[sq], pdt_ref.at[sq], z_ref.at[sq], ya_ref.at[sq], dtb_ref, alog_ref, dsk_ref, gs_ref, e_ref,
                    cat_ref.at[sq], y2_ref.at[sq], hp_ref.at[sq], h_ref.at[sq])

    def one_seq(c_ref, pdt_ref, z_ref, ya_ref, dtb_ref, alog_ref, dsk_ref, gs_ref, e_ref, cat_ref, y2_ref, hp_ref, h_ref):
        q = _ssd_common(c_ref, pdt_ref, dtb_ref, alog_ref, e_ref, h)
        act, cs, lo, ecs_l = q["act"], q["cs"], q["lo"], q["ecs_l"]
        xs = act[:, :d]
        xd = xs * q["dt_l"]
        ys = []
        for g in range(SSM_GROUPS):
            bg = act[:, bc0 + g * D_STATE: bc0 + (g + 1) * D_STATE]
            cgb = act[:, cc0 + g * D_STATE: cc0 + (g + 1) * D_STATE].astype(BF16)
            s = _dot_nt(cgb, bg.astype(BF16))
            bg_t = bg.T
            for jj in range(ppg):
                j = g * ppg + jj
                sl = slice(LANES * j, LANES * (j + 1))
                xdj = xd[:, sl]
                x2 = jnp.concatenate([jnp.where(lo, xdj, 0.0), jnp.where(lo, 0.0, xdj)], axis=0).astype(BF16)
                hprev = h_ref[j]
                hp_ref[j] = hprev.astype(BF16)
                ms, bws_t = [], []
                for hh in (2 * j, 2 * j + 1):
                    csc = cs[:, hh:hh + 1]
                    cs_row = q["cs_t"][hh:hh + 1, :]
                    seg = jnp.broadcast_to(csc, (CHUNK, CHUNK)) - jnp.broadcast_to(cs_row, (CHUNK, CHUNK))
                    ms.append(s * jnp.exp(jnp.where(q["tril"], seg, -jnp.inf)))
                    bws_t.append(bg_t * jnp.exp(cs_row[:, CHUNK - 1:CHUNK] - cs_row))
                ydiag = _dot(jnp.concatenate(ms, axis=1).astype(BF16), x2)
                st = _dot(jnp.concatenate(bws_t, axis=1).astype(BF16), x2)
                ecs = ecs_l[:, sl]
                yoff = _dot(cgb, hprev.astype(BF16)) * ecs
                h_ref[j] = hprev * ecs[CHUNK - 1:CHUNK] + st
                ys.append(ydiag + yoff)
        y = jnp.concatenate(ys, axis=1) + dsk_ref[...] * xs
        y2_ref[...] = y.astype(BF16)
        zv = z_ref[...].astype(F32)
        y3 = y * (zv * _sigmoid(zv))
        cat_ref[:, 0:d] = ya_ref[...]
        for gi in range(SSM_GROUPS):
            seg = y3[:, gi * gw:(gi + 1) * gw]
            cat_ref[:, d + gi * gw:d + (gi + 1) * gw] = (seg * _rms(seg) * gs_ref[:, gi * gw:(gi + 1) * gw]).astype(BF16)

    chunk = lambda w, cb=0: pl.BlockSpec((nseq, CHUNK, w), lambda c, cb=cb: (0, c, cb))
    vec = lambda w: pl.BlockSpec((1, w), lambda c: (0, 0))
    hp_spec = pl.BlockSpec((nseq, None, npair, D_STATE, LANES), lambda c: (0, c, 0, 0, 0))
    per_seq = lambda a: a.reshape(nseq, seq, a.shape[1])
    cat, y2, hp = _call(
        body, name=name, grid=(nc,),
        in_specs=[chunk(xbc), chunk(LANES), chunk(d, 3), chunk(d), vec(h), vec(h), vec(d), vec(d),
                  pl.BlockSpec((h, d), lambda c: (0, 0))],
        out_specs=[chunk(2 * d), chunk(d), hp_spec],
        out_shape=[jax.ShapeDtypeStruct((nseq, seq, 2 * d), BF16), jax.ShapeDtypeStruct((nseq, seq, d), BF16),
                   jax.ShapeDtypeStruct((nseq, nc, npair, D_STATE, LANES), BF16)],
        scratch_shapes=[pltpu.VMEM((nseq, npair, D_STATE, LANES), F32)],
        compiler_params=_params(("arbitrary",)))(
            per_seq(cpre), per_seq(pdt), per_seq(pz), per_seq(ya), dtb, alog, dsk_lane, gs, emat)
    return cat.reshape(t, 2 * d), y2.reshape(t, d), hp


def _ssd_bwd(cpre, pdt, pz, y2, hprev_all, dcat, dtb, alog, dsk_lane, gs, emat, dproj, *, nseq, seq, name):
    t, xbc = cpre.shape
    d = y2.shape[1]
    h = d // HEAD_DIM
    npair = h // 2
    ppg = npair // SSM_GROUPS
    nc = seq // CHUNK
    gw = d // SSM_GROUPS
    bc0 = d
    cc0 = d + SSM_GROUPS * D_STATE

    def body(c_ref, pdt_ref, z_ref, y2_ref, hp_ref, dys_ref, dtb_ref, alog_ref, dsk_ref, gs_ref, e_ref, dproj_in,
             dconv_ref, dz_ref, dpdt_ref, dgs_ref, ddsk_ref, ddtb_ref, dalog_ref, dh_ref):
        del dproj_in
        b = pl.program_id(0)
        c = pl.program_id(1)

        @pl.when(c == 0)
        def _():
            dh_ref[...] = jnp.zeros_like(dh_ref)

        @pl.when((b == 0) & (c == 0))
        def _():
            dgs_ref[...] = jnp.zeros_like(dgs_ref)
            ddsk_ref[...] = jnp.zeros_like(ddsk_ref)
            ddtb_ref[...] = jnp.zeros_like(ddtb_ref)
            dalog_ref[...] = jnp.zeros_like(dalog_ref)

        q = _ssd_common(c_ref, pdt_ref, dtb_ref, alog_ref, e_ref, h)
        cp, sg, act, cs, a, dt, lo = q["cp"], q["sg"], q["act"], q["cs"], q["a"], q["dt"], q["lo"]
        ecs_l, dt_l = q["ecs_l"], q["dt_l"]
        ev = e_ref[...]
        xs = act[:, :d]
        xd = xs * dt_l
        row16 = lax.broadcasted_iota(jnp.int32, (CHUNK, h), 0)
        hid = lax.broadcasted_iota(jnp.int32, (1, h), 1)
        hid_t = lax.broadcasted_iota(jnp.int32, (h, 1), 0)

        zv = z_ref[...].astype(F32)
        sz = _sigmoid(zv)
        siluz = zv * sz
        y2v = y2_ref[...].astype(F32)
        y3 = y2v * siluz
        dysv = dys_ref[...].astype(F32)
        dy3s = []
        for gi in range(SSM_GROUPS):
            gsl = slice(gi * gw, (gi + 1) * gw)
            seg = y3[:, gsl]
            dseg, dgt = _rms_bwd(seg, _rms(seg), gs_ref[:, gsl], dysv[:, gsl])
            dy3s.append(dseg)
            dgs_ref[:, gsl] += jnp.sum(dgt, axis=0, keepdims=True)
        dy3 = jnp.concatenate(dy3s, axis=1)
        dy = dy3 * siluz
        dz_ref[...] = (dy3 * y2v * (sz * (1.0 + zv * (1.0 - sz)))).astype(BF16)
        ddsk_ref[...] += jnp.sum(_head_sums(dy * xs, ev), axis=0, keepdims=True)

        dcs = jnp.zeros((CHUNK, h), F32)
        dcs_t = jnp.zeros((h, CHUNK), F32)
        dxd_parts, yoff_parts, db_parts, dc_parts = [], [], [], []
        for g in range(SSM_GROUPS):
            bg = act[:, bc0 + g * D_STATE: bc0 + (g + 1) * D_STATE]
            cg = act[:, cc0 + g * D_STATE: cc0 + (g + 1) * D_STATE]
            bgb, cgb = bg.astype(BF16), cg.astype(BF16)
            cgb_t = cg.T.astype(BF16)
            s = _dot_nt(cgb, bgb)
            ds = jnp.zeros((CHUNK, CHUNK), F32)
            dbg = jnp.zeros((CHUNK, D_STATE), F32)
            dcg = jnp.zeros((CHUNK, D_STATE), F32)
            for jj in range(ppg):
                j = g * ppg + jj
                sl = slice(LANES * j, LANES * (j + 1))
                xdj = xd[:, sl]
                xdb = xdj.astype(BF16)
                x2 = jnp.concatenate([jnp.where(lo, xdj, 0.0), jnp.where(lo, 0.0, xdj)], axis=0).astype(BF16)
                dyj = dy[:, sl]
                dy2 = jnp.concatenate([jnp.where(lo, dyj, 0.0), jnp.where(lo, 0.0, dyj)], axis=0).astype(BF16)
                hpb = hp_ref[j]
                hprev = hpb.astype(F32)
                dhn = dh_ref[j]
                dhb = dhn.astype(BF16)
                dh2 = jnp.concatenate([jnp.where(lo, dhn, 0.0), jnp.where(lo, 0.0, dhn)], axis=0).astype(BF16)
                ecs = ecs_l[:, sl]
                gmat = (dyj * ecs).astype(BF16)
                yoff_parts.append(_dot(cgb, hpb) * ecs)
                dcg = dcg + _dot_nt(gmat, hpb)
                dh_ref[j] = dhn * ecs[CHUNK - 1:CHUNK] + _dot(cgb_t, gmat)
                t2 = dhn * hprev
                dbw2 = _dot_nt(x2, dhb)
                dm2 = _dot_nt(dy2, xdb)
                ms, bws = [], []
                for idx, hh in enumerate((2 * j, 2 * j + 1)):
                    msk = lo if idx == 0 else jnp.logical_not(lo)
                    onehot = (hid == hh).astype(F32)
                    csc = cs[:, hh:hh + 1]
                    seg = jnp.broadcast_to(csc, (CHUNK, CHUNK)) - jnp.broadcast_to(q["cs_t"][hh:hh + 1, :], (CHUNK, CHUNK))
                    lm = jnp.exp(jnp.where(q["tril"], seg, -jnp.inf))
                    m = s * lm
                    cs_last = cs[CHUNK - 1:CHUNK, hh:hh + 1]
                    dte = jnp.exp(cs_last - csc)
                    ms.append(m)
                    bws.append(bg * dte)
                    dbw = dbw2[idx * CHUNK:(idx + 1) * CHUNK]
                    dbg = dbg + dbw * dte
                    qv = jnp.sum(dbw * bg, axis=-1, keepdims=True) * dte
                    dm = dm2[idx * CHUNK:(idx + 1) * CHUNK]
                    wm = dm * m
                    rc = jnp.sum(wm, axis=-1, keepdims=True)
                    dcs_t = dcs_t - (hid_t == hh).astype(F32) * jnp.sum(wm, axis=0, keepdims=True)
                    ds = ds + dm * lm
                    ddec = jnp.sum(jnp.where(msk, t2, 0.0)) * jnp.exp(cs_last)
                    last = jnp.sum(qv) + ddec
                    dcs = dcs + (rc - qv) * onehot + jnp.where(row16 == CHUNK - 1, last * onehot, 0.0)
                dxd_s = _dot(jnp.concatenate(bws, axis=1).astype(BF16), dh2)
                dxd_d = _dot_tn(jnp.concatenate(ms, axis=0).astype(BF16), dy2)
                dxd_parts.append(dxd_s + dxd_d)
            dsb = ds.astype(BF16)
            dc_parts.append(dcg + _dot(dsb, bgb))
            db_parts.append(dbg + _dot_tn(dsb, cgb))
        yoff_all = jnp.concatenate(yoff_parts, axis=1)
        dxd_all = jnp.concatenate(dxd_parts, axis=1)
        dcs = dcs + _head_sums(dy * yoff_all, ev)
        triu = (q["col"] >= q["row"]).astype(F32)
        dadt = (jnp.dot(triu, dcs, precision=HIGHEST, preferred_element_type=F32)
                + lax.dot_general(triu, dcs_t, (((1,), (1,)), ((), ())), precision=HIGHEST, preferred_element_type=F32))
        ddt = dadt * a + _head_sums(dxd_all * xs, ev)
        dalog_ref[...] += jnp.sum(dadt * dt, axis=0, keepdims=True) * a
        dpre = ddt * _sigmoid(q["pre"])
        ddtb_ref[...] += jnp.sum(dpre, axis=0, keepdims=True)
        dpdt_ref[...] = jnp.zeros_like(dpdt_ref)
        dpdt_ref[:, 0:h] = dpre.astype(BF16)
        dxs = dxd_all * dt_l + dy * dsk_ref[...]
        dact = jnp.concatenate([dxs] + db_parts + dc_parts, axis=1)
        dconv_ref[...] = (dact * (sg * (1.0 + cp * (1.0 - sg)))).astype(BF16)

    rchunk = lambda w, cb=0: pl.BlockSpec((CHUNK, w), lambda b, c, cb=cb: (b * nc + nc - 1 - c, cb))
    vec = lambda w: pl.BlockSpec((1, w), lambda b, c: (0, 0))
    hp_spec = pl.BlockSpec((None, None, npair, D_STATE, LANES), lambda b, c: (b, nc - 1 - c, 0, 0, 0))
    return _call(body, name=name, grid=(nseq, nc),
                 in_specs=[rchunk(xbc), rchunk(LANES), rchunk(d, 3), rchunk(d), hp_spec, rchunk(d, 1),
                           vec(h), vec(h), vec(d), vec(d), pl.BlockSpec((h, d), lambda b, c: (0, 0)),
                           pl.BlockSpec(memory_space=pl.ANY)],
                 out_specs=[rchunk(xbc), rchunk(d, 3), rchunk(LANES), vec(d), vec(h), vec(h), vec(h)],
                 out_shape=[jax.ShapeDtypeStruct((t, xbc), BF16), jax.ShapeDtypeStruct(dproj.shape, BF16),
                            jax.ShapeDtypeStruct((t, LANES), BF16), jax.ShapeDtypeStruct((1, d), F32),
                            jax.ShapeDtypeStruct((1, h), F32), jax.ShapeDtypeStruct((1, h), F32),
                            jax.ShapeDtypeStruct((1, h), F32)],
                 input_output_aliases={11: 1},
                 scratch_shapes=[pltpu.VMEM((npair, D_STATE, LANES), F32)],
                 compiler_params=_params(("arbitrary", "arbitrary")))(
                     cpre, pdt, pz, y2, hprev_all, dcat, dtb, alog, dsk_lane, gs, emat, dproj)


def _sum_adamw(parts, w, m, v, *, name, layer=None, outs=None):
    n, r, c = parts.shape
    tr = _pick_rows(r, 256)
    bc1 = 1.0 - ADAM_B1 ** ADAM_STEP
    bc2 = 1.0 - ADAM_B2 ** ADAM_STEP

    def body(p_ref, w_ref, m_ref, v_ref, *rest):
        g_ref, d_ref, mo_ref, vo_ref = rest[-4:]
        g = p_ref[0].astype(F32)
        for k in range(1, n):
            g = g + p_ref[k].astype(F32)
        mn = ADAM_B1 * m_ref[...] + (1.0 - ADAM_B1) * g
        vn = ADAM_B2 * v_ref[...] + (1.0 - ADAM_B2) * (g * g)
        g_ref[...] = g
        mo_ref[...] = mn
        vo_ref[...] = vn
        d_ref[...] = -ADAM_LR * ((mn / bc1) / (jnp.sqrt(vn / bc2) + ADAM_EPS) + ADAM_WD * w_ref[...])

    p_spec = pl.BlockSpec((n, tr, c), lambda i: (0, i, 0))
    if layer is None:
        blk = pl.BlockSpec((tr, c), lambda i: (i, 0))
        return _call(body, name=name, grid=(r // tr,), in_specs=[p_spec, blk, blk, blk], out_specs=[blk] * 4,
                     out_shape=[jax.ShapeDtypeStruct((r, c), F32)] * 4,
                     compiler_params=_params(("parallel",)))(parts, w, m, v)
    blk = pl.BlockSpec((None, tr, c), lambda i: (layer, i, 0))
    if outs is None:
        outs = [lax.empty(w.shape, F32) for _ in range(4)]
    return _call(body, name=name, grid=(r // tr,),
                 in_specs=[p_spec, blk, blk, blk] + [pl.BlockSpec(memory_space=pl.ANY)] * 4, out_specs=[blk] * 4,
                 out_shape=[jax.ShapeDtypeStruct(w.shape, F32)] * 4, input_output_aliases={4 + k: k for k in range(4)},
                 compiler_params=_params(("parallel",)))(parts, w, m, v, *outs)


def _assemble_cols(blocks, *, name):
    nb, r, c = blocks.shape
    width = -(-nb * c // LANES) * LANES
    tr = _pick_rows(r, 256)

    def body(b_ref, o_ref):
        pieces = [b_ref[j] for j in range(nb)]
        if width > nb * c:
            pieces.append(jnp.zeros((tr, width - nb * c), blocks.dtype))
        o_ref[...] = jnp.concatenate(pieces, axis=1)

    return _call(body, name=name, grid=(r // tr,), in_specs=[pl.BlockSpec((nb, tr, c), lambda i: (0, i, 0))],
                 out_specs=pl.BlockSpec((tr, width), lambda i: (i, 0)), out_shape=jax.ShapeDtypeStruct((r, width), blocks.dtype),
                 compiler_params=_params(("parallel",)))(blocks)


def _split_cols(pieces, c, *, name):
    r = pieces[0].shape[0]
    tr = _pick_rows(r, 256)
    n_in = len(pieces)

    def body(*refs):
        o_ref = refs[n_in]
        x = jnp.concatenate([p[...] for p in refs[:n_in]], axis=1) if n_in > 1 else refs[0][...]
        for j in range(N_DEV):
            o_ref[j] = x[:, c * j:c * (j + 1)]

    return _call(body, name=name, grid=(r // tr,),
                 in_specs=[pl.BlockSpec((tr, p.shape[1]), lambda i: (i, 0)) for p in pieces],
                 out_specs=pl.BlockSpec((N_DEV, tr, c), lambda i: (0, i, 0)),
                 out_shape=jax.ShapeDtypeStruct((N_DEV, r, c), pieces[0].dtype),
                 compiler_params=_params(("parallel",)))(*pieces)


def _sum_parts(parts, *, name):
    n, r, c = parts.shape
    tr = _pick_rows(r, 256)

    def body(p_ref, g_ref):
        g = p_ref[0].astype(F32)
        for k in range(1, n):
            g = g + p_ref[k].astype(F32)
        g_ref[...] = g

    return _call(body, name=name, grid=(r // tr,), in_specs=[pl.BlockSpec((n, tr, c), lambda i: (0, i, 0))],
                 out_specs=pl.BlockSpec((tr, c), lambda i: (i, 0)), out_shape=jax.ShapeDtypeStruct((r, c), F32),
                 compiler_params=_params(("parallel",)))(parts)


def _peers():
    x, y, c = lax.axis_index("x"), lax.axis_index("y"), lax.axis_index("c")
    me = 4 * x + 2 * y + c
    out = []
    for k in range(1, N_DEV):
        px = (1 - x) if (k >> 2) & 1 else x
        py = (1 - y) if (k >> 1) & 1 else y
        pc = (1 - c) if k & 1 else c
        out.append(((px, py, pc), 4 * px + 2 * py + pc))
    return me, out


_HBM = pl.BlockSpec(memory_space=pltpu.HBM)
_SEM = pl.BlockSpec(memory_space=pltpu.SEMAPHORE)
_EFFECT = pltpu.SideEffectType.DATAFLOW_SIDE_EFFECTING


ALL_PEERS = tuple(range(1, N_DEV))
SAME_CORE_PEERS = (2, 4, 6)


def _split_copies(s_refs, l_refs, send_sems, recv_sems, gather, incoming, ks=ALL_PEERS):
    me, peers = _peers()
    local, remote = [], []
    for ti, (s_ref, l_ref) in enumerate(zip(s_refs, l_refs)):
        base = ti * N_DEV
        local.append(pltpu.make_async_copy(s_ref if gather else s_ref.at[me], l_ref.at[me], recv_sems.at[base + N_DEV - 1]))
        for k, (dev, pid) in enumerate(peers):
            if k + 1 not in ks:
                continue
            sems = dict(send_sem=send_sems.at[base + k], recv_sem=recv_sems.at[base + k], device_id=dev, device_id_type=MESH)
            src = s_ref if gather else s_ref.at[pid]
            remote.append((
                pltpu.make_async_remote_copy(src_ref=src, dst_ref=l_ref.at[me], **sems),
                pltpu.make_async_remote_copy(src_ref=src, dst_ref=l_ref.at[pid], **sems) if incoming else None))
    return local, remote


def _exchange_start(srcs, *, gather, name, after=(), ks=ALL_PEERS):
    n = len(srcs)
    after = list(after)
    srcs = [pltpu.with_memory_space_constraint(s, pltpu.HBM) for s in srcs]
    lands = [pltpu.with_memory_space_constraint(
        lax.empty((N_DEV,) + tuple(s.shape if gather else s.shape[1:]), s.dtype), pltpu.HBM) for s in srcs]

    def body(*refs):
        s_refs, l_refs = refs[:n], refs[n:2 * n]
        outs = refs[2 * n + len(after):]
        send_sems, recv_sems, token = outs[0], outs[1], outs[-1]
        local, remote = _split_copies(s_refs, l_refs, send_sems, recv_sems, gather, incoming=False, ks=ks)
        for cp in local:
            cp.start()
        for out_cp, _ in remote:
            out_cp.start()
        token[...] = jnp.zeros_like(token)

    outs = _call(
        body, name=name,
        out_shape=(pltpu.SemaphoreType.DMA((n * N_DEV,)), pltpu.SemaphoreType.DMA((n * N_DEV,)),
                   *[pltpu.HBM(s.shape, s.dtype) for s in srcs], *[pltpu.HBM(l.shape, l.dtype) for l in lands],
                   jax.ShapeDtypeStruct((SUBLANES, LANES), F32)),
        in_specs=[_HBM] * (2 * n) + [pl.BlockSpec(memory_space=pl.ANY)] * len(after),
        out_specs=(_SEM, _SEM, *[_HBM] * (2 * n), pl.BlockSpec(memory_space=pltpu.VMEM)),
        input_output_aliases={k: k + 2 for k in range(2 * n)},
        compiler_params=pltpu.CompilerParams(has_side_effects=_EFFECT),
    )(*srcs, *lands, *after)
    return dict(n=n, gather=gather, ks=ks, sems=outs[:2], srcs=outs[2:2 + n], lands=outs[2 + n:2 + 2 * n]), outs[-1]


def _exchange_wait(state, after, *, name):
    n, gather, ks = state["n"], state["gather"], state["ks"]
    after = list(after)

    def body(*refs):
        s_refs, l_refs = refs[:n], refs[n:2 * n]
        send_sems, recv_sems = refs[2 * n], refs[2 * n + 1]
        local, remote = _split_copies(s_refs, l_refs, send_sems, recv_sems, gather, incoming=True, ks=ks)
        for out_cp, in_cp in remote:
            out_cp.wait_send()
            in_cp.wait_recv()
        for cp in local:
            cp.wait()

    outs = _call(
        body, name=name,
        out_shape=tuple(pltpu.HBM(a.shape, a.dtype) for a in (*state["srcs"], *state["lands"])),
        in_specs=[_HBM] * (2 * n) + [_SEM, _SEM] + [pl.BlockSpec(memory_space=pl.ANY)] * len(after),
        out_specs=tuple([_HBM] * (2 * n)),
        input_output_aliases={k: k for k in range(2 * n)},
        compiler_params=pltpu.CompilerParams(has_side_effects=_EFFECT),
    )(*state["srcs"], *state["lands"], *state["sems"], *after)
    return outs[n:]


def _sibling_copies(l_refs, send_sems, recv_sems, incoming):
    x, y, c = lax.axis_index("x"), lax.axis_index("y"), lax.axis_index("c")
    out = []
    for ti, l_ref in enumerate(l_refs):
        for q in range(4):
            px = (1 - x) if q & 2 else x
            py = (1 - y) if q & 1 else y
            mine, theirs = 4 * px + 2 * py + c, 4 * px + 2 * py + (1 - c)
            sems = dict(send_sem=send_sems.at[4 * ti + q], recv_sem=recv_sems.at[4 * ti + q],
                        device_id=(x, y, 1 - c), device_id_type=MESH)
            out.append((
                pltpu.make_async_remote_copy(src_ref=l_ref.at[mine], dst_ref=l_ref.at[mine], **sems),
                pltpu.make_async_remote_copy(src_ref=l_ref.at[mine], dst_ref=l_ref.at[theirs], **sems) if incoming else None))
    return out


def _sibling_start(lands, *, name, after=()):
    n = len(lands)
    after = list(after)
    lands = [pltpu.with_memory_space_constraint(l, pltpu.HBM) for l in lands]

    def body(*refs):
        l_refs = refs[:n]
        outs = refs[n + len(after):]
        for out_cp, _ in _sibling_copies(l_refs, outs[0], outs[1], incoming=False):
            out_cp.start()
        outs[-1][...] = jnp.zeros_like(outs[-1])

    outs = _call(
        body, name=name,
        out_shape=(pltpu.SemaphoreType.DMA((4 * n,)), pltpu.SemaphoreType.DMA((4 * n,)),
                   *[pltpu.HBM(l.shape, l.dtype) for l in lands], jax.ShapeDtypeStruct((SUBLANES, LANES), F32)),
        in_specs=[_HBM] * n + [pl.BlockSpec(memory_space=pl.ANY)] * len(after),
        out_specs=(_SEM, _SEM, *[_HBM] * n, pl.BlockSpec(memory_space=pltpu.VMEM)),
        input_output_aliases={k: k + 2 for k in range(n)},
        compiler_params=pltpu.CompilerParams(has_side_effects=_EFFECT),
    )(*lands, *after)
    return dict(n=n, sems=outs[:2], lands=outs[2:2 + n]), outs[-1]


def _sibling_wait(state, after, *, name):
    n = state["n"]
    after = list(after)

    def body(*refs):
        l_refs = refs[:n]
        for out_cp, in_cp in _sibling_copies(l_refs, refs[n], refs[n + 1], incoming=True):
            out_cp.wait_send()
            in_cp.wait_recv()

    return _call(
        body, name=name,
        out_shape=tuple(pltpu.HBM(a.shape, a.dtype) for a in state["lands"]),
        in_specs=[_HBM] * n + [_SEM, _SEM] + [pl.BlockSpec(memory_space=pl.ANY)] * len(after),
        out_specs=tuple([_HBM] * n), input_output_aliases={k: k for k in range(n)},
        compiler_params=pltpu.CompilerParams(has_side_effects=_EFFECT),
    )(*state["lands"], *state["sems"], *after)


def _pack(arrs):
    flat = jnp.concatenate([a.reshape(-1).astype(F32) for a in arrs])
    pad = (-flat.shape[0]) % (SUBLANES * LANES)
    return jnp.pad(flat, (0, pad)).reshape(-1, LANES)


def _unpack(packed, shapes):
    flat = packed.reshape(-1)
    out, off = [], 0
    for s in shapes:
        n = 1
        for v in s:
            n *= v
        out.append(flat[off:off + n].reshape(s))
        off += n
    return out


SMALL = ("norm_mix_pre", "ssm_conv_b", "dt_bias", "a_log", "d_skip", "conv_out_norm", "ssm_out_norm",
         "norm_mix_post", "norm_mlp_pre", "norm_mlp_post", "conv_a_w", "ssm_conv_w")
BIG = ("w_in", "w_out", "w_up", "w_down")
ORDER = ("norm_mix_pre", "w_in", "conv_a_w", "ssm_conv_w", "ssm_conv_b", "dt_bias", "a_log", "d_skip",
         "conv_out_norm", "ssm_out_norm", "w_out", "norm_mix_post", "norm_mlp_pre", "w_up", "w_down", "norm_mlp_post")


def kernel(x, norm_mix_pre, w_in, conv_a_w, ssm_conv_w, ssm_conv_b, dt_bias, a_log, d_skip, conv_out_norm, ssm_out_norm, w_out, norm_mix_post, norm_mlp_pre, w_up, w_down, norm_mlp_post, loss_target, m_norm_mix_pre, m_w_in, m_conv_a_w, m_ssm_conv_w, m_ssm_conv_b, m_dt_bias, m_a_log, m_d_skip, m_conv_out_norm, m_ssm_out_norm, m_w_out, m_norm_mix_post, m_norm_mlp_pre, m_w_up, m_w_down, m_norm_mlp_post, v_norm_mix_pre, v_w_in, v_conv_a_w, v_ssm_conv_w, v_ssm_conv_b, v_dt_bias, v_a_log, v_d_skip, v_conv_out_norm, v_ssm_out_norm, v_w_out, v_norm_mix_post, v_norm_mlp_pre, v_w_up, v_w_down, v_norm_mlp_post):
    W = dict(norm_mix_pre=norm_mix_pre, w_in=w_in, conv_a_w=conv_a_w, ssm_conv_w=ssm_conv_w, ssm_conv_b=ssm_conv_b,
             dt_bias=dt_bias, a_log=a_log, d_skip=d_skip, conv_out_norm=conv_out_norm, ssm_out_norm=ssm_out_norm,
             w_out=w_out, norm_mix_post=norm_mix_post, norm_mlp_pre=norm_mlp_pre, w_up=w_up, w_down=w_down,
             norm_mlp_post=norm_mlp_post)
    M = dict(norm_mix_pre=m_norm_mix_pre, w_in=m_w_in, conv_a_w=m_conv_a_w, ssm_conv_w=m_ssm_conv_w,
             ssm_conv_b=m_ssm_conv_b, dt_bias=m_dt_bias, a_log=m_a_log, d_skip=m_d_skip,
             conv_out_norm=m_conv_out_norm, ssm_out_norm=m_ssm_out_norm, w_out=m_w_out,
             norm_mix_post=m_norm_mix_post, norm_mlp_pre=m_norm_mlp_pre, w_up=m_w_up, w_down=m_w_down,
             norm_mlp_post=m_norm_mlp_post)
    V = dict(norm_mix_pre=v_norm_mix_pre, w_in=v_w_in, conv_a_w=v_conv_a_w, ssm_conv_w=v_ssm_conv_w,
             ssm_conv_b=v_ssm_conv_b, dt_bias=v_dt_bias, a_log=v_a_log, d_skip=v_d_skip,
             conv_out_norm=v_conv_out_norm, ssm_out_norm=v_ssm_out_norm, w_out=v_w_out,
             norm_mix_post=v_norm_mix_post, norm_mlp_pre=v_norm_mlp_pre, w_up=v_w_up, w_down=v_w_down,
             norm_mlp_post=v_norm_mlp_post)

    nseq, seq, d = x.shape
    t = nseq * seq
    depth = w_in.shape[0]
    h = d // HEAD_DIM
    xbc = d + 2 * SSM_GROUPS * D_STATE
    in_cols = w_in.shape[2] * N_DEV
    d_mix = w_out.shape[1] * N_DEV
    d_ff = w_up.shape[2] * N_DEV
    me = 4 * lax.axis_index("x") + 2 * lax.axis_index("y") + lax.axis_index("c")
    ca_shard = conv_a_w.shape[2]
    sc_shard = ssm_conv_w.shape[2]

    tap_shapes = [conv_a_w.shape[1:], ssm_conv_w.shape[1:]]

    def gather_start(i, after=()):
        ks = SAME_CORE_PEERS if i == 0 else ALL_PEERS
        st_in, tok_in = _exchange_start([w_in[i].astype(BF16), _pack([conv_a_w[i], ssm_conv_w[i]])], gather=True,
                                        name=f"gather_start_in_{i}", after=after, ks=ks)
        st_rest, tok_rest = _exchange_start([W[n][i].astype(BF16) for n in ("w_out", "w_up", "w_down")], gather=True,
                                            name=f"gather_start_rest_{i}", after=[tok_in], ks=ks)
        return st_in, st_rest, tok_rest

    vec = lambda name, i: W[name][i].reshape(1, -1)
    emat = (lax.broadcasted_iota(jnp.int32, (h, d), 1) // HEAD_DIM == lax.broadcasted_iota(jnp.int32, (h, d), 0)).astype(F32)

    xcur = x.reshape(t, d)
    hcur = _norm_fwd(xcur, vec("norm_mix_pre", 0), name="norm_first")
    saved = []
    nxt = gather_start(0)
    for i in range(depth):
        st_in, st_rest, tok = nxt
        landed = _exchange_wait(st_in, [hcur, tok], name=f"gather_wait_in_{i}")
        if i == 0:
            st_sib, tok_sib = _sibling_start(landed, name="gather_sibling_start_in_0")
            landed = _sibling_wait(st_sib, [tok_sib], name="gather_sibling_wait_in_0")
        win_g, taps_g = landed
        win = _assemble_cols(win_g, name=f"assemble_w_in_{i}")
        taps_j = [_unpack(taps_g[j], tap_shapes) for j in range(N_DEV)]
        conv_a_i = jnp.concatenate([tj[0] for tj in taps_j], axis=1)
        ssm_conv_i = jnp.concatenate([tj[1] for tj in taps_j], axis=1)
        proj = _mm(hcur, win, n=4 * d + xbc, name=f"fwd_proj_{i}", out_dtypes=(BF16,))
        pdt = _mm(hcur, win, n=LANES, b_off=4 * d + xbc, name=f"fwd_proj_dt_{i}")
        ya, va = _conva_fwd(proj, conv_a_i, vec("conv_out_norm", i), d=d, seq=seq, name=f"fwd_conv_a_{i}")
        cpre = _convb_fwd(proj, ssm_conv_i, vec("ssm_conv_b", i), col0=4 * d, seq=seq, name=f"fwd_conv_b_{i}")
        dsk_lane = jnp.repeat(W["d_skip"][i], HEAD_DIM).reshape(1, d)
        dtb = vec("dt_bias", i)
        if i == 0:
            st_sib, tok_sib = _sibling_start(_exchange_wait(st_rest, [cpre], name="gather_wait_rest_0"),
                                             name="gather_sibling_start_rest_0")
            dtb = dtb + tok_sib[0:1, 0:1]
        cat, y2, hprev = _ssd_fwd(cpre, pdt, proj, ya, dtb, vec("a_log", i), dsk_lane,
                                  vec("ssm_out_norm", i), emat, nseq=nseq, seq=seq, name=f"fwd_ssd_{i}")
        if i == 0:
            wout_g, wup_g, wdown_g = _sibling_wait(st_sib, [cat], name="gather_sibling_wait_rest_0")
        else:
            wout_g, wup_g, wdown_g = _exchange_wait(st_rest, [cat], name=f"gather_wait_rest_{i}")
        lw = dict(win=win, wout=wout_g.reshape(d_mix, d),
                  wup=_assemble_cols(wup_g, name=f"assemble_w_up_{i}"), wdown=wdown_g.reshape(d_ff, d),
                  conv_a=conv_a_i, ssm_conv=ssm_conv_i)
        after = []
        if i + 1 < depth:
            nxt = gather_start(i + 1, after=[wout_g])
            after = [nxt[2]]
        x1, h2, mix = _mm(cat, lw["wout"], name=f"fwd_out_{i}", after=after, out_dtypes=(F32, BF16, BF16),
                          epi=_epi_resid_norm, extras=(xcur,), vecs=(vec("norm_mix_post", i), vec("norm_mlp_pre", i)),
                          tm_cap=FUSED_ROWS)
        f = _mm(h2, lw["wup"], name=f"fwd_up_{i}", out_dtypes=(BF16,), epi=_epi_relu2)
        g_next = vec("norm_mix_pre", i + 1) if i + 1 < depth else vec("norm_mix_pre", 0)
        dn = _mm(f, lw["wdown"], name=f"fwd_down_{i}", out_dtypes=(BF16,))
        x2, hnext = _resid_norm(x1, dn, vec("norm_mlp_post", i), g_next, name=f"fwd_post_mlp_{i}")
        saved.append(dict(lw=lw, x0=xcur, h=hcur, proj=proj, pdt=pdt, va=va, cpre=cpre, y2=y2,
                          hprev=hprev, cat=cat, mix=mix, x1=x1, h2=h2, f=f, dn=dn, dsk_lane=dsk_lane))
        xcur, hcur = x2, hnext

    dx, loss_part = _loss_fwd_bwd(xcur, loss_target.reshape(t, d), name="loss")
    loss = lax.psum(loss_part[0, 0], ("x", "y", "c"))

    small_grads = {n: [None] * depth for n in SMALL}
    big_out = {n: None for n in BIG}

    def finish(pending, after):
        li, st_a, st_b = pending

        def update(n, parts):
            big_out[n] = _sum_adamw(parts, W[n], M[n], V[n], layer=li, outs=big_out[n], name=f"adamw_{n}_{li}")

        p_down, p_up, p_out = _exchange_wait(st_a, after, name=f"scatter_wait_a_{li}")
        update("w_down", p_down)
        update("w_up", p_up)
        update("w_out", p_out)
        p_in, = _exchange_wait(st_b, after + [big_out["w_out"][0]], name=f"scatter_wait_b_{li}")
        update("w_in", p_in)

    pending = None
    for i in reversed(range(depth)):
        s = saved[i]
        lw = s["lw"]
        ddn, dg = _bwd_norm_out(s["dn"], vec("norm_mlp_post", i), dx, name=f"bwd_norm_mlp_post_{i}")
        small_grads["norm_mlp_post"][i] = dg
        dup = _mm(ddn, lw["wdown"], tb=True, name=f"bwd_down_dx_{i}", out_dtypes=(BF16,), epi=_epi_drelu2,
                  extras=(s["f"],))
        g_wdown = _mm(s["f"], ddn, ta=True, name=f"bwd_down_dw_{i}", out_dtypes=(BF16,))
        dh2 = _mm(dup, lw["wup"], tb=True, name=f"bwd_up_dx_{i}", out_dtypes=(BF16,))
        g_wup = _mm(s["h2"], dup, ta=True, name=f"bwd_up_dw_{i}", out_dtypes=(BF16,))
        dx1, dmix, dg_pre, dg_post = _bwd_norm_pair(s["x1"], [dh2], dx, s["mix"], vec("norm_mlp_pre", i),
                                                    vec("norm_mix_post", i), name=f"bwd_norm_mix_post_{i}")
        small_grads["norm_mlp_pre"][i] = dg_pre
        small_grads["norm_mix_post"][i] = dg_post
        dcat = _mm(dmix, lw["wout"], tb=True, name=f"bwd_out_dx_{i}", out_dtypes=(BF16,))
        g_wout = _mm(s["cat"], dmix, ta=True, name=f"bwd_out_dw_{i}", out_dtypes=(BF16,))
        st_a, tok_a = _exchange_start(
            [g_wdown.reshape(N_DEV, d_ff // N_DEV, d), _split_cols([g_wup], d_ff // N_DEV, name=f"split_g_w_up_{i}"),
             g_wout.reshape(N_DEV, d_mix // N_DEV, d)], gather=False, name=f"scatter_start_a_{i}")
        dproj, dcaw, dgca = _conva_bwd(dcat, s["proj"], s["va"], lw["conv_a"],
                                       vec("conv_out_norm", i) + tok_a[0:1, 0:1], d=d, seq=seq, name=f"bwd_conv_a_{i}")
        small_grads["conv_a_w"][i] = dcaw
        small_grads["conv_out_norm"][i] = dgca
        dconv, dproj, dpdt, dgs, ddsk, ddtb, dalog = _ssd_bwd(
            s["cpre"], s["pdt"], s["proj"], s["y2"], s["hprev"], dcat, vec("dt_bias", i), vec("a_log", i),
            s["dsk_lane"], vec("ssm_out_norm", i), emat, dproj, nseq=nseq, seq=seq, name=f"bwd_ssd_{i}")
        small_grads["ssm_out_norm"][i] = dgs
        small_grads["d_skip"][i] = ddsk
        small_grads["dt_bias"][i] = ddtb
        small_grads["a_log"][i] = dalog
        dproj, dscw, dscb = _convb_bwd(dconv, s["proj"], lw["ssm_conv"], dproj, col0=4 * d, seq=seq,
                                       name=f"bwd_conv_b_{i}")
        small_grads["ssm_conv_w"][i] = dscw
        small_grads["ssm_conv_b"][i] = dscb
        g_win = _split_cols([
            _mm(s["h"], dproj, ta=True, name=f"bwd_proj_dw_{i}", out_dtypes=(BF16,)),
            _mm(s["h"], dpdt, ta=True, name=f"bwd_proj_dt_dw_{i}", out_dtypes=(BF16,))],
            in_cols // N_DEV, name=f"split_g_w_in_{i}")
        st_b, tok_b = _exchange_start([g_win], gather=False, name=f"scatter_start_b_{i}")
        dh_parts = [_mm(dp, lw["win"], tb=True, b_koff=off, name=f"bwd_proj_{nm}dx_{i}", after=[tok_b], out_dtypes=(BF16,))
                    for nm, dp, off in (("", dproj, 0), ("dt_", dpdt, 4 * d + xbc))]
        dx, dg_in = _bwd_norm_in(s["x0"], dh_parts, dx1, vec("norm_mix_pre", i), name=f"bwd_norm_mix_pre_{i}")
        small_grads["norm_mix_pre"][i] = dg_in
        if pending is not None:
            finish(pending, [dx])
        pending = (i, st_a, st_b)

    grad_x = dx.reshape(nseq, seq, d)

    small_shapes_full = {n: (depth,) + tuple(small_grads[n][0].shape) for n in SMALL}
    gpack = _pack([jnp.stack(small_grads[n]) for n in SMALL])
    st_small, tok_small = _exchange_start([gpack], gather=True, name="allreduce_small_start")
    finish(pending, [dx, tok_small])
    gparts, = _exchange_wait(st_small, [big_out["w_in"][0]], name="allreduce_small_wait")

    def shard_of(n, full):
        if n == "conv_a_w":
            return lax.dynamic_slice_in_dim(full, me * ca_shard, ca_shard, axis=2)
        if n == "ssm_conv_w":
            return lax.dynamic_slice_in_dim(full, me * sc_shard, sc_shard, axis=2)
        return full.reshape(W[n].shape)

    gsum = _sum_parts(gparts, name="sum_small")
    gfull = _unpack(gsum, [small_shapes_full[n] for n in SMALL])
    gsmall = {n: shard_of(n, gf) for n, gf in zip(SMALL, gfull)}
    res = _sum_adamw(_pack([gsmall[n] for n in SMALL])[None], _pack([W[n] for n in SMALL]),
                     _pack([M[n] for n in SMALL]), _pack([V[n] for n in SMALL]), name="adamw_small")
    small_out = [dict(zip(SMALL, _unpack(r, [W[n].shape for n in SMALL]))) for r in res]

    def out_of(kind, n):
        return big_out[n][kind] if n in BIG else small_out[kind][n]

    return (loss, grad_x, *[out_of(k, n) for k in range(4) for n in ORDER])
```

```python
import jax
import jax.numpy as jnp
from jax import lax
from jax.experimental import pallas as pl
from jax.experimental.pallas import tpu as pltpu

F32 = jnp.float32
BF16 = jnp.bfloat16
HIGHEST = lax.Precision.HIGHEST
MESH = pl.DeviceIdType.MESH

EPS = 1e-6
HEAD_DIM = 64
D_STATE = 128
SSM_GROUPS = 2
CHUNK = 128
CONV_K = 3
SSM_CONV_K = 4
ADAM_LR = 0.001
ADAM_B1 = 0.9
ADAM_B2 = 0.999
ADAM_EPS = 1e-08
ADAM_WD = 0.01
ADAM_STEP = 10

N_DEV = 8
LANES = 128
SUBLANES = 8
VMEM_LIMIT = 48 * 1024 * 1024
ROW_TILE = 512
MM_TILE = 1024
MM_TILE_N = 2816
MM_VMEM_BUDGET = 40 * 1024 * 1024
FUSED_ROWS = 512


def _params(sem):
    return pltpu.CompilerParams(dimension_semantics=sem, vmem_limit_bytes=VMEM_LIMIT)


def _call(body, **kw):
    return pl.pallas_call(body, **kw)


def _pick(n, cap):
    best = None
    for t in range(LANES, min(n, cap) + 1, LANES):
        if n % t == 0:
            best = t
    return best or n


def _pick_rows(n, cap):
    best = None
    for t in range(SUBLANES, min(n, cap) + 1, SUBLANES):
        if n % t == 0:
            best = t
    return best or n


def _sigmoid(x):
    return 1.0 / (1.0 + jnp.exp(-x))


def _softplus(x):
    return jnp.maximum(x, 0.0) + jnp.log1p(jnp.exp(-jnp.abs(x)))


def _rms(x):
    return lax.rsqrt(jnp.mean(x * x, axis=-1, keepdims=True) + EPS)


def _rms_bwd(x, r, g, dy):
    gy = dy * g
    dx = r * gy - x * (r * r * r) * jnp.mean(gy * x, axis=-1, keepdims=True)
    return dx, dy * x * r


def _full(shape):
    return pl.BlockSpec(shape, lambda *_: (0,) * len(shape))


def _mm(a, b, *, name, ta=False, tb=False, out_dtypes=(F32,), epi=None, extras=(), n=None, b_off=0, b_koff=0,
        after=(), vecs=(), tm_cap=MM_TILE):
    m, k = (a.shape[1], a.shape[0]) if ta else a.shape
    if n is None:
        n = b.shape[0] if tb else b.shape[1]
    tm, tn, tk = _pick(m, tm_cap), _pick(n, MM_TILE_N), _pick(k, MM_TILE)
    while b_off % tn or n % tn:
        tn -= LANES
    if b_koff == 0 and k > MM_TILE:
        tk = _pick(k, MM_TILE_N)
    while b_koff % tk or k % tk:
        tk -= LANES

    def vmem_bytes(tk_):
        per_out = sum(jnp.dtype(dt).itemsize for dt in out_dtypes) + sum(e.dtype.itemsize for e in extras)
        return 2 * tk_ * (tm * a.dtype.itemsize + tn * b.dtype.itemsize) + tm * tn * (2 * per_out + 4)

    while vmem_bytes(tk) > MM_VMEM_BUDGET and tk % (2 * LANES) == 0 and not b_koff % (tk // 2):
        tk //= 2
    nk = k // tk
    nm, nn = m // tm, n // tn
    jo = b_off // tn
    ko = b_koff // tk
    a_bytes = m * k * a.dtype.itemsize
    b_bytes = n * k * b.dtype.itemsize
    m_outer = a_bytes + nm * b_bytes <= b_bytes + nn * a_bytes
    ij = (lambda g0, g1: (g0, g1)) if m_outer else (lambda g0, g1: (g1, g0))
    grid = (nm, nn, nk) if m_outer else (nn, nm, nk)

    def a_map(g0, g1, kk):
        i, _ = ij(g0, g1)
        return (kk, i) if ta else (i, kk)

    def b_map(g0, g1, kk):
        _, j = ij(g0, g1)
        return (j + jo, kk + ko) if tb else (kk + ko, j + jo)

    def o_map(g0, g1, kk):
        return ij(g0, g1)

    a_spec = pl.BlockSpec((tk, tm) if ta else (tm, tk), a_map)
    b_spec = pl.BlockSpec((tn, tk) if tb else (tk, tn), b_map)
    o_spec = pl.BlockSpec((tm, tn), o_map)
    dims = (((0 if ta else 1,), (1 if tb else 0,)), ((), ()))
    n_ex = len(extras) + len(vecs)
    after = list(after)
    o0 = 2 + n_ex + len(after)

    def finish(acc, ex, outs):
        res = (acc,) if epi is None else epi(acc, *[e[...] for e in ex])
        for o, r in zip(outs, res):
            o[...] = r.astype(o.dtype)

    def body_single(*refs):
        a_ref, b_ref = refs[:2]
        acc = lax.dot_general(a_ref[...].astype(BF16), b_ref[...].astype(BF16), dims, preferred_element_type=F32)
        finish(acc, refs[2:2 + n_ex], refs[o0:])

    def body_multi(*refs):
        a_ref, b_ref = refs[:2]
        acc = refs[-1]
        kk = pl.program_id(2)

        @pl.when(kk == 0)
        def _():
            acc[...] = jnp.zeros_like(acc)

        acc[...] += lax.dot_general(a_ref[...].astype(BF16), b_ref[...].astype(BF16), dims, preferred_element_type=F32)

        @pl.when(kk == nk - 1)
        def _():
            finish(acc[...], refs[2:2 + n_ex], refs[o0:-1])

    v_spec = pl.BlockSpec((1, tn), lambda g0, g1, kk: (0, ij(g0, g1)[1]))
    outs = _call(
        body_single if nk == 1 else body_multi, name=name, grid=grid,
        in_specs=([a_spec, b_spec] + [o_spec] * len(extras) + [v_spec] * len(vecs)
                  + [pl.BlockSpec(memory_space=pl.ANY)] * len(after)),
        out_specs=[o_spec] * len(out_dtypes),
        out_shape=[jax.ShapeDtypeStruct((m, n), dt) for dt in out_dtypes],
        scratch_shapes=[] if nk == 1 else [pltpu.VMEM((tm, tn), F32)],
        compiler_params=_params(("parallel", "parallel", "arbitrary")),
    )(a, b, *extras, *vecs, *after)
    return outs[0] if len(outs) == 1 else outs


def _epi_resid_norm(acc, x, g_res, g_next):
    xn = x + acc * _rms(acc) * g_res
    return xn, xn * _rms(xn) * g_next, acc


def _epi_relu2(acc):
    r = jnp.maximum(acc, 0.0)
    return (r * r,)


def _epi_drelu2(acc, f):
    return (acc * (2.0 * jnp.sqrt(f.astype(F32))),)


def _norm_fwd(x, g, *, name):
    t, d = x.shape
    tt = _pick_rows(t, ROW_TILE)

    def body(x_ref, g_ref, h_ref):
        xv = x_ref[...]
        h_ref[...] = (xv * _rms(xv) * g_ref[...]).astype(BF16)

    row = pl.BlockSpec((tt, d), lambda i: (i, 0))
    return _call(body, name=name, grid=(t // tt,), in_specs=[row, _full((1, d))], out_specs=row,
                 out_shape=jax.ShapeDtypeStruct((t, d), BF16), compiler_params=_params(("parallel",)))(x, g)


def _resid_norm(x, n, g1, g2, *, name):
    t, d = x.shape
    tt = _pick_rows(t, ROW_TILE)

    def body(x_ref, n_ref, g1_ref, g2_ref, xo_ref, h_ref):
        nv = n_ref[...].astype(F32)
        xn = x_ref[...] + nv * _rms(nv) * g1_ref[...]
        xo_ref[...] = xn
        h_ref[...] = (xn * _rms(xn) * g2_ref[...]).astype(BF16)

    row = pl.BlockSpec((tt, d), lambda i: (i, 0))
    return _call(body, name=name, grid=(t // tt,), in_specs=[row, row, _full((1, d)), _full((1, d))],
                 out_specs=[row, row],
                 out_shape=[jax.ShapeDtypeStruct((t, d), F32), jax.ShapeDtypeStruct((t, d), BF16)],
                 compiler_params=_params(("parallel",)))(x, n, g1, g2)


def _loss_fwd_bwd(xf, target, *, name):
    t, d = xf.shape
    tt = _pick_rows(t, ROW_TILE)
    nt = t // tt

    def body(x_ref, t_ref, dy_ref, loss_ref, acc):
        i = pl.program_id(0)

        @pl.when(i == 0)
        def _():
            acc[...] = jnp.zeros_like(acc)

        e = x_ref[...] - t_ref[...]
        dy_ref[...] = e * (1.0 / d)
        acc[...] += jnp.sum(e * e, axis=0, keepdims=True)

        @pl.when(i == nt - 1)
        def _():
            loss_ref[...] = jnp.sum(acc[...], axis=-1, keepdims=True) * (0.5 / d)

    row = pl.BlockSpec((tt, d), lambda i: (i, 0))
    return _call(body, name=name, grid=(nt,), in_specs=[row, row], out_specs=[row, _full((1, 1))],
                 out_shape=[jax.ShapeDtypeStruct((t, d), F32), jax.ShapeDtypeStruct((1, 1), F32)],
                 scratch_shapes=[pltpu.VMEM((1, d), F32)], compiler_params=_params(("arbitrary",)))(xf, target)


def _bwd_norm_pair(xin, dh, dres, n, g_in, g_out, *, name):
    t, d = xin.shape
    tt = _pick_rows(t, ROW_TILE)
    n_dh = len(dh)

    def body(*refs):
        x_ref = refs[0]
        dh_refs = refs[1:1 + n_dh]
        dres_ref, n_ref, gi_ref, go_ref, dx_ref, dn_ref, dgi_ref, dgo_ref = refs[1 + n_dh:]
        i = pl.program_id(0)

        @pl.when(i == 0)
        def _():
            dgi_ref[...] = jnp.zeros_like(dgi_ref)
            dgo_ref[...] = jnp.zeros_like(dgo_ref)

        xv = x_ref[...]
        dhv = dh_refs[0][...].astype(F32)
        for r in dh_refs[1:]:
            dhv = dhv + r[...].astype(F32)
        dxh, dgi = _rms_bwd(xv, _rms(xv), gi_ref[...], dhv)
        dx = dres_ref[...] + dxh
        dx_ref[...] = dx
        dgi_ref[...] += jnp.sum(dgi, axis=0, keepdims=True)
        nv = n_ref[...].astype(F32)
        dn, dgo = _rms_bwd(nv, _rms(nv), go_ref[...], dx)
        dn_ref[...] = dn.astype(BF16)
        dgo_ref[...] += jnp.sum(dgo, axis=0, keepdims=True)

    row = pl.BlockSpec((tt, d), lambda i: (i, 0))
    vec = _full((1, d))
    return _call(body, name=name, grid=(t // tt,), in_specs=[row] * (n_dh + 3) + [vec, vec],
                 out_specs=[row, row, vec, vec],
                 out_shape=[jax.ShapeDtypeStruct((t, d), F32), jax.ShapeDtypeStruct((t, d), BF16),
                            jax.ShapeDtypeStruct((1, d), F32), jax.ShapeDtypeStruct((1, d), F32)],
                 compiler_params=_params(("arbitrary",)))(xin, *dh, dres, n, g_in, g_out)


def _bwd_norm_in(xin, dh, dres, g_in, *, name):
    t, d = xin.shape
    tt = _pick_rows(t, ROW_TILE)
    n_dh = len(dh)

    def body(*refs):
        x_ref = refs[0]
        dh_refs = refs[1:1 + n_dh]
        dres_ref, gi_ref, dx_ref, dgi_ref = refs[1 + n_dh:]
        i = pl.program_id(0)

        @pl.when(i == 0)
        def _():
            dgi_ref[...] = jnp.zeros_like(dgi_ref)

        xv = x_ref[...]
        dhv = dh_refs[0][...].astype(F32)
        for r in dh_refs[1:]:
            dhv = dhv + r[...].astype(F32)
        dxh, dgi = _rms_bwd(xv, _rms(xv), gi_ref[...], dhv)
        dx_ref[...] = dres_ref[...] + dxh
        dgi_ref[...] += jnp.sum(dgi, axis=0, keepdims=True)

    row = pl.BlockSpec((tt, d), lambda i: (i, 0))
    vec = _full((1, d))
    return _call(body, name=name, grid=(t // tt,), in_specs=[row] * (n_dh + 2) + [vec],
                 out_specs=[row, vec],
                 out_shape=[jax.ShapeDtypeStruct((t, d), F32), jax.ShapeDtypeStruct((1, d), F32)],
                 compiler_params=_params(("arbitrary",)))(xin, *dh, dres, g_in)


def _bwd_norm_out(n, g_out, dx, *, name):
    t, d = n.shape
    tt = _pick_rows(t, ROW_TILE)

    def body(n_ref, go_ref, dx_ref, dn_ref, dgo_ref):
        i = pl.program_id(0)

        @pl.when(i == 0)
        def _():
            dgo_ref[...] = jnp.zeros_like(dgo_ref)

        nv = n_ref[...].astype(F32)
        dn, dgo = _rms_bwd(nv, _rms(nv), go_ref[...], dx_ref[...])
        dn_ref[...] = dn.astype(BF16)
        dgo_ref[...] += jnp.sum(dgo, axis=0, keepdims=True)

    row = pl.BlockSpec((tt, d), lambda i: (i, 0))
    vec = _full((1, d))
    return _call(body, name=name, grid=(t // tt,), in_specs=[row, vec, row], out_specs=[row, vec],
                 out_shape=[jax.ShapeDtypeStruct((t, d), BF16), jax.ShapeDtypeStruct((1, d), F32)],
                 compiler_params=_params(("arbitrary",)))(n, g_out, dx)


def _shift_down(cur, halo, s):
    return jnp.concatenate([halo[SUBLANES - s:], cur[:cur.shape[0] - s]], axis=0)


def _shift_up(cur, halo, s):
    return jnp.concatenate([cur[s:], halo[:s]], axis=0)


def _conva_fwd(pa, w, g, *, d, seq, name):
    t = pa.shape[0]
    tt = _pick_rows(seq, ROW_TILE)
    tps = seq // tt

    def body(xa_ref, ca_ref, ba_ref, w_ref, g_ref, ya_ref, v_ref, carry):
        i = pl.program_id(0)

        @pl.when(i % tps == 0)
        def _():
            carry[...] = jnp.zeros_like(carry)

        u = ca_ref[...].astype(F32) * xa_ref[...].astype(F32)
        halo = carry[...]
        wv = w_ref[...]
        v = wv[2:3] * u + wv[1:2] * _shift_down(u, halo, 1) + wv[0:1] * _shift_down(u, halo, 2)
        carry[...] = u[tt - SUBLANES:]
        yp = ba_ref[...].astype(F32) * v
        ya_ref[...] = (yp * _rms(yp) * g_ref[...]).astype(BF16)
        v_ref[...] = v.astype(BF16)

    col = lambda c: pl.BlockSpec((tt, d), lambda i, c=c: (i, c))
    row = pl.BlockSpec((tt, d), lambda i: (i, 0))
    return _call(body, name=name, grid=(t // tt,),
                 in_specs=[col(0), col(1), col(2), _full((CONV_K, d)), _full((1, d))], out_specs=[row, row],
                 out_shape=[jax.ShapeDtypeStruct((t, d), BF16), jax.ShapeDtypeStruct((t, d), BF16)],
                 scratch_shapes=[pltpu.VMEM((SUBLANES, d), F32)],
                 compiler_params=_params(("arbitrary",)))(pa, pa, pa, w, g)


def _conva_bwd(dcat, pa, v, w, g, *, d, seq, name):
    t, width = pa.shape
    d3 = 3 * d
    tt = _pick_rows(seq, ROW_TILE)
    tps = seq // tt
    nt = t // tt

    def body(dya_ref, xa_ref, ca_ref, ba_ref, v_ref, w_ref, g_ref, dpa_ref, dw_ref, dg_ref, carry):
        i = pl.program_id(0)

        @pl.when(i == 0)
        def _():
            dw_ref[...] = jnp.zeros_like(dw_ref)
            dg_ref[...] = jnp.zeros_like(dg_ref)

        @pl.when(i % tps == 0)
        def _():
            carry[...] = jnp.zeros_like(carry)

        xa, ca, ba, vv = [r[...].astype(F32) for r in (xa_ref, ca_ref, ba_ref, v_ref)]
        yp = ba * vv
        dyp, dgt = _rms_bwd(yp, _rms(yp), g_ref[...], dya_ref[...].astype(F32))
        dg_ref[...] += jnp.sum(dgt, axis=0, keepdims=True)
        dv = dyp * ba
        halo = carry[...]
        dv1 = _shift_up(dv, halo, 1)
        dv2 = _shift_up(dv, halo, 2)
        carry[...] = dv[:SUBLANES]
        wv = w_ref[...]
        du = wv[2:3] * dv + wv[1:2] * dv1 + wv[0:1] * dv2
        u = ca * xa
        dw_ref[0:1, :] += jnp.sum(u * dv2, axis=0, keepdims=True)
        dw_ref[1:2, :] += jnp.sum(u * dv1, axis=0, keepdims=True)
        dw_ref[2:3, :] += jnp.sum(u * dv, axis=0, keepdims=True)
        dpa_ref[:, 0:d] = (du * ca).astype(BF16)
        dpa_ref[:, d:2 * d] = (du * xa).astype(BF16)
        dpa_ref[:, 2 * d:3 * d] = (dyp * vv).astype(BF16)

    rcol = lambda c: pl.BlockSpec((tt, d), lambda i, c=c: (nt - 1 - i, c))
    return _call(body, name=name, grid=(nt,),
                 in_specs=[rcol(0), rcol(0), rcol(1), rcol(2), rcol(0), _full((CONV_K, d)), _full((1, d))],
                 out_specs=[pl.BlockSpec((tt, d3), lambda i: (nt - 1 - i, 0)), _full((CONV_K, d)), _full((1, d))],
                 out_shape=[jax.ShapeDtypeStruct((t, width), BF16), jax.ShapeDtypeStruct((CONV_K, d), F32),
                            jax.ShapeDtypeStruct((1, d), F32)],
                 scratch_shapes=[pltpu.VMEM((SUBLANES, d), F32)],
                 compiler_params=_params(("arbitrary",)))(dcat, pa, pa, pa, v, w, g)


CONV_CH = 512


def _convb_fwd(proj, w, bias, *, col0, seq, name):
    t = proj.shape[0]
    c = w.shape[1]
    cb = _pick(c, CONV_CH)
    assert col0 % cb == 0
    tt = _pick_rows(seq, 2 * ROW_TILE)
    tps = seq // tt

    def body(p_ref, w_ref, b_ref, o_ref, carry):
        i = pl.program_id(1)

        @pl.when(i % tps == 0)
        def _():
            carry[...] = jnp.zeros_like(carry)

        p = p_ref[...].astype(F32)
        halo = carry[...]
        wv = w_ref[...]
        o = wv[3:4] * p + b_ref[...]
        for s in (1, 2, 3):
            o = o + wv[3 - s:4 - s] * _shift_down(p, halo, s)
        carry[...] = p[tt - SUBLANES:]
        o_ref[...] = o.astype(BF16)

    return _call(body, name=name, grid=(c // cb, t // tt),
                 in_specs=[pl.BlockSpec((tt, cb), lambda jc, i: (i, col0 // cb + jc)),
                           pl.BlockSpec((SSM_CONV_K, cb), lambda jc, i: (0, jc)), pl.BlockSpec((1, cb), lambda jc, i: (0, jc))],
                 out_specs=pl.BlockSpec((tt, cb), lambda jc, i: (i, jc)), out_shape=jax.ShapeDtypeStruct((t, c), BF16),
                 scratch_shapes=[pltpu.VMEM((SUBLANES, cb), F32)],
                 compiler_params=_params(("arbitrary", "arbitrary")))(proj, w, bias)


def _convb_bwd(dconv, proj, w, dproj, *, col0, seq, name):
    t, c = dconv.shape
    cb = _pick(c, CONV_CH)
    assert col0 % cb == 0
    tt = _pick_rows(seq, 2 * ROW_TILE)
    tps = seq // tt
    nt = t // tt

    def body(dc_ref, p_ref, w_ref, dproj_in, dp_ref, dw_ref, db_ref, carry):
        del dproj_in
        i = pl.program_id(1)

        @pl.when(i == 0)
        def _():
            dw_ref[...] = jnp.zeros_like(dw_ref)
            db_ref[...] = jnp.zeros_like(db_ref)

        @pl.when(i % tps == 0)
        def _():
            carry[...] = jnp.zeros_like(carry)

        dc = dc_ref[...].astype(F32)
        p = p_ref[...].astype(F32)
        halo = carry[...]
        wv = w_ref[...]
        dp = wv[3:4] * dc
        dw_ref[3:4, :] += jnp.sum(p * dc, axis=0, keepdims=True)
        for s in (1, 2, 3):
            dcs = _shift_up(dc, halo, s)
            dp = dp + wv[3 - s:4 - s] * dcs
            dw_ref[3 - s:4 - s, :] += jnp.sum(p * dcs, axis=0, keepdims=True)
        carry[...] = dc[:SUBLANES]
        db_ref[...] += jnp.sum(dc, axis=0, keepdims=True)
        dp_ref[...] = dp.astype(BF16)

    win_spec = pl.BlockSpec((tt, cb), lambda jc, i: (nt - 1 - i, col0 // cb + jc))
    taps = pl.BlockSpec((SSM_CONV_K, cb), lambda jc, i: (0, jc))
    return _call(body, name=name, grid=(c // cb, nt),
                 in_specs=[pl.BlockSpec((tt, cb), lambda jc, i: (nt - 1 - i, jc)), win_spec, taps,
                           pl.BlockSpec(memory_space=pl.ANY)],
                 out_specs=[win_spec, taps, pl.BlockSpec((1, cb), lambda jc, i: (0, jc))],
                 out_shape=[jax.ShapeDtypeStruct(dproj.shape, BF16), jax.ShapeDtypeStruct((SSM_CONV_K, c), F32),
                            jax.ShapeDtypeStruct((1, c), F32)],
                 input_output_aliases={3: 0},
                 scratch_shapes=[pltpu.VMEM((SUBLANES, cb), F32)],
                 compiler_params=_params(("arbitrary", "arbitrary")))(dconv, proj, w, dproj)


def _expand_heads(x, ev):
    return jnp.dot(x, ev, precision=HIGHEST, preferred_element_type=F32)


def _head_sums(v, ev):
    return lax.dot_general(v, ev, (((1,), (1,)), ((), ())), precision=HIGHEST, preferred_element_type=F32)


def _ssd_common(c_ref, pdt_ref, dtb_ref, alog_ref, e_ref, h):
    cp = c_ref[...].astype(F32)
    sg = _sigmoid(cp)
    act = cp * sg
    pre = pdt_ref[:, 0:h] + dtb_ref[...]
    dt = _softplus(pre)
    a = -jnp.exp(alog_ref[...])
    adt = dt * a
    row = lax.broadcasted_iota(jnp.int32, (CHUNK, CHUNK), 0)
    col = lax.broadcasted_iota(jnp.int32, (CHUNK, CHUNK), 1)
    tril = row >= col
    cs = jnp.dot(tril.astype(F32), adt, precision=HIGHEST, preferred_element_type=F32)
    cs_t = lax.dot_general(adt, (col >= row).astype(F32), (((0,), (0,)), ((), ())), precision=HIGHEST,
                           preferred_element_type=F32)
    ev = e_ref[...]
    dt_l = _expand_heads(dt, ev)
    ecs_l = jnp.exp(_expand_heads(cs, ev))
    return dict(cp=cp, sg=sg, act=act, pre=pre, dt=dt, a=a, cs=cs, cs_t=cs_t, dt_l=dt_l, ecs_l=ecs_l,
                tril=tril, row=row, col=col, lo=col < HEAD_DIM)


def _dot_nt(a, b):
    return lax.dot_general(a, b, (((1,), (1,)), ((), ())), preferred_element_type=F32)


def _dot_tn(a, b):
    return lax.dot_general(a, b, (((0,), (0,)), ((), ())), preferred_element_type=F32)


def _dot(a, b):
    return jnp.dot(a, b, preferred_element_type=F32)


def _ssd_fwd(cpre, pdt, pz, ya, dtb, alog, dsk_lane, gs, emat, *, nseq, seq, name):
    t, xbc = cpre.shape
    d = ya.shape[1]
    h = d // HEAD_DIM
    npair = h // 2
    ppg = npair // SSM_GROUPS
    nc = seq // CHUNK
    gw = d // SSM_GROUPS
    bc0 = d
    cc0 = d + SSM_GROUPS * D_STATE

    def body(c_ref, pdt_ref, z_ref, ya_ref, dtb_ref, alog_ref, dsk_ref, gs_ref, e_ref, cat_ref, y2_ref, hp_ref, h_ref):
        @pl.when(pl.program_id(0) == 0)
        def _():
            h_ref[...] = jnp.zeros_like(h_ref)

        for sq in range(nseq):
            one_seq(c_ref.at[sq], pdt_ref.at[sq], z_ref.at[sq], ya_ref.at[sq], dtb_ref, alog_ref, dsk_ref, gs_ref, e_ref,
                    cat_ref.at[sq], y2_ref.at[sq], hp_ref.at[sq], h_ref.at[sq])

    def one_seq(c_ref, pdt_ref, z_ref, ya_ref, dtb_ref, alog_ref, dsk_ref, gs_ref, e_ref, cat_ref, y2_ref, hp_ref, h_ref):
        q = _ssd_common(c_ref, pdt_ref, dtb_ref, alog_ref, e_ref, h)
        act, cs, lo, ecs_l = q["act"], q["cs"], q["lo"], q["ecs_l"]
        xs = act[:, :d]
        xd = xs * q["dt_l"]
        ys = []
        for g in range(SSM_GROUPS):
            bg = act[:, bc0 + g * D_STATE: bc0 + (g + 1) * D_STATE]
            cgb = act[:, cc0 + g * D_STATE: cc0 + (g + 1) * D_STATE].astype(BF16)
            s = _dot_nt(cgb, bg.astype(BF16))
            bg_t = bg.T
            for jj in range(ppg):
                j = g * ppg + jj
                sl = slice(LANES * j, LANES * (j + 1))
                xdj = xd[:, sl]
                x2 = jnp.concatenate([jnp.where(lo, xdj, 0.0), jnp.where(lo, 0.0, xdj)], axis=0).astype(BF16)
                hprev = h_ref[j]
                hp_ref[j] = hprev.astype(BF16)
                ms, bws_t = [], []
                for hh in (2 * j, 2 * j + 1):
                    csc = cs[:, hh:hh + 1]
                    cs_row = q["cs_t"][hh:hh + 1, :]
                    seg = jnp.broadcast_to(csc, (CHUNK, CHUNK)) - jnp.broadcast_to(cs_row, (CHUNK, CHUNK))
                    ms.append(s * jnp.exp(jnp.where(q["tril"], seg, -jnp.inf)))
                    bws_t.append(bg_t * jnp.exp(cs_row[:, CHUNK - 1:CHUNK] - cs_row))
                ydiag = _dot(jnp.concatenate(ms, axis=1).astype(BF16), x2)
                st = _dot(jnp.concatenate(bws_t, axis=1).astype(BF16), x2)
                ecs = ecs_l[:, sl]
                yoff = _dot(cgb, hprev.astype(BF16)) * ecs
                h_ref[j] = hprev * ecs[CHUNK - 1:CHUNK] + st
                ys.append(ydiag + yoff)
        y = jnp.concatenate(ys, axis=1) + dsk_ref[...] * xs
        y2_ref[...] = y.astype(BF16)
        zv = z_ref[...].astype(F32)
        y3 = y * (zv * _sigmoid(zv))
        cat_ref[:, 0:d] = ya_ref[...]
        for gi in range(SSM_GROUPS):
            seg = y3[:, gi * gw:(gi + 1) * gw]
            cat_ref[:, d + gi * gw:d + (gi + 1) * gw] = (seg * _rms(seg) * gs_ref[:, gi * gw:(gi + 1) * gw]).astype(BF16)

    chunk = lambda w, cb=0: pl.BlockSpec((nseq, CHUNK, w), lambda c, cb=cb: (0, c, cb))
    vec = lambda w: pl.BlockSpec((1, w), lambda c: (0, 0))
    hp_spec = pl.BlockSpec((nseq, None, npair, D_STATE, LANES), lambda c: (0, c, 0, 0, 0))
    per_seq = lambda a: a.reshape(nseq, seq, a.shape[1])
    cat, y2, hp = _call(
        body, name=name, grid=(nc,),
        in_specs=[chunk(xbc), chunk(LANES), chunk(d, 3), chunk(d), vec(h), vec(h), vec(d), vec(d),
                  pl.BlockSpec((h, d), lambda c: (0, 0))],
        out_specs=[chunk(2 * d), chunk(d), hp_spec],
        out_shape=[jax.ShapeDtypeStruct((nseq, seq, 2 * d), BF16), jax.ShapeDtypeStruct((nseq, seq, d), BF16),
                   jax.ShapeDtypeStruct((nseq, nc, npair, D_STATE, LANES), BF16)],
        scratch_shapes=[pltpu.VMEM((nseq, npair, D_STATE, LANES), F32)],
        compiler_params=_params(("arbitrary",)))(
            per_seq(cpre), per_seq(pdt), per_seq(pz), per_seq(ya), dtb, alog, dsk_lane, gs, emat)
    return cat.reshape(t, 2 * d), y2.reshape(t, d), hp


def _ssd_bwd(cpre, pdt, pz, y2, hprev_all, dcat, dtb, alog, dsk_lane, gs, emat, dproj, *, nseq, seq, name):
    t, xbc = cpre.shape
    d = y2.shape[1]
    h = d // HEAD_DIM
    npair = h // 2
    ppg = npair // SSM_GROUPS
    nc = seq // CHUNK
    gw = d // SSM_GROUPS
    bc0 = d
    cc0 = d + SSM_GROUPS * D_STATE

    def body(c_ref, pdt_ref, z_ref, y2_ref, hp_ref, dys_ref, dtb_ref, alog_ref, dsk_ref, gs_ref, e_ref, dproj_in,
             dconv_ref, dz_ref, dpdt_ref, dgs_ref, ddsk_ref, ddtb_ref, dalog_ref, dh_ref):
        del dproj_in
        b = pl.program_id(0)
        c = pl.program_id(1)

        @pl.when(c == 0)
        def _():
            dh_ref[...] = jnp.zeros_like(dh_ref)

        @pl.when((b == 0) & (c == 0))
        def _():
            dgs_ref[...] = jnp.zeros_like(dgs_ref)
            ddsk_ref[...] = jnp.zeros_like(ddsk_ref)
            ddtb_ref[...] = jnp.zeros_like(ddtb_ref)
            dalog_ref[...] = jnp.zeros_like(dalog_ref)

        q = _ssd_common(c_ref, pdt_ref, dtb_ref, alog_ref, e_ref, h)
        cp, sg, act, cs, a, dt, lo = q["cp"], q["sg"], q["act"], q["cs"], q["a"], q["dt"], q["lo"]
        ecs_l, dt_l = q["ecs_l"], q["dt_l"]
        ev = e_ref[...]
        xs = act[:, :d]
        xd = xs * dt_l
        row16 = lax.broadcasted_iota(jnp.int32, (CHUNK, h), 0)
        hid = lax.broadcasted_iota(jnp.int32, (1, h), 1)
        hid_t = lax.broadcasted_iota(jnp.int32, (h, 1), 0)

        zv = z_ref[...].astype(F32)
        sz = _sigmoid(zv)
        siluz = zv * sz
        y2v = y2_ref[...].astype(F32)
        y3 = y2v * siluz
        dysv = dys_ref[...].astype(F32)
        dy3s = []
        for gi in range(SSM_GROUPS):
            gsl = slice(gi * gw, (gi + 1) * gw)
            seg = y3[:, gsl]
            dseg, dgt = _rms_bwd(seg, _rms(seg), gs_ref[:, gsl], dysv[:, gsl])
            dy3s.append(dseg)
            dgs_ref[:, gsl] += jnp.sum(dgt, axis=0, keepdims=True)
        dy3 = jnp.concatenate(dy3s, axis=1)
        dy = dy3 * siluz
        dz_ref[...] = (dy3 * y2v * (sz * (1.0 + zv * (1.0 - sz)))).astype(BF16)
        ddsk_ref[...] += jnp.sum(_head_sums(dy * xs, ev), axis=0, keepdims=True)

        dcs = jnp.zeros((CHUNK, h), F32)
        dcs_t = jnp.zeros((h, CHUNK), F32)
        dxd_parts, yoff_parts, db_parts, dc_parts = [], [], [], []
        for g in range(SSM_GROUPS):
            bg = act[:, bc0 + g * D_STATE: bc0 + (g + 1) * D_STATE]
            cg = act[:, cc0 + g * D_STATE: cc0 + (g + 1) * D_STATE]
            bgb, cgb = bg.astype(BF16), cg.astype(BF16)
            cgb_t = cg.T.astype(BF16)
            s = _dot_nt(cgb, bgb)
            ds = jnp.zeros((CHUNK, CHUNK), F32)
            dbg = jnp.zeros((CHUNK, D_STATE), F32)
            dcg = jnp.zeros((CHUNK, D_STATE), F32)
            for jj in range(ppg):
                j = g * ppg + jj
                sl = slice(LANES * j, LANES * (j + 1))
                xdj = xd[:, sl]
                xdb = xdj.astype(BF16)
                x2 = jnp.concatenate([jnp.where(lo, xdj, 0.0), jnp.where(lo, 0.0, xdj)], axis=0).astype(BF16)
                dyj = dy[:, sl]
                dy2 = jnp.concatenate([jnp.where(lo, dyj, 0.0), jnp.where(lo, 0.0, dyj)], axis=0).astype(BF16)
                hpb = hp_ref[j]
                hprev = hpb.astype(F32)
                dhn = dh_ref[j]
                dhb = dhn.astype(BF16)
                dh2 = jnp.concatenate([jnp.where(lo, dhn, 0.0), jnp.where(lo, 0.0, dhn)], axis=0).astype(BF16)
                ecs = ecs_l[:, sl]
                gmat = (dyj * ecs).astype(BF16)
                yoff_parts.append(_dot(cgb, hpb) * ecs)
                dcg = dcg + _dot_nt(gmat, hpb)
                dh_ref[j] = dhn * ecs[CHUNK - 1:CHUNK] + _dot(cgb_t, gmat)
                t2 = dhn * hprev
                dbw2 = _dot_nt(x2, dhb)
                dm2 = _dot_nt(dy2, xdb)
                ms, bws = [], []
                for idx, hh in enumerate((2 * j, 2 * j + 1)):
                    msk = lo if idx == 0 else jnp.logical_not(lo)
                    onehot = (hid == hh).astype(F32)
                    csc = cs[:, hh:hh + 1]
                    seg = jnp.broadcast_to(csc, (CHUNK, CHUNK)) - jnp.broadcast_to(q["cs_t"][hh:hh + 1, :], (CHUNK, CHUNK))
                    lm = jnp.exp(jnp.where(q["tril"], seg, -jnp.inf))
                    m = s * lm
                    cs_last = cs[CHUNK - 1:CHUNK, hh:hh + 1]
                    dte = jnp.exp(cs_last - csc)
                    ms.append(m)
                    bws.append(bg * dte)
                    dbw = dbw2[idx * CHUNK:(idx + 1) * CHUNK]
                    dbg = dbg + dbw * dte
                    qv = jnp.sum(dbw * bg, axis=-1, keepdims=True) * dte
                    dm = dm2[idx * CHUNK:(idx + 1) * CHUNK]
                    wm = dm * m
                    rc = jnp.sum(wm, axis=-1, keepdims=True)
                    dcs_t = dcs_t - (hid_t == hh).astype(F32) * jnp.sum(wm, axis=0, keepdims=True)
                    ds = ds + dm * lm
                    ddec = jnp.sum(jnp.where(msk, t2, 0.0)) * jnp.exp(cs_last)
                    last = jnp.sum(qv) + ddec
                    dcs = dcs + (rc - qv) * onehot + jnp.where(row16 == CHUNK - 1, last * onehot, 0.0)
                dxd_s = _dot(jnp.concatenate(bws, axis=1).astype(BF16), dh2)
                dxd_d = _dot_tn(jnp.concatenate(ms, axis=0).astype(BF16), dy2)
                dxd_parts.append(dxd_s + dxd_d)
            dsb = ds.astype(BF16)
            dc_parts.append(dcg + _dot(dsb, bgb))
            db_parts.append(dbg + _dot_tn(dsb, cgb))
        yoff_all = jnp.concatenate(yoff_parts, axis=1)
        dxd_all = jnp.concatenate(dxd_parts, axis=1)
        dcs = dcs + _head_sums(dy * yoff_all, ev)
        triu = (q["col"] >= q["row"]).astype(F32)
        dadt = (jnp.dot(triu, dcs, precision=HIGHEST, preferred_element_type=F32)
                + lax.dot_general(triu, dcs_t, (((1,), (1,)), ((), ())), precision=HIGHEST, preferred_element_type=F32))
        ddt = dadt * a + _head_sums(dxd_all * xs, ev)
        dalog_ref[...] += jnp.sum(dadt * dt, axis=0, keepdims=True) * a
        dpre = ddt * _sigmoid(q["pre"])
        ddtb_ref[...] += jnp.sum(dpre, axis=0, keepdims=True)
        dpdt_ref[...] = jnp.zeros_like(dpdt_ref)
        dpdt_ref[:, 0:h] = dpre.astype(BF16)
        dxs = dxd_all * dt_l + dy * dsk_ref[...]
        dact = jnp.concatenate([dxs] + db_parts + dc_parts, axis=1)
        dconv_ref[...] = (dact * (sg * (1.0 + cp * (1.0 - sg)))).astype(BF16)

    rchunk = lambda w, cb=0: pl.BlockSpec((CHUNK, w), lambda b, c, cb=cb: (b * nc + nc - 1 - c, cb))
    vec = lambda w: pl.BlockSpec((1, w), lambda b, c: (0, 0))
    hp_spec = pl.BlockSpec((None, None, npair, D_STATE, LANES), lambda b, c: (b, nc - 1 - c, 0, 0, 0))
    return _call(body, name=name, grid=(nseq, nc),
                 in_specs=[rchunk(xbc), rchunk(LANES), rchunk(d, 3), rchunk(d), hp_spec, rchunk(d, 1),
                           vec(h), vec(h), vec(d), vec(d), pl.BlockSpec((h, d), lambda b, c: (0, 0)),
                           pl.BlockSpec(memory_space=pl.ANY)],
                 out_specs=[rchunk(xbc), rchunk(d, 3), rchunk(LANES), vec(d), vec(h), vec(h), vec(h)],
                 out_shape=[jax.ShapeDtypeStruct((t, xbc), BF16), jax.ShapeDtypeStruct(dproj.shape, BF16),
                            jax.ShapeDtypeStruct((t, LANES), BF16), jax.ShapeDtypeStruct((1, d), F32),
                            jax.ShapeDtypeStruct((1, h), F32), jax.ShapeDtypeStruct((1, h), F32),
                            jax.ShapeDtypeStruct((1, h), F32)],
                 input_output_aliases={11: 1},
                 scratch_shapes=[pltpu.VMEM((npair, D_STATE, LANES), F32)],
                 compiler_params=_params(("arbitrary", "arbitrary")))(
                     cpre, pdt, pz, y2, hprev_all, dcat, dtb, alog, dsk_lane, gs, emat, dproj)


def _sum_adamw(parts, w, m, v, *, name, layer=None, outs=None):
    n, r, c = parts.shape
    tr = _pick_rows(r, 256)
    bc1 = 1.0 - ADAM_B1 ** ADAM_STEP
    bc2 = 1.0 - ADAM_B2 ** ADAM_STEP

    def body(p_ref, w_ref, m_ref, v_ref, *rest):
        g_ref, d_ref, mo_ref, vo_ref = rest[-4:]
        g = p_ref[0].astype(F32)
        for k in range(1, n):
            g = g + p_ref[k].astype(F32)
        mn = ADAM_B1 * m_ref[...] + (1.0 - ADAM_B1) * g
        vn = ADAM_B2 * v_ref[...] + (1.0 - ADAM_B2) * (g * g)
        g_ref[...] = g
        mo_ref[...] = mn
        vo_ref[...] = vn
        d_ref[...] = -ADAM_LR * ((mn / bc1) / (jnp.sqrt(vn / bc2) + ADAM_EPS) + ADAM_WD * w_ref[...])

    p_spec = pl.BlockSpec((n, tr, c), lambda i: (0, i, 0))
    if layer is None:
        blk = pl.BlockSpec((tr, c), lambda i: (i, 0))
        return _call(body, name=name, grid=(r // tr,), in_specs=[p_spec, blk, blk, blk], out_specs=[blk] * 4,
                     out_shape=[jax.ShapeDtypeStruct((r, c), F32)] * 4,
                     compiler_params=_params(("parallel",)))(parts, w, m, v)
    blk = pl.BlockSpec((None, tr, c), lambda i: (layer, i, 0))
    if outs is None:
        outs = [lax.empty(w.shape, F32) for _ in range(4)]
    return _call(body, name=name, grid=(r // tr,),
                 in_specs=[p_spec, blk, blk, blk] + [pl.BlockSpec(memory_space=pl.ANY)] * 4, out_specs=[blk] * 4,
                 out_shape=[jax.ShapeDtypeStruct(w.shape, F32)] * 4, input_output_aliases={4 + k: k for k in range(4)},
                 compiler_params=_params(("parallel",)))(parts, w, m, v, *outs)


def _assemble_cols(blocks, *, name):
    nb, r, c = blocks.shape
    width = -(-nb * c // LANES) * LANES
    tr = _pick_rows(r, 256)

    def body(b_ref, o_ref):
        pieces = [b_ref[j] for j in range(nb)]
        if width > nb * c:
            pieces.append(jnp.zeros((tr, width - nb * c), blocks.dtype))
        o_ref[...] = jnp.concatenate(pieces, axis=1)

    return _call(body, name=name, grid=(r // tr,), in_specs=[pl.BlockSpec((nb, tr, c), lambda i: (0, i, 0))],
                 out_specs=pl.BlockSpec((tr, width), lambda i: (i, 0)), out_shape=jax.ShapeDtypeStruct((r, width), blocks.dtype),
                 compiler_params=_params(("parallel",)))(blocks)


def _split_cols(pieces, c, *, name):
    r = pieces[0].shape[0]
    tr = _pick_rows(r, 256)
    n_in = len(pieces)

    def body(*refs):
        o_ref = refs[n_in]
        x = jnp.concatenate([p[...] for p in refs[:n_in]], axis=1) if n_in > 1 else refs[0][...]
        for j in range(N_DEV):
            o_ref[j] = x[:, c * j:c * (j + 1)]

    return _call(body, name=name, grid=(r // tr,),
                 in_specs=[pl.BlockSpec((tr, p.shape[1]), lambda i: (i, 0)) for p in pieces],
                 out_specs=pl.BlockSpec((N_DEV, tr, c), lambda i: (0, i, 0)),
                 out_shape=jax.ShapeDtypeStruct((N_DEV, r, c), pieces[0].dtype),
                 compiler_params=_params(("parallel",)))(*pieces)


def _sum_parts(parts, *, name):
    n, r, c = parts.shape
    tr = _pick_rows(r, 256)

    def body(p_ref, g_ref):
        g = p_ref[0].astype(F32)
        for k in range(1, n):
            g = g + p_ref[k].astype(F32)
        g_ref[...] = g

    return _call(body, name=name, grid=(r // tr,), in_specs=[pl.BlockSpec((n, tr, c), lambda i: (0, i, 0))],
                 out_specs=pl.BlockSpec((tr, c), lambda i: (i, 0)), out_shape=jax.ShapeDtypeStruct((r, c), F32),
                 compiler_params=_params(("parallel",)))(parts)


def _peers():
    x, y, c = lax.axis_index("x"), lax.axis_index("y"), lax.axis_index("c")
    me = 4 * x + 2 * y + c
    out = []
    for k in range(1, N_DEV):
        px = (1 - x) if (k >> 2) & 1 else x
        py = (1 - y) if (k >> 1) & 1 else y
        pc = (1 - c) if k & 1 else c
        out.append(((px, py, pc), 4 * px + 2 * py + pc))
    return me, out


_HBM = pl.BlockSpec(memory_space=pltpu.HBM)
_SEM = pl.BlockSpec(memory_space=pltpu.SEMAPHORE)
_EFFECT = pltpu.SideEffectType.DATAFLOW_SIDE_EFFECTING


ALL_PEERS = tuple(range(1, N_DEV))
SAME_CORE_PEERS = (2, 4, 6)


def _split_copies(s_refs, l_refs, send_sems, recv_sems, gather, incoming, ks=ALL_PEERS):
    me, peers = _peers()
    local, remote = [], []
    for ti, (s_ref, l_ref) in enumerate(zip(s_refs, l_refs)):
        base = ti * N_DEV
        local.append(pltpu.make_async_copy(s_ref if gather else s_ref.at[me], l_ref.at[me], recv_sems.at[base + N_DEV - 1]))
        for k, (dev, pid) in enumerate(peers):
            if k + 1 not in ks:
                continue
            sems = dict(send_sem=send_sems.at[base + k], recv_sem=recv_sems.at[base + k], device_id=dev, device_id_type=MESH)
            src = s_ref if gather else s_ref.at[pid]
            remote.append((
                pltpu.make_async_remote_copy(src_ref=src, dst_ref=l_ref.at[me], **sems),
                pltpu.make_async_remote_copy(src_ref=src, dst_ref=l_ref.at[pid], **sems) if incoming else None))
    return local, remote


def _exchange_start(srcs, *, gather, name, after=(), ks=ALL_PEERS):
    n = len(srcs)
    after = list(after)
    srcs = [pltpu.with_memory_space_constraint(s, pltpu.HBM) for s in srcs]
    lands = [pltpu.with_memory_space_constraint(
        lax.empty((N_DEV,) + tuple(s.shape if gather else s.shape[1:]), s.dtype), pltpu.HBM) for s in srcs]

    def body(*refs):
        s_refs, l_refs = refs[:n], refs[n:2 * n]
        outs = refs[2 * n + len(after):]
        send_sems, recv_sems, token = outs[0], outs[1], outs[-1]
        local, remote = _split_copies(s_refs, l_refs, send_sems, recv_sems, gather, incoming=False, ks=ks)
        for cp in local:
            cp.start()
        for out_cp, _ in remote:
            out_cp.start()
        token[...] = jnp.zeros_like(token)

    outs = _call(
        body, name=name,
        out_shape=(pltpu.SemaphoreType.DMA((n * N_DEV,)), pltpu.SemaphoreType.DMA((n * N_DEV,)),
                   *[pltpu.HBM(s.shape, s.dtype) for s in srcs], *[pltpu.HBM(l.shape, l.dtype) for l in lands],
                   jax.ShapeDtypeStruct((SUBLANES, LANES), F32)),
        in_specs=[_HBM] * (2 * n) + [pl.BlockSpec(memory_space=pl.ANY)] * len(after),
        out_specs=(_SEM, _SEM, *[_HBM] * (2 * n), pl.BlockSpec(memory_space=pltpu.VMEM)),
        input_output_aliases={k: k + 2 for k in range(2 * n)},
        compiler_params=pltpu.CompilerParams(has_side_effects=_EFFECT),
    )(*srcs, *lands, *after)
    return dict(n=n, gather=gather, ks=ks, sems=outs[:2], srcs=outs[2:2 + n], lands=outs[2 + n:2 + 2 * n]), outs[-1]


def _exchange_wait(state, after, *, name):
    n, gather, ks = state["n"], state["gather"], state["ks"]
    after = list(after)

    def body(*refs):
        s_refs, l_refs = refs[:n], refs[n:2 * n]
        send_sems, recv_sems = refs[2 * n], refs[2 * n + 1]
        local, remote = _split_copies(s_refs, l_refs, send_sems, recv_sems, gather, incoming=True, ks=ks)
        for out_cp, in_cp in remote:
            out_cp.wait_send()
            in_cp.wait_recv()
        for cp in local:
            cp.wait()

    outs = _call(
        body, name=name,
        out_shape=tuple(pltpu.HBM(a.shape, a.dtype) for a in (*state["srcs"], *state["lands"])),
        in_specs=[_HBM] * (2 * n) + [_SEM, _SEM] + [pl.BlockSpec(memory_space=pl.ANY)] * len(after),
        out_specs=tuple([_HBM] * (2 * n)),
        input_output_aliases={k: k for k in range(2 * n)},
        compiler_params=pltpu.CompilerParams(has_side_effects=_EFFECT),
    )(*state["srcs"], *state["lands"], *state["sems"], *after)
    return outs[n:]


def _sibling_copies(l_refs, send_sems, recv_sems, incoming):
    x, y, c = lax.axis_index("x"), lax.axis_index("y"), lax.axis_index("c")
    out = []
    for ti, l_ref in enumerate(l_refs):
        for q in range(4):
            px = (1 - x) if q & 2 else x
            py = (1 - y) if q & 1 else y
            mine, theirs = 4 * px + 2 * py + c, 4 * px + 2 * py + (1 - c)
            sems = dict(send_sem=send_sems.at[4 * ti + q], recv_sem=recv_sems.at[4 * ti + q],
                        device_id=(x, y, 1 - c), device_id_type=MESH)
            out.append((
                pltpu.make_async_remote_copy(src_ref=l_ref.at[mine], dst_ref=l_ref.at[mine], **sems),
                pltpu.make_async_remote_copy(src_ref=l_ref.at[mine], dst_ref=l_ref.at[theirs], **sems) if incoming else None))
    return out


def _sibling_start(lands, *, name, after=()):
    n = len(lands)
    after = list(after)
    lands = [pltpu.with_memory_space_constraint(l, pltpu.HBM) for l in lands]

    def body(*refs):
        l_refs = refs[:n]
        outs = refs[n + len(after):]
        for out_cp, _ in _sibling_copies(l_refs, outs[0], outs[1], incoming=False):
            out_cp.start()
        outs[-1][...] = jnp.zeros_like(outs[-1])

    outs = _call(
        body, name=name,
        out_shape=(pltpu.SemaphoreType.DMA((4 * n,)), pltpu.SemaphoreType.DMA((4 * n,)),
                   *[pltpu.HBM(l.shape, l.dtype) for l in lands], jax.ShapeDtypeStruct((SUBLANES, LANES), F32)),
        in_specs=[_HBM] * n + [pl.BlockSpec(memory_space=pl.ANY)] * len(after),
        out_specs=(_SEM, _SEM, *[_HBM] * n, pl.BlockSpec(memory_space=pltpu.VMEM)),
        input_output_aliases={k: k + 2 for k in range(n)},
        compiler_params=pltpu.CompilerParams(has_side_effects=_EFFECT),
    )(*lands, *after)
    return dict(n=n, sems=outs[:2], lands=outs[2:2 + n]), outs[-1]


def _sibling_wait(state, after, *, name):
    n = state["n"]
    after = list(after)

    def body(*refs):
        l_refs = refs[:n]
        for out_cp, in_cp in _sibling_copies(l_refs, refs[n], refs[n + 1], incoming=True):
            out_cp.wait_send()
            in_cp.wait_recv()

    return _call(
        body, name=name,
        out_shape=tuple(pltpu.HBM(a.shape, a.dtype) for a in state["lands"]),
        in_specs=[_HBM] * n + [_SEM, _SEM] + [pl.BlockSpec(memory_space=pl.ANY)] * len(after),
        out_specs=tuple([_HBM] * n), input_output_aliases={k: k for k in range(n)},
        compiler_params=pltpu.CompilerParams(has_side_effects=_EFFECT),
    )(*state["lands"], *state["sems"], *after)


def _pack(arrs):
    flat = jnp.concatenate([a.reshape(-1).astype(F32) for a in arrs])
    pad = (-flat.shape[0]) % (SUBLANES * LANES)
    return jnp.pad(flat, (0, pad)).reshape(-1, LANES)


def _unpack(packed, shapes):
    flat = packed.reshape(-1)
    out, off = [], 0
    for s in shapes:
        n = 1
        for v in s:
            n *= v
        out.append(flat[off:off + n].reshape(s))
        off += n
    return out


SMALL = ("norm_mix_pre", "ssm_conv_b", "dt_bias", "a_log", "d_skip", "conv_out_norm", "ssm_out_norm",
         "norm_mix_post", "norm_mlp_pre", "norm_mlp_post", "conv_a_w", "ssm_conv_w")
BIG = ("w_in", "w_out", "w_up", "w_down")
ORDER = ("norm_mix_pre", "w_in", "conv_a_w", "ssm_conv_w", "ssm_conv_b", "dt_bias", "a_log", "d_skip",
         "conv_out_norm", "ssm_out_norm", "w_out", "norm_mix_post", "norm_mlp_pre", "w_up", "w_down", "norm_mlp_post")


def kernel(x, norm_mix_pre, w_in, conv_a_w, ssm_conv_w, ssm_conv_b, dt_bias, a_log, d_skip, conv_out_norm, ssm_out_norm, w_out, norm_mix_post, norm_mlp_pre, w_up, w_down, norm_mlp_post, loss_target, m_norm_mix_pre, m_w_in, m_conv_a_w, m_ssm_conv_w, m_ssm_conv_b, m_dt_bias, m_a_log, m_d_skip, m_conv_out_norm, m_ssm_out_norm, m_w_out, m_norm_mix_post, m_norm_mlp_pre, m_w_up, m_w_down, m_norm_mlp_post, v_norm_mix_pre, v_w_in, v_conv_a_w, v_ssm_conv_w, v_ssm_conv_b, v_dt_bias, v_a_log, v_d_skip, v_conv_out_norm, v_ssm_out_norm, v_w_out, v_norm_mix_post, v_norm_mlp_pre, v_w_up, v_w_down, v_norm_mlp_post):
    W = dict(norm_mix_pre=norm_mix_pre, w_in=w_in, conv_a_w=conv_a_w, ssm_conv_w=ssm_conv_w, ssm_conv_b=ssm_conv_b,
             dt_bias=dt_bias, a_log=a_log, d_skip=d_skip, conv_out_norm=conv_out_norm, ssm_out_norm=ssm_out_norm,
             w_out=w_out, norm_mix_post=norm_mix_post, norm_mlp_pre=norm_mlp_pre, w_up=w_up, w_down=w_down,
             norm_mlp_post=norm_mlp_post)
    M = dict(norm_mix_pre=m_norm_mix_pre, w_in=m_w_in, conv_a_w=m_conv_a_w, ssm_conv_w=m_ssm_conv_w,
             ssm_conv_b=m_ssm_conv_b, dt_bias=m_dt_bias, a_log=m_a_log, d_skip=m_d_skip,
             conv_out_norm=m_conv_out_norm, ssm_out_norm=m_ssm_out_norm, w_out=m_w_out,
             norm_mix_post=m_norm_mix_post, norm_mlp_pre=m_norm_mlp_pre, w_up=m_w_up, w_down=m_w_down,
             norm_mlp_post=m_norm_mlp_post)
    V = dict(norm_mix_pre=v_norm_mix_pre, w_in=v_w_in, conv_a_w=v_conv_a_w, ssm_conv_w=v_ssm_conv_w,
             ssm_conv_b=v_ssm_conv_b, dt_bias=v_dt_bias, a_log=v_a_log, d_skip=v_d_skip,
             conv_out_norm=v_conv_out_norm, ssm_out_norm=v_ssm_out_norm, w_out=v_w_out,
             norm_mix_post=v_norm_mix_post, norm_mlp_pre=v_norm_mlp_pre, w_up=v_w_up, w_down=v_w_down,
             norm_mlp_post=v_norm_mlp_post)

    nseq, seq, d = x.shape
    t = nseq * seq
    depth = w_in.shape[0]
    h = d // HEAD_DIM
    xbc = d + 2 * SSM_GROUPS * D_STATE
    in_cols = w_in.shape[2] * N_DEV
    d_mix = w_out.shape[1] * N_DEV
    d_ff = w_up.shape[2] * N_DEV
    me = 4 * lax.axis_index("x") + 2 * lax.axis_index("y") + lax.axis_index("c")
    ca_shard = conv_a_w.shape[2]
    sc_shard = ssm_conv_w.shape[2]

    tap_shapes = [conv_a_w.shape[1:], ssm_conv_w.shape[1:]]

    def gather_start(i, after=()):
        ks = SAME_CORE_PEERS
        st_in, tok_in = _exchange_start([w_in[i].astype(BF16), _pack([conv_a_w[i], ssm_conv_w[i]])], gather=True,
                                        name=f"gather_start_in_{i}", after=after, ks=ks)
        st_rest, tok_rest = _exchange_start([W[n][i].astype(BF16) for n in ("w_out", "w_up", "w_down")], gather=True,
                                            name=f"gather_start_rest_{i}", after=[tok_in], ks=ks)
        return st_in, st_rest, tok_rest

    vec = lambda name, i: W[name][i].reshape(1, -1)
    emat = (lax.broadcasted_iota(jnp.int32, (h, d), 1) // HEAD_DIM == lax.broadcasted_iota(jnp.int32, (h, d), 0)).astype(F32)

    xcur = x.reshape(t, d)
    hcur = _norm_fwd(xcur, vec("norm_mix_pre", 0), name="norm_first")
    saved = []
    nxt = gather_start(0)
    sib_in = None
    for i in range(depth):
        st_in, st_rest, tok = nxt
        if sib_in is None:
            sib_in, _ = _sibling_start(_exchange_wait(st_in, [hcur, tok], name=f"gather_wait_in_{i}"),
                                       name=f"gather_sibling_start_in_{i}")
        win_g, taps_g = _sibling_wait(sib_in, [hcur], name=f"gather_sibling_wait_in_{i}")
        win = _assemble_cols(win_g, name=f"assemble_w_in_{i}")
        taps_j = [_unpack(taps_g[j], tap_shapes) for j in range(N_DEV)]
        conv_a_i = jnp.concatenate([tj[0] for tj in taps_j], axis=1)
        ssm_conv_i = jnp.concatenate([tj[1] for tj in taps_j], axis=1)
        proj = _mm(hcur, win, n=4 * d + xbc, name=f"fwd_proj_{i}", out_dtypes=(BF16,))
        pdt = _mm(hcur, win, n=LANES, b_off=4 * d + xbc, name=f"fwd_proj_dt_{i}")
        ya, va = _conva_fwd(proj, conv_a_i, vec("conv_out_norm", i), d=d, seq=seq, name=f"fwd_conv_a_{i}")
        cpre = _convb_fwd(proj, ssm_conv_i, vec("ssm_conv_b", i), col0=4 * d, seq=seq, name=f"fwd_conv_b_{i}")
        dsk_lane = jnp.repeat(W["d_skip"][i], HEAD_DIM).reshape(1, d)
        st_sib, tok_sib = _sibling_start(_exchange_wait(st_rest, [cpre], name=f"gather_wait_rest_{i}"),
                                         name=f"gather_sibling_start_rest_{i}")
        cat, y2, hprev = _ssd_fwd(cpre, pdt, proj, ya, vec("dt_bias", i) + tok_sib[0:1, 0:1], vec("a_log", i), dsk_lane,
                                  vec("ssm_out_norm", i), emat, nseq=nseq, seq=seq, name=f"fwd_ssd_{i}")
        wout_g, wup_g, wdown_g = _sibling_wait(st_sib, [cat], name=f"gather_sibling_wait_rest_{i}")
        lw = dict(win=win, wout=wout_g.reshape(d_mix, d),
                  wup=_assemble_cols(wup_g, name=f"assemble_w_up_{i}"), wdown=wdown_g.reshape(d_ff, d),
                  conv_a=conv_a_i, ssm_conv=ssm_conv_i)
        after = []
        if i + 1 < depth:
            nxt = gather_start(i + 1, after=[wout_g])
            after = [nxt[2]]
        x1, h2, mix = _mm(cat, lw["wout"], name=f"fwd_out_{i}", after=after, out_dtypes=(F32, BF16, BF16),
                          epi=_epi_resid_norm, extras=(xcur,), vecs=(vec("norm_mix_post", i), vec("norm_mlp_pre", i)),
                          tm_cap=FUSED_ROWS)
        f = _mm(h2, lw["wup"], name=f"fwd_up_{i}", out_dtypes=(BF16,), epi=_epi_relu2)
        g_next = vec("norm_mix_pre", i + 1) if i + 1 < depth else vec("norm_mix_pre", 0)
        after = []
        if i + 1 < depth:
            sib_in, tok_in = _sibling_start(_exchange_wait(nxt[0], [f], name=f"gather_wait_in_{i + 1}"),
                                            name=f"gather_sibling_start_in_{i + 1}")
            after = [tok_in]
        dn = _mm(f, lw["wdown"], name=f"fwd_down_{i}", out_dtypes=(BF16,), after=after)
        x2, hnext = _resid_norm(x1, dn, vec("norm_mlp_post", i), g_next, name=f"fwd_post_mlp_{i}")
        saved.append(dict(lw=lw, x0=xcur, h=hcur, proj=proj, pdt=pdt, va=va, cpre=cpre, y2=y2,
                          hprev=hprev, cat=cat, mix=mix, x1=x1, h2=h2, f=f, dn=dn, dsk_lane=dsk_lane))
        xcur, hcur = x2, hnext

    dx, loss_part = _loss_fwd_bwd(xcur, loss_target.reshape(t, d), name="loss")
    loss = lax.psum(loss_part[0, 0], ("x", "y", "c"))

    small_grads = {n: [None] * depth for n in SMALL}
    big_out = {n: None for n in BIG}

    def finish(pending, after):
        li, st_a, st_b = pending

        def update(n, parts):
            big_out[n] = _sum_adamw(parts, W[n], M[n], V[n], layer=li, outs=big_out[n], name=f"adamw_{n}_{li}")

        p_down, p_up, p_out = _exchange_wait(st_a, after, name=f"scatter_wait_a_{li}")
        update("w_down", p_down)
        update("w_up", p_up)
        update("w_out", p_out)
        p_in, = _exchange_wait(st_b, after + [big_out["w_out"][0]], name=f"scatter_wait_b_{li}")
        update("w_in", p_in)

    pending = None
    for i in reversed(range(depth)):
        s = saved[i]
        lw = s["lw"]
        ddn, dg = _bwd_norm_out(s["dn"], vec("norm_mlp_post", i), dx, name=f"bwd_norm_mlp_post_{i}")
        small_grads["norm_mlp_post"][i] = dg
        dup = _mm(ddn, lw["wdown"], tb=True, name=f"bwd_down_dx_{i}", out_dtypes=(BF16,), epi=_epi_drelu2,
                  extras=(s["f"],))
        g_wdown = _mm(s["f"], ddn, ta=True, name=f"bwd_down_dw_{i}", out_dtypes=(BF16,))
        dh2 = _mm(dup, lw["wup"], tb=True, name=f"bwd_up_dx_{i}", out_dtypes=(BF16,))
        g_wup = _mm(s["h2"], dup, ta=True, name=f"bwd_up_dw_{i}", out_dtypes=(BF16,))
        dx1, dmix, dg_pre, dg_post = _bwd_norm_pair(s["x1"], [dh2], dx, s["mix"], vec("norm_mlp_pre", i),
                                                    vec("norm_mix_post", i), name=f"bwd_norm_mix_post_{i}")
        small_grads["norm_mlp_pre"][i] = dg_pre
        small_grads["norm_mix_post"][i] = dg_post
        dcat = _mm(dmix, lw["wout"], tb=True, name=f"bwd_out_dx_{i}", out_dtypes=(BF16,))
        g_wout = _mm(s["cat"], dmix, ta=True, name=f"bwd_out_dw_{i}", out_dtypes=(BF16,))
        st_a, tok_a = _exchange_start(
            [g_wdown.reshape(N_DEV, d_ff // N_DEV, d), _split_cols([g_wup], d_ff // N_DEV, name=f"split_g_w_up_{i}"),
             g_wout.reshape(N_DEV, d_mix // N_DEV, d)], gather=False, name=f"scatter_start_a_{i}")
        dproj, dcaw, dgca = _conva_bwd(dcat, s["proj"], s["va"], lw["conv_a"],
                                       vec("conv_out_norm", i) + tok_a[0:1, 0:1], d=d, seq=seq, name=f"bwd_conv_a_{i}")
        small_grads["conv_a_w"][i] = dcaw
        small_grads["conv_out_norm"][i] = dgca
        dconv, dproj, dpdt, dgs, ddsk, ddtb, dalog = _ssd_bwd(
            s["cpre"], s["pdt"], s["proj"], s["y2"], s["hprev"], dcat, vec("dt_bias", i), vec("a_log", i),
            s["dsk_lane"], vec("ssm_out_norm", i), emat, dproj, nseq=nseq, seq=seq, name=f"bwd_ssd_{i}")
        small_grads["ssm_out_norm"][i] = dgs
        small_grads["d_skip"][i] = ddsk
        small_grads["dt_bias"][i] = ddtb
        small_grads["a_log"][i] = dalog
        dproj, dscw, dscb = _convb_bwd(dconv, s["proj"], lw["ssm_conv"], dproj, col0=4 * d, seq=seq,
                                       name=f"bwd_conv_b_{i}")
        small_grads["ssm_conv_w"][i] = dscw
        small_grads["ssm_conv_b"][i] = dscb
        g_win = _split_cols([
            _mm(s["h"], dproj, ta=True, name=f"bwd_proj_dw_{i}", out_dtypes=(BF16,)),
            _mm(s["h"], dpdt, ta=True, name=f"bwd_proj_dt_dw_{i}", out_dtypes=(BF16,))],
            in_cols // N_DEV, name=f"split_g_w_in_{i}")
        st_b, tok_b = _exchange_start([g_win], gather=False, name=f"scatter_start_b_{i}")
        dh_parts = [_mm(dp, lw["win"], tb=True, b_koff=off, name=f"bwd_proj_{nm}dx_{i}", after=[tok_b], out_dtypes=(BF16,))
                    for nm, dp, off in (("", dproj, 0), ("dt_", dpdt, 4 * d + xbc))]
        dx, dg_in = _bwd_norm_in(s["x0"], dh_parts, dx1, vec("norm_mix_pre", i), name=f"bwd_norm_mix_pre_{i}")
        small_grads["norm_mix_pre"][i] = dg_in
        if pending is not None:
            finish(pending, [dx])
        pending = (i, st_a, st_b)

    grad_x = dx.reshape(nseq, seq, d)

    small_shapes_full = {n: (depth,) + tuple(small_grads[n][0].shape) for n in SMALL}
    gpack = _pack([jnp.stack(small_grads[n]) for n in SMALL])
    st_small, tok_small = _exchange_start([gpack], gather=True, name="allreduce_small_start")
    finish(pending, [dx, tok_small])
    gparts, = _exchange_wait(st_small, [big_out["w_in"][0]], name="allreduce_small_wait")

    def shard_of(n, full):
        if n == "conv_a_w":
            return lax.dynamic_slice_in_dim(full, me * ca_shard, ca_shard, axis=2)
        if n == "ssm_conv_w":
            return lax.dynamic_slice_in_dim(full, me * sc_shard, sc_shard, axis=2)
        return full.reshape(W[n].shape)

    gsum = _sum_parts(gparts, name="sum_small")
    gfull = _unpack(gsum, [small_shapes_full[n] for n in SMALL])
    gsmall = {n: shard_of(n, gf) for n, gf in zip(SMALL, gfull)}
    res = _sum_adamw(_pack([gsmall[n] for n in SMALL])[None], _pack([W[n] for n in SMALL]),
                     _pack([M[n] for n in SMALL]), _pack([V[n] for n in SMALL]), name="adamw_small")
    small_out = [dict(zip(SMALL, _unpack(r, [W[n].shape for n in SMALL]))) for r in res]

    def out_of(kind, n):
        return big_out[n][kind] if n in BIG else small_out[kind][n]

    return (loss, grad_x, *[out_of(k, n) for k in range(4) for n in ORDER])
```

```python
import jax
import jax.numpy as jnp
from jax import lax
from jax.experimental import pallas as pl
from jax.experimental.pallas import tpu as pltpu

F32 = jnp.float32
BF16 = jnp.bfloat16
HIGHEST = lax.Precision.HIGHEST
MESH = pl.DeviceIdType.MESH

EPS = 1e-6
HEAD_DIM = 64
D_STATE = 128
SSM_GROUPS = 2
CHUNK = 128
CONV_K = 3
SSM_CONV_K = 4
ADAM_LR = 0.001
ADAM_B1 = 0.9
ADAM_B2 = 0.999
ADAM_EPS = 1e-08
ADAM_WD = 0.01
ADAM_STEP = 10

N_DEV = 8
LANES = 128
SUBLANES = 8
VMEM_LIMIT = 48 * 1024 * 1024
ROW_TILE = 512
MM_TILE = 1024
MM_TILE_N = 2816
MM_VMEM_BUDGET = 40 * 1024 * 1024
FUSED_ROWS = 512


def _params(sem):
    return pltpu.CompilerParams(dimension_semantics=sem, vmem_limit_bytes=VMEM_LIMIT)


def _call(body, **kw):
    return pl.pallas_call(body, **kw)


def _pick(n, cap):
    best = None
    for t in range(LANES, min(n, cap) + 1, LANES):
        if n % t == 0:
            best = t
    return best or n


def _pick_rows(n, cap):
    best = None
    for t in range(SUBLANES, min(n, cap) + 1, SUBLANES):
        if n % t == 0:
            best = t
    return best or n


def _sigmoid(x):
    return 1.0 / (1.0 + jnp.exp(-x))


def _softplus(x):
    return jnp.maximum(x, 0.0) + jnp.log1p(jnp.exp(-jnp.abs(x)))


def _rms(x):
    return lax.rsqrt(jnp.mean(x * x, axis=-1, keepdims=True) + EPS)


def _rms_bwd(x, r, g, dy):
    gy = dy * g
    dx = r * gy - x * (r * r * r) * jnp.mean(gy * x, axis=-1, keepdims=True)
    return dx, dy * x * r


def _full(shape):
    return pl.BlockSpec(shape, lambda *_: (0,) * len(shape))


def _mm(a, b, *, name, ta=False, tb=False, out_dtypes=(F32,), epi=None, extras=(), n=None, b_off=0, b_koff=0,
        after=(), vecs=(), tm_cap=MM_TILE):
    m, k = (a.shape[1], a.shape[0]) if ta else a.shape
    if n is None:
        n = b.shape[0] if tb else b.shape[1]
    tm, tn, tk = _pick(m, tm_cap), _pick(n, MM_TILE_N), _pick(k, MM_TILE)
    while b_off % tn or n % tn:
        tn -= LANES
    if b_koff == 0 and k > MM_TILE:
        tk = _pick(k, MM_TILE_N)
    while b_koff % tk or k % tk:
        tk -= LANES

    def vmem_bytes(tk_):
        per_out = sum(jnp.dtype(dt).itemsize for dt in out_dtypes) + sum(e.dtype.itemsize for e in extras)
        return 2 * tk_ * (tm * a.dtype.itemsize + tn * b.dtype.itemsize) + tm * tn * (2 * per_out + 4)

    while vmem_bytes(tk) > MM_VMEM_BUDGET and tk % (2 * LANES) == 0 and not b_koff % (tk // 2):
        tk //= 2
    nk = k // tk
    nm, nn = m // tm, n // tn
    jo = b_off // tn
    ko = b_koff // tk
    a_bytes = m * k * a.dtype.itemsize
    b_bytes = n * k * b.dtype.itemsize
    m_outer = a_bytes + nm * b_bytes <= b_bytes + nn * a_bytes
    ij = (lambda g0, g1: (g0, g1)) if m_outer else (lambda g0, g1: (g1, g0))
    grid = (nm, nn, nk) if m_outer else (nn, nm, nk)

    def a_map(g0, g1, kk):
        i, _ = ij(g0, g1)
        return (kk, i) if ta else (i, kk)

    def b_map(g0, g1, kk):
        _, j = ij(g0, g1)
        return (j + jo, kk + ko) if tb else (kk + ko, j + jo)

    def o_map(g0, g1, kk):
        return ij(g0, g1)

    a_spec = pl.BlockSpec((tk, tm) if ta else (tm, tk), a_map)
    b_spec = pl.BlockSpec((tn, tk) if tb else (tk, tn), b_map)
    o_spec = pl.BlockSpec((tm, tn), o_map)
    dims = (((0 if ta else 1,), (1 if tb else 0,)), ((), ()))
    n_ex = len(extras) + len(vecs)
    after = list(after)
    o0 = 2 + n_ex + len(after)

    def finish(acc, ex, outs):
        res = (acc,) if epi is None else epi(acc, *[e[...] for e in ex])
        for o, r in zip(outs, res):
            o[...] = r.astype(o.dtype)

    def body_single(*refs):
        a_ref, b_ref = refs[:2]
        acc = lax.dot_general(a_ref[...].astype(BF16), b_ref[...].astype(BF16), dims, preferred_element_type=F32)
        finish(acc, refs[2:2 + n_ex], refs[o0:])

    def body_multi(*refs):
        a_ref, b_ref = refs[:2]
        acc = refs[-1]
        kk = pl.program_id(2)

        @pl.when(kk == 0)
        def _():
            acc[...] = jnp.zeros_like(acc)

        acc[...] += lax.dot_general(a_ref[...].astype(BF16), b_ref[...].astype(BF16), dims, preferred_element_type=F32)

        @pl.when(kk == nk - 1)
        def _():
            finish(acc[...], refs[2:2 + n_ex], refs[o0:-1])

    v_spec = pl.BlockSpec((1, tn), lambda g0, g1, kk: (0, ij(g0, g1)[1]))
    outs = _call(
        body_single if nk == 1 else body_multi, name=name, grid=grid,
        in_specs=([a_spec, b_spec] + [o_spec] * len(extras) + [v_spec] * len(vecs)
                  + [pl.BlockSpec(memory_space=pl.ANY)] * len(after)),
        out_specs=[o_spec] * len(out_dtypes),
        out_shape=[jax.ShapeDtypeStruct((m, n), dt) for dt in out_dtypes],
        scratch_shapes=[] if nk == 1 else [pltpu.VMEM((tm, tn), F32)],
        compiler_params=_params(("parallel", "parallel", "arbitrary")),
    )(a, b, *extras, *vecs, *after)
    return outs[0] if len(outs) == 1 else outs


def _epi_resid_norm(acc, x, g_res, g_next):
    xn = x + acc * _rms(acc) * g_res
    return xn, xn * _rms(xn) * g_next, acc


def _epi_relu2(acc):
    r = jnp.maximum(acc, 0.0)
    return (r * r,)


def _epi_drelu2(acc, f):
    return (acc * (2.0 * jnp.sqrt(f).astype(F32)),)


def _norm_fwd(x, g, *, name):
    t, d = x.shape
    tt = _pick_rows(t, ROW_TILE)

    def body(x_ref, g_ref, h_ref):
        xv = x_ref[...]
        h_ref[...] = (xv * _rms(xv) * g_ref[...]).astype(BF16)

    row = pl.BlockSpec((tt, d), lambda i: (i, 0))
    return _call(body, name=name, grid=(t // tt,), in_specs=[row, _full((1, d))], out_specs=row,
                 out_shape=jax.ShapeDtypeStruct((t, d), BF16), compiler_params=_params(("parallel",)))(x, g)


def _resid_norm(x, n, g1, g2, *, name):
    t, d = x.shape
    tt = _pick_rows(t, ROW_TILE)
    gains = [g1] if g2 is None else [g1, g2]

    def body(x_ref, n_ref, *refs):
        nv = n_ref[...].astype(F32)
        xn = x_ref[...] + nv * _rms(nv) * refs[0][...]
        refs[len(gains)][...] = xn
        if g2 is not None:
            refs[3][...] = (xn * _rms(xn) * refs[1][...]).astype(BF16)

    row = pl.BlockSpec((tt, d), lambda i: (i, 0))
    outs = _call(body, name=name, grid=(t // tt,), in_specs=[row, row] + [_full((1, d))] * len(gains),
                 out_specs=[row] * len(gains),
                 out_shape=[jax.ShapeDtypeStruct((t, d), F32), jax.ShapeDtypeStruct((t, d), BF16)][:len(gains)],
                 compiler_params=_params(("parallel",)))(x, n, *gains)
    return (outs[0], None) if g2 is None else outs


def _loss_fwd_bwd(xf, target, *, name):
    t, d = xf.shape
    tt = _pick_rows(t, ROW_TILE)
    nt = t // tt

    def body(x_ref, t_ref, dy_ref, loss_ref, acc):
        i = pl.program_id(0)

        @pl.when(i == 0)
        def _():
            acc[...] = jnp.zeros_like(acc)

        e = x_ref[...] - t_ref[...]
        dy_ref[...] = e * (1.0 / d)
        acc[...] += jnp.sum(e * e, axis=0, keepdims=True)

        @pl.when(i == nt - 1)
        def _():
            loss_ref[...] = jnp.sum(acc[...], axis=-1, keepdims=True) * (0.5 / d)

    row = pl.BlockSpec((tt, d), lambda i: (i, 0))
    return _call(body, name=name, grid=(nt,), in_specs=[row, row], out_specs=[row, _full((1, 1))],
                 out_shape=[jax.ShapeDtypeStruct((t, d), F32), jax.ShapeDtypeStruct((1, 1), F32)],
                 scratch_shapes=[pltpu.VMEM((1, d), F32)], compiler_params=_params(("arbitrary",)))(xf, target)


def _bwd_norm_pair(xin, dh, dres, n, g_in, g_out, *, name):
    t, d = xin.shape
    tt = _pick_rows(t, ROW_TILE)
    n_dh = len(dh)

    def body(*refs):
        x_ref = refs[0]
        dh_refs = refs[1:1 + n_dh]
        dres_ref, n_ref, gi_ref, go_ref, dx_ref, dn_ref, dgi_ref, dgo_ref = refs[1 + n_dh:]
        i = pl.program_id(0)

        @pl.when(i == 0)
        def _():
            dgi_ref[...] = jnp.zeros_like(dgi_ref)
            dgo_ref[...] = jnp.zeros_like(dgo_ref)

        xv = x_ref[...]
        dhv = dh_refs[0][...].astype(F32)
        for r in dh_refs[1:]:
            dhv = dhv + r[...].astype(F32)
        dxh, dgi = _rms_bwd(xv, _rms(xv), gi_ref[...], dhv)
        dx = dres_ref[...] + dxh
        dx_ref[...] = dx
        dgi_ref[...] += jnp.sum(dgi, axis=0, keepdims=True)
        nv = n_ref[...].astype(F32)
        dn, dgo = _rms_bwd(nv, _rms(nv), go_ref[...], dx)
        dn_ref[...] = dn.astype(BF16)
        dgo_ref[...] += jnp.sum(dgo, axis=0, keepdims=True)

    row = pl.BlockSpec((tt, d), lambda i: (i, 0))
    vec = _full((1, d))
    return _call(body, name=name, grid=(t // tt,), in_specs=[row] * (n_dh + 3) + [vec, vec],
                 out_specs=[row, row, vec, vec],
                 out_shape=[jax.ShapeDtypeStruct((t, d), F32), jax.ShapeDtypeStruct((t, d), BF16),
                            jax.ShapeDtypeStruct((1, d), F32), jax.ShapeDtypeStruct((1, d), F32)],
                 compiler_params=_params(("arbitrary",)))(xin, *dh, dres, n, g_in, g_out)


def _bwd_norm_in(xin, dh, dres, g_in, *, name):
    t, d = xin.shape
    tt = _pick_rows(t, ROW_TILE)
    n_dh = len(dh)

    def body(*refs):
        x_ref = refs[0]
        dh_refs = refs[1:1 + n_dh]
        dres_ref, gi_ref, dx_ref, dgi_ref = refs[1 + n_dh:]
        i = pl.program_id(0)

        @pl.when(i == 0)
        def _():
            dgi_ref[...] = jnp.zeros_like(dgi_ref)

        xv = x_ref[...]
        dhv = dh_refs[0][...].astype(F32)
        for r in dh_refs[1:]:
            dhv = dhv + r[...].astype(F32)
        dxh, dgi = _rms_bwd(xv, _rms(xv), gi_ref[...], dhv)
        dx_ref[...] = dres_ref[...] + dxh
        dgi_ref[...] += jnp.sum(dgi, axis=0, keepdims=True)

    row = pl.BlockSpec((tt, d), lambda i: (i, 0))
    vec = _full((1, d))
    return _call(body, name=name, grid=(t // tt,), in_specs=[row] * (n_dh + 2) + [vec],
                 out_specs=[row, vec],
                 out_shape=[jax.ShapeDtypeStruct((t, d), F32), jax.ShapeDtypeStruct((1, d), F32)],
                 compiler_params=_params(("arbitrary",)))(xin, *dh, dres, g_in)


def _bwd_norm_out(n, g_out, dx, *, name):
    t, d = n.shape
    tt = _pick_rows(t, ROW_TILE)

    def body(n_ref, go_ref, dx_ref, dn_ref, dgo_ref):
        i = pl.program_id(0)

        @pl.when(i == 0)
        def _():
            dgo_ref[...] = jnp.zeros_like(dgo_ref)

        nv = n_ref[...].astype(F32)
        dn, dgo = _rms_bwd(nv, _rms(nv), go_ref[...], dx_ref[...])
        dn_ref[...] = dn.astype(BF16)
        dgo_ref[...] += jnp.sum(dgo, axis=0, keepdims=True)

    row = pl.BlockSpec((tt, d), lambda i: (i, 0))
    vec = _full((1, d))
    return _call(body, name=name, grid=(t // tt,), in_specs=[row, vec, row], out_specs=[row, vec],
                 out_shape=[jax.ShapeDtypeStruct((t, d), BF16), jax.ShapeDtypeStruct((1, d), F32)],
                 compiler_params=_params(("arbitrary",)))(n, g_out, dx)


def _shift_down(cur, halo, s):
    return jnp.concatenate([halo[SUBLANES - s:], cur[:cur.shape[0] - s]], axis=0)


def _shift_up(cur, halo, s):
    return jnp.concatenate([cur[s:], halo[:s]], axis=0)


def _conva_fwd(pa, w, g, *, d, seq, name):
    t = pa.shape[0]
    tt = _pick_rows(seq, ROW_TILE)
    tps = seq // tt

    def body(xa_ref, ca_ref, ba_ref, w_ref, g_ref, ya_ref, v_ref, carry):
        i = pl.program_id(0)

        @pl.when(i % tps == 0)
        def _():
            carry[...] = jnp.zeros_like(carry)

        u = ca_ref[...].astype(F32) * xa_ref[...].astype(F32)
        halo = carry[...]
        wv = w_ref[...]
        v = wv[2:3] * u + wv[1:2] * _shift_down(u, halo, 1) + wv[0:1] * _shift_down(u, halo, 2)
        carry[...] = u[tt - SUBLANES:]
        yp = ba_ref[...].astype(F32) * v
        ya_ref[...] = (yp * _rms(yp) * g_ref[...]).astype(BF16)
        v_ref[...] = v.astype(BF16)

    col = lambda c: pl.BlockSpec((tt, d), lambda i, c=c: (i, c))
    row = pl.BlockSpec((tt, d), lambda i: (i, 0))
    return _call(body, name=name, grid=(t // tt,),
                 in_specs=[col(0), col(1), col(2), _full((CONV_K, d)), _full((1, d))], out_specs=[row, row],
                 out_shape=[jax.ShapeDtypeStruct((t, d), BF16), jax.ShapeDtypeStruct((t, d), BF16)],
                 scratch_shapes=[pltpu.VMEM((SUBLANES, d), F32)],
                 compiler_params=_params(("arbitrary",)))(pa, pa, pa, w, g)


def _conva_bwd(dcat, pa, v, w, g, *, d, seq, name):
    t, width = pa.shape
    d3 = 3 * d
    tt = _pick_rows(seq, ROW_TILE)
    tps = seq // tt
    nt = t // tt

    def body(dya_ref, xa_ref, ca_ref, ba_ref, v_ref, w_ref, g_ref, dpa_ref, dw_ref, dg_ref, carry):
        i = pl.program_id(0)

        @pl.when(i == 0)
        def _():
            dw_ref[...] = jnp.zeros_like(dw_ref)
            dg_ref[...] = jnp.zeros_like(dg_ref)

        @pl.when(i % tps == 0)
        def _():
            carry[...] = jnp.zeros_like(carry)

        xa, ca, ba, vv = [r[...].astype(F32) for r in (xa_ref, ca_ref, ba_ref, v_ref)]
        yp = ba * vv
        dyp, dgt = _rms_bwd(yp, _rms(yp), g_ref[...], dya_ref[...].astype(F32))
        dg_ref[...] += jnp.sum(dgt, axis=0, keepdims=True)
        dv = dyp * ba
        halo = carry[...]
        dv1 = _shift_up(dv, halo, 1)
        dv2 = _shift_up(dv, halo, 2)
        carry[...] = dv[:SUBLANES]
        wv = w_ref[...]
        du = wv[2:3] * dv + wv[1:2] * dv1 + wv[0:1] * dv2
        u = ca * xa
        dw_ref[0:1, :] += jnp.sum(u * dv2, axis=0, keepdims=True)
        dw_ref[1:2, :] += jnp.sum(u * dv1, axis=0, keepdims=True)
        dw_ref[2:3, :] += jnp.sum(u * dv, axis=0, keepdims=True)
        dpa_ref[:, 0:d] = (du * ca).astype(BF16)
        dpa_ref[:, d:2 * d] = (du * xa).astype(BF16)
        dpa_ref[:, 2 * d:3 * d] = (dyp * vv).astype(BF16)

    rcol = lambda c: pl.BlockSpec((tt, d), lambda i, c=c: (nt - 1 - i, c))
    return _call(body, name=name, grid=(nt,),
                 in_specs=[rcol(0), rcol(0), rcol(1), rcol(2), rcol(0), _full((CONV_K, d)), _full((1, d))],
                 out_specs=[pl.BlockSpec((tt, d3), lambda i: (nt - 1 - i, 0)), _full((CONV_K, d)), _full((1, d))],
                 out_shape=[jax.ShapeDtypeStruct((t, width), BF16), jax.ShapeDtypeStruct((CONV_K, d), F32),
                            jax.ShapeDtypeStruct((1, d), F32)],
                 scratch_shapes=[pltpu.VMEM((SUBLANES, d), F32)],
                 compiler_params=_params(("arbitrary",)))(dcat, pa, pa, pa, v, w, g)


CONV_CH = 512


def _convb_fwd(proj, w, bias, *, col0, seq, name):
    t = proj.shape[0]
    c = w.shape[1]
    cb = _pick(c, CONV_CH)
    assert col0 % cb == 0
    tt = _pick_rows(seq, 2 * ROW_TILE)
    tps = seq // tt

    def body(p_ref, w_ref, b_ref, o_ref, carry):
        i = pl.program_id(1)

        @pl.when(i % tps == 0)
        def _():
            carry[...] = jnp.zeros_like(carry)

        p = p_ref[...].astype(F32)
        halo = carry[...]
        wv = w_ref[...]
        o = wv[3:4] * p + b_ref[...]
        for s in (1, 2, 3):
            o = o + wv[3 - s:4 - s] * _shift_down(p, halo, s)
        carry[...] = p[tt - SUBLANES:]
        o_ref[...] = o.astype(BF16)

    return _call(body, name=name, grid=(c // cb, t // tt),
                 in_specs=[pl.BlockSpec((tt, cb), lambda jc, i: (i, col0 // cb + jc)),
                           pl.BlockSpec((SSM_CONV_K, cb), lambda jc, i: (0, jc)), pl.BlockSpec((1, cb), lambda jc, i: (0, jc))],
                 out_specs=pl.BlockSpec((tt, cb), lambda jc, i: (i, jc)), out_shape=jax.ShapeDtypeStruct((t, c), BF16),
                 scratch_shapes=[pltpu.VMEM((SUBLANES, cb), F32)],
                 compiler_params=_params(("arbitrary", "arbitrary")))(proj, w, bias)


def _convb_bwd(dconv, proj, w, dproj, *, col0, seq, name):
    t, c = dconv.shape
    cb = _pick(c, CONV_CH)
    assert col0 % cb == 0
    tt = _pick_rows(seq, 2 * ROW_TILE)
    tps = seq // tt
    nt = t // tt

    def body(dc_ref, p_ref, w_ref, dproj_in, dp_ref, dw_ref, db_ref, carry):
        del dproj_in
        i = pl.program_id(1)

        @pl.when(i == 0)
        def _():
            dw_ref[...] = jnp.zeros_like(dw_ref)
            db_ref[...] = jnp.zeros_like(db_ref)

        @pl.when(i % tps == 0)
        def _():
            carry[...] = jnp.zeros_like(carry)

        dc = dc_ref[...].astype(F32)
        p = p_ref[...].astype(F32)
        halo = carry[...]
        wv = w_ref[...]
        dp = wv[3:4] * dc
        dw_ref[3:4, :] += jnp.sum(p * dc, axis=0, keepdims=True)
        for s in (1, 2, 3):
            dcs = _shift_up(dc, halo, s)
            dp = dp + wv[3 - s:4 - s] * dcs
            dw_ref[3 - s:4 - s, :] += jnp.sum(p * dcs, axis=0, keepdims=True)
        carry[...] = dc[:SUBLANES]
        db_ref[...] += jnp.sum(dc, axis=0, keepdims=True)
        dp_ref[...] = dp.astype(BF16)

    win_spec = pl.BlockSpec((tt, cb), lambda jc, i: (nt - 1 - i, col0 // cb + jc))
    taps = pl.BlockSpec((SSM_CONV_K, cb), lambda jc, i: (0, jc))
    return _call(body, name=name, grid=(c // cb, nt),
                 in_specs=[pl.BlockSpec((tt, cb), lambda jc, i: (nt - 1 - i, jc)), win_spec, taps,
                           pl.BlockSpec(memory_space=pl.ANY)],
                 out_specs=[win_spec, taps, pl.BlockSpec((1, cb), lambda jc, i: (0, jc))],
                 out_shape=[jax.ShapeDtypeStruct(dproj.shape, BF16), jax.ShapeDtypeStruct((SSM_CONV_K, c), F32),
                            jax.ShapeDtypeStruct((1, c), F32)],
                 input_output_aliases={3: 0},
                 scratch_shapes=[pltpu.VMEM((SUBLANES, cb), F32)],
                 compiler_params=_params(("arbitrary", "arbitrary")))(dconv, proj, w, dproj)


def _expand_heads(x, ev):
    return jnp.dot(x, ev, precision=HIGHEST, preferred_element_type=F32)


def _head_sums(v, ev):
    return lax.dot_general(v, ev, (((1,), (1,)), ((), ())), precision=HIGHEST, preferred_element_type=F32)


def _ssd_common(c_ref, pdt_ref, dtb_ref, alog_ref, e_ref, h):
    cp = c_ref[...].astype(F32)
    sg = _sigmoid(cp)
    act = cp * sg
    pre = pdt_ref[:, 0:h] + dtb_ref[...]
    dt = _softplus(pre)
    a = -jnp.exp(alog_ref[...])
    adt = dt * a
    row = lax.broadcasted_iota(jnp.int32, (CHUNK, CHUNK), 0)
    col = lax.broadcasted_iota(jnp.int32, (CHUNK, CHUNK), 1)
    tril = row >= col
    cs = jnp.dot(tril.astype(F32), adt, precision=HIGHEST, preferred_element_type=F32)
    cs_t = lax.dot_general(adt, (col >= row).astype(F32), (((0,), (0,)), ((), ())), precision=HIGHEST,
                           preferred_element_type=F32)
    ev = e_ref[...]
    dt_l = _expand_heads(dt, ev)
    ecs_l = jnp.exp(_expand_heads(cs, ev))
    return dict(cp=cp, sg=sg, act=act, pre=pre, dt=dt, a=a, cs=cs, cs_t=cs_t, dt_l=dt_l, ecs_l=ecs_l,
                tril=tril, row=row, col=col, lo=col < HEAD_DIM)


def _dot_nt(a, b):
    return lax.dot_general(a, b, (((1,), (1,)), ((), ())), preferred_element_type=F32)


def _dot_tn(a, b):
    return lax.dot_general(a, b, (((0,), (0,)), ((), ())), preferred_element_type=F32)


def _dot(a, b):
    return jnp.dot(a, b, preferred_element_type=F32)


def _ssd_fwd(cpre, pdt, pz, ya, dtb, alog, dsk_lane, gs, emat, *, nseq, seq, name):
    t, xbc = cpre.shape
    d = ya.shape[1]
    h = d // HEAD_DIM
    npair = h // 2
    ppg = npair // SSM_GROUPS
    nc = seq // CHUNK
    gw = d // SSM_GROUPS
    bc0 = d
    cc0 = d + SSM_GROUPS * D_STATE

    def body(c_ref, pdt_ref, z_ref, ya_ref, dtb_ref, alog_ref, dsk_ref, gs_ref, e_ref, cat_ref, y2_ref, hp_ref, h_ref):
        @pl.when(pl.program_id(0) == 0)
        def _():
            h_ref[...] = jnp.zeros_like(h_ref)

        for sq in range(nseq):
            one_seq(c_ref.at[sq], pdt_ref.at[sq], z_ref.at[sq], ya_ref.at[sq], dtb_ref, alog_ref, dsk_ref, gs_ref, e_ref,
                    cat_ref.at[sq], y2_ref.at[sq], hp_ref.at[sq], h_ref.at[sq])

    def one_seq(c_ref, pdt_ref, z_ref, ya_ref, dtb_ref, alog_ref, dsk_ref, gs_ref, e_ref, cat_ref, y2_ref, hp_ref, h_ref):
        q = _ssd_common(c_ref, pdt_ref, dtb_ref, alog_ref, e_ref, h)
        act, cs, lo, ecs_l = q["act"], q["cs"], q["lo"], q["ecs_l"]
        xs = act[:, :d]
        xd = xs * q["dt_l"]
        ys = []
        for g in range(SSM_GROUPS):
            bg = act[:, bc0 + g * D_STATE: bc0 + (g + 1) * D_STATE]
            cgb = act[:, cc0 + g * D_STATE: cc0 + (g + 1) * D_STATE].astype(BF16)
            s = _dot_nt(cgb, bg.astype(BF16))
            bg_t = bg.T
            for jj in range(ppg):
                j = g * ppg + jj
                sl = slice(LANES * j, LANES * (j + 1))
                xdj = xd[:, sl]
                x2 = jnp.concatenate([jnp.where(lo, xdj, 0.0), jnp.where(lo, 0.0, xdj)], axis=0).astype(BF16)
                hprev = h_ref[j]
                hp_ref[j] = hprev.astype(BF16)
                ms, bws_t = [], []
                for hh in (2 * j, 2 * j + 1):
                    csc = cs[:, hh:hh + 1]
                    cs_row = q["cs_t"][hh:hh + 1, :]
                    seg = jnp.broadcast_to(csc, (CHUNK, CHUNK)) - jnp.broadcast_to(cs_row, (CHUNK, CHUNK))
                    ms.append(s * jnp.exp(jnp.where(q["tril"], seg, -jnp.inf)))
                    bws_t.append(bg_t * jnp.exp(cs_row[:, CHUNK - 1:CHUNK] - cs_row))
                ydiag = _dot(jnp.concatenate(ms, axis=1).astype(BF16), x2)
                st = _dot(jnp.concatenate(bws_t, axis=1).astype(BF16), x2)
                ecs = ecs_l[:, sl]
                yoff = _dot(cgb, hprev.astype(BF16)) * ecs
                h_ref[j] = hprev * ecs[CHUNK - 1:CHUNK] + st
                ys.append(ydiag + yoff)
        y = jnp.concatenate(ys, axis=1) + dsk_ref[...] * xs
        y2_ref[...] = y.astype(BF16)
        zv = z_ref[...].astype(F32)
        y3 = y * (zv * _sigmoid(zv))
        cat_ref[:, 0:d] = ya_ref[...]
        for gi in range(SSM_GROUPS):
            seg = y3[:, gi * gw:(gi + 1) * gw]
            cat_ref[:, d + gi * gw:d + (gi + 1) * gw] = (seg * _rms(seg) * gs_ref[:, gi * gw:(gi + 1) * gw]).astype(BF16)

    chunk = lambda w, cb=0: pl.BlockSpec((nseq, CHUNK, w), lambda c, cb=cb: (0, c, cb))
    vec = lambda w: pl.BlockSpec((1, w), lambda c: (0, 0))
    hp_spec = pl.BlockSpec((nseq, None, npair, D_STATE, LANES), lambda c: (0, c, 0, 0, 0))
    per_seq = lambda a: a.reshape(nseq, seq, a.shape[1])
    cat, y2, hp = _call(
        body, name=name, grid=(nc,),
        in_specs=[chunk(xbc), chunk(LANES), chunk(d, 3), chunk(d), vec(h), vec(h), vec(d), vec(d),
                  pl.BlockSpec((h, d), lambda c: (0, 0))],
        out_specs=[chunk(2 * d), chunk(d), hp_spec],
        out_shape=[jax.ShapeDtypeStruct((nseq, seq, 2 * d), BF16), jax.ShapeDtypeStruct((nseq, seq, d), BF16),
                   jax.ShapeDtypeStruct((nseq, nc, npair, D_STATE, LANES), BF16)],
        scratch_shapes=[pltpu.VMEM((nseq, npair, D_STATE, LANES), F32)],
        compiler_params=_params(("arbitrary",)))(
            per_seq(cpre), per_seq(pdt), per_seq(pz), per_seq(ya), dtb, alog, dsk_lane, gs, emat)
    return cat.reshape(t, 2 * d), y2.reshape(t, d), hp


def _ssd_bwd(cpre, pdt, pz, y2, hprev_all, dcat, dtb, alog, dsk_lane, gs, emat, dproj, *, nseq, seq, name):
    t, xbc = cpre.shape
    d = y2.shape[1]
    h = d // HEAD_DIM
    npair = h // 2
    ppg = npair // SSM_GROUPS
    nc = seq // CHUNK
    gw = d // SSM_GROUPS
    bc0 = d
    cc0 = d + SSM_GROUPS * D_STATE

    def body(c_ref, pdt_ref, z_ref, y2_ref, hp_ref, dys_ref, dtb_ref, alog_ref, dsk_ref, gs_ref, e_ref, dproj_in,
             dconv_ref, dz_ref, dpdt_ref, dgs_ref, ddsk_ref, ddtb_ref, dalog_ref, dh_ref):
        del dproj_in
        b = pl.program_id(0)
        c = pl.program_id(1)

        @pl.when(c == 0)
        def _():
            dh_ref[...] = jnp.zeros_like(dh_ref)

        @pl.when((b == 0) & (c == 0))
        def _():
            dgs_ref[...] = jnp.zeros_like(dgs_ref)
            ddsk_ref[...] = jnp.zeros_like(ddsk_ref)
            ddtb_ref[...] = jnp.zeros_like(ddtb_ref)
            dalog_ref[...] = jnp.zeros_like(dalog_ref)

        q = _ssd_common(c_ref, pdt_ref, dtb_ref, alog_ref, e_ref, h)
        cp, sg, act, cs, a, dt, lo = q["cp"], q["sg"], q["act"], q["cs"], q["a"], q["dt"], q["lo"]
        ecs_l, dt_l = q["ecs_l"], q["dt_l"]
        ev = e_ref[...]
        xs = act[:, :d]
        xd = xs * dt_l
        row16 = lax.broadcasted_iota(jnp.int32, (CHUNK, h), 0)
        hid = lax.broadcasted_iota(jnp.int32, (1, h), 1)
        hid_t = lax.broadcasted_iota(jnp.int32, (h, 1), 0)

        zv = z_ref[...].astype(F32)
        sz = _sigmoid(zv)
        siluz = zv * sz
        y2v = y2_ref[...].astype(F32)
        y3 = y2v * siluz
        dysv = dys_ref[...].astype(F32)
        dy3s = []
        for gi in range(SSM_GROUPS):
            gsl = slice(gi * gw, (gi + 1) * gw)
            seg = y3[:, gsl]
            dseg, dgt = _rms_bwd(seg, _rms(seg), gs_ref[:, gsl], dysv[:, gsl])
            dy3s.append(dseg)
            dgs_ref[:, gsl] += jnp.sum(dgt, axis=0, keepdims=True)
        dy3 = jnp.concatenate(dy3s, axis=1)
        dy = dy3 * siluz
        dz_ref[...] = (dy3 * y2v * (sz * (1.0 + zv * (1.0 - sz)))).astype(BF16)
        ddsk_ref[...] += jnp.sum(_head_sums(dy * xs, ev), axis=0, keepdims=True)

        dcs = jnp.zeros((CHUNK, h), F32)
        dcs_t = jnp.zeros((h, CHUNK), F32)
        dxd_parts, yoff_parts, db_parts, dc_parts = [], [], [], []
        for g in range(SSM_GROUPS):
            bg = act[:, bc0 + g * D_STATE: bc0 + (g + 1) * D_STATE]
            cg = act[:, cc0 + g * D_STATE: cc0 + (g + 1) * D_STATE]
            bgb, cgb = bg.astype(BF16), cg.astype(BF16)
            cgb_t = cg.T.astype(BF16)
            s = _dot_nt(cgb, bgb)
            ds = jnp.zeros((CHUNK, CHUNK), F32)
            dbg = jnp.zeros((CHUNK, D_STATE), F32)
            dcg = jnp.zeros((CHUNK, D_STATE), F32)
            for jj in range(ppg):
                j = g * ppg + jj
                sl = slice(LANES * j, LANES * (j + 1))
                xdj = xd[:, sl]
                xdb = xdj.astype(BF16)
                x2 = jnp.concatenate([jnp.where(lo, xdj, 0.0), jnp.where(lo, 0.0, xdj)], axis=0).astype(BF16)
                dyj = dy[:, sl]
                dy2 = jnp.concatenate([jnp.where(lo, dyj, 0.0), jnp.where(lo, 0.0, dyj)], axis=0).astype(BF16)
                hpb = hp_ref[j]
                hprev = hpb.astype(F32)
                dhn = dh_ref[j]
                dhb = dhn.astype(BF16)
                dh2 = jnp.concatenate([jnp.where(lo, dhn, 0.0), jnp.where(lo, 0.0, dhn)], axis=0).astype(BF16)
                ecs = ecs_l[:, sl]
                gmat = (dyj * ecs).astype(BF16)
                yoff_parts.append(_dot(cgb, hpb) * ecs)
                dcg = dcg + _dot_nt(gmat, hpb)
                dh_ref[j] = dhn * ecs[CHUNK - 1:CHUNK] + _dot(cgb_t, gmat)
                t2 = dhn * hprev
                dbw2 = _dot_nt(x2, dhb)
                dm2 = _dot_nt(dy2, xdb)
                ms, bws = [], []
                for idx, hh in enumerate((2 * j, 2 * j + 1)):
                    msk = lo if idx == 0 else jnp.logical_not(lo)
                    onehot = (hid == hh).astype(F32)
                    csc = cs[:, hh:hh + 1]
                    seg = jnp.broadcast_to(csc, (CHUNK, CHUNK)) - jnp.broadcast_to(q["cs_t"][hh:hh + 1, :], (CHUNK, CHUNK))
                    lm = jnp.exp(jnp.where(q["tril"], seg, -jnp.inf))
                    m = s * lm
                    cs_last = cs[CHUNK - 1:CHUNK, hh:hh + 1]
                    dte = jnp.exp(cs_last - csc)
                    ms.append(m)
                    bws.append(bg * dte)
                    dbw = dbw2[idx * CHUNK:(idx + 1) * CHUNK]
                    dbg = dbg + dbw * dte
                    qv = jnp.sum(dbw * bg, axis=-1, keepdims=True) * dte
                    dm = dm2[idx * CHUNK:(idx + 1) * CHUNK]
                    wm = dm * m
                    rc = jnp.sum(wm, axis=-1, keepdims=True)
                    dcs_t = dcs_t - (hid_t == hh).astype(F32) * jnp.sum(wm, axis=0, keepdims=True)
                    ds = ds + dm * lm
                    ddec = jnp.sum(jnp.where(msk, t2, 0.0)) * jnp.exp(cs_last)
                    last = jnp.sum(qv) + ddec
                    dcs = dcs + (rc - qv) * onehot + jnp.where(row16 == CHUNK - 1, last * onehot, 0.0)
                dxd_s = _dot(jnp.concatenate(bws, axis=1).astype(BF16), dh2)
                dxd_d = _dot_tn(jnp.concatenate(ms, axis=0).astype(BF16), dy2)
                dxd_parts.append(dxd_s + dxd_d)
            dsb = ds.astype(BF16)
            dc_parts.append(dcg + _dot(dsb, bgb))
            db_parts.append(dbg + _dot_tn(dsb, cgb))
        yoff_all = jnp.concatenate(yoff_parts, axis=1)
        dxd_all = jnp.concatenate(dxd_parts, axis=1)
        dcs = dcs + _head_sums(dy * yoff_all, ev)
        triu = (q["col"] >= q["row"]).astype(F32)
        dadt = (jnp.dot(triu, dcs, precision=HIGHEST, preferred_element_type=F32)
                + lax.dot_general(triu, dcs_t, (((1,), (1,)), ((), ())), precision=HIGHEST, preferred_element_type=F32))
        ddt = dadt * a + _head_sums(dxd_all * xs, ev)
        dalog_ref[...] += jnp.sum(dadt * dt, axis=0, keepdims=True) * a
        dpre = ddt * _sigmoid(q["pre"])
        ddtb_ref[...] += jnp.sum(dpre, axis=0, keepdims=True)
        dpdt_ref[...] = jnp.zeros_like(dpdt_ref)
        dpdt_ref[:, 0:h] = dpre.astype(BF16)
        dxs = dxd_all * dt_l + dy * dsk_ref[...]
        dact = jnp.concatenate([dxs] + db_parts + dc_parts, axis=1)
        dconv_ref[...] = (dact * (sg * (1.0 + cp * (1.0 - sg)))).astype(BF16)

    rchunk = lambda w, cb=0: pl.BlockSpec((CHUNK, w), lambda b, c, cb=cb: (b * nc + nc - 1 - c, cb))
    vec = lambda w: pl.BlockSpec((1, w), lambda b, c: (0, 0))
    hp_spec = pl.BlockSpec((None, None, npair, D_STATE, LANES), lambda b, c: (b, nc - 1 - c, 0, 0, 0))
    return _call(body, name=name, grid=(nseq, nc),
                 in_specs=[rchunk(xbc), rchunk(LANES), rchunk(d, 3), rchunk(d), hp_spec, rchunk(d, 1),
                           vec(h), vec(h), vec(d), vec(d), pl.BlockSpec((h, d), lambda b, c: (0, 0)),
                           pl.BlockSpec(memory_space=pl.ANY)],
                 out_specs=[rchunk(xbc), rchunk(d, 3), rchunk(LANES), vec(d), vec(h), vec(h), vec(h)],
                 out_shape=[jax.ShapeDtypeStruct((t, xbc), BF16), jax.ShapeDtypeStruct(dproj.shape, BF16),
                            jax.ShapeDtypeStruct((t, LANES), BF16), jax.ShapeDtypeStruct((1, d), F32),
                            jax.ShapeDtypeStruct((1, h), F32), jax.ShapeDtypeStruct((1, h), F32),
                            jax.ShapeDtypeStruct((1, h), F32)],
                 input_output_aliases={11: 1},
                 scratch_shapes=[pltpu.VMEM((npair, D_STATE, LANES), F32)],
                 compiler_params=_params(("arbitrary", "arbitrary")))(
                     cpre, pdt, pz, y2, hprev_all, dcat, dtb, alog, dsk_lane, gs, emat, dproj)


def _sum_adamw(parts, w, m, v, *, name, layer=None, outs=None):
    n, r, c = parts.shape
    tr = _pick_rows(r, 256)
    bc1 = 1.0 - ADAM_B1 ** ADAM_STEP
    bc2 = 1.0 - ADAM_B2 ** ADAM_STEP

    def body(p_ref, w_ref, m_ref, v_ref, *rest):
        g_ref, d_ref, mo_ref, vo_ref = rest[-4:]
        g = p_ref[0].astype(F32)
        for k in range(1, n):
            g = g + p_ref[k].astype(F32)
        mn = ADAM_B1 * m_ref[...] + (1.0 - ADAM_B1) * g
        vn = ADAM_B2 * v_ref[...] + (1.0 - ADAM_B2) * (g * g)
        g_ref[...] = g
        mo_ref[...] = mn
        vo_ref[...] = vn
        d_ref[...] = -ADAM_LR * ((mn / bc1) / (jnp.sqrt(vn / bc2) + ADAM_EPS) + ADAM_WD * w_ref[...])

    p_spec = pl.BlockSpec((n, tr, c), lambda i: (0, i, 0))
    if layer is None:
        blk = pl.BlockSpec((tr, c), lambda i: (i, 0))
        return _call(body, name=name, grid=(r // tr,), in_specs=[p_spec, blk, blk, blk], out_specs=[blk] * 4,
                     out_shape=[jax.ShapeDtypeStruct((r, c), F32)] * 4,
                     compiler_params=_params(("parallel",)))(parts, w, m, v)
    blk = pl.BlockSpec((None, tr, c), lambda i: (layer, i, 0))
    if outs is None:
        outs = [lax.empty(w.shape, F32) for _ in range(4)]
    return _call(body, name=name, grid=(r // tr,),
                 in_specs=[p_spec, blk, blk, blk] + [pl.BlockSpec(memory_space=pl.ANY)] * 4, out_specs=[blk] * 4,
                 out_shape=[jax.ShapeDtypeStruct(w.shape, F32)] * 4, input_output_aliases={4 + k: k for k in range(4)},
                 compiler_params=_params(("parallel",)))(parts, w, m, v, *outs)


def _assemble_cols(blocks, *, name):
    nb, r, c = blocks.shape
    width = -(-nb * c // LANES) * LANES
    tr = _pick_rows(r, 256)

    def body(b_ref, o_ref):
        pieces = [b_ref[j] for j in range(nb)]
        if width > nb * c:
            pieces.append(jnp.zeros((tr, width - nb * c), blocks.dtype))
        o_ref[...] = jnp.concatenate(pieces, axis=1)

    return _call(body, name=name, grid=(r // tr,), in_specs=[pl.BlockSpec((nb, tr, c), lambda i: (0, i, 0))],
                 out_specs=pl.BlockSpec((tr, width), lambda i: (i, 0)), out_shape=jax.ShapeDtypeStruct((r, width), blocks.dtype),
                 compiler_params=_params(("parallel",)))(blocks)


def _split_cols(pieces, c, *, name):
    r = pieces[0].shape[0]
    tr = _pick_rows(r, 256)
    n_in = len(pieces)

    def body(*refs):
        o_ref = refs[n_in]
        x = jnp.concatenate([p[...] for p in refs[:n_in]], axis=1) if n_in > 1 else refs[0][...]
        for j in range(N_DEV):
            o_ref[j] = x[:, c * j:c * (j + 1)]

    return _call(body, name=name, grid=(r // tr,),
                 in_specs=[pl.BlockSpec((tr, p.shape[1]), lambda i: (i, 0)) for p in pieces],
                 out_specs=pl.BlockSpec((N_DEV, tr, c), lambda i: (0, i, 0)),
                 out_shape=jax.ShapeDtypeStruct((N_DEV, r, c), pieces[0].dtype),
                 compiler_params=_params(("parallel",)))(*pieces)


def _sum_parts(parts, *, name):
    n, r, c = parts.shape
    tr = _pick_rows(r, 256)

    def body(p_ref, g_ref):
        g = p_ref[0].astype(F32)
        for k in range(1, n):
            g = g + p_ref[k].astype(F32)
        g_ref[...] = g

    return _call(body, name=name, grid=(r // tr,), in_specs=[pl.BlockSpec((n, tr, c), lambda i: (0, i, 0))],
                 out_specs=pl.BlockSpec((tr, c), lambda i: (i, 0)), out_shape=jax.ShapeDtypeStruct((r, c), F32),
                 compiler_params=_params(("parallel",)))(parts)


def _peers():
    x, y, c = lax.axis_index("x"), lax.axis_index("y"), lax.axis_index("c")
    me = 4 * x + 2 * y + c
    out = []
    for k in range(1, N_DEV):
        px = (1 - x) if (k >> 2) & 1 else x
        py = (1 - y) if (k >> 1) & 1 else y
        pc = (1 - c) if k & 1 else c
        out.append(((px, py, pc), 4 * px + 2 * py + pc))
    return me, out


_HBM = pl.BlockSpec(memory_space=pltpu.HBM)
_SEM = pl.BlockSpec(memory_space=pltpu.SEMAPHORE)
_EFFECT = pltpu.SideEffectType.DATAFLOW_SIDE_EFFECTING


ALL_PEERS = tuple(range(1, N_DEV))
SAME_CORE_PEERS = (2, 4, 6)


def _split_copies(s_refs, l_refs, send_sems, recv_sems, gather, incoming, ks=ALL_PEERS):
    me, peers = _peers()
    local, remote = [], []
    for ti, (s_ref, l_ref) in enumerate(zip(s_refs, l_refs)):
        base = ti * N_DEV
        local.append(pltpu.make_async_copy(s_ref if gather else s_ref.at[me], l_ref.at[me], recv_sems.at[base + N_DEV - 1]))
        for k, (dev, pid) in enumerate(peers):
            if k + 1 not in ks:
                continue
            sems = dict(send_sem=send_sems.at[base + k], recv_sem=recv_sems.at[base + k], device_id=dev, device_id_type=MESH)
            src = s_ref if gather else s_ref.at[pid]
            remote.append((
                pltpu.make_async_remote_copy(src_ref=src, dst_ref=l_ref.at[me], **sems),
                pltpu.make_async_remote_copy(src_ref=src, dst_ref=l_ref.at[pid], **sems) if incoming else None))
    return local, remote


def _exchange_start(srcs, *, gather, name, after=(), ks=ALL_PEERS):
    n = len(srcs)
    after = list(after)
    srcs = [pltpu.with_memory_space_constraint(s, pltpu.HBM) for s in srcs]
    lands = [pltpu.with_memory_space_constraint(
        lax.empty((N_DEV,) + tuple(s.shape if gather else s.shape[1:]), s.dtype), pltpu.HBM) for s in srcs]

    def body(*refs):
        s_refs, l_refs = refs[:n], refs[n:2 * n]
        outs = refs[2 * n + len(after):]
        send_sems, recv_sems, token = outs[0], outs[1], outs[-1]
        local, remote = _split_copies(s_refs, l_refs, send_sems, recv_sems, gather, incoming=False, ks=ks)
        for cp in local:
            cp.start()
        for out_cp, _ in remote:
            out_cp.start()
        token[...] = jnp.zeros_like(token)

    outs = _call(
        body, name=name,
        out_shape=(pltpu.SemaphoreType.DMA((n * N_DEV,)), pltpu.SemaphoreType.DMA((n * N_DEV,)),
                   *[pltpu.HBM(s.shape, s.dtype) for s in srcs], *[pltpu.HBM(l.shape, l.dtype) for l in lands],
                   jax.ShapeDtypeStruct((SUBLANES, LANES), F32)),
        in_specs=[_HBM] * (2 * n) + [pl.BlockSpec(memory_space=pl.ANY)] * len(after),
        out_specs=(_SEM, _SEM, *[_HBM] * (2 * n), pl.BlockSpec(memory_space=pltpu.VMEM)),
        input_output_aliases={k: k + 2 for k in range(2 * n)},
        compiler_params=pltpu.CompilerParams(has_side_effects=_EFFECT),
    )(*srcs, *lands, *after)
    return dict(n=n, gather=gather, ks=ks, sems=outs[:2], srcs=outs[2:2 + n], lands=outs[2 + n:2 + 2 * n]), outs[-1]


def _exchange_wait(state, after, *, name):
    n, gather, ks = state["n"], state["gather"], state["ks"]
    after = list(after)

    def body(*refs):
        s_refs, l_refs = refs[:n], refs[n:2 * n]
        send_sems, recv_sems = refs[2 * n], refs[2 * n + 1]
        local, remote = _split_copies(s_refs, l_refs, send_sems, recv_sems, gather, incoming=True, ks=ks)
        for out_cp, in_cp in remote:
            out_cp.wait_send()
            in_cp.wait_recv()
        for cp in local:
            cp.wait()

    outs = _call(
        body, name=name,
        out_shape=tuple(pltpu.HBM(a.shape, a.dtype) for a in (*state["srcs"], *state["lands"])),
        in_specs=[_HBM] * (2 * n) + [_SEM, _SEM] + [pl.BlockSpec(memory_space=pl.ANY)] * len(after),
        out_specs=tuple([_HBM] * (2 * n)),
        input_output_aliases={k: k for k in range(2 * n)},
        compiler_params=pltpu.CompilerParams(has_side_effects=_EFFECT),
    )(*state["srcs"], *state["lands"], *state["sems"], *after)
    return outs[n:]


def _sibling_copies(l_refs, send_sems, recv_sems, incoming):
    x, y, c = lax.axis_index("x"), lax.axis_index("y"), lax.axis_index("c")
    out = []
    for ti, l_ref in enumerate(l_refs):
        for q in range(4):
            px = (1 - x) if q & 2 else x
            py = (1 - y) if q & 1 else y
            mine, theirs = 4 * px + 2 * py + c, 4 * px + 2 * py + (1 - c)
            sems = dict(send_sem=send_sems.at[4 * ti + q], recv_sem=recv_sems.at[4 * ti + q],
                        device_id=(x, y, 1 - c), device_id_type=MESH)
            out.append((
                pltpu.make_async_remote_copy(src_ref=l_ref.at[mine], dst_ref=l_ref.at[mine], **sems),
                pltpu.make_async_remote_copy(src_ref=l_ref.at[mine], dst_ref=l_ref.at[theirs], **sems) if incoming else None))
    return out


def _sibling_start(lands, *, name, after=()):
    n = len(lands)
    after = list(after)
    lands = [pltpu.with_memory_space_constraint(l, pltpu.HBM) for l in lands]

    def body(*refs):
        l_refs = refs[:n]
        outs = refs[n + len(after):]
        for out_cp, _ in _sibling_copies(l_refs, outs[0], outs[1], incoming=False):
            out_cp.start()
        outs[-1][...] = jnp.zeros_like(outs[-1])

    outs = _call(
        body, name=name,
        out_shape=(pltpu.SemaphoreType.DMA((4 * n,)), pltpu.SemaphoreType.DMA((4 * n,)),
                   *[pltpu.HBM(l.shape, l.dtype) for l in lands], jax.ShapeDtypeStruct((SUBLANES, LANES), F32)),
        in_specs=[_HBM] * n + [pl.BlockSpec(memory_space=pl.ANY)] * len(after),
        out_specs=(_SEM, _SEM, *[_HBM] * n, pl.BlockSpec(memory_space=pltpu.VMEM)),
        input_output_aliases={k: k + 2 for k in range(n)},
        compiler_params=pltpu.CompilerParams(has_side_effects=_EFFECT),
    )(*lands, *after)
    return dict(n=n, sems=outs[:2], lands=outs[2:2 + n]), outs[-1]


def _sibling_wait(state, after, *, name):
    n = state["n"]
    after = list(after)

    def body(*refs):
        l_refs = refs[:n]
        for out_cp, in_cp in _sibling_copies(l_refs, refs[n], refs[n + 1], incoming=True):
            out_cp.wait_send()
            in_cp.wait_recv()

    return _call(
        body, name=name,
        out_shape=tuple(pltpu.HBM(a.shape, a.dtype) for a in state["lands"]),
        in_specs=[_HBM] * n + [_SEM, _SEM] + [pl.BlockSpec(memory_space=pl.ANY)] * len(after),
        out_specs=tuple([_HBM] * n), input_output_aliases={k: k for k in range(n)},
        compiler_params=pltpu.CompilerParams(has_side_effects=_EFFECT),
    )(*state["lands"], *state["sems"], *after)


def _pack(arrs):
    flat = jnp.concatenate([a.reshape(-1).astype(F32) for a in arrs])
    pad = (-flat.shape[0]) % (SUBLANES * LANES)
    return jnp.pad(flat, (0, pad)).reshape(-1, LANES)


def _unpack(packed, shapes):
    flat = packed.reshape(-1)
    out, off = [], 0
    for s in shapes:
        n = 1
        for v in s:
            n *= v
        out.append(flat[off:off + n].reshape(s))
        off += n
    return out


SMALL = ("norm_mix_pre", "ssm_conv_b", "dt_bias", "a_log", "d_skip", "conv_out_norm", "ssm_out_norm",
         "norm_mix_post", "norm_mlp_pre", "norm_mlp_post", "conv_a_w", "ssm_conv_w")
BIG = ("w_in", "w_out", "w_up", "w_down")
ORDER = ("norm_mix_pre", "w_in", "conv_a_w", "ssm_conv_w", "ssm_conv_b", "dt_bias", "a_log", "d_skip",
         "conv_out_norm", "ssm_out_norm", "w_out", "norm_mix_post", "norm_mlp_pre", "w_up", "w_down", "norm_mlp_post")


def kernel(x, norm_mix_pre, w_in, conv_a_w, ssm_conv_w, ssm_conv_b, dt_bias, a_log, d_skip, conv_out_norm, ssm_out_norm, w_out, norm_mix_post, norm_mlp_pre, w_up, w_down, norm_mlp_post, loss_target, m_norm_mix_pre, m_w_in, m_conv_a_w, m_ssm_conv_w, m_ssm_conv_b, m_dt_bias, m_a_log, m_d_skip, m_conv_out_norm, m_ssm_out_norm, m_w_out, m_norm_mix_post, m_norm_mlp_pre, m_w_up, m_w_down, m_norm_mlp_post, v_norm_mix_pre, v_w_in, v_conv_a_w, v_ssm_conv_w, v_ssm_conv_b, v_dt_bias, v_a_log, v_d_skip, v_conv_out_norm, v_ssm_out_norm, v_w_out, v_norm_mix_post, v_norm_mlp_pre, v_w_up, v_w_down, v_norm_mlp_post):
    W = dict(norm_mix_pre=norm_mix_pre, w_in=w_in, conv_a_w=conv_a_w, ssm_conv_w=ssm_conv_w, ssm_conv_b=ssm_conv_b,
             dt_bias=dt_bias, a_log=a_log, d_skip=d_skip, conv_out_norm=conv_out_norm, ssm_out_norm=ssm_out_norm,
             w_out=w_out, norm_mix_post=norm_mix_post, norm_mlp_pre=norm_mlp_pre, w_up=w_up, w_down=w_down,
             norm_mlp_post=norm_mlp_post)
    M = dict(norm_mix_pre=m_norm_mix_pre, w_in=m_w_in, conv_a_w=m_conv_a_w, ssm_conv_w=m_ssm_conv_w,
             ssm_conv_b=m_ssm_conv_b, dt_bias=m_dt_bias, a_log=m_a_log, d_skip=m_d_skip,
             conv_out_norm=m_conv_out_norm, ssm_out_norm=m_ssm_out_norm, w_out=m_w_out,
             norm_mix_post=m_norm_mix_post, norm_mlp_pre=m_norm_mlp_pre, w_up=m_w_up, w_down=m_w_down,
             norm_mlp_post=m_norm_mlp_post)
    V = dict(norm_mix_pre=v_norm_mix_pre, w_in=v_w_in, conv_a_w=v_conv_a_w, ssm_conv_w=v_ssm_conv_w,
             ssm_conv_b=v_ssm_conv_b, dt_bias=v_dt_bias, a_log=v_a_log, d_skip=v_d_skip,
             conv_out_norm=v_conv_out_norm, ssm_out_norm=v_ssm_out_norm, w_out=v_w_out,
             norm_mix_post=v_norm_mix_post, norm_mlp_pre=v_norm_mlp_pre, w_up=v_w_up, w_down=v_w_down,
             norm_mlp_post=v_norm_mlp_post)

    nseq, seq, d = x.shape
    t = nseq * seq
    depth = w_in.shape[0]
    h = d // HEAD_DIM
    xbc = d + 2 * SSM_GROUPS * D_STATE
    in_cols = w_in.shape[2] * N_DEV
    d_mix = w_out.shape[1] * N_DEV
    d_ff = w_up.shape[2] * N_DEV
    me = 4 * lax.axis_index("x") + 2 * lax.axis_index("y") + lax.axis_index("c")
    ca_shard = conv_a_w.shape[2]
    sc_shard = ssm_conv_w.shape[2]

    tap_shapes = [conv_a_w.shape[1:], ssm_conv_w.shape[1:]]

    def gather_start(i, after=()):
        ks = SAME_CORE_PEERS
        st_in, tok_in = _exchange_start([w_in[i].astype(BF16), _pack([conv_a_w[i], ssm_conv_w[i]])], gather=True,
                                        name=f"gather_start_in_{i}", after=after, ks=ks)
        st_rest, tok_rest = _exchange_start([W[n][i].astype(BF16) for n in ("w_out", "w_up", "w_down")], gather=True,
                                            name=f"gather_start_rest_{i}", after=[tok_in], ks=ks)
        return st_in, st_rest, tok_rest

    vec = lambda name, i: W[name][i].reshape(1, -1)
    emat = (lax.broadcasted_iota(jnp.int32, (h, d), 1) // HEAD_DIM == lax.broadcasted_iota(jnp.int32, (h, d), 0)).astype(F32)

    xcur = x.reshape(t, d)
    hcur = _norm_fwd(xcur, vec("norm_mix_pre", 0), name="norm_first")
    saved = []
    nxt = gather_start(0)
    sib_in = None
    for i in range(depth):
        st_in, st_rest, tok = nxt
        if sib_in is None:
            sib_in, _ = _sibling_start(_exchange_wait(st_in, [hcur, tok], name=f"gather_wait_in_{i}"),
                                       name=f"gather_sibling_start_in_{i}")
        win_g, taps_g = _sibling_wait(sib_in, [hcur], name=f"gather_sibling_wait_in_{i}")
        win = _assemble_cols(win_g, name=f"assemble_w_in_{i}")
        taps_j = [_unpack(taps_g[j], tap_shapes) for j in range(N_DEV)]
        conv_a_i = jnp.concatenate([tj[0] for tj in taps_j], axis=1)
        ssm_conv_i = jnp.concatenate([tj[1] for tj in taps_j], axis=1)
        proj = _mm(hcur, win, n=4 * d + xbc, name=f"fwd_proj_{i}", out_dtypes=(BF16,))
        pdt = _mm(hcur, win, n=LANES, b_off=4 * d + xbc, name=f"fwd_proj_dt_{i}")
        ya, va = _conva_fwd(proj, conv_a_i, vec("conv_out_norm", i), d=d, seq=seq, name=f"fwd_conv_a_{i}")
        cpre = _convb_fwd(proj, ssm_conv_i, vec("ssm_conv_b", i), col0=4 * d, seq=seq, name=f"fwd_conv_b_{i}")
        dsk_lane = jnp.repeat(W["d_skip"][i], HEAD_DIM).reshape(1, d)
        st_sib, tok_sib = _sibling_start(_exchange_wait(st_rest, [cpre], name=f"gather_wait_rest_{i}"),
                                         name=f"gather_sibling_start_rest_{i}")
        cat, y2, hprev = _ssd_fwd(cpre, pdt, proj, ya, vec("dt_bias", i) + tok_sib[0:1, 0:1], vec("a_log", i), dsk_lane,
                                  vec("ssm_out_norm", i), emat, nseq=nseq, seq=seq, name=f"fwd_ssd_{i}")
        wout_g, wup_g, wdown_g = _sibling_wait(st_sib, [cat], name=f"gather_sibling_wait_rest_{i}")
        lw = dict(win=win, wout=wout_g.reshape(d_mix, d),
                  wup=_assemble_cols(wup_g, name=f"assemble_w_up_{i}"), wdown=wdown_g.reshape(d_ff, d),
                  conv_a=conv_a_i, ssm_conv=ssm_conv_i)
        after = []
        if i + 1 < depth:
            nxt = gather_start(i + 1, after=[wout_g])
            after = [nxt[2]]
        x1, h2, mix = _mm(cat, lw["wout"], name=f"fwd_out_{i}", after=after, out_dtypes=(F32, BF16, BF16),
                          epi=_epi_resid_norm, extras=(xcur,), vecs=(vec("norm_mix_post", i), vec("norm_mlp_pre", i)),
                          tm_cap=FUSED_ROWS)
        f = _mm(h2, lw["wup"], name=f"fwd_up_{i}", out_dtypes=(BF16,), epi=_epi_relu2)
        g_next = vec("norm_mix_pre", i + 1) if i + 1 < depth else None
        after = []
        if i + 1 < depth:
            sib_in, tok_in = _sibling_start(_exchange_wait(nxt[0], [f], name=f"gather_wait_in_{i + 1}"),
                                            name=f"gather_sibling_start_in_{i + 1}")
            after = [tok_in]
        dn = _mm(f, lw["wdown"], name=f"fwd_down_{i}", out_dtypes=(BF16,), after=after)
        x2, hnext = _resid_norm(x1, dn, vec("norm_mlp_post", i), g_next, name=f"fwd_post_mlp_{i}")
        saved.append(dict(lw=lw, x0=xcur, h=hcur, proj=proj, pdt=pdt, va=va, cpre=cpre, y2=y2,
                          hprev=hprev, cat=cat, mix=mix, x1=x1, h2=h2, f=f, dn=dn, dsk_lane=dsk_lane))
        xcur, hcur = x2, hnext

    dx, loss_part = _loss_fwd_bwd(xcur, loss_target.reshape(t, d), name="loss")
    loss = lax.psum(loss_part[0, 0], ("x", "y", "c"))

    small_grads = {n: [None] * depth for n in SMALL}
    big_out = {n: None for n in BIG}

    def finish(pending, after):
        li, st_a, st_b = pending

        def update(n, parts):
            big_out[n] = _sum_adamw(parts, W[n], M[n], V[n], layer=li, outs=big_out[n], name=f"adamw_{n}_{li}")

        p_down, p_up, p_out = _exchange_wait(st_a, after, name=f"scatter_wait_a_{li}")
        update("w_down", p_down)
        update("w_up", p_up)
        update("w_out", p_out)
        p_in, = _exchange_wait(st_b, after + [big_out["w_out"][0]], name=f"scatter_wait_b_{li}")
        update("w_in", p_in)

    pending = None
    for i in reversed(range(depth)):
        s = saved[i]
        lw = s["lw"]
        ddn, dg = _bwd_norm_out(s["dn"], vec("norm_mlp_post", i), dx, name=f"bwd_norm_mlp_post_{i}")
        small_grads["norm_mlp_post"][i] = dg
        dup = _mm(ddn, lw["wdown"], tb=True, name=f"bwd_down_dx_{i}", out_dtypes=(BF16,), epi=_epi_drelu2,
                  extras=(s["f"],))
        g_wdown = _mm(s["f"], ddn, ta=True, name=f"bwd_down_dw_{i}", out_dtypes=(BF16,))
        dh2 = _mm(dup, lw["wup"], tb=True, name=f"bwd_up_dx_{i}", out_dtypes=(BF16,))
        g_wup = _mm(s["h2"], dup, ta=True, name=f"bwd_up_dw_{i}", out_dtypes=(BF16,))
        dx1, dmix, dg_pre, dg_post = _bwd_norm_pair(s["x1"], [dh2], dx, s["mix"], vec("norm_mlp_pre", i),
                                                    vec("norm_mix_post", i), name=f"bwd_norm_mix_post_{i}")
        small_grads["norm_mlp_pre"][i] = dg_pre
        small_grads["norm_mix_post"][i] = dg_post
        dcat = _mm(dmix, lw["wout"], tb=True, name=f"bwd_out_dx_{i}", out_dtypes=(BF16,))
        g_wout = _mm(s["cat"], dmix, ta=True, name=f"bwd_out_dw_{i}", out_dtypes=(BF16,))
        st_a, tok_a = _exchange_start(
            [g_wdown.reshape(N_DEV, d_ff // N_DEV, d), _split_cols([g_wup], d_ff // N_DEV, name=f"split_g_w_up_{i}"),
             g_wout.reshape(N_DEV, d_mix // N_DEV, d)], gather=False, name=f"scatter_start_a_{i}")
        dproj, dcaw, dgca = _conva_bwd(dcat, s["proj"], s["va"], lw["conv_a"],
                                       vec("conv_out_norm", i) + tok_a[0:1, 0:1], d=d, seq=seq, name=f"bwd_conv_a_{i}")
        small_grads["conv_a_w"][i] = dcaw
        small_grads["conv_out_norm"][i] = dgca
        dconv, dproj, dpdt, dgs, ddsk, ddtb, dalog = _ssd_bwd(
            s["cpre"], s["pdt"], s["proj"], s["y2"], s["hprev"], dcat, vec("dt_bias", i), vec("a_log", i),
            s["dsk_lane"], vec("ssm_out_norm", i), emat, dproj, nseq=nseq, seq=seq, name=f"bwd_ssd_{i}")
        small_grads["ssm_out_norm"][i] = dgs
        small_grads["d_skip"][i] = ddsk
        small_grads["dt_bias"][i] = ddtb
        small_grads["a_log"][i] = dalog
        dproj, dscw, dscb = _convb_bwd(dconv, s["proj"], lw["ssm_conv"], dproj, col0=4 * d, seq=seq,
                                       name=f"bwd_conv_b_{i}")
        small_grads["ssm_conv_w"][i] = dscw
        small_grads["ssm_conv_b"][i] = dscb
        g_win = _split_cols([
            _mm(s["h"], dproj, ta=True, name=f"bwd_proj_dw_{i}", out_dtypes=(BF16,)),
            _mm(s["h"], dpdt, ta=True, name=f"bwd_proj_dt_dw_{i}", out_dtypes=(BF16,))],
            in_cols // N_DEV, name=f"split_g_w_in_{i}")
        st_b, tok_b = _exchange_start([g_win], gather=False, name=f"scatter_start_b_{i}")
        dh_parts = [_mm(dp, lw["win"], tb=True, b_koff=off, name=f"bwd_proj_{nm}dx_{i}", after=[tok_b], out_dtypes=(BF16,))
                    for nm, dp, off in (("", dproj, 0), ("dt_", dpdt, 4 * d + xbc))]
        dx, dg_in = _bwd_norm_in(s["x0"], dh_parts, dx1, vec("norm_mix_pre", i), name=f"bwd_norm_mix_pre_{i}")
        small_grads["norm_mix_pre"][i] = dg_in
        if pending is not None:
            finish(pending, [dx])
        pending = (i, st_a, st_b)

    grad_x = dx.reshape(nseq, seq, d)

    small_shapes_full = {n: (depth,) + tuple(small_grads[n][0].shape) for n in SMALL}
    gpack = _pack([jnp.stack(small_grads[n]) for n in SMALL])
    st_small, tok_small = _exchange_start([gpack], gather=True, name="allreduce_small_start")
    finish(pending, [dx, tok_small])
    gparts, = _exchange_wait(st_small, [big_out["w_in"][0]], name="allreduce_small_wait")

    def shard_of(n, full):
        if n == "conv_a_w":
            return lax.dynamic_slice_in_dim(full, me * ca_shard, ca_shard, axis=2)
        if n == "ssm_conv_w":
            return lax.dynamic_slice_in_dim(full, me * sc_shard, sc_shard, axis=2)
        return full.reshape(W[n].shape)

    gsum = _sum_parts(gparts, name="sum_small")
    gfull = _unpack(gsum, [small_shapes_full[n] for n in SMALL])
    gsmall = {n: shard_of(n, gf) for n, gf in zip(SMALL, gfull)}
    res = _sum_adamw(_pack([gsmall[n] for n in SMALL])[None], _pack([W[n] for n in SMALL]),
                     _pack([M[n] for n in SMALL]), _pack([V[n] for n in SMALL]), name="adamw_small")
    small_out = [dict(zip(SMALL, _unpack(r, [W[n].shape for n in SMALL]))) for r in res]

    def out_of(kind, n):
        return big_out[n][kind] if n in BIG else small_out[kind][n]

    return (loss, grad_x, *[out_of(k, n) for k in range(4) for n in ORDER])
```

```python
import jax
import jax.numpy as jnp
from jax import lax
from jax.experimental import pallas as pl
from jax.experimental.pallas import tpu as pltpu

F32 = jnp.float32
BF16 = jnp.bfloat16
HIGHEST = lax.Precision.HIGHEST
MESH = pl.DeviceIdType.MESH

EPS = 1e-6
HEAD_DIM = 64
D_STATE = 128
SSM_GROUPS = 2
CHUNK = 128
CONV_K = 3
SSM_CONV_K = 4
ADAM_LR = 0.001
ADAM_B1 = 0.9
ADAM_B2 = 0.999
ADAM_EPS = 1e-08
ADAM_WD = 0.01
ADAM_STEP = 10

N_DEV = 8
LANES = 128
SUBLANES = 8
VMEM_LIMIT = 48 * 1024 * 1024
ROW_TILE = 512
MM_TILE = 1024
MM_TILE_N = 2816
MM_VMEM_BUDGET = 40 * 1024 * 1024
FUSED_ROWS = 512


def _params(sem):
    return pltpu.CompilerParams(dimension_semantics=sem, vmem_limit_bytes=VMEM_LIMIT)


def _call(body, **kw):
    return pl.pallas_call(body, **kw)


def _pick(n, cap):
    best = None
    for t in range(LANES, min(n, cap) + 1, LANES):
        if n % t == 0:
            best = t
    return best or n


def _pick_rows(n, cap):
    best = None
    for t in range(SUBLANES, min(n, cap) + 1, SUBLANES):
        if n % t == 0:
            best = t
    return best or n


def _sigmoid(x):
    return 1.0 / (1.0 + jnp.exp(-x))


def _softplus(x):
    return jnp.maximum(x, 0.0) + jnp.log1p(jnp.exp(-jnp.abs(x)))


def _rms(x):
    return lax.rsqrt(jnp.mean(x * x, axis=-1, keepdims=True) + EPS)


def _rms_bwd(x, r, g, dy):
    gy = dy * g
    dx = r * gy - x * (r * r * r) * jnp.mean(gy * x, axis=-1, keepdims=True)
    return dx, dy * x * r


def _full(shape):
    return pl.BlockSpec(shape, lambda *_: (0,) * len(shape))


def _mm(a, b, *, name, ta=False, tb=False, out_dtypes=(F32,), epi=None, extras=(), n=None, b_off=0, b_koff=0,
        after=(), vecs=(), tm_cap=MM_TILE):
    m, k = (a.shape[1], a.shape[0]) if ta else a.shape
    if n is None:
        n = b.shape[0] if tb else b.shape[1]
    tm, tn, tk = _pick(m, tm_cap), _pick(n, MM_TILE_N), _pick(k, MM_TILE)
    while b_off % tn or n % tn:
        tn -= LANES
    if b_koff == 0 and k > MM_TILE:
        tk = _pick(k, MM_TILE_N)
    while b_koff % tk or k % tk:
        tk -= LANES

    def vmem_bytes(tk_):
        per_out = sum(jnp.dtype(dt).itemsize for dt in out_dtypes) + sum(e.dtype.itemsize for e in extras)
        return 2 * tk_ * (tm * a.dtype.itemsize + tn * b.dtype.itemsize) + tm * tn * (2 * per_out + 4)

    while vmem_bytes(tk) > MM_VMEM_BUDGET and tk % (2 * LANES) == 0 and not b_koff % (tk // 2):
        tk //= 2
    nk = k // tk
    nm, nn = m // tm, n // tn
    jo = b_off // tn
    ko = b_koff // tk
    a_bytes = m * k * a.dtype.itemsize
    b_bytes = n * k * b.dtype.itemsize
    m_outer = a_bytes + nm * b_bytes <= b_bytes + nn * a_bytes
    ij = (lambda g0, g1: (g0, g1)) if m_outer else (lambda g0, g1: (g1, g0))
    grid = (nm, nn, nk) if m_outer else (nn, nm, nk)

    def a_map(g0, g1, kk):
        i, _ = ij(g0, g1)
        return (kk, i) if ta else (i, kk)

    def b_map(g0, g1, kk):
        _, j = ij(g0, g1)
        return (j + jo, kk + ko) if tb else (kk + ko, j + jo)

    def o_map(g0, g1, kk):
        return ij(g0, g1)

    a_spec = pl.BlockSpec((tk, tm) if ta else (tm, tk), a_map)
    b_spec = pl.BlockSpec((tn, tk) if tb else (tk, tn), b_map)
    o_spec = pl.BlockSpec((tm, tn), o_map)
    dims = (((0 if ta else 1,), (1 if tb else 0,)), ((), ()))
    n_ex = len(extras) + len(vecs)
    after = list(after)
    o0 = 2 + n_ex + len(after)

    def finish(acc, ex, outs):
        res = (acc,) if epi is None else epi(acc, *[e[...] for e in ex])
        for o, r in zip(outs, res):
            o[...] = r.astype(o.dtype)

    def body_single(*refs):
        a_ref, b_ref = refs[:2]
        acc = lax.dot_general(a_ref[...].astype(BF16), b_ref[...].astype(BF16), dims, preferred_element_type=F32)
        finish(acc, refs[2:2 + n_ex], refs[o0:])

    def body_multi(*refs):
        a_ref, b_ref = refs[:2]
        acc = refs[-1]
        kk = pl.program_id(2)

        @pl.when(kk == 0)
        def _():
            acc[...] = jnp.zeros_like(acc)

        acc[...] += lax.dot_general(a_ref[...].astype(BF16), b_ref[...].astype(BF16), dims, preferred_element_type=F32)

        @pl.when(kk == nk - 1)
        def _():
            finish(acc[...], refs[2:2 + n_ex], refs[o0:-1])

    v_spec = pl.BlockSpec((1, tn), lambda g0, g1, kk: (0, ij(g0, g1)[1]))
    outs = _call(
        body_single if nk == 1 else body_multi, name=name, grid=grid,
        in_specs=([a_spec, b_spec] + [o_spec] * len(extras) + [v_spec] * len(vecs)
                  + [pl.BlockSpec(memory_space=pl.ANY)] * len(after)),
        out_specs=[o_spec] * len(out_dtypes),
        out_shape=[jax.ShapeDtypeStruct((m, n), dt) for dt in out_dtypes],
        scratch_shapes=[] if nk == 1 else [pltpu.VMEM((tm, tn), F32)],
        compiler_params=_params(("parallel", "parallel", "arbitrary")),
    )(a, b, *extras, *vecs, *after)
    return outs[0] if len(outs) == 1 else outs


def _epi_resid_norm(acc, x, g_res, g_next):
    xn = x + acc * _rms(acc) * g_res
    return xn, xn * _rms(xn) * g_next, acc


def _epi_relu2(acc):
    r = jnp.maximum(acc, 0.0)
    return (r * r,)


def _epi_drelu2(acc, f):
    return (acc * (2.0 * jnp.sqrt(f).astype(F32)),)


def _norm_fwd(x, g, *, name):
    t, d = x.shape
    tt = _pick_rows(t, ROW_TILE)

    def body(x_ref, g_ref, h_ref):
        xv = x_ref[...]
        h_ref[...] = (xv * _rms(xv) * g_ref[...]).astype(BF16)

    row = pl.BlockSpec((tt, d), lambda i: (i, 0))
    return _call(body, name=name, grid=(t // tt,), in_specs=[row, _full((1, d))], out_specs=row,
                 out_shape=jax.ShapeDtypeStruct((t, d), BF16), compiler_params=_params(("parallel",)))(x, g)


def _resid_norm(x, n, g1, g2, *, name):
    t, d = x.shape
    tt = _pick_rows(t, ROW_TILE)
    gains = [g1] if g2 is None else [g1, g2]

    def body(x_ref, n_ref, *refs):
        nv = n_ref[...].astype(F32)
        xn = x_ref[...] + nv * _rms(nv) * refs[0][...]
        refs[len(gains)][...] = xn
        if g2 is not None:
            refs[3][...] = (xn * _rms(xn) * refs[1][...]).astype(BF16)

    row = pl.BlockSpec((tt, d), lambda i: (i, 0))
    outs = _call(body, name=name, grid=(t // tt,), in_specs=[row, row] + [_full((1, d))] * len(gains),
                 out_specs=[row] * len(gains),
                 out_shape=[jax.ShapeDtypeStruct((t, d), F32), jax.ShapeDtypeStruct((t, d), BF16)][:len(gains)],
                 compiler_params=_params(("parallel",)))(x, n, *gains)
    return (outs[0], None) if g2 is None else outs


def _loss_fwd_bwd(xf, target, *, name):
    t, d = xf.shape
    tt = _pick_rows(t, ROW_TILE)
    nt = t // tt

    def body(x_ref, t_ref, dy_ref, loss_ref, acc):
        i = pl.program_id(0)

        @pl.when(i == 0)
        def _():
            acc[...] = jnp.zeros_like(acc)

        e = x_ref[...] - t_ref[...]
        dy_ref[...] = e * (1.0 / d)
        acc[...] += jnp.sum(e * e, axis=0, keepdims=True)

        @pl.when(i == nt - 1)
        def _():
            loss_ref[...] = jnp.sum(acc[...], axis=-1, keepdims=True) * (0.5 / d)

    row = pl.BlockSpec((tt, d), lambda i: (i, 0))
    return _call(body, name=name, grid=(nt,), in_specs=[row, row], out_specs=[row, _full((1, 1))],
                 out_shape=[jax.ShapeDtypeStruct((t, d), F32), jax.ShapeDtypeStruct((1, 1), F32)],
                 scratch_shapes=[pltpu.VMEM((1, d), F32)], compiler_params=_params(("arbitrary",)))(xf, target)


def _bwd_norm_pair(xin, dh, dres, n, g_in, g_out, *, name):
    t, d = xin.shape
    tt = _pick_rows(t, ROW_TILE)
    n_dh = len(dh)

    def body(*refs):
        x_ref = refs[0]
        dh_refs = refs[1:1 + n_dh]
        dres_ref, n_ref, gi_ref, go_ref, dx_ref, dn_ref, dgi_ref, dgo_ref = refs[1 + n_dh:]
        i = pl.program_id(0)

        @pl.when(i == 0)
        def _():
            dgi_ref[...] = jnp.zeros_like(dgi_ref)
            dgo_ref[...] = jnp.zeros_like(dgo_ref)

        xv = x_ref[...]
        dhv = dh_refs[0][...].astype(F32)
        for r in dh_refs[1:]:
            dhv = dhv + r[...].astype(F32)
        dxh, dgi = _rms_bwd(xv, _rms(xv), gi_ref[...], dhv)
        dx = dres_ref[...] + dxh
        dx_ref[...] = dx
        dgi_ref[...] += jnp.sum(dgi, axis=0, keepdims=True)
        nv = n_ref[...].astype(F32)
        dn, dgo = _rms_bwd(nv, _rms(nv), go_ref[...], dx)
        dn_ref[...] = dn.astype(BF16)
        dgo_ref[...] += jnp.sum(dgo, axis=0, keepdims=True)

    row = pl.BlockSpec((tt, d), lambda i: (i, 0))
    vec = _full((1, d))
    return _call(body, name=name, grid=(t // tt,), in_specs=[row] * (n_dh + 3) + [vec, vec],
                 out_specs=[row, row, vec, vec],
                 out_shape=[jax.ShapeDtypeStruct((t, d), F32), jax.ShapeDtypeStruct((t, d), BF16),
                            jax.ShapeDtypeStruct((1, d), F32), jax.ShapeDtypeStruct((1, d), F32)],
                 compiler_params=_params(("arbitrary",)))(xin, *dh, dres, n, g_in, g_out)


def _bwd_norm_in(xin, dh, dres, g_in, *, name):
    t, d = xin.shape
    tt = _pick_rows(t, ROW_TILE)
    n_dh = len(dh)

    def body(*refs):
        x_ref = refs[0]
        dh_refs = refs[1:1 + n_dh]
        dres_ref, gi_ref, dx_ref, dgi_ref = refs[1 + n_dh:]
        i = pl.program_id(0)

        @pl.when(i == 0)
        def _():
            dgi_ref[...] = jnp.zeros_like(dgi_ref)

        xv = x_ref[...]
        dhv = dh_refs[0][...].astype(F32)
        for r in dh_refs[1:]:
            dhv = dhv + r[...].astype(F32)
        dxh, dgi = _rms_bwd(xv, _rms(xv), gi_ref[...], dhv)
        dx_ref[...] = dres_ref[...] + dxh
        dgi_ref[...] += jnp.sum(dgi, axis=0, keepdims=True)

    row = pl.BlockSpec((tt, d), lambda i: (i, 0))
    vec = _full((1, d))
    return _call(body, name=name, grid=(t // tt,), in_specs=[row] * (n_dh + 2) + [vec],
                 out_specs=[row, vec],
                 out_shape=[jax.ShapeDtypeStruct((t, d), F32), jax.ShapeDtypeStruct((1, d), F32)],
                 compiler_params=_params(("arbitrary",)))(xin, *dh, dres, g_in)


def _bwd_norm_out(n, g_out, dx, *, name):
    t, d = n.shape
    tt = _pick_rows(t, ROW_TILE)

    def body(n_ref, go_ref, dx_ref, dn_ref, dgo_ref):
        i = pl.program_id(0)

        @pl.when(i == 0)
        def _():
            dgo_ref[...] = jnp.zeros_like(dgo_ref)

        nv = n_ref[...].astype(F32)
        dn, dgo = _rms_bwd(nv, _rms(nv), go_ref[...], dx_ref[...])
        dn_ref[...] = dn.astype(BF16)
        dgo_ref[...] += jnp.sum(dgo, axis=0, keepdims=True)

    row = pl.BlockSpec((tt, d), lambda i: (i, 0))
    vec = _full((1, d))
    return _call(body, name=name, grid=(t // tt,), in_specs=[row, vec, row], out_specs=[row, vec],
                 out_shape=[jax.ShapeDtypeStruct((t, d), BF16), jax.ShapeDtypeStruct((1, d), F32)],
                 compiler_params=_params(("arbitrary",)))(n, g_out, dx)


def _shift_down(cur, halo, s):
    return jnp.concatenate([halo[SUBLANES - s:], cur[:cur.shape[0] - s]], axis=0)


def _shift_up(cur, halo, s):
    return jnp.concatenate([cur[s:], halo[:s]], axis=0)


def _conva_fwd(pa, w, g, *, d, seq, name):
    t = pa.shape[0]
    tt = _pick_rows(seq, ROW_TILE)
    tps = seq // tt

    def body(xa_ref, ca_ref, ba_ref, w_ref, g_ref, ya_ref, v_ref, carry):
        i = pl.program_id(0)

        @pl.when(i % tps == 0)
        def _():
            carry[...] = jnp.zeros_like(carry)

        u = ca_ref[...].astype(F32) * xa_ref[...].astype(F32)
        halo = carry[...]
        wv = w_ref[...]
        v = wv[2:3] * u + wv[1:2] * _shift_down(u, halo, 1) + wv[0:1] * _shift_down(u, halo, 2)
        carry[...] = u[tt - SUBLANES:]
        yp = ba_ref[...].astype(F32) * v
        ya_ref[...] = (yp * _rms(yp) * g_ref[...]).astype(BF16)
        v_ref[...] = v.astype(BF16)

    col = lambda c: pl.BlockSpec((tt, d), lambda i, c=c: (i, c))
    row = pl.BlockSpec((tt, d), lambda i: (i, 0))
    return _call(body, name=name, grid=(t // tt,),
                 in_specs=[col(0), col(1), col(2), _full((CONV_K, d)), _full((1, d))], out_specs=[row, row],
                 out_shape=[jax.ShapeDtypeStruct((t, d), BF16), jax.ShapeDtypeStruct((t, d), BF16)],
                 scratch_shapes=[pltpu.VMEM((SUBLANES, d), F32)],
                 compiler_params=_params(("arbitrary",)))(pa, pa, pa, w, g)


def _conva_bwd(dcat, pa, v, w, g, *, d, seq, name):
    t, width = pa.shape
    d3 = 3 * d
    tt = _pick_rows(seq, ROW_TILE)
    tps = seq // tt
    nt = t // tt

    def body(dya_ref, xa_ref, ca_ref, ba_ref, v_ref, w_ref, g_ref, dpa_ref, dw_ref, dg_ref, carry):
        i = pl.program_id(0)

        @pl.when(i == 0)
        def _():
            dw_ref[...] = jnp.zeros_like(dw_ref)
            dg_ref[...] = jnp.zeros_like(dg_ref)

        @pl.when(i % tps == 0)
        def _():
            carry[...] = jnp.zeros_like(carry)

        xa, ca, ba, vv = [r[...].astype(F32) for r in (xa_ref, ca_ref, ba_ref, v_ref)]
        yp = ba * vv
        dyp, dgt = _rms_bwd(yp, _rms(yp), g_ref[...], dya_ref[...].astype(F32))
        dg_ref[...] += jnp.sum(dgt, axis=0, keepdims=True)
        dv = dyp * ba
        halo = carry[...]
        dv1 = _shift_up(dv, halo, 1)
        dv2 = _shift_up(dv, halo, 2)
        carry[...] = dv[:SUBLANES]
        wv = w_ref[...]
        du = wv[2:3] * dv + wv[1:2] * dv1 + wv[0:1] * dv2
        u = ca * xa
        dw_ref[0:1, :] += jnp.sum(u * dv2, axis=0, keepdims=True)
        dw_ref[1:2, :] += jnp.sum(u * dv1, axis=0, keepdims=True)
        dw_ref[2:3, :] += jnp.sum(u * dv, axis=0, keepdims=True)
        dpa_ref[:, 0:d] = (du * ca).astype(BF16)
        dpa_ref[:, d:2 * d] = (du * xa).astype(BF16)
        dpa_ref[:, 2 * d:3 * d] = (dyp * vv).astype(BF16)

    rcol = lambda c: pl.BlockSpec((tt, d), lambda i, c=c: (nt - 1 - i, c))
    return _call(body, name=name, grid=(nt,),
                 in_specs=[rcol(0), rcol(0), rcol(1), rcol(2), rcol(0), _full((CONV_K, d)), _full((1, d))],
                 out_specs=[pl.BlockSpec((tt, d3), lambda i: (nt - 1 - i, 0)), _full((CONV_K, d)), _full((1, d))],
                 out_shape=[jax.ShapeDtypeStruct((t, width), BF16), jax.ShapeDtypeStruct((CONV_K, d), F32),
                            jax.ShapeDtypeStruct((1, d), F32)],
                 scratch_shapes=[pltpu.VMEM((SUBLANES, d), F32)],
                 compiler_params=_params(("arbitrary",)))(dcat, pa, pa, pa, v, w, g)


CONV_CH = 512


def _convb_fwd(proj, w, bias, *, col0, seq, name):
    t = proj.shape[0]
    c = w.shape[1]
    cb = _pick(c, CONV_CH)
    assert col0 % cb == 0
    tt = _pick_rows(seq, 2 * ROW_TILE)
    tps = seq // tt

    def body(p_ref, w_ref, b_ref, o_ref, carry):
        i = pl.program_id(1)

        @pl.when(i % tps == 0)
        def _():
            carry[...] = jnp.zeros_like(carry)

        p = p_ref[...].astype(F32)
        halo = carry[...]
        wv = w_ref[...]
        o = wv[3:4] * p + b_ref[...]
        for s in (1, 2, 3):
            o = o + wv[3 - s:4 - s] * _shift_down(p, halo, s)
        carry[...] = p[tt - SUBLANES:]
        o_ref[...] = o.astype(BF16)

    return _call(body, name=name, grid=(c // cb, t // tt),
                 in_specs=[pl.BlockSpec((tt, cb), lambda jc, i: (i, col0 // cb + jc)),
                           pl.BlockSpec((SSM_CONV_K, cb), lambda jc, i: (0, jc)), pl.BlockSpec((1, cb), lambda jc, i: (0, jc))],
                 out_specs=pl.BlockSpec((tt, cb), lambda jc, i: (i, jc)), out_shape=jax.ShapeDtypeStruct((t, c), BF16),
                 scratch_shapes=[pltpu.VMEM((SUBLANES, cb), F32)],
                 compiler_params=_params(("arbitrary", "arbitrary")))(proj, w, bias)


def _convb_bwd(dconv, proj, w, dproj, *, col0, seq, name):
    t, c = dconv.shape
    cb = _pick(c, CONV_CH)
    assert col0 % cb == 0
    tt = _pick_rows(seq, 2 * ROW_TILE)
    tps = seq // tt
    nt = t // tt

    def body(dc_ref, p_ref, w_ref, dproj_in, dp_ref, dw_ref, db_ref, carry):
        del dproj_in
        i = pl.program_id(1)

        @pl.when(i == 0)
        def _():
            dw_ref[...] = jnp.zeros_like(dw_ref)
            db_ref[...] = jnp.zeros_like(db_ref)

        @pl.when(i % tps == 0)
        def _():
            carry[...] = jnp.zeros_like(carry)

        dc = dc_ref[...].astype(F32)
        p = p_ref[...].astype(F32)
        halo = carry[...]
        wv = w_ref[...]
        dp = wv[3:4] * dc
        dw_ref[3:4, :] += jnp.sum(p * dc, axis=0, keepdims=True)
        for s in (1, 2, 3):
            dcs = _shift_up(dc, halo, s)
            dp = dp + wv[3 - s:4 - s] * dcs
            dw_ref[3 - s:4 - s, :] += jnp.sum(p * dcs, axis=0, keepdims=True)
        carry[...] = dc[:SUBLANES]
        db_ref[...] += jnp.sum(dc, axis=0, keepdims=True)
        dp_ref[...] = dp.astype(BF16)

    win_spec = pl.BlockSpec((tt, cb), lambda jc, i: (nt - 1 - i, col0 // cb + jc))
    taps = pl.BlockSpec((SSM_CONV_K, cb), lambda jc, i: (0, jc))
    return _call(body, name=name, grid=(c // cb, nt),
                 in_specs=[pl.BlockSpec((tt, cb), lambda jc, i: (nt - 1 - i, jc)), win_spec, taps,
                           pl.BlockSpec(memory_space=pl.ANY)],
                 out_specs=[win_spec, taps, pl.BlockSpec((1, cb), lambda jc, i: (0, jc))],
                 out_shape=[jax.ShapeDtypeStruct(dproj.shape, BF16), jax.ShapeDtypeStruct((SSM_CONV_K, c), F32),
                            jax.ShapeDtypeStruct((1, c), F32)],
                 input_output_aliases={3: 0},
                 scratch_shapes=[pltpu.VMEM((SUBLANES, cb), F32)],
                 compiler_params=_params(("arbitrary", "arbitrary")))(dconv, proj, w, dproj)


def _expand_heads(x, ev):
    return jnp.dot(x, ev, precision=HIGHEST, preferred_element_type=F32)


def _head_sums(v, ev):
    return lax.dot_general(v, ev, (((1,), (1,)), ((), ())), precision=HIGHEST, preferred_element_type=F32)


def _ssd_common(c_ref, pdt_ref, dtb_ref, alog_ref, e_ref, h):
    cp = c_ref[...].astype(F32)
    sg = _sigmoid(cp)
    act = cp * sg
    pre = pdt_ref[:, 0:h] + dtb_ref[...]
    dt = _softplus(pre)
    a = -jnp.exp(alog_ref[...])
    adt = dt * a
    row = lax.broadcasted_iota(jnp.int32, (CHUNK, CHUNK), 0)
    col = lax.broadcasted_iota(jnp.int32, (CHUNK, CHUNK), 1)
    tril = row >= col
    cs = jnp.dot(tril.astype(F32), adt, precision=HIGHEST, preferred_element_type=F32)
    cs_t = lax.dot_general(adt, (col >= row).astype(F32), (((0,), (0,)), ((), ())), precision=HIGHEST,
                           preferred_element_type=F32)
    ev = e_ref[...]
    dt_l = _expand_heads(dt, ev)
    ecs_l = jnp.exp(_expand_heads(cs, ev))
    return dict(cp=cp, sg=sg, act=act, pre=pre, dt=dt, a=a, cs=cs, cs_t=cs_t, dt_l=dt_l, ecs_l=ecs_l,
                tril=tril, row=row, col=col, lo=col < HEAD_DIM)


def _dot_nt(a, b):
    return lax.dot_general(a, b, (((1,), (1,)), ((), ())), preferred_element_type=F32)


def _dot_tn(a, b):
    return lax.dot_general(a, b, (((0,), (0,)), ((), ())), preferred_element_type=F32)


def _dot(a, b):
    return jnp.dot(a, b, preferred_element_type=F32)


def _ssd_fwd(cpre, pdt, pz, ya, dtb, alog, dsk_lane, gs, emat, *, nseq, seq, name):
    t, xbc = cpre.shape
    d = ya.shape[1]
    h = d // HEAD_DIM
    npair = h // 2
    ppg = npair // SSM_GROUPS
    nc = seq // CHUNK
    gw = d // SSM_GROUPS
    bc0 = d
    cc0 = d + SSM_GROUPS * D_STATE

    def body(c_ref, pdt_ref, z_ref, ya_ref, dtb_ref, alog_ref, dsk_ref, gs_ref, e_ref, cat_ref, y2_ref, hp_ref, h_ref):
        @pl.when(pl.program_id(0) == 0)
        def _():
            h_ref[...] = jnp.zeros_like(h_ref)

        for sq in range(nseq):
            one_seq(c_ref.at[sq], pdt_ref.at[sq], z_ref.at[sq], ya_ref.at[sq], dtb_ref, alog_ref, dsk_ref, gs_ref, e_ref,
                    cat_ref.at[sq], y2_ref.at[sq], hp_ref.at[sq], h_ref.at[sq])

    def one_seq(c_ref, pdt_ref, z_ref, ya_ref, dtb_ref, alog_ref, dsk_ref, gs_ref, e_ref, cat_ref, y2_ref, hp_ref, h_ref):
        q = _ssd_common(c_ref, pdt_ref, dtb_ref, alog_ref, e_ref, h)
        act, cs, lo, ecs_l = q["act"], q["cs"], q["lo"], q["ecs_l"]
        xs = act[:, :d]
        xd = xs * q["dt_l"]
        ys = []
        for g in range(SSM_GROUPS):
            bg = act[:, bc0 + g * D_STATE: bc0 + (g + 1) * D_STATE]
            cgb = act[:, cc0 + g * D_STATE: cc0 + (g + 1) * D_STATE].astype(BF16)
            s = _dot_nt(cgb, bg.astype(BF16))
            bg_t = bg.T
            for jj in range(ppg):
                j = g * ppg + jj
                sl = slice(LANES * j, LANES * (j + 1))
                xdj = xd[:, sl]
                x2 = jnp.concatenate([jnp.where(lo, xdj, 0.0), jnp.where(lo, 0.0, xdj)], axis=0).astype(BF16)
                hprev = h_ref[j]
                hp_ref[j] = hprev.astype(BF16)
                ms, bws_t = [], []
                for hh in (2 * j, 2 * j + 1):
                    csc = cs[:, hh:hh + 1]
                    cs_row = q["cs_t"][hh:hh + 1, :]
                    seg = jnp.broadcast_to(csc, (CHUNK, CHUNK)) - jnp.broadcast_to(cs_row, (CHUNK, CHUNK))
                    ms.append(s * jnp.exp(jnp.where(q["tril"], seg, -jnp.inf)))
                    bws_t.append(bg_t * jnp.exp(cs_row[:, CHUNK - 1:CHUNK] - cs_row))
                ydiag = _dot(jnp.concatenate(ms, axis=1).astype(BF16), x2)
                st = _dot(jnp.concatenate(bws_t, axis=1).astype(BF16), x2)
                ecs = ecs_l[:, sl]
                yoff = _dot(cgb, hprev.astype(BF16)) * ecs
                h_ref[j] = hprev * ecs[CHUNK - 1:CHUNK] + st
                ys.append(ydiag + yoff)
        y = jnp.concatenate(ys, axis=1) + dsk_ref[...] * xs
        y2_ref[...] = y.astype(BF16)
        zv = z_ref[...].astype(F32)
        y3 = y * (zv * _sigmoid(zv))
        cat_ref[:, 0:d] = ya_ref[...]
        for gi in range(SSM_GROUPS):
            seg = y3[:, gi * gw:(gi + 1) * gw]
            cat_ref[:, d + gi * gw:d + (gi + 1) * gw] = (seg * _rms(seg) * gs_ref[:, gi * gw:(gi + 1) * gw]).astype(BF16)

    chunk = lambda w, cb=0: pl.BlockSpec((nseq, CHUNK, w), lambda c, cb=cb: (0, c, cb))
    vec = lambda w: pl.BlockSpec((1, w), lambda c: (0, 0))
    hp_spec = pl.BlockSpec((nseq, None, npair, D_STATE, LANES), lambda c: (0, c, 0, 0, 0))
    per_seq = lambda a: a.reshape(nseq, seq, a.shape[1])
    cat, y2, hp = _call(
        body, name=name, grid=(nc,),
        in_specs=[chunk(xbc), chunk(LANES), chunk(d, 3), chunk(d), vec(h), vec(h), vec(d), vec(d),
                  pl.BlockSpec((h, d), lambda c: (0, 0))],
        out_specs=[chunk(2 * d), chunk(d), hp_spec],
        out_shape=[jax.ShapeDtypeStruct((nseq, seq, 2 * d), BF16), jax.ShapeDtypeStruct((nseq, seq, d), BF16),
                   jax.ShapeDtypeStruct((nseq, nc, npair, D_STATE, LANES), BF16)],
        scratch_shapes=[pltpu.VMEM((nseq, npair, D_STATE, LANES), F32)],
        compiler_params=_params(("arbitrary",)))(
            per_seq(cpre), per_seq(pdt), per_seq(pz), per_seq(ya), dtb, alog, dsk_lane, gs, emat)
    return cat.reshape(t, 2 * d), y2.reshape(t, d), hp


def _ssd_bwd(cpre, pdt, pz, y2, hprev_all, dcat, dtb, alog, dsk_lane, gs, emat, dproj, *, nseq, seq, name):
    t, xbc = cpre.shape
    d = y2.shape[1]
    h = d // HEAD_DIM
    npair = h // 2
    ppg = npair // SSM_GROUPS
    nc = seq // CHUNK
    gw = d // SSM_GROUPS
    bc0 = d
    cc0 = d + SSM_GROUPS * D_STATE

    def body(c_ref, pdt_ref, z_ref, y2_ref, hp_ref, dys_ref, dtb_ref, alog_ref, dsk_ref, gs_ref, e_ref, dproj_in,
             dconv_ref, dz_ref, dpdt_ref, dgs_ref, ddsk_ref, ddtb_ref, dalog_ref, dh_ref):
        del dproj_in
        b = pl.program_id(0)
        c = pl.program_id(1)

        @pl.when(c == 0)
        def _():
            dh_ref[...] = jnp.zeros_like(dh_ref)

        @pl.when((b == 0) & (c == 0))
        def _():
            dgs_ref[...] = jnp.zeros_like(dgs_ref)
            ddsk_ref[...] = jnp.zeros_like(ddsk_ref)
            ddtb_ref[...] = jnp.zeros_like(ddtb_ref)
            dalog_ref[...] = jnp.zeros_like(dalog_ref)

        q = _ssd_common(c_ref, pdt_ref, dtb_ref, alog_ref, e_ref, h)
        cp, sg, act, cs, a, dt, lo = q["cp"], q["sg"], q["act"], q["cs"], q["a"], q["dt"], q["lo"]
        ecs_l, dt_l = q["ecs_l"], q["dt_l"]
        ev = e_ref[...]
        xs = act[:, :d]
        xd = xs * dt_l
        row16 = lax.broadcasted_iota(jnp.int32, (CHUNK, h), 0)
        hid = lax.broadcasted_iota(jnp.int32, (1, h), 1)
        hid_t = lax.broadcasted_iota(jnp.int32, (h, 1), 0)

        zv = z_ref[...].astype(F32)
        sz = _sigmoid(zv)
        siluz = zv * sz
        y2v = y2_ref[...].astype(F32)
        y3 = y2v * siluz
        dysv = dys_ref[...].astype(F32)
        dy3s = []
        for gi in range(SSM_GROUPS):
            gsl = slice(gi * gw, (gi + 1) * gw)
            seg = y3[:, gsl]
            dseg, dgt = _rms_bwd(seg, _rms(seg), gs_ref[:, gsl], dysv[:, gsl])
            dy3s.append(dseg)
            dgs_ref[:, gsl] += jnp.sum(dgt, axis=0, keepdims=True)
        dy3 = jnp.concatenate(dy3s, axis=1)
        dy = dy3 * siluz
        dz_ref[...] = (dy3 * y2v * (sz * (1.0 + zv * (1.0 - sz)))).astype(BF16)
        ddsk_ref[...] += jnp.sum(_head_sums(dy * xs, ev), axis=0, keepdims=True)

        dcs = jnp.zeros((CHUNK, h), F32)
        dcs_t = jnp.zeros((h, CHUNK), F32)
        dxd_parts, yoff_parts, db_parts, dc_parts = [], [], [], []
        for g in range(SSM_GROUPS):
            bg = act[:, bc0 + g * D_STATE: bc0 + (g + 1) * D_STATE]
            cg = act[:, cc0 + g * D_STATE: cc0 + (g + 1) * D_STATE]
            bgb, cgb = bg.astype(BF16), cg.astype(BF16)
            cgb_t = cg.T.astype(BF16)
            s = _dot_nt(cgb, bgb)
            ds = jnp.zeros((CHUNK, CHUNK), F32)
            dbg = jnp.zeros((CHUNK, D_STATE), F32)
            dcg = jnp.zeros((CHUNK, D_STATE), F32)
            for jj in range(ppg):
                j = g * ppg + jj
                sl = slice(LANES * j, LANES * (j + 1))
                xdj = xd[:, sl]
                xdb = xdj.astype(BF16)
                x2 = jnp.concatenate([jnp.where(lo, xdj, 0.0), jnp.where(lo, 0.0, xdj)], axis=0).astype(BF16)
                dyj = dy[:, sl]
                dy2 = jnp.concatenate([jnp.where(lo, dyj, 0.0), jnp.where(lo, 0.0, dyj)], axis=0).astype(BF16)
                hpb = hp_ref[j]
                hprev = hpb.astype(F32)
                dhn = dh_ref[j]
                dhb = dhn.astype(BF16)
                dh2 = jnp.concatenate([jnp.where(lo, dhn, 0.0), jnp.where(lo, 0.0, dhn)], axis=0).astype(BF16)
                ecs = ecs_l[:, sl]
                gmat = (dyj * ecs).astype(BF16)
                yoff_parts.append(_dot(cgb, hpb) * ecs)
                dcg = dcg + _dot_nt(gmat, hpb)
                dh_ref[j] = dhn * ecs[CHUNK - 1:CHUNK] + _dot(cgb_t, gmat)
                t2 = dhn * hprev
                dbw2 = _dot_nt(x2, dhb)
                dm2 = _dot_nt(dy2, xdb)
                ms, bws = [], []
                for idx, hh in enumerate((2 * j, 2 * j + 1)):
                    msk = lo if idx == 0 else jnp.logical_not(lo)
                    onehot = (hid == hh).astype(F32)
                    csc = cs[:, hh:hh + 1]
                    seg = jnp.broadcast_to(csc, (CHUNK, CHUNK)) - jnp.broadcast_to(q["cs_t"][hh:hh + 1, :], (CHUNK, CHUNK))
                    lm = jnp.exp(jnp.where(q["tril"], seg, -jnp.inf))
                    m = s * lm
                    cs_last = cs[CHUNK - 1:CHUNK, hh:hh + 1]
                    dte = jnp.exp(cs_last - csc)
                    ms.append(m)
                    bws.append(bg * dte)
                    dbw = dbw2[idx * CHUNK:(idx + 1) * CHUNK]
                    dbg = dbg + dbw * dte
                    qv = jnp.sum(dbw * bg, axis=-1, keepdims=True) * dte
                    dm = dm2[idx * CHUNK:(idx + 1) * CHUNK]
                    wm = dm * m
                    rc = jnp.sum(wm, axis=-1, keepdims=True)
                    dcs_t = dcs_t - (hid_t == hh).astype(F32) * jnp.sum(wm, axis=0, keepdims=True)
                    ds = ds + dm * lm
                    ddec = jnp.sum(jnp.where(msk, t2, 0.0)) * jnp.exp(cs_last)
                    last = jnp.sum(qv) + ddec
                    dcs = dcs + (rc - qv) * onehot + jnp.where(row16 == CHUNK - 1, last * onehot, 0.0)
                dxd_s = _dot(jnp.concatenate(bws, axis=1).astype(BF16), dh2)
                dxd_d = _dot_tn(jnp.concatenate(ms, axis=0).astype(BF16), dy2)
                dxd_parts.append(dxd_s + dxd_d)
            dsb = ds.astype(BF16)
            dc_parts.append(dcg + _dot(dsb, bgb))
            db_parts.append(dbg + _dot_tn(dsb, cgb))
        yoff_all = jnp.concatenate(yoff_parts, axis=1)
        dxd_all = jnp.concatenate(dxd_parts, axis=1)
        dcs = dcs + _head_sums(dy * yoff_all, ev)
        triu = (q["col"] >= q["row"]).astype(F32)
        dadt = (jnp.dot(triu, dcs, precision=HIGHEST, preferred_element_type=F32)
                + lax.dot_general(triu, dcs_t, (((1,), (1,)), ((), ())), precision=HIGHEST, preferred_element_type=F32))
        ddt = dadt * a + _head_sums(dxd_all * xs, ev)
        dalog_ref[...] += jnp.sum(dadt * dt, axis=0, keepdims=True) * a
        dpre = ddt * _sigmoid(q["pre"])
        ddtb_ref[...] += jnp.sum(dpre, axis=0, keepdims=True)
        dpdt_ref[...] = jnp.zeros_like(dpdt_ref)
        dpdt_ref[:, 0:h] = dpre.astype(BF16)
        dxs = dxd_all * dt_l + dy * dsk_ref[...]
        dact = jnp.concatenate([dxs] + db_parts + dc_parts, axis=1)
        dconv_ref[...] = (dact * (sg * (1.0 + cp * (1.0 - sg)))).astype(BF16)

    rchunk = lambda w, cb=0: pl.BlockSpec((CHUNK, w), lambda b, c, cb=cb: (b * nc + nc - 1 - c, cb))
    vec = lambda w: pl.BlockSpec((1, w), lambda b, c: (0, 0))
    hp_spec = pl.BlockSpec((None, None, npair, D_STATE, LANES), lambda b, c: (b, nc - 1 - c, 0, 0, 0))
    return _call(body, name=name, grid=(nseq, nc),
                 in_specs=[rchunk(xbc), rchunk(LANES), rchunk(d, 3), rchunk(d), hp_spec, rchunk(d, 1),
                           vec(h), vec(h), vec(d), vec(d), pl.BlockSpec((h, d), lambda b, c: (0, 0)),
                           pl.BlockSpec(memory_space=pl.ANY)],
                 out_specs=[rchunk(xbc), rchunk(d, 3), rchunk(LANES), vec(d), vec(h), vec(h), vec(h)],
                 out_shape=[jax.ShapeDtypeStruct((t, xbc), BF16), jax.ShapeDtypeStruct(dproj.shape, BF16),
                            jax.ShapeDtypeStruct((t, LANES), BF16), jax.ShapeDtypeStruct((1, d), F32),
                            jax.ShapeDtypeStruct((1, h), F32), jax.ShapeDtypeStruct((1, h), F32),
                            jax.ShapeDtypeStruct((1, h), F32)],
                 input_output_aliases={11: 1},
                 scratch_shapes=[pltpu.VMEM((npair, D_STATE, LANES), F32)],
                 compiler_params=_params(("arbitrary", "arbitrary")))(
                     cpre, pdt, pz, y2, hprev_all, dcat, dtb, alog, dsk_lane, gs, emat, dproj)


def _sum_adamw(parts, w, m, v, *, name, layer=None, outs=None):
    n, r, c = parts.shape
    tr = _pick_rows(r, 256)
    bc1 = 1.0 - ADAM_B1 ** ADAM_STEP
    bc2 = 1.0 - ADAM_B2 ** ADAM_STEP

    def body(p_ref, w_ref, m_ref, v_ref, *rest):
        g_ref, d_ref, mo_ref, vo_ref = rest[-4:]
        g = p_ref[0].astype(F32)
        for k in range(1, n):
            g = g + p_ref[k].astype(F32)
        mn = ADAM_B1 * m_ref[...] + (1.0 - ADAM_B1) * g
        vn = ADAM_B2 * v_ref[...] + (1.0 - ADAM_B2) * (g * g)
        g_ref[...] = g
        mo_ref[...] = mn
        vo_ref[...] = vn
        d_ref[...] = -ADAM_LR * ((mn / bc1) / (jnp.sqrt(vn / bc2) + ADAM_EPS) + ADAM_WD * w_ref[...])

    p_spec = pl.BlockSpec((n, tr, c), lambda i: (0, i, 0))
    if layer is None:
        blk = pl.BlockSpec((tr, c), lambda i: (i, 0))
        return _call(body, name=name, grid=(r // tr,), in_specs=[p_spec, blk, blk, blk], out_specs=[blk] * 4,
                     out_shape=[jax.ShapeDtypeStruct((r, c), F32)] * 4,
                     compiler_params=_params(("parallel",)))(parts, w, m, v)
    blk = pl.BlockSpec((None, tr, c), lambda i: (layer, i, 0))
    if outs is None:
        outs = [lax.empty(w.shape, F32) for _ in range(4)]
    return _call(body, name=name, grid=(r // tr,),
                 in_specs=[p_spec, blk, blk, blk] + [pl.BlockSpec(memory_space=pl.ANY)] * 4, out_specs=[blk] * 4,
                 out_shape=[jax.ShapeDtypeStruct(w.shape, F32)] * 4, input_output_aliases={4 + k: k for k in range(4)},
                 compiler_params=_params(("parallel",)))(parts, w, m, v, *outs)


def _assemble_cols(blocks, *, name):
    nb, r, c = blocks.shape
    width = -(-nb * c // LANES) * LANES
    tr = _pick_rows(r, 256)

    def body(b_ref, o_ref):
        pieces = [b_ref[j] for j in range(nb)]
        if width > nb * c:
            pieces.append(jnp.zeros((tr, width - nb * c), blocks.dtype))
        o_ref[...] = jnp.concatenate(pieces, axis=1)

    return _call(body, name=name, grid=(r // tr,), in_specs=[pl.BlockSpec((nb, tr, c), lambda i: (0, i, 0))],
                 out_specs=pl.BlockSpec((tr, width), lambda i: (i, 0)), out_shape=jax.ShapeDtypeStruct((r, width), blocks.dtype),
                 compiler_params=_params(("parallel",)))(blocks)


def _split_cols(pieces, c, *, name):
    r = pieces[0].shape[0]
    tr = _pick_rows(r, 256)
    n_in = len(pieces)

    def body(*refs):
        o_ref = refs[n_in]
        x = jnp.concatenate([p[...] for p in refs[:n_in]], axis=1) if n_in > 1 else refs[0][...]
        for j in range(N_DEV):
            o_ref[j] = x[:, c * j:c * (j + 1)]

    return _call(body, name=name, grid=(r // tr,),
                 in_specs=[pl.BlockSpec((tr, p.shape[1]), lambda i: (i, 0)) for p in pieces],
                 out_specs=pl.BlockSpec((N_DEV, tr, c), lambda i: (0, i, 0)),
                 out_shape=jax.ShapeDtypeStruct((N_DEV, r, c), pieces[0].dtype),
                 compiler_params=_params(("parallel",)))(*pieces)


def _sum_parts(parts, *, name):
    n, r, c = parts.shape
    tr = _pick_rows(r, 256)

    def body(p_ref, g_ref):
        g = p_ref[0].astype(F32)
        for k in range(1, n):
            g = g + p_ref[k].astype(F32)
        g_ref[...] = g

    return _call(body, name=name, grid=(r // tr,), in_specs=[pl.BlockSpec((n, tr, c), lambda i: (0, i, 0))],
                 out_specs=pl.BlockSpec((tr, c), lambda i: (i, 0)), out_shape=jax.ShapeDtypeStruct((r, c), F32),
                 compiler_params=_params(("parallel",)))(parts)


def _peers():
    x, y, c = lax.axis_index("x"), lax.axis_index("y"), lax.axis_index("c")
    me = 4 * x + 2 * y + c
    out = []
    for k in range(1, N_DEV):
        px = (1 - x) if (k >> 2) & 1 else x
        py = (1 - y) if (k >> 1) & 1 else y
        pc = (1 - c) if k & 1 else c
        out.append(((px, py, pc), 4 * px + 2 * py + pc))
    return me, out


_HBM = pl.BlockSpec(memory_space=pltpu.HBM)
_SEM = pl.BlockSpec(memory_space=pltpu.SEMAPHORE)
_EFFECT = pltpu.SideEffectType.DATAFLOW_SIDE_EFFECTING


ALL_PEERS = tuple(range(1, N_DEV))
SAME_CORE_PEERS = (2, 4, 6)


def _slot(ref, j, c):
    if c is None:
        return ref.at[j]
    start = j * c
    return ref.at[:, pl.ds(start if isinstance(start, int) else pl.multiple_of(start, c), c)]


def _split_copies(s_refs, l_refs, send_sems, recv_sems, gather, incoming, ks, src_cols, land_cols):
    me, peers = _peers()
    local, remote = [], []
    for ti, (s_ref, l_ref) in enumerate(zip(s_refs, l_refs)):
        base = ti * N_DEV
        sc, lc = src_cols[ti], land_cols[ti]
        local.append(pltpu.make_async_copy(s_ref if gather else _slot(s_ref, me, sc), _slot(l_ref, me, lc),
                                           recv_sems.at[base + N_DEV - 1]))
        for k, (dev, pid) in enumerate(peers):
            if k + 1 not in ks:
                continue
            sems = dict(send_sem=send_sems.at[base + k], recv_sem=recv_sems.at[base + k], device_id=dev, device_id_type=MESH)
            src = s_ref if gather else _slot(s_ref, pid, sc)
            remote.append((
                pltpu.make_async_remote_copy(src_ref=src, dst_ref=_slot(l_ref, me, lc), **sems),
                pltpu.make_async_remote_copy(src_ref=src, dst_ref=_slot(l_ref, pid, lc), **sems) if incoming else None))
    return local, remote


def _exchange_start(srcs, *, gather, name, after=(), ks=ALL_PEERS, src_cols=None, land_cols=None):
    n = len(srcs)
    after = list(after)
    src_cols = list(src_cols or [None] * n)
    land_cols = list(land_cols or [None] * n)
    srcs = [pltpu.with_memory_space_constraint(s, pltpu.HBM) for s in srcs]

    def land_shape(s, sc, lc):
        block = tuple(s.shape) if gather else ((s.shape[0], sc) if sc else tuple(s.shape[1:]))
        return (block[0], N_DEV * lc) if lc else (N_DEV,) + block

    lands = [pltpu.with_memory_space_constraint(lax.empty(land_shape(s, sc, lc), s.dtype), pltpu.HBM)
             for s, sc, lc in zip(srcs, src_cols, land_cols)]

    def body(*refs):
        s_refs, l_refs = refs[:n], refs[n:2 * n]
        outs = refs[2 * n + len(after):]
        send_sems, recv_sems, token = outs[0], outs[1], outs[-1]
        local, remote = _split_copies(s_refs, l_refs, send_sems, recv_sems, gather, False, ks, src_cols, land_cols)
        for cp in local:
            cp.start()
        for out_cp, _ in remote:
            out_cp.start()
        token[...] = jnp.zeros_like(token)

    outs = _call(
        body, name=name,
        out_shape=(pltpu.SemaphoreType.DMA((n * N_DEV,)), pltpu.SemaphoreType.DMA((n * N_DEV,)),
                   *[pltpu.HBM(s.shape, s.dtype) for s in srcs], *[pltpu.HBM(l.shape, l.dtype) for l in lands],
                   jax.ShapeDtypeStruct((SUBLANES, LANES), F32)),
        in_specs=[_HBM] * (2 * n) + [pl.BlockSpec(memory_space=pl.ANY)] * len(after),
        out_specs=(_SEM, _SEM, *[_HBM] * (2 * n), pl.BlockSpec(memory_space=pltpu.VMEM)),
        input_output_aliases={k: k + 2 for k in range(2 * n)},
        compiler_params=pltpu.CompilerParams(has_side_effects=_EFFECT),
    )(*srcs, *lands, *after)
    return dict(n=n, gather=gather, ks=ks, src_cols=src_cols, land_cols=land_cols, sems=outs[:2], srcs=outs[2:2 + n],
                lands=outs[2 + n:2 + 2 * n]), outs[-1]


def _exchange_wait(state, after, *, name):
    n, gather, ks = state["n"], state["gather"], state["ks"]
    after = list(after)

    def body(*refs):
        s_refs, l_refs = refs[:n], refs[n:2 * n]
        send_sems, recv_sems = refs[2 * n], refs[2 * n + 1]
        local, remote = _split_copies(s_refs, l_refs, send_sems, recv_sems, gather, True, ks, state["src_cols"],
                                      state["land_cols"])
        for out_cp, in_cp in remote:
            out_cp.wait_send()
            in_cp.wait_recv()
        for cp in local:
            cp.wait()

    outs = _call(
        body, name=name,
        out_shape=tuple(pltpu.HBM(a.shape, a.dtype) for a in (*state["srcs"], *state["lands"])),
        in_specs=[_HBM] * (2 * n) + [_SEM, _SEM] + [pl.BlockSpec(memory_space=pl.ANY)] * len(after),
        out_specs=tuple([_HBM] * (2 * n)),
        input_output_aliases={k: k for k in range(2 * n)},
        compiler_params=pltpu.CompilerParams(has_side_effects=_EFFECT),
    )(*state["srcs"], *state["lands"], *state["sems"], *after)
    return outs[n:]


def _sibling_copies(l_refs, send_sems, recv_sems, incoming, land_cols):
    x, y, c = lax.axis_index("x"), lax.axis_index("y"), lax.axis_index("c")
    out = []
    for ti, l_ref in enumerate(l_refs):
        for q in range(4):
            px = (1 - x) if q & 2 else x
            py = (1 - y) if q & 1 else y
            mine = _slot(l_ref, 4 * px + 2 * py + c, land_cols[ti])
            theirs = _slot(l_ref, 4 * px + 2 * py + (1 - c), land_cols[ti])
            sems = dict(send_sem=send_sems.at[4 * ti + q], recv_sem=recv_sems.at[4 * ti + q],
                        device_id=(x, y, 1 - c), device_id_type=MESH)
            out.append((
                pltpu.make_async_remote_copy(src_ref=mine, dst_ref=mine, **sems),
                pltpu.make_async_remote_copy(src_ref=mine, dst_ref=theirs, **sems) if incoming else None))
    return out


def _sibling_start(lands, *, name, after=(), land_cols=None):
    n = len(lands)
    after = list(after)
    land_cols = list(land_cols or [None] * n)
    lands = [pltpu.with_memory_space_constraint(l, pltpu.HBM) for l in lands]

    def body(*refs):
        l_refs = refs[:n]
        outs = refs[n + len(after):]
        for out_cp, _ in _sibling_copies(l_refs, outs[0], outs[1], False, land_cols):
            out_cp.start()
        outs[-1][...] = jnp.zeros_like(outs[-1])

    outs = _call(
        body, name=name,
        out_shape=(pltpu.SemaphoreType.DMA((4 * n,)), pltpu.SemaphoreType.DMA((4 * n,)),
                   *[pltpu.HBM(l.shape, l.dtype) for l in lands], jax.ShapeDtypeStruct((SUBLANES, LANES), F32)),
        in_specs=[_HBM] * n + [pl.BlockSpec(memory_space=pl.ANY)] * len(after),
        out_specs=(_SEM, _SEM, *[_HBM] * n, pl.BlockSpec(memory_space=pltpu.VMEM)),
        input_output_aliases={k: k + 2 for k in range(n)},
        compiler_params=pltpu.CompilerParams(has_side_effects=_EFFECT),
    )(*lands, *after)
    return dict(n=n, land_cols=land_cols, sems=outs[:2], lands=outs[2:2 + n]), outs[-1]


def _sibling_wait(state, after, *, name):
    n = state["n"]
    after = list(after)

    def body(*refs):
        l_refs = refs[:n]
        for out_cp, in_cp in _sibling_copies(l_refs, refs[n], refs[n + 1], True, state["land_cols"]):
            out_cp.wait_send()
            in_cp.wait_recv()

    return _call(
        body, name=name,
        out_shape=tuple(pltpu.HBM(a.shape, a.dtype) for a in state["lands"]),
        in_specs=[_HBM] * n + [_SEM, _SEM] + [pl.BlockSpec(memory_space=pl.ANY)] * len(after),
        out_specs=tuple([_HBM] * n), input_output_aliases={k: k for k in range(n)},
        compiler_params=pltpu.CompilerParams(has_side_effects=_EFFECT),
    )(*state["lands"], *state["sems"], *after)


def _pack(arrs):
    flat = jnp.concatenate([a.reshape(-1).astype(F32) for a in arrs])
    pad = (-flat.shape[0]) % (SUBLANES * LANES)
    return jnp.pad(flat, (0, pad)).reshape(-1, LANES)


def _unpack(packed, shapes):
    flat = packed.reshape(-1)
    out, off = [], 0
    for s in shapes:
        n = 1
        for v in s:
            n *= v
        out.append(flat[off:off + n].reshape(s))
        off += n
    return out


SMALL = ("norm_mix_pre", "ssm_conv_b", "dt_bias", "a_log", "d_skip", "conv_out_norm", "ssm_out_norm",
         "norm_mix_post", "norm_mlp_pre", "norm_mlp_post", "conv_a_w", "ssm_conv_w")
BIG = ("w_in", "w_out", "w_up", "w_down")
ORDER = ("norm_mix_pre", "w_in", "conv_a_w", "ssm_conv_w", "ssm_conv_b", "dt_bias", "a_log", "d_skip",
         "conv_out_norm", "ssm_out_norm", "w_out", "norm_mix_post", "norm_mlp_pre", "w_up", "w_down", "norm_mlp_post")


def kernel(x, norm_mix_pre, w_in, conv_a_w, ssm_conv_w, ssm_conv_b, dt_bias, a_log, d_skip, conv_out_norm, ssm_out_norm, w_out, norm_mix_post, norm_mlp_pre, w_up, w_down, norm_mlp_post, loss_target, m_norm_mix_pre, m_w_in, m_conv_a_w, m_ssm_conv_w, m_ssm_conv_b, m_dt_bias, m_a_log, m_d_skip, m_conv_out_norm, m_ssm_out_norm, m_w_out, m_norm_mix_post, m_norm_mlp_pre, m_w_up, m_w_down, m_norm_mlp_post, v_norm_mix_pre, v_w_in, v_conv_a_w, v_ssm_conv_w, v_ssm_conv_b, v_dt_bias, v_a_log, v_d_skip, v_conv_out_norm, v_ssm_out_norm, v_w_out, v_norm_mix_post, v_norm_mlp_pre, v_w_up, v_w_down, v_norm_mlp_post):
    W = dict(norm_mix_pre=norm_mix_pre, w_in=w_in, conv_a_w=conv_a_w, ssm_conv_w=ssm_conv_w, ssm_conv_b=ssm_conv_b,
             dt_bias=dt_bias, a_log=a_log, d_skip=d_skip, conv_out_norm=conv_out_norm, ssm_out_norm=ssm_out_norm,
             w_out=w_out, norm_mix_post=norm_mix_post, norm_mlp_pre=norm_mlp_pre, w_up=w_up, w_down=w_down,
             norm_mlp_post=norm_mlp_post)
    M = dict(norm_mix_pre=m_norm_mix_pre, w_in=m_w_in, conv_a_w=m_conv_a_w, ssm_conv_w=m_ssm_conv_w,
             ssm_conv_b=m_ssm_conv_b, dt_bias=m_dt_bias, a_log=m_a_log, d_skip=m_d_skip,
             conv_out_norm=m_conv_out_norm, ssm_out_norm=m_ssm_out_norm, w_out=m_w_out,
             norm_mix_post=m_norm_mix_post, norm_mlp_pre=m_norm_mlp_pre, w_up=m_w_up, w_down=m_w_down,
             norm_mlp_post=m_norm_mlp_post)
    V = dict(norm_mix_pre=v_norm_mix_pre, w_in=v_w_in, conv_a_w=v_conv_a_w, ssm_conv_w=v_ssm_conv_w,
             ssm_conv_b=v_ssm_conv_b, dt_bias=v_dt_bias, a_log=v_a_log, d_skip=v_d_skip,
             conv_out_norm=v_conv_out_norm, ssm_out_norm=v_ssm_out_norm, w_out=v_w_out,
             norm_mix_post=v_norm_mix_post, norm_mlp_pre=v_norm_mlp_pre, w_up=v_w_up, w_down=v_w_down,
             norm_mlp_post=v_norm_mlp_post)

    nseq, seq, d = x.shape
    t = nseq * seq
    depth = w_in.shape[0]
    h = d // HEAD_DIM
    xbc = d + 2 * SSM_GROUPS * D_STATE
    in_cols = w_in.shape[2] * N_DEV
    d_mix = w_out.shape[1] * N_DEV
    d_ff = w_up.shape[2] * N_DEV
    me = 4 * lax.axis_index("x") + 2 * lax.axis_index("y") + lax.axis_index("c")
    ca_shard = conv_a_w.shape[2]
    sc_shard = ssm_conv_w.shape[2]

    tap_shapes = [conv_a_w.shape[1:], ssm_conv_w.shape[1:]]

    def gather_start(i, after=()):
        ks = SAME_CORE_PEERS
        st_in, tok_in = _exchange_start([w_in[i].astype(BF16), _pack([conv_a_w[i], ssm_conv_w[i]])], gather=True,
                                        name=f"gather_start_in_{i}", after=after, ks=ks)
        st_rest, tok_rest = _exchange_start([W[n][i].astype(BF16) for n in ("w_out", "w_up", "w_down")], gather=True,
                                            name=f"gather_start_rest_{i}", after=[tok_in], ks=ks, land_cols=rest_cols)
        return st_in, st_rest, tok_rest

    rest_cols = [None, d_ff // N_DEV, None]

    vec = lambda name, i: W[name][i].reshape(1, -1)
    emat = (lax.broadcasted_iota(jnp.int32, (h, d), 1) // HEAD_DIM == lax.broadcasted_iota(jnp.int32, (h, d), 0)).astype(F32)

    xcur = x.reshape(t, d)
    hcur = _norm_fwd(xcur, vec("norm_mix_pre", 0), name="norm_first")
    saved = []
    nxt = gather_start(0)
    sib_in = None
    for i in range(depth):
        st_in, st_rest, tok = nxt
        if sib_in is None:
            sib_in, _ = _sibling_start(_exchange_wait(st_in, [hcur, tok], name=f"gather_wait_in_{i}"),
                                       name=f"gather_sibling_start_in_{i}")
        win_g, taps_g = _sibling_wait(sib_in, [hcur], name=f"gather_sibling_wait_in_{i}")
        win = _assemble_cols(win_g, name=f"assemble_w_in_{i}")
        taps_j = [_unpack(taps_g[j], tap_shapes) for j in range(N_DEV)]
        conv_a_i = jnp.concatenate([tj[0] for tj in taps_j], axis=1)
        ssm_conv_i = jnp.concatenate([tj[1] for tj in taps_j], axis=1)
        proj = _mm(hcur, win, n=4 * d + xbc, name=f"fwd_proj_{i}", out_dtypes=(BF16,))
        pdt = _mm(hcur, win, n=LANES, b_off=4 * d + xbc, name=f"fwd_proj_dt_{i}")
        ya, va = _conva_fwd(proj, conv_a_i, vec("conv_out_norm", i), d=d, seq=seq, name=f"fwd_conv_a_{i}")
        cpre = _convb_fwd(proj, ssm_conv_i, vec("ssm_conv_b", i), col0=4 * d, seq=seq, name=f"fwd_conv_b_{i}")
        dsk_lane = jnp.repeat(W["d_skip"][i], HEAD_DIM).reshape(1, d)
        st_sib, tok_sib = _sibling_start(_exchange_wait(st_rest, [cpre], name=f"gather_wait_rest_{i}"),
                                         name=f"gather_sibling_start_rest_{i}", land_cols=rest_cols)
        cat, y2, hprev = _ssd_fwd(cpre, pdt, proj, ya, vec("dt_bias", i) + tok_sib[0:1, 0:1], vec("a_log", i), dsk_lane,
                                  vec("ssm_out_norm", i), emat, nseq=nseq, seq=seq, name=f"fwd_ssd_{i}")
        wout_g, wup_g, wdown_g = _sibling_wait(st_sib, [cat], name=f"gather_sibling_wait_rest_{i}")
        lw = dict(win=win, wout=wout_g.reshape(d_mix, d),
                  wup=wup_g, wdown=wdown_g.reshape(d_ff, d),
                  conv_a=conv_a_i, ssm_conv=ssm_conv_i)
        after = []
        if i + 1 < depth:
            nxt = gather_start(i + 1, after=[wout_g])
            after = [nxt[2]]
        x1, h2, mix = _mm(cat, lw["wout"], name=f"fwd_out_{i}", after=after, out_dtypes=(F32, BF16, BF16),
                          epi=_epi_resid_norm, extras=(xcur,), vecs=(vec("norm_mix_post", i), vec("norm_mlp_pre", i)),
                          tm_cap=FUSED_ROWS)
        f = _mm(h2, lw["wup"], name=f"fwd_up_{i}", out_dtypes=(BF16,), epi=_epi_relu2)
        g_next = vec("norm_mix_pre", i + 1) if i + 1 < depth else None
        after = []
        if i + 1 < depth:
            sib_in, tok_in = _sibling_start(_exchange_wait(nxt[0], [f], name=f"gather_wait_in_{i + 1}"),
                                            name=f"gather_sibling_start_in_{i + 1}")
            after = [tok_in]
        dn = _mm(f, lw["wdown"], name=f"fwd_down_{i}", out_dtypes=(BF16,), after=after)
        x2, hnext = _resid_norm(x1, dn, vec("norm_mlp_post", i), g_next, name=f"fwd_post_mlp_{i}")
        saved.append(dict(lw=lw, x0=xcur, h=hcur, proj=proj, pdt=pdt, va=va, cpre=cpre, y2=y2,
                          hprev=hprev, cat=cat, mix=mix, x1=x1, h2=h2, f=f, dn=dn, dsk_lane=dsk_lane))
        xcur, hcur = x2, hnext

    dx, loss_part = _loss_fwd_bwd(xcur, loss_target.reshape(t, d), name="loss")
    loss = lax.psum(loss_part[0, 0], ("x", "y", "c"))

    small_grads = {n: [None] * depth for n in SMALL}
    big_out = {n: None for n in BIG}

    def finish(pending, after):
        li, st_a, st_b = pending

        def update(n, parts):
            big_out[n] = _sum_adamw(parts, W[n], M[n], V[n], layer=li, outs=big_out[n], name=f"adamw_{n}_{li}")

        p_down, p_up, p_out = _exchange_wait(st_a, after, name=f"scatter_wait_a_{li}")
        update("w_down", p_down)
        update("w_up", p_up)
        update("w_out", p_out)
        p_in, = _exchange_wait(st_b, after + [big_out["w_out"][0]], name=f"scatter_wait_b_{li}")
        update("w_in", p_in)

    pending = None
    for i in reversed(range(depth)):
        s = saved[i]
        lw = s["lw"]
        ddn, dg = _bwd_norm_out(s["dn"], vec("norm_mlp_post", i), dx, name=f"bwd_norm_mlp_post_{i}")
        small_grads["norm_mlp_post"][i] = dg
        dup = _mm(ddn, lw["wdown"], tb=True, name=f"bwd_down_dx_{i}", out_dtypes=(BF16,), epi=_epi_drelu2,
                  extras=(s["f"],))
        g_wdown = _mm(s["f"], ddn, ta=True, name=f"bwd_down_dw_{i}", out_dtypes=(BF16,))
        dh2 = _mm(dup, lw["wup"], tb=True, name=f"bwd_up_dx_{i}", out_dtypes=(BF16,))
        g_wup = _mm(s["h2"], dup, ta=True, name=f"bwd_up_dw_{i}", out_dtypes=(BF16,))
        dx1, dmix, dg_pre, dg_post = _bwd_norm_pair(s["x1"], [dh2], dx, s["mix"], vec("norm_mlp_pre", i),
                                                    vec("norm_mix_post", i), name=f"bwd_norm_mix_post_{i}")
        small_grads["norm_mlp_pre"][i] = dg_pre
        small_grads["norm_mix_post"][i] = dg_post
        dcat = _mm(dmix, lw["wout"], tb=True, name=f"bwd_out_dx_{i}", out_dtypes=(BF16,))
        g_wout = _mm(s["cat"], dmix, ta=True, name=f"bwd_out_dw_{i}", out_dtypes=(BF16,))
        st_a, tok_a = _exchange_start(
            [g_wdown.reshape(N_DEV, d_ff // N_DEV, d), g_wup, g_wout.reshape(N_DEV, d_mix // N_DEV, d)],
            gather=False, name=f"scatter_start_a_{i}", src_cols=[None, d_ff // N_DEV, None])
        dproj, dcaw, dgca = _conva_bwd(dcat, s["proj"], s["va"], lw["conv_a"],
                                       vec("conv_out_norm", i) + tok_a[0:1, 0:1], d=d, seq=seq, name=f"bwd_conv_a_{i}")
        small_grads["conv_a_w"][i] = dcaw
        small_grads["conv_out_norm"][i] = dgca
        dconv, dproj, dpdt, dgs, ddsk, ddtb, dalog = _ssd_bwd(
            s["cpre"], s["pdt"], s["proj"], s["y2"], s["hprev"], dcat, vec("dt_bias", i), vec("a_log", i),
            s["dsk_lane"], vec("ssm_out_norm", i), emat, dproj, nseq=nseq, seq=seq, name=f"bwd_ssd_{i}")
        small_grads["ssm_out_norm"][i] = dgs
        small_grads["d_skip"][i] = ddsk
        small_grads["dt_bias"][i] = ddtb
        small_grads["a_log"][i] = dalog
        dproj, dscw, dscb = _convb_bwd(dconv, s["proj"], lw["ssm_conv"], dproj, col0=4 * d, seq=seq,
                                       name=f"bwd_conv_b_{i}")
        small_grads["ssm_conv_w"][i] = dscw
        small_grads["ssm_conv_b"][i] = dscb
        g_win = _split_cols([
            _mm(s["h"], dproj, ta=True, name=f"bwd_proj_dw_{i}", out_dtypes=(BF16,)),
            _mm(s["h"], dpdt, ta=True, name=f"bwd_proj_dt_dw_{i}", out_dtypes=(BF16,))],
            in_cols // N_DEV, name=f"split_g_w_in_{i}")
        st_b, tok_b = _exchange_start([g_win], gather=False, name=f"scatter_start_b_{i}")
        dh_parts = [_mm(dp, lw["win"], tb=True, b_koff=off, name=f"bwd_proj_{nm}dx_{i}", after=[tok_b], out_dtypes=(BF16,))
                    for nm, dp, off in (("", dproj, 0), ("dt_", dpdt, 4 * d + xbc))]
        dx, dg_in = _bwd_norm_in(s["x0"], dh_parts, dx1, vec("norm_mix_pre", i), name=f"bwd_norm_mix_pre_{i}")
        small_grads["norm_mix_pre"][i] = dg_in
        if pending is not None:
            finish(pending, [dx])
        pending = (i, st_a, st_b)

    grad_x = dx.reshape(nseq, seq, d)

    small_shapes_full = {n: (depth,) + tuple(small_grads[n][0].shape) for n in SMALL}
    gpack = _pack([jnp.stack(small_grads[n]) for n in SMALL])
    st_small, tok_small = _exchange_start([gpack], gather=True, name="allreduce_small_start")
    finish(pending, [dx, tok_small])
    gparts, = _exchange_wait(st_small, [big_out["w_in"][0]], name="allreduce_small_wait")

    def shard_of(n, full):
        if n == "conv_a_w":
            return lax.dynamic_slice_in_dim(full, me * ca_shard, ca_shard, axis=2)
        if n == "ssm_conv_w":
            return lax.dynamic_slice_in_dim(full, me * sc_shard, sc_shard, axis=2)
        return full.reshape(W[n].shape)

    gsum = _sum_parts(gparts, name="sum_small")
    gfull = _unpack(gsum, [small_shapes_full[n] for n in SMALL])
    gsmall = {n: shard_of(n, gf) for n, gf in zip(SMALL, gfull)}
    res = _sum_adamw(_pack([gsmall[n] for n in SMALL])[None], _pack([W[n] for n in SMALL]),
                     _pack([M[n] for n in SMALL]), _pack([V[n] for n in SMALL]), name="adamw_small")
    small_out = [dict(zip(SMALL, _unpack(r, [W[n].shape for n in SMALL]))) for r in res]

    def out_of(kind, n):
        return big_out[n][kind] if n in BIG else small_out[kind][n]

    return (loss, grad_x, *[out_of(k, n) for k in range(4) for n in ORDER])
```

```python
import jax
import jax.numpy as jnp
from jax import lax
from jax.experimental import pallas as pl
from jax.experimental.pallas import tpu as pltpu

F32 = jnp.float32
BF16 = jnp.bfloat16
HIGHEST = lax.Precision.HIGHEST
MESH = pl.DeviceIdType.MESH

EPS = 1e-6
HEAD_DIM = 64
D_STATE = 128
SSM_GROUPS = 2
CHUNK = 128
CONV_K = 3
SSM_CONV_K = 4
ADAM_LR = 0.001
ADAM_B1 = 0.9
ADAM_B2 = 0.999
ADAM_EPS = 1e-08
ADAM_WD = 0.01
ADAM_STEP = 10

N_DEV = 8
LANES = 128
SUBLANES = 8
VMEM_LIMIT = 48 * 1024 * 1024
ROW_TILE = 512
MM_TILE = 1024
MM_TILE_N = 2816
MM_VMEM_BUDGET = 40 * 1024 * 1024
FUSED_ROWS = 512


def _params(sem):
    return pltpu.CompilerParams(dimension_semantics=sem, vmem_limit_bytes=VMEM_LIMIT)


def _call(body, **kw):
    return pl.pallas_call(body, **kw)


def _pick(n, cap):
    best = None
    for t in range(LANES, min(n, cap) + 1, LANES):
        if n % t == 0:
            best = t
    return best or n


def _pick_rows(n, cap):
    best = None
    for t in range(SUBLANES, min(n, cap) + 1, SUBLANES):
        if n % t == 0:
            best = t
    return best or n


def _sigmoid(x):
    return 1.0 / (1.0 + jnp.exp(-x))


def _softplus(x):
    return jnp.maximum(x, 0.0) + jnp.log1p(jnp.exp(-jnp.abs(x)))


def _rms(x):
    return lax.rsqrt(jnp.mean(x * x, axis=-1, keepdims=True) + EPS)


def _rms_bwd(x, r, g, dy):
    gy = dy * g
    dx = r * gy - x * (r * r * r) * jnp.mean(gy * x, axis=-1, keepdims=True)
    return dx, dy * x * r


def _full(shape):
    return pl.BlockSpec(shape, lambda *_: (0,) * len(shape))


def _mm(a, b, *, name, ta=False, tb=False, out_dtypes=(F32,), epi=None, extras=(), n=None, b_off=0, b_koff=0,
        after=(), vecs=(), tm_cap=MM_TILE):
    m, k = (a.shape[1], a.shape[0]) if ta else a.shape
    if n is None:
        n = b.shape[0] if tb else b.shape[1]
    tm, tn, tk = _pick(m, tm_cap), _pick(n, MM_TILE_N), _pick(k, MM_TILE)
    while b_off % tn or n % tn:
        tn -= LANES
    if b_koff == 0 and k > MM_TILE:
        tk = _pick(k, MM_TILE_N)
    while b_koff % tk or k % tk:
        tk -= LANES

    def vmem_bytes(tk_):
        per_out = sum(jnp.dtype(dt).itemsize for dt in out_dtypes) + sum(e.dtype.itemsize for e in extras)
        return 2 * tk_ * (tm * a.dtype.itemsize + tn * b.dtype.itemsize) + tm * tn * (2 * per_out + 4)

    while vmem_bytes(tk) > MM_VMEM_BUDGET and tk % (2 * LANES) == 0 and not b_koff % (tk // 2):
        tk //= 2
    nk = k // tk
    nm, nn = m // tm, n // tn
    jo = b_off // tn
    ko = b_koff // tk
    a_bytes = m * k * a.dtype.itemsize
    b_bytes = n * k * b.dtype.itemsize
    m_outer = a_bytes + nm * b_bytes <= b_bytes + nn * a_bytes
    ij = (lambda g0, g1: (g0, g1)) if m_outer else (lambda g0, g1: (g1, g0))
    grid = (nm, nn, nk) if m_outer else (nn, nm, nk)

    def a_map(g0, g1, kk):
        i, _ = ij(g0, g1)
        return (kk, i) if ta else (i, kk)

    def b_map(g0, g1, kk):
        _, j = ij(g0, g1)
        return (j + jo, kk + ko) if tb else (kk + ko, j + jo)

    def o_map(g0, g1, kk):
        return ij(g0, g1)

    a_spec = pl.BlockSpec((tk, tm) if ta else (tm, tk), a_map)
    b_spec = pl.BlockSpec((tn, tk) if tb else (tk, tn), b_map)
    o_spec = pl.BlockSpec((tm, tn), o_map)
    dims = (((0 if ta else 1,), (1 if tb else 0,)), ((), ()))
    n_ex = len(extras) + len(vecs)
    after = list(after)
    o0 = 2 + n_ex + len(after)

    def finish(acc, ex, outs):
        res = (acc,) if epi is None else epi(acc, *[e[...] for e in ex])
        for o, r in zip(outs, res):
            o[...] = r.astype(o.dtype)

    def body_single(*refs):
        a_ref, b_ref = refs[:2]
        acc = lax.dot_general(a_ref[...].astype(BF16), b_ref[...].astype(BF16), dims, preferred_element_type=F32)
        finish(acc, refs[2:2 + n_ex], refs[o0:])

    def body_multi(*refs):
        a_ref, b_ref = refs[:2]
        acc = refs[-1]
        kk = pl.program_id(2)

        @pl.when(kk == 0)
        def _():
            acc[...] = jnp.zeros_like(acc)

        acc[...] += lax.dot_general(a_ref[...].astype(BF16), b_ref[...].astype(BF16), dims, preferred_element_type=F32)

        @pl.when(kk == nk - 1)
        def _():
            finish(acc[...], refs[2:2 + n_ex], refs[o0:-1])

    v_spec = pl.BlockSpec((1, tn), lambda g0, g1, kk: (0, ij(g0, g1)[1]))
    outs = _call(
        body_single if nk == 1 else body_multi, name=name, grid=grid,
        in_specs=([a_spec, b_spec] + [o_spec] * len(extras) + [v_spec] * len(vecs)
                  + [pl.BlockSpec(memory_space=pl.ANY)] * len(after)),
        out_specs=[o_spec] * len(out_dtypes),
        out_shape=[jax.ShapeDtypeStruct((m, n), dt) for dt in out_dtypes],
        scratch_shapes=[] if nk == 1 else [pltpu.VMEM((tm, tn), F32)],
        compiler_params=_params(("parallel", "parallel", "arbitrary")),
    )(a, b, *extras, *vecs, *after)
    return outs[0] if len(outs) == 1 else outs


def _epi_resid_norm(acc, x, g_res, g_next):
    xn = x + acc * _rms(acc) * g_res
    return xn, xn * _rms(xn) * g_next, acc


def _epi_relu2(acc):
    r = jnp.maximum(acc, 0.0)
    return (r * r,)


def _epi_drelu2(acc, f):
    return (acc * (2.0 * jnp.sqrt(f).astype(F32)),)


def _norm_fwd(x, g, *, name):
    t, d = x.shape
    tt = _pick_rows(t, ROW_TILE)

    def body(x_ref, g_ref, h_ref):
        xv = x_ref[...]
        h_ref[...] = (xv * _rms(xv) * g_ref[...]).astype(BF16)

    row = pl.BlockSpec((tt, d), lambda i: (i, 0))
    return _call(body, name=name, grid=(t // tt,), in_specs=[row, _full((1, d))], out_specs=row,
                 out_shape=jax.ShapeDtypeStruct((t, d), BF16), compiler_params=_params(("parallel",)))(x, g)


def _resid_norm(x, n, g1, g2, *, name):
    t, d = x.shape
    tt = _pick_rows(t, ROW_TILE)
    gains = [g1] if g2 is None else [g1, g2]

    def body(x_ref, n_ref, *refs):
        nv = n_ref[...].astype(F32)
        xn = x_ref[...] + nv * _rms(nv) * refs[0][...]
        refs[len(gains)][...] = xn
        if g2 is not None:
            refs[3][...] = (xn * _rms(xn) * refs[1][...]).astype(BF16)

    row = pl.BlockSpec((tt, d), lambda i: (i, 0))
    outs = _call(body, name=name, grid=(t // tt,), in_specs=[row, row] + [_full((1, d))] * len(gains),
                 out_specs=[row] * len(gains),
                 out_shape=[jax.ShapeDtypeStruct((t, d), F32), jax.ShapeDtypeStruct((t, d), BF16)][:len(gains)],
                 compiler_params=_params(("parallel",)))(x, n, *gains)
    return (outs[0], None) if g2 is None else outs


def _loss_fwd_bwd(xf, target, *, name):
    t, d = xf.shape
    tt = _pick_rows(t, ROW_TILE)
    nt = t // tt

    def body(x_ref, t_ref, dy_ref, loss_ref, acc):
        i = pl.program_id(0)

        @pl.when(i == 0)
        def _():
            acc[...] = jnp.zeros_like(acc)

        e = x_ref[...] - t_ref[...]
        dy_ref[...] = e * (1.0 / d)
        acc[...] += jnp.sum(e * e, axis=0, keepdims=True)

        @pl.when(i == nt - 1)
        def _():
            loss_ref[...] = jnp.sum(acc[...], axis=-1, keepdims=True) * (0.5 / d)

    row = pl.BlockSpec((tt, d), lambda i: (i, 0))
    return _call(body, name=name, grid=(nt,), in_specs=[row, row], out_specs=[row, _full((1, 1))],
                 out_shape=[jax.ShapeDtypeStruct((t, d), F32), jax.ShapeDtypeStruct((1, 1), F32)],
                 scratch_shapes=[pltpu.VMEM((1, d), F32)], compiler_params=_params(("arbitrary",)))(xf, target)


def _bwd_norm_pair(xin, dh, dres, n, g_in, g_out, *, name):
    t, d = xin.shape
    tt = _pick_rows(t, ROW_TILE)
    n_dh = len(dh)

    def body(*refs):
        x_ref = refs[0]
        dh_refs = refs[1:1 + n_dh]
        dres_ref, n_ref, gi_ref, go_ref, dx_ref, dn_ref, dgi_ref, dgo_ref = refs[1 + n_dh:]
        i = pl.program_id(0)

        @pl.when(i == 0)
        def _():
            dgi_ref[...] = jnp.zeros_like(dgi_ref)
            dgo_ref[...] = jnp.zeros_like(dgo_ref)

        xv = x_ref[...]
        dhv = dh_refs[0][...].astype(F32)
        for r in dh_refs[1:]:
            dhv = dhv + r[...].astype(F32)
        dxh, dgi = _rms_bwd(xv, _rms(xv), gi_ref[...], dhv)
        dx = dres_ref[...] + dxh
        dx_ref[...] = dx
        dgi_ref[...] += jnp.sum(dgi, axis=0, keepdims=True)
        nv = n_ref[...].astype(F32)
        dn, dgo = _rms_bwd(nv, _rms(nv), go_ref[...], dx)
        dn_ref[...] = dn.astype(BF16)
        dgo_ref[...] += jnp.sum(dgo, axis=0, keepdims=True)

    row = pl.BlockSpec((tt, d), lambda i: (i, 0))
    vec = _full((1, d))
    return _call(body, name=name, grid=(t // tt,), in_specs=[row] * (n_dh + 3) + [vec, vec],
                 out_specs=[row, row, vec, vec],
                 out_shape=[jax.ShapeDtypeStruct((t, d), F32), jax.ShapeDtypeStruct((t, d), BF16),
                            jax.ShapeDtypeStruct((1, d), F32), jax.ShapeDtypeStruct((1, d), F32)],
                 compiler_params=_params(("arbitrary",)))(xin, *dh, dres, n, g_in, g_out)


def _bwd_norm_in(xin, dh, dres, g_in, *, name):
    t, d = xin.shape
    tt = _pick_rows(t, ROW_TILE)
    n_dh = len(dh)

    def body(*refs):
        x_ref = refs[0]
        dh_refs = refs[1:1 + n_dh]
        dres_ref, gi_ref, dx_ref, dgi_ref = refs[1 + n_dh:]
        i = pl.program_id(0)

        @pl.when(i == 0)
        def _():
            dgi_ref[...] = jnp.zeros_like(dgi_ref)

        xv = x_ref[...]
        dhv = dh_refs[0][...].astype(F32)
        for r in dh_refs[1:]:
            dhv = dhv + r[...].astype(F32)
        dxh, dgi = _rms_bwd(xv, _rms(xv), gi_ref[...], dhv)
        dx_ref[...] = dres_ref[...] + dxh
        dgi_ref[...] += jnp.sum(dgi, axis=0, keepdims=True)

    row = pl.BlockSpec((tt, d), lambda i: (i, 0))
    vec = _full((1, d))
    return _call(body, name=name, grid=(t // tt,), in_specs=[row] * (n_dh + 2) + [vec],
                 out_specs=[row, vec],
                 out_shape=[jax.ShapeDtypeStruct((t, d), F32), jax.ShapeDtypeStruct((1, d), F32)],
                 compiler_params=_params(("arbitrary",)))(xin, *dh, dres, g_in)


def _bwd_norm_out(n, g_out, dx, *, name):
    t, d = n.shape
    tt = _pick_rows(t, ROW_TILE)

    def body(n_ref, go_ref, dx_ref, dn_ref, dgo_ref):
        i = pl.program_id(0)

        @pl.when(i == 0)
        def _():
            dgo_ref[...] = jnp.zeros_like(dgo_ref)

        nv = n_ref[...].astype(F32)
        dn, dgo = _rms_bwd(nv, _rms(nv), go_ref[...], dx_ref[...])
        dn_ref[...] = dn.astype(BF16)
        dgo_ref[...] += jnp.sum(dgo, axis=0, keepdims=True)

    row = pl.BlockSpec((tt, d), lambda i: (i, 0))
    vec = _full((1, d))
    return _call(body, name=name, grid=(t // tt,), in_specs=[row, vec, row], out_specs=[row, vec],
                 out_shape=[jax.ShapeDtypeStruct((t, d), BF16), jax.ShapeDtypeStruct((1, d), F32)],
                 compiler_params=_params(("arbitrary",)))(n, g_out, dx)


def _shift_down(cur, halo, s):
    return jnp.concatenate([halo[SUBLANES - s:], cur[:cur.shape[0] - s]], axis=0)


def _shift_up(cur, halo, s):
    return jnp.concatenate([cur[s:], halo[:s]], axis=0)


def _conva_fwd(pa, w, g, *, d, seq, name):
    t = pa.shape[0]
    tt = _pick_rows(seq, ROW_TILE)
    tps = seq // tt

    def body(xa_ref, ca_ref, ba_ref, w_ref, g_ref, ya_ref, v_ref, carry):
        i = pl.program_id(0)

        @pl.when(i % tps == 0)
        def _():
            carry[...] = jnp.zeros_like(carry)

        u = ca_ref[...].astype(F32) * xa_ref[...].astype(F32)
        halo = carry[...]
        wv = w_ref[...]
        v = wv[2:3] * u + wv[1:2] * _shift_down(u, halo, 1) + wv[0:1] * _shift_down(u, halo, 2)
        carry[...] = u[tt - SUBLANES:]
        yp = ba_ref[...].astype(F32) * v
        ya_ref[...] = (yp * _rms(yp) * g_ref[...]).astype(BF16)
        v_ref[...] = v.astype(BF16)

    col = lambda c: pl.BlockSpec((tt, d), lambda i, c=c: (i, c))
    row = pl.BlockSpec((tt, d), lambda i: (i, 0))
    return _call(body, name=name, grid=(t // tt,),
                 in_specs=[col(0), col(1), col(2), _full((CONV_K, d)), _full((1, d))], out_specs=[row, row],
                 out_shape=[jax.ShapeDtypeStruct((t, d), BF16), jax.ShapeDtypeStruct((t, d), BF16)],
                 scratch_shapes=[pltpu.VMEM((SUBLANES, d), F32)],
                 compiler_params=_params(("arbitrary",)))(pa, pa, pa, w, g)


def _conva_bwd(dcat, pa, v, w, g, *, d, seq, name):
    t, width = pa.shape
    d3 = 3 * d
    tt = _pick_rows(seq, ROW_TILE)
    tps = seq // tt
    nt = t // tt

    def body(dya_ref, xa_ref, ca_ref, ba_ref, v_ref, w_ref, g_ref, dpa_ref, dw_ref, dg_ref, carry):
        i = pl.program_id(0)

        @pl.when(i == 0)
        def _():
            dw_ref[...] = jnp.zeros_like(dw_ref)
            dg_ref[...] = jnp.zeros_like(dg_ref)

        @pl.when(i % tps == 0)
        def _():
            carry[...] = jnp.zeros_like(carry)

        xa, ca, ba, vv = [r[...].astype(F32) for r in (xa_ref, ca_ref, ba_ref, v_ref)]
        yp = ba * vv
        dyp, dgt = _rms_bwd(yp, _rms(yp), g_ref[...], dya_ref[...].astype(F32))
        dg_ref[...] += jnp.sum(dgt, axis=0, keepdims=True)
        dv = dyp * ba
        halo = carry[...]
        dv1 = _shift_up(dv, halo, 1)
        dv2 = _shift_up(dv, halo, 2)
        carry[...] = dv[:SUBLANES]
        wv = w_ref[...]
        du = wv[2:3] * dv + wv[1:2] * dv1 + wv[0:1] * dv2
        u = ca * xa
        dw_ref[0:1, :] += jnp.sum(u * dv2, axis=0, keepdims=True)
        dw_ref[1:2, :] += jnp.sum(u * dv1, axis=0, keepdims=True)
        dw_ref[2:3, :] += jnp.sum(u * dv, axis=0, keepdims=True)
        dpa_ref[:, 0:d] = (du * ca).astype(BF16)
        dpa_ref[:, d:2 * d] = (du * xa).astype(BF16)
        dpa_ref[:, 2 * d:3 * d] = (dyp * vv).astype(BF16)

    rcol = lambda c: pl.BlockSpec((tt, d), lambda i, c=c: (nt - 1 - i, c))
    return _call(body, name=name, grid=(nt,),
                 in_specs=[rcol(0), rcol(0), rcol(1), rcol(2), rcol(0), _full((CONV_K, d)), _full((1, d))],
                 out_specs=[pl.BlockSpec((tt, d3), lambda i: (nt - 1 - i, 0)), _full((CONV_K, d)), _full((1, d))],
                 out_shape=[jax.ShapeDtypeStruct((t, width), BF16), jax.ShapeDtypeStruct((CONV_K, d), F32),
                            jax.ShapeDtypeStruct((1, d), F32)],
                 scratch_shapes=[pltpu.VMEM((SUBLANES, d), F32)],
                 compiler_params=_params(("arbitrary",)))(dcat, pa, pa, pa, v, w, g)


CONV_CH = 512


def _convb_fwd(proj, w, bias, *, col0, seq, name):
    t = proj.shape[0]
    c = w.shape[1]
    cb = _pick(c, CONV_CH)
    assert col0 % cb == 0
    tt = _pick_rows(seq, 2 * ROW_TILE)
    tps = seq // tt

    def body(p_ref, w_ref, b_ref, o_ref, carry):
        i = pl.program_id(1)

        @pl.when(i % tps == 0)
        def _():
            carry[...] = jnp.zeros_like(carry)

        p = p_ref[...].astype(F32)
        halo = carry[...]
        wv = w_ref[...]
        o = wv[3:4] * p + b_ref[...]
        for s in (1, 2, 3):
            o = o + wv[3 - s:4 - s] * _shift_down(p, halo, s)
        carry[...] = p[tt - SUBLANES:]
        o_ref[...] = o.astype(BF16)

    return _call(body, name=name, grid=(c // cb, t // tt),
                 in_specs=[pl.BlockSpec((tt, cb), lambda jc, i: (i, col0 // cb + jc)),
                           pl.BlockSpec((SSM_CONV_K, cb), lambda jc, i: (0, jc)), pl.BlockSpec((1, cb), lambda jc, i: (0, jc))],
                 out_specs=pl.BlockSpec((tt, cb), lambda jc, i: (i, jc)), out_shape=jax.ShapeDtypeStruct((t, c), BF16),
                 scratch_shapes=[pltpu.VMEM((SUBLANES, cb), F32)],
                 compiler_params=_params(("arbitrary", "arbitrary")))(proj, w, bias)


def _convb_bwd(dconv, proj, w, dproj, *, col0, seq, name):
    t, c = dconv.shape
    cb = _pick(c, CONV_CH)
    assert col0 % cb == 0
    tt = _pick_rows(seq, 2 * ROW_TILE)
    tps = seq // tt
    nt = t // tt

    def body(dc_ref, p_ref, w_ref, dproj_in, dp_ref, dw_ref, db_ref, carry):
        del dproj_in
        i = pl.program_id(1)

        @pl.when(i == 0)
        def _():
            dw_ref[...] = jnp.zeros_like(dw_ref)
            db_ref[...] = jnp.zeros_like(db_ref)

        @pl.when(i % tps == 0)
        def _():
            carry[...] = jnp.zeros_like(carry)

        dc = dc_ref[...].astype(F32)
        p = p_ref[...].astype(F32)
        halo = carry[...]
        wv = w_ref[...]
        dp = wv[3:4] * dc
        dw_ref[3:4, :] += jnp.sum(p * dc, axis=0, keepdims=True)
        for s in (1, 2, 3):
            dcs = _shift_up(dc, halo, s)
            dp = dp + wv[3 - s:4 - s] * dcs
            dw_ref[3 - s:4 - s, :] += jnp.sum(p * dcs, axis=0, keepdims=True)
        carry[...] = dc[:SUBLANES]
        db_ref[...] += jnp.sum(dc, axis=0, keepdims=True)
        dp_ref[...] = dp.astype(BF16)

    win_spec = pl.BlockSpec((tt, cb), lambda jc, i: (nt - 1 - i, col0 // cb + jc))
    taps = pl.BlockSpec((SSM_CONV_K, cb), lambda jc, i: (0, jc))
    return _call(body, name=name, grid=(c // cb, nt),
                 in_specs=[pl.BlockSpec((tt, cb), lambda jc, i: (nt - 1 - i, jc)), win_spec, taps,
                           pl.BlockSpec(memory_space=pl.ANY)],
                 out_specs=[win_spec, taps, pl.BlockSpec((1, cb), lambda jc, i: (0, jc))],
                 out_shape=[jax.ShapeDtypeStruct(dproj.shape, BF16), jax.ShapeDtypeStruct((SSM_CONV_K, c), F32),
                            jax.ShapeDtypeStruct((1, c), F32)],
                 input_output_aliases={3: 0},
                 scratch_shapes=[pltpu.VMEM((SUBLANES, cb), F32)],
                 compiler_params=_params(("arbitrary", "arbitrary")))(dconv, proj, w, dproj)


def _expand_heads(x, ev):
    return jnp.dot(x, ev, precision=HIGHEST, preferred_element_type=F32)


def _head_sums(v, ev):
    return lax.dot_general(v, ev, (((1,), (1,)), ((), ())), precision=HIGHEST, preferred_element_type=F32)


def _ssd_common(c_ref, pdt_ref, dtb_ref, alog_ref, e_ref, h):
    cp = c_ref[...].astype(F32)
    sg = _sigmoid(cp)
    act = cp * sg
    pre = pdt_ref[:, 0:h] + dtb_ref[...]
    dt = _softplus(pre)
    a = -jnp.exp(alog_ref[...])
    adt = dt * a
    row = lax.broadcasted_iota(jnp.int32, (CHUNK, CHUNK), 0)
    col = lax.broadcasted_iota(jnp.int32, (CHUNK, CHUNK), 1)
    tril = row >= col
    cs = jnp.dot(tril.astype(F32), adt, precision=HIGHEST, preferred_element_type=F32)
    cs_t = lax.dot_general(adt, (col >= row).astype(F32), (((0,), (0,)), ((), ())), precision=HIGHEST,
                           preferred_element_type=F32)
    ev = e_ref[...]
    dt_l = _expand_heads(dt, ev)
    ecs_l = jnp.exp(_expand_heads(cs, ev))
    return dict(cp=cp, sg=sg, act=act, pre=pre, dt=dt, a=a, cs=cs, cs_t=cs_t, dt_l=dt_l, ecs_l=ecs_l,
                tril=tril, row=row, col=col, lo=col < HEAD_DIM)


def _dot_nt(a, b):
    return lax.dot_general(a, b, (((1,), (1,)), ((), ())), preferred_element_type=F32)


def _dot_tn(a, b):
    return lax.dot_general(a, b, (((0,), (0,)), ((), ())), preferred_element_type=F32)


def _dot(a, b):
    return jnp.dot(a, b, preferred_element_type=F32)


def _ssd_fwd(cpre, pdt, pz, ya, dtb, alog, dsk_lane, gs, emat, *, nseq, seq, name):
    t, xbc = cpre.shape
    d = ya.shape[1]
    h = d // HEAD_DIM
    npair = h // 2
    ppg = npair // SSM_GROUPS
    nc = seq // CHUNK
    gw = d // SSM_GROUPS
    bc0 = d
    cc0 = d + SSM_GROUPS * D_STATE

    def body(c_ref, pdt_ref, z_ref, ya_ref, dtb_ref, alog_ref, dsk_ref, gs_ref, e_ref, cat_ref, y2_ref, hp_ref, h_ref):
        @pl.when(pl.program_id(0) == 0)
        def _():
            h_ref[...] = jnp.zeros_like(h_ref)

        for sq in range(nseq):
            one_seq(c_ref.at[sq], pdt_ref.at[sq], z_ref.at[sq], ya_ref.at[sq], dtb_ref, alog_ref, dsk_ref, gs_ref, e_ref,
                    cat_ref.at[sq], y2_ref.at[sq], hp_ref.at[sq], h_ref.at[sq])

    def one_seq(c_ref, pdt_ref, z_ref, ya_ref, dtb_ref, alog_ref, dsk_ref, gs_ref, e_ref, cat_ref, y2_ref, hp_ref, h_ref):
        q = _ssd_common(c_ref, pdt_ref, dtb_ref, alog_ref, e_ref, h)
        act, cs, lo, ecs_l = q["act"], q["cs"], q["lo"], q["ecs_l"]
        xs = act[:, :d]
        xd = xs * q["dt_l"]
        ys = []
        for g in range(SSM_GROUPS):
            bg = act[:, bc0 + g * D_STATE: bc0 + (g + 1) * D_STATE]
            cgb = act[:, cc0 + g * D_STATE: cc0 + (g + 1) * D_STATE].astype(BF16)
            s = _dot_nt(cgb, bg.astype(BF16))
            bg_t = bg.T
            for jj in range(ppg):
                j = g * ppg + jj
                sl = slice(LANES * j, LANES * (j + 1))
                xdj = xd[:, sl]
                x2 = jnp.concatenate([jnp.where(lo, xdj, 0.0), jnp.where(lo, 0.0, xdj)], axis=0).astype(BF16)
                hprev = h_ref[j]
                hp_ref[j] = hprev.astype(BF16)
                ms, bws_t = [], []
                for hh in (2 * j, 2 * j + 1):
                    csc = cs[:, hh:hh + 1]
                    cs_row = q["cs_t"][hh:hh + 1, :]
                    seg = jnp.broadcast_to(csc, (CHUNK, CHUNK)) - jnp.broadcast_to(cs_row, (CHUNK, CHUNK))
                    ms.append(s * jnp.exp(jnp.where(q["tril"], seg, -jnp.inf)))
                    bws_t.append(bg_t * jnp.exp(cs_row[:, CHUNK - 1:CHUNK] - cs_row))
                ydiag = _dot(jnp.concatenate(ms, axis=1).astype(BF16), x2)
                st = _dot(jnp.concatenate(bws_t, axis=1).astype(BF16), x2)
                ecs = ecs_l[:, sl]
                yoff = _dot(cgb, hprev.astype(BF16)) * ecs
                h_ref[j] = hprev * ecs[CHUNK - 1:CHUNK] + st
                ys.append(ydiag + yoff)
        y = jnp.concatenate(ys, axis=1) + dsk_ref[...] * xs
        y2_ref[...] = y.astype(BF16)
        zv = z_ref[...].astype(F32)
        y3 = y * (zv * _sigmoid(zv))
        cat_ref[:, 0:d] = ya_ref[...]
        for gi in range(SSM_GROUPS):
            seg = y3[:, gi * gw:(gi + 1) * gw]
            cat_ref[:, d + gi * gw:d + (gi + 1) * gw] = (seg * _rms(seg) * gs_ref[:, gi * gw:(gi + 1) * gw]).astype(BF16)

    chunk = lambda w, cb=0: pl.BlockSpec((nseq, CHUNK, w), lambda c, cb=cb: (0, c, cb))
    vec = lambda w: pl.BlockSpec((1, w), lambda c: (0, 0))
    hp_spec = pl.BlockSpec((nseq, None, npair, D_STATE, LANES), lambda c: (0, c, 0, 0, 0))
    per_seq = lambda a: a.reshape(nseq, seq, a.shape[1])
    cat, y2, hp = _call(
        body, name=name, grid=(nc,),
        in_specs=[chunk(xbc), chunk(LANES), chunk(d, 3), chunk(d), vec(h), vec(h), vec(d), vec(d),
                  pl.BlockSpec((h, d), lambda c: (0, 0))],
        out_specs=[chunk(2 * d), chunk(d), hp_spec],
        out_shape=[jax.ShapeDtypeStruct((nseq, seq, 2 * d), BF16), jax.ShapeDtypeStruct((nseq, seq, d), BF16),
                   jax.ShapeDtypeStruct((nseq, nc, npair, D_STATE, LANES), BF16)],
        scratch_shapes=[pltpu.VMEM((nseq, npair, D_STATE, LANES), F32)],
        compiler_params=_params(("arbitrary",)))(
            per_seq(cpre), per_seq(pdt), per_seq(pz), per_seq(ya), dtb, alog, dsk_lane, gs, emat)
    return cat.reshape(t, 2 * d), y2.reshape(t, d), hp


def _ssd_bwd(cpre, pdt, pz, y2, hprev_all, dcat, dtb, alog, dsk_lane, gs, emat, dproj, *, nseq, seq, name):
    t, xbc = cpre.shape
    d = y2.shape[1]
    h = d // HEAD_DIM
    npair = h // 2
    ppg = npair // SSM_GROUPS
    nc = seq // CHUNK
    gw = d // SSM_GROUPS
    bc0 = d
    cc0 = d + SSM_GROUPS * D_STATE

    def body(c_ref, pdt_ref, z_ref, y2_ref, hp_ref, dys_ref, dtb_ref, alog_ref, dsk_ref, gs_ref, e_ref, dproj_in,
             dconv_ref, dz_ref, dpdt_ref, dgs_ref, ddsk_ref, ddtb_ref, dalog_ref, dh_ref):
        del dproj_in
        b = pl.program_id(0)
        c = pl.program_id(1)

        @pl.when(c == 0)
        def _():
            dh_ref[...] = jnp.zeros_like(dh_ref)

        @pl.when((b == 0) & (c == 0))
        def _():
            dgs_ref[...] = jnp.zeros_like(dgs_ref)
            ddsk_ref[...] = jnp.zeros_like(ddsk_ref)
            ddtb_ref[...] = jnp.zeros_like(ddtb_ref)
            dalog_ref[...] = jnp.zeros_like(dalog_ref)

        q = _ssd_common(c_ref, pdt_ref, dtb_ref, alog_ref, e_ref, h)
        cp, sg, act, cs, a, dt, lo = q["cp"], q["sg"], q["act"], q["cs"], q["a"], q["dt"], q["lo"]
        ecs_l, dt_l = q["ecs_l"], q["dt_l"]
        ev = e_ref[...]
        xs = act[:, :d]
        xd = xs * dt_l
        row16 = lax.broadcasted_iota(jnp.int32, (CHUNK, h), 0)
        hid = lax.broadcasted_iota(jnp.int32, (1, h), 1)
        hid_t = lax.broadcasted_iota(jnp.int32, (h, 1), 0)

        zv = z_ref[...].astype(F32)
        sz = _sigmoid(zv)
        siluz = zv * sz
        y2v = y2_ref[...].astype(F32)
        y3 = y2v * siluz
        dysv = dys_ref[...].astype(F32)
        dy3s = []
        for gi in range(SSM_GROUPS):
            gsl = slice(gi * gw, (gi + 1) * gw)
            seg = y3[:, gsl]
            dseg, dgt = _rms_bwd(seg, _rms(seg), gs_ref[:, gsl], dysv[:, gsl])
            dy3s.append(dseg)
            dgs_ref[:, gsl] += jnp.sum(dgt, axis=0, keepdims=True)
        dy3 = jnp.concatenate(dy3s, axis=1)
        dy = dy3 * siluz
        dz_ref[...] = (dy3 * y2v * (sz * (1.0 + zv * (1.0 - sz)))).astype(BF16)
        ddsk_ref[...] += jnp.sum(_head_sums(dy * xs, ev), axis=0, keepdims=True)

        dcs = jnp.zeros((CHUNK, h), F32)
        dcs_t = jnp.zeros((h, CHUNK), F32)
        dxd_parts, yoff_parts, db_parts, dc_parts = [], [], [], []
        for g in range(SSM_GROUPS):
            bg = act[:, bc0 + g * D_STATE: bc0 + (g + 1) * D_STATE]
            cg = act[:, cc0 + g * D_STATE: cc0 + (g + 1) * D_STATE]
            bgb, cgb = bg.astype(BF16), cg.astype(BF16)
            cgb_t = cg.T.astype(BF16)
            s = _dot_nt(cgb, bgb)
            ds = jnp.zeros((CHUNK, CHUNK), F32)
            dbg = jnp.zeros((CHUNK, D_STATE), F32)
            dcg = jnp.zeros((CHUNK, D_STATE), F32)
            for jj in range(ppg):
                j = g * ppg + jj
                sl = slice(LANES * j, LANES * (j + 1))
                xdj = xd[:, sl]
                xdb = xdj.astype(BF16)
                x2 = jnp.concatenate([jnp.where(lo, xdj, 0.0), jnp.where(lo, 0.0, xdj)], axis=0).astype(BF16)
                dyj = dy[:, sl]
                dy2 = jnp.concatenate([jnp.where(lo, dyj, 0.0), jnp.where(lo, 0.0, dyj)], axis=0).astype(BF16)
                hpb = hp_ref[j]
                hprev = hpb.astype(F32)
                dhn = dh_ref[j]
                dhb = dhn.astype(BF16)
                dh2 = jnp.concatenate([jnp.where(lo, dhn, 0.0), jnp.where(lo, 0.0, dhn)], axis=0).astype(BF16)
                ecs = ecs_l[:, sl]
                gmat = (dyj * ecs).astype(BF16)
                yoff_parts.append(_dot(cgb, hpb) * ecs)
                dcg = dcg + _dot_nt(gmat, hpb)
                dh_ref[j] = dhn * ecs[CHUNK - 1:CHUNK] + _dot(cgb_t, gmat)
                t2 = dhn * hprev
                dbw2 = _dot_nt(x2, dhb)
                dm2 = _dot_nt(dy2, xdb)
                ms, bws = [], []
                for idx, hh in enumerate((2 * j, 2 * j + 1)):
                    msk = lo if idx == 0 else jnp.logical_not(lo)
                    onehot = (hid == hh).astype(F32)
                    csc = cs[:, hh:hh + 1]
                    seg = jnp.broadcast_to(csc, (CHUNK, CHUNK)) - jnp.broadcast_to(q["cs_t"][hh:hh + 1, :], (CHUNK, CHUNK))
                    lm = jnp.exp(jnp.where(q["tril"], seg, -jnp.inf))
                    m = s * lm
                    cs_last = cs[CHUNK - 1:CHUNK, hh:hh + 1]
                    dte = jnp.exp(cs_last - csc)
                    ms.append(m)
                    bws.append(bg * dte)
                    dbw = dbw2[idx * CHUNK:(idx + 1) * CHUNK]
                    dbg = dbg + dbw * dte
                    qv = jnp.sum(dbw * bg, axis=-1, keepdims=True) * dte
                    dm = dm2[idx * CHUNK:(idx + 1) * CHUNK]
                    wm = dm * m
                    rc = jnp.sum(wm, axis=-1, keepdims=True)
                    dcs_t = dcs_t - (hid_t == hh).astype(F32) * jnp.sum(wm, axis=0, keepdims=True)
                    ds = ds + dm * lm
                    ddec = jnp.sum(jnp.where(msk, t2, 0.0)) * jnp.exp(cs_last)
                    last = jnp.sum(qv) + ddec
                    dcs = dcs + (rc - qv) * onehot + jnp.where(row16 == CHUNK - 1, last * onehot, 0.0)
                dxd_s = _dot(jnp.concatenate(bws, axis=1).astype(BF16), dh2)
                dxd_d = _dot_tn(jnp.concatenate(ms, axis=0).astype(BF16), dy2)
                dxd_parts.append(dxd_s + dxd_d)
            dsb = ds.astype(BF16)
            dc_parts.append(dcg + _dot(dsb, bgb))
            db_parts.append(dbg + _dot_tn(dsb, cgb))
        yoff_all = jnp.concatenate(yoff_parts, axis=1)
        dxd_all = jnp.concatenate(dxd_parts, axis=1)
        dcs = dcs + _head_sums(dy * yoff_all, ev)
        triu = (q["col"] >= q["row"]).astype(F32)
        dadt = (jnp.dot(triu, dcs, precision=HIGHEST, preferred_element_type=F32)
                + lax.dot_general(triu, dcs_t, (((1,), (1,)), ((), ())), precision=HIGHEST, preferred_element_type=F32))
        ddt = dadt * a + _head_sums(dxd_all * xs, ev)
        dalog_ref[...] += jnp.sum(dadt * dt, axis=0, keepdims=True) * a
        dpre = ddt * _sigmoid(q["pre"])
        ddtb_ref[...] += jnp.sum(dpre, axis=0, keepdims=True)
        dpdt_ref[...] = jnp.zeros_like(dpdt_ref)
        dpdt_ref[:, 0:h] = dpre.astype(BF16)
        dxs = dxd_all * dt_l + dy * dsk_ref[...]
        dact = jnp.concatenate([dxs] + db_parts + dc_parts, axis=1)
        dconv_ref[...] = (dact * (sg * (1.0 + cp * (1.0 - sg)))).astype(BF16)

    rchunk = lambda w, cb=0: pl.BlockSpec((CHUNK, w), lambda b, c, cb=cb: (b * nc + nc - 1 - c, cb))
    vec = lambda w: pl.BlockSpec((1, w), lambda b, c: (0, 0))
    hp_spec = pl.BlockSpec((None, None, npair, D_STATE, LANES), lambda b, c: (b, nc - 1 - c, 0, 0, 0))
    return _call(body, name=name, grid=(nseq, nc),
                 in_specs=[rchunk(xbc), rchunk(LANES), rchunk(d, 3), rchunk(d), hp_spec, rchunk(d, 1),
                           vec(h), vec(h), vec(d), vec(d), pl.BlockSpec((h, d), lambda b, c: (0, 0)),
                           pl.BlockSpec(memory_space=pl.ANY)],
                 out_specs=[rchunk(xbc), rchunk(d, 3), rchunk(LANES), vec(d), vec(h), vec(h), vec(h)],
                 out_shape=[jax.ShapeDtypeStruct((t, xbc), BF16), jax.ShapeDtypeStruct(dproj.shape, BF16),
                            jax.ShapeDtypeStruct((t, LANES), BF16), jax.ShapeDtypeStruct((1, d), F32),
                            jax.ShapeDtypeStruct((1, h), F32), jax.ShapeDtypeStruct((1, h), F32),
                            jax.ShapeDtypeStruct((1, h), F32)],
                 input_output_aliases={11: 1},
                 scratch_shapes=[pltpu.VMEM((npair, D_STATE, LANES), F32)],
                 compiler_params=_params(("arbitrary", "arbitrary")))(
                     cpre, pdt, pz, y2, hprev_all, dcat, dtb, alog, dsk_lane, gs, emat, dproj)


def _sum_adamw(parts, w, m, v, *, name, layer=None, outs=None):
    n, r, c = parts.shape
    tr = _pick_rows(r, 256)
    bc1 = 1.0 - ADAM_B1 ** ADAM_STEP
    bc2 = 1.0 - ADAM_B2 ** ADAM_STEP

    def body(p_ref, w_ref, m_ref, v_ref, *rest):
        g_ref, d_ref, mo_ref, vo_ref = rest[-4:]
        g = p_ref[0].astype(F32)
        for k in range(1, n):
            g = g + p_ref[k].astype(F32)
        mn = ADAM_B1 * m_ref[...] + (1.0 - ADAM_B1) * g
        vn = ADAM_B2 * v_ref[...] + (1.0 - ADAM_B2) * (g * g)
        g_ref[...] = g
        mo_ref[...] = mn
        vo_ref[...] = vn
        d_ref[...] = -ADAM_LR * ((mn / bc1) / (jnp.sqrt(vn / bc2) + ADAM_EPS) + ADAM_WD * w_ref[...])

    p_spec = pl.BlockSpec((n, tr, c), lambda i: (0, i, 0))
    if layer is None:
        blk = pl.BlockSpec((tr, c), lambda i: (i, 0))
        return _call(body, name=name, grid=(r // tr,), in_specs=[p_spec, blk, blk, blk], out_specs=[blk] * 4,
                     out_shape=[jax.ShapeDtypeStruct((r, c), F32)] * 4,
                     compiler_params=_params(("parallel",)))(parts, w, m, v)
    blk = pl.BlockSpec((None, tr, c), lambda i: (layer, i, 0))
    if outs is None:
        outs = [lax.empty(w.shape, F32) for _ in range(4)]
    return _call(body, name=name, grid=(r // tr,),
                 in_specs=[p_spec, blk, blk, blk] + [pl.BlockSpec(memory_space=pl.ANY)] * 4, out_specs=[blk] * 4,
                 out_shape=[jax.ShapeDtypeStruct(w.shape, F32)] * 4, input_output_aliases={4 + k: k for k in range(4)},
                 compiler_params=_params(("parallel",)))(parts, w, m, v, *outs)


def _assemble_cols(blocks, *, name):
    nb, r, c = blocks.shape
    width = -(-nb * c // LANES) * LANES
    tr = _pick_rows(r, 256)

    def body(b_ref, o_ref):
        pieces = [b_ref[j] for j in range(nb)]
        if width > nb * c:
            pieces.append(jnp.zeros((tr, width - nb * c), blocks.dtype))
        o_ref[...] = jnp.concatenate(pieces, axis=1)

    return _call(body, name=name, grid=(r // tr,), in_specs=[pl.BlockSpec((nb, tr, c), lambda i: (0, i, 0))],
                 out_specs=pl.BlockSpec((tr, width), lambda i: (i, 0)), out_shape=jax.ShapeDtypeStruct((r, width), blocks.dtype),
                 compiler_params=_params(("parallel",)))(blocks)


def _split_cols(pieces, c, *, name):
    r = pieces[0].shape[0]
    tr = _pick_rows(r, 256)
    n_in = len(pieces)

    def body(*refs):
        o_ref = refs[n_in]
        x = jnp.concatenate([p[...] for p in refs[:n_in]], axis=1) if n_in > 1 else refs[0][...]
        for j in range(N_DEV):
            o_ref[j] = x[:, c * j:c * (j + 1)]

    return _call(body, name=name, grid=(r // tr,),
                 in_specs=[pl.BlockSpec((tr, p.shape[1]), lambda i: (i, 0)) for p in pieces],
                 out_specs=pl.BlockSpec((N_DEV, tr, c), lambda i: (0, i, 0)),
                 out_shape=jax.ShapeDtypeStruct((N_DEV, r, c), pieces[0].dtype),
                 compiler_params=_params(("parallel",)))(*pieces)


def _sum_parts(parts, *, name):
    n, r, c = parts.shape
    tr = _pick_rows(r, 256)

    def body(p_ref, g_ref):
        g = p_ref[0].astype(F32)
        for k in range(1, n):
            g = g + p_ref[k].astype(F32)
        g_ref[...] = g

    return _call(body, name=name, grid=(r // tr,), in_specs=[pl.BlockSpec((n, tr, c), lambda i: (0, i, 0))],
                 out_specs=pl.BlockSpec((tr, c), lambda i: (i, 0)), out_shape=jax.ShapeDtypeStruct((r, c), F32),
                 compiler_params=_params(("parallel",)))(parts)


def _peers():
    x, y, c = lax.axis_index("x"), lax.axis_index("y"), lax.axis_index("c")
    me = 4 * x + 2 * y + c
    out = []
    for k in range(1, N_DEV):
        px = (1 - x) if (k >> 2) & 1 else x
        py = (1 - y) if (k >> 1) & 1 else y
        pc = (1 - c) if k & 1 else c
        out.append(((px, py, pc), 4 * px + 2 * py + pc))
    return me, out


_HBM = pl.BlockSpec(memory_space=pltpu.HBM)
_SEM = pl.BlockSpec(memory_space=pltpu.SEMAPHORE)
_EFFECT = pltpu.SideEffectType.DATAFLOW_SIDE_EFFECTING


ALL_PEERS = tuple(range(1, N_DEV))
SAME_CORE_PEERS = (2, 4, 6)


def _slot(ref, j, c):
    if c is None:
        return ref.at[j]
    start = j * c
    return ref.at[:, pl.ds(start if isinstance(start, int) else pl.multiple_of(start, c), c)]


def _split_copies(s_refs, l_refs, send_sems, recv_sems, gather, incoming, ks, src_cols, land_cols):
    me, peers = _peers()
    local, remote = [], []
    for ti, (s_ref, l_ref) in enumerate(zip(s_refs, l_refs)):
        base = ti * N_DEV
        sc, lc = src_cols[ti], land_cols[ti]
        local.append(pltpu.make_async_copy(s_ref if gather else _slot(s_ref, me, sc), _slot(l_ref, me, lc),
                                           recv_sems.at[base + N_DEV - 1]))
        for k, (dev, pid) in enumerate(peers):
            if k + 1 not in ks:
                continue
            sems = dict(send_sem=send_sems.at[base + k], recv_sem=recv_sems.at[base + k], device_id=dev, device_id_type=MESH)
            src = s_ref if gather else _slot(s_ref, pid, sc)
            remote.append((
                pltpu.make_async_remote_copy(src_ref=src, dst_ref=_slot(l_ref, me, lc), **sems),
                pltpu.make_async_remote_copy(src_ref=src, dst_ref=_slot(l_ref, pid, lc), **sems) if incoming else None))
    return local, remote


def _exchange_start(srcs, *, gather, name, after=(), ks=ALL_PEERS, src_cols=None, land_cols=None):
    n = len(srcs)
    after = list(after)
    src_cols = list(src_cols or [None] * n)
    land_cols = list(land_cols or [None] * n)
    srcs = [pltpu.with_memory_space_constraint(s, pltpu.HBM) for s in srcs]

    def land_shape(s, sc, lc):
        block = tuple(s.shape) if gather else ((s.shape[0], sc) if sc else tuple(s.shape[1:]))
        return (block[0], N_DEV * lc) if lc else (N_DEV,) + block

    lands = [pltpu.with_memory_space_constraint(lax.empty(land_shape(s, sc, lc), s.dtype), pltpu.HBM)
             for s, sc, lc in zip(srcs, src_cols, land_cols)]

    def body(*refs):
        s_refs, l_refs = refs[:n], refs[n:2 * n]
        outs = refs[2 * n + len(after):]
        send_sems, recv_sems, token = outs[0], outs[1], outs[-1]
        local, remote = _split_copies(s_refs, l_refs, send_sems, recv_sems, gather, False, ks, src_cols, land_cols)
        for cp in local:
            cp.start()
        for out_cp, _ in remote:
            out_cp.start()
        token[...] = jnp.zeros_like(token)

    outs = _call(
        body, name=name,
        out_shape=(pltpu.SemaphoreType.DMA((n * N_DEV,)), pltpu.SemaphoreType.DMA((n * N_DEV,)),
                   *[pltpu.HBM(s.shape, s.dtype) for s in srcs], *[pltpu.HBM(l.shape, l.dtype) for l in lands],
                   jax.ShapeDtypeStruct((SUBLANES, LANES), F32)),
        in_specs=[_HBM] * (2 * n) + [pl.BlockSpec(memory_space=pl.ANY)] * len(after),
        out_specs=(_SEM, _SEM, *[_HBM] * (2 * n), pl.BlockSpec(memory_space=pltpu.VMEM)),
        input_output_aliases={k: k + 2 for k in range(2 * n)},
        compiler_params=pltpu.CompilerParams(has_side_effects=_EFFECT),
    )(*srcs, *lands, *after)
    return dict(n=n, gather=gather, ks=ks, src_cols=src_cols, land_cols=land_cols, sems=outs[:2], srcs=outs[2:2 + n],
                lands=outs[2 + n:2 + 2 * n]), outs[-1]


def _exchange_wait(state, after, *, name):
    n, gather, ks = state["n"], state["gather"], state["ks"]
    after = list(after)

    def body(*refs):
        s_refs, l_refs = refs[:n], refs[n:2 * n]
        send_sems, recv_sems = refs[2 * n], refs[2 * n + 1]
        local, remote = _split_copies(s_refs, l_refs, send_sems, recv_sems, gather, True, ks, state["src_cols"],
                                      state["land_cols"])
        for out_cp, in_cp in remote:
            out_cp.wait_send()
            in_cp.wait_recv()
        for cp in local:
            cp.wait()

    outs = _call(
        body, name=name,
        out_shape=tuple(pltpu.HBM(a.shape, a.dtype) for a in (*state["srcs"], *state["lands"])),
        in_specs=[_HBM] * (2 * n) + [_SEM, _SEM] + [pl.BlockSpec(memory_space=pl.ANY)] * len(after),
        out_specs=tuple([_HBM] * (2 * n)),
        input_output_aliases={k: k for k in range(2 * n)},
        compiler_params=pltpu.CompilerParams(has_side_effects=_EFFECT),
    )(*state["srcs"], *state["lands"], *state["sems"], *after)
    return outs[n:]


def _sibling_copies(l_refs, send_sems, recv_sems, incoming, land_cols):
    x, y, c = lax.axis_index("x"), lax.axis_index("y"), lax.axis_index("c")
    out = []
    for ti, l_ref in enumerate(l_refs):
        for q in range(4):
            px = (1 - x) if q & 2 else x
            py = (1 - y) if q & 1 else y
            mine = _slot(l_ref, 4 * px + 2 * py + c, land_cols[ti])
            theirs = _slot(l_ref, 4 * px + 2 * py + (1 - c), land_cols[ti])
            sems = dict(send_sem=send_sems.at[4 * ti + q], recv_sem=recv_sems.at[4 * ti + q],
                        device_id=(x, y, 1 - c), device_id_type=MESH)
            out.append((
                pltpu.make_async_remote_copy(src_ref=mine, dst_ref=mine, **sems),
                pltpu.make_async_remote_copy(src_ref=mine, dst_ref=theirs, **sems) if incoming else None))
    return out


def _sibling_start(lands, *, name, after=(), land_cols=None):
    n = len(lands)
    after = list(after)
    land_cols = list(land_cols or [None] * n)
    lands = [pltpu.with_memory_space_constraint(l, pltpu.HBM) for l in lands]

    def body(*refs):
        l_refs = refs[:n]
        outs = refs[n + len(after):]
        for out_cp, _ in _sibling_copies(l_refs, outs[0], outs[1], False, land_cols):
            out_cp.start()
        outs[-1][...] = jnp.zeros_like(outs[-1])

    outs = _call(
        body, name=name,
        out_shape=(pltpu.SemaphoreType.DMA((4 * n,)), pltpu.SemaphoreType.DMA((4 * n,)),
                   *[pltpu.HBM(l.shape, l.dtype) for l in lands], jax.ShapeDtypeStruct((SUBLANES, LANES), F32)),
        in_specs=[_HBM] * n + [pl.BlockSpec(memory_space=pl.ANY)] * len(after),
        out_specs=(_SEM, _SEM, *[_HBM] * n, pl.BlockSpec(memory_space=pltpu.VMEM)),
        input_output_aliases={k: k + 2 for k in range(n)},
        compiler_params=pltpu.CompilerParams(has_side_effects=_EFFECT),
    )(*lands, *after)
    return dict(n=n, land_cols=land_cols, sems=outs[:2], lands=outs[2:2 + n]), outs[-1]


def _sibling_wait(state, after, *, name):
    n = state["n"]
    after = list(after)

    def body(*refs):
        l_refs = refs[:n]
        for out_cp, in_cp in _sibling_copies(l_refs, refs[n], refs[n + 1], True, state["land_cols"]):
            out_cp.wait_send()
            in_cp.wait_recv()

    return _call(
        body, name=name,
        out_shape=tuple(pltpu.HBM(a.shape, a.dtype) for a in state["lands"]),
        in_specs=[_HBM] * n + [_SEM, _SEM] + [pl.BlockSpec(memory_space=pl.ANY)] * len(after),
        out_specs=tuple([_HBM] * n), input_output_aliases={k: k for k in range(n)},
        compiler_params=pltpu.CompilerParams(has_side_effects=_EFFECT),
    )(*state["lands"], *state["sems"], *after)


def _pack(arrs):
    flat = jnp.concatenate([a.reshape(-1).astype(F32) for a in arrs])
    pad = (-flat.shape[0]) % (SUBLANES * LANES)
    return jnp.pad(flat, (0, pad)).reshape(-1, LANES)


def _unpack(packed, shapes):
    flat = packed.reshape(-1)
    out, off = [], 0
    for s in shapes:
        n = 1
        for v in s:
            n *= v
        out.append(flat[off:off + n].reshape(s))
        off += n
    return out


SMALL = ("norm_mix_pre", "ssm_conv_b", "dt_bias", "a_log", "d_skip", "conv_out_norm", "ssm_out_norm",
         "norm_mix_post", "norm_mlp_pre", "norm_mlp_post", "conv_a_w", "ssm_conv_w")
BIG = ("w_in", "w_out", "w_up", "w_down")
ORDER = ("norm_mix_pre", "w_in", "conv_a_w", "ssm_conv_w", "ssm_conv_b", "dt_bias", "a_log", "d_skip",
         "conv_out_norm", "ssm_out_norm", "w_out", "norm_mix_post", "norm_mlp_pre", "w_up", "w_down", "norm_mlp_post")


def kernel(x, norm_mix_pre, w_in, conv_a_w, ssm_conv_w, ssm_conv_b, dt_bias, a_log, d_skip, conv_out_norm, ssm_out_norm, w_out, norm_mix_post, norm_mlp_pre, w_up, w_down, norm_mlp_post, loss_target, m_norm_mix_pre, m_w_in, m_conv_a_w, m_ssm_conv_w, m_ssm_conv_b, m_dt_bias, m_a_log, m_d_skip, m_conv_out_norm, m_ssm_out_norm, m_w_out, m_norm_mix_post, m_norm_mlp_pre, m_w_up, m_w_down, m_norm_mlp_post, v_norm_mix_pre, v_w_in, v_conv_a_w, v_ssm_conv_w, v_ssm_conv_b, v_dt_bias, v_a_log, v_d_skip, v_conv_out_norm, v_ssm_out_norm, v_w_out, v_norm_mix_post, v_norm_mlp_pre, v_w_up, v_w_down, v_norm_mlp_post):
    W = dict(norm_mix_pre=norm_mix_pre, w_in=w_in, conv_a_w=conv_a_w, ssm_conv_w=ssm_conv_w, ssm_conv_b=ssm_conv_b,
             dt_bias=dt_bias, a_log=a_log, d_skip=d_skip, conv_out_norm=conv_out_norm, ssm_out_norm=ssm_out_norm,
             w_out=w_out, norm_mix_post=norm_mix_post, norm_mlp_pre=norm_mlp_pre, w_up=w_up, w_down=w_down,
             norm_mlp_post=norm_mlp_post)
    M = dict(norm_mix_pre=m_norm_mix_pre, w_in=m_w_in, conv_a_w=m_conv_a_w, ssm_conv_w=m_ssm_conv_w,
             ssm_conv_b=m_ssm_conv_b, dt_bias=m_dt_bias, a_log=m_a_log, d_skip=m_d_skip,
             conv_out_norm=m_conv_out_norm, ssm_out_norm=m_ssm_out_norm, w_out=m_w_out,
             norm_mix_post=m_norm_mix_post, norm_mlp_pre=m_norm_mlp_pre, w_up=m_w_up, w_down=m_w_down,
             norm_mlp_post=m_norm_mlp_post)
    V = dict(norm_mix_pre=v_norm_mix_pre, w_in=v_w_in, conv_a_w=v_conv_a_w, ssm_conv_w=v_ssm_conv_w,
             ssm_conv_b=v_ssm_conv_b, dt_bias=v_dt_bias, a_log=v_a_log, d_skip=v_d_skip,
             conv_out_norm=v_conv_out_norm, ssm_out_norm=v_ssm_out_norm, w_out=v_w_out,
             norm_mix_post=v_norm_mix_post, norm_mlp_pre=v_norm_mlp_pre, w_up=v_w_up, w_down=v_w_down,
             norm_mlp_post=v_norm_mlp_post)

    nseq, seq, d = x.shape
    t = nseq * seq
    depth = w_in.shape[0]
    h = d // HEAD_DIM
    xbc = d + 2 * SSM_GROUPS * D_STATE
    in_cols = w_in.shape[2] * N_DEV
    d_mix = w_out.shape[1] * N_DEV
    d_ff = w_up.shape[2] * N_DEV
    me = 4 * lax.axis_index("x") + 2 * lax.axis_index("y") + lax.axis_index("c")
    ca_shard = conv_a_w.shape[2]
    sc_shard = ssm_conv_w.shape[2]

    tap_shapes = [conv_a_w.shape[1:], ssm_conv_w.shape[1:]]

    def gather_start(i, after=()):
        ks = SAME_CORE_PEERS
        st_in, tok_in = _exchange_start([w_in[i].astype(BF16), _pack([conv_a_w[i], ssm_conv_w[i]])], gather=True,
                                        name=f"gather_start_in_{i}", after=after, ks=ks)
        st_rest, tok_rest = _exchange_start([W[n][i].astype(BF16) for n in ("w_out", "w_up", "w_down")], gather=True,
                                            name=f"gather_start_rest_{i}", after=[tok_in], ks=ks, land_cols=rest_cols)
        return st_in, st_rest, tok_rest

    rest_cols = [None, d_ff // N_DEV, None]

    vec = lambda name, i: W[name][i].reshape(1, -1)
    emat = (lax.broadcasted_iota(jnp.int32, (h, d), 1) // HEAD_DIM == lax.broadcasted_iota(jnp.int32, (h, d), 0)).astype(F32)

    xcur = x.reshape(t, d)
    hcur = _norm_fwd(xcur, vec("norm_mix_pre", 0), name="norm_first")
    saved = []
    nxt = gather_start(0)
    sib_in = None
    for i in range(depth):
        st_in, st_rest, tok = nxt
        if sib_in is None:
            sib_in, _ = _sibling_start(_exchange_wait(st_in, [hcur, tok], name=f"gather_wait_in_{i}"),
                                       name=f"gather_sibling_start_in_{i}")
        win_g, taps_g = _sibling_wait(sib_in, [hcur], name=f"gather_sibling_wait_in_{i}")
        win = _assemble_cols(win_g, name=f"assemble_w_in_{i}")
        taps_j = [_unpack(taps_g[j], tap_shapes) for j in range(N_DEV)]
        conv_a_i = jnp.concatenate([tj[0] for tj in taps_j], axis=1)
        ssm_conv_i = jnp.concatenate([tj[1] for tj in taps_j], axis=1)
        proj = _mm(hcur, win, n=4 * d + xbc, name=f"fwd_proj_{i}", out_dtypes=(BF16,))
        pdt = _mm(hcur, win, n=LANES, b_off=4 * d + xbc, name=f"fwd_proj_dt_{i}")
        ya, va = _conva_fwd(proj, conv_a_i, vec("conv_out_norm", i), d=d, seq=seq, name=f"fwd_conv_a_{i}")
        cpre = _convb_fwd(proj, ssm_conv_i, vec("ssm_conv_b", i), col0=4 * d, seq=seq, name=f"fwd_conv_b_{i}")
        dsk_lane = jnp.repeat(W["d_skip"][i], HEAD_DIM).reshape(1, d)
        st_sib, tok_sib = _sibling_start(_exchange_wait(st_rest, [cpre], name=f"gather_wait_rest_{i}"),
                                         name=f"gather_sibling_start_rest_{i}", land_cols=rest_cols)
        cat, y2, hprev = _ssd_fwd(cpre, pdt, proj, ya, vec("dt_bias", i) + tok_sib[0:1, 0:1], vec("a_log", i), dsk_lane,
                                  vec("ssm_out_norm", i), emat, nseq=nseq, seq=seq, name=f"fwd_ssd_{i}")
        wout_g, wup_g, wdown_g = _sibling_wait(st_sib, [cat], name=f"gather_sibling_wait_rest_{i}")
        lw = dict(win=win, wout=wout_g.reshape(d_mix, d),
                  wup=wup_g, wdown=wdown_g.reshape(d_ff, d),
                  conv_a=conv_a_i, ssm_conv=ssm_conv_i)
        after = []
        if i + 1 < depth:
            nxt = gather_start(i + 1, after=[wout_g])
            after = [nxt[2]]
        x1, h2, mix = _mm(cat, lw["wout"], name=f"fwd_out_{i}", after=after, out_dtypes=(F32, BF16, BF16),
                          epi=_epi_resid_norm, extras=(xcur,), vecs=(vec("norm_mix_post", i), vec("norm_mlp_pre", i)),
                          tm_cap=FUSED_ROWS)
        f = _mm(h2, lw["wup"], name=f"fwd_up_{i}", out_dtypes=(BF16,), epi=_epi_relu2)
        g_next = vec("norm_mix_pre", i + 1) if i + 1 < depth else None
        after = []
        if i + 1 < depth:
            sib_in, tok_in = _sibling_start(_exchange_wait(nxt[0], [f], name=f"gather_wait_in_{i + 1}"),
                                            name=f"gather_sibling_start_in_{i + 1}")
            after = [tok_in]
        dn = _mm(f, lw["wdown"], name=f"fwd_down_{i}", out_dtypes=(BF16,), after=after)
        x2, hnext = _resid_norm(x1, dn, vec("norm_mlp_post", i), g_next, name=f"fwd_post_mlp_{i}")
        saved.append(dict(lw=lw, x0=xcur, h=hcur, proj=proj, pdt=pdt, va=va, cpre=cpre, y2=y2,
                          hprev=hprev, cat=cat, mix=mix, x1=x1, h2=h2, f=f, dn=dn, dsk_lane=dsk_lane))
        xcur, hcur = x2, hnext

    dx, loss_part = _loss_fwd_bwd(xcur, loss_target.reshape(t, d), name="loss")
    loss = lax.psum(loss_part[0, 0], ("x", "y", "c"))

    small_grads = {n: [None] * depth for n in SMALL}
    big_out = {n: None for n in BIG}

    def finish(pending, after):
        li, st_a, st_b = pending

        def update(n, parts):
            big_out[n] = _sum_adamw(parts, W[n], M[n], V[n], layer=li, outs=big_out[n], name=f"adamw_{n}_{li}")

        p_down, p_up, p_out = _exchange_wait(st_a, after, name=f"scatter_wait_a_{li}")
        update("w_down", p_down)
        update("w_up", p_up)
        update("w_out", p_out)
        p_in, = _exchange_wait(st_b, after + [big_out["w_out"][0]], name=f"scatter_wait_b_{li}")
        update("w_in", p_in)

    pending = None
    for i in reversed(range(depth)):
        s = saved[i]
        lw = s["lw"]
        if i == depth - 1:
            ddn, dg = _bwd_norm_out(s["dn"], vec("norm_mlp_post", i), dx, name=f"bwd_norm_mlp_post_{i}")
            small_grads["norm_mlp_post"][i] = dg
        dup = _mm(ddn, lw["wdown"], tb=True, name=f"bwd_down_dx_{i}", out_dtypes=(BF16,), epi=_epi_drelu2,
                  extras=(s["f"],))
        g_wdown = _mm(s["f"], ddn, ta=True, name=f"bwd_down_dw_{i}", out_dtypes=(BF16,))
        dh2 = _mm(dup, lw["wup"], tb=True, name=f"bwd_up_dx_{i}", out_dtypes=(BF16,))
        g_wup = _mm(s["h2"], dup, ta=True, name=f"bwd_up_dw_{i}", out_dtypes=(BF16,))
        dx1, dmix, dg_pre, dg_post = _bwd_norm_pair(s["x1"], [dh2], dx, s["mix"], vec("norm_mlp_pre", i),
                                                    vec("norm_mix_post", i), name=f"bwd_norm_mix_post_{i}")
        small_grads["norm_mlp_pre"][i] = dg_pre
        small_grads["norm_mix_post"][i] = dg_post
        dcat = _mm(dmix, lw["wout"], tb=True, name=f"bwd_out_dx_{i}", out_dtypes=(BF16,))
        g_wout = _mm(s["cat"], dmix, ta=True, name=f"bwd_out_dw_{i}", out_dtypes=(BF16,))
        st_a, tok_a = _exchange_start(
            [g_wdown.reshape(N_DEV, d_ff // N_DEV, d), g_wup, g_wout.reshape(N_DEV, d_mix // N_DEV, d)],
            gather=False, name=f"scatter_start_a_{i}", src_cols=[None, d_ff // N_DEV, None])
        dproj, dcaw, dgca = _conva_bwd(dcat, s["proj"], s["va"], lw["conv_a"],
                                       vec("conv_out_norm", i) + tok_a[0:1, 0:1], d=d, seq=seq, name=f"bwd_conv_a_{i}")
        small_grads["conv_a_w"][i] = dcaw
        small_grads["conv_out_norm"][i] = dgca
        dconv, dproj, dpdt, dgs, ddsk, ddtb, dalog = _ssd_bwd(
            s["cpre"], s["pdt"], s["proj"], s["y2"], s["hprev"], dcat, vec("dt_bias", i), vec("a_log", i),
            s["dsk_lane"], vec("ssm_out_norm", i), emat, dproj, nseq=nseq, seq=seq, name=f"bwd_ssd_{i}")
        small_grads["ssm_out_norm"][i] = dgs
        small_grads["d_skip"][i] = ddsk
        small_grads["dt_bias"][i] = ddtb
        small_grads["a_log"][i] = dalog
        dproj, dscw, dscb = _convb_bwd(dconv, s["proj"], lw["ssm_conv"], dproj, col0=4 * d, seq=seq,
                                       name=f"bwd_conv_b_{i}")
        small_grads["ssm_conv_w"][i] = dscw
        small_grads["ssm_conv_b"][i] = dscb
        g_win = _split_cols([
            _mm(s["h"], dproj, ta=True, name=f"bwd_proj_dw_{i}", out_dtypes=(BF16,)),
            _mm(s["h"], dpdt, ta=True, name=f"bwd_proj_dt_dw_{i}", out_dtypes=(BF16,))],
            in_cols // N_DEV, name=f"split_g_w_in_{i}")
        st_b, tok_b = _exchange_start([g_win], gather=False, name=f"scatter_start_b_{i}")
        dh_parts = [_mm(dp, lw["win"], tb=True, b_koff=off, name=f"bwd_proj_{nm}dx_{i}", after=[tok_b], out_dtypes=(BF16,))
                    for nm, dp, off in (("", dproj, 0), ("dt_", dpdt, 4 * d + xbc))]
        if i > 0:
            dx, ddn, dg_in, dg_below = _bwd_norm_pair(s["x0"], dh_parts, dx1, saved[i - 1]["dn"], vec("norm_mix_pre", i),
                                                      vec("norm_mlp_post", i - 1), name=f"bwd_norm_mix_pre_{i}")
            small_grads["norm_mlp_post"][i - 1] = dg_below
        else:
            dx, dg_in = _bwd_norm_in(s["x0"], dh_parts, dx1, vec("norm_mix_pre", i), name=f"bwd_norm_mix_pre_{i}")
        small_grads["norm_mix_pre"][i] = dg_in
        if pending is not None:
            finish(pending, [dx])
        pending = (i, st_a, st_b)

    grad_x = dx.reshape(nseq, seq, d)

    small_shapes_full = {n: (depth,) + tuple(small_grads[n][0].shape) for n in SMALL}
    gpack = _pack([jnp.stack(small_grads[n]) for n in SMALL])
    st_small, tok_small = _exchange_start([gpack], gather=True, name="allreduce_small_start")
    finish(pending, [dx, tok_small])
    gparts, = _exchange_wait(st_small, [big_out["w_in"][0]], name="allreduce_small_wait")

    def shard_of(n, full):
        if n == "conv_a_w":
            return lax.dynamic_slice_in_dim(full, me * ca_shard, ca_shard, axis=2)
        if n == "ssm_conv_w":
            return lax.dynamic_slice_in_dim(full, me * sc_shard, sc_shard, axis=2)
        return full.reshape(W[n].shape)

    gsum = _sum_parts(gparts, name="sum_small")
    gfull = _unpack(gsum, [small_shapes_full[n] for n in SMALL])
    gsmall = {n: shard_of(n, gf) for n, gf in zip(SMALL, gfull)}
    res = _sum_adamw(_pack([gsmall[n] for n in SMALL])[None], _pack([W[n] for n in SMALL]),
                     _pack([M[n] for n in SMALL]), _pack([V[n] for n in SMALL]), name="adamw_small")
    small_out = [dict(zip(SMALL, _unpack(r, [W[n].shape for n in SMALL]))) for r in res]

    def out_of(kind, n):
        return big_out[n][kind] if n in BIG else small_out[kind][n]

    return (loss, grad_x, *[out_of(k, n) for k in range(4) for n in ORDER])
```

```python
import jax
import jax.numpy as jnp
from jax import lax
from jax.experimental import pallas as pl
from jax.experimental.pallas import tpu as pltpu

F32 = jnp.float32
BF16 = jnp.bfloat16
HIGHEST = lax.Precision.HIGHEST
MESH = pl.DeviceIdType.MESH

EPS = 1e-6
HEAD_DIM = 64
D_STATE = 128
SSM_GROUPS = 2
CHUNK = 128
CONV_K = 3
SSM_CONV_K = 4
ADAM_LR = 0.001
ADAM_B1 = 0.9
ADAM_B2 = 0.999
ADAM_EPS = 1e-08
ADAM_WD = 0.01
ADAM_STEP = 10

N_DEV = 8
LANES = 128
SUBLANES = 8
VMEM_LIMIT = 48 * 1024 * 1024
ROW_TILE = 512
MM_TILE = 1024
MM_TILE_N = 2816
MM_VMEM_BUDGET = 40 * 1024 * 1024
FUSED_ROWS = 512


def _params(sem):
    return pltpu.CompilerParams(dimension_semantics=sem, vmem_limit_bytes=VMEM_LIMIT)


def _call(body, **kw):
    return pl.pallas_call(body, **kw)


def _pick(n, cap):
    best = None
    for t in range(LANES, min(n, cap) + 1, LANES):
        if n % t == 0:
            best = t
    return best or n


def _pick_rows(n, cap):
    best = None
    for t in range(SUBLANES, min(n, cap) + 1, SUBLANES):
        if n % t == 0:
            best = t
    return best or n


def _sigmoid(x):
    return 1.0 / (1.0 + jnp.exp(-x))


def _softplus(x):
    return jnp.maximum(x, 0.0) + jnp.log1p(jnp.exp(-jnp.abs(x)))


def _rms(x):
    return lax.rsqrt(jnp.mean(x * x, axis=-1, keepdims=True) + EPS)


def _rms_bwd(x, r, g, dy):
    gy = dy * g
    dx = r * gy - x * (r * r * r) * jnp.mean(gy * x, axis=-1, keepdims=True)
    return dx, dy * x * r


def _full(shape):
    return pl.BlockSpec(shape, lambda *_: (0,) * len(shape))


def _mm(a, b, *, name, ta=False, tb=False, out_dtypes=(F32,), epi=None, extras=(), n=None, b_off=0, b_koff=0,
        after=(), vecs=(), tm_cap=MM_TILE):
    m, k = (a.shape[1], a.shape[0]) if ta else a.shape
    if n is None:
        n = b.shape[0] if tb else b.shape[1]
    tm, tn, tk = _pick(m, tm_cap), _pick(n, MM_TILE_N), _pick(k, MM_TILE)
    while b_off % tn or n % tn:
        tn -= LANES
    if b_koff == 0 and k > MM_TILE:
        tk = _pick(k, MM_TILE_N)
    while b_koff % tk or k % tk:
        tk -= LANES

    def vmem_bytes(tk_):
        per_out = sum(jnp.dtype(dt).itemsize for dt in out_dtypes) + sum(e.dtype.itemsize for e in extras)
        return 2 * tk_ * (tm * a.dtype.itemsize + tn * b.dtype.itemsize) + tm * tn * (2 * per_out + 4)

    while vmem_bytes(tk) > MM_VMEM_BUDGET and tk % (2 * LANES) == 0 and not b_koff % (tk // 2):
        tk //= 2
    nk = k // tk
    nm, nn = m // tm, n // tn
    jo = b_off // tn
    ko = b_koff // tk
    a_bytes = m * k * a.dtype.itemsize
    b_bytes = n * k * b.dtype.itemsize
    m_outer = a_bytes + nm * b_bytes <= b_bytes + nn * a_bytes
    ij = (lambda g0, g1: (g0, g1)) if m_outer else (lambda g0, g1: (g1, g0))
    grid = (nm, nn, nk) if m_outer else (nn, nm, nk)

    def a_map(g0, g1, kk):
        i, _ = ij(g0, g1)
        return (kk, i) if ta else (i, kk)

    def b_map(g0, g1, kk):
        _, j = ij(g0, g1)
        return (j + jo, kk + ko) if tb else (kk + ko, j + jo)

    def o_map(g0, g1, kk):
        return ij(g0, g1)

    a_spec = pl.BlockSpec((tk, tm) if ta else (tm, tk), a_map)
    b_spec = pl.BlockSpec((tn, tk) if tb else (tk, tn), b_map)
    o_spec = pl.BlockSpec((tm, tn), o_map)
    dims = (((0 if ta else 1,), (1 if tb else 0,)), ((), ()))
    n_ex = len(extras) + len(vecs)
    after = list(after)
    o0 = 2 + n_ex + len(after)

    def finish(acc, ex, outs):
        res = (acc,) if epi is None else epi(acc, *[e[...] for e in ex])
        for o, r in zip(outs, res):
            o[...] = r.astype(o.dtype)

    def body_single(*refs):
        a_ref, b_ref = refs[:2]
        acc = lax.dot_general(a_ref[...].astype(BF16), b_ref[...].astype(BF16), dims, preferred_element_type=F32)
        finish(acc, refs[2:2 + n_ex], refs[o0:])

    def body_multi(*refs):
        a_ref, b_ref = refs[:2]
        acc = refs[-1]
        kk = pl.program_id(2)

        @pl.when(kk == 0)
        def _():
            acc[...] = jnp.zeros_like(acc)

        acc[...] += lax.dot_general(a_ref[...].astype(BF16), b_ref[...].astype(BF16), dims, preferred_element_type=F32)

        @pl.when(kk == nk - 1)
        def _():
            finish(acc[...], refs[2:2 + n_ex], refs[o0:-1])

    v_spec = pl.BlockSpec((1, tn), lambda g0, g1, kk: (0, ij(g0, g1)[1]))
    outs = _call(
        body_single if nk == 1 else body_multi, name=name, grid=grid,
        in_specs=([a_spec, b_spec] + [o_spec] * len(extras) + [v_spec] * len(vecs)
                  + [pl.BlockSpec(memory_space=pl.ANY)] * len(after)),
        out_specs=[o_spec] * len(out_dtypes),
        out_shape=[jax.ShapeDtypeStruct((m, n), dt) for dt in out_dtypes],
        scratch_shapes=[] if nk == 1 else [pltpu.VMEM((tm, tn), F32)],
        compiler_params=_params(("parallel", "parallel", "arbitrary")),
    )(a, b, *extras, *vecs, *after)
    return outs[0] if len(outs) == 1 else outs


def _epi_resid_norm(acc, x, g_res, g_next):
    xn = x + acc * _rms(acc) * g_res
    return xn, xn * _rms(xn) * g_next, acc


def _epi_relu2(acc):
    r = jnp.maximum(acc, 0.0)
    return (r * r,)


def _epi_drelu2(acc, f):
    return (acc * (2.0 * jnp.sqrt(f).astype(F32)),)


def _norm_fwd(x, g, *, name):
    t, d = x.shape
    tt = _pick_rows(t, ROW_TILE)

    def body(x_ref, g_ref, h_ref):
        xv = x_ref[...]
        h_ref[...] = (xv * _rms(xv) * g_ref[...]).astype(BF16)

    row = pl.BlockSpec((tt, d), lambda i: (i, 0))
    return _call(body, name=name, grid=(t // tt,), in_specs=[row, _full((1, d))], out_specs=row,
                 out_shape=jax.ShapeDtypeStruct((t, d), BF16), compiler_params=_params(("parallel",)))(x, g)


def _resid_norm(x, n, g1, g2, *, name):
    t, d = x.shape
    tt = _pick_rows(t, ROW_TILE)
    gains = [g1] if g2 is None else [g1, g2]

    def body(x_ref, n_ref, *refs):
        nv = n_ref[...].astype(F32)
        xn = x_ref[...] + nv * _rms(nv) * refs[0][...]
        refs[len(gains)][...] = xn
        if g2 is not None:
            refs[3][...] = (xn * _rms(xn) * refs[1][...]).astype(BF16)

    row = pl.BlockSpec((tt, d), lambda i: (i, 0))
    outs = _call(body, name=name, grid=(t // tt,), in_specs=[row, row] + [_full((1, d))] * len(gains),
                 out_specs=[row] * len(gains),
                 out_shape=[jax.ShapeDtypeStruct((t, d), F32), jax.ShapeDtypeStruct((t, d), BF16)][:len(gains)],
                 compiler_params=_params(("parallel",)))(x, n, *gains)
    return (outs[0], None) if g2 is None else outs


def _loss_fwd_bwd(xf, target, *, name):
    t, d = xf.shape
    tt = _pick_rows(t, ROW_TILE)
    nt = t // tt

    def body(x_ref, t_ref, dy_ref, loss_ref, acc):
        i = pl.program_id(0)

        @pl.when(i == 0)
        def _():
            acc[...] = jnp.zeros_like(acc)

        e = x_ref[...] - t_ref[...]
        dy_ref[...] = e * (1.0 / d)
        acc[...] += jnp.sum(e * e, axis=0, keepdims=True)

        @pl.when(i == nt - 1)
        def _():
            loss_ref[...] = jnp.sum(acc[...], axis=-1, keepdims=True) * (0.5 / d)

    row = pl.BlockSpec((tt, d), lambda i: (i, 0))
    return _call(body, name=name, grid=(nt,), in_specs=[row, row], out_specs=[row, _full((1, 1))],
                 out_shape=[jax.ShapeDtypeStruct((t, d), F32), jax.ShapeDtypeStruct((1, 1), F32)],
                 scratch_shapes=[pltpu.VMEM((1, d), F32)], compiler_params=_params(("arbitrary",)))(xf, target)


def _tail_loss(x, n, g, target, *, name):
    t, d = x.shape
    tt = _pick_rows(t, ROW_TILE)
    nt = t // tt

    def body(x_ref, n_ref, g_ref, t_ref, dy_ref, dn_ref, dg_ref, loss_ref, acc):
        i = pl.program_id(0)

        @pl.when(i == 0)
        def _():
            acc[...] = jnp.zeros_like(acc)
            dg_ref[...] = jnp.zeros_like(dg_ref)

        nv = n_ref[...].astype(F32)
        r = _rms(nv)
        e = (x_ref[...] + nv * r * g_ref[...]) - t_ref[...]
        dy = e * (1.0 / d)
        dy_ref[...] = dy
        acc[...] += jnp.sum(e * e, axis=0, keepdims=True)
        dn, dg = _rms_bwd(nv, r, g_ref[...], dy)
        dn_ref[...] = dn.astype(BF16)
        dg_ref[...] += jnp.sum(dg, axis=0, keepdims=True)

        @pl.when(i == nt - 1)
        def _():
            loss_ref[...] = jnp.sum(acc[...], axis=-1, keepdims=True) * (0.5 / d)

    row = pl.BlockSpec((tt, d), lambda i: (i, 0))
    vec = _full((1, d))
    return _call(body, name=name, grid=(nt,), in_specs=[row, row, vec, row],
                 out_specs=[row, row, vec, _full((1, 1))],
                 out_shape=[jax.ShapeDtypeStruct((t, d), F32), jax.ShapeDtypeStruct((t, d), BF16),
                            jax.ShapeDtypeStruct((1, d), F32), jax.ShapeDtypeStruct((1, 1), F32)],
                 scratch_shapes=[pltpu.VMEM((1, d), F32)], compiler_params=_params(("arbitrary",)))(x, n, g, target)


def _bwd_norm_pair(xin, dh, dres, n, g_in, g_out, *, name):
    t, d = xin.shape
    tt = _pick_rows(t, ROW_TILE)
    n_dh = len(dh)

    def body(*refs):
        x_ref = refs[0]
        dh_refs = refs[1:1 + n_dh]
        dres_ref, n_ref, gi_ref, go_ref, dx_ref, dn_ref, dgi_ref, dgo_ref = refs[1 + n_dh:]
        i = pl.program_id(0)

        @pl.when(i == 0)
        def _():
            dgi_ref[...] = jnp.zeros_like(dgi_ref)
            dgo_ref[...] = jnp.zeros_like(dgo_ref)

        xv = x_ref[...]
        dhv = dh_refs[0][...].astype(F32)
        for r in dh_refs[1:]:
            dhv = dhv + r[...].astype(F32)
        dxh, dgi = _rms_bwd(xv, _rms(xv), gi_ref[...], dhv)
        dx = dres_ref[...] + dxh
        dx_ref[...] = dx
        dgi_ref[...] += jnp.sum(dgi, axis=0, keepdims=True)
        nv = n_ref[...].astype(F32)
        dn, dgo = _rms_bwd(nv, _rms(nv), go_ref[...], dx)
        dn_ref[...] = dn.astype(BF16)
        dgo_ref[...] += jnp.sum(dgo, axis=0, keepdims=True)

    row = pl.BlockSpec((tt, d), lambda i: (i, 0))
    vec = _full((1, d))
    return _call(body, name=name, grid=(t // tt,), in_specs=[row] * (n_dh + 3) + [vec, vec],
                 out_specs=[row, row, vec, vec],
                 out_shape=[jax.ShapeDtypeStruct((t, d), F32), jax.ShapeDtypeStruct((t, d), BF16),
                            jax.ShapeDtypeStruct((1, d), F32), jax.ShapeDtypeStruct((1, d), F32)],
                 compiler_params=_params(("arbitrary",)))(xin, *dh, dres, n, g_in, g_out)


def _bwd_norm_in(xin, dh, dres, g_in, *, name):
    t, d = xin.shape
    tt = _pick_rows(t, ROW_TILE)
    n_dh = len(dh)

    def body(*refs):
        x_ref = refs[0]
        dh_refs = refs[1:1 + n_dh]
        dres_ref, gi_ref, dx_ref, dgi_ref = refs[1 + n_dh:]
        i = pl.program_id(0)

        @pl.when(i == 0)
        def _():
            dgi_ref[...] = jnp.zeros_like(dgi_ref)

        xv = x_ref[...]
        dhv = dh_refs[0][...].astype(F32)
        for r in dh_refs[1:]:
            dhv = dhv + r[...].astype(F32)
        dxh, dgi = _rms_bwd(xv, _rms(xv), gi_ref[...], dhv)
        dx_ref[...] = dres_ref[...] + dxh
        dgi_ref[...] += jnp.sum(dgi, axis=0, keepdims=True)

    row = pl.BlockSpec((tt, d), lambda i: (i, 0))
    vec = _full((1, d))
    return _call(body, name=name, grid=(t // tt,), in_specs=[row] * (n_dh + 2) + [vec],
                 out_specs=[row, vec],
                 out_shape=[jax.ShapeDtypeStruct((t, d), F32), jax.ShapeDtypeStruct((1, d), F32)],
                 compiler_params=_params(("arbitrary",)))(xin, *dh, dres, g_in)


def _bwd_norm_out(n, g_out, dx, *, name):
    t, d = n.shape
    tt = _pick_rows(t, ROW_TILE)

    def body(n_ref, go_ref, dx_ref, dn_ref, dgo_ref):
        i = pl.program_id(0)

        @pl.when(i == 0)
        def _():
            dgo_ref[...] = jnp.zeros_like(dgo_ref)

        nv = n_ref[...].astype(F32)
        dn, dgo = _rms_bwd(nv, _rms(nv), go_ref[...], dx_ref[...])
        dn_ref[...] = dn.astype(BF16)
        dgo_ref[...] += jnp.sum(dgo, axis=0, keepdims=True)

    row = pl.BlockSpec((tt, d), lambda i: (i, 0))
    vec = _full((1, d))
    return _call(body, name=name, grid=(t // tt,), in_specs=[row, vec, row], out_specs=[row, vec],
                 out_shape=[jax.ShapeDtypeStruct((t, d), BF16), jax.ShapeDtypeStruct((1, d), F32)],
                 compiler_params=_params(("arbitrary",)))(n, g_out, dx)


def _shift_down(cur, halo, s):
    return jnp.concatenate([halo[SUBLANES - s:], cur[:cur.shape[0] - s]], axis=0)


def _shift_up(cur, halo, s):
    return jnp.concatenate([cur[s:], halo[:s]], axis=0)


def _conva_fwd(pa, w, g, *, d, seq, name):
    t = pa.shape[0]
    tt = _pick_rows(seq, ROW_TILE)
    tps = seq // tt

    def body(xa_ref, ca_ref, ba_ref, w_ref, g_ref, ya_ref, v_ref, carry):
        i = pl.program_id(0)

        @pl.when(i % tps == 0)
        def _():
            carry[...] = jnp.zeros_like(carry)

        u = ca_ref[...].astype(F32) * xa_ref[...].astype(F32)
        halo = carry[...]
        wv = w_ref[...]
        v = wv[2:3] * u + wv[1:2] * _shift_down(u, halo, 1) + wv[0:1] * _shift_down(u, halo, 2)
        carry[...] = u[tt - SUBLANES:]
        yp = ba_ref[...].astype(F32) * v
        ya_ref[...] = (yp * _rms(yp) * g_ref[...]).astype(BF16)
        v_ref[...] = v.astype(BF16)

    col = lambda c: pl.BlockSpec((tt, d), lambda i, c=c: (i, c))
    row = pl.BlockSpec((tt, d), lambda i: (i, 0))
    return _call(body, name=name, grid=(t // tt,),
                 in_specs=[col(0), col(1), col(2), _full((CONV_K, d)), _full((1, d))], out_specs=[row, row],
                 out_shape=[jax.ShapeDtypeStruct((t, d), BF16), jax.ShapeDtypeStruct((t, d), BF16)],
                 scratch_shapes=[pltpu.VMEM((SUBLANES, d), F32)],
                 compiler_params=_params(("arbitrary",)))(pa, pa, pa, w, g)


def _conva_bwd(dcat, pa, v, w, g, *, d, seq, name):
    t, width = pa.shape
    d3 = 3 * d
    tt = _pick_rows(seq, ROW_TILE)
    tps = seq // tt
    nt = t // tt

    def body(dya_ref, xa_ref, ca_ref, ba_ref, v_ref, w_ref, g_ref, dpa_ref, dw_ref, dg_ref, carry):
        i = pl.program_id(0)

        @pl.when(i == 0)
        def _():
            dw_ref[...] = jnp.zeros_like(dw_ref)
            dg_ref[...] = jnp.zeros_like(dg_ref)

        @pl.when(i % tps == 0)
        def _():
            carry[...] = jnp.zeros_like(carry)

        xa, ca, ba, vv = [r[...].astype(F32) for r in (xa_ref, ca_ref, ba_ref, v_ref)]
        yp = ba * vv
        dyp, dgt = _rms_bwd(yp, _rms(yp), g_ref[...], dya_ref[...].astype(F32))
        dg_ref[...] += jnp.sum(dgt, axis=0, keepdims=True)
        dv = dyp * ba
        halo = carry[...]
        dv1 = _shift_up(dv, halo, 1)
        dv2 = _shift_up(dv, halo, 2)
        carry[...] = dv[:SUBLANES]
        wv = w_ref[...]
        du = wv[2:3] * dv + wv[1:2] * dv1 + wv[0:1] * dv2
        u = ca * xa
        dw_ref[0:1, :] += jnp.sum(u * dv2, axis=0, keepdims=True)
        dw_ref[1:2, :] += jnp.sum(u * dv1, axis=0, keepdims=True)
        dw_ref[2:3, :] += jnp.sum(u * dv, axis=0, keepdims=True)
        dpa_ref[:, 0:d] = (du * ca).astype(BF16)
        dpa_ref[:, d:2 * d] = (du * xa).astype(BF16)
        dpa_ref[:, 2 * d:3 * d] = (dyp * vv).astype(BF16)

    rcol = lambda c: pl.BlockSpec((tt, d), lambda i, c=c: (nt - 1 - i, c))
    return _call(body, name=name, grid=(nt,),
                 in_specs=[rcol(0), rcol(0), rcol(1), rcol(2), rcol(0), _full((CONV_K, d)), _full((1, d))],
                 out_specs=[pl.BlockSpec((tt, d3), lambda i: (nt - 1 - i, 0)), _full((CONV_K, d)), _full((1, d))],
                 out_shape=[jax.ShapeDtypeStruct((t, width), BF16), jax.ShapeDtypeStruct((CONV_K, d), F32),
                            jax.ShapeDtypeStruct((1, d), F32)],
                 scratch_shapes=[pltpu.VMEM((SUBLANES, d), F32)],
                 compiler_params=_params(("arbitrary",)))(dcat, pa, pa, pa, v, w, g)


CONV_CH = 512


def _convb_fwd(proj, w, bias, *, col0, seq, name):
    t = proj.shape[0]
    c = w.shape[1]
    cb = _pick(c, CONV_CH)
    assert col0 % cb == 0
    tt = _pick_rows(seq, 2 * ROW_TILE)
    tps = seq // tt

    def body(p_ref, w_ref, b_ref, o_ref, carry):
        i = pl.program_id(1)

        @pl.when(i % tps == 0)
        def _():
            carry[...] = jnp.zeros_like(carry)

        p = p_ref[...].astype(F32)
        halo = carry[...]
        wv = w_ref[...]
        o = wv[3:4] * p + b_ref[...]
        for s in (1, 2, 3):
            o = o + wv[3 - s:4 - s] * _shift_down(p, halo, s)
        carry[...] = p[tt - SUBLANES:]
        o_ref[...] = o.astype(BF16)

    return _call(body, name=name, grid=(c // cb, t // tt),
                 in_specs=[pl.BlockSpec((tt, cb), lambda jc, i: (i, col0 // cb + jc)),
                           pl.BlockSpec((SSM_CONV_K, cb), lambda jc, i: (0, jc)), pl.BlockSpec((1, cb), lambda jc, i: (0, jc))],
                 out_specs=pl.BlockSpec((tt, cb), lambda jc, i: (i, jc)), out_shape=jax.ShapeDtypeStruct((t, c), BF16),
                 scratch_shapes=[pltpu.VMEM((SUBLANES, cb), F32)],
                 compiler_params=_params(("arbitrary", "arbitrary")))(proj, w, bias)


def _convb_bwd(dconv, proj, w, dproj, *, col0, seq, name):
    t, c = dconv.shape
    cb = _pick(c, CONV_CH)
    assert col0 % cb == 0
    tt = _pick_rows(seq, 2 * ROW_TILE)
    tps = seq // tt
    nt = t // tt

    def body(dc_ref, p_ref, w_ref, dproj_in, dp_ref, dw_ref, db_ref, carry):
        del dproj_in
        i = pl.program_id(1)

        @pl.when(i == 0)
        def _():
            dw_ref[...] = jnp.zeros_like(dw_ref)
            db_ref[...] = jnp.zeros_like(db_ref)

        @pl.when(i % tps == 0)
        def _():
            carry[...] = jnp.zeros_like(carry)

        dc = dc_ref[...].astype(F32)
        p = p_ref[...].astype(F32)
        halo = carry[...]
        wv = w_ref[...]
        dp = wv[3:4] * dc
        dw_ref[3:4, :] += jnp.sum(p * dc, axis=0, keepdims=True)
        for s in (1, 2, 3):
            dcs = _shift_up(dc, halo, s)
            dp = dp + wv[3 - s:4 - s] * dcs
            dw_ref[3 - s:4 - s, :] += jnp.sum(p * dcs, axis=0, keepdims=True)
        carry[...] = dc[:SUBLANES]
        db_ref[...] += jnp.sum(dc, axis=0, keepdims=True)
        dp_ref[...] = dp.astype(BF16)

    win_spec = pl.BlockSpec((tt, cb), lambda jc, i: (nt - 1 - i, col0 // cb + jc))
    taps = pl.BlockSpec((SSM_CONV_K, cb), lambda jc, i: (0, jc))
    return _call(body, name=name, grid=(c // cb, nt),
                 in_specs=[pl.BlockSpec((tt, cb), lambda jc, i: (nt - 1 - i, jc)), win_spec, taps,
                           pl.BlockSpec(memory_space=pl.ANY)],
                 out_specs=[win_spec, taps, pl.BlockSpec((1, cb), lambda jc, i: (0, jc))],
                 out_shape=[jax.ShapeDtypeStruct(dproj.shape, BF16), jax.ShapeDtypeStruct((SSM_CONV_K, c), F32),
                            jax.ShapeDtypeStruct((1, c), F32)],
                 input_output_aliases={3: 0},
                 scratch_shapes=[pltpu.VMEM((SUBLANES, cb), F32)],
                 compiler_params=_params(("arbitrary", "arbitrary")))(dconv, proj, w, dproj)


def _expand_heads(x, ev):
    return jnp.dot(x, ev, precision=HIGHEST, preferred_element_type=F32)


def _head_sums(v, ev):
    return lax.dot_general(v, ev, (((1,), (1,)), ((), ())), precision=HIGHEST, preferred_element_type=F32)


def _ssd_common(c_ref, pdt_ref, dtb_ref, alog_ref, e_ref, h):
    cp = c_ref[...].astype(F32)
    sg = _sigmoid(cp)
    act = cp * sg
    pre = pdt_ref[:, 0:h] + dtb_ref[...]
    dt = _softplus(pre)
    a = -jnp.exp(alog_ref[...])
    adt = dt * a
    row = lax.broadcasted_iota(jnp.int32, (CHUNK, CHUNK), 0)
    col = lax.broadcasted_iota(jnp.int32, (CHUNK, CHUNK), 1)
    tril = row >= col
    cs = jnp.dot(tril.astype(F32), adt, precision=HIGHEST, preferred_element_type=F32)
    cs_t = lax.dot_general(adt, (col >= row).astype(F32), (((0,), (0,)), ((), ())), precision=HIGHEST,
                           preferred_element_type=F32)
    ev = e_ref[...]
    dt_l = _expand_heads(dt, ev)
    ecs_l = jnp.exp(_expand_heads(cs, ev))
    return dict(cp=cp, sg=sg, act=act, pre=pre, dt=dt, a=a, cs=cs, cs_t=cs_t, dt_l=dt_l, ecs_l=ecs_l,
                tril=tril, row=row, col=col, lo=col < HEAD_DIM)


def _dot_nt(a, b):
    return lax.dot_general(a, b, (((1,), (1,)), ((), ())), preferred_element_type=F32)


def _dot_tn(a, b):
    return lax.dot_general(a, b, (((0,), (0,)), ((), ())), preferred_element_type=F32)


def _dot(a, b):
    return jnp.dot(a, b, preferred_element_type=F32)


def _ssd_fwd(cpre, pdt, pz, ya, dtb, alog, dsk_lane, gs, emat, *, nseq, seq, name):
    t, xbc = cpre.shape
    d = ya.shape[1]
    h = d // HEAD_DIM
    npair = h // 2
    ppg = npair // SSM_GROUPS
    nc = seq // CHUNK
    gw = d // SSM_GROUPS
    bc0 = d
    cc0 = d + SSM_GROUPS * D_STATE

    def body(c_ref, pdt_ref, z_ref, ya_ref, dtb_ref, alog_ref, dsk_ref, gs_ref, e_ref, cat_ref, y2_ref, hp_ref, h_ref):
        @pl.when(pl.program_id(0) == 0)
        def _():
            h_ref[...] = jnp.zeros_like(h_ref)

        for sq in range(nseq):
            one_seq(c_ref.at[sq], pdt_ref.at[sq], z_ref.at[sq], ya_ref.at[sq], dtb_ref, alog_ref, dsk_ref, gs_ref, e_ref,
                    cat_ref.at[sq], y2_ref.at[sq], hp_ref.at[sq], h_ref.at[sq])

    def one_seq(c_ref, pdt_ref, z_ref, ya_ref, dtb_ref, alog_ref, dsk_ref, gs_ref, e_ref, cat_ref, y2_ref, hp_ref, h_ref):
        q = _ssd_common(c_ref, pdt_ref, dtb_ref, alog_ref, e_ref, h)
        act, cs, lo, ecs_l = q["act"], q["cs"], q["lo"], q["ecs_l"]
        xs = act[:, :d]
        xd = xs * q["dt_l"]
        ys = []
        for g in range(SSM_GROUPS):
            bg = act[:, bc0 + g * D_STATE: bc0 + (g + 1) * D_STATE]
            cgb = act[:, cc0 + g * D_STATE: cc0 + (g + 1) * D_STATE].astype(BF16)
            s = _dot_nt(cgb, bg.astype(BF16))
            bg_t = bg.T
            for jj in range(ppg):
                j = g * ppg + jj
                sl = slice(LANES * j, LANES * (j + 1))
                xdj = xd[:, sl]
                x2 = jnp.concatenate([jnp.where(lo, xdj, 0.0), jnp.where(lo, 0.0, xdj)], axis=0).astype(BF16)
                hprev = h_ref[j]
                hp_ref[j] = hprev.astype(BF16)
                ms, bws_t = [], []
                for hh in (2 * j, 2 * j + 1):
                    csc = cs[:, hh:hh + 1]
                    cs_row = q["cs_t"][hh:hh + 1, :]
                    seg = jnp.broadcast_to(csc, (CHUNK, CHUNK)) - jnp.broadcast_to(cs_row, (CHUNK, CHUNK))
                    ms.append(s * jnp.exp(jnp.where(q["tril"], seg, -jnp.inf)))
                    bws_t.append(bg_t * jnp.exp(cs_row[:, CHUNK - 1:CHUNK] - cs_row))
                ydiag = _dot(jnp.concatenate(ms, axis=1).astype(BF16), x2)
                st = _dot(jnp.concatenate(bws_t, axis=1).astype(BF16), x2)
                ecs = ecs_l[:, sl]
                yoff = _dot(cgb, hprev.astype(BF16)) * ecs
                h_ref[j] = hprev * ecs[CHUNK - 1:CHUNK] + st
                ys.append(ydiag + yoff)
        y = jnp.concatenate(ys, axis=1) + dsk_ref[...] * xs
        y2_ref[...] = y.astype(BF16)
        zv = z_ref[...].astype(F32)
        y3 = y * (zv * _sigmoid(zv))
        cat_ref[:, 0:d] = ya_ref[...]
        for gi in range(SSM_GROUPS):
            seg = y3[:, gi * gw:(gi + 1) * gw]
            cat_ref[:, d + gi * gw:d + (gi + 1) * gw] = (seg * _rms(seg) * gs_ref[:, gi * gw:(gi + 1) * gw]).astype(BF16)

    chunk = lambda w, cb=0: pl.BlockSpec((nseq, CHUNK, w), lambda c, cb=cb: (0, c, cb))
    vec = lambda w: pl.BlockSpec((1, w), lambda c: (0, 0))
    hp_spec = pl.BlockSpec((nseq, None, npair, D_STATE, LANES), lambda c: (0, c, 0, 0, 0))
    per_seq = lambda a: a.reshape(nseq, seq, a.shape[1])
    cat, y2, hp = _call(
        body, name=name, grid=(nc,),
        in_specs=[chunk(xbc), chunk(LANES), chunk(d, 3), chunk(d), vec(h), vec(h), vec(d), vec(d),
                  pl.BlockSpec((h, d), lambda c: (0, 0))],
        out_specs=[chunk(2 * d), chunk(d), hp_spec],
        out_shape=[jax.ShapeDtypeStruct((nseq, seq, 2 * d), BF16), jax.ShapeDtypeStruct((nseq, seq, d), BF16),
                   jax.ShapeDtypeStruct((nseq, nc, npair, D_STATE, LANES), BF16)],
        scratch_shapes=[pltpu.VMEM((nseq, npair, D_STATE, LANES), F32)],
        compiler_params=_params(("arbitrary",)))(
            per_seq(cpre), per_seq(pdt), per_seq(pz), per_seq(ya), dtb, alog, dsk_lane, gs, emat)
    return cat.reshape(t, 2 * d), y2.reshape(t, d), hp


def _ssd_bwd(cpre, pdt, pz, y2, hprev_all, dcat, dtb, alog, dsk_lane, gs, emat, dproj, *, nseq, seq, name):
    t, xbc = cpre.shape
    d = y2.shape[1]
    h = d // HEAD_DIM
    npair = h // 2
    ppg = npair // SSM_GROUPS
    nc = seq // CHUNK
    gw = d // SSM_GROUPS
    bc0 = d
    cc0 = d + SSM_GROUPS * D_STATE

    def body(c_ref, pdt_ref, z_ref, y2_ref, hp_ref, dys_ref, dtb_ref, alog_ref, dsk_ref, gs_ref, e_ref, dproj_in,
             dconv_ref, dz_ref, dpdt_ref, dgs_ref, ddsk_ref, ddtb_ref, dalog_ref, dh_ref):
        del dproj_in
        b = pl.program_id(0)
        c = pl.program_id(1)

        @pl.when(c == 0)
        def _():
            dh_ref[...] = jnp.zeros_like(dh_ref)

        @pl.when((b == 0) & (c == 0))
        def _():
            dgs_ref[...] = jnp.zeros_like(dgs_ref)
            ddsk_ref[...] = jnp.zeros_like(ddsk_ref)
            ddtb_ref[...] = jnp.zeros_like(ddtb_ref)
            dalog_ref[...] = jnp.zeros_like(dalog_ref)

        q = _ssd_common(c_ref, pdt_ref, dtb_ref, alog_ref, e_ref, h)
        cp, sg, act, cs, a, dt, lo = q["cp"], q["sg"], q["act"], q["cs"], q["a"], q["dt"], q["lo"]
        ecs_l, dt_l = q["ecs_l"], q["dt_l"]
        ev = e_ref[...]
        xs = act[:, :d]
        xd = xs * dt_l
        row16 = lax.broadcasted_iota(jnp.int32, (CHUNK, h), 0)
        hid = lax.broadcasted_iota(jnp.int32, (1, h), 1)
        hid_t = lax.broadcasted_iota(jnp.int32, (h, 1), 0)

        zv = z_ref[...].astype(F32)
        sz = _sigmoid(zv)
        siluz = zv * sz
        y2v = y2_ref[...].astype(F32)
        y3 = y2v * siluz
        dysv = dys_ref[...].astype(F32)
        dy3s = []
        for gi in range(SSM_GROUPS):
            gsl = slice(gi * gw, (gi + 1) * gw)
            seg = y3[:, gsl]
            dseg, dgt = _rms_bwd(seg, _rms(seg), gs_ref[:, gsl], dysv[:, gsl])
            dy3s.append(dseg)
            dgs_ref[:, gsl] += jnp.sum(dgt, axis=0, keepdims=True)
        dy3 = jnp.concatenate(dy3s, axis=1)
        dy = dy3 * siluz
        dz_ref[...] = (dy3 * y2v * (sz * (1.0 + zv * (1.0 - sz)))).astype(BF16)
        ddsk_ref[...] += jnp.sum(_head_sums(dy * xs, ev), axis=0, keepdims=True)

        dcs = jnp.zeros((CHUNK, h), F32)
        dcs_t = jnp.zeros((h, CHUNK), F32)
        dxd_parts, yoff_parts, db_parts, dc_parts = [], [], [], []
        for g in range(SSM_GROUPS):
            bg = act[:, bc0 + g * D_STATE: bc0 + (g + 1) * D_STATE]
            cg = act[:, cc0 + g * D_STATE: cc0 + (g + 1) * D_STATE]
            bgb, cgb = bg.astype(BF16), cg.astype(BF16)
            cgb_t = cg.T.astype(BF16)
            s = _dot_nt(cgb, bgb)
            ds = jnp.zeros((CHUNK, CHUNK), F32)
            dbg = jnp.zeros((CHUNK, D_STATE), F32)
            dcg = jnp.zeros((CHUNK, D_STATE), F32)
            for jj in range(ppg):
                j = g * ppg + jj
                sl = slice(LANES * j, LANES * (j + 1))
                xdj = xd[:, sl]
                xdb = xdj.astype(BF16)
                x2 = jnp.concatenate([jnp.where(lo, xdj, 0.0), jnp.where(lo, 0.0, xdj)], axis=0).astype(BF16)
                dyj = dy[:, sl]
                dy2 = jnp.concatenate([jnp.where(lo, dyj, 0.0), jnp.where(lo, 0.0, dyj)], axis=0).astype(BF16)
                hpb = hp_ref[j]
                hprev = hpb.astype(F32)
                dhn = dh_ref[j]
                dhb = dhn.astype(BF16)
                dh2 = jnp.concatenate([jnp.where(lo, dhn, 0.0), jnp.where(lo, 0.0, dhn)], axis=0).astype(BF16)
                ecs = ecs_l[:, sl]
                gmat = (dyj * ecs).astype(BF16)
                yoff_parts.append(_dot(cgb, hpb) * ecs)
                dcg = dcg + _dot_nt(gmat, hpb)
                dh_ref[j] = dhn * ecs[CHUNK - 1:CHUNK] + _dot(cgb_t, gmat)
                t2 = dhn * hprev
                dbw2 = _dot_nt(x2, dhb)
                dm2 = _dot_nt(dy2, xdb)
                ms, bws = [], []
                for idx, hh in enumerate((2 * j, 2 * j + 1)):
                    msk = lo if idx == 0 else jnp.logical_not(lo)
                    onehot = (hid == hh).astype(F32)
                    csc = cs[:, hh:hh + 1]
                    seg = jnp.broadcast_to(csc, (CHUNK, CHUNK)) - jnp.broadcast_to(q["cs_t"][hh:hh + 1, :], (CHUNK, CHUNK))
                    lm = jnp.exp(jnp.where(q["tril"], seg, -jnp.inf))
                    m = s * lm
                    cs_last = cs[CHUNK - 1:CHUNK, hh:hh + 1]
                    dte = jnp.exp(cs_last - csc)
                    ms.append(m)
                    bws.append(bg * dte)
                    dbw = dbw2[idx * CHUNK:(idx + 1) * CHUNK]
                    dbg = dbg + dbw * dte
                    qv = jnp.sum(dbw * bg, axis=-1, keepdims=True) * dte
                    dm = dm2[idx * CHUNK:(idx + 1) * CHUNK]
                    wm = dm * m
                    rc = jnp.sum(wm, axis=-1, keepdims=True)
                    dcs_t = dcs_t - (hid_t == hh).astype(F32) * jnp.sum(wm, axis=0, keepdims=True)
                    ds = ds + dm * lm
                    ddec = jnp.sum(jnp.where(msk, t2, 0.0)) * jnp.exp(cs_last)
                    last = jnp.sum(qv) + ddec
                    dcs = dcs + (rc - qv) * onehot + jnp.where(row16 == CHUNK - 1, last * onehot, 0.0)
                dxd_s = _dot(jnp.concatenate(bws, axis=1).astype(BF16), dh2)
                dxd_d = _dot_tn(jnp.concatenate(ms, axis=0).astype(BF16), dy2)
                dxd_parts.append(dxd_s + dxd_d)
            dsb = ds.astype(BF16)
            dc_parts.append(dcg + _dot(dsb, bgb))
            db_parts.append(dbg + _dot_tn(dsb, cgb))
        yoff_all = jnp.concatenate(yoff_parts, axis=1)
        dxd_all = jnp.concatenate(dxd_parts, axis=1)
        dcs = dcs + _head_sums(dy * yoff_all, ev)
        triu = (q["col"] >= q["row"]).astype(F32)
        dadt = (jnp.dot(triu, dcs, precision=HIGHEST, preferred_element_type=F32)
                + lax.dot_general(triu, dcs_t, (((1,), (1,)), ((), ())), precision=HIGHEST, preferred_element_type=F32))
        ddt = dadt * a + _head_sums(dxd_all * xs, ev)
        dalog_ref[...] += jnp.sum(dadt * dt, axis=0, keepdims=True) * a
        dpre = ddt * _sigmoid(q["pre"])
        ddtb_ref[...] += jnp.sum(dpre, axis=0, keepdims=True)
        dpdt_ref[...] = jnp.zeros_like(dpdt_ref)
        dpdt_ref[:, 0:h] = dpre.astype(BF16)
        dxs = dxd_all * dt_l + dy * dsk_ref[...]
        dact = jnp.concatenate([dxs] + db_parts + dc_parts, axis=1)
        dconv_ref[...] = (dact * (sg * (1.0 + cp * (1.0 - sg)))).astype(BF16)

    rchunk = lambda w, cb=0: pl.BlockSpec((CHUNK, w), lambda b, c, cb=cb: (b * nc + nc - 1 - c, cb))
    vec = lambda w: pl.BlockSpec((1, w), lambda b, c: (0, 0))
    hp_spec = pl.BlockSpec((None, None, npair, D_STATE, LANES), lambda b, c: (b, nc - 1 - c, 0, 0, 0))
    return _call(body, name=name, grid=(nseq, nc),
                 in_specs=[rchunk(xbc), rchunk(LANES), rchunk(d, 3), rchunk(d), hp_spec, rchunk(d, 1),
                           vec(h), vec(h), vec(d), vec(d), pl.BlockSpec((h, d), lambda b, c: (0, 0)),
                           pl.BlockSpec(memory_space=pl.ANY)],
                 out_specs=[rchunk(xbc), rchunk(d, 3), rchunk(LANES), vec(d), vec(h), vec(h), vec(h)],
                 out_shape=[jax.ShapeDtypeStruct((t, xbc), BF16), jax.ShapeDtypeStruct(dproj.shape, BF16),
                            jax.ShapeDtypeStruct((t, LANES), BF16), jax.ShapeDtypeStruct((1, d), F32),
                            jax.ShapeDtypeStruct((1, h), F32), jax.ShapeDtypeStruct((1, h), F32),
                            jax.ShapeDtypeStruct((1, h), F32)],
                 input_output_aliases={11: 1},
                 scratch_shapes=[pltpu.VMEM((npair, D_STATE, LANES), F32)],
                 compiler_params=_params(("arbitrary", "arbitrary")))(
                     cpre, pdt, pz, y2, hprev_all, dcat, dtb, alog, dsk_lane, gs, emat, dproj)


def _sum_adamw(parts, w, m, v, *, name, layer=None, outs=None):
    n, r, c = parts.shape
    tr = _pick_rows(r, 256)
    bc1 = 1.0 - ADAM_B1 ** ADAM_STEP
    bc2 = 1.0 - ADAM_B2 ** ADAM_STEP

    def body(p_ref, w_ref, m_ref, v_ref, *rest):
        g_ref, d_ref, mo_ref, vo_ref = rest[-4:]
        g = p_ref[0].astype(F32)
        for k in range(1, n):
            g = g + p_ref[k].astype(F32)
        mn = ADAM_B1 * m_ref[...] + (1.0 - ADAM_B1) * g
        vn = ADAM_B2 * v_ref[...] + (1.0 - ADAM_B2) * (g * g)
        g_ref[...] = g
        mo_ref[...] = mn
        vo_ref[...] = vn
        d_ref[...] = -ADAM_LR * ((mn / bc1) / (jnp.sqrt(vn / bc2) + ADAM_EPS) + ADAM_WD * w_ref[...])

    p_spec = pl.BlockSpec((n, tr, c), lambda i: (0, i, 0))
    if layer is None:
        blk = pl.BlockSpec((tr, c), lambda i: (i, 0))
        return _call(body, name=name, grid=(r // tr,), in_specs=[p_spec, blk, blk, blk], out_specs=[blk] * 4,
                     out_shape=[jax.ShapeDtypeStruct((r, c), F32)] * 4,
                     compiler_params=_params(("parallel",)))(parts, w, m, v)
    blk = pl.BlockSpec((None, tr, c), lambda i: (layer, i, 0))
    if outs is None:
        outs = [lax.empty(w.shape, F32) for _ in range(4)]
    return _call(body, name=name, grid=(r // tr,),
                 in_specs=[p_spec, blk, blk, blk] + [pl.BlockSpec(memory_space=pl.ANY)] * 4, out_specs=[blk] * 4,
                 out_shape=[jax.ShapeDtypeStruct(w.shape, F32)] * 4, input_output_aliases={4 + k: k for k in range(4)},
                 compiler_params=_params(("parallel",)))(parts, w, m, v, *outs)


def _assemble_cols(blocks, *, name):
    nb, r, c = blocks.shape
    width = -(-nb * c // LANES) * LANES
    tr = _pick_rows(r, 256)

    def body(b_ref, o_ref):
        pieces = [b_ref[j] for j in range(nb)]
        if width > nb * c:
            pieces.append(jnp.zeros((tr, width - nb * c), blocks.dtype))
        o_ref[...] = jnp.concatenate(pieces, axis=1)

    return _call(body, name=name, grid=(r // tr,), in_specs=[pl.BlockSpec((nb, tr, c), lambda i: (0, i, 0))],
                 out_specs=pl.BlockSpec((tr, width), lambda i: (i, 0)), out_shape=jax.ShapeDtypeStruct((r, width), blocks.dtype),
                 compiler_params=_params(("parallel",)))(blocks)


def _split_cols(pieces, c, *, name):
    r = pieces[0].shape[0]
    tr = _pick_rows(r, 256)
    n_in = len(pieces)

    def body(*refs):
        o_ref = refs[n_in]
        x = jnp.concatenate([p[...] for p in refs[:n_in]], axis=1) if n_in > 1 else refs[0][...]
        for j in range(N_DEV):
            o_ref[j] = x[:, c * j:c * (j + 1)]

    return _call(body, name=name, grid=(r // tr,),
                 in_specs=[pl.BlockSpec((tr, p.shape[1]), lambda i: (i, 0)) for p in pieces],
                 out_specs=pl.BlockSpec((N_DEV, tr, c), lambda i: (0, i, 0)),
                 out_shape=jax.ShapeDtypeStruct((N_DEV, r, c), pieces[0].dtype),
                 compiler_params=_params(("parallel",)))(*pieces)


def _sum_parts(parts, *, name):
    n, r, c = parts.shape
    tr = _pick_rows(r, 256)

    def body(p_ref, g_ref):
        g = p_ref[0].astype(F32)
        for k in range(1, n):
            g = g + p_ref[k].astype(F32)
        g_ref[...] = g

    return _call(body, name=name, grid=(r // tr,), in_specs=[pl.BlockSpec((n, tr, c), lambda i: (0, i, 0))],
                 out_specs=pl.BlockSpec((tr, c), lambda i: (i, 0)), out_shape=jax.ShapeDtypeStruct((r, c), F32),
                 compiler_params=_params(("parallel",)))(parts)


def _peers():
    x, y, c = lax.axis_index("x"), lax.axis_index("y"), lax.axis_index("c")
    me = 4 * x + 2 * y + c
    out = []
    for k in range(1, N_DEV):
        px = (1 - x) if (k >> 2) & 1 else x
        py = (1 - y) if (k >> 1) & 1 else y
        pc = (1 - c) if k & 1 else c
        out.append(((px, py, pc), 4 * px + 2 * py + pc))
    return me, out


_HBM = pl.BlockSpec(memory_space=pltpu.HBM)
_SEM = pl.BlockSpec(memory_space=pltpu.SEMAPHORE)
_EFFECT = pltpu.SideEffectType.DATAFLOW_SIDE_EFFECTING


ALL_PEERS = tuple(range(1, N_DEV))
SAME_CORE_PEERS = (2, 4, 6)


def _slot(ref, j, c):
    if c is None:
        return ref.at[j]
    start = j * c
    return ref.at[:, pl.ds(start if isinstance(start, int) else pl.multiple_of(start, c), c)]


def _split_copies(s_refs, l_refs, send_sems, recv_sems, gather, incoming, ks, src_cols, land_cols):
    me, peers = _peers()
    local, remote = [], []
    for ti, (s_ref, l_ref) in enumerate(zip(s_refs, l_refs)):
        base = ti * N_DEV
        sc, lc = src_cols[ti], land_cols[ti]
        local.append(pltpu.make_async_copy(s_ref if gather else _slot(s_ref, me, sc), _slot(l_ref, me, lc),
                                           recv_sems.at[base + N_DEV - 1]))
        for k, (dev, pid) in enumerate(peers):
            if k + 1 not in ks:
                continue
            sems = dict(send_sem=send_sems.at[base + k], recv_sem=recv_sems.at[base + k], device_id=dev, device_id_type=MESH)
            src = s_ref if gather else _slot(s_ref, pid, sc)
            remote.append((
                pltpu.make_async_remote_copy(src_ref=src, dst_ref=_slot(l_ref, me, lc), **sems),
                pltpu.make_async_remote_copy(src_ref=src, dst_ref=_slot(l_ref, pid, lc), **sems) if incoming else None))
    return local, remote


def _exchange_start(srcs, *, gather, name, after=(), ks=ALL_PEERS, src_cols=None, land_cols=None):
    n = len(srcs)
    after = list(after)
    src_cols = list(src_cols or [None] * n)
    land_cols = list(land_cols or [None] * n)
    srcs = [pltpu.with_memory_space_constraint(s, pltpu.HBM) for s in srcs]

    def land_shape(s, sc, lc):
        block = tuple(s.shape) if gather else ((s.shape[0], sc) if sc else tuple(s.shape[1:]))
        return (block[0], N_DEV * lc) if lc else (N_DEV,) + block

    lands = [pltpu.with_memory_space_constraint(lax.empty(land_shape(s, sc, lc), s.dtype), pltpu.HBM)
             for s, sc, lc in zip(srcs, src_cols, land_cols)]

    def body(*refs):
        s_refs, l_refs = refs[:n], refs[n:2 * n]
        outs = refs[2 * n + len(after):]
        send_sems, recv_sems, token = outs[0], outs[1], outs[-1]
        local, remote = _split_copies(s_refs, l_refs, send_sems, recv_sems, gather, False, ks, src_cols, land_cols)
        for cp in local:
            cp.start()
        for out_cp, _ in remote:
            out_cp.start()
        token[...] = jnp.zeros_like(token)

    outs = _call(
        body, name=name,
        out_shape=(pltpu.SemaphoreType.DMA((n * N_DEV,)), pltpu.SemaphoreType.DMA((n * N_DEV,)),
                   *[pltpu.HBM(s.shape, s.dtype) for s in srcs], *[pltpu.HBM(l.shape, l.dtype) for l in lands],
                   jax.ShapeDtypeStruct((SUBLANES, LANES), F32)),
        in_specs=[_HBM] * (2 * n) + [pl.BlockSpec(memory_space=pl.ANY)] * len(after),
        out_specs=(_SEM, _SEM, *[_HBM] * (2 * n), pl.BlockSpec(memory_space=pltpu.VMEM)),
        input_output_aliases={k: k + 2 for k in range(2 * n)},
        compiler_params=pltpu.CompilerParams(has_side_effects=_EFFECT),
    )(*srcs, *lands, *after)
    return dict(n=n, gather=gather, ks=ks, src_cols=src_cols, land_cols=land_cols, sems=outs[:2], srcs=outs[2:2 + n],
                lands=outs[2 + n:2 + 2 * n]), outs[-1]


def _exchange_wait(state, after, *, name):
    n, gather, ks = state["n"], state["gather"], state["ks"]
    after = list(after)

    def body(*refs):
        s_refs, l_refs = refs[:n], refs[n:2 * n]
        send_sems, recv_sems = refs[2 * n], refs[2 * n + 1]
        local, remote = _split_copies(s_refs, l_refs, send_sems, recv_sems, gather, True, ks, state["src_cols"],
                                      state["land_cols"])
        for out_cp, in_cp in remote:
            out_cp.wait_send()
            in_cp.wait_recv()
        for cp in local:
            cp.wait()

    outs = _call(
        body, name=name,
        out_shape=tuple(pltpu.HBM(a.shape, a.dtype) for a in (*state["srcs"], *state["lands"])),
        in_specs=[_HBM] * (2 * n) + [_SEM, _SEM] + [pl.BlockSpec(memory_space=pl.ANY)] * len(after),
        out_specs=tuple([_HBM] * (2 * n)),
        input_output_aliases={k: k for k in range(2 * n)},
        compiler_params=pltpu.CompilerParams(has_side_effects=_EFFECT),
    )(*state["srcs"], *state["lands"], *state["sems"], *after)
    return outs[n:]


def _sibling_copies(l_refs, send_sems, recv_sems, incoming, land_cols):
    x, y, c = lax.axis_index("x"), lax.axis_index("y"), lax.axis_index("c")
    out = []
    for ti, l_ref in enumerate(l_refs):
        for q in range(4):
            px = (1 - x) if q & 2 else x
            py = (1 - y) if q & 1 else y
            mine = _slot(l_ref, 4 * px + 2 * py + c, land_cols[ti])
            theirs = _slot(l_ref, 4 * px + 2 * py + (1 - c), land_cols[ti])
            sems = dict(send_sem=send_sems.at[4 * ti + q], recv_sem=recv_sems.at[4 * ti + q],
                        device_id=(x, y, 1 - c), device_id_type=MESH)
            out.append((
                pltpu.make_async_remote_copy(src_ref=mine, dst_ref=mine, **sems),
                pltpu.make_async_remote_copy(src_ref=mine, dst_ref=theirs, **sems) if incoming else None))
    return out


def _sibling_start(lands, *, name, after=(), land_cols=None):
    n = len(lands)
    after = list(after)
    land_cols = list(land_cols or [None] * n)
    lands = [pltpu.with_memory_space_constraint(l, pltpu.HBM) for l in lands]

    def body(*refs):
        l_refs = refs[:n]
        outs = refs[n + len(after):]
        for out_cp, _ in _sibling_copies(l_refs, outs[0], outs[1], False, land_cols):
            out_cp.start()
        outs[-1][...] = jnp.zeros_like(outs[-1])

    outs = _call(
        body, name=name,
        out_shape=(pltpu.SemaphoreType.DMA((4 * n,)), pltpu.SemaphoreType.DMA((4 * n,)),
                   *[pltpu.HBM(l.shape, l.dtype) for l in lands], jax.ShapeDtypeStruct((SUBLANES, LANES), F32)),
        in_specs=[_HBM] * n + [pl.BlockSpec(memory_space=pl.ANY)] * len(after),
        out_specs=(_SEM, _SEM, *[_HBM] * n, pl.BlockSpec(memory_space=pltpu.VMEM)),
        input_output_aliases={k: k + 2 for k in range(n)},
        compiler_params=pltpu.CompilerParams(has_side_effects=_EFFECT),
    )(*lands, *after)
    return dict(n=n, land_cols=land_cols, sems=outs[:2], lands=outs[2:2 + n]), outs[-1]


def _sibling_wait(state, after, *, name):
    n = state["n"]
    after = list(after)

    def body(*refs):
        l_refs = refs[:n]
        for out_cp, in_cp in _sibling_copies(l_refs, refs[n], refs[n + 1], True, state["land_cols"]):
            out_cp.wait_send()
            in_cp.wait_recv()

    return _call(
        body, name=name,
        out_shape=tuple(pltpu.HBM(a.shape, a.dtype) for a in state["lands"]),
        in_specs=[_HBM] * n + [_SEM, _SEM] + [pl.BlockSpec(memory_space=pl.ANY)] * len(after),
        out_specs=tuple([_HBM] * n), input_output_aliases={k: k for k in range(n)},
        compiler_params=pltpu.CompilerParams(has_side_effects=_EFFECT),
    )(*state["lands"], *state["sems"], *after)


def _pack(arrs):
    flat = jnp.concatenate([a.reshape(-1).astype(F32) for a in arrs])
    pad = (-flat.shape[0]) % (SUBLANES * LANES)
    return jnp.pad(flat, (0, pad)).reshape(-1, LANES)


def _unpack(packed, shapes):
    flat = packed.reshape(-1)
    out, off = [], 0
    for s in shapes:
        n = 1
        for v in s:
            n *= v
        out.append(flat[off:off + n].reshape(s))
        off += n
    return out


SMALL = ("norm_mix_pre", "ssm_conv_b", "dt_bias", "a_log", "d_skip", "conv_out_norm", "ssm_out_norm",
         "norm_mix_post", "norm_mlp_pre", "norm_mlp_post", "conv_a_w", "ssm_conv_w")
BIG = ("w_in", "w_out", "w_up", "w_down")
ORDER = ("norm_mix_pre", "w_in", "conv_a_w", "ssm_conv_w", "ssm_conv_b", "dt_bias", "a_log", "d_skip",
         "conv_out_norm", "ssm_out_norm", "w_out", "norm_mix_post", "norm_mlp_pre", "w_up", "w_down", "norm_mlp_post")


def kernel(x, norm_mix_pre, w_in, conv_a_w, ssm_conv_w, ssm_conv_b, dt_bias, a_log, d_skip, conv_out_norm, ssm_out_norm, w_out, norm_mix_post, norm_mlp_pre, w_up, w_down, norm_mlp_post, loss_target, m_norm_mix_pre, m_w_in, m_conv_a_w, m_ssm_conv_w, m_ssm_conv_b, m_dt_bias, m_a_log, m_d_skip, m_conv_out_norm, m_ssm_out_norm, m_w_out, m_norm_mix_post, m_norm_mlp_pre, m_w_up, m_w_down, m_norm_mlp_post, v_norm_mix_pre, v_w_in, v_conv_a_w, v_ssm_conv_w, v_ssm_conv_b, v_dt_bias, v_a_log, v_d_skip, v_conv_out_norm, v_ssm_out_norm, v_w_out, v_norm_mix_post, v_norm_mlp_pre, v_w_up, v_w_down, v_norm_mlp_post):
    W = dict(norm_mix_pre=norm_mix_pre, w_in=w_in, conv_a_w=conv_a_w, ssm_conv_w=ssm_conv_w, ssm_conv_b=ssm_conv_b,
             dt_bias=dt_bias, a_log=a_log, d_skip=d_skip, conv_out_norm=conv_out_norm, ssm_out_norm=ssm_out_norm,
             w_out=w_out, norm_mix_post=norm_mix_post, norm_mlp_pre=norm_mlp_pre, w_up=w_up, w_down=w_down,
             norm_mlp_post=norm_mlp_post)
    M = dict(norm_mix_pre=m_norm_mix_pre, w_in=m_w_in, conv_a_w=m_conv_a_w, ssm_conv_w=m_ssm_conv_w,
             ssm_conv_b=m_ssm_conv_b, dt_bias=m_dt_bias, a_log=m_a_log, d_skip=m_d_skip,
             conv_out_norm=m_conv_out_norm, ssm_out_norm=m_ssm_out_norm, w_out=m_w_out,
             norm_mix_post=m_norm_mix_post, norm_mlp_pre=m_norm_mlp_pre, w_up=m_w_up, w_down=m_w_down,
             norm_mlp_post=m_norm_mlp_post)
    V = dict(norm_mix_pre=v_norm_mix_pre, w_in=v_w_in, conv_a_w=v_conv_a_w, ssm_conv_w=v_ssm_conv_w,
             ssm_conv_b=v_ssm_conv_b, dt_bias=v_dt_bias, a_log=v_a_log, d_skip=v_d_skip,
             conv_out_norm=v_conv_out_norm, ssm_out_norm=v_ssm_out_norm, w_out=v_w_out,
             norm_mix_post=v_norm_mix_post, norm_mlp_pre=v_norm_mlp_pre, w_up=v_w_up, w_down=v_w_down,
             norm_mlp_post=v_norm_mlp_post)

    nseq, seq, d = x.shape
    t = nseq * seq
    depth = w_in.shape[0]
    h = d // HEAD_DIM
    xbc = d + 2 * SSM_GROUPS * D_STATE
    in_cols = w_in.shape[2] * N_DEV
    d_mix = w_out.shape[1] * N_DEV
    d_ff = w_up.shape[2] * N_DEV
    me = 4 * lax.axis_index("x") + 2 * lax.axis_index("y") + lax.axis_index("c")
    ca_shard = conv_a_w.shape[2]
    sc_shard = ssm_conv_w.shape[2]

    tap_shapes = [conv_a_w.shape[1:], ssm_conv_w.shape[1:]]

    def gather_start(i, after=()):
        ks = SAME_CORE_PEERS
        st_in, tok_in = _exchange_start([w_in[i].astype(BF16), _pack([conv_a_w[i], ssm_conv_w[i]])], gather=True,
                                        name=f"gather_start_in_{i}", after=after, ks=ks)
        st_rest, tok_rest = _exchange_start([W[n][i].astype(BF16) for n in ("w_out", "w_up", "w_down")], gather=True,
                                            name=f"gather_start_rest_{i}", after=[tok_in], ks=ks, land_cols=rest_cols)
        return st_in, st_rest, tok_rest

    rest_cols = [None, d_ff // N_DEV, None]

    vec = lambda name, i: W[name][i].reshape(1, -1)
    emat = (lax.broadcasted_iota(jnp.int32, (h, d), 1) // HEAD_DIM == lax.broadcasted_iota(jnp.int32, (h, d), 0)).astype(F32)

    xcur = x.reshape(t, d)
    hcur = _norm_fwd(xcur, vec("norm_mix_pre", 0), name="norm_first")
    saved = []
    nxt = gather_start(0)
    sib_in = None
    for i in range(depth):
        st_in, st_rest, tok = nxt
        if sib_in is None:
            sib_in, _ = _sibling_start(_exchange_wait(st_in, [hcur, tok], name=f"gather_wait_in_{i}"),
                                       name=f"gather_sibling_start_in_{i}")
        win_g, taps_g = _sibling_wait(sib_in, [hcur], name=f"gather_sibling_wait_in_{i}")
        win = _assemble_cols(win_g, name=f"assemble_w_in_{i}")
        taps_j = [_unpack(taps_g[j], tap_shapes) for j in range(N_DEV)]
        conv_a_i = jnp.concatenate([tj[0] for tj in taps_j], axis=1)
        ssm_conv_i = jnp.concatenate([tj[1] for tj in taps_j], axis=1)
        proj = _mm(hcur, win, n=4 * d + xbc, name=f"fwd_proj_{i}", out_dtypes=(BF16,))
        pdt = _mm(hcur, win, n=LANES, b_off=4 * d + xbc, name=f"fwd_proj_dt_{i}")
        ya, va = _conva_fwd(proj, conv_a_i, vec("conv_out_norm", i), d=d, seq=seq, name=f"fwd_conv_a_{i}")
        cpre = _convb_fwd(proj, ssm_conv_i, vec("ssm_conv_b", i), col0=4 * d, seq=seq, name=f"fwd_conv_b_{i}")
        dsk_lane = jnp.repeat(W["d_skip"][i], HEAD_DIM).reshape(1, d)
        st_sib, tok_sib = _sibling_start(_exchange_wait(st_rest, [cpre], name=f"gather_wait_rest_{i}"),
                                         name=f"gather_sibling_start_rest_{i}", land_cols=rest_cols)
        cat, y2, hprev = _ssd_fwd(cpre, pdt, proj, ya, vec("dt_bias", i) + tok_sib[0:1, 0:1], vec("a_log", i), dsk_lane,
                                  vec("ssm_out_norm", i), emat, nseq=nseq, seq=seq, name=f"fwd_ssd_{i}")
        wout_g, wup_g, wdown_g = _sibling_wait(st_sib, [cat], name=f"gather_sibling_wait_rest_{i}")
        lw = dict(win=win, wout=wout_g.reshape(d_mix, d),
                  wup=wup_g, wdown=wdown_g.reshape(d_ff, d),
                  conv_a=conv_a_i, ssm_conv=ssm_conv_i)
        after = []
        if i + 1 < depth:
            nxt = gather_start(i + 1, after=[wout_g])
            after = [nxt[2]]
        x1, h2, mix = _mm(cat, lw["wout"], name=f"fwd_out_{i}", after=after, out_dtypes=(F32, BF16, BF16),
                          epi=_epi_resid_norm, extras=(xcur,), vecs=(vec("norm_mix_post", i), vec("norm_mlp_pre", i)),
                          tm_cap=FUSED_ROWS)
        f = _mm(h2, lw["wup"], name=f"fwd_up_{i}", out_dtypes=(BF16,), epi=_epi_relu2)
        g_next = vec("norm_mix_pre", i + 1) if i + 1 < depth else None
        after = []
        if i + 1 < depth:
            sib_in, tok_in = _sibling_start(_exchange_wait(nxt[0], [f], name=f"gather_wait_in_{i + 1}"),
                                            name=f"gather_sibling_start_in_{i + 1}")
            after = [tok_in]
        dn = _mm(f, lw["wdown"], name=f"fwd_down_{i}", out_dtypes=(BF16,), after=after)
        if i + 1 < depth:
            x2, hnext = _resid_norm(x1, dn, vec("norm_mlp_post", i), g_next, name=f"fwd_post_mlp_{i}")
        else:
            x2 = hnext = None
        saved.append(dict(lw=lw, x0=xcur, h=hcur, proj=proj, pdt=pdt, va=va, cpre=cpre, y2=y2,
                          hprev=hprev, cat=cat, mix=mix, x1=x1, h2=h2, f=f, dn=dn, dsk_lane=dsk_lane))
        xcur, hcur = x2, hnext

    last = saved[depth - 1]
    dx, ddn, dg_last, loss_part = _tail_loss(last["x1"], last["dn"], vec("norm_mlp_post", depth - 1),
                                             loss_target.reshape(t, d), name="loss")
    loss = lax.psum(loss_part[0, 0], ("x", "y", "c"))

    small_grads = {n: [None] * depth for n in SMALL}
    big_out = {n: None for n in BIG}

    def finish(pending, after):
        li, st_a, st_b = pending

        def update(n, parts):
            big_out[n] = _sum_adamw(parts, W[n], M[n], V[n], layer=li, outs=big_out[n], name=f"adamw_{n}_{li}")

        p_down, p_up, p_out = _exchange_wait(st_a, after, name=f"scatter_wait_a_{li}")
        update("w_down", p_down)
        update("w_up", p_up)
        update("w_out", p_out)
        p_in, = _exchange_wait(st_b, after + [big_out["w_out"][0]], name=f"scatter_wait_b_{li}")
        update("w_in", p_in)

    pending = None
    for i in reversed(range(depth)):
        s = saved[i]
        lw = s["lw"]
        if i == depth - 1:
            small_grads["norm_mlp_post"][i] = dg_last
        dup = _mm(ddn, lw["wdown"], tb=True, name=f"bwd_down_dx_{i}", out_dtypes=(BF16,), epi=_epi_drelu2,
                  extras=(s["f"],))
        g_wdown = _mm(s["f"], ddn, ta=True, name=f"bwd_down_dw_{i}", out_dtypes=(BF16,))
        dh2 = _mm(dup, lw["wup"], tb=True, name=f"bwd_up_dx_{i}", out_dtypes=(BF16,))
        g_wup = _mm(s["h2"], dup, ta=True, name=f"bwd_up_dw_{i}", out_dtypes=(BF16,))
        dx1, dmix, dg_pre, dg_post = _bwd_norm_pair(s["x1"], [dh2], dx, s["mix"], vec("norm_mlp_pre", i),
                                                    vec("norm_mix_post", i), name=f"bwd_norm_mix_post_{i}")
        small_grads["norm_mlp_pre"][i] = dg_pre
        small_grads["norm_mix_post"][i] = dg_post
        dcat = _mm(dmix, lw["wout"], tb=True, name=f"bwd_out_dx_{i}", out_dtypes=(BF16,))
        g_wout = _mm(s["cat"], dmix, ta=True, name=f"bwd_out_dw_{i}", out_dtypes=(BF16,))
        st_a, tok_a = _exchange_start(
            [g_wdown.reshape(N_DEV, d_ff // N_DEV, d), g_wup, g_wout.reshape(N_DEV, d_mix // N_DEV, d)],
            gather=False, name=f"scatter_start_a_{i}", src_cols=[None, d_ff // N_DEV, None])
        dproj, dcaw, dgca = _conva_bwd(dcat, s["proj"], s["va"], lw["conv_a"],
                                       vec("conv_out_norm", i) + tok_a[0:1, 0:1], d=d, seq=seq, name=f"bwd_conv_a_{i}")
        small_grads["conv_a_w"][i] = dcaw
        small_grads["conv_out_norm"][i] = dgca
        dconv, dproj, dpdt, dgs, ddsk, ddtb, dalog = _ssd_bwd(
            s["cpre"], s["pdt"], s["proj"], s["y2"], s["hprev"], dcat, vec("dt_bias", i), vec("a_log", i),
            s["dsk_lane"], vec("ssm_out_norm", i), emat, dproj, nseq=nseq, seq=seq, name=f"bwd_ssd_{i}")
        small_grads["ssm_out_norm"][i] = dgs
        small_grads["d_skip"][i] = ddsk
        small_grads["dt_bias"][i] = ddtb
        small_grads["a_log"][i] = dalog
        dproj, dscw, dscb = _convb_bwd(dconv, s["proj"], lw["ssm_conv"], dproj, col0=4 * d, seq=seq,
                                       name=f"bwd_conv_b_{i}")
        small_grads["ssm_conv_w"][i] = dscw
        small_grads["ssm_conv_b"][i] = dscb
        g_win = _split_cols([
            _mm(s["h"], dproj, ta=True, name=f"bwd_proj_dw_{i}", out_dtypes=(BF16,)),
            _mm(s["h"], dpdt, ta=True, name=f"bwd_proj_dt_dw_{i}", out_dtypes=(BF16,))],
            in_cols // N_DEV, name=f"split_g_w_in_{i}")
        st_b, tok_b = _exchange_start([g_win], gather=False, name=f"scatter_start_b_{i}")
        dh_parts = [_mm(dp, lw["win"], tb=True, b_koff=off, name=f"bwd_proj_{nm}dx_{i}", after=[tok_b], out_dtypes=(BF16,))
                    for nm, dp, off in (("", dproj, 0), ("dt_", dpdt, 4 * d + xbc))]
        if i > 0:
            dx, ddn, dg_in, dg_below = _bwd_norm_pair(s["x0"], dh_parts, dx1, saved[i - 1]["dn"], vec("norm_mix_pre", i),
                                                      vec("norm_mlp_post", i - 1), name=f"bwd_norm_mix_pre_{i}")
            small_grads["norm_mlp_post"][i - 1] = dg_below
        else:
            dx, dg_in = _bwd_norm_in(s["x0"], dh_parts, dx1, vec("norm_mix_pre", i), name=f"bwd_norm_mix_pre_{i}")
        small_grads["norm_mix_pre"][i] = dg_in
        if pending is not None:
            finish(pending, [dx])
        pending = (i, st_a, st_b)

    grad_x = dx.reshape(nseq, seq, d)

    small_shapes_full = {n: (depth,) + tuple(small_grads[n][0].shape) for n in SMALL}
    gpack = _pack([jnp.stack(small_grads[n]) for n in SMALL])
    st_small, tok_small = _exchange_start([gpack], gather=True, name="allreduce_small_start")
    finish(pending, [dx, tok_small])
    gparts, = _exchange_wait(st_small, [big_out["w_in"][0]], name="allreduce_small_wait")

    def shard_of(n, full):
        if n == "conv_a_w":
            return lax.dynamic_slice_in_dim(full, me * ca_shard, ca_shard, axis=2)
        if n == "ssm_conv_w":
            return lax.dynamic_slice_in_dim(full, me * sc_shard, sc_shard, axis=2)
        return full.reshape(W[n].shape)

    gsum = _sum_parts(gparts, name="sum_small")
    gfull = _unpack(gsum, [small_shapes_full[n] for n in SMALL])
    gsmall = {n: shard_of(n, gf) for n, gf in zip(SMALL, gfull)}
    res = _sum_adamw(_pack([gsmall[n] for n in SMALL])[None], _pack([W[n] for n in SMALL]),
                     _pack([M[n] for n in SMALL]), _pack([V[n] for n in SMALL]), name="adamw_small")
    small_out = [dict(zip(SMALL, _unpack(r, [W[n].shape for n in SMALL]))) for r in res]

    def out_of(kind, n):
        return big_out[n][kind] if n in BIG else small_out[kind][n]

    return (loss, grad_x, *[out_of(k, n) for k in range(4) for n in ORDER])
```

```python
import jax
import jax.numpy as jnp
from jax import lax
from jax.experimental import pallas as pl
from jax.experimental.pallas import tpu as pltpu

F32 = jnp.float32
BF16 = jnp.bfloat16
HIGHEST = lax.Precision.HIGHEST
MESH = pl.DeviceIdType.MESH

EPS = 1e-6
HEAD_DIM = 64
D_STATE = 128
SSM_GROUPS = 2
CHUNK = 128
CONV_K = 3
SSM_CONV_K = 4
ADAM_LR = 0.001
ADAM_B1 = 0.9
ADAM_B2 = 0.999
ADAM_EPS = 1e-08
ADAM_WD = 0.01
ADAM_STEP = 10

N_DEV = 8
LANES = 128
SUBLANES = 8
VMEM_LIMIT = 48 * 1024 * 1024
ROW_TILE = 512
MM_TILE = 1024
MM_TILE_N = 2816
MM_VMEM_BUDGET = 40 * 1024 * 1024
FUSED_ROWS = 512


def _params(sem):
    return pltpu.CompilerParams(dimension_semantics=sem, vmem_limit_bytes=VMEM_LIMIT)


def _call(body, **kw):
    return pl.pallas_call(body, **kw)


def _pick(n, cap):
    best = None
    for t in range(LANES, min(n, cap) + 1, LANES):
        if n % t == 0:
            best = t
    return best or n


def _pick_rows(n, cap):
    best = None
    for t in range(SUBLANES, min(n, cap) + 1, SUBLANES):
        if n % t == 0:
            best = t
    return best or n


def _sigmoid(x):
    return 1.0 / (1.0 + jnp.exp(-x))


def _softplus(x):
    return jnp.maximum(x, 0.0) + jnp.log1p(jnp.exp(-jnp.abs(x)))


def _rms(x):
    return lax.rsqrt(jnp.mean(x * x, axis=-1, keepdims=True) + EPS)


def _rms_bwd(x, r, g, dy):
    gy = dy * g
    dx = r * gy - x * (r * r * r) * jnp.mean(gy * x, axis=-1, keepdims=True)
    return dx, dy * x * r


def _full(shape):
    return pl.BlockSpec(shape, lambda *_: (0,) * len(shape))


def _mm(a, b, *, name, ta=False, tb=False, out_dtypes=(F32,), epi=None, extras=(), n=None, b_off=0, b_koff=0,
        after=(), vecs=(), tm_cap=MM_TILE):
    m, k = (a.shape[1], a.shape[0]) if ta else a.shape
    if n is None:
        n = b.shape[0] if tb else b.shape[1]
    tm, tn, tk = _pick(m, tm_cap), _pick(n, MM_TILE_N), _pick(k, MM_TILE)
    while b_off % tn or n % tn:
        tn -= LANES
    if b_koff == 0 and k > MM_TILE:
        tk = _pick(k, MM_TILE_N)
    while b_koff % tk or k % tk:
        tk -= LANES

    def vmem_bytes(tk_):
        per_out = sum(jnp.dtype(dt).itemsize for dt in out_dtypes) + sum(e.dtype.itemsize for e in extras)
        return 2 * tk_ * (tm * a.dtype.itemsize + tn * b.dtype.itemsize) + tm * tn * (2 * per_out + 4)

    while vmem_bytes(tk) > MM_VMEM_BUDGET and tk % (2 * LANES) == 0 and not b_koff % (tk // 2):
        tk //= 2
    nk = k // tk
    nm, nn = m // tm, n // tn
    jo = b_off // tn
    ko = b_koff // tk
    a_bytes = m * k * a.dtype.itemsize
    b_bytes = n * k * b.dtype.itemsize
    m_outer = a_bytes + nm * b_bytes <= b_bytes + nn * a_bytes
    ij = (lambda g0, g1: (g0, g1)) if m_outer else (lambda g0, g1: (g1, g0))
    grid = (nm, nn, nk) if m_outer else (nn, nm, nk)

    def a_map(g0, g1, kk):
        i, _ = ij(g0, g1)
        return (kk, i) if ta else (i, kk)

    def b_map(g0, g1, kk):
        _, j = ij(g0, g1)
        return (j + jo, kk + ko) if tb else (kk + ko, j + jo)

    def o_map(g0, g1, kk):
        return ij(g0, g1)

    a_spec = pl.BlockSpec((tk, tm) if ta else (tm, tk), a_map)
    b_spec = pl.BlockSpec((tn, tk) if tb else (tk, tn), b_map)
    o_spec = pl.BlockSpec((tm, tn), o_map)
    dims = (((0 if ta else 1,), (1 if tb else 0,)), ((), ()))
    n_ex = len(extras) + len(vecs)
    after = list(after)
    o0 = 2 + n_ex + len(after)

    def finish(acc, ex, outs):
        res = (acc,) if epi is None else epi(acc, *[e[...] for e in ex])
        for o, r in zip(outs, res):
            o[...] = r.astype(o.dtype)

    def body_single(*refs):
        a_ref, b_ref = refs[:2]
        acc = lax.dot_general(a_ref[...].astype(BF16), b_ref[...].astype(BF16), dims, preferred_element_type=F32)
        finish(acc, refs[2:2 + n_ex], refs[o0:])

    def body_multi(*refs):
        a_ref, b_ref = refs[:2]
        acc = refs[-1]
        kk = pl.program_id(2)

        @pl.when(kk == 0)
        def _():
            acc[...] = jnp.zeros_like(acc)

        acc[...] += lax.dot_general(a_ref[...].astype(BF16), b_ref[...].astype(BF16), dims, preferred_element_type=F32)

        @pl.when(kk == nk - 1)
        def _():
            finish(acc[...], refs[2:2 + n_ex], refs[o0:-1])

    v_spec = pl.BlockSpec((1, tn), lambda g0, g1, kk: (0, ij(g0, g1)[1]))
    outs = _call(
        body_single if nk == 1 else body_multi, name=name, grid=grid,
        in_specs=([a_spec, b_spec] + [o_spec] * len(extras) + [v_spec] * len(vecs)
                  + [pl.BlockSpec(memory_space=pl.ANY)] * len(after)),
        out_specs=[o_spec] * len(out_dtypes),
        out_shape=[jax.ShapeDtypeStruct((m, n), dt) for dt in out_dtypes],
        scratch_shapes=[] if nk == 1 else [pltpu.VMEM((tm, tn), F32)],
        compiler_params=_params(("parallel", "parallel", "arbitrary")),
    )(a, b, *extras, *vecs, *after)
    return outs[0] if len(outs) == 1 else outs


def _epi_resid_norm(acc, x, g_res, g_next):
    xn = x + acc * _rms(acc) * g_res
    return xn, xn * _rms(xn) * g_next, acc


def _epi_relu2(acc):
    r = jnp.maximum(acc, 0.0)
    return (r * r,)


def _epi_drelu2(acc, f):
    return (acc * (2.0 * jnp.sqrt(f).astype(F32)),)


def _norm_fwd(x, g, *, name):
    t, d = x.shape
    tt = _pick_rows(t, ROW_TILE)

    def body(x_ref, g_ref, h_ref):
        xv = x_ref[...]
        h_ref[...] = (xv * _rms(xv) * g_ref[...]).astype(BF16)

    row = pl.BlockSpec((tt, d), lambda i: (i, 0))
    return _call(body, name=name, grid=(t // tt,), in_specs=[row, _full((1, d))], out_specs=row,
                 out_shape=jax.ShapeDtypeStruct((t, d), BF16), compiler_params=_params(("parallel",)))(x, g)


def _resid_norm(x, n, g1, g2, *, name):
    t, d = x.shape
    tt = _pick_rows(t, ROW_TILE)
    gains = [g1] if g2 is None else [g1, g2]

    def body(x_ref, n_ref, *refs):
        nv = n_ref[...].astype(F32)
        xn = x_ref[...] + nv * _rms(nv) * refs[0][...]
        refs[len(gains)][...] = xn
        if g2 is not None:
            refs[3][...] = (xn * _rms(xn) * refs[1][...]).astype(BF16)

    row = pl.BlockSpec((tt, d), lambda i: (i, 0))
    outs = _call(body, name=name, grid=(t // tt,), in_specs=[row, row] + [_full((1, d))] * len(gains),
                 out_specs=[row] * len(gains),
                 out_shape=[jax.ShapeDtypeStruct((t, d), F32), jax.ShapeDtypeStruct((t, d), BF16)][:len(gains)],
                 compiler_params=_params(("parallel",)))(x, n, *gains)
    return (outs[0], None) if g2 is None else outs


def _loss_fwd_bwd(xf, target, *, name):
    t, d = xf.shape
    tt = _pick_rows(t, ROW_TILE)
    nt = t // tt

    def body(x_ref, t_ref, dy_ref, loss_ref, acc):
        i = pl.program_id(0)

        @pl.when(i == 0)
        def _():
            acc[...] = jnp.zeros_like(acc)

        e = x_ref[...] - t_ref[...]
        dy_ref[...] = e * (1.0 / d)
        acc[...] += jnp.sum(e * e, axis=0, keepdims=True)

        @pl.when(i == nt - 1)
        def _():
            loss_ref[...] = jnp.sum(acc[...], axis=-1, keepdims=True) * (0.5 / d)

    row = pl.BlockSpec((tt, d), lambda i: (i, 0))
    return _call(body, name=name, grid=(nt,), in_specs=[row, row], out_specs=[row, _full((1, 1))],
                 out_shape=[jax.ShapeDtypeStruct((t, d), F32), jax.ShapeDtypeStruct((1, 1), F32)],
                 scratch_shapes=[pltpu.VMEM((1, d), F32)], compiler_params=_params(("arbitrary",)))(xf, target)


def _tail_loss(x, n, g, target, *, name):
    t, d = x.shape
    tt = _pick_rows(t, ROW_TILE)
    nt = t // tt

    def body(x_ref, n_ref, g_ref, t_ref, dy_ref, dn_ref, dg_ref, loss_ref, acc):
        i = pl.program_id(0)

        @pl.when(i == 0)
        def _():
            acc[...] = jnp.zeros_like(acc)
            dg_ref[...] = jnp.zeros_like(dg_ref)

        nv = n_ref[...].astype(F32)
        r = _rms(nv)
        e = (x_ref[...] + nv * r * g_ref[...]) - t_ref[...]
        dy = e * (1.0 / d)
        dy_ref[...] = dy
        acc[...] += jnp.sum(e * e, axis=0, keepdims=True)
        dn, dg = _rms_bwd(nv, r, g_ref[...], dy)
        dn_ref[...] = dn.astype(BF16)
        dg_ref[...] += jnp.sum(dg, axis=0, keepdims=True)

        @pl.when(i == nt - 1)
        def _():
            loss_ref[...] = jnp.sum(acc[...], axis=-1, keepdims=True) * (0.5 / d)

    row = pl.BlockSpec((tt, d), lambda i: (i, 0))
    vec = _full((1, d))
    return _call(body, name=name, grid=(nt,), in_specs=[row, row, vec, row],
                 out_specs=[row, row, vec, _full((1, 1))],
                 out_shape=[jax.ShapeDtypeStruct((t, d), F32), jax.ShapeDtypeStruct((t, d), BF16),
                            jax.ShapeDtypeStruct((1, d), F32), jax.ShapeDtypeStruct((1, 1), F32)],
                 scratch_shapes=[pltpu.VMEM((1, d), F32)], compiler_params=_params(("arbitrary",)))(x, n, g, target)


def _bwd_norm_pair(xin, dh, dres, n, g_in, g_out, *, name):
    t, d = xin.shape
    tt = _pick_rows(t, ROW_TILE)
    n_dh = len(dh)

    def body(*refs):
        x_ref = refs[0]
        dh_refs = refs[1:1 + n_dh]
        dres_ref, n_ref, gi_ref, go_ref, dx_ref, dn_ref, dgi_ref, dgo_ref = refs[1 + n_dh:]
        i = pl.program_id(0)

        @pl.when(i == 0)
        def _():
            dgi_ref[...] = jnp.zeros_like(dgi_ref)
            dgo_ref[...] = jnp.zeros_like(dgo_ref)

        xv = x_ref[...]
        dhv = dh_refs[0][...].astype(F32)
        for r in dh_refs[1:]:
            dhv = dhv + r[...].astype(F32)
        dxh, dgi = _rms_bwd(xv, _rms(xv), gi_ref[...], dhv)
        dx = dres_ref[...] + dxh
        dx_ref[...] = dx
        dgi_ref[...] += jnp.sum(dgi, axis=0, keepdims=True)
        nv = n_ref[...].astype(F32)
        dn, dgo = _rms_bwd(nv, _rms(nv), go_ref[...], dx)
        dn_ref[...] = dn.astype(BF16)
        dgo_ref[...] += jnp.sum(dgo, axis=0, keepdims=True)

    row = pl.BlockSpec((tt, d), lambda i: (i, 0))
    vec = _full((1, d))
    return _call(body, name=name, grid=(t // tt,), in_specs=[row] * (n_dh + 3) + [vec, vec],
                 out_specs=[row, row, vec, vec],
                 out_shape=[jax.ShapeDtypeStruct((t, d), F32), jax.ShapeDtypeStruct((t, d), BF16),
                            jax.ShapeDtypeStruct((1, d), F32), jax.ShapeDtypeStruct((1, d), F32)],
                 compiler_params=_params(("arbitrary",)))(xin, *dh, dres, n, g_in, g_out)


def _bwd_norm_in(xin, dh, dres, g_in, *, name):
    t, d = xin.shape
    tt = _pick_rows(t, ROW_TILE)
    n_dh = len(dh)

    def body(*refs):
        x_ref = refs[0]
        dh_refs = refs[1:1 + n_dh]
        dres_ref, gi_ref, dx_ref, dgi_ref = refs[1 + n_dh:]
        i = pl.program_id(0)

        @pl.when(i == 0)
        def _():
            dgi_ref[...] = jnp.zeros_like(dgi_ref)

        xv = x_ref[...]
        dhv = dh_refs[0][...].astype(F32)
        for r in dh_refs[1:]:
            dhv = dhv + r[...].astype(F32)
        dxh, dgi = _rms_bwd(xv, _rms(xv), gi_ref[...], dhv)
        dx_ref[...] = dres_ref[...] + dxh
        dgi_ref[...] += jnp.sum(dgi, axis=0, keepdims=True)

    row = pl.BlockSpec((tt, d), lambda i: (i, 0))
    vec = _full((1, d))
    return _call(body, name=name, grid=(t // tt,), in_specs=[row] * (n_dh + 2) + [vec],
                 out_specs=[row, vec],
                 out_shape=[jax.ShapeDtypeStruct((t, d), F32), jax.ShapeDtypeStruct((1, d), F32)],
                 compiler_params=_params(("arbitrary",)))(xin, *dh, dres, g_in)


def _bwd_norm_out(n, g_out, dx, *, name):
    t, d = n.shape
    tt = _pick_rows(t, ROW_TILE)

    def body(n_ref, go_ref, dx_ref, dn_ref, dgo_ref):
        i = pl.program_id(0)

        @pl.when(i == 0)
        def _():
            dgo_ref[...] = jnp.zeros_like(dgo_ref)

        nv = n_ref[...].astype(F32)
        dn, dgo = _rms_bwd(nv, _rms(nv), go_ref[...], dx_ref[...])
        dn_ref[...] = dn.astype(BF16)
        dgo_ref[...] += jnp.sum(dgo, axis=0, keepdims=True)

    row = pl.BlockSpec((tt, d), lambda i: (i, 0))
    vec = _full((1, d))
    return _call(body, name=name, grid=(t // tt,), in_specs=[row, vec, row], out_specs=[row, vec],
                 out_shape=[jax.ShapeDtypeStruct((t, d), BF16), jax.ShapeDtypeStruct((1, d), F32)],
                 compiler_params=_params(("arbitrary",)))(n, g_out, dx)


def _shift_down(cur, halo, s):
    return jnp.concatenate([halo[SUBLANES - s:], cur[:cur.shape[0] - s]], axis=0)


def _shift_up(cur, halo, s):
    return jnp.concatenate([cur[s:], halo[:s]], axis=0)


def _conva_fwd(pa, w, g, *, d, seq, name):
    t = pa.shape[0]
    tt = _pick_rows(seq, ROW_TILE)
    tps = seq // tt

    def body(xa_ref, ca_ref, ba_ref, w_ref, g_ref, ya_ref, v_ref, carry):
        i = pl.program_id(0)

        @pl.when(i % tps == 0)
        def _():
            carry[...] = jnp.zeros_like(carry)

        u = ca_ref[...].astype(F32) * xa_ref[...].astype(F32)
        halo = carry[...]
        wv = w_ref[...]
        v = wv[2:3] * u + wv[1:2] * _shift_down(u, halo, 1) + wv[0:1] * _shift_down(u, halo, 2)
        carry[...] = u[tt - SUBLANES:]
        yp = ba_ref[...].astype(F32) * v
        ya_ref[...] = (yp * _rms(yp) * g_ref[...]).astype(BF16)
        v_ref[...] = v.astype(BF16)

    col = lambda c: pl.BlockSpec((tt, d), lambda i, c=c: (i, c))
    row = pl.BlockSpec((tt, d), lambda i: (i, 0))
    return _call(body, name=name, grid=(t // tt,),
                 in_specs=[col(0), col(1), col(2), _full((CONV_K, d)), _full((1, d))], out_specs=[row, row],
                 out_shape=[jax.ShapeDtypeStruct((t, d), BF16), jax.ShapeDtypeStruct((t, d), BF16)],
                 scratch_shapes=[pltpu.VMEM((SUBLANES, d), F32)],
                 compiler_params=_params(("arbitrary",)))(pa, pa, pa, w, g)


def _conva_bwd(dcat, pa, v, w, g, *, d, seq, name):
    t, width = pa.shape
    d3 = 3 * d
    tt = _pick_rows(seq, ROW_TILE)
    tps = seq // tt
    nt = t // tt

    def body(dya_ref, xa_ref, ca_ref, ba_ref, v_ref, w_ref, g_ref, dpa_ref, dw_ref, dg_ref, carry):
        i = pl.program_id(0)

        @pl.when(i == 0)
        def _():
            dw_ref[...] = jnp.zeros_like(dw_ref)
            dg_ref[...] = jnp.zeros_like(dg_ref)

        @pl.when(i % tps == 0)
        def _():
            carry[...] = jnp.zeros_like(carry)

        xa, ca, ba, vv = [r[...].astype(F32) for r in (xa_ref, ca_ref, ba_ref, v_ref)]
        yp = ba * vv
        dyp, dgt = _rms_bwd(yp, _rms(yp), g_ref[...], dya_ref[...].astype(F32))
        dg_ref[...] += jnp.sum(dgt, axis=0, keepdims=True)
        dv = dyp * ba
        halo = carry[...]
        dv1 = _shift_up(dv, halo, 1)
        dv2 = _shift_up(dv, halo, 2)
        carry[...] = dv[:SUBLANES]
        wv = w_ref[...]
        du = wv[2:3] * dv + wv[1:2] * dv1 + wv[0:1] * dv2
        u = ca * xa
        dw_ref[0:1, :] += jnp.sum(u * dv2, axis=0, keepdims=True)
        dw_ref[1:2, :] += jnp.sum(u * dv1, axis=0, keepdims=True)
        dw_ref[2:3, :] += jnp.sum(u * dv, axis=0, keepdims=True)
        dpa_ref[:, 0:d] = (du * ca).astype(BF16)
        dpa_ref[:, d:2 * d] = (du * xa).astype(BF16)
        dpa_ref[:, 2 * d:3 * d] = (dyp * vv).astype(BF16)

    rcol = lambda c: pl.BlockSpec((tt, d), lambda i, c=c: (nt - 1 - i, c))
    return _call(body, name=name, grid=(nt,),
                 in_specs=[rcol(0), rcol(0), rcol(1), rcol(2), rcol(0), _full((CONV_K, d)), _full((1, d))],
                 out_specs=[pl.BlockSpec((tt, d3), lambda i: (nt - 1 - i, 0)), _full((CONV_K, d)), _full((1, d))],
                 out_shape=[jax.ShapeDtypeStruct((t, width), BF16), jax.ShapeDtypeStruct((CONV_K, d), F32),
                            jax.ShapeDtypeStruct((1, d), F32)],
                 scratch_shapes=[pltpu.VMEM((SUBLANES, d), F32)],
                 compiler_params=_params(("arbitrary",)))(dcat, pa, pa, pa, v, w, g)


CONV_CH = 512


def _convb_fwd(proj, w, bias, *, col0, seq, name):
    t = proj.shape[0]
    c = w.shape[1]
    cb = _pick(c, CONV_CH)
    assert col0 % cb == 0
    tt = _pick_rows(seq, 2 * ROW_TILE)
    tps = seq // tt

    def body(p_ref, w_ref, b_ref, o_ref, carry):
        i = pl.program_id(1)

        @pl.when(i % tps == 0)
        def _():
            carry[...] = jnp.zeros_like(carry)

        p = p_ref[...].astype(F32)
        halo = carry[...]
        wv = w_ref[...]
        o = wv[3:4] * p + b_ref[...]
        for s in (1, 2, 3):
            o = o + wv[3 - s:4 - s] * _shift_down(p, halo, s)
        carry[...] = p[tt - SUBLANES:]
        o_ref[...] = o.astype(BF16)

    return _call(body, name=name, grid=(c // cb, t // tt),
                 in_specs=[pl.BlockSpec((tt, cb), lambda jc, i: (i, col0 // cb + jc)),
                           pl.BlockSpec((SSM_CONV_K, cb), lambda jc, i: (0, jc)), pl.BlockSpec((1, cb), lambda jc, i: (0, jc))],
                 out_specs=pl.BlockSpec((tt, cb), lambda jc, i: (i, jc)), out_shape=jax.ShapeDtypeStruct((t, c), BF16),
                 scratch_shapes=[pltpu.VMEM((SUBLANES, cb), F32)],
                 compiler_params=_params(("arbitrary", "arbitrary")))(proj, w, bias)


def _convb_bwd(dconv, proj, w, dproj, *, col0, seq, name):
    t, c = dconv.shape
    cb = _pick(c, CONV_CH)
    assert col0 % cb == 0
    tt = _pick_rows(seq, 2 * ROW_TILE)
    tps = seq // tt
    nt = t // tt

    def body(dc_ref, p_ref, w_ref, dproj_in, dp_ref, dw_ref, db_ref, carry):
        del dproj_in
        i = pl.program_id(1)

        @pl.when(i == 0)
        def _():
            dw_ref[...] = jnp.zeros_like(dw_ref)
            db_ref[...] = jnp.zeros_like(db_ref)

        @pl.when(i % tps == 0)
        def _():
            carry[...] = jnp.zeros_like(carry)

        dc = dc_ref[...].astype(F32)
        p = p_ref[...].astype(F32)
        halo = carry[...]
        wv = w_ref[...]
        dp = wv[3:4] * dc
        dw_ref[3:4, :] += jnp.sum(p * dc, axis=0, keepdims=True)
        for s in (1, 2, 3):
            dcs = _shift_up(dc, halo, s)
            dp = dp + wv[3 - s:4 - s] * dcs
            dw_ref[3 - s:4 - s, :] += jnp.sum(p * dcs, axis=0, keepdims=True)
        carry[...] = dc[:SUBLANES]
        db_ref[...] += jnp.sum(dc, axis=0, keepdims=True)
        dp_ref[...] = dp.astype(BF16)

    win_spec = pl.BlockSpec((tt, cb), lambda jc, i: (nt - 1 - i, col0 // cb + jc))
    taps = pl.BlockSpec((SSM_CONV_K, cb), lambda jc, i: (0, jc))
    return _call(body, name=name, grid=(c // cb, nt),
                 in_specs=[pl.BlockSpec((tt, cb), lambda jc, i: (nt - 1 - i, jc)), win_spec, taps,
                           pl.BlockSpec(memory_space=pl.ANY)],
                 out_specs=[win_spec, taps, pl.BlockSpec((1, cb), lambda jc, i: (0, jc))],
                 out_shape=[jax.ShapeDtypeStruct(dproj.shape, BF16), jax.ShapeDtypeStruct((SSM_CONV_K, c), F32),
                            jax.ShapeDtypeStruct((1, c), F32)],
                 input_output_aliases={3: 0},
                 scratch_shapes=[pltpu.VMEM((SUBLANES, cb), F32)],
                 compiler_params=_params(("arbitrary", "arbitrary")))(dconv, proj, w, dproj)


def _expand_heads(x, ev):
    return jnp.dot(x, ev, precision=HIGHEST, preferred_element_type=F32)


def _head_sums(v, ev):
    return lax.dot_general(v, ev, (((1,), (1,)), ((), ())), precision=HIGHEST, preferred_element_type=F32)


def _ssd_common(c_ref, pdt_ref, dtb_ref, alog_ref, e_ref, h):
    cp = c_ref[...].astype(F32)
    sg = _sigmoid(cp)
    act = cp * sg
    pre = pdt_ref[:, 0:h] + dtb_ref[...]
    dt = _softplus(pre)
    a = -jnp.exp(alog_ref[...])
    adt = dt * a
    row = lax.broadcasted_iota(jnp.int32, (CHUNK, CHUNK), 0)
    col = lax.broadcasted_iota(jnp.int32, (CHUNK, CHUNK), 1)
    tril = row >= col
    cs = jnp.dot(tril.astype(F32), adt, precision=HIGHEST, preferred_element_type=F32)
    cs_t = lax.dot_general(adt, (col >= row).astype(F32), (((0,), (0,)), ((), ())), precision=HIGHEST,
                           preferred_element_type=F32)
    ev = e_ref[...]
    dt_l = _expand_heads(dt, ev)
    ecs_l = jnp.exp(_expand_heads(cs, ev))
    return dict(cp=cp, sg=sg, act=act, pre=pre, dt=dt, a=a, cs=cs, cs_t=cs_t, dt_l=dt_l, ecs_l=ecs_l,
                tril=tril, row=row, col=col, lo=col < HEAD_DIM)


def _dot_nt(a, b):
    return lax.dot_general(a, b, (((1,), (1,)), ((), ())), preferred_element_type=F32)


def _dot_tn(a, b):
    return lax.dot_general(a, b, (((0,), (0,)), ((), ())), preferred_element_type=F32)


def _dot(a, b):
    return jnp.dot(a, b, preferred_element_type=F32)


def _ssd_fwd(cpre, pdt, pz, ya, dtb, alog, dsk_lane, gs, emat, *, nseq, seq, name):
    t, xbc = cpre.shape
    d = ya.shape[1]
    h = d // HEAD_DIM
    npair = h // 2
    ppg = npair // SSM_GROUPS
    nc = seq // CHUNK
    gw = d // SSM_GROUPS
    bc0 = d
    cc0 = d + SSM_GROUPS * D_STATE

    def body(c_ref, pdt_ref, z_ref, ya_ref, dtb_ref, alog_ref, dsk_ref, gs_ref, e_ref, cat_ref, y2_ref, hp_ref, h_ref):
        @pl.when(pl.program_id(0) == 0)
        def _():
            h_ref[...] = jnp.zeros_like(h_ref)

        for sq in range(nseq):
            one_seq(c_ref.at[sq], pdt_ref.at[sq], z_ref.at[sq], ya_ref.at[sq], dtb_ref, alog_ref, dsk_ref, gs_ref, e_ref,
                    cat_ref.at[sq], y2_ref.at[sq], hp_ref.at[sq], h_ref.at[sq])

    def one_seq(c_ref, pdt_ref, z_ref, ya_ref, dtb_ref, alog_ref, dsk_ref, gs_ref, e_ref, cat_ref, y2_ref, hp_ref, h_ref):
        q = _ssd_common(c_ref, pdt_ref, dtb_ref, alog_ref, e_ref, h)
        act, cs, lo, ecs_l = q["act"], q["cs"], q["lo"], q["ecs_l"]
        xs = act[:, :d]
        xd = xs * q["dt_l"]
        ys = []
        for g in range(SSM_GROUPS):
            bg = act[:, bc0 + g * D_STATE: bc0 + (g + 1) * D_STATE]
            cgb = act[:, cc0 + g * D_STATE: cc0 + (g + 1) * D_STATE].astype(BF16)
            s = _dot_nt(cgb, bg.astype(BF16))
            bg_t = bg.T
            for jj in range(ppg):
                j = g * ppg + jj
                sl = slice(LANES * j, LANES * (j + 1))
                xdj = xd[:, sl]
                x2 = jnp.concatenate([jnp.where(lo, xdj, 0.0), jnp.where(lo, 0.0, xdj)], axis=0).astype(BF16)
                hprev = h_ref[j]
                hp_ref[j] = hprev.astype(BF16)
                ms, bws_t = [], []
                for hh in (2 * j, 2 * j + 1):
                    csc = cs[:, hh:hh + 1]
                    cs_row = q["cs_t"][hh:hh + 1, :]
                    seg = jnp.broadcast_to(csc, (CHUNK, CHUNK)) - jnp.broadcast_to(cs_row, (CHUNK, CHUNK))
                    ms.append(s * jnp.exp(jnp.where(q["tril"], seg, -jnp.inf)))
                    bws_t.append(bg_t * jnp.exp(cs_row[:, CHUNK - 1:CHUNK] - cs_row))
                ydiag = _dot(jnp.concatenate(ms, axis=1).astype(BF16), x2)
                st = _dot(jnp.concatenate(bws_t, axis=1).astype(BF16), x2)
                ecs = ecs_l[:, sl]
                yoff = _dot(cgb, hprev.astype(BF16)) * ecs
                h_ref[j] = hprev * ecs[CHUNK - 1:CHUNK] + st
                ys.append(ydiag + yoff)
        y = jnp.concatenate(ys, axis=1) + dsk_ref[...] * xs
        y2_ref[...] = y.astype(BF16)
        zv = z_ref[...].astype(F32)
        y3 = y * (zv * _sigmoid(zv))
        cat_ref[:, 0:d] = ya_ref[...]
        for gi in range(SSM_GROUPS):
            seg = y3[:, gi * gw:(gi + 1) * gw]
            cat_ref[:, d + gi * gw:d + (gi + 1) * gw] = (seg * _rms(seg) * gs_ref[:, gi * gw:(gi + 1) * gw]).astype(BF16)

    chunk = lambda w, cb=0: pl.BlockSpec((nseq, CHUNK, w), lambda c, cb=cb: (0, c, cb))
    vec = lambda w: pl.BlockSpec((1, w), lambda c: (0, 0))
    hp_spec = pl.BlockSpec((nseq, None, npair, D_STATE, LANES), lambda c: (0, c, 0, 0, 0))
    per_seq = lambda a: a.reshape(nseq, seq, a.shape[1])
    cat, y2, hp = _call(
        body, name=name, grid=(nc,),
        in_specs=[chunk(xbc), chunk(LANES), chunk(d, 3), chunk(d), vec(h), vec(h), vec(d), vec(d),
                  pl.BlockSpec((h, d), lambda c: (0, 0))],
        out_specs=[chunk(2 * d), chunk(d), hp_spec],
        out_shape=[jax.ShapeDtypeStruct((nseq, seq, 2 * d), BF16), jax.ShapeDtypeStruct((nseq, seq, d), BF16),
                   jax.ShapeDtypeStruct((nseq, nc, npair, D_STATE, LANES), BF16)],
        scratch_shapes=[pltpu.VMEM((nseq, npair, D_STATE, LANES), F32)],
        compiler_params=_params(("arbitrary",)))(
            per_seq(cpre), per_seq(pdt), per_seq(pz), per_seq(ya), dtb, alog, dsk_lane, gs, emat)
    return cat.reshape(t, 2 * d), y2.reshape(t, d), hp


def _ssd_bwd(cpre, pdt, pz, y2, hprev_all, dcat, dtb, alog, dsk_lane, gs, emat, dproj, *, nseq, seq, name):
    t, xbc = cpre.shape
    d = y2.shape[1]
    h = d // HEAD_DIM
    npair = h // 2
    ppg = npair // SSM_GROUPS
    nc = seq // CHUNK
    gw = d // SSM_GROUPS
    bc0 = d
    cc0 = d + SSM_GROUPS * D_STATE

    def body(c_ref, pdt_ref, z_ref, y2_ref, hp_ref, dys_ref, dtb_ref, alog_ref, dsk_ref, gs_ref, e_ref, dproj_in,
             dconv_ref, dz_ref, dpdt_ref, dgs_ref, ddsk_ref, ddtb_ref, dalog_ref, dh_ref):
        del dproj_in
        b = pl.program_id(0)
        c = pl.program_id(1)

        @pl.when(c == 0)
        def _():
            dh_ref[...] = jnp.zeros_like(dh_ref)

        @pl.when((b == 0) & (c == 0))
        def _():
            dgs_ref[...] = jnp.zeros_like(dgs_ref)
            ddsk_ref[...] = jnp.zeros_like(ddsk_ref)
            ddtb_ref[...] = jnp.zeros_like(ddtb_ref)
            dalog_ref[...] = jnp.zeros_like(dalog_ref)

        q = _ssd_common(c_ref, pdt_ref, dtb_ref, alog_ref, e_ref, h)
        cp, sg, act, cs, a, dt, lo = q["cp"], q["sg"], q["act"], q["cs"], q["a"], q["dt"], q["lo"]
        ecs_l, dt_l = q["ecs_l"], q["dt_l"]
        ev = e_ref[...]
        xs = act[:, :d]
        xd = xs * dt_l
        row16 = lax.broadcasted_iota(jnp.int32, (CHUNK, h), 0)
        hid = lax.broadcasted_iota(jnp.int32, (1, h), 1)
        hid_t = lax.broadcasted_iota(jnp.int32, (h, 1), 0)

        zv = z_ref[...].astype(F32)
        sz = _sigmoid(zv)
        siluz = zv * sz
        y2v = y2_ref[...].astype(F32)
        y3 = y2v * siluz
        dysv = dys_ref[...].astype(F32)
        dy3s = []
        for gi in range(SSM_GROUPS):
            gsl = slice(gi * gw, (gi + 1) * gw)
            seg = y3[:, gsl]
            dseg, dgt = _rms_bwd(seg, _rms(seg), gs_ref[:, gsl], dysv[:, gsl])
            dy3s.append(dseg)
            dgs_ref[:, gsl] += jnp.sum(dgt, axis=0, keepdims=True)
        dy3 = jnp.concatenate(dy3s, axis=1)
        dy = dy3 * siluz
        dz_ref[...] = (dy3 * y2v * (sz * (1.0 + zv * (1.0 - sz)))).astype(BF16)
        ddsk_ref[...] += jnp.sum(_head_sums(dy * xs, ev), axis=0, keepdims=True)

        dcs = jnp.zeros((CHUNK, h), F32)
        dcs_t = jnp.zeros((h, CHUNK), F32)
        dxd_parts, yoff_parts, db_parts, dc_parts = [], [], [], []
        for g in range(SSM_GROUPS):
            bg = act[:, bc0 + g * D_STATE: bc0 + (g + 1) * D_STATE]
            cg = act[:, cc0 + g * D_STATE: cc0 + (g + 1) * D_STATE]
            bgb, cgb = bg.astype(BF16), cg.astype(BF16)
            cgb_t = cg.T.astype(BF16)
            s = _dot_nt(cgb, bgb)
            ds = jnp.zeros((CHUNK, CHUNK), F32)
            dbg = jnp.zeros((CHUNK, D_STATE), F32)
            dcg = jnp.zeros((CHUNK, D_STATE), F32)
            for jj in range(ppg):
                j = g * ppg + jj
                sl = slice(LANES * j, LANES * (j + 1))
                xdj = xd[:, sl]
                xdb = xdj.astype(BF16)
                x2 = jnp.concatenate([jnp.where(lo, xdj, 0.0), jnp.where(lo, 0.0, xdj)], axis=0).astype(BF16)
                dyj = dy[:, sl]
                dy2 = jnp.concatenate([jnp.where(lo, dyj, 0.0), jnp.where(lo, 0.0, dyj)], axis=0).astype(BF16)
                hpb = hp_ref[j]
                hprev = hpb.astype(F32)
                dhn = dh_ref[j]
                dhb = dhn.astype(BF16)
                dh2 = jnp.concatenate([jnp.where(lo, dhn, 0.0), jnp.where(lo, 0.0, dhn)], axis=0).astype(BF16)
                ecs = ecs_l[:, sl]
                gmat = (dyj * ecs).astype(BF16)
                yoff_parts.append(_dot(cgb, hpb) * ecs)
                dcg = dcg + _dot_nt(gmat, hpb)
                dh_ref[j] = dhn * ecs[CHUNK - 1:CHUNK] + _dot(cgb_t, gmat)
                t2 = dhn * hprev
                dbw2 = _dot_nt(x2, dhb)
                dm2 = _dot_nt(dy2, xdb)
                ms, bws = [], []
                for idx, hh in enumerate((2 * j, 2 * j + 1)):
                    msk = lo if idx == 0 else jnp.logical_not(lo)
                    onehot = (hid == hh).astype(F32)
                    csc = cs[:, hh:hh + 1]
                    seg = jnp.broadcast_to(csc, (CHUNK, CHUNK)) - jnp.broadcast_to(q["cs_t"][hh:hh + 1, :], (CHUNK, CHUNK))
                    lm = jnp.exp(jnp.where(q["tril"], seg, -jnp.inf))
                    m = s * lm
                    cs_last = cs[CHUNK - 1:CHUNK, hh:hh + 1]
                    dte = jnp.exp(cs_last - csc)
                    ms.append(m)
                    bws.append(bg * dte)
                    dbw = dbw2[idx * CHUNK:(idx + 1) * CHUNK]
                    dbg = dbg + dbw * dte
                    qv = jnp.sum(dbw * bg, axis=-1, keepdims=True) * dte
                    dm = dm2[idx * CHUNK:(idx + 1) * CHUNK]
                    wm = dm * m
                    rc = jnp.sum(wm, axis=-1, keepdims=True)
                    dcs_t = dcs_t - (hid_t == hh).astype(F32) * jnp.sum(wm, axis=0, keepdims=True)
                    ds = ds + dm * lm
                    ddec = jnp.sum(jnp.where(msk, t2, 0.0)) * jnp.exp(cs_last)
                    last = jnp.sum(qv) + ddec
                    dcs = dcs + (rc - qv) * onehot + jnp.where(row16 == CHUNK - 1, last * onehot, 0.0)
                dxd_s = _dot(jnp.concatenate(bws, axis=1).astype(BF16), dh2)
                dxd_d = _dot_tn(jnp.concatenate(ms, axis=0).astype(BF16), dy2)
                dxd_parts.append(dxd_s + dxd_d)
            dsb = ds.astype(BF16)
            dc_parts.append(dcg + _dot(dsb, bgb))
            db_parts.append(dbg + _dot_tn(dsb, cgb))
        yoff_all = jnp.concatenate(yoff_parts, axis=1)
        dxd_all = jnp.concatenate(dxd_parts, axis=1)
        dcs = dcs + _head_sums(dy * yoff_all, ev)
        triu = (q["col"] >= q["row"]).astype(F32)
        dadt = (jnp.dot(triu, dcs, precision=HIGHEST, preferred_element_type=F32)
                + lax.dot_general(triu, dcs_t, (((1,), (1,)), ((), ())), precision=HIGHEST, preferred_element_type=F32))
        ddt = dadt * a + _head_sums(dxd_all * xs, ev)
        dalog_ref[...] += jnp.sum(dadt * dt, axis=0, keepdims=True) * a
        dpre = ddt * _sigmoid(q["pre"])
        ddtb_ref[...] += jnp.sum(dpre, axis=0, keepdims=True)
        dpdt_ref[...] = jnp.zeros_like(dpdt_ref)
        dpdt_ref[:, 0:h] = dpre.astype(BF16)
        dxs = dxd_all * dt_l + dy * dsk_ref[...]
        dact = jnp.concatenate([dxs] + db_parts + dc_parts, axis=1)
        dconv_ref[...] = (dact * (sg * (1.0 + cp * (1.0 - sg)))).astype(BF16)

    rchunk = lambda w, cb=0: pl.BlockSpec((CHUNK, w), lambda b, c, cb=cb: (b * nc + nc - 1 - c, cb))
    vec = lambda w: pl.BlockSpec((1, w), lambda b, c: (0, 0))
    hp_spec = pl.BlockSpec((None, None, npair, D_STATE, LANES), lambda b, c: (b, nc - 1 - c, 0, 0, 0))
    return _call(body, name=name, grid=(nseq, nc),
                 in_specs=[rchunk(xbc), rchunk(LANES), rchunk(d, 3), rchunk(d), hp_spec, rchunk(d, 1),
                           vec(h), vec(h), vec(d), vec(d), pl.BlockSpec((h, d), lambda b, c: (0, 0)),
                           pl.BlockSpec(memory_space=pl.ANY)],
                 out_specs=[rchunk(xbc), rchunk(d, 3), rchunk(LANES), vec(d), vec(h), vec(h), vec(h)],
                 out_shape=[jax.ShapeDtypeStruct((t, xbc), BF16), jax.ShapeDtypeStruct(dproj.shape, BF16),
                            jax.ShapeDtypeStruct((t, LANES), BF16), jax.ShapeDtypeStruct((1, d), F32),
                            jax.ShapeDtypeStruct((1, h), F32), jax.ShapeDtypeStruct((1, h), F32),
                            jax.ShapeDtypeStruct((1, h), F32)],
                 input_output_aliases={11: 1},
                 scratch_shapes=[pltpu.VMEM((npair, D_STATE, LANES), F32)],
                 compiler_params=_params(("arbitrary", "arbitrary")))(
                     cpre, pdt, pz, y2, hprev_all, dcat, dtb, alog, dsk_lane, gs, emat, dproj)


def _sum_adamw(parts, w, m, v, *, name, layer=None, outs=None):
    n, r, c = parts.shape
    tr = _pick_rows(r, 256)
    bc1 = 1.0 - ADAM_B1 ** ADAM_STEP
    bc2 = 1.0 - ADAM_B2 ** ADAM_STEP

    def body(p_ref, w_ref, m_ref, v_ref, *rest):
        g_ref, d_ref, mo_ref, vo_ref = rest[-4:]
        g = p_ref[0].astype(F32)
        for k in range(1, n):
            g = g + p_ref[k].astype(F32)
        mn = ADAM_B1 * m_ref[...] + (1.0 - ADAM_B1) * g
        vn = ADAM_B2 * v_ref[...] + (1.0 - ADAM_B2) * (g * g)
        g_ref[...] = g
        mo_ref[...] = mn
        vo_ref[...] = vn
        d_ref[...] = -ADAM_LR * ((mn / bc1) / (jnp.sqrt(vn / bc2) + ADAM_EPS) + ADAM_WD * w_ref[...])

    p_spec = pl.BlockSpec((n, tr, c), lambda i: (0, i, 0))
    if layer is None:
        blk = pl.BlockSpec((tr, c), lambda i: (i, 0))
        return _call(body, name=name, grid=(r // tr,), in_specs=[p_spec, blk, blk, blk], out_specs=[blk] * 4,
                     out_shape=[jax.ShapeDtypeStruct((r, c), F32)] * 4,
                     compiler_params=_params(("parallel",)))(parts, w, m, v)
    blk = pl.BlockSpec((None, tr, c), lambda i: (layer, i, 0))
    if outs is None:
        outs = [lax.empty(w.shape, F32) for _ in range(4)]
    return _call(body, name=name, grid=(r // tr,),
                 in_specs=[p_spec, blk, blk, blk] + [pl.BlockSpec(memory_space=pl.ANY)] * 4, out_specs=[blk] * 4,
                 out_shape=[jax.ShapeDtypeStruct(w.shape, F32)] * 4, input_output_aliases={4 + k: k for k in range(4)},
                 compiler_params=_params(("parallel",)))(parts, w, m, v, *outs)


def _assemble_cols(blocks, *, name):
    nb, r, c = blocks.shape
    width = -(-nb * c // LANES) * LANES
    tr = _pick_rows(r, 256)

    def body(b_ref, o_ref):
        pieces = [b_ref[j] for j in range(nb)]
        if width > nb * c:
            pieces.append(jnp.zeros((tr, width - nb * c), blocks.dtype))
        o_ref[...] = jnp.concatenate(pieces, axis=1)

    return _call(body, name=name, grid=(r // tr,), in_specs=[pl.BlockSpec((nb, tr, c), lambda i: (0, i, 0))],
                 out_specs=pl.BlockSpec((tr, width), lambda i: (i, 0)), out_shape=jax.ShapeDtypeStruct((r, width), blocks.dtype),
                 compiler_params=_params(("parallel",)))(blocks)


def _split_cols(pieces, c, *, name):
    r = pieces[0].shape[0]
    tr = _pick_rows(r, 256)
    n_in = len(pieces)

    def body(*refs):
        o_ref = refs[n_in]
        x = jnp.concatenate([p[...] for p in refs[:n_in]], axis=1) if n_in > 1 else refs[0][...]
        for j in range(N_DEV):
            o_ref[j] = x[:, c * j:c * (j + 1)]

    return _call(body, name=name, grid=(r // tr,),
                 in_specs=[pl.BlockSpec((tr, p.shape[1]), lambda i: (i, 0)) for p in pieces],
                 out_specs=pl.BlockSpec((N_DEV, tr, c), lambda i: (0, i, 0)),
                 out_shape=jax.ShapeDtypeStruct((N_DEV, r, c), pieces[0].dtype),
                 compiler_params=_params(("parallel",)))(*pieces)


def _sum_parts(parts, *, name):
    n, r, c = parts.shape
    tr = _pick_rows(r, 256)

    def body(p_ref, g_ref):
        g = p_ref[0].astype(F32)
        for k in range(1, n):
            g = g + p_ref[k].astype(F32)
        g_ref[...] = g

    return _call(body, name=name, grid=(r // tr,), in_specs=[pl.BlockSpec((n, tr, c), lambda i: (0, i, 0))],
                 out_specs=pl.BlockSpec((tr, c), lambda i: (i, 0)), out_shape=jax.ShapeDtypeStruct((r, c), F32),
                 compiler_params=_params(("parallel",)))(parts)


def _peers():
    x, y, c = lax.axis_index("x"), lax.axis_index("y"), lax.axis_index("c")
    me = 4 * x + 2 * y + c
    out = []
    for k in range(1, N_DEV):
        px = (1 - x) if (k >> 2) & 1 else x
        py = (1 - y) if (k >> 1) & 1 else y
        pc = (1 - c) if k & 1 else c
        out.append(((px, py, pc), 4 * px + 2 * py + pc))
    return me, out


_HBM = pl.BlockSpec(memory_space=pltpu.HBM)
_SEM = pl.BlockSpec(memory_space=pltpu.SEMAPHORE)
_EFFECT = pltpu.SideEffectType.DATAFLOW_SIDE_EFFECTING


ALL_PEERS = tuple(range(1, N_DEV))
SAME_CORE_PEERS = (2, 4, 6)


def _slot(ref, j, c):
    if c is None:
        return ref.at[j]
    start = j * c
    return ref.at[:, pl.ds(start if isinstance(start, int) else pl.multiple_of(start, c), c)]


def _split_copies(s_refs, l_refs, send_sems, recv_sems, gather, incoming, ks, src_cols, land_cols):
    me, peers = _peers()
    local, remote = [], []
    for ti, (s_ref, l_ref) in enumerate(zip(s_refs, l_refs)):
        base = ti * N_DEV
        sc, lc = src_cols[ti], land_cols[ti]
        local.append(pltpu.make_async_copy(s_ref if gather else _slot(s_ref, me, sc), _slot(l_ref, me, lc),
                                           recv_sems.at[base + N_DEV - 1]))
        for k, (dev, pid) in enumerate(peers):
            if k + 1 not in ks:
                continue
            sems = dict(send_sem=send_sems.at[base + k], recv_sem=recv_sems.at[base + k], device_id=dev, device_id_type=MESH)
            src = s_ref if gather else _slot(s_ref, pid, sc)
            remote.append((
                pltpu.make_async_remote_copy(src_ref=src, dst_ref=_slot(l_ref, me, lc), **sems),
                pltpu.make_async_remote_copy(src_ref=src, dst_ref=_slot(l_ref, pid, lc), **sems) if incoming else None))
    return local, remote


def _exchange_start(srcs, *, gather, name, after=(), ks=ALL_PEERS, src_cols=None, land_cols=None):
    n = len(srcs)
    after = list(after)
    src_cols = list(src_cols or [None] * n)
    land_cols = list(land_cols or [None] * n)
    srcs = [pltpu.with_memory_space_constraint(s, pltpu.HBM) for s in srcs]

    def land_shape(s, sc, lc):
        block = tuple(s.shape) if gather else ((s.shape[0], sc) if sc else tuple(s.shape[1:]))
        return (block[0], N_DEV * lc) if lc else (N_DEV,) + block

    lands = [pltpu.with_memory_space_constraint(lax.empty(land_shape(s, sc, lc), s.dtype), pltpu.HBM)
             for s, sc, lc in zip(srcs, src_cols, land_cols)]

    def body(*refs):
        s_refs, l_refs = refs[:n], refs[n:2 * n]
        outs = refs[2 * n + len(after):]
        send_sems, recv_sems, token = outs[0], outs[1], outs[-1]
        local, remote = _split_copies(s_refs, l_refs, send_sems, recv_sems, gather, False, ks, src_cols, land_cols)
        for cp in local:
            cp.start()
        for out_cp, _ in remote:
            out_cp.start()
        token[...] = jnp.zeros_like(token)

    outs = _call(
        body, name=name,
        out_shape=(pltpu.SemaphoreType.DMA((n * N_DEV,)), pltpu.SemaphoreType.DMA((n * N_DEV,)),
                   *[pltpu.HBM(s.shape, s.dtype) for s in srcs], *[pltpu.HBM(l.shape, l.dtype) for l in lands],
                   jax.ShapeDtypeStruct((SUBLANES, LANES), F32)),
        in_specs=[_HBM] * (2 * n) + [pl.BlockSpec(memory_space=pl.ANY)] * len(after),
        out_specs=(_SEM, _SEM, *[_HBM] * (2 * n), pl.BlockSpec(memory_space=pltpu.VMEM)),
        input_output_aliases={k: k + 2 for k in range(2 * n)},
        compiler_params=pltpu.CompilerParams(has_side_effects=_EFFECT),
    )(*srcs, *lands, *after)
    return dict(n=n, gather=gather, ks=ks, src_cols=src_cols, land_cols=land_cols, sems=outs[:2], srcs=outs[2:2 + n],
                lands=outs[2 + n:2 + 2 * n]), outs[-1]


def _exchange_wait(state, after, *, name):
    n, gather, ks = state["n"], state["gather"], state["ks"]
    after = list(after)

    def body(*refs):
        s_refs, l_refs = refs[:n], refs[n:2 * n]
        send_sems, recv_sems = refs[2 * n], refs[2 * n + 1]
        local, remote = _split_copies(s_refs, l_refs, send_sems, recv_sems, gather, True, ks, state["src_cols"],
                                      state["land_cols"])
        for out_cp, in_cp in remote:
            out_cp.wait_send()
            in_cp.wait_recv()
        for cp in local:
            cp.wait()

    outs = _call(
        body, name=name,
        out_shape=tuple(pltpu.HBM(a.shape, a.dtype) for a in (*state["srcs"], *state["lands"])),
        in_specs=[_HBM] * (2 * n) + [_SEM, _SEM] + [pl.BlockSpec(memory_space=pl.ANY)] * len(after),
        out_specs=tuple([_HBM] * (2 * n)),
        input_output_aliases={k: k for k in range(2 * n)},
        compiler_params=pltpu.CompilerParams(has_side_effects=_EFFECT),
    )(*state["srcs"], *state["lands"], *state["sems"], *after)
    return outs[n:]


def _sibling_copies(l_refs, send_sems, recv_sems, incoming, land_cols):
    x, y, c = lax.axis_index("x"), lax.axis_index("y"), lax.axis_index("c")
    out = []
    for ti, l_ref in enumerate(l_refs):
        for q in range(4):
            px = (1 - x) if q & 2 else x
            py = (1 - y) if q & 1 else y
            mine = _slot(l_ref, 4 * px + 2 * py + c, land_cols[ti])
            theirs = _slot(l_ref, 4 * px + 2 * py + (1 - c), land_cols[ti])
            sems = dict(send_sem=send_sems.at[4 * ti + q], recv_sem=recv_sems.at[4 * ti + q],
                        device_id=(x, y, 1 - c), device_id_type=MESH)
            out.append((
                pltpu.make_async_remote_copy(src_ref=mine, dst_ref=mine, **sems),
                pltpu.make_async_remote_copy(src_ref=mine, dst_ref=theirs, **sems) if incoming else None))
    return out


def _sibling_start(lands, *, name, after=(), land_cols=None):
    n = len(lands)
    after = list(after)
    land_cols = list(land_cols or [None] * n)
    lands = [pltpu.with_memory_space_constraint(l, pltpu.HBM) for l in lands]

    def body(*refs):
        l_refs = refs[:n]
        outs = refs[n + len(after):]
        for out_cp, _ in _sibling_copies(l_refs, outs[0], outs[1], False, land_cols):
            out_cp.start()
        outs[-1][...] = jnp.zeros_like(outs[-1])

    outs = _call(
        body, name=name,
        out_shape=(pltpu.SemaphoreType.DMA((4 * n,)), pltpu.SemaphoreType.DMA((4 * n,)),
                   *[pltpu.HBM(l.shape, l.dtype) for l in lands], jax.ShapeDtypeStruct((SUBLANES, LANES), F32)),
        in_specs=[_HBM] * n + [pl.BlockSpec(memory_space=pl.ANY)] * len(after),
        out_specs=(_SEM, _SEM, *[_HBM] * n, pl.BlockSpec(memory_space=pltpu.VMEM)),
        input_output_aliases={k: k + 2 for k in range(n)},
        compiler_params=pltpu.CompilerParams(has_side_effects=_EFFECT),
    )(*lands, *after)
    return dict(n=n, land_cols=land_cols, sems=outs[:2], lands=outs[2:2 + n]), outs[-1]


def _sibling_wait(state, after, *, name):
    n = state["n"]
    after = list(after)

    def body(*refs):
        l_refs = refs[:n]
        for out_cp, in_cp in _sibling_copies(l_refs, refs[n], refs[n + 1], True, state["land_cols"]):
            out_cp.wait_send()
            in_cp.wait_recv()

    return _call(
        body, name=name,
        out_shape=tuple(pltpu.HBM(a.shape, a.dtype) for a in state["lands"]),
        in_specs=[_HBM] * n + [_SEM, _SEM] + [pl.BlockSpec(memory_space=pl.ANY)] * len(after),
        out_specs=tuple([_HBM] * n), input_output_aliases={k: k for k in range(n)},
        compiler_params=pltpu.CompilerParams(has_side_effects=_EFFECT),
    )(*state["lands"], *state["sems"], *after)


def _pack(arrs):
    flat = jnp.concatenate([a.reshape(-1).astype(F32) for a in arrs])
    pad = (-flat.shape[0]) % (SUBLANES * LANES)
    return jnp.pad(flat, (0, pad)).reshape(-1, LANES)


def _unpack(packed, shapes):
    flat = packed.reshape(-1)
    out, off = [], 0
    for s in shapes:
        n = 1
        for v in s:
            n *= v
        out.append(flat[off:off + n].reshape(s))
        off += n
    return out


SMALL = ("norm_mix_pre", "ssm_conv_b", "dt_bias", "a_log", "d_skip", "conv_out_norm", "ssm_out_norm",
         "norm_mix_post", "norm_mlp_pre", "norm_mlp_post", "conv_a_w", "ssm_conv_w")
BIG = ("w_in", "w_out", "w_up", "w_down")
ORDER = ("norm_mix_pre", "w_in", "conv_a_w", "ssm_conv_w", "ssm_conv_b", "dt_bias", "a_log", "d_skip",
         "conv_out_norm", "ssm_out_norm", "w_out", "norm_mix_post", "norm_mlp_pre", "w_up", "w_down", "norm_mlp_post")


def kernel(x, norm_mix_pre, w_in, conv_a_w, ssm_conv_w, ssm_conv_b, dt_bias, a_log, d_skip, conv_out_norm, ssm_out_norm, w_out, norm_mix_post, norm_mlp_pre, w_up, w_down, norm_mlp_post, loss_target, m_norm_mix_pre, m_w_in, m_conv_a_w, m_ssm_conv_w, m_ssm_conv_b, m_dt_bias, m_a_log, m_d_skip, m_conv_out_norm, m_ssm_out_norm, m_w_out, m_norm_mix_post, m_norm_mlp_pre, m_w_up, m_w_down, m_norm_mlp_post, v_norm_mix_pre, v_w_in, v_conv_a_w, v_ssm_conv_w, v_ssm_conv_b, v_dt_bias, v_a_log, v_d_skip, v_conv_out_norm, v_ssm_out_norm, v_w_out, v_norm_mix_post, v_norm_mlp_pre, v_w_up, v_w_down, v_norm_mlp_post):
    W = dict(norm_mix_pre=norm_mix_pre, w_in=w_in, conv_a_w=conv_a_w, ssm_conv_w=ssm_conv_w, ssm_conv_b=ssm_conv_b,
             dt_bias=dt_bias, a_log=a_log, d_skip=d_skip, conv_out_norm=conv_out_norm, ssm_out_norm=ssm_out_norm,
             w_out=w_out, norm_mix_post=norm_mix_post, norm_mlp_pre=norm_mlp_pre, w_up=w_up, w_down=w_down,
             norm_mlp_post=norm_mlp_post)
    M = dict(norm_mix_pre=m_norm_mix_pre, w_in=m_w_in, conv_a_w=m_conv_a_w, ssm_conv_w=m_ssm_conv_w,
             ssm_conv_b=m_ssm_conv_b, dt_bias=m_dt_bias, a_log=m_a_log, d_skip=m_d_skip,
             conv_out_norm=m_conv_out_norm, ssm_out_norm=m_ssm_out_norm, w_out=m_w_out,
             norm_mix_post=m_norm_mix_post, norm_mlp_pre=m_norm_mlp_pre, w_up=m_w_up, w_down=m_w_down,
             norm_mlp_post=m_norm_mlp_post)
    V = dict(norm_mix_pre=v_norm_mix_pre, w_in=v_w_in, conv_a_w=v_conv_a_w, ssm_conv_w=v_ssm_conv_w,
             ssm_conv_b=v_ssm_conv_b, dt_bias=v_dt_bias, a_log=v_a_log, d_skip=v_d_skip,
             conv_out_norm=v_conv_out_norm, ssm_out_norm=v_ssm_out_norm, w_out=v_w_out,
             norm_mix_post=v_norm_mix_post, norm_mlp_pre=v_norm_mlp_pre, w_up=v_w_up, w_down=v_w_down,
             norm_mlp_post=v_norm_mlp_post)

    nseq, seq, d = x.shape
    t = nseq * seq
    depth = w_in.shape[0]
    h = d // HEAD_DIM
    xbc = d + 2 * SSM_GROUPS * D_STATE
    in_cols = w_in.shape[2] * N_DEV
    d_mix = w_out.shape[1] * N_DEV
    d_ff = w_up.shape[2] * N_DEV
    me = 4 * lax.axis_index("x") + 2 * lax.axis_index("y") + lax.axis_index("c")
    ca_shard = conv_a_w.shape[2]
    sc_shard = ssm_conv_w.shape[2]

    tap_shapes = [conv_a_w.shape[1:], ssm_conv_w.shape[1:]]

    def gather_start(i, after=()):
        ks = SAME_CORE_PEERS
        st_in, tok_in = _exchange_start([w_in[i].astype(BF16), _pack([conv_a_w[i], ssm_conv_w[i]])], gather=True,
                                        name=f"gather_start_in_{i}", after=after, ks=ks)
        st_rest, tok_rest = _exchange_start([W[n][i].astype(BF16) for n in ("w_out", "w_up", "w_down")], gather=True,
                                            name=f"gather_start_rest_{i}", after=[tok_in], ks=ks, land_cols=rest_cols)
        return st_in, st_rest, tok_rest

    rest_cols = [None, d_ff // N_DEV, None]

    vec = lambda name, i: W[name][i].reshape(1, -1)
    emat = (lax.broadcasted_iota(jnp.int32, (h, d), 1) // HEAD_DIM == lax.broadcasted_iota(jnp.int32, (h, d), 0)).astype(F32)

    xcur = x.reshape(t, d)
    hcur = _norm_fwd(xcur, vec("norm_mix_pre", 0), name="norm_first")
    saved = []
    nxt = gather_start(0)
    sib_in = None
    for i in range(depth):
        st_in, st_rest, tok = nxt
        if sib_in is None:
            sib_in, _ = _sibling_start(_exchange_wait(st_in, [hcur, tok], name=f"gather_wait_in_{i}"),
                                       name=f"gather_sibling_start_in_{i}")
        win_g, taps_g = _sibling_wait(sib_in, [hcur], name=f"gather_sibling_wait_in_{i}")
        win = _assemble_cols(win_g, name=f"assemble_w_in_{i}")
        taps_j = [_unpack(taps_g[j], tap_shapes) for j in range(N_DEV)]
        conv_a_i = jnp.concatenate([tj[0] for tj in taps_j], axis=1)
        ssm_conv_i = jnp.concatenate([tj[1] for tj in taps_j], axis=1)
        proj = _mm(hcur, win, n=4 * d + xbc, name=f"fwd_proj_{i}", out_dtypes=(BF16,))
        pdt = _mm(hcur, win, n=LANES, b_off=4 * d + xbc, name=f"fwd_proj_dt_{i}")
        ya, va = _conva_fwd(proj, conv_a_i, vec("conv_out_norm", i), d=d, seq=seq, name=f"fwd_conv_a_{i}")
        cpre = _convb_fwd(proj, ssm_conv_i, vec("ssm_conv_b", i), col0=4 * d, seq=seq, name=f"fwd_conv_b_{i}")
        dsk_lane = jnp.repeat(W["d_skip"][i], HEAD_DIM).reshape(1, d)
        st_sib, tok_sib = _sibling_start(_exchange_wait(st_rest, [cpre], name=f"gather_wait_rest_{i}"),
                                         name=f"gather_sibling_start_rest_{i}", land_cols=rest_cols)
        cat, y2, hprev = _ssd_fwd(cpre, pdt, proj, ya, vec("dt_bias", i) + tok_sib[0:1, 0:1], vec("a_log", i), dsk_lane,
                                  vec("ssm_out_norm", i), emat, nseq=nseq, seq=seq, name=f"fwd_ssd_{i}")
        wout_g, wup_g, wdown_g = _sibling_wait(st_sib, [cat], name=f"gather_sibling_wait_rest_{i}")
        lw = dict(win=win, wout=wout_g.reshape(d_mix, d),
                  wup=wup_g, wdown=wdown_g.reshape(d_ff, d),
                  conv_a=conv_a_i, ssm_conv=ssm_conv_i)
        after = []
        if i + 1 < depth:
            nxt = gather_start(i + 1, after=[wout_g])
            after = [nxt[2]]
        x1, h2, mix = _mm(cat, lw["wout"], name=f"fwd_out_{i}", after=after, out_dtypes=(F32, BF16, BF16),
                          epi=_epi_resid_norm, extras=(xcur,), vecs=(vec("norm_mix_post", i), vec("norm_mlp_pre", i)),
                          tm_cap=FUSED_ROWS)
        f = _mm(h2, lw["wup"], name=f"fwd_up_{i}", out_dtypes=(BF16,), epi=_epi_relu2)
        g_next = vec("norm_mix_pre", i + 1) if i + 1 < depth else None
        after = []
        if i + 1 < depth:
            sib_in, tok_in = _sibling_start(_exchange_wait(nxt[0], [f], name=f"gather_wait_in_{i + 1}"),
                                            name=f"gather_sibling_start_in_{i + 1}")
            after = [tok_in]
        dn = _mm(f, lw["wdown"], name=f"fwd_down_{i}", out_dtypes=(BF16,), after=after)
        if i + 1 < depth:
            x2, hnext = _resid_norm(x1, dn, vec("norm_mlp_post", i), g_next, name=f"fwd_post_mlp_{i}")
        else:
            x2 = hnext = None
        saved.append(dict(lw=lw, x0=xcur, h=hcur, proj=proj, pdt=pdt, va=va, cpre=cpre, y2=y2,
                          hprev=hprev, cat=cat, mix=mix, x1=x1, h2=h2, f=f, dn=dn, dsk_lane=dsk_lane))
        xcur, hcur = x2, hnext

    last = saved[depth - 1]
    dx, ddn, dg_last, loss_part = _tail_loss(last["x1"], last["dn"], vec("norm_mlp_post", depth - 1),
                                             loss_target.reshape(t, d), name="loss")

    small_grads = {n: [None] * depth for n in SMALL}
    big_out = {n: None for n in BIG}

    def finish(pending, after):
        li, st_a, st_b = pending

        def update(n, parts):
            big_out[n] = _sum_adamw(parts, W[n], M[n], V[n], layer=li, outs=big_out[n], name=f"adamw_{n}_{li}")

        p_down, p_up, p_out = _exchange_wait(st_a, after, name=f"scatter_wait_a_{li}")
        update("w_down", p_down)
        update("w_up", p_up)
        update("w_out", p_out)
        p_in, = _exchange_wait(st_b, after + [big_out["w_out"][0]], name=f"scatter_wait_b_{li}")
        update("w_in", p_in)

    pending = None
    for i in reversed(range(depth)):
        s = saved[i]
        lw = s["lw"]
        if i == depth - 1:
            small_grads["norm_mlp_post"][i] = dg_last
        dup = _mm(ddn, lw["wdown"], tb=True, name=f"bwd_down_dx_{i}", out_dtypes=(BF16,), epi=_epi_drelu2,
                  extras=(s["f"],))
        g_wdown = _mm(s["f"], ddn, ta=True, name=f"bwd_down_dw_{i}", out_dtypes=(BF16,))
        dh2 = _mm(dup, lw["wup"], tb=True, name=f"bwd_up_dx_{i}", out_dtypes=(BF16,))
        g_wup = _mm(s["h2"], dup, ta=True, name=f"bwd_up_dw_{i}", out_dtypes=(BF16,))
        dx1, dmix, dg_pre, dg_post = _bwd_norm_pair(s["x1"], [dh2], dx, s["mix"], vec("norm_mlp_pre", i),
                                                    vec("norm_mix_post", i), name=f"bwd_norm_mix_post_{i}")
        small_grads["norm_mlp_pre"][i] = dg_pre
        small_grads["norm_mix_post"][i] = dg_post
        dcat = _mm(dmix, lw["wout"], tb=True, name=f"bwd_out_dx_{i}", out_dtypes=(BF16,))
        g_wout = _mm(s["cat"], dmix, ta=True, name=f"bwd_out_dw_{i}", out_dtypes=(BF16,))
        st_a, tok_a = _exchange_start(
            [g_wdown.reshape(N_DEV, d_ff // N_DEV, d), g_wup, g_wout.reshape(N_DEV, d_mix // N_DEV, d)],
            gather=False, name=f"scatter_start_a_{i}", src_cols=[None, d_ff // N_DEV, None])
        dproj, dcaw, dgca = _conva_bwd(dcat, s["proj"], s["va"], lw["conv_a"],
                                       vec("conv_out_norm", i) + tok_a[0:1, 0:1], d=d, seq=seq, name=f"bwd_conv_a_{i}")
        small_grads["conv_a_w"][i] = dcaw
        small_grads["conv_out_norm"][i] = dgca
        dconv, dproj, dpdt, dgs, ddsk, ddtb, dalog = _ssd_bwd(
            s["cpre"], s["pdt"], s["proj"], s["y2"], s["hprev"], dcat, vec("dt_bias", i), vec("a_log", i),
            s["dsk_lane"], vec("ssm_out_norm", i), emat, dproj, nseq=nseq, seq=seq, name=f"bwd_ssd_{i}")
        small_grads["ssm_out_norm"][i] = dgs
        small_grads["d_skip"][i] = ddsk
        small_grads["dt_bias"][i] = ddtb
        small_grads["a_log"][i] = dalog
        dproj, dscw, dscb = _convb_bwd(dconv, s["proj"], lw["ssm_conv"], dproj, col0=4 * d, seq=seq,
                                       name=f"bwd_conv_b_{i}")
        small_grads["ssm_conv_w"][i] = dscw
        small_grads["ssm_conv_b"][i] = dscb
        g_win = _split_cols([
            _mm(s["h"], dproj, ta=True, name=f"bwd_proj_dw_{i}", out_dtypes=(BF16,)),
            _mm(s["h"], dpdt, ta=True, name=f"bwd_proj_dt_dw_{i}", out_dtypes=(BF16,))],
            in_cols // N_DEV, name=f"split_g_w_in_{i}")
        st_b, tok_b = _exchange_start([g_win], gather=False, name=f"scatter_start_b_{i}")
        dh_parts = [_mm(dp, lw["win"], tb=True, b_koff=off, name=f"bwd_proj_{nm}dx_{i}", after=[tok_b], out_dtypes=(BF16,))
                    for nm, dp, off in (("", dproj, 0), ("dt_", dpdt, 4 * d + xbc))]
        if i > 0:
            dx, ddn, dg_in, dg_below = _bwd_norm_pair(s["x0"], dh_parts, dx1, saved[i - 1]["dn"], vec("norm_mix_pre", i),
                                                      vec("norm_mlp_post", i - 1), name=f"bwd_norm_mix_pre_{i}")
            small_grads["norm_mlp_post"][i - 1] = dg_below
        else:
            dx, dg_in = _bwd_norm_in(s["x0"], dh_parts, dx1, vec("norm_mix_pre", i), name=f"bwd_norm_mix_pre_{i}")
        small_grads["norm_mix_pre"][i] = dg_in
        if pending is not None:
            finish(pending, [dx])
        pending = (i, st_a, st_b)

    grad_x = dx.reshape(nseq, seq, d)

    small_shapes_full = {n: (depth,) + tuple(small_grads[n][0].shape) for n in SMALL}
    gpack = _pack([jnp.stack(small_grads[n]) for n in SMALL] + [loss_part])
    st_small, tok_small = _exchange_start([gpack], gather=True, name="allreduce_small_start")
    finish(pending, [dx, tok_small])
    gparts, = _exchange_wait(st_small, [big_out["w_in"][0]], name="allreduce_small_wait")

    def shard_of(n, full):
        if n == "conv_a_w":
            return lax.dynamic_slice_in_dim(full, me * ca_shard, ca_shard, axis=2)
        if n == "ssm_conv_w":
            return lax.dynamic_slice_in_dim(full, me * sc_shard, sc_shard, axis=2)
        return full.reshape(W[n].shape)

    gsum = _sum_parts(gparts, name="sum_small")
    gfull = _unpack(gsum, [small_shapes_full[n] for n in SMALL] + [(1, 1)])
    loss = gfull[-1][0, 0]
    gsmall = {n: shard_of(n, gf) for n, gf in zip(SMALL, gfull)}
    res = _sum_adamw(_pack([gsmall[n] for n in SMALL])[None], _pack([W[n] for n in SMALL]),
                     _pack([M[n] for n in SMALL]), _pack([V[n] for n in SMALL]), name="adamw_small")
    small_out = [dict(zip(SMALL, _unpack(r, [W[n].shape for n in SMALL]))) for r in res]

    def out_of(kind, n):
        return big_out[n][kind] if n in BIG else small_out[kind][n]

    return (loss, grad_x, *[out_of(k, n) for k in range(4) for n in ORDER])
```

```python
import jax
import jax.numpy as jnp
from jax import lax
from jax.experimental import pallas as pl
from jax.experimental.pallas import tpu as pltpu

F32 = jnp.float32
BF16 = jnp.bfloat16
HIGHEST = lax.Precision.HIGHEST
MESH = pl.DeviceIdType.MESH

EPS = 1e-6
HEAD_DIM = 64
D_STATE = 128
SSM_GROUPS = 2
CHUNK = 128
CONV_K = 3
SSM_CONV_K = 4
ADAM_LR = 0.001
ADAM_B1 = 0.9
ADAM_B2 = 0.999
ADAM_EPS = 1e-08
ADAM_WD = 0.01
ADAM_STEP = 10

N_DEV = 8
LANES = 128
SUBLANES = 8
VMEM_LIMIT = 48 * 1024 * 1024
ROW_TILE = 512
MM_TILE = 1024
MM_TILE_N = 2816
MM_VMEM_BUDGET = 40 * 1024 * 1024
FUSED_ROWS = 512


def _params(sem):
    return pltpu.CompilerParams(dimension_semantics=sem, vmem_limit_bytes=VMEM_LIMIT)


def _call(body, **kw):
    return pl.pallas_call(body, **kw)


def _pick(n, cap):
    best = None
    for t in range(LANES, min(n, cap) + 1, LANES):
        if n % t == 0:
            best = t
    return best or n


def _pick_rows(n, cap):
    best = None
    for t in range(SUBLANES, min(n, cap) + 1, SUBLANES):
        if n % t == 0:
            best = t
    return best or n


def _sigmoid(x):
    return 1.0 / (1.0 + jnp.exp(-x))


def _softplus(x):
    return jnp.maximum(x, 0.0) + jnp.log1p(jnp.exp(-jnp.abs(x)))


def _rms(x):
    return lax.rsqrt(jnp.mean(x * x, axis=-1, keepdims=True) + EPS)


def _rms_bwd(x, r, g, dy):
    gy = dy * g
    dx = r * gy - x * (r * r * r) * jnp.mean(gy * x, axis=-1, keepdims=True)
    return dx, dy * x * r


def _full(shape):
    return pl.BlockSpec(shape, lambda *_: (0,) * len(shape))


def _mm(a, b, *, name, ta=False, tb=False, out_dtypes=(F32,), epi=None, extras=(), n=None, b_off=0, b_koff=0,
        after=(), vecs=(), tm_cap=MM_TILE):
    m, k = (a.shape[1], a.shape[0]) if ta else a.shape
    if n is None:
        n = b.shape[0] if tb else b.shape[1]
    tm, tn, tk = _pick(m, tm_cap), _pick(n, MM_TILE_N), _pick(k, MM_TILE)
    while b_off % tn or n % tn:
        tn -= LANES
    if b_koff == 0 and k > MM_TILE:
        tk = _pick(k, MM_TILE_N)
    while b_koff % tk or k % tk:
        tk -= LANES

    def vmem_bytes(tk_):
        per_out = sum(jnp.dtype(dt).itemsize for dt in out_dtypes) + sum(e.dtype.itemsize for e in extras)
        return 2 * tk_ * (tm * a.dtype.itemsize + tn * b.dtype.itemsize) + tm * tn * (2 * per_out + 4)

    while vmem_bytes(tk) > MM_VMEM_BUDGET and tk % (2 * LANES) == 0 and not b_koff % (tk // 2):
        tk //= 2
    nk = k // tk
    nm, nn = m // tm, n // tn
    jo = b_off // tn
    ko = b_koff // tk
    a_bytes = m * k * a.dtype.itemsize
    b_bytes = n * k * b.dtype.itemsize
    m_outer = a_bytes + nm * b_bytes <= b_bytes + nn * a_bytes
    ij = (lambda g0, g1: (g0, g1)) if m_outer else (lambda g0, g1: (g1, g0))
    grid = (nm, nn, nk) if m_outer else (nn, nm, nk)

    def a_map(g0, g1, kk):
        i, _ = ij(g0, g1)
        return (kk, i) if ta else (i, kk)

    def b_map(g0, g1, kk):
        _, j = ij(g0, g1)
        return (j + jo, kk + ko) if tb else (kk + ko, j + jo)

    def o_map(g0, g1, kk):
        return ij(g0, g1)

    a_spec = pl.BlockSpec((tk, tm) if ta else (tm, tk), a_map)
    b_spec = pl.BlockSpec((tn, tk) if tb else (tk, tn), b_map)
    o_spec = pl.BlockSpec((tm, tn), o_map)
    dims = (((0 if ta else 1,), (1 if tb else 0,)), ((), ()))
    n_ex = len(extras) + len(vecs)
    after = list(after)
    o0 = 2 + n_ex + len(after)

    def finish(acc, ex, outs):
        res = (acc,) if epi is None else epi(acc, *[e[...] for e in ex])
        for o, r in zip(outs, res):
            o[...] = r.astype(o.dtype)

    def body_single(*refs):
        a_ref, b_ref = refs[:2]
        acc = lax.dot_general(a_ref[...].astype(BF16), b_ref[...].astype(BF16), dims, preferred_element_type=F32)
        finish(acc, refs[2:2 + n_ex], refs[o0:])

    def body_multi(*refs):
        a_ref, b_ref = refs[:2]
        acc = refs[-1]
        kk = pl.program_id(2)

        @pl.when(kk == 0)
        def _():
            acc[...] = jnp.zeros_like(acc)

        acc[...] += lax.dot_general(a_ref[...].astype(BF16), b_ref[...].astype(BF16), dims, preferred_element_type=F32)

        @pl.when(kk == nk - 1)
        def _():
            finish(acc[...], refs[2:2 + n_ex], refs[o0:-1])

    v_spec = pl.BlockSpec((1, tn), lambda g0, g1, kk: (0, ij(g0, g1)[1]))
    outs = _call(
        body_single if nk == 1 else body_multi, name=name, grid=grid,
        in_specs=([a_spec, b_spec] + [o_spec] * len(extras) + [v_spec] * len(vecs)
                  + [pl.BlockSpec(memory_space=pl.ANY)] * len(after)),
        out_specs=[o_spec] * len(out_dtypes),
        out_shape=[jax.ShapeDtypeStruct((m, n), dt) for dt in out_dtypes],
        scratch_shapes=[] if nk == 1 else [pltpu.VMEM((tm, tn), F32)],
        compiler_params=_params(("parallel", "parallel", "arbitrary")),
    )(a, b, *extras, *vecs, *after)
    return outs[0] if len(outs) == 1 else outs


def _epi_resid_norm(acc, x, g_res, g_next):
    xn = x + acc * _rms(acc) * g_res
    return xn, xn * _rms(xn) * g_next, acc


def _epi_relu2(acc):
    r = jnp.maximum(acc, 0.0)
    return (r * r,)


def _epi_drelu2(acc, f):
    return (acc * (2.0 * jnp.sqrt(f).astype(F32)),)


def _norm_fwd(x, g, *, name):
    t, d = x.shape
    tt = _pick_rows(t, ROW_TILE)

    def body(x_ref, g_ref, h_ref):
        xv = x_ref[...]
        h_ref[...] = (xv * _rms(xv) * g_ref[...]).astype(BF16)

    row = pl.BlockSpec((tt, d), lambda i: (i, 0))
    return _call(body, name=name, grid=(t // tt,), in_specs=[row, _full((1, d))], out_specs=row,
                 out_shape=jax.ShapeDtypeStruct((t, d), BF16), compiler_params=_params(("parallel",)))(x, g)


def _resid_norm(x, n, g1, g2, *, name):
    t, d = x.shape
    tt = _pick_rows(t, ROW_TILE)
    gains = [g1] if g2 is None else [g1, g2]

    def body(x_ref, n_ref, *refs):
        nv = n_ref[...].astype(F32)
        xn = x_ref[...] + nv * _rms(nv) * refs[0][...]
        refs[len(gains)][...] = xn
        if g2 is not None:
            refs[3][...] = (xn * _rms(xn) * refs[1][...]).astype(BF16)

    row = pl.BlockSpec((tt, d), lambda i: (i, 0))
    outs = _call(body, name=name, grid=(t // tt,), in_specs=[row, row] + [_full((1, d))] * len(gains),
                 out_specs=[row] * len(gains),
                 out_shape=[jax.ShapeDtypeStruct((t, d), F32), jax.ShapeDtypeStruct((t, d), BF16)][:len(gains)],
                 compiler_params=_params(("parallel",)))(x, n, *gains)
    return (outs[0], None) if g2 is None else outs


def _loss_fwd_bwd(xf, target, *, name):
    t, d = xf.shape
    tt = _pick_rows(t, ROW_TILE)
    nt = t // tt

    def body(x_ref, t_ref, dy_ref, loss_ref, acc):
        i = pl.program_id(0)

        @pl.when(i == 0)
        def _():
            acc[...] = jnp.zeros_like(acc)

        e = x_ref[...] - t_ref[...]
        dy_ref[...] = e * (1.0 / d)
        acc[...] += jnp.sum(e * e, axis=0, keepdims=True)

        @pl.when(i == nt - 1)
        def _():
            loss_ref[...] = jnp.sum(acc[...], axis=-1, keepdims=True) * (0.5 / d)

    row = pl.BlockSpec((tt, d), lambda i: (i, 0))
    return _call(body, name=name, grid=(nt,), in_specs=[row, row], out_specs=[row, _full((1, 1))],
                 out_shape=[jax.ShapeDtypeStruct((t, d), F32), jax.ShapeDtypeStruct((1, 1), F32)],
                 scratch_shapes=[pltpu.VMEM((1, d), F32)], compiler_params=_params(("arbitrary",)))(xf, target)


def _tail_loss(x, n, g, target, *, name):
    t, d = x.shape
    tt = _pick_rows(t, ROW_TILE)
    nt = t // tt

    def body(x_ref, n_ref, g_ref, t_ref, dy_ref, dn_ref, dg_ref, loss_ref, acc):
        i = pl.program_id(0)

        @pl.when(i == 0)
        def _():
            acc[...] = jnp.zeros_like(acc)
            dg_ref[...] = jnp.zeros_like(dg_ref)

        nv = n_ref[...].astype(F32)
        r = _rms(nv)
        e = (x_ref[...] + nv * r * g_ref[...]) - t_ref[...]
        dy = e * (1.0 / d)
        dy_ref[...] = dy
        acc[...] += jnp.sum(e * e, axis=0, keepdims=True)
        dn, dg = _rms_bwd(nv, r, g_ref[...], dy)
        dn_ref[...] = dn.astype(BF16)
        dg_ref[...] += jnp.sum(dg, axis=0, keepdims=True)

        @pl.when(i == nt - 1)
        def _():
            loss_ref[...] = jnp.sum(acc[...], axis=-1, keepdims=True) * (0.5 / d)

    row = pl.BlockSpec((tt, d), lambda i: (i, 0))
    vec = _full((1, d))
    return _call(body, name=name, grid=(nt,), in_specs=[row, row, vec, row],
                 out_specs=[row, row, vec, _full((1, 1))],
                 out_shape=[jax.ShapeDtypeStruct((t, d), F32), jax.ShapeDtypeStruct((t, d), BF16),
                            jax.ShapeDtypeStruct((1, d), F32), jax.ShapeDtypeStruct((1, 1), F32)],
                 scratch_shapes=[pltpu.VMEM((1, d), F32)], compiler_params=_params(("arbitrary",)))(x, n, g, target)


def _bwd_norm_pair(xin, dh, dres, n, g_in, g_out, *, name):
    t, d = xin.shape
    tt = _pick_rows(t, ROW_TILE)
    n_dh = len(dh)

    def body(*refs):
        x_ref = refs[0]
        dh_refs = refs[1:1 + n_dh]
        dres_ref, n_ref, gi_ref, go_ref, dx_ref, dn_ref, dgi_ref, dgo_ref = refs[1 + n_dh:]
        i = pl.program_id(0)

        @pl.when(i == 0)
        def _():
            dgi_ref[...] = jnp.zeros_like(dgi_ref)
            dgo_ref[...] = jnp.zeros_like(dgo_ref)

        xv = x_ref[...]
        dhv = dh_refs[0][...].astype(F32)
        for r in dh_refs[1:]:
            dhv = dhv + r[...].astype(F32)
        dxh, dgi = _rms_bwd(xv, _rms(xv), gi_ref[...], dhv)
        dx = dres_ref[...] + dxh
        dx_ref[...] = dx
        dgi_ref[...] += jnp.sum(dgi, axis=0, keepdims=True)
        nv = n_ref[...].astype(F32)
        dn, dgo = _rms_bwd(nv, _rms(nv), go_ref[...], dx)
        dn_ref[...] = dn.astype(BF16)
        dgo_ref[...] += jnp.sum(dgo, axis=0, keepdims=True)

    row = pl.BlockSpec((tt, d), lambda i: (i, 0))
    vec = _full((1, d))
    return _call(body, name=name, grid=(t // tt,), in_specs=[row] * (n_dh + 3) + [vec, vec],
                 out_specs=[row, row, vec, vec],
                 out_shape=[jax.ShapeDtypeStruct((t, d), F32), jax.ShapeDtypeStruct((t, d), BF16),
                            jax.ShapeDtypeStruct((1, d), F32), jax.ShapeDtypeStruct((1, d), F32)],
                 compiler_params=_params(("arbitrary",)))(xin, *dh, dres, n, g_in, g_out)


def _bwd_norm_in(xin, dh, dres, g_in, *, name):
    t, d = xin.shape
    tt = _pick_rows(t, ROW_TILE)
    n_dh = len(dh)

    def body(*refs):
        x_ref = refs[0]
        dh_refs = refs[1:1 + n_dh]
        dres_ref, gi_ref, dx_ref, dgi_ref = refs[1 + n_dh:]
        i = pl.program_id(0)

        @pl.when(i == 0)
        def _():
            dgi_ref[...] = jnp.zeros_like(dgi_ref)

        xv = x_ref[...]
        dhv = dh_refs[0][...].astype(F32)
        for r in dh_refs[1:]:
            dhv = dhv + r[...].astype(F32)
        dxh, dgi = _rms_bwd(xv, _rms(xv), gi_ref[...], dhv)
        dx_ref[...] = dres_ref[...] + dxh
        dgi_ref[...] += jnp.sum(dgi, axis=0, keepdims=True)

    row = pl.BlockSpec((tt, d), lambda i: (i, 0))
    vec = _full((1, d))
    return _call(body, name=name, grid=(t // tt,), in_specs=[row] * (n_dh + 2) + [vec],
                 out_specs=[row, vec],
                 out_shape=[jax.ShapeDtypeStruct((t, d), F32), jax.ShapeDtypeStruct((1, d), F32)],
                 compiler_params=_params(("arbitrary",)))(xin, *dh, dres, g_in)


def _bwd_norm_out(n, g_out, dx, *, name):
    t, d = n.shape
    tt = _pick_rows(t, ROW_TILE)

    def body(n_ref, go_ref, dx_ref, dn_ref, dgo_ref):
        i = pl.program_id(0)

        @pl.when(i == 0)
        def _():
            dgo_ref[...] = jnp.zeros_like(dgo_ref)

        nv = n_ref[...].astype(F32)
        dn, dgo = _rms_bwd(nv, _rms(nv), go_ref[...], dx_ref[...])
        dn_ref[...] = dn.astype(BF16)
        dgo_ref[...] += jnp.sum(dgo, axis=0, keepdims=True)

    row = pl.BlockSpec((tt, d), lambda i: (i, 0))
    vec = _full((1, d))
    return _call(body, name=name, grid=(t // tt,), in_specs=[row, vec, row], out_specs=[row, vec],
                 out_shape=[jax.ShapeDtypeStruct((t, d), BF16), jax.ShapeDtypeStruct((1, d), F32)],
                 compiler_params=_params(("arbitrary",)))(n, g_out, dx)


def _shift_down(cur, halo, s):
    return jnp.concatenate([halo[SUBLANES - s:], cur[:cur.shape[0] - s]], axis=0)


def _shift_up(cur, halo, s):
    return jnp.concatenate([cur[s:], halo[:s]], axis=0)


def _conva_fwd(pa, w, g, *, d, seq, name):
    t = pa.shape[0]
    tt = _pick_rows(seq, ROW_TILE)
    tps = seq // tt

    def body(xa_ref, ca_ref, ba_ref, w_ref, g_ref, ya_ref, v_ref, carry):
        i = pl.program_id(0)

        @pl.when(i % tps == 0)
        def _():
            carry[...] = jnp.zeros_like(carry)

        u = ca_ref[...].astype(F32) * xa_ref[...].astype(F32)
        halo = carry[...]
        wv = w_ref[...]
        v = wv[2:3] * u + wv[1:2] * _shift_down(u, halo, 1) + wv[0:1] * _shift_down(u, halo, 2)
        carry[...] = u[tt - SUBLANES:]
        yp = ba_ref[...].astype(F32) * v
        ya_ref[...] = (yp * _rms(yp) * g_ref[...]).astype(BF16)
        v_ref[...] = v.astype(BF16)

    col = lambda c: pl.BlockSpec((tt, d), lambda i, c=c: (i, c))
    row = pl.BlockSpec((tt, d), lambda i: (i, 0))
    return _call(body, name=name, grid=(t // tt,),
                 in_specs=[col(0), col(1), col(2), _full((CONV_K, d)), _full((1, d))], out_specs=[row, row],
                 out_shape=[jax.ShapeDtypeStruct((t, d), BF16), jax.ShapeDtypeStruct((t, d), BF16)],
                 scratch_shapes=[pltpu.VMEM((SUBLANES, d), F32)],
                 compiler_params=_params(("arbitrary",)))(pa, pa, pa, w, g)


def _conva_bwd(dcat, pa, v, w, g, *, d, seq, name):
    t, width = pa.shape
    d3 = 3 * d
    tt = _pick_rows(seq, ROW_TILE)
    tps = seq // tt
    nt = t // tt

    def body(dya_ref, xa_ref, ca_ref, ba_ref, v_ref, w_ref, g_ref, dpa_ref, dw_ref, dg_ref, carry):
        i = pl.program_id(0)

        @pl.when(i == 0)
        def _():
            dw_ref[...] = jnp.zeros_like(dw_ref)
            dg_ref[...] = jnp.zeros_like(dg_ref)

        @pl.when(i % tps == 0)
        def _():
            carry[...] = jnp.zeros_like(carry)

        xa, ca, ba, vv = [r[...].astype(F32) for r in (xa_ref, ca_ref, ba_ref, v_ref)]
        yp = ba * vv
        dyp, dgt = _rms_bwd(yp, _rms(yp), g_ref[...], dya_ref[...].astype(F32))
        dg_ref[...] += jnp.sum(dgt, axis=0, keepdims=True)
        dv = dyp * ba
        halo = carry[...]
        dv1 = _shift_up(dv, halo, 1)
        dv2 = _shift_up(dv, halo, 2)
        carry[...] = dv[:SUBLANES]
        wv = w_ref[...]
        du = wv[2:3] * dv + wv[1:2] * dv1 + wv[0:1] * dv2
        u = ca * xa
        dw_ref[0:1, :] += jnp.sum(u * dv2, axis=0, keepdims=True)
        dw_ref[1:2, :] += jnp.sum(u * dv1, axis=0, keepdims=True)
        dw_ref[2:3, :] += jnp.sum(u * dv, axis=0, keepdims=True)
        dpa_ref[:, 0:d] = (du * ca).astype(BF16)
        dpa_ref[:, d:2 * d] = (du * xa).astype(BF16)
        dpa_ref[:, 2 * d:3 * d] = (dyp * vv).astype(BF16)

    rcol = lambda c: pl.BlockSpec((tt, d), lambda i, c=c: (nt - 1 - i, c))
    return _call(body, name=name, grid=(nt,),
                 in_specs=[rcol(0), rcol(0), rcol(1), rcol(2), rcol(0), _full((CONV_K, d)), _full((1, d))],
                 out_specs=[pl.BlockSpec((tt, d3), lambda i: (nt - 1 - i, 0)), _full((CONV_K, d)), _full((1, d))],
                 out_shape=[jax.ShapeDtypeStruct((t, width), BF16), jax.ShapeDtypeStruct((CONV_K, d), F32),
                            jax.ShapeDtypeStruct((1, d), F32)],
                 scratch_shapes=[pltpu.VMEM((SUBLANES, d), F32)],
                 compiler_params=_params(("arbitrary",)))(dcat, pa, pa, pa, v, w, g)


CONV_CH = 512


def _convb_fwd(proj, w, bias, *, col0, seq, name):
    t = proj.shape[0]
    c = w.shape[1]
    cb = _pick(c, CONV_CH)
    assert col0 % cb == 0
    tt = _pick_rows(seq, 2 * ROW_TILE)
    tps = seq // tt

    def body(p_ref, w_ref, b_ref, o_ref, carry):
        i = pl.program_id(1)

        @pl.when(i % tps == 0)
        def _():
            carry[...] = jnp.zeros_like(carry)

        p = p_ref[...].astype(F32)
        halo = carry[...]
        wv = w_ref[...]
        o = wv[3:4] * p + b_ref[...]
        for s in (1, 2, 3):
            o = o + wv[3 - s:4 - s] * _shift_down(p, halo, s)
        carry[...] = p[tt - SUBLANES:]
        o_ref[...] = o.astype(BF16)

    return _call(body, name=name, grid=(c // cb, t // tt),
                 in_specs=[pl.BlockSpec((tt, cb), lambda jc, i: (i, col0 // cb + jc)),
                           pl.BlockSpec((SSM_CONV_K, cb), lambda jc, i: (0, jc)), pl.BlockSpec((1, cb), lambda jc, i: (0, jc))],
                 out_specs=pl.BlockSpec((tt, cb), lambda jc, i: (i, jc)), out_shape=jax.ShapeDtypeStruct((t, c), BF16),
                 scratch_shapes=[pltpu.VMEM((SUBLANES, cb), F32)],
                 compiler_params=_params(("arbitrary", "arbitrary")))(proj, w, bias)


def _convb_bwd(dconv, proj, w, dproj, *, col0, seq, name):
    t, c = dconv.shape
    cb = _pick(c, CONV_CH)
    assert col0 % cb == 0
    tt = _pick_rows(seq, 2 * ROW_TILE)
    tps = seq // tt
    nt = t // tt

    def body(dc_ref, p_ref, w_ref, dproj_in, dp_ref, dw_ref, db_ref, carry):
        del dproj_in
        i = pl.program_id(1)

        @pl.when(i == 0)
        def _():
            dw_ref[...] = jnp.zeros_like(dw_ref)
            db_ref[...] = jnp.zeros_like(db_ref)

        @pl.when(i % tps == 0)
        def _():
            carry[...] = jnp.zeros_like(carry)

        dc = dc_ref[...].astype(F32)
        p = p_ref[...].astype(F32)
        halo = carry[...]
        wv = w_ref[...]
        dp = wv[3:4] * dc
        dw_ref[3:4, :] += jnp.sum(p * dc, axis=0, keepdims=True)
        for s in (1, 2, 3):
            dcs = _shift_up(dc, halo, s)
            dp = dp + wv[3 - s:4 - s] * dcs
            dw_ref[3 - s:4 - s, :] += jnp.sum(p * dcs, axis=0, keepdims=True)
        carry[...] = dc[:SUBLANES]
        db_ref[...] += jnp.sum(dc, axis=0, keepdims=True)
        dp_ref[...] = dp.astype(BF16)

    win_spec = pl.BlockSpec((tt, cb), lambda jc, i: (nt - 1 - i, col0 // cb + jc))
    taps = pl.BlockSpec((SSM_CONV_K, cb), lambda jc, i: (0, jc))
    return _call(body, name=name, grid=(c // cb, nt),
                 in_specs=[pl.BlockSpec((tt, cb), lambda jc, i: (nt - 1 - i, jc)), win_spec, taps,
                           pl.BlockSpec(memory_space=pl.ANY)],
                 out_specs=[win_spec, taps, pl.BlockSpec((1, cb), lambda jc, i: (0, jc))],
                 out_shape=[jax.ShapeDtypeStruct(dproj.shape, BF16), jax.ShapeDtypeStruct((SSM_CONV_K, c), F32),
                            jax.ShapeDtypeStruct((1, c), F32)],
                 input_output_aliases={3: 0},
                 scratch_shapes=[pltpu.VMEM((SUBLANES, cb), F32)],
                 compiler_params=_params(("arbitrary", "arbitrary")))(dconv, proj, w, dproj)


def _expand_heads(x, ev):
    return jnp.dot(x, ev, precision=HIGHEST, preferred_element_type=F32)


def _head_sums(v, ev):
    return lax.dot_general(v, ev, (((1,), (1,)), ((), ())), precision=HIGHEST, preferred_element_type=F32)


def _ssd_common(c_ref, pdt_ref, dtb_ref, alog_ref, e_ref, h):
    cp = c_ref[...].astype(F32)
    sg = _sigmoid(cp)
    act = cp * sg
    pre = pdt_ref[:, 0:h] + dtb_ref[...]
    dt = _softplus(pre)
    a = -jnp.exp(alog_ref[...])
    adt = dt * a
    row = lax.broadcasted_iota(jnp.int32, (CHUNK, CHUNK), 0)
    col = lax.broadcasted_iota(jnp.int32, (CHUNK, CHUNK), 1)
    tril = row >= col
    cs = jnp.dot(tril.astype(F32), adt, precision=HIGHEST, preferred_element_type=F32)
    cs_t = lax.dot_general(adt, (col >= row).astype(F32), (((0,), (0,)), ((), ())), precision=HIGHEST,
                           preferred_element_type=F32)
    ev = e_ref[...]
    dt_l = _expand_heads(dt, ev)
    ecs_l = jnp.exp(_expand_heads(cs, ev))
    return dict(cp=cp, sg=sg, act=act, pre=pre, dt=dt, a=a, cs=cs, cs_t=cs_t, dt_l=dt_l, ecs_l=ecs_l,
                tril=tril, row=row, col=col, lo=col < HEAD_DIM)


def _dot_nt(a, b):
    return lax.dot_general(a, b, (((1,), (1,)), ((), ())), preferred_element_type=F32)


def _dot_tn(a, b):
    return lax.dot_general(a, b, (((0,), (0,)), ((), ())), preferred_element_type=F32)


def _dot(a, b):
    return jnp.dot(a, b, preferred_element_type=F32)


def _ssd_fwd(cpre, pdt, pz, ya, dtb, alog, dsk_lane, gs, emat, *, nseq, seq, name):
    t, xbc = cpre.shape
    d = ya.shape[1]
    h = d // HEAD_DIM
    npair = h // 2
    ppg = npair // SSM_GROUPS
    nc = seq // CHUNK
    gw = d // SSM_GROUPS
    bc0 = d
    cc0 = d + SSM_GROUPS * D_STATE

    def body(c_ref, pdt_ref, z_ref, ya_ref, dtb_ref, alog_ref, dsk_ref, gs_ref, e_ref, cat_ref, y2_ref, hp_ref, h_ref):
        @pl.when(pl.program_id(0) == 0)
        def _():
            h_ref[...] = jnp.zeros_like(h_ref)

        for sq in range(nseq):
            one_seq(c_ref.at[sq], pdt_ref.at[sq], z_ref.at[sq], ya_ref.at[sq], dtb_ref, alog_ref, dsk_ref, gs_ref, e_ref,
                    cat_ref.at[sq], y2_ref.at[sq], hp_ref.at[sq], h_ref.at[sq])

    def one_seq(c_ref, pdt_ref, z_ref, ya_ref, dtb_ref, alog_ref, dsk_ref, gs_ref, e_ref, cat_ref, y2_ref, hp_ref, h_ref):
        q = _ssd_common(c_ref, pdt_ref, dtb_ref, alog_ref, e_ref, h)
        act, cs, lo, ecs_l = q["act"], q["cs"], q["lo"], q["ecs_l"]
        xs = act[:, :d]
        xd = xs * q["dt_l"]
        ys = []
        for g in range(SSM_GROUPS):
            bg = act[:, bc0 + g * D_STATE: bc0 + (g + 1) * D_STATE]
            cgb = act[:, cc0 + g * D_STATE: cc0 + (g + 1) * D_STATE].astype(BF16)
            s = _dot_nt(cgb, bg.astype(BF16))
            bg_t = bg.T
            for jj in range(ppg):
                j = g * ppg + jj
                sl = slice(LANES * j, LANES * (j + 1))
                xdj = xd[:, sl]
                x2 = jnp.concatenate([jnp.where(lo, xdj, 0.0), jnp.where(lo, 0.0, xdj)], axis=0).astype(BF16)
                hprev = h_ref[j]
                hp_ref[j] = hprev.astype(BF16)
                ms, bws_t = [], []
                for hh in (2 * j, 2 * j + 1):
                    csc = cs[:, hh:hh + 1]
                    cs_row = q["cs_t"][hh:hh + 1, :]
                    seg = jnp.broadcast_to(csc, (CHUNK, CHUNK)) - jnp.broadcast_to(cs_row, (CHUNK, CHUNK))
                    ms.append(s * jnp.exp(jnp.where(q["tril"], seg, -jnp.inf)))
                    bws_t.append(bg_t * jnp.exp(cs_row[:, CHUNK - 1:CHUNK] - cs_row))
                ydiag = _dot(jnp.concatenate(ms, axis=1).astype(BF16), x2)
                st = _dot(jnp.concatenate(bws_t, axis=1).astype(BF16), x2)
                ecs = ecs_l[:, sl]
                yoff = _dot(cgb, hprev.astype(BF16)) * ecs
                h_ref[j] = hprev * ecs[CHUNK - 1:CHUNK] + st
                ys.append(ydiag + yoff)
        y = jnp.concatenate(ys, axis=1) + dsk_ref[...] * xs
        y2_ref[...] = y.astype(BF16)
        zv = z_ref[...].astype(F32)
        y3 = y * (zv * _sigmoid(zv))
        cat_ref[:, 0:d] = ya_ref[...]
        for gi in range(SSM_GROUPS):
            seg = y3[:, gi * gw:(gi + 1) * gw]
            cat_ref[:, d + gi * gw:d + (gi + 1) * gw] = (seg * _rms(seg) * gs_ref[:, gi * gw:(gi + 1) * gw]).astype(BF16)

    chunk = lambda w, cb=0: pl.BlockSpec((nseq, CHUNK, w), lambda c, cb=cb: (0, c, cb))
    vec = lambda w: pl.BlockSpec((1, w), lambda c: (0, 0))
    hp_spec = pl.BlockSpec((nseq, None, npair, D_STATE, LANES), lambda c: (0, c, 0, 0, 0))
    per_seq = lambda a: a.reshape(nseq, seq, a.shape[1])
    cat, y2, hp = _call(
        body, name=name, grid=(nc,),
        in_specs=[chunk(xbc), chunk(LANES), chunk(d, 3), chunk(d), vec(h), vec(h), vec(d), vec(d),
                  pl.BlockSpec((h, d), lambda c: (0, 0))],
        out_specs=[chunk(2 * d), chunk(d), hp_spec],
        out_shape=[jax.ShapeDtypeStruct((nseq, seq, 2 * d), BF16), jax.ShapeDtypeStruct((nseq, seq, d), BF16),
                   jax.ShapeDtypeStruct((nseq, nc, npair, D_STATE, LANES), BF16)],
        scratch_shapes=[pltpu.VMEM((nseq, npair, D_STATE, LANES), F32)],
        compiler_params=_params(("arbitrary",)))(
            per_seq(cpre), per_seq(pdt), per_seq(pz), per_seq(ya), dtb, alog, dsk_lane, gs, emat)
    return cat.reshape(t, 2 * d), y2.reshape(t, d), hp


def _ssd_bwd(cpre, pdt, pz, y2, hprev_all, dcat, dtb, alog, dsk_lane, gs, emat, dproj, *, nseq, seq, name):
    t, xbc = cpre.shape
    d = y2.shape[1]
    h = d // HEAD_DIM
    npair = h // 2
    ppg = npair // SSM_GROUPS
    nc = seq // CHUNK
    gw = d // SSM_GROUPS
    bc0 = d
    cc0 = d + SSM_GROUPS * D_STATE

    def body(c_ref, pdt_ref, z_ref, y2_ref, hp_ref, dys_ref, dtb_ref, alog_ref, dsk_ref, gs_ref, e_ref, dproj_in,
             dconv_ref, dz_ref, dpdt_ref, dgs_ref, ddsk_ref, ddtb_ref, dalog_ref, dh_ref):
        del dproj_in
        b = pl.program_id(0)
        c = pl.program_id(1)

        @pl.when(c == 0)
        def _():
            dh_ref[...] = jnp.zeros_like(dh_ref)

        @pl.when((b == 0) & (c == 0))
        def _():
            dgs_ref[...] = jnp.zeros_like(dgs_ref)
            ddsk_ref[...] = jnp.zeros_like(ddsk_ref)
            ddtb_ref[...] = jnp.zeros_like(ddtb_ref)
            dalog_ref[...] = jnp.zeros_like(dalog_ref)

        q = _ssd_common(c_ref, pdt_ref, dtb_ref, alog_ref, e_ref, h)
        cp, sg, act, cs, a, dt, lo = q["cp"], q["sg"], q["act"], q["cs"], q["a"], q["dt"], q["lo"]
        ecs_l, dt_l = q["ecs_l"], q["dt_l"]
        ev = e_ref[...]
        xs = act[:, :d]
        xd = xs * dt_l
        row16 = lax.broadcasted_iota(jnp.int32, (CHUNK, h), 0)
        hid = lax.broadcasted_iota(jnp.int32, (1, h), 1)
        hid_t = lax.broadcasted_iota(jnp.int32, (h, 1), 0)

        zv = z_ref[...].astype(F32)
        sz = _sigmoid(zv)
        siluz = zv * sz
        y2v = y2_ref[...].astype(F32)
        y3 = y2v * siluz
        dysv = dys_ref[...].astype(F32)
        dy3s = []
        for gi in range(SSM_GROUPS):
            gsl = slice(gi * gw, (gi + 1) * gw)
            seg = y3[:, gsl]
            dseg, dgt = _rms_bwd(seg, _rms(seg), gs_ref[:, gsl], dysv[:, gsl])
            dy3s.append(dseg)
            dgs_ref[:, gsl] += jnp.sum(dgt, axis=0, keepdims=True)
        dy3 = jnp.concatenate(dy3s, axis=1)
        dy = dy3 * siluz
        dz_ref[...] = (dy3 * y2v * (sz * (1.0 + zv * (1.0 - sz)))).astype(BF16)
        ddsk_ref[...] += jnp.sum(_head_sums(dy * xs, ev), axis=0, keepdims=True)

        dcs = jnp.zeros((CHUNK, h), F32)
        dcs_t = jnp.zeros((h, CHUNK), F32)
        dxd_parts, yoff_parts, db_parts, dc_parts = [], [], [], []
        for g in range(SSM_GROUPS):
            bg = act[:, bc0 + g * D_STATE: bc0 + (g + 1) * D_STATE]
            cg = act[:, cc0 + g * D_STATE: cc0 + (g + 1) * D_STATE]
            bgb, cgb = bg.astype(BF16), cg.astype(BF16)
            cgb_t = cg.T.astype(BF16)
            s = _dot_nt(cgb, bgb)
            ds = jnp.zeros((CHUNK, CHUNK), F32)
            dbg = jnp.zeros((CHUNK, D_STATE), F32)
            dcg = jnp.zeros((CHUNK, D_STATE), F32)
            for jj in range(ppg):
                j = g * ppg + jj
                sl = slice(LANES * j, LANES * (j + 1))
                xdj = xd[:, sl]
                xdb = xdj.astype(BF16)
                x2 = jnp.concatenate([jnp.where(lo, xdj, 0.0), jnp.where(lo, 0.0, xdj)], axis=0).astype(BF16)
                dyj = dy[:, sl]
                dy2 = jnp.concatenate([jnp.where(lo, dyj, 0.0), jnp.where(lo, 0.0, dyj)], axis=0).astype(BF16)
                hpb = hp_ref[j]
                hprev = hpb.astype(F32)
                dhn = dh_ref[j]
                dhb = dhn.astype(BF16)
                dh2 = jnp.concatenate([jnp.where(lo, dhn, 0.0), jnp.where(lo, 0.0, dhn)], axis=0).astype(BF16)
                ecs = ecs_l[:, sl]
                gmat = (dyj * ecs).astype(BF16)
                yoff_parts.append(_dot(cgb, hpb) * ecs)
                dcg = dcg + _dot_nt(gmat, hpb)
                dh_ref[j] = dhn * ecs[CHUNK - 1:CHUNK] + _dot(cgb_t, gmat)
                t2 = dhn * hprev
                dbw2 = _dot_nt(x2, dhb)
                dm2 = _dot_nt(dy2, xdb)
                ms, bws = [], []
                for idx, hh in enumerate((2 * j, 2 * j + 1)):
                    msk = lo if idx == 0 else jnp.logical_not(lo)
                    onehot = (hid == hh).astype(F32)
                    csc = cs[:, hh:hh + 1]
                    seg = jnp.broadcast_to(csc, (CHUNK, CHUNK)) - jnp.broadcast_to(q["cs_t"][hh:hh + 1, :], (CHUNK, CHUNK))
                    lm = jnp.exp(jnp.where(q["tril"], seg, -jnp.inf))
                    m = s * lm
                    cs_last = cs[CHUNK - 1:CHUNK, hh:hh + 1]
                    dte = jnp.exp(cs_last - csc)
                    ms.append(m)
                    bws.append(bg * dte)
                    dbw = dbw2[idx * CHUNK:(idx + 1) * CHUNK]
                    dbg = dbg + dbw * dte
                    qv = jnp.sum(dbw * bg, axis=-1, keepdims=True) * dte
                    dm = dm2[idx * CHUNK:(idx + 1) * CHUNK]
                    wm = dm * m
                    rc = jnp.sum(wm, axis=-1, keepdims=True)
                    dcs_t = dcs_t - (hid_t == hh).astype(F32) * jnp.sum(wm, axis=0, keepdims=True)
                    ds = ds + dm * lm
                    ddec = jnp.sum(jnp.where(msk, t2, 0.0)) * jnp.exp(cs_last)
                    last = jnp.sum(qv) + ddec
                    dcs = dcs + (rc - qv) * onehot + jnp.where(row16 == CHUNK - 1, last * onehot, 0.0)
                dxd_s = _dot(jnp.concatenate(bws, axis=1).astype(BF16), dh2)
                dxd_d = _dot_tn(jnp.concatenate(ms, axis=0).astype(BF16), dy2)
                dxd_parts.append(dxd_s + dxd_d)
            dsb = ds.astype(BF16)
            dc_parts.append(dcg + _dot(dsb, bgb))
            db_parts.append(dbg + _dot_tn(dsb, cgb))
        yoff_all = jnp.concatenate(yoff_parts, axis=1)
        dxd_all = jnp.concatenate(dxd_parts, axis=1)
        dcs = dcs + _head_sums(dy * yoff_all, ev)
        triu = (q["col"] >= q["row"]).astype(F32)
        dadt = (jnp.dot(triu, dcs, precision=HIGHEST, preferred_element_type=F32)
                + lax.dot_general(triu, dcs_t, (((1,), (1,)), ((), ())), precision=HIGHEST, preferred_element_type=F32))
        ddt = dadt * a + _head_sums(dxd_all * xs, ev)
        dalog_ref[...] += jnp.sum(dadt * dt, axis=0, keepdims=True) * a
        dpre = ddt * _sigmoid(q["pre"])
        ddtb_ref[...] += jnp.sum(dpre, axis=0, keepdims=True)
        dpdt_ref[...] = jnp.zeros_like(dpdt_ref)
        dpdt_ref[:, 0:h] = dpre.astype(BF16)
        dxs = dxd_all * dt_l + dy * dsk_ref[...]
        dact = jnp.concatenate([dxs] + db_parts + dc_parts, axis=1)
        dconv_ref[...] = (dact * (sg * (1.0 + cp * (1.0 - sg)))).astype(BF16)

    rchunk = lambda w, cb=0: pl.BlockSpec((CHUNK, w), lambda b, c, cb=cb: (b * nc + nc - 1 - c, cb))
    vec = lambda w: pl.BlockSpec((1, w), lambda b, c: (0, 0))
    hp_spec = pl.BlockSpec((None, None, npair, D_STATE, LANES), lambda b, c: (b, nc - 1 - c, 0, 0, 0))
    return _call(body, name=name, grid=(nseq, nc),
                 in_specs=[rchunk(xbc), rchunk(LANES), rchunk(d, 3), rchunk(d), hp_spec, rchunk(d, 1),
                           vec(h), vec(h), vec(d), vec(d), pl.BlockSpec((h, d), lambda b, c: (0, 0)),
                           pl.BlockSpec(memory_space=pl.ANY)],
                 out_specs=[rchunk(xbc), rchunk(d, 3), rchunk(LANES), vec(d), vec(h), vec(h), vec(h)],
                 out_shape=[jax.ShapeDtypeStruct((t, xbc), BF16), jax.ShapeDtypeStruct(dproj.shape, BF16),
                            jax.ShapeDtypeStruct((t, LANES), BF16), jax.ShapeDtypeStruct((1, d), F32),
                            jax.ShapeDtypeStruct((1, h), F32), jax.ShapeDtypeStruct((1, h), F32),
                            jax.ShapeDtypeStruct((1, h), F32)],
                 input_output_aliases={11: 1},
                 scratch_shapes=[pltpu.VMEM((npair, D_STATE, LANES), F32)],
                 compiler_params=_params(("arbitrary", "arbitrary")))(
                     cpre, pdt, pz, y2, hprev_all, dcat, dtb, alog, dsk_lane, gs, emat, dproj)


def _sum_adamw(parts, w, m, v, *, name, layer=None, outs=None):
    n, r, c = parts.shape
    tr = _pick_rows(r, 256)
    bc1 = 1.0 - ADAM_B1 ** ADAM_STEP
    bc2 = 1.0 - ADAM_B2 ** ADAM_STEP

    def body(p_ref, w_ref, m_ref, v_ref, *rest):
        g_ref, d_ref, mo_ref, vo_ref = rest[-4:]
        g = p_ref[0].astype(F32)
        for k in range(1, n):
            g = g + p_ref[k].astype(F32)
        mn = ADAM_B1 * m_ref[...] + (1.0 - ADAM_B1) * g
        vn = ADAM_B2 * v_ref[...] + (1.0 - ADAM_B2) * (g * g)
        g_ref[...] = g
        mo_ref[...] = mn
        vo_ref[...] = vn
        d_ref[...] = -ADAM_LR * ((mn / bc1) / (jnp.sqrt(vn / bc2) + ADAM_EPS) + ADAM_WD * w_ref[...])

    p_spec = pl.BlockSpec((n, tr, c), lambda i: (0, i, 0))
    if layer is None:
        blk = pl.BlockSpec((tr, c), lambda i: (i, 0))
        return _call(body, name=name, grid=(r // tr,), in_specs=[p_spec, blk, blk, blk], out_specs=[blk] * 4,
                     out_shape=[jax.ShapeDtypeStruct((r, c), F32)] * 4,
                     compiler_params=_params(("parallel",)))(parts, w, m, v)
    blk = pl.BlockSpec((None, tr, c), lambda i: (layer, i, 0))
    if outs is None:
        outs = [lax.empty(w.shape, F32) for _ in range(4)]
    return _call(body, name=name, grid=(r // tr,),
                 in_specs=[p_spec, blk, blk, blk] + [pl.BlockSpec(memory_space=pl.ANY)] * 4, out_specs=[blk] * 4,
                 out_shape=[jax.ShapeDtypeStruct(w.shape, F32)] * 4, input_output_aliases={4 + k: k for k in range(4)},
                 compiler_params=_params(("parallel",)))(parts, w, m, v, *outs)


def _assemble_cols(blocks, *, name):
    nb, r, c = blocks.shape
    width = -(-nb * c // LANES) * LANES
    tr = _pick_rows(r, 256)

    def body(b_ref, o_ref):
        pieces = [b_ref[j] for j in range(nb)]
        if width > nb * c:
            pieces.append(jnp.zeros((tr, width - nb * c), blocks.dtype))
        o_ref[...] = jnp.concatenate(pieces, axis=1)

    return _call(body, name=name, grid=(r // tr,), in_specs=[pl.BlockSpec((nb, tr, c), lambda i: (0, i, 0))],
                 out_specs=pl.BlockSpec((tr, width), lambda i: (i, 0)), out_shape=jax.ShapeDtypeStruct((r, width), blocks.dtype),
                 compiler_params=_params(("parallel",)))(blocks)


def _split_cols(pieces, c, *, name):
    r = pieces[0].shape[0]
    tr = _pick_rows(r, 256)
    n_in = len(pieces)

    def body(*refs):
        o_ref = refs[n_in]
        x = jnp.concatenate([p[...] for p in refs[:n_in]], axis=1) if n_in > 1 else refs[0][...]
        for j in range(N_DEV):
            o_ref[j] = x[:, c * j:c * (j + 1)]

    return _call(body, name=name, grid=(r // tr,),
                 in_specs=[pl.BlockSpec((tr, p.shape[1]), lambda i: (i, 0)) for p in pieces],
                 out_specs=pl.BlockSpec((N_DEV, tr, c), lambda i: (0, i, 0)),
                 out_shape=jax.ShapeDtypeStruct((N_DEV, r, c), pieces[0].dtype),
                 compiler_params=_params(("parallel",)))(*pieces)


def _sum_parts(parts, *, name):
    n, r, c = parts.shape
    tr = _pick_rows(r, 256)

    def body(p_ref, g_ref):
        g = p_ref[0].astype(F32)
        for k in range(1, n):
            g = g + p_ref[k].astype(F32)
        g_ref[...] = g

    return _call(body, name=name, grid=(r // tr,), in_specs=[pl.BlockSpec((n, tr, c), lambda i: (0, i, 0))],
                 out_specs=pl.BlockSpec((tr, c), lambda i: (i, 0)), out_shape=jax.ShapeDtypeStruct((r, c), F32),
                 compiler_params=_params(("parallel",)))(parts)


def _peers():
    x, y, c = lax.axis_index("x"), lax.axis_index("y"), lax.axis_index("c")
    me = 4 * x + 2 * y + c
    out = []
    for k in range(1, N_DEV):
        px = (1 - x) if (k >> 2) & 1 else x
        py = (1 - y) if (k >> 1) & 1 else y
        pc = (1 - c) if k & 1 else c
        out.append(((px, py, pc), 4 * px + 2 * py + pc))
    return me, out


_HBM = pl.BlockSpec(memory_space=pltpu.HBM)
_SEM = pl.BlockSpec(memory_space=pltpu.SEMAPHORE)
_EFFECT = pltpu.SideEffectType.DATAFLOW_SIDE_EFFECTING


ALL_PEERS = tuple(range(1, N_DEV))
SAME_CORE_PEERS = (2, 4, 6)


def _slot(ref, j, c):
    if c is None:
        return ref.at[j]
    start = j * c
    return ref.at[:, pl.ds(start if isinstance(start, int) else pl.multiple_of(start, c), c)]


def _split_copies(s_refs, l_refs, send_sems, recv_sems, gather, incoming, ks, src_cols, land_cols):
    me, peers = _peers()
    local, remote = [], []
    for ti, (s_ref, l_ref) in enumerate(zip(s_refs, l_refs)):
        base = ti * N_DEV
        sc, lc = src_cols[ti], land_cols[ti]
        local.append(pltpu.make_async_copy(s_ref if gather else _slot(s_ref, me, sc), _slot(l_ref, me, lc),
                                           recv_sems.at[base + N_DEV - 1]))
        for k, (dev, pid) in enumerate(peers):
            if k + 1 not in ks:
                continue
            sems = dict(send_sem=send_sems.at[base + k], recv_sem=recv_sems.at[base + k], device_id=dev, device_id_type=MESH)
            src = s_ref if gather else _slot(s_ref, pid, sc)
            remote.append((
                pltpu.make_async_remote_copy(src_ref=src, dst_ref=_slot(l_ref, me, lc), **sems),
                pltpu.make_async_remote_copy(src_ref=src, dst_ref=_slot(l_ref, pid, lc), **sems) if incoming else None))
    return local, remote


def _exchange_start(srcs, *, gather, name, after=(), ks=ALL_PEERS, src_cols=None, land_cols=None):
    n = len(srcs)
    after = list(after)
    src_cols = list(src_cols or [None] * n)
    land_cols = list(land_cols or [None] * n)
    srcs = [pltpu.with_memory_space_constraint(s, pltpu.HBM) for s in srcs]

    def land_shape(s, sc, lc):
        block = tuple(s.shape) if gather else ((s.shape[0], sc) if sc else tuple(s.shape[1:]))
        return (block[0], N_DEV * lc) if lc else (N_DEV,) + block

    lands = [pltpu.with_memory_space_constraint(lax.empty(land_shape(s, sc, lc), s.dtype), pltpu.HBM)
             for s, sc, lc in zip(srcs, src_cols, land_cols)]

    def body(*refs):
        s_refs, l_refs = refs[:n], refs[n:2 * n]
        outs = refs[2 * n + len(after):]
        send_sems, recv_sems, token = outs[0], outs[1], outs[-1]
        local, remote = _split_copies(s_refs, l_refs, send_sems, recv_sems, gather, False, ks, src_cols, land_cols)
        for cp in local:
            cp.start()
        for out_cp, _ in remote:
            out_cp.start()
        token[...] = jnp.zeros_like(token)

    outs = _call(
        body, name=name,
        out_shape=(pltpu.SemaphoreType.DMA((n * N_DEV,)), pltpu.SemaphoreType.DMA((n * N_DEV,)),
                   *[pltpu.HBM(s.shape, s.dtype) for s in srcs], *[pltpu.HBM(l.shape, l.dtype) for l in lands],
                   jax.ShapeDtypeStruct((SUBLANES, LANES), F32)),
        in_specs=[_HBM] * (2 * n) + [pl.BlockSpec(memory_space=pl.ANY)] * len(after),
        out_specs=(_SEM, _SEM, *[_HBM] * (2 * n), pl.BlockSpec(memory_space=pltpu.VMEM)),
        input_output_aliases={k: k + 2 for k in range(2 * n)},
        compiler_params=pltpu.CompilerParams(has_side_effects=_EFFECT),
    )(*srcs, *lands, *after)
    return dict(n=n, gather=gather, ks=ks, src_cols=src_cols, land_cols=land_cols, sems=outs[:2], srcs=outs[2:2 + n],
                lands=outs[2 + n:2 + 2 * n]), outs[-1]


def _exchange_wait(state, after, *, name):
    n, gather, ks = state["n"], state["gather"], state["ks"]
    after = list(after)

    def body(*refs):
        s_refs, l_refs = refs[:n], refs[n:2 * n]
        send_sems, recv_sems = refs[2 * n], refs[2 * n + 1]
        local, remote = _split_copies(s_refs, l_refs, send_sems, recv_sems, gather, True, ks, state["src_cols"],
                                      state["land_cols"])
        for out_cp, in_cp in remote:
            out_cp.wait_send()
            in_cp.wait_recv()
        for cp in local:
            cp.wait()

    outs = _call(
        body, name=name,
        out_shape=tuple(pltpu.HBM(a.shape, a.dtype) for a in (*state["srcs"], *state["lands"])),
        in_specs=[_HBM] * (2 * n) + [_SEM, _SEM] + [pl.BlockSpec(memory_space=pl.ANY)] * len(after),
        out_specs=tuple([_HBM] * (2 * n)),
        input_output_aliases={k: k for k in range(2 * n)},
        compiler_params=pltpu.CompilerParams(has_side_effects=_EFFECT),
    )(*state["srcs"], *state["lands"], *state["sems"], *after)
    return outs[n:]


def _sibling_copies(l_refs, send_sems, recv_sems, incoming, land_cols):
    x, y, c = lax.axis_index("x"), lax.axis_index("y"), lax.axis_index("c")
    out = []
    for ti, l_ref in enumerate(l_refs):
        for q in range(4):
            px = (1 - x) if q & 2 else x
            py = (1 - y) if q & 1 else y
            mine = _slot(l_ref, 4 * px + 2 * py + c, land_cols[ti])
            theirs = _slot(l_ref, 4 * px + 2 * py + (1 - c), land_cols[ti])
            sems = dict(send_sem=send_sems.at[4 * ti + q], recv_sem=recv_sems.at[4 * ti + q],
                        device_id=(x, y, 1 - c), device_id_type=MESH)
            out.append((
                pltpu.make_async_remote_copy(src_ref=mine, dst_ref=mine, **sems),
                pltpu.make_async_remote_copy(src_ref=mine, dst_ref=theirs, **sems) if incoming else None))
    return out


def _sibling_start(lands, *, name, after=(), land_cols=None):
    n = len(lands)
    after = list(after)
    land_cols = list(land_cols or [None] * n)
    lands = [pltpu.with_memory_space_constraint(l, pltpu.HBM) for l in lands]

    def body(*refs):
        l_refs = refs[:n]
        outs = refs[n + len(after):]
        for out_cp, _ in _sibling_copies(l_refs, outs[0], outs[1], False, land_cols):
            out_cp.start()
        outs[-1][...] = jnp.zeros_like(outs[-1])

    outs = _call(
        body, name=name,
        out_shape=(pltpu.SemaphoreType.DMA((4 * n,)), pltpu.SemaphoreType.DMA((4 * n,)),
                   *[pltpu.HBM(l.shape, l.dtype) for l in lands], jax.ShapeDtypeStruct((SUBLANES, LANES), F32)),
        in_specs=[_HBM] * n + [pl.BlockSpec(memory_space=pl.ANY)] * len(after),
        out_specs=(_SEM, _SEM, *[_HBM] * n, pl.BlockSpec(memory_space=pltpu.VMEM)),
        input_output_aliases={k: k + 2 for k in range(n)},
        compiler_params=pltpu.CompilerParams(has_side_effects=_EFFECT),
    )(*lands, *after)
    return dict(n=n, land_cols=land_cols, sems=outs[:2], lands=outs[2:2 + n]), outs[-1]


def _sibling_wait(state, after, *, name):
    n = state["n"]
    after = list(after)

    def body(*refs):
        l_refs = refs[:n]
        for out_cp, in_cp in _sibling_copies(l_refs, refs[n], refs[n + 1], True, state["land_cols"]):
            out_cp.wait_send()
            in_cp.wait_recv()

    return _call(
        body, name=name,
        out_shape=tuple(pltpu.HBM(a.shape, a.dtype) for a in state["lands"]),
        in_specs=[_HBM] * n + [_SEM, _SEM] + [pl.BlockSpec(memory_space=pl.ANY)] * len(after),
        out_specs=tuple([_HBM] * n), input_output_aliases={k: k for k in range(n)},
        compiler_params=pltpu.CompilerParams(has_side_effects=_EFFECT),
    )(*state["lands"], *state["sems"], *after)


def _pack(arrs):
    flat = jnp.concatenate([a.reshape(-1).astype(F32) for a in arrs])
    pad = (-flat.shape[0]) % (SUBLANES * LANES)
    return jnp.pad(flat, (0, pad)).reshape(-1, LANES)


def _unpack(packed, shapes):
    flat = packed.reshape(-1)
    out, off = [], 0
    for s in shapes:
        n = 1
        for v in s:
            n *= v
        out.append(flat[off:off + n].reshape(s))
        off += n
    return out


SMALL = ("norm_mix_pre", "ssm_conv_b", "dt_bias", "a_log", "d_skip", "conv_out_norm", "ssm_out_norm",
         "norm_mix_post", "norm_mlp_pre", "norm_mlp_post", "conv_a_w", "ssm_conv_w")
BIG = ("w_in", "w_out", "w_up", "w_down")
ORDER = ("norm_mix_pre", "w_in", "conv_a_w", "ssm_conv_w", "ssm_conv_b", "dt_bias", "a_log", "d_skip",
         "conv_out_norm", "ssm_out_norm", "w_out", "norm_mix_post", "norm_mlp_pre", "w_up", "w_down", "norm_mlp_post")


def kernel(x, norm_mix_pre, w_in, conv_a_w, ssm_conv_w, ssm_conv_b, dt_bias, a_log, d_skip, conv_out_norm, ssm_out_norm, w_out, norm_mix_post, norm_mlp_pre, w_up, w_down, norm_mlp_post, loss_target, m_norm_mix_pre, m_w_in, m_conv_a_w, m_ssm_conv_w, m_ssm_conv_b, m_dt_bias, m_a_log, m_d_skip, m_conv_out_norm, m_ssm_out_norm, m_w_out, m_norm_mix_post, m_norm_mlp_pre, m_w_up, m_w_down, m_norm_mlp_post, v_norm_mix_pre, v_w_in, v_conv_a_w, v_ssm_conv_w, v_ssm_conv_b, v_dt_bias, v_a_log, v_d_skip, v_conv_out_norm, v_ssm_out_norm, v_w_out, v_norm_mix_post, v_norm_mlp_pre, v_w_up, v_w_down, v_norm_mlp_post):
    W = dict(norm_mix_pre=norm_mix_pre, w_in=w_in, conv_a_w=conv_a_w, ssm_conv_w=ssm_conv_w, ssm_conv_b=ssm_conv_b,
             dt_bias=dt_bias, a_log=a_log, d_skip=d_skip, conv_out_norm=conv_out_norm, ssm_out_norm=ssm_out_norm,
             w_out=w_out, norm_mix_post=norm_mix_post, norm_mlp_pre=norm_mlp_pre, w_up=w_up, w_down=w_down,
             norm_mlp_post=norm_mlp_post)
    M = dict(norm_mix_pre=m_norm_mix_pre, w_in=m_w_in, conv_a_w=m_conv_a_w, ssm_conv_w=m_ssm_conv_w,
             ssm_conv_b=m_ssm_conv_b, dt_bias=m_dt_bias, a_log=m_a_log, d_skip=m_d_skip,
             conv_out_norm=m_conv_out_norm, ssm_out_norm=m_ssm_out_norm, w_out=m_w_out,
             norm_mix_post=m_norm_mix_post, norm_mlp_pre=m_norm_mlp_pre, w_up=m_w_up, w_down=m_w_down,
             norm_mlp_post=m_norm_mlp_post)
    V = dict(norm_mix_pre=v_norm_mix_pre, w_in=v_w_in, conv_a_w=v_conv_a_w, ssm_conv_w=v_ssm_conv_w,
             ssm_conv_b=v_ssm_conv_b, dt_bias=v_dt_bias, a_log=v_a_log, d_skip=v_d_skip,
             conv_out_norm=v_conv_out_norm, ssm_out_norm=v_ssm_out_norm, w_out=v_w_out,
             norm_mix_post=v_norm_mix_post, norm_mlp_pre=v_norm_mlp_pre, w_up=v_w_up, w_down=v_w_down,
             norm_mlp_post=v_norm_mlp_post)

    nseq, seq, d = x.shape
    t = nseq * seq
    depth = w_in.shape[0]
    h = d // HEAD_DIM
    xbc = d + 2 * SSM_GROUPS * D_STATE
    in_cols = w_in.shape[2] * N_DEV
    d_mix = w_out.shape[1] * N_DEV
    d_ff = w_up.shape[2] * N_DEV
    me = 4 * lax.axis_index("x") + 2 * lax.axis_index("y") + lax.axis_index("c")
    ca_shard = conv_a_w.shape[2]
    sc_shard = ssm_conv_w.shape[2]

    tap_shapes = [conv_a_w.shape[1:], ssm_conv_w.shape[1:]]

    def gather_start(i, after=()):
        ks = SAME_CORE_PEERS
        st_in, tok_in = _exchange_start([w_in[i].astype(BF16), _pack([conv_a_w[i], ssm_conv_w[i]])], gather=True,
                                        name=f"gather_start_in_{i}", after=after, ks=ks)
        st_rest, tok_rest = _exchange_start([W[n][i].astype(BF16) for n in ("w_out", "w_up", "w_down")], gather=True,
                                            name=f"gather_start_rest_{i}", after=[tok_in], ks=ks, land_cols=rest_cols)
        return st_in, st_rest, tok_rest

    rest_cols = [None, d_ff // N_DEV, None]

    vec = lambda name, i: W[name][i].reshape(1, -1)
    emat = (lax.broadcasted_iota(jnp.int32, (h, d), 1) // HEAD_DIM == lax.broadcasted_iota(jnp.int32, (h, d), 0)).astype(F32)

    xcur = x.reshape(t, d)
    hcur = _norm_fwd(xcur, vec("norm_mix_pre", 0), name="norm_first")
    saved = []
    nxt = gather_start(0)
    adam_in = [a + nxt[2][0, 0] for a in (W["w_in"], M["w_in"], V["w_in"])]
    sib_in = None
    for i in range(depth):
        st_in, st_rest, tok = nxt
        if sib_in is None:
            sib_in, _ = _sibling_start(_exchange_wait(st_in, [hcur, tok] + adam_in, name=f"gather_wait_in_{i}"),
                                       name=f"gather_sibling_start_in_{i}")
        win_g, taps_g = _sibling_wait(sib_in, [hcur], name=f"gather_sibling_wait_in_{i}")
        win = _assemble_cols(win_g, name=f"assemble_w_in_{i}")
        taps_j = [_unpack(taps_g[j], tap_shapes) for j in range(N_DEV)]
        conv_a_i = jnp.concatenate([tj[0] for tj in taps_j], axis=1)
        ssm_conv_i = jnp.concatenate([tj[1] for tj in taps_j], axis=1)
        proj = _mm(hcur, win, n=4 * d + xbc, name=f"fwd_proj_{i}", out_dtypes=(BF16,))
        pdt = _mm(hcur, win, n=LANES, b_off=4 * d + xbc, name=f"fwd_proj_dt_{i}")
        ya, va = _conva_fwd(proj, conv_a_i, vec("conv_out_norm", i), d=d, seq=seq, name=f"fwd_conv_a_{i}")
        cpre = _convb_fwd(proj, ssm_conv_i, vec("ssm_conv_b", i), col0=4 * d, seq=seq, name=f"fwd_conv_b_{i}")
        dsk_lane = jnp.repeat(W["d_skip"][i], HEAD_DIM).reshape(1, d)
        st_sib, tok_sib = _sibling_start(_exchange_wait(st_rest, [cpre], name=f"gather_wait_rest_{i}"),
                                         name=f"gather_sibling_start_rest_{i}", land_cols=rest_cols)
        cat, y2, hprev = _ssd_fwd(cpre, pdt, proj, ya, vec("dt_bias", i) + tok_sib[0:1, 0:1], vec("a_log", i), dsk_lane,
                                  vec("ssm_out_norm", i), emat, nseq=nseq, seq=seq, name=f"fwd_ssd_{i}")
        wout_g, wup_g, wdown_g = _sibling_wait(st_sib, [cat], name=f"gather_sibling_wait_rest_{i}")
        lw = dict(win=win, wout=wout_g.reshape(d_mix, d),
                  wup=wup_g, wdown=wdown_g.reshape(d_ff, d),
                  conv_a=conv_a_i, ssm_conv=ssm_conv_i)
        after = []
        if i + 1 < depth:
            nxt = gather_start(i + 1, after=[wout_g])
            after = [nxt[2]]
        x1, h2, mix = _mm(cat, lw["wout"], name=f"fwd_out_{i}", after=after, out_dtypes=(F32, BF16, BF16),
                          epi=_epi_resid_norm, extras=(xcur,), vecs=(vec("norm_mix_post", i), vec("norm_mlp_pre", i)),
                          tm_cap=FUSED_ROWS)
        f = _mm(h2, lw["wup"], name=f"fwd_up_{i}", out_dtypes=(BF16,), epi=_epi_relu2)
        g_next = vec("norm_mix_pre", i + 1) if i + 1 < depth else None
        after = []
        if i + 1 < depth:
            sib_in, tok_in = _sibling_start(_exchange_wait(nxt[0], [f], name=f"gather_wait_in_{i + 1}"),
                                            name=f"gather_sibling_start_in_{i + 1}")
            after = [tok_in]
        dn = _mm(f, lw["wdown"], name=f"fwd_down_{i}", out_dtypes=(BF16,), after=after)
        if i + 1 < depth:
            x2, hnext = _resid_norm(x1, dn, vec("norm_mlp_post", i), g_next, name=f"fwd_post_mlp_{i}")
        else:
            x2 = hnext = None
        saved.append(dict(lw=lw, x0=xcur, h=hcur, proj=proj, pdt=pdt, va=va, cpre=cpre, y2=y2,
                          hprev=hprev, cat=cat, mix=mix, x1=x1, h2=h2, f=f, dn=dn, dsk_lane=dsk_lane))
        xcur, hcur = x2, hnext

    last = saved[depth - 1]
    dx, ddn, dg_last, loss_part = _tail_loss(last["x1"], last["dn"], vec("norm_mlp_post", depth - 1),
                                             loss_target.reshape(t, d), name="loss")

    small_grads = {n: [None] * depth for n in SMALL}
    big_out = {n: None for n in BIG}

    def finish(pending, after):
        li, st_a, st_b = pending

        def update(n, parts):
            wmv = adam_in if n == "w_in" else (W[n], M[n], V[n])
            big_out[n] = _sum_adamw(parts, *wmv, layer=li, outs=big_out[n], name=f"adamw_{n}_{li}")

        p_down, p_up, p_out = _exchange_wait(st_a, after, name=f"scatter_wait_a_{li}")
        update("w_down", p_down)
        update("w_up", p_up)
        update("w_out", p_out)
        p_in, = _exchange_wait(st_b, after + [big_out["w_out"][0]], name=f"scatter_wait_b_{li}")
        update("w_in", p_in)

    pending = None
    for i in reversed(range(depth)):
        s = saved[i]
        lw = s["lw"]
        if i == depth - 1:
            small_grads["norm_mlp_post"][i] = dg_last
        dup = _mm(ddn, lw["wdown"], tb=True, name=f"bwd_down_dx_{i}", out_dtypes=(BF16,), epi=_epi_drelu2,
                  extras=(s["f"],))
        g_wdown = _mm(s["f"], ddn, ta=True, name=f"bwd_down_dw_{i}", out_dtypes=(BF16,))
        dh2 = _mm(dup, lw["wup"], tb=True, name=f"bwd_up_dx_{i}", out_dtypes=(BF16,))
        g_wup = _mm(s["h2"], dup, ta=True, name=f"bwd_up_dw_{i}", out_dtypes=(BF16,))
        dx1, dmix, dg_pre, dg_post = _bwd_norm_pair(s["x1"], [dh2], dx, s["mix"], vec("norm_mlp_pre", i),
                                                    vec("norm_mix_post", i), name=f"bwd_norm_mix_post_{i}")
        small_grads["norm_mlp_pre"][i] = dg_pre
        small_grads["norm_mix_post"][i] = dg_post
        dcat = _mm(dmix, lw["wout"], tb=True, name=f"bwd_out_dx_{i}", out_dtypes=(BF16,))
        g_wout = _mm(s["cat"], dmix, ta=True, name=f"bwd_out_dw_{i}", out_dtypes=(BF16,))
        st_a, tok_a = _exchange_start(
            [g_wdown.reshape(N_DEV, d_ff // N_DEV, d), g_wup, g_wout.reshape(N_DEV, d_mix // N_DEV, d)],
            gather=False, name=f"scatter_start_a_{i}", src_cols=[None, d_ff // N_DEV, None])
        dproj, dcaw, dgca = _conva_bwd(dcat, s["proj"], s["va"], lw["conv_a"],
                                       vec("conv_out_norm", i) + tok_a[0:1, 0:1], d=d, seq=seq, name=f"bwd_conv_a_{i}")
        small_grads["conv_a_w"][i] = dcaw
        small_grads["conv_out_norm"][i] = dgca
        dconv, dproj, dpdt, dgs, ddsk, ddtb, dalog = _ssd_bwd(
            s["cpre"], s["pdt"], s["proj"], s["y2"], s["hprev"], dcat, vec("dt_bias", i), vec("a_log", i),
            s["dsk_lane"], vec("ssm_out_norm", i), emat, dproj, nseq=nseq, seq=seq, name=f"bwd_ssd_{i}")
        small_grads["ssm_out_norm"][i] = dgs
        small_grads["d_skip"][i] = ddsk
        small_grads["dt_bias"][i] = ddtb
        small_grads["a_log"][i] = dalog
        dproj, dscw, dscb = _convb_bwd(dconv, s["proj"], lw["ssm_conv"], dproj, col0=4 * d, seq=seq,
                                       name=f"bwd_conv_b_{i}")
        small_grads["ssm_conv_w"][i] = dscw
        small_grads["ssm_conv_b"][i] = dscb
        g_win = _split_cols([
            _mm(s["h"], dproj, ta=True, name=f"bwd_proj_dw_{i}", out_dtypes=(BF16,)),
            _mm(s["h"], dpdt, ta=True, name=f"bwd_proj_dt_dw_{i}", out_dtypes=(BF16,))],
            in_cols // N_DEV, name=f"split_g_w_in_{i}")
        st_b, tok_b = _exchange_start([g_win], gather=False, name=f"scatter_start_b_{i}")
        dh_parts = [_mm(dp, lw["win"], tb=True, b_koff=off, name=f"bwd_proj_{nm}dx_{i}", after=[tok_b], out_dtypes=(BF16,))
                    for nm, dp, off in (("", dproj, 0), ("dt_", dpdt, 4 * d + xbc))]
        if i > 0:
            dx, ddn, dg_in, dg_below = _bwd_norm_pair(s["x0"], dh_parts, dx1, saved[i - 1]["dn"], vec("norm_mix_pre", i),
                                                      vec("norm_mlp_post", i - 1), name=f"bwd_norm_mix_pre_{i}")
            small_grads["norm_mlp_post"][i - 1] = dg_below
        else:
            dx, dg_in = _bwd_norm_in(s["x0"], dh_parts, dx1, vec("norm_mix_pre", i), name=f"bwd_norm_mix_pre_{i}")
        small_grads["norm_mix_pre"][i] = dg_in
        if pending is not None:
            finish(pending, [dx])
        pending = (i, st_a, st_b)

    grad_x = dx.reshape(nseq, seq, d)

    small_shapes_full = {n: (depth,) + tuple(small_grads[n][0].shape) for n in SMALL}
    gpack = _pack([jnp.stack(small_grads[n]) for n in SMALL] + [loss_part])
    st_small, tok_small = _exchange_start([gpack], gather=True, name="allreduce_small_start")
    finish(pending, [dx, tok_small])
    gparts, = _exchange_wait(st_small, [big_out["w_in"][0]], name="allreduce_small_wait")

    def shard_of(n, full):
        if n == "conv_a_w":
            return lax.dynamic_slice_in_dim(full, me * ca_shard, ca_shard, axis=2)
        if n == "ssm_conv_w":
            return lax.dynamic_slice_in_dim(full, me * sc_shard, sc_shard, axis=2)
        return full.reshape(W[n].shape)

    gsum = _sum_parts(gparts, name="sum_small")
    gfull = _unpack(gsum, [small_shapes_full[n] for n in SMALL] + [(1, 1)])
    loss = gfull[-1][0, 0]
    gsmall = {n: shard_of(n, gf) for n, gf in zip(SMALL, gfull)}
    res = _sum_adamw(_pack([gsmall[n] for n in SMALL])[None], _pack([W[n] for n in SMALL]),
                     _pack([M[n] for n in SMALL]), _pack([V[n] for n in SMALL]), name="adamw_small")
    small_out = [dict(zip(SMALL, _unpack(r, [W[n].shape for n in SMALL]))) for r in res]

    def out_of(kind, n):
        return big_out[n][kind] if n in BIG else small_out[kind][n]

    return (loss, grad_x, *[out_of(k, n) for k in range(4) for n in ORDER])
```
